```python
import math
import jax, jax.numpy as jnp
from jax import lax
import numpy as np

D_MODEL = 1024
BATCH = 8
SEQ = 8192
DEPTH = 1

D_MIX = D_MODEL
RET_WIDTH = D_MIX // 2
RET_HEADS = 4
RET_HEAD_DIM = RET_WIDTH // RET_HEADS
RET_CHUNK = 128
ROPE_BASE = 10000.0
SSM_WIDTH = D_MIX - RET_WIDTH
SSM_GROUP = 16
SSM_GROUPS = SSM_WIDTH // SSM_GROUP
SSM_STATE = 64
DT_MIN = 1e-3
DT_MAX = 1e-1
D_FF = 4 * D_MODEL
NORM_EPS = 1e-6
IN_COLS = 4 * RET_WIDTH + SSM_WIDTH

kernel_name = "hymba_retnet_s5_sandwich_block"


def rmsnorm(x, g):
    xf = x.astype(jnp.float32)
    y = xf * lax.rsqrt(jnp.mean(xf * xf, axis=-1, keepdims=True) + NORM_EPS) * g.astype(jnp.float32)
    return y.astype(x.dtype)


def rope(x):
    L, d = x.shape[1], x.shape[-1]
    half = d // 2
    inv_freq = ROPE_BASE ** (-jnp.arange(half, dtype=jnp.float32) / half)
    ang = jnp.arange(L, dtype=jnp.float32)[:, None] * inv_freq[None, :]
    cos = jnp.cos(ang)[None, :, None, :]
    sin = jnp.sin(ang)[None, :, None, :]
    x1, x2 = x[..., :half], x[..., half:]
    return jnp.concatenate([x1 * cos - x2 * sin, x1 * sin + x2 * cos], axis=-1)


def retention_chunkwise(q, k, v):
    B, L, H, d = q.shape
    C = RET_CHUNK
    nc = L // C
    log_gamma = jnp.log(1.0 - jnp.exp(jnp.linspace(math.log(1.0 / 32), math.log(1.0 / 512), H))).astype(jnp.float32)
    q = q.reshape(B, nc, C, H, d)
    k = k.reshape(B, nc, C, H, d)
    v = v.reshape(B, nc, C, H, d)
    idx = jnp.arange(C, dtype=jnp.float32)
    diff = idx[:, None] - idx[None, :]
    decay = jnp.where(diff[None] >= 0, jnp.exp(jnp.maximum(diff, 0.0)[None] * log_gamma[:, None, None]), 0.0)
    s = jnp.einsum('bnihk,bnjhk->bnhij', q, k) * decay[None, None]
    inner = jnp.einsum('bnhij,bnjhd->bnihd', s, v)
    zeta = jnp.exp((C - 1 - idx)[None, :] * log_gamma[:, None])
    S = jnp.einsum('bnjhk,bnjhd,hj->bnhkd', k, v, zeta)
    g_chunk = jnp.exp(C * log_gamma)[None, :, None, None]

    def step(R, S_i):
        return g_chunk * R + S_i, R

    R0 = jnp.zeros((B, H, d, d), jnp.float32)
    _, R_prev = lax.scan(step, R0, jnp.moveaxis(S, 1, 0))
    R_prev = jnp.moveaxis(R_prev, 0, 1)
    xi = jnp.exp((idx + 1.0)[None, :] * log_gamma[:, None])
    cross = jnp.einsum('bnihk,bnhkd,hi->bnihd', q, R_prev, xi)
    return (inner + cross).reshape(B, L, H, d)


def head_groupnorm(y, g):
    mu = jnp.mean(y, axis=-1, keepdims=True)
    var = jnp.mean(jnp.square(y - mu), axis=-1, keepdims=True)
    yn = (y - mu) * lax.rsqrt(var + NORM_EPS)
    return yn * g.astype(jnp.float32).reshape(RET_HEADS, RET_HEAD_DIM)


def s5_scan(u, lam_re, lam_im, log_dt, b_re, b_im, c_re, c_im, d_skip):
    B, L, _ = u.shape
    uf = u.astype(jnp.float32).reshape(B, L, SSM_GROUPS, SSM_GROUP)
    lam = lax.complex(jnp.minimum(lam_re.astype(jnp.float32), -1e-4), lam_im.astype(jnp.float32))
    dt = jnp.exp(log_dt.astype(jnp.float32))[:, None]
    lam_bar = jnp.exp(lam * dt)
    b_c = lax.complex(b_re.astype(jnp.float32), b_im.astype(jnp.float32))
    b_bar = ((lam_bar - 1.0) / lam)[:, :, None] * b_c
    bu = jnp.einsum('blgc,gpc->blgp', uf.astype(jnp.complex64), b_bar)
    a = jnp.broadcast_to(lam_bar, bu.shape)

    def combine(e1, e2):
        a1, x1 = e1
        a2, x2 = e2
        return a2 * a1, a2 * x1 + x2

    _, states = lax.associative_scan(combine, (a, bu), axis=1)
    c_c = lax.complex(c_re.astype(jnp.float32), c_im.astype(jnp.float32))
    y = jnp.real(jnp.einsum('blgp,gcp->blgc', states, c_c))
    y = y + d_skip.astype(jnp.float32).reshape(SSM_GROUPS, SSM_GROUP) * uf
    return y.reshape(B, L, SSM_WIDTH)


def _fwd_setup_inputs(seed: int = 0) -> dict:
    key = jax.random.key(seed)
    ks = jax.random.split(key, 20)
    f32 = jnp.float32
    nrm = lambda k, shape, scale: (jax.random.normal(k, shape, f32) * scale)
    gain = lambda k, shape: 1.0 + 0.02 * jax.random.normal(k, shape, f32)
    x = jax.random.normal(ks[0], (BATCH, SEQ, D_MODEL), f32)
    lam_im_base = math.pi * jnp.arange(SSM_STATE, dtype=f32)
    return {
        "x": x,
        "norm_mix_pre": gain(ks[1], (DEPTH, D_MODEL)),
        "norm_mix_post": gain(ks[2], (DEPTH, D_MODEL)),
        "w_in": nrm(ks[3], (DEPTH, D_MODEL, IN_COLS), D_MODEL ** -0.5),
        "ret_gn_gain": gain(ks[4], (DEPTH, RET_WIDTH)),
        "ssm_lambda_re": -0.5 + 0.01 * jax.random.normal(ks[5], (DEPTH, SSM_GROUPS, SSM_STATE), f32),
        "ssm_lambda_im": lam_im_base + 0.01 * jax.random.normal(ks[6], (DEPTH, SSM_GROUPS, SSM_STATE), f32),
        "ssm_log_dt": jax.random.uniform(ks[7], (DEPTH, SSM_GROUPS), f32, math.log(DT_MIN), math.log(DT_MAX)),
        "ssm_b_re": nrm(ks[8], (DEPTH, SSM_GROUPS, SSM_STATE, SSM_GROUP), (2 * SSM_GROUP) ** -0.5),
        "ssm_b_im": nrm(ks[9], (DEPTH, SSM_GROUPS, SSM_STATE, SSM_GROUP), (2 * SSM_GROUP) ** -0.5),
        "ssm_c_re": nrm(ks[10], (DEPTH, SSM_GROUPS, SSM_GROUP, SSM_STATE), (2 * SSM_STATE) ** -0.5),
        "ssm_c_im": nrm(ks[11], (DEPTH, SSM_GROUPS, SSM_GROUP, SSM_STATE), (2 * SSM_STATE) ** -0.5),
        "ssm_d": nrm(ks[12], (DEPTH, SSM_WIDTH), 1.0),
        "w_glu": nrm(ks[13], (DEPTH, SSM_WIDTH, 2 * SSM_WIDTH), SSM_WIDTH ** -0.5),
        "w_out": nrm(ks[14], (DEPTH, D_MIX, D_MODEL), D_MIX ** -0.5),
        "norm_mlp_pre": gain(ks[15], (DEPTH, D_MODEL)),
        "norm_mlp_post": gain(ks[16], (DEPTH, D_MODEL)),
        "w_ff1": nrm(ks[17], (DEPTH, D_MODEL, D_FF), D_MODEL ** -0.5),
        "w_ff2": nrm(ks[18], (DEPTH, D_FF, D_MODEL), D_FF ** -0.5),
    }


def _fwd_reference(x, norm_mix_pre, norm_mix_post, w_in, ret_gn_gain, ssm_lambda_re, ssm_lambda_im,
              ssm_log_dt, ssm_b_re, ssm_b_im, ssm_c_re, ssm_c_im, ssm_d, w_glu, w_out,
              norm_mlp_pre, norm_mlp_post, w_ff1, w_ff2):
    B, L, _ = x.shape
    for i in range(DEPTH):
        h = rmsnorm(x, norm_mix_pre[i])
        proj = h @ w_in[i]
        q, k, v, gate, u = jnp.split(proj, [RET_WIDTH, 2 * RET_WIDTH, 3 * RET_WIDTH, 4 * RET_WIDTH], axis=-1)
        heads = lambda t: t.astype(jnp.float32).reshape(B, L, RET_HEADS, RET_HEAD_DIM)
        qh = rope(heads(q))
        kh = rope(heads(k)) * (RET_HEAD_DIM ** -0.5)
        vh = heads(v)
        y_ret = head_groupnorm(retention_chunkwise(qh, kh, vh), ret_gn_gain[i]).reshape(B, L, RET_WIDTH)
        y_ret = (jax.nn.silu(gate.astype(jnp.float32)) * y_ret).astype(x.dtype)

        y_ssm = jax.nn.gelu(s5_scan(u, ssm_lambda_re[i], ssm_lambda_im[i], ssm_log_dt[i], ssm_b_re[i],
                                    ssm_b_im[i], ssm_c_re[i], ssm_c_im[i], ssm_d[i])).astype(x.dtype)
        glu_a, glu_b = jnp.split(y_ssm @ w_glu[i], 2, axis=-1)
        y_ssm = glu_a * jax.nn.sigmoid(glu_b)

        mix = jnp.concatenate([y_ret, y_ssm], axis=-1) @ w_out[i]
        x = x + rmsnorm(mix, norm_mix_post[i])

        h = rmsnorm(x, norm_mlp_pre[i])
        m = jnp.square(jax.nn.relu(h @ w_ff1[i])) @ w_ff2[i]
        x = x + rmsnorm(m, norm_mlp_post[i])
    return x


import jax as _jax
import jax.numpy as _jnp

TWIN_FORMAT = 'train_step'
FWD_PARAMS = ['x', 'norm_mix_pre', 'norm_mix_post', 'w_in', 'ret_gn_gain', 'ssm_lambda_re', 'ssm_lambda_im', 'ssm_log_dt', 'ssm_b_re', 'ssm_b_im', 'ssm_c_re', 'ssm_c_im', 'ssm_d', 'w_glu', 'w_out', 'norm_mlp_pre', 'norm_mlp_post', 'w_ff1', 'w_ff2']
TWIN_WEIGHTS = ['norm_mix_pre', 'norm_mix_post', 'w_in', 'ret_gn_gain', 'ssm_lambda_re', 'ssm_lambda_im', 'ssm_log_dt', 'ssm_b_re', 'ssm_b_im', 'ssm_c_re', 'ssm_c_im', 'ssm_d', 'w_glu', 'w_out', 'norm_mlp_pre', 'norm_mlp_post', 'w_ff1', 'w_ff2']
TWIN_DIFF_INPUT = 'x'
TWIN_INPUTS = ['x', 'norm_mix_pre', 'norm_mix_post', 'w_in', 'ret_gn_gain', 'ssm_lambda_re', 'ssm_lambda_im', 'ssm_log_dt', 'ssm_b_re', 'ssm_b_im', 'ssm_c_re', 'ssm_c_im', 'ssm_d', 'w_glu', 'w_out', 'norm_mlp_pre', 'norm_mlp_post', 'w_ff1', 'w_ff2', 'loss_target', 'm_norm_mix_pre', 'm_norm_mix_post', 'm_w_in', 'm_ret_gn_gain', 'm_ssm_lambda_re', 'm_ssm_lambda_im', 'm_ssm_log_dt', 'm_ssm_b_re', 'm_ssm_b_im', 'm_ssm_c_re', 'm_ssm_c_im', 'm_ssm_d', 'm_w_glu', 'm_w_out', 'm_norm_mlp_pre', 'm_norm_mlp_post', 'm_w_ff1', 'm_w_ff2', 'v_norm_mix_pre', 'v_norm_mix_post', 'v_w_in', 'v_ret_gn_gain', 'v_ssm_lambda_re', 'v_ssm_lambda_im', 'v_ssm_log_dt', 'v_ssm_b_re', 'v_ssm_b_im', 'v_ssm_c_re', 'v_ssm_c_im', 'v_ssm_d', 'v_w_glu', 'v_w_out', 'v_norm_mlp_pre', 'v_norm_mlp_post', 'v_w_ff1', 'v_w_ff2']
TWIN_OUTPUTS = ['loss', 'grad_x', 'grad_norm_mix_pre', 'grad_norm_mix_post', 'grad_w_in', 'grad_ret_gn_gain', 'grad_ssm_lambda_re', 'grad_ssm_lambda_im', 'grad_ssm_log_dt', 'grad_ssm_b_re', 'grad_ssm_b_im', 'grad_ssm_c_re', 'grad_ssm_c_im', 'grad_ssm_d', 'grad_w_glu', 'grad_w_out', 'grad_norm_mlp_pre', 'grad_norm_mlp_post', 'grad_w_ff1', 'grad_w_ff2', 'delta_norm_mix_pre', 'delta_norm_mix_post', 'delta_w_in', 'delta_ret_gn_gain', 'delta_ssm_lambda_re', 'delta_ssm_lambda_im', 'delta_ssm_log_dt', 'delta_ssm_b_re', 'delta_ssm_b_im', 'delta_ssm_c_re', 'delta_ssm_c_im', 'delta_ssm_d', 'delta_w_glu', 'delta_w_out', 'delta_norm_mlp_pre', 'delta_norm_mlp_post', 'delta_w_ff1', 'delta_w_ff2', 'new_m_norm_mix_pre', 'new_m_norm_mix_post', 'new_m_w_in', 'new_m_ret_gn_gain', 'new_m_ssm_lambda_re', 'new_m_ssm_lambda_im', 'new_m_ssm_log_dt', 'new_m_ssm_b_re', 'new_m_ssm_b_im', 'new_m_ssm_c_re', 'new_m_ssm_c_im', 'new_m_ssm_d', 'new_m_w_glu', 'new_m_w_out', 'new_m_norm_mlp_pre', 'new_m_norm_mlp_post', 'new_m_w_ff1', 'new_m_w_ff2', 'new_v_norm_mix_pre', 'new_v_norm_mix_post', 'new_v_w_in', 'new_v_ret_gn_gain', 'new_v_ssm_lambda_re', 'new_v_ssm_lambda_im', 'new_v_ssm_log_dt', 'new_v_ssm_b_re', 'new_v_ssm_b_im', 'new_v_ssm_c_re', 'new_v_ssm_c_im', 'new_v_ssm_d', 'new_v_w_glu', 'new_v_w_out', 'new_v_norm_mlp_pre', 'new_v_norm_mlp_post', 'new_v_w_ff1', 'new_v_w_ff2']
TWIN_LEAF_KINDS = {'loss': 'loss', 'grad_x': 'grad_x', 'grad_norm_mix_pre': 'grad_w', 'grad_norm_mix_post': 'grad_w', 'grad_w_in': 'grad_w', 'grad_ret_gn_gain': 'grad_w', 'grad_ssm_lambda_re': 'grad_w', 'grad_ssm_lambda_im': 'grad_w', 'grad_ssm_log_dt': 'grad_w', 'grad_ssm_b_re': 'grad_w', 'grad_ssm_b_im': 'grad_w', 'grad_ssm_c_re': 'grad_w', 'grad_ssm_c_im': 'grad_w', 'grad_ssm_d': 'grad_w', 'grad_w_glu': 'grad_w', 'grad_w_out': 'grad_w', 'grad_norm_mlp_pre': 'grad_w', 'grad_norm_mlp_post': 'grad_w', 'grad_w_ff1': 'grad_w', 'grad_w_ff2': 'grad_w', 'delta_norm_mix_pre': 'delta_w', 'delta_norm_mix_post': 'delta_w', 'delta_w_in': 'delta_w', 'delta_ret_gn_gain': 'delta_w', 'delta_ssm_lambda_re': 'delta_w', 'delta_ssm_lambda_im': 'delta_w', 'delta_ssm_log_dt': 'delta_w', 'delta_ssm_b_re': 'delta_w', 'delta_ssm_b_im': 'delta_w', 'delta_ssm_c_re': 'delta_w', 'delta_ssm_c_im': 'delta_w', 'delta_ssm_d': 'delta_w', 'delta_w_glu': 'delta_w', 'delta_w_out': 'delta_w', 'delta_norm_mlp_pre': 'delta_w', 'delta_norm_mlp_post': 'delta_w', 'delta_w_ff1': 'delta_w', 'delta_w_ff2': 'delta_w', 'new_m_norm_mix_pre': 'new_m', 'new_m_norm_mix_post': 'new_m', 'new_m_w_in': 'new_m', 'new_m_ret_gn_gain': 'new_m', 'new_m_ssm_lambda_re': 'new_m', 'new_m_ssm_lambda_im': 'new_m', 'new_m_ssm_log_dt': 'new_m', 'new_m_ssm_b_re': 'new_m', 'new_m_ssm_b_im': 'new_m', 'new_m_ssm_c_re': 'new_m', 'new_m_ssm_c_im': 'new_m', 'new_m_ssm_d': 'new_m', 'new_m_w_glu': 'new_m', 'new_m_w_out': 'new_m', 'new_m_norm_mlp_pre': 'new_m', 'new_m_norm_mlp_post': 'new_m', 'new_m_w_ff1': 'new_m', 'new_m_w_ff2': 'new_m', 'new_v_norm_mix_pre': 'new_v', 'new_v_norm_mix_post': 'new_v', 'new_v_w_in': 'new_v', 'new_v_ret_gn_gain': 'new_v', 'new_v_ssm_lambda_re': 'new_v', 'new_v_ssm_lambda_im': 'new_v', 'new_v_ssm_log_dt': 'new_v', 'new_v_ssm_b_re': 'new_v', 'new_v_ssm_b_im': 'new_v', 'new_v_ssm_c_re': 'new_v', 'new_v_ssm_c_im': 'new_v', 'new_v_ssm_d': 'new_v', 'new_v_w_glu': 'new_v', 'new_v_w_out': 'new_v', 'new_v_norm_mlp_pre': 'new_v', 'new_v_norm_mlp_post': 'new_v', 'new_v_w_ff1': 'new_v', 'new_v_w_ff2': 'new_v'}


def _forward(args):
    return _fwd_reference(*[args[k] for k in FWD_PARAMS])


def _output_shape():
    def fwd():
        inp = _fwd_setup_inputs(0)
        return _fwd_reference(*[inp[k] for k in FWD_PARAMS])
    out = _jax.eval_shape(fwd)
    return out.shape, out.dtype

N_MICROBATCH = 1
ADAM_LR = 0.001
ADAM_B1 = 0.9
ADAM_B2 = 0.999
ADAM_EPS = 1e-08
ADAM_WD = 0.01
ADAM_STEP = 10
PER_EXAMPLE_BATCH_AXIS = {'x': 0, 'loss_target': 0}
SHARED_INPUTS = []
_WEIGHT_DTYPES = {'norm_mix_pre': _jnp.float32, 'norm_mix_post': _jnp.float32, 'w_in': _jnp.float32, 'ret_gn_gain': _jnp.float32, 'ssm_lambda_re': _jnp.float32, 'ssm_lambda_im': _jnp.float32, 'ssm_log_dt': _jnp.float32, 'ssm_b_re': _jnp.float32, 'ssm_b_im': _jnp.float32, 'ssm_c_re': _jnp.float32, 'ssm_c_im': _jnp.float32, 'ssm_d': _jnp.float32, 'w_glu': _jnp.float32, 'w_out': _jnp.float32, 'norm_mlp_pre': _jnp.float32, 'norm_mlp_post': _jnp.float32, 'w_ff1': _jnp.float32, 'w_ff2': _jnp.float32}
MOMENT_SCALE = {'norm_mix_pre': 1.418727e+00, 'norm_mix_post': 6.462435e+01, 'w_in': 7.544300e-01, 'ret_gn_gain': 1.802169e+00, 'ssm_lambda_re': 2.945242e-02, 'ssm_lambda_im': 3.583090e-02, 'ssm_log_dt': 2.497435e+01, 'ssm_b_re': 1.758791e-02, 'ssm_b_im': 1.898992e-02, 'ssm_c_re': 3.781282e-02, 'ssm_c_im': 3.607607e-02, 'ssm_d': 1.008720e+01, 'w_glu': 6.383799e+00, 'w_out': 6.165068e+00, 'norm_mlp_pre': 2.130698e+00, 'norm_mlp_post': 6.645285e+01, 'w_ff1': 1.103276e+00, 'w_ff2': 6.167134e+00}


def _to_microbatches(a, axis):
    t = _jnp.moveaxis(a, axis, 0)
    t = t.reshape((N_MICROBATCH, t.shape[0] // N_MICROBATCH) + t.shape[1:])
    return _jnp.moveaxis(t, 1, axis + 1)


def setup_inputs(seed: int = 0) -> dict:
    inp = _fwd_setup_inputs(seed)
    key = _jax.random.fold_in(_jax.random.key(seed), 7919)
    shape, _ = _output_shape()
    out = dict(inp)
    out["loss_target"] = _jax.random.normal(_jax.random.fold_in(key, 0), shape, _jnp.float32)
    for i, name in enumerate(TWIN_WEIGHTS):
        w = inp[name].astype(_jnp.float32)
        if MOMENT_SCALE is None:
            s = _jnp.sqrt(_jnp.mean(_jnp.square(w)) + 1e-30)
        else:
            s = MOMENT_SCALE[name]
        km, kv = _jax.random.split(_jax.random.fold_in(key, i + 1))
        out[name] = w
        out["m_" + name] = s * _jax.random.normal(km, w.shape, _jnp.float32)
        out["v_" + name] = (s * s) * _jax.random.uniform(kv, w.shape, _jnp.float32, 0.5, 1.5)
    if N_MICROBATCH > 1:
        for name, axis in PER_EXAMPLE_BATCH_AXIS.items():
            out[name] = _to_microbatches(out[name], axis)
    return {'x': out['x'], 'norm_mix_pre': out['norm_mix_pre'], 'norm_mix_post': out['norm_mix_post'], 'w_in': out['w_in'], 'ret_gn_gain': out['ret_gn_gain'], 'ssm_lambda_re': out['ssm_lambda_re'], 'ssm_lambda_im': out['ssm_lambda_im'], 'ssm_log_dt': out['ssm_log_dt'], 'ssm_b_re': out['ssm_b_re'], 'ssm_b_im': out['ssm_b_im'], 'ssm_c_re': out['ssm_c_re'], 'ssm_c_im': out['ssm_c_im'], 'ssm_d': out['ssm_d'], 'w_glu': out['w_glu'], 'w_out': out['w_out'], 'norm_mlp_pre': out['norm_mlp_pre'], 'norm_mlp_post': out['norm_mlp_post'], 'w_ff1': out['w_ff1'], 'w_ff2': out['w_ff2'], 'loss_target': out['loss_target'], 'm_norm_mix_pre': out['m_norm_mix_pre'], 'm_norm_mix_post': out['m_norm_mix_post'], 'm_w_in': out['m_w_in'], 'm_ret_gn_gain': out['m_ret_gn_gain'], 'm_ssm_lambda_re': out['m_ssm_lambda_re'], 'm_ssm_lambda_im': out['m_ssm_lambda_im'], 'm_ssm_log_dt': out['m_ssm_log_dt'], 'm_ssm_b_re': out['m_ssm_b_re'], 'm_ssm_b_im': out['m_ssm_b_im'], 'm_ssm_c_re': out['m_ssm_c_re'], 'm_ssm_c_im': out['m_ssm_c_im'], 'm_ssm_d': out['m_ssm_d'], 'm_w_glu': out['m_w_glu'], 'm_w_out': out['m_w_out'], 'm_norm_mlp_pre': out['m_norm_mlp_pre'], 'm_norm_mlp_post': out['m_norm_mlp_post'], 'm_w_ff1': out['m_w_ff1'], 'm_w_ff2': out['m_w_ff2'], 'v_norm_mix_pre': out['v_norm_mix_pre'], 'v_norm_mix_post': out['v_norm_mix_post'], 'v_w_in': out['v_w_in'], 'v_ret_gn_gain': out['v_ret_gn_gain'], 'v_ssm_lambda_re': out['v_ssm_lambda_re'], 'v_ssm_lambda_im': out['v_ssm_lambda_im'], 'v_ssm_log_dt': out['v_ssm_log_dt'], 'v_ssm_b_re': out['v_ssm_b_re'], 'v_ssm_b_im': out['v_ssm_b_im'], 'v_ssm_c_re': out['v_ssm_c_re'], 'v_ssm_c_im': out['v_ssm_c_im'], 'v_ssm_d': out['v_ssm_d'], 'v_w_glu': out['v_w_glu'], 'v_w_out': out['v_w_out'], 'v_norm_mlp_pre': out['v_norm_mlp_pre'], 'v_norm_mlp_post': out['v_norm_mlp_post'], 'v_w_ff1': out['v_w_ff1'], 'v_w_ff2': out['v_w_ff2']}


def _loss(weights, diff, rest, loss_target):
    with _jax.named_scope("forward"):
        args = {**rest, TWIN_DIFF_INPUT: diff, **{k: w.astype(_WEIGHT_DTYPES[k]) for k, w in weights.items()}}
        y = _forward(args)
    with _jax.named_scope("loss_head"):
        err = _jnp.square(y.astype(_jnp.float32) - loss_target)
        return 0.5 * _jnp.sum(_jnp.mean(err, axis=-1)) if err.ndim else 0.5 * err


def _adamw(w, g, m, v):
    m = ADAM_B1 * m + (1.0 - ADAM_B1) * g
    v = ADAM_B2 * v + (1.0 - ADAM_B2) * _jnp.square(g)
    m_hat = m / (1.0 - ADAM_B1 ** ADAM_STEP)
    v_hat = v / (1.0 - ADAM_B2 ** ADAM_STEP)
    delta = -ADAM_LR * (m_hat / (_jnp.sqrt(v_hat) + ADAM_EPS) + ADAM_WD * w)
    return delta, m, v


def reference(x, norm_mix_pre, norm_mix_post, w_in, ret_gn_gain, ssm_lambda_re, ssm_lambda_im, ssm_log_dt, ssm_b_re, ssm_b_im, ssm_c_re, ssm_c_im, ssm_d, w_glu, w_out, norm_mlp_pre, norm_mlp_post, w_ff1, w_ff2, loss_target, m_norm_mix_pre, m_norm_mix_post, m_w_in, m_ret_gn_gain, m_ssm_lambda_re, m_ssm_lambda_im, m_ssm_log_dt, m_ssm_b_re, m_ssm_b_im, m_ssm_c_re, m_ssm_c_im, m_ssm_d, m_w_glu, m_w_out, m_norm_mlp_pre, m_norm_mlp_post, m_w_ff1, m_w_ff2, v_norm_mix_pre, v_norm_mix_post, v_w_in, v_ret_gn_gain, v_ssm_lambda_re, v_ssm_lambda_im, v_ssm_log_dt, v_ssm_b_re, v_ssm_b_im, v_ssm_c_re, v_ssm_c_im, v_ssm_d, v_w_glu, v_w_out, v_norm_mlp_pre, v_norm_mlp_post, v_w_ff1, v_w_ff2):
    given = dict(x=x, norm_mix_pre=norm_mix_pre, norm_mix_post=norm_mix_post, w_in=w_in, ret_gn_gain=ret_gn_gain, ssm_lambda_re=ssm_lambda_re, ssm_lambda_im=ssm_lambda_im, ssm_log_dt=ssm_log_dt, ssm_b_re=ssm_b_re, ssm_b_im=ssm_b_im, ssm_c_re=ssm_c_re, ssm_c_im=ssm_c_im, ssm_d=ssm_d, w_glu=w_glu, w_out=w_out, norm_mlp_pre=norm_mlp_pre, norm_mlp_post=norm_mlp_post, w_ff1=w_ff1, w_ff2=w_ff2, loss_target=loss_target, m_norm_mix_pre=m_norm_mix_pre, m_norm_mix_post=m_norm_mix_post, m_w_in=m_w_in, m_ret_gn_gain=m_ret_gn_gain, m_ssm_lambda_re=m_ssm_lambda_re, m_ssm_lambda_im=m_ssm_lambda_im, m_ssm_log_dt=m_ssm_log_dt, m_ssm_b_re=m_ssm_b_re, m_ssm_b_im=m_ssm_b_im, m_ssm_c_re=m_ssm_c_re, m_ssm_c_im=m_ssm_c_im, m_ssm_d=m_ssm_d, m_w_glu=m_w_glu, m_w_out=m_w_out, m_norm_mlp_pre=m_norm_mlp_pre, m_norm_mlp_post=m_norm_mlp_post, m_w_ff1=m_w_ff1, m_w_ff2=m_w_ff2, v_norm_mix_pre=v_norm_mix_pre, v_norm_mix_post=v_norm_mix_post, v_w_in=v_w_in, v_ret_gn_gain=v_ret_gn_gain, v_ssm_lambda_re=v_ssm_lambda_re, v_ssm_lambda_im=v_ssm_lambda_im, v_ssm_log_dt=v_ssm_log_dt, v_ssm_b_re=v_ssm_b_re, v_ssm_b_im=v_ssm_b_im, v_ssm_c_re=v_ssm_c_re, v_ssm_c_im=v_ssm_c_im, v_ssm_d=v_ssm_d, v_w_glu=v_w_glu, v_w_out=v_w_out, v_norm_mlp_pre=v_norm_mlp_pre, v_norm_mlp_post=v_norm_mlp_post, v_w_ff1=v_w_ff1, v_w_ff2=v_w_ff2)
    weights = {n: given[n] for n in TWIN_WEIGHTS}
    shared = {n: given[n] for n in SHARED_INPUTS}
    per_example = {n: given[n] for n in ['x']}
    grad_fn = _jax.value_and_grad(_loss, argnums=(0, 1))

    def one_microbatch(ex, loss_target):
        ex = dict(ex)
        diff = ex.pop(TWIN_DIFF_INPUT)
        return grad_fn(weights, diff, {**shared, **ex}, loss_target)

    if N_MICROBATCH == 1:
        loss, (grad_w, grad_x) = one_microbatch(per_example, given["loss_target"])
    else:
        def body(carry, xs):
            loss_sum, grad_sum = carry
            l_k, (gw_k, gx_k) = one_microbatch(xs[0], xs[1])
            with _jax.named_scope("update"):
                return (loss_sum + l_k, _jax.tree.map(_jnp.add, grad_sum, gw_k)), gx_k

        init = (_jnp.zeros((), _jnp.float32), _jax.tree.map(_jnp.zeros_like, weights))
        (loss, grad_w), grad_x = _jax.lax.scan(body, init, (per_example, given["loss_target"]))
    with _jax.named_scope("update"):
        delta_w, new_m, new_v = {}, {}, {}
        for n in TWIN_WEIGHTS:
            delta_w[n], new_m[n], new_v[n] = _adamw(weights[n], grad_w[n], given["m_" + n], given["v_" + n])
    return (loss, grad_x, *[grad_w[n] for n in TWIN_WEIGHTS], *[delta_w[n] for n in TWIN_WEIGHTS],
            *[new_m[n] for n in TWIN_WEIGHTS], *[new_v[n] for n in TWIN_WEIGHTS])
```

```python
import functools
import math

import jax
import jax.numpy as jnp
from jax import lax
from jax.experimental import pallas as pl
from jax.experimental.pallas import tpu as pltpu

F32 = jnp.float32
BF16 = jnp.bfloat16

D_MODEL = 1024
RET_WIDTH = 512
RET_HEADS = 4
HEAD_DIM = 128
RET_CHUNK = 128
ROPE_BASE = 10000.0
SSM_WIDTH = 512
SSM_GROUP = 16
SSM_GROUPS = 32
SSM_STATE = 64
N_STATES = SSM_GROUPS * SSM_STATE
D_FF = 4096
IN_COLS = 4 * RET_WIDTH + SSM_WIDTH
NORM_EPS = 1e-6
K_SCALE = HEAD_DIM ** -0.5

ADAM_LR = 0.001
ADAM_B1 = 0.9
ADAM_B2 = 0.999
ADAM_EPS = 1e-08
ADAM_WD = 0.01
ADAM_STEP = 10

LANES = 128
SUBLANES = 8
VMEM_LIMIT = 56 * 2 ** 20

TOK_TILE = 512
MLP_TILE = 256
RET_CHUNKS_PER_STEP = 4
S5_TILE = 256
S5_STEPS = S5_TILE // SUBLANES
S5_COLBLK = N_STATES // LANES
S5_GROUP = 4

N_CHIPS = 4
MESH = pl.DeviceIdType.MESH


def _dot(a, b):
    return jnp.dot(a, b, preferred_element_type=F32)


def _dot_nt(a, b):
    return lax.dot_general(a, b, (((1,), (1,)), ((), ())), preferred_element_type=F32)


def _dot_tn(a, b):
    return lax.dot_general(a, b, (((0,), (0,)), ((), ())), preferred_element_type=F32)


def _sigmoid(x):
    return 1.0 / (1.0 + jnp.exp(-x))


_GELU_C = math.sqrt(2.0 / math.pi)


def _gelu(x):
    return 0.5 * x * (1.0 + jnp.tanh(_GELU_C * (x + 0.044715 * (x * x * x))))


def _gelu_grad(x):
    t = jnp.tanh(_GELU_C * (x + 0.044715 * (x * x * x)))
    return 0.5 * (1.0 + t) + 0.5 * x * (1.0 - t * t) * (_GELU_C * (1.0 + 3.0 * 0.044715 * (x * x)))


def _rms(x):
    r = lax.rsqrt(jnp.mean(x * x, axis=-1, keepdims=True) + NORM_EPS)
    return x * r, r


def _rms_bwd(n, r, gain, dy):
    dn = dy * gain
    dx = r * (dn - n * jnp.mean(dn * n, axis=-1, keepdims=True))
    return dx, jnp.sum(dy * n, axis=0, keepdims=True)


def _full(shape):
    nd = len(shape)
    return pl.BlockSpec(shape, lambda i, _n=nd: (0,) * _n, pipeline_mode=pl.Buffered(1))


def _rows(tile, width):
    return pl.BlockSpec((tile, width), lambda i: (i, 0))


def _rows_rev(tile, width, n):
    return pl.BlockSpec((tile, width), lambda i, _n=n: (_n - 1 - i, 0))


def _params(sem):
    return pltpu.CompilerParams(dimension_semantics=(sem,), vmem_limit_bytes=VMEM_LIMIT)


def _rope(t, cos2, sin2):
    return t * cos2 + pltpu.roll(t, HEAD_DIM // 2, 1) * sin2


def _rope_bwd(d, cos2, sin2):
    return d * cos2 + pltpu.roll(d * sin2, HEAD_DIM // 2, 1)


def _inproj_fwd(x, gain, w_in, cos2, sin2):
    seq = x.shape[0]

    def body(x_ref, g_ref, w_ref, c_ref, s_ref, q_ref, k_ref, v_ref, gate_ref, u_ref):
        n, _ = _rms(x_ref[...])
        h = (n * g_ref[...]).astype(BF16)
        proj = _dot(h, w_ref[...])
        c = c_ref[...]
        s = s_ref[...]
        for hd in range(RET_HEADS):
            lo = hd * HEAD_DIM
            q_ref[:, lo:lo + HEAD_DIM] = _rope(proj[:, lo:lo + HEAD_DIM], c, s).astype(BF16)
            kh = proj[:, RET_WIDTH + lo:RET_WIDTH + lo + HEAD_DIM]
            k_ref[:, lo:lo + HEAD_DIM] = (_rope(kh, c, s) * K_SCALE).astype(BF16)
        v_ref[...] = proj[:, 2 * RET_WIDTH:3 * RET_WIDTH].astype(BF16)
        gate_ref[...] = proj[:, 3 * RET_WIDTH:4 * RET_WIDTH]
        u_ref[...] = proj[:, 4 * RET_WIDTH:]

    t = TOK_TILE
    half = lambda dt: jax.ShapeDtypeStruct((seq, RET_WIDTH), dt)
    return pl.pallas_call(
        body, name="inproj_fwd", grid=(seq // t,),
        in_specs=[_rows(t, D_MODEL), _full((1, D_MODEL)), _full((D_MODEL, IN_COLS)), _rows(t, HEAD_DIM), _rows(t, HEAD_DIM)],
        out_specs=[_rows(t, RET_WIDTH)] * 5,
        out_shape=[half(BF16), half(BF16), half(BF16), half(F32), half(F32)],
        compiler_params=_params("parallel"),
    )(x, gain, w_in, cos2, sin2)


def _inproj_bwd(x, gain, w_in, cos2, sin2, dq, dk, dv, dgate, du, dres):
    seq = x.shape[0]
    t = TOK_TILE

    def body(x_ref, g_ref, w_ref, c_ref, s_ref, dq_ref, dk_ref, dv_ref, dgate_ref, du_ref, dres_ref,
             dx_ref, dw_ref, dg_ref, dproj):
        @pl.when(pl.program_id(0) == 0)
        def _():
            dw_ref[...] = jnp.zeros_like(dw_ref)
            dg_ref[...] = jnp.zeros_like(dg_ref)

        c = c_ref[...]
        s = s_ref[...]
        for hd in range(RET_HEADS):
            lo = hd * HEAD_DIM
            dproj[:, lo:lo + HEAD_DIM] = _rope_bwd(dq_ref[:, lo:lo + HEAD_DIM], c, s).astype(BF16)
            dproj[:, RET_WIDTH + lo:RET_WIDTH + lo + HEAD_DIM] = _rope_bwd(
                dk_ref[:, lo:lo + HEAD_DIM] * K_SCALE, c, s).astype(BF16)
        dproj[:, 2 * RET_WIDTH:3 * RET_WIDTH] = dv_ref[...].astype(BF16)
        dproj[:, 3 * RET_WIDTH:4 * RET_WIDTH] = dgate_ref[...].astype(BF16)
        dproj[:, 4 * RET_WIDTH:] = du_ref[...].astype(BF16)
        dp = dproj[...]
        n, r = _rms(x_ref[...])
        gain_v = g_ref[...]
        h = (n * gain_v).astype(BF16)
        dw_ref[...] += _dot_tn(h, dp)
        dh = _dot_nt(dp, w_ref[...])
        dx, dgs = _rms_bwd(n, r, gain_v, dh)
        dg_ref[...] += dgs
        dx_ref[...] = dres_ref[...] + dx

    return pl.pallas_call(
        body, name="inproj_bwd", grid=(seq // t,),
        in_specs=[_rows(t, D_MODEL), _full((1, D_MODEL)), _full((D_MODEL, IN_COLS)), _rows(t, HEAD_DIM), _rows(t, HEAD_DIM)]
        + [_rows(t, RET_WIDTH)] * 5 + [_rows(t, D_MODEL)],
        out_specs=[_rows(t, D_MODEL), _full((D_MODEL, IN_COLS)), _full((1, D_MODEL))],
        out_shape=[jax.ShapeDtypeStruct((seq, D_MODEL), F32), jax.ShapeDtypeStruct((D_MODEL, IN_COLS), F32),
                   jax.ShapeDtypeStruct((1, D_MODEL), F32)],
        scratch_shapes=[pltpu.VMEM((t, IN_COLS), BF16)],
        compiler_params=_params("arbitrary"),
    )(x, gain, w_in, cos2, sin2, dq, dk, dv, dgate, du, dres)


def _ret_consts():
    c = RET_CHUNK
    log_gamma = jnp.log(1.0 - jnp.exp(jnp.linspace(math.log(1.0 / 32), math.log(1.0 / 512), RET_HEADS))).astype(F32)
    idx = jnp.arange(c, dtype=F32)
    diff = idx[:, None] - idx[None, :]
    decay = jnp.where(diff[None] >= 0, jnp.exp(jnp.maximum(diff, 0.0)[None] * log_gamma[:, None, None]), 0.0)
    zeta = jnp.exp((c - 1 - idx)[None, :] * log_gamma[:, None])
    xi = jnp.exp((idx + 1.0)[None, :] * log_gamma[:, None])
    g_chunk = jnp.exp(c * log_gamma)
    wide = lambda rowvals: jnp.broadcast_to(rowvals[:, :, None], (RET_HEADS, c, c)).astype(F32)
    return decay.astype(F32), wide(xi), wide(zeta), jnp.broadcast_to(g_chunk[:, None, None], (RET_HEADS, c, c)).astype(F32)


def _ret_fwd(q, k, v, gate, gn_gain, consts):
    seq = q.shape[0]
    cps = RET_CHUNKS_PER_STEP
    t = cps * RET_CHUNK
    n_chunks = seq // RET_CHUNK
    dec, xi, zeta, gch = consts

    def body(q_ref, k_ref, v_ref, gate_ref, gain_ref, dec_ref, xi_ref, zeta_ref, gch_ref, y_ref, rprev_ref, state):
        @pl.when(pl.program_id(0) == 0)
        def _():
            state[...] = jnp.zeros_like(state)

        for cc in range(cps):
            r0 = cc * RET_CHUNK
            for hd in range(RET_HEADS):
                lo = hd * HEAD_DIM
                qh = q_ref[r0:r0 + RET_CHUNK, lo:lo + HEAD_DIM]
                kh = k_ref[r0:r0 + RET_CHUNK, lo:lo + HEAD_DIM]
                vh = v_ref[r0:r0 + RET_CHUNK, lo:lo + HEAD_DIM]
                s = _dot_nt(qh, kh) * dec_ref[hd]
                rp = state[hd]
                rpb = rp.astype(BF16)
                rprev_ref[cc, hd] = rpb
                qx = (qh.astype(F32) * xi_ref[hd]).astype(BF16)
                o = _dot(s.astype(BF16), vh) + _dot(qx, rpb)
                oc = o - jnp.mean(o, axis=-1, keepdims=True)
                on = oc * lax.rsqrt(jnp.mean(oc * oc, axis=-1, keepdims=True) + NORM_EPS)
                g = gate_ref[r0:r0 + RET_CHUNK, lo:lo + HEAD_DIM]
                y_ref[r0:r0 + RET_CHUNK, lo:lo + HEAD_DIM] = (g * _sigmoid(g) * (on * gain_ref[:, lo:lo + HEAD_DIM])).astype(BF16)
                vz = (vh.astype(F32) * zeta_ref[hd]).astype(BF16)
                state[hd] = gch_ref[hd] * rp + _dot_tn(kh, vz)

    cst = _full((RET_HEADS, RET_CHUNK, RET_CHUNK))
    return pl.pallas_call(
        body, name="ret_fwd", grid=(seq // t,),
        in_specs=[_rows(t, RET_WIDTH)] * 4 + [_full((1, RET_WIDTH)), cst, cst, cst, cst],
        out_specs=[_rows(t, RET_WIDTH), pl.BlockSpec((cps, RET_HEADS, HEAD_DIM, HEAD_DIM), lambda i: (i, 0, 0, 0))],
        out_shape=[jax.ShapeDtypeStruct((seq, RET_WIDTH), BF16),
                   jax.ShapeDtypeStruct((n_chunks, RET_HEADS, HEAD_DIM, HEAD_DIM), BF16)],
        scratch_shapes=[pltpu.VMEM((RET_HEADS, HEAD_DIM, HEAD_DIM), F32)],
        compiler_params=_params("arbitrary"),
    )(q, k, v, gate, gn_gain, dec, xi, zeta, gch)


def _ret_bwd(q, k, v, gate, gn_gain, consts, rprev, dy_ret):
    seq = q.shape[0]
    cps = RET_CHUNKS_PER_STEP
    t = cps * RET_CHUNK
    nt = seq // t
    dec, xi, zeta, gch = consts

    def body(q_ref, k_ref, v_ref, gate_ref, gain_ref, dec_ref, xi_ref, zeta_ref, gch_ref, rprev_ref, dyr_ref,
             dq_ref, dk_ref, dv_ref, dgate_ref, dgain_ref, dstate):
        @pl.when(pl.program_id(0) == 0)
        def _():
            dstate[...] = jnp.zeros_like(dstate)
            dgain_ref[...] = jnp.zeros_like(dgain_ref)

        for cc in reversed(range(cps)):
            rows = slice(cc * RET_CHUNK, (cc + 1) * RET_CHUNK)
            for hd in range(RET_HEADS):
                cols = slice(hd * HEAD_DIM, (hd + 1) * HEAD_DIM)
                qh = q_ref[rows, cols]
                kh = k_ref[rows, cols]
                vh = v_ref[rows, cols]
                dech = dec_ref[hd]
                xih = xi_ref[hd]
                zetah = zeta_ref[hd]
                rpb = rprev_ref[cc, hd]
                sdb = (_dot_nt(qh, kh) * dech).astype(BF16)
                qx = (qh.astype(F32) * xih).astype(BF16)
                o = _dot(sdb, vh) + _dot(qx, rpb)
                oc = o - jnp.mean(o, axis=-1, keepdims=True)
                rstd = lax.rsqrt(jnp.mean(oc * oc, axis=-1, keepdims=True) + NORM_EPS)
                on = oc * rstd
                g = gate_ref[rows, cols]
                sg = _sigmoid(g)
                gain_h = gain_ref[:, cols]
                dyr = dyr_ref[rows, cols]
                dgate_ref[rows, cols] = dyr * (on * gain_h) * (sg * (1.0 + g * (1.0 - sg)))
                dy = dyr * (g * sg)
                dgain_ref[:, cols] += jnp.sum(dy * on, axis=0, keepdims=True)
                don = dy * gain_h
                do = rstd * (don - jnp.mean(don, axis=-1, keepdims=True) - on * jnp.mean(don * on, axis=-1, keepdims=True))
                dob = do.astype(BF16)
                ds = (_dot_nt(dob, vh) * dech).astype(BF16)
                gh = dstate[hd]
                gb = gh.astype(BF16)
                vz = (vh.astype(F32) * zetah).astype(BF16)
                dq_ref[rows, cols] = _dot(ds, kh) + _dot_nt(dob, rpb) * xih
                dk_ref[rows, cols] = _dot_tn(ds, qh) + _dot_nt(vz, gb)
                dv_ref[rows, cols] = _dot_tn(sdb, dob) + _dot(kh, gb) * zetah
                dstate[hd] = _dot_tn(qx, dob) + gch_ref[hd] * gh

    cst = _full((RET_HEADS, RET_CHUNK, RET_CHUNK))
    rev = _rows_rev(t, RET_WIDTH, nt)
    act = jax.ShapeDtypeStruct((seq, RET_WIDTH), F32)
    return pl.pallas_call(
        body, name="ret_bwd", grid=(nt,),
        in_specs=[rev] * 4 + [_full((1, RET_WIDTH)), cst, cst, cst, cst,
                              pl.BlockSpec((cps, RET_HEADS, HEAD_DIM, HEAD_DIM), lambda i: (nt - 1 - i, 0, 0, 0)), rev],
        out_specs=[rev] * 4 + [_full((1, RET_WIDTH))],
        out_shape=[act, act, act, act, jax.ShapeDtypeStruct((1, RET_WIDTH), F32)],
        scratch_shapes=[pltpu.VMEM((RET_HEADS, HEAD_DIM, HEAD_DIM), F32)],
        compiler_params=_params("arbitrary"),
    )(q, k, v, gate, gn_gain, dec, xi, zeta, gch, rprev, dy_ret)


def _s5_discretise(lam_re, lam_im, log_dt, b_re, b_im):
    lam = lax.complex(jnp.minimum(lam_re, -1e-4), lam_im)
    dt = jnp.exp(log_dt)[:, None]
    lam_bar = jnp.exp(lam * dt)
    b_bar = ((lam_bar - 1.0) / lam)[:, :, None] * lax.complex(b_re, b_im)
    return jnp.real(lam_bar), jnp.imag(lam_bar), jnp.real(b_bar), jnp.imag(b_bar)


def _block_diag(blocks):
    g, a, b = blocks.shape
    eye = jnp.eye(g, dtype=blocks.dtype)
    return (blocks[:, :, None, :] * eye[:, None, :, None]).reshape(g * a, g * b)


def _block_diag_extract(dense, a, b):
    g = SSM_GROUPS
    d4 = dense.reshape(g, a, g, b)
    eye = jnp.eye(g, dtype=dense.dtype)
    return jnp.sum(d4 * eye[:, None, :, None], axis=2)


def _s5_tables(lbr, lbi):
    lr = lbr.reshape(1, N_STATES)
    li = lbi.reshape(1, N_STATES)
    pr, pi = [lr], [li]
    for _ in range(S5_STEPS - 1):
        pr, pi = pr + [pr[-1] * lr - pi[-1] * li], pi + [pr[-1] * li + pi[-1] * lr]
    bc = lambda a: jnp.broadcast_to(a, (SUBLANES, N_STATES))
    rep = lambda rows: jnp.concatenate([bc(r) for r in rows], axis=0)
    return bc(lr), bc(li), rep(pr), rep(pi)


def _cmul(ar, ai, br, bi):
    return ar * br - ai * bi, ar * bi + ai * br


def _seg(i):
    return pl.ds(i, SUBLANES, stride=S5_STEPS)


def _s5_input_states(ub, b_ref, xs):
    for kk in range(2 * N_STATES // 256):
        res = _dot(ub, b_ref[:, kk * 256:(kk + 1) * 256])
        xs[2 * kk] = res[:, :LANES]
        xs[2 * kk + 1] = res[:, LANES:]


def _s5_scan_tile(xs, lr_ref, li_ref, pr_ref, pi_ref, carry_re, carry_im, conj, reverse):
    sgn = -1.0 if conj else 1.0
    seg_in_re, seg_in_im = [], []
    for grp in range(S5_COLBLK // S5_GROUP):
        blks = [grp * S5_GROUP + j for j in range(S5_GROUP)]
        lrs = [lr_ref[:, b * LANES:(b + 1) * LANES] for b in blks]
        lis = [sgn * li_ref[:, b * LANES:(b + 1) * LANES] for b in blks]

        def step(it, carry, blks=blks, lrs=lrs, lis=lis):
            i = (S5_STEPS - 1 - it) if reverse else it
            out_r, out_i = [], []
            for j, b in enumerate(blks):
                xr, xi = _cmul(lrs[j], lis[j], carry[j], carry[S5_GROUP + j])
                xr = xr + xs[b, _seg(i), :]
                xi = xi + xs[S5_COLBLK + b, _seg(i), :]
                xs[b, _seg(i), :] = xr
                xs[S5_COLBLK + b, _seg(i), :] = xi
                out_r.append(xr)
                out_i.append(xi)
            return tuple(out_r + out_i)

        zeros = tuple(jnp.zeros((SUBLANES, LANES), F32) for _ in range(2 * S5_GROUP))
        ends = lax.fori_loop(0, S5_STEPS, step, zeros, unroll=4)

        ins_r, ins_i = [], []
        for j, b in enumerate(blks):
            cols = slice(b * LANES, (b + 1) * LANES)
            pr = pr_ref[SUBLANES * S5_STEPS - 1:SUBLANES * S5_STEPS, cols]
            pi = sgn * pi_ref[SUBLANES * S5_STEPS - 1:SUBLANES * S5_STEPS, cols]
            cur_r = carry_re[:, cols]
            cur_i = carry_im[:, cols]
            row_id = lax.broadcasted_iota(jnp.int32, (SUBLANES, LANES), 0)
            in_r = jnp.zeros((SUBLANES, LANES), F32)
            in_i = jnp.zeros((SUBLANES, LANES), F32)
            order = range(SUBLANES - 1, -1, -1) if reverse else range(SUBLANES)
            for sgm in order:
                in_r = jnp.where(row_id == sgm, cur_r, in_r)
                in_i = jnp.where(row_id == sgm, cur_i, in_i)
                mr, mi = _cmul(pr, pi, cur_r, cur_i)
                cur_r = mr + ends[j][sgm:sgm + 1, :]
                cur_i = mi + ends[S5_GROUP + j][sgm:sgm + 1, :]
            carry_re[:, cols] = cur_r
            carry_im[:, cols] = cur_i
            ins_r.append(in_r)
            ins_i.append(in_i)

        def fix(it, c, blks=blks, ins_r=ins_r, ins_i=ins_i):
            pw = (S5_STEPS - 1 - it) if reverse else it
            for j, b in enumerate(blks):
                cols = slice(b * LANES, (b + 1) * LANES)
                prow = pl.ds(pl.multiple_of(pw * SUBLANES, SUBLANES), SUBLANES)
                fr, fi = _cmul(pr_ref[prow, cols], sgn * pi_ref[prow, cols], ins_r[j], ins_i[j])
                xs[b, _seg(it), :] = xs[b, _seg(it), :] + fr
                xs[S5_COLBLK + b, _seg(it), :] = xs[S5_COLBLK + b, _seg(it), :] + fi
            return c

        lax.fori_loop(0, S5_STEPS, fix, 0, unroll=4)
        seg_in_re += ins_r
        seg_in_im += ins_i
    return seg_in_re, seg_in_im


def _s5_pack(xs, dst):
    for b in range(2 * S5_COLBLK):
        dst[:, b * LANES:(b + 1) * LANES] = xs[b].astype(BF16)


def _s5_fwd(u, b_blk, c_blk, tables, d_skip):
    seq = u.shape[0]
    t = S5_TILE
    nt = seq // t
    lr8, li8, p_re, p_im = tables

    def body(u_ref, b_ref, c_ref, lr_ref, li_ref, pr_ref, pi_ref, d_ref, y_ref, start_ref, xs, xb, carry_re, carry_im):
        @pl.when(pl.program_id(0) == 0)
        def _():
            carry_re[...] = jnp.zeros_like(carry_re)
            carry_im[...] = jnp.zeros_like(carry_im)

        start_ref[0, 0:1, :] = carry_re[...]
        start_ref[0, 1:2, :] = carry_im[...]
        uv = u_ref[...]
        _s5_input_states(uv.astype(BF16), b_ref, xs)
        _s5_scan_tile(xs, lr_ref, li_ref, pr_ref, pi_ref, carry_re, carry_im, conj=False, reverse=False)
        _s5_pack(xs, xb)
        y_ref[...] = _dot(xb[...], c_ref[...]) + d_ref[...] * uv

    return pl.pallas_call(
        body, name="s5_fwd", grid=(nt,),
        in_specs=[_rows(t, SSM_WIDTH), _full((SSM_WIDTH, 2 * N_STATES)), _full((2 * N_STATES, SSM_WIDTH)),
                  _full((SUBLANES, N_STATES)), _full((SUBLANES, N_STATES)), _full((SUBLANES * S5_STEPS, N_STATES)),
                  _full((SUBLANES * S5_STEPS, N_STATES)), _full((1, SSM_WIDTH))],
        out_specs=[_rows(t, SSM_WIDTH), pl.BlockSpec((1, 2, N_STATES), lambda i: (i, 0, 0))],
        out_shape=[jax.ShapeDtypeStruct((seq, SSM_WIDTH), F32), jax.ShapeDtypeStruct((nt, 2, N_STATES), F32)],
        scratch_shapes=[pltpu.VMEM((2 * S5_COLBLK, t, LANES), F32), pltpu.VMEM((t, 2 * N_STATES), BF16),
                        pltpu.VMEM((1, N_STATES), F32), pltpu.VMEM((1, N_STATES), F32)],
        compiler_params=_params("arbitrary"),
    )(u, b_blk, c_blk, lr8, li8, p_re, p_im, d_skip)


def _s5_bwd(u, dy, b_blk, c_blk, tables, d_skip, starts):
    seq = u.shape[0]
    t = S5_TILE
    nt = seq // t
    lr8, li8, p_re, p_im = tables

    def body(u_ref, dy_ref, b_ref, c_ref, lr_ref, li_ref, pr_ref, pi_ref, d_ref, start_ref,
             du_ref, dd_ref, dlam_ref, db_ref, dc_ref,
             xs, as_, xb, ab, carry_re, carry_im, acar_re, acar_im):
        @pl.when(pl.program_id(0) == 0)
        def _():
            acar_re[...] = jnp.zeros_like(acar_re)
            acar_im[...] = jnp.zeros_like(acar_im)
            dd_ref[...] = jnp.zeros_like(dd_ref)
            dlam_ref[...] = jnp.zeros_like(dlam_ref)
            db_ref[...] = jnp.zeros_like(db_ref)
            dc_ref[...] = jnp.zeros_like(dc_ref)

        uv = u_ref[...]
        ub = uv.astype(BF16)
        dyv = dy_ref[...]
        dyb = dyv.astype(BF16)
        carry_re[...] = start_ref[0, 0:1, :]
        carry_im[...] = start_ref[0, 1:2, :]
        _s5_input_states(ub, b_ref, xs)
        in_re, in_im = _s5_scan_tile(xs, lr_ref, li_ref, pr_ref, pi_ref, carry_re, carry_im, conj=False, reverse=False)
        for kk in range(2 * N_STATES // 256):
            res = _dot_nt(dyb, c_ref[kk * 256:(kk + 1) * 256, :])
            as_[2 * kk] = res[:, :LANES]
            as_[2 * kk + 1] = res[:, LANES:]
        _s5_scan_tile(as_, lr_ref, li_ref, pr_ref, pi_ref, acar_re, acar_im, conj=True, reverse=True)
        for b in range(S5_COLBLK):
            def acc(i, c, b=b):
                xpr, xpi, sr, si = c
                ar = as_[b, _seg(i), :]
                ai = as_[S5_COLBLK + b, _seg(i), :]
                sr = sr + ar * xpr + ai * xpi
                si = si + ai * xpr - ar * xpi
                return xs[b, _seg(i), :], xs[S5_COLBLK + b, _seg(i), :], sr, si

            z = jnp.zeros((SUBLANES, LANES), F32)
            _, _, sr, si = lax.fori_loop(0, S5_STEPS, acc, (in_re[b], in_im[b], z, z), unroll=4)
            cols = slice(b * LANES, (b + 1) * LANES)
            dlam_ref[0, :, cols] += sr
            dlam_ref[1, :, cols] += si
        _s5_pack(xs, xb)
        _s5_pack(as_, ab)
        abv = ab[...]
        du_ref[...] = _dot_nt(abv, b_ref[...]) + d_ref[...] * dyv
        dd_ref[...] += jnp.sum(dyv * uv, axis=0, keepdims=True)
        db_ref[...] += _dot_tn(ub, abv)
        dc_ref[...] += _dot_tn(xb[...], dyb)

    rev = _rows_rev(t, SSM_WIDTH, nt)
    vec = lambda: pltpu.VMEM((1, N_STATES), F32)
    return pl.pallas_call(
        body, name="s5_bwd", grid=(nt,),
        in_specs=[rev, rev, _full((SSM_WIDTH, 2 * N_STATES)), _full((2 * N_STATES, SSM_WIDTH)),
                  _full((SUBLANES, N_STATES)), _full((SUBLANES, N_STATES)), _full((SUBLANES * S5_STEPS, N_STATES)),
                  _full((SUBLANES * S5_STEPS, N_STATES)), _full((1, SSM_WIDTH)),
                  pl.BlockSpec((1, 2, N_STATES), lambda i: (nt - 1 - i, 0, 0))],
        out_specs=[rev, _full((1, SSM_WIDTH)), _full((2, SUBLANES, N_STATES)), _full((SSM_WIDTH, 2 * N_STATES)),
                   _full((2 * N_STATES, SSM_WIDTH))],
        out_shape=[jax.ShapeDtypeStruct((seq, SSM_WIDTH), F32), jax.ShapeDtypeStruct((1, SSM_WIDTH), F32),
                   jax.ShapeDtypeStruct((2, SUBLANES, N_STATES), F32), jax.ShapeDtypeStruct((SSM_WIDTH, 2 * N_STATES), F32),
                   jax.ShapeDtypeStruct((2 * N_STATES, SSM_WIDTH), F32)],
        scratch_shapes=[pltpu.VMEM((2 * S5_COLBLK, t, LANES), F32), pltpu.VMEM((2 * S5_COLBLK, t, LANES), F32),
                        pltpu.VMEM((t, 2 * N_STATES), BF16), pltpu.VMEM((t, 2 * N_STATES), BF16),
                        vec(), vec(), vec(), vec()],
        compiler_params=_params("arbitrary"),
    )(u, dy, b_blk, c_blk, lr8, li8, p_re, p_im, d_skip, starts)


def _mix_fwd(y_ssm, y_ret, x, gain, w_glu, w_out):
    seq = x.shape[0]
    t = TOK_TILE

    def body(ys_ref, yr_ref, x_ref, g_ref, wg_ref, wo_ref, x1_ref, mix_ref):
        g0 = _gelu(ys_ref[...]).astype(BF16)
        z = _dot(g0, wg_ref[...])
        glu = (z[:, :SSM_WIDTH] * _sigmoid(z[:, SSM_WIDTH:])).astype(BF16)
        mix = _dot(yr_ref[...], wo_ref[:RET_WIDTH, :]) + _dot(glu, wo_ref[RET_WIDTH:, :])
        mix_ref[...] = mix
        n, _ = _rms(mix)
        x1_ref[...] = x_ref[...] + n * g_ref[...]

    act = jax.ShapeDtypeStruct((seq, D_MODEL), F32)
    return pl.pallas_call(
        body, name="mix_fwd", grid=(seq // t,),
        in_specs=[_rows(t, SSM_WIDTH), _rows(t, RET_WIDTH), _rows(t, D_MODEL), _full((1, D_MODEL)),
                  _full((SSM_WIDTH, 2 * SSM_WIDTH)), _full((D_MODEL, D_MODEL))],
        out_specs=[_rows(t, D_MODEL)] * 2, out_shape=[act, act],
        compiler_params=_params("parallel"),
    )(y_ssm, y_ret, x, gain, w_glu, w_out)


def _mix_bwd(dx1, mix, gain, y_ssm, y_ret, w_glu, w_out):
    seq = dx1.shape[0]
    t = TOK_TILE

    def body(dx1_ref, mix_ref, g_ref, ys_ref, yr_ref, wg_ref, wo_ref, dyr_ref, dys_ref, dwo_ref, dwg_ref, dg_ref):
        @pl.when(pl.program_id(0) == 0)
        def _():
            dwo_ref[...] = jnp.zeros_like(dwo_ref)
            dwg_ref[...] = jnp.zeros_like(dwg_ref)
            dg_ref[...] = jnp.zeros_like(dg_ref)

        n, r = _rms(mix_ref[...])
        dmix, dgs = _rms_bwd(n, r, g_ref[...], dx1_ref[...])
        dg_ref[...] += dgs
        dmb = dmix.astype(BF16)
        dcat = _dot_nt(dmb, wo_ref[...])
        dyr_ref[...] = dcat[:, :RET_WIDTH]
        dglu = dcat[:, RET_WIDTH:]
        ys = ys_ref[...]
        g0 = _gelu(ys).astype(BF16)
        z = _dot(g0, wg_ref[...])
        a = z[:, :SSM_WIDTH]
        sb = _sigmoid(z[:, SSM_WIDTH:])
        dwo_ref[:RET_WIDTH, :] += _dot_tn(yr_ref[...], dmb)
        dwo_ref[RET_WIDTH:, :] += _dot_tn((a * sb).astype(BF16), dmb)
        dz = jnp.concatenate([dglu * sb, dglu * a * sb * (1.0 - sb)], axis=1).astype(BF16)
        dwg_ref[...] += _dot_tn(g0, dz)
        dys_ref[...] = _dot_nt(dz, wg_ref[...]) * _gelu_grad(ys)

    half = jax.ShapeDtypeStruct((seq, RET_WIDTH), F32)
    return pl.pallas_call(
        body, name="mix_bwd", grid=(seq // t,),
        in_specs=[_rows(t, D_MODEL), _rows(t, D_MODEL), _full((1, D_MODEL)), _rows(t, SSM_WIDTH), _rows(t, RET_WIDTH),
                  _full((SSM_WIDTH, 2 * SSM_WIDTH)), _full((D_MODEL, D_MODEL))],
        out_specs=[_rows(t, RET_WIDTH), _rows(t, SSM_WIDTH), _full((D_MODEL, D_MODEL)), _full((SSM_WIDTH, 2 * SSM_WIDTH)),
                   _full((1, D_MODEL))],
        out_shape=[half, half, jax.ShapeDtypeStruct((D_MODEL, D_MODEL), F32),
                   jax.ShapeDtypeStruct((SSM_WIDTH, 2 * SSM_WIDTH), F32), jax.ShapeDtypeStruct((1, D_MODEL), F32)],
        compiler_params=_params("arbitrary"),
    )(dx1, mix, gain, y_ssm, y_ret, w_glu, w_out)


def _mlp_a(x1, target, gain_pre, gain_post, w1, w2):
    seq = x1.shape[0]
    t = MLP_TILE

    def body(x1_ref, tg_ref, gp_ref, gq_ref, w1_ref, w2_ref, df_ref, dx2_ref, dw2_ref, dgq_ref, sq_ref):
        @pl.when(pl.program_id(0) == 0)
        def _():
            dw2_ref[...] = jnp.zeros_like(dw2_ref)
            dgq_ref[...] = jnp.zeros_like(dgq_ref)
            sq_ref[...] = jnp.zeros_like(sq_ref)

        x1v = x1_ref[...]
        n3, _ = _rms(x1v)
        h = (n3 * gp_ref[...]).astype(BF16)
        rl = jnp.maximum(_dot(h, w1_ref[...]), 0.0)
        act = (rl * rl).astype(BF16)
        n4, r4 = _rms(_dot(act, w2_ref[...]))
        gq = gq_ref[...]
        err = x1v + n4 * gq - tg_ref[...]
        sq_ref[...] += jnp.sum(err * err, axis=0, keepdims=True)
        dx2 = err * (1.0 / D_MODEL)
        dx2_ref[...] = dx2
        dm, dgs = _rms_bwd(n4, r4, gq, dx2)
        dgq_ref[...] += dgs
        dmb = dm.astype(BF16)
        dw2_ref[...] += _dot_tn(act, dmb)
        df_ref[...] = (_dot_nt(dmb, w2_ref[...]) * (2.0 * rl)).astype(BF16)

    return pl.pallas_call(
        body, name="mlp_a", grid=(seq // t,),
        in_specs=[_rows(t, D_MODEL), _rows(t, D_MODEL), _full((1, D_MODEL)), _full((1, D_MODEL)),
                  _full((D_MODEL, D_FF)), _full((D_FF, D_MODEL))],
        out_specs=[_rows(t, D_FF), _rows(t, D_MODEL), _full((D_FF, D_MODEL)), _full((1, D_MODEL)), _full((1, D_MODEL))],
        out_shape=[jax.ShapeDtypeStruct((seq, D_FF), BF16), jax.ShapeDtypeStruct((seq, D_MODEL), F32),
                   jax.ShapeDtypeStruct((D_FF, D_MODEL), F32), jax.ShapeDtypeStruct((1, D_MODEL), F32),
                   jax.ShapeDtypeStruct((1, D_MODEL), F32)],
        compiler_params=_params("arbitrary"),
    )(x1, target, gain_pre, gain_post, w1, w2)


def _mlp_b(df, dx2, x1, gain_pre, w1):
    seq = x1.shape[0]
    t = MLP_TILE

    def body(df_ref, dx2_ref, x1_ref, gp_ref, w1_ref, dx1_ref, dw1_ref, dgp_ref):
        @pl.when(pl.program_id(0) == 0)
        def _():
            dw1_ref[...] = jnp.zeros_like(dw1_ref)
            dgp_ref[...] = jnp.zeros_like(dgp_ref)

        n3, r3 = _rms(x1_ref[...])
        gp = gp_ref[...]
        h = (n3 * gp).astype(BF16)
        dfv = df_ref[...]
        dw1_ref[...] += _dot_tn(h, dfv)
        dx, dgs = _rms_bwd(n3, r3, gp, _dot_nt(dfv, w1_ref[...]))
        dgp_ref[...] += dgs
        dx1_ref[...] = dx2_ref[...] + dx

    return pl.pallas_call(
        body, name="mlp_b", grid=(seq // t,),
        in_specs=[_rows(t, D_FF), _rows(t, D_MODEL), _rows(t, D_MODEL), _full((1, D_MODEL)), _full((D_MODEL, D_FF))],
        out_specs=[_rows(t, D_MODEL), _full((D_MODEL, D_FF)), _full((1, D_MODEL))],
        out_shape=[jax.ShapeDtypeStruct((seq, D_MODEL), F32), jax.ShapeDtypeStruct((D_MODEL, D_FF), F32),
                   jax.ShapeDtypeStruct((1, D_MODEL), F32)],
        compiler_params=_params("arbitrary"),
    )(df, dx2, x1, gain_pre, w1)


def _local_step(x, target, small, w_in, w_glu, w_out, w_ff1, w_ff2):
    seq = x.shape[0]
    half = HEAD_DIM // 2
    inv_freq = ROPE_BASE ** (-jnp.arange(half, dtype=F32) / half)
    ang = jnp.arange(seq, dtype=F32)[:, None] * inv_freq[None, :]
    cos, sin = jnp.cos(ang), jnp.sin(ang)
    cos2 = jnp.concatenate([cos, cos], axis=1)
    sin2 = jnp.concatenate([-sin, sin], axis=1)
    ret_consts = _ret_consts()

    def discretise(lam_re, lam_im, log_dt, b_re, b_im):
        return _s5_discretise(lam_re, lam_im, log_dt, b_re, b_im)

    (lbr, lbi, bbr, bbi), disc_vjp = jax.vjp(discretise, small["ssm_lambda_re"], small["ssm_lambda_im"], small["ssm_log_dt"],
                                              small["ssm_b_re"], small["ssm_b_im"])
    tables = _s5_tables(lbr, lbi)
    b_blk = jnp.concatenate([_block_diag(jnp.swapaxes(bbr, 1, 2)), _block_diag(jnp.swapaxes(bbi, 1, 2))], axis=1).astype(BF16)
    c_blk = jnp.concatenate([_block_diag(jnp.swapaxes(small["ssm_c_re"], 1, 2)),
                             -_block_diag(jnp.swapaxes(small["ssm_c_im"], 1, 2))], axis=0).astype(BF16)
    d_skip = small["ssm_d"].reshape(1, SSM_WIDTH)
    gn_gain = small["ret_gn_gain"].reshape(1, RET_WIDTH)
    g_mix_pre = small["norm_mix_pre"].reshape(1, D_MODEL)
    g_mix_post = small["norm_mix_post"].reshape(1, D_MODEL)
    g_mlp_pre = small["norm_mlp_pre"].reshape(1, D_MODEL)
    g_mlp_post = small["norm_mlp_post"].reshape(1, D_MODEL)

    q, k, v, gate, u = _inproj_fwd(x, g_mix_pre, w_in, cos2, sin2)
    y_ret, rprev = _ret_fwd(q, k, v, gate, gn_gain, ret_consts)
    y_ssm, starts = _s5_fwd(u, b_blk, c_blk, tables, d_skip)
    x1, mix = _mix_fwd(y_ssm, y_ret, x, g_mix_post, w_glu, w_out)
    df, dx2, d_ff2, dg_mlp_post, sq = _mlp_a(x1, target, g_mlp_pre, g_mlp_post, w_ff1, w_ff2)
    dx1, d_ff1, dg_mlp_pre = _mlp_b(df, dx2, x1, g_mlp_pre, w_ff1)
    dy_ret, dy_ssm, d_out, d_glu, dg_mix_post = _mix_bwd(dx1, mix, g_mix_post, y_ssm, y_ret, w_glu, w_out)
    du, dd, dlam8, db_dense, dc_dense = _s5_bwd(u, dy_ssm, b_blk, c_blk, tables, d_skip, starts)
    dq, dk, dv, dgate, dgn = _ret_bwd(q, k, v, gate, gn_gain, ret_consts, rprev, dy_ret)
    dx, d_in, dg_mix_pre = _inproj_bwd(x, g_mix_pre, w_in, cos2, sin2, dq, dk, dv, dgate, du, dx1)

    dlam = jnp.sum(dlam8, axis=1)
    dlbr = dlam[0].reshape(SSM_GROUPS, SSM_STATE)
    dlbi = dlam[1].reshape(SSM_GROUPS, SSM_STATE)
    dbbr = jnp.swapaxes(_block_diag_extract(db_dense[:, :N_STATES], SSM_GROUP, SSM_STATE), 1, 2)
    dbbi = jnp.swapaxes(_block_diag_extract(db_dense[:, N_STATES:], SSM_GROUP, SSM_STATE), 1, 2)
    d_lam_re, d_lam_im, d_log_dt, d_b_re, d_b_im = disc_vjp((dlbr, dlbi, dbbr, dbbi))
    d_c_re = jnp.swapaxes(_block_diag_extract(dc_dense[:N_STATES], SSM_STATE, SSM_GROUP), 1, 2)
    d_c_im = -jnp.swapaxes(_block_diag_extract(dc_dense[N_STATES:], SSM_STATE, SSM_GROUP), 1, 2)

    small_grads = {
        "norm_mix_pre": dg_mix_pre, "norm_mix_post": dg_mix_post, "ret_gn_gain": dgn,
        "ssm_lambda_re": d_lam_re, "ssm_lambda_im": d_lam_im, "ssm_log_dt": d_log_dt,
        "ssm_b_re": d_b_re, "ssm_b_im": d_b_im, "ssm_c_re": d_c_re, "ssm_c_im": d_c_im,
        "ssm_d": dd, "norm_mlp_pre": dg_mlp_pre, "norm_mlp_post": dg_mlp_post,
    }
    return sq, dx, (d_in, d_glu, d_out, d_ff1, d_ff2), small_grads


BIG = (
    ("w_in", D_MODEL, IN_COLS, 1),
    ("w_glu", SSM_WIDTH, 2 * SSM_WIDTH, 1),
    ("w_out", D_MODEL, D_MODEL, 0),
    ("w_ff1", D_MODEL, D_FF, 1),
    ("w_ff2", D_FF, D_MODEL, 0),
)
SMALL = (
    ("norm_mix_pre", (D_MODEL,)), ("norm_mix_post", (D_MODEL,)), ("ret_gn_gain", (RET_WIDTH,)),
    ("ssm_lambda_re", (SSM_GROUPS, SSM_STATE)), ("ssm_lambda_im", (SSM_GROUPS, SSM_STATE)), ("ssm_log_dt", (SSM_GROUPS,)),
    ("ssm_b_re", (SSM_GROUPS, SSM_STATE, SSM_GROUP)), ("ssm_b_im", (SSM_GROUPS, SSM_STATE, SSM_GROUP)),
    ("ssm_c_re", (SSM_GROUPS, SSM_GROUP, SSM_STATE)), ("ssm_c_im", (SSM_GROUPS, SSM_GROUP, SSM_STATE)),
    ("ssm_d", (SSM_WIDTH,)), ("norm_mlp_pre", (D_MODEL,)), ("norm_mlp_post", (D_MODEL,)),
)
SMALL_ROWS = 1152
EXCH_TILES = 8


def _pack_small(tree):
    rows = []
    for name, shape in SMALL:
        flat = tree[name].reshape(-1).astype(F32)
        n = -(-flat.shape[0] // LANES) * LANES
        rows.append(jnp.pad(flat, (0, n - flat.shape[0])).reshape(-1, LANES))
    used = sum(r.shape[0] for r in rows)
    rows.append(jnp.zeros((SMALL_ROWS - used, LANES), F32))
    return jnp.concatenate(rows, axis=0)


def _unpack_small(packed):
    out, row = {}, 0
    for name, shape in SMALL:
        size = math.prod(shape)
        n = -(-size // LANES)
        out[name] = packed[row:row + n].reshape(-1)[:size].reshape((1,) + shape)
        row += n
    return out


def _half_shape(r, c, axis):
    return (r // 2, c) if axis == 1 else (r, c // 2)


def _region_shape(r, c, axis):
    return (r // 2, c // N_CHIPS) if axis == 1 else (r // N_CHIPS, c // 2)


def _shard_shape(r, c, axis):
    return (r, c // N_CHIPS) if axis == 1 else (r // N_CHIPS, c)


def _ds(start, size):
    return pl.ds(pl.multiple_of(start * size, size), size)


def _region_of_full(ref, r, c, axis, shard, half):
    if axis == 1:
        return ref.at[_ds(half, r // 2), _ds(shard, c // N_CHIPS)]
    return ref.at[_ds(shard, r // N_CHIPS), _ds(half, c // 2)]


def _shard_of_full(ref, r, c, axis, shard):
    if axis == 1:
        return ref.at[:, _ds(shard, c // N_CHIPS)]
    return ref.at[_ds(shard, r // N_CHIPS), :]


def _half_of_full(ref, r, c, axis, half):
    if axis == 1:
        return ref.at[_ds(half, r // 2), :]
    return ref.at[:, _ds(half, c // 2)]


def _half_of_shard(ref, r, c, axis, half):
    if axis == 1:
        return ref.at[_ds(half, r // 2), :]
    return ref.at[:, _ds(half, c // 2)]


def _region_of_half(ref, r, c, axis, shard):
    if axis == 1:
        return ref.at[:, _ds(shard, c // N_CHIPS)]
    return ref.at[_ds(shard, r // N_CHIPS), :]


def _place():
    x, y, c = lax.axis_index("x"), lax.axis_index("y"), lax.axis_index("c")
    chips = [(1 - x, y), (x, 1 - y), (1 - x, 1 - y)]
    return x, y, c, chips


ANY = pl.BlockSpec(memory_space=pl.ANY)


def _remote(src, dst, send_sem, recv_sem, to):
    return pltpu.make_async_remote_copy(src_ref=src, dst_ref=dst, send_sem=send_sem, recv_sem=recv_sem,
                                        device_id=to, device_id_type=MESH)


def _gather_weights(shards):
    nw = len(BIG)

    def body(*refs):
        shard_refs, full_refs = refs[:nw], refs[nw:2 * nw]
        send_sems, recv_sems, pass_send, pass_recv, local_sems = refs[2 * nw:]
        x, y, c, chips = _place()
        mine = 2 * x + y
        local = []
        for w, (_, r, cc, axis) in enumerate(BIG):
            cp = pltpu.make_async_copy(shard_refs[w], _shard_of_full(full_refs[w], r, cc, axis, mine), local_sems.at[w])
            cp.start()
            local.append(cp)
        sent = []
        for w, (_, r, cc, axis) in enumerate(BIG):
            for k, (cx, cy) in enumerate(chips):
                cp = _remote(_half_of_shard(shard_refs[w], r, cc, axis, c), _region_of_full(full_refs[w], r, cc, axis, mine, c),
                             send_sems.at[3 * w + k], recv_sems.at[3 * w + k], (cx, cy, c))
                cp.start()
                sent.append(cp)
        passed = []
        for w, (_, r, cc, axis) in enumerate(BIG):
            for k, (cx, cy) in enumerate(chips):
                landed = _region_of_full(full_refs[w], r, cc, axis, 2 * cx + cy, c)
                _remote(landed, landed, send_sems.at[3 * w + k], recv_sems.at[3 * w + k], (cx, cy, c)).wait_recv()
                cp = _remote(landed, landed, pass_send.at[3 * w + k], pass_recv.at[3 * w + k], (x, y, 1 - c))
                cp.start()
                passed.append(cp)
        for w, (_, r, cc, axis) in enumerate(BIG):
            for k, (cx, cy) in enumerate(chips):
                other = _region_of_full(full_refs[w], r, cc, axis, 2 * cx + cy, 1 - c)
                _remote(other, other, pass_send.at[3 * w + k], pass_recv.at[3 * w + k], (x, y, 1 - c)).wait_recv()
        for cp in sent + passed:
            cp.wait_send()
        for cp in local:
            cp.wait()

    n = 3 * nw
    return pl.pallas_call(
        body, name="gather_weights",
        in_specs=[ANY] * nw, out_specs=[ANY] * nw,
        out_shape=[jax.ShapeDtypeStruct((r, cc), BF16) for _, r, cc, _ in BIG],
        scratch_shapes=[pltpu.SemaphoreType.DMA((n,)), pltpu.SemaphoreType.DMA((n,)), pltpu.SemaphoreType.DMA((n,)),
                        pltpu.SemaphoreType.DMA((n,)), pltpu.SemaphoreType.DMA((nw,))],
    )(*shards)


def _pair_exchange(grads, small):
    nw = len(BIG)

    def body(*refs):
        g_refs, got_refs = refs[:nw + 1], refs[nw + 1:2 * nw + 2]
        send_sems, recv_sems = refs[2 * nw + 2:]
        x, y, c, _ = _place()
        copies = []
        for w, (_, r, cc, axis) in enumerate(BIG):
            copies.append(_remote(_half_of_full(g_refs[w], r, cc, axis, 1 - c), got_refs[w], send_sems.at[w], recv_sems.at[w],
                                  (x, y, 1 - c)))
        copies.append(_remote(g_refs[nw].at[_ds(1 - c, SMALL_ROWS // 2), :], got_refs[nw], send_sems.at[nw], recv_sems.at[nw],
                              (x, y, 1 - c)))
        for cp in copies:
            cp.start()
        for cp in copies:
            cp.wait()

    return pl.pallas_call(
        body, name="pair_exchange",
        in_specs=[ANY] * (nw + 1), out_specs=[ANY] * (nw + 1),
        out_shape=[jax.ShapeDtypeStruct(_half_shape(r, cc, axis), F32) for _, r, cc, axis in BIG]
        + [jax.ShapeDtypeStruct((SMALL_ROWS // 2, LANES), F32)],
        scratch_shapes=[pltpu.SemaphoreType.DMA((nw + 1,)), pltpu.SemaphoreType.DMA((nw + 1,))],
    )(*grads, small)


def _tile_rows(shape):
    return (shape[0] // EXCH_TILES, shape[1])


def _pair_sum(ids, grads, small, got):
    nw = len(BIG)
    halves = [_half_shape(r, cc, axis) for _, r, cc, axis in BIG] + [(SMALL_ROWS // 2, LANES)]

    def body(ids_ref, *refs):
        n = nw + 1
        for j in range(n):
            total = refs[j][...] + refs[n + j][...]
            refs[2 * n + j][...] = total
            if j < nw:
                refs[3 * n + j][...] = total.astype(BF16)

    def mine_spec(j):
        tile = _tile_rows(halves[j])
        if j < nw and BIG[j][3] == 0:
            return pl.BlockSpec(tile, lambda i, ids: (i, ids[0]))
        return pl.BlockSpec(tile, lambda i, ids: (ids[0] * EXCH_TILES + i, 0))

    plain = lambda j: pl.BlockSpec(_tile_rows(halves[j]), lambda i, ids: (i, 0))
    n = nw + 1
    return pl.pallas_call(
        body, name="pair_sum",
        grid_spec=pltpu.PrefetchScalarGridSpec(
            num_scalar_prefetch=1, grid=(EXCH_TILES,),
            in_specs=[mine_spec(j) for j in range(n)] + [plain(j) for j in range(n)],
            out_specs=[plain(j) for j in range(n)] + [plain(j) for j in range(nw)]),
        out_shape=[jax.ShapeDtypeStruct(h, F32) for h in halves] + [jax.ShapeDtypeStruct(h, BF16) for h in halves[:nw]],
        compiler_params=pltpu.CompilerParams(dimension_semantics=("parallel",)),
    )(ids, *grads, small, *got)


def _chip_exchange(sums_bf16, small_sum):
    nw = len(BIG)

    def body(*refs):
        src_refs, got_refs = refs[:nw + 1], refs[nw + 1:2 * nw + 2]
        send_sems, recv_sems = refs[2 * nw + 2:]
        x, y, c, chips = _place()
        copies = []
        for w, (_, r, cc, axis) in enumerate(BIG):
            for k, (cx, cy) in enumerate(chips):
                copies.append(_remote(_region_of_half(src_refs[w], r, cc, axis, 2 * cx + cy), got_refs[w].at[k],
                                      send_sems.at[3 * w + k], recv_sems.at[3 * w + k], (cx, cy, c)))
        for k, (cx, cy) in enumerate(chips):
            copies.append(_remote(src_refs[nw], got_refs[nw].at[k], send_sems.at[3 * nw + k], recv_sems.at[3 * nw + k], (cx, cy, c)))
        for cp in copies:
            cp.start()
        for cp in copies:
            cp.wait()

    n = 3 * (nw + 1)
    return pl.pallas_call(
        body, name="chip_exchange",
        in_specs=[ANY] * (nw + 1), out_specs=[ANY] * (nw + 1),
        out_shape=[jax.ShapeDtypeStruct((3,) + _region_shape(r, cc, axis), BF16) for _, r, cc, axis in BIG]
        + [jax.ShapeDtypeStruct((3, SMALL_ROWS // 2, LANES), F32)],
        scratch_shapes=[pltpu.SemaphoreType.DMA((n,)), pltpu.SemaphoreType.DMA((n,))],
    )(*sums_bf16, small_sum)


def _chip_sum(ids, sums, got):
    nw = len(BIG)
    regions = [_region_shape(r, cc, axis) for _, r, cc, axis in BIG] + [(SMALL_ROWS // 2, LANES)]

    def body(ids_ref, *refs):
        n = nw + 1
        for j in range(n):
            own, arrived, out = refs[j], refs[n + j], refs[2 * n + j]
            if j < nw:
                out[...] = ((own[...] + arrived[0].astype(F32)) + arrived[1].astype(F32)) + arrived[2].astype(F32)
            else:
                out[...] = (own[...] + arrived[1]) + (arrived[0] + arrived[2])

    def own_spec(j):
        tile = _tile_rows(regions[j])
        if j == nw:
            return pl.BlockSpec(tile, lambda i, ids: (i, 0))
        if BIG[j][3] == 1:
            return pl.BlockSpec(tile, lambda i, ids: (i, ids[1]))
        return pl.BlockSpec(tile, lambda i, ids: (ids[1] * EXCH_TILES + i, 0))

    arrived_spec = lambda j: pl.BlockSpec((3,) + _tile_rows(regions[j]), lambda i, ids: (0, i, 0))
    plain = lambda j: pl.BlockSpec(_tile_rows(regions[j]), lambda i, ids: (i, 0))
    n = nw + 1
    return pl.pallas_call(
        body, name="chip_sum",
        grid_spec=pltpu.PrefetchScalarGridSpec(
            num_scalar_prefetch=1, grid=(EXCH_TILES,),
            in_specs=[own_spec(j) for j in range(n)] + [arrived_spec(j) for j in range(n)],
            out_specs=[plain(j) for j in range(n)]),
        out_shape=[jax.ShapeDtypeStruct(rg, F32) for rg in regions],
        compiler_params=pltpu.CompilerParams(dimension_semantics=("parallel",)),
    )(ids, *sums, *got)


def _pair_share(parts):
    nw = len(BIG)

    def body(*refs):
        part_refs, whole_refs = refs[:nw + 1], refs[nw + 1:2 * nw + 2]
        send_sems, recv_sems, local_sems = refs[2 * nw + 2:]
        x, y, c, _ = _place()

        def half_of_whole(w, half):
            if w == nw:
                return whole_refs[w].at[_ds(half, SMALL_ROWS // 2), :]
            _, r, cc, axis = BIG[w]
            return _half_of_shard(whole_refs[w], r, cc, axis, half)

        local, remote = [], []
        for w in range(nw + 1):
            lc = pltpu.make_async_copy(part_refs[w], half_of_whole(w, c), local_sems.at[w])
            lc.start()
            local.append(lc)
            rc = _remote(part_refs[w], half_of_whole(w, c), send_sems.at[w], recv_sems.at[w], (x, y, 1 - c))
            rc.start()
            remote.append(rc)
        for w in range(nw + 1):
            _remote(part_refs[w], half_of_whole(w, 1 - c), send_sems.at[w], recv_sems.at[w], (x, y, 1 - c)).wait_recv()
        for cp in remote:
            cp.wait_send()
        for cp in local:
            cp.wait()

    return pl.pallas_call(
        body, name="pair_share",
        in_specs=[ANY] * (nw + 1), out_specs=[ANY] * (nw + 1),
        out_shape=[jax.ShapeDtypeStruct(_shard_shape(r, cc, axis), F32) for _, r, cc, axis in BIG]
        + [jax.ShapeDtypeStruct((SMALL_ROWS, LANES), F32)],
        scratch_shapes=[pltpu.SemaphoreType.DMA((nw + 1,)), pltpu.SemaphoreType.DMA((nw + 1,)),
                        pltpu.SemaphoreType.DMA((nw + 1,))],
    )(*parts)


def _adamw(ws, gs, ms, vs):
    n = len(ws)
    c1 = 1.0 - ADAM_B1 ** ADAM_STEP
    c2 = 1.0 - ADAM_B2 ** ADAM_STEP

    def body(*refs):
        for j in range(n):
            w, g, m, v = (refs[k * n + j][...] for k in range(4))
            m = ADAM_B1 * m + (1.0 - ADAM_B1) * g
            v = ADAM_B2 * v + (1.0 - ADAM_B2) * (g * g)
            refs[4 * n + j][...] = -ADAM_LR * ((m / c1) / (jnp.sqrt(v / c2) + ADAM_EPS) + ADAM_WD * w)
            refs[5 * n + j][...] = m
            refs[6 * n + j][...] = v

    specs = [pl.BlockSpec(_tile_rows(w.shape), lambda i: (i, 0)) for w in ws]
    shapes = [jax.ShapeDtypeStruct(w.shape, F32) for w in ws]
    out = pl.pallas_call(
        body, name="adamw", grid=(EXCH_TILES,),
        in_specs=specs * 4, out_specs=specs * 3, out_shape=shapes * 3,
        compiler_params=pltpu.CompilerParams(dimension_semantics=("parallel",)),
    )(*ws, *gs, *ms, *vs)
    return out[:n], out[n:2 * n], out[2 * n:]


def kernel(x, norm_mix_pre, norm_mix_post, w_in, ret_gn_gain, ssm_lambda_re, ssm_lambda_im, ssm_log_dt, ssm_b_re, ssm_b_im, ssm_c_re, ssm_c_im, ssm_d, w_glu, w_out, norm_mlp_pre, norm_mlp_post, w_ff1, w_ff2, loss_target, m_norm_mix_pre, m_norm_mix_post, m_w_in, m_ret_gn_gain, m_ssm_lambda_re, m_ssm_lambda_im, m_ssm_log_dt, m_ssm_b_re, m_ssm_b_im, m_ssm_c_re, m_ssm_c_im, m_ssm_d, m_w_glu, m_w_out, m_norm_mlp_pre, m_norm_mlp_post, m_w_ff1, m_w_ff2, v_norm_mix_pre, v_norm_mix_post, v_w_in, v_ret_gn_gain, v_ssm_lambda_re, v_ssm_lambda_im, v_ssm_log_dt, v_ssm_b_re, v_ssm_b_im, v_ssm_c_re, v_ssm_c_im, v_ssm_d, v_w_glu, v_w_out, v_norm_mlp_pre, v_norm_mlp_post, v_w_ff1, v_w_ff2):
    given = dict(locals())
    order = ["norm_mix_pre", "norm_mix_post", "w_in", "ret_gn_gain", "ssm_lambda_re", "ssm_lambda_im", "ssm_log_dt",
             "ssm_b_re", "ssm_b_im", "ssm_c_re", "ssm_c_im", "ssm_d", "w_glu", "w_out", "norm_mlp_pre", "norm_mlp_post",
             "w_ff1", "w_ff2"]
    big_names = [name for name, _, _, _ in BIG]
    ids = jnp.stack([lax.axis_index("c"), 2 * lax.axis_index("x") + lax.axis_index("y")]).astype(jnp.int32)

    whole = _gather_weights([given[name][0].astype(BF16) for name in big_names])
    small_w = {name: given[name][0] for name, _ in SMALL}
    sq, dx, big_grads, small_grads = _local_step(x[0], loss_target[0], small_w, *whole)

    small_packed = _pack_small(small_grads)
    got = _pair_exchange(big_grads, small_packed)
    sums = _pair_sum(ids, big_grads, small_packed, got)
    nw = len(BIG)
    arrived = _chip_exchange(sums[nw + 1:], sums[nw])
    parts = _chip_sum(ids, sums[:nw + 1], arrived)
    reduced = _pair_share(parts)

    pack = lambda prefix: _pack_small({name: given[prefix + name][0] for name, _ in SMALL})
    ws = [given[name][0] for name in big_names] + [pack("")]
    ms = [given["m_" + name][0] for name in big_names] + [pack("m_")]
    vs = [given["v_" + name][0] for name in big_names] + [pack("v_")]
    deltas, new_m, new_v = _adamw(ws, list(reduced), ms, vs)

    def unpacked(arrays):
        tree = _unpack_small(arrays[nw])
        tree.update({name: arrays[j][None] for j, name in enumerate(big_names)})
        return [tree[name] for name in order]

    loss = lax.psum(0.5 * jnp.sum(sq) / D_MODEL, ("x", "y", "c"))
    return (loss, dx[None], *unpacked(list(reduced)), *unpacked(deltas), *unpacked(new_m), *unpacked(new_v))
```

```python
import functools
import math

import jax
import jax.numpy as jnp
from jax import lax
from jax.experimental import pallas as pl
from jax.experimental.pallas import tpu as pltpu

F32 = jnp.float32
BF16 = jnp.bfloat16

D_MODEL = 1024
RET_WIDTH = 512
RET_HEADS = 4
HEAD_DIM = 128
RET_CHUNK = 128
ROPE_BASE = 10000.0
SSM_WIDTH = 512
SSM_GROUP = 16
SSM_GROUPS = 32
SSM_STATE = 64
N_STATES = SSM_GROUPS * SSM_STATE
D_FF = 4096
IN_COLS = 4 * RET_WIDTH + SSM_WIDTH
NORM_EPS = 1e-6
K_SCALE = HEAD_DIM ** -0.5

ADAM_LR = 0.001
ADAM_B1 = 0.9
ADAM_B2 = 0.999
ADAM_EPS = 1e-08
ADAM_WD = 0.01
ADAM_STEP = 10

LANES = 128
SUBLANES = 8
VMEM_LIMIT = 56 * 2 ** 20

TOK_TILE = 512
MLP_TILE = 256
RET_CHUNKS_PER_STEP = 4
S5_TILE = 256
S5_STEPS = S5_TILE // SUBLANES
S5_COLBLK = N_STATES // LANES
S5_GROUP = 4

N_CHIPS = 4
MESH = pl.DeviceIdType.MESH


def _dot(a, b):
    return jnp.dot(a, b, preferred_element_type=F32)


def _dot_nt(a, b):
    return lax.dot_general(a, b, (((1,), (1,)), ((), ())), preferred_element_type=F32)


def _dot_tn(a, b):
    return lax.dot_general(a, b, (((0,), (0,)), ((), ())), preferred_element_type=F32)


def _sigmoid(x):
    return 1.0 / (1.0 + jnp.exp(-x))


_GELU_C = math.sqrt(2.0 / math.pi)


def _gelu(x):
    return 0.5 * x * (1.0 + jnp.tanh(_GELU_C * (x + 0.044715 * (x * x * x))))


def _gelu_grad(x):
    t = jnp.tanh(_GELU_C * (x + 0.044715 * (x * x * x)))
    return 0.5 * (1.0 + t) + 0.5 * x * (1.0 - t * t) * (_GELU_C * (1.0 + 3.0 * 0.044715 * (x * x)))


def _rms(x):
    r = lax.rsqrt(jnp.mean(x * x, axis=-1, keepdims=True) + NORM_EPS)
    return x * r, r


def _rms_bwd(n, r, gain, dy):
    dn = dy * gain
    dx = r * (dn - n * jnp.mean(dn * n, axis=-1, keepdims=True))
    return dx, jnp.sum(dy * n, axis=0, keepdims=True)


def _full(shape):
    nd = len(shape)
    return pl.BlockSpec(shape, lambda i, _n=nd: (0,) * _n, pipeline_mode=pl.Buffered(1))


def _rows(tile, width):
    return pl.BlockSpec((tile, width), lambda i: (i, 0))


def _rows_rev(tile, width, n):
    return pl.BlockSpec((tile, width), lambda i, _n=n: (_n - 1 - i, 0))


def _params(sem):
    return pltpu.CompilerParams(dimension_semantics=(sem,), vmem_limit_bytes=VMEM_LIMIT)


def _rope(t, cos2, sin2):
    return t * cos2 + pltpu.roll(t, HEAD_DIM // 2, 1) * sin2


def _rope_bwd(d, cos2, sin2):
    return d * cos2 + pltpu.roll(d * sin2, HEAD_DIM // 2, 1)


def _inproj_fwd(x, gain, w_in, cos2, sin2):
    seq = x.shape[0]

    def body(x_ref, g_ref, w_ref, c_ref, s_ref, q_ref, k_ref, v_ref, gate_ref, u_ref):
        n, _ = _rms(x_ref[...])
        h = (n * g_ref[...]).astype(BF16)
        proj = _dot(h, w_ref[...])
        c = c_ref[...]
        s = s_ref[...]
        for hd in range(RET_HEADS):
            lo = hd * HEAD_DIM
            q_ref[:, lo:lo + HEAD_DIM] = _rope(proj[:, lo:lo + HEAD_DIM], c, s).astype(BF16)
            kh = proj[:, RET_WIDTH + lo:RET_WIDTH + lo + HEAD_DIM]
            k_ref[:, lo:lo + HEAD_DIM] = (_rope(kh, c, s) * K_SCALE).astype(BF16)
        v_ref[...] = proj[:, 2 * RET_WIDTH:3 * RET_WIDTH].astype(BF16)
        gate_ref[...] = proj[:, 3 * RET_WIDTH:4 * RET_WIDTH]
        u_ref[...] = proj[:, 4 * RET_WIDTH:]

    t = TOK_TILE
    half = lambda dt: jax.ShapeDtypeStruct((seq, RET_WIDTH), dt)
    return pl.pallas_call(
        body, name="inproj_fwd", grid=(seq // t,),
        in_specs=[_rows(t, D_MODEL), _full((1, D_MODEL)), _full((D_MODEL, IN_COLS)), _rows(t, HEAD_DIM), _rows(t, HEAD_DIM)],
        out_specs=[_rows(t, RET_WIDTH)] * 5,
        out_shape=[half(BF16), half(BF16), half(BF16), half(F32), half(F32)],
        compiler_params=_params("parallel"),
    )(x, gain, w_in, cos2, sin2)


def _inproj_bwd(x, gain, w_in, cos2, sin2, dq, dk, dv, dgate, du, dres):
    seq = x.shape[0]
    t = TOK_TILE

    def body(x_ref, g_ref, w_ref, c_ref, s_ref, dq_ref, dk_ref, dv_ref, dgate_ref, du_ref, dres_ref,
             dx_ref, dw_ref, dg_ref, dproj):
        @pl.when(pl.program_id(0) == 0)
        def _():
            dw_ref[...] = jnp.zeros_like(dw_ref)
            dg_ref[...] = jnp.zeros_like(dg_ref)

        c = c_ref[...]
        s = s_ref[...]
        for hd in range(RET_HEADS):
            lo = hd * HEAD_DIM
            dproj[:, lo:lo + HEAD_DIM] = _rope_bwd(dq_ref[:, lo:lo + HEAD_DIM], c, s).astype(BF16)
            dproj[:, RET_WIDTH + lo:RET_WIDTH + lo + HEAD_DIM] = _rope_bwd(
                dk_ref[:, lo:lo + HEAD_DIM] * K_SCALE, c, s).astype(BF16)
        dproj[:, 2 * RET_WIDTH:3 * RET_WIDTH] = dv_ref[...].astype(BF16)
        dproj[:, 3 * RET_WIDTH:4 * RET_WIDTH] = dgate_ref[...].astype(BF16)
        dproj[:, 4 * RET_WIDTH:] = du_ref[...].astype(BF16)
        dp = dproj[...]
        n, r = _rms(x_ref[...])
        gain_v = g_ref[...]
        h = (n * gain_v).astype(BF16)
        dw_ref[...] += _dot_tn(h, dp)
        dh = _dot_nt(dp, w_ref[...])
        dx, dgs = _rms_bwd(n, r, gain_v, dh)
        dg_ref[...] += dgs
        dx_ref[...] = dres_ref[...] + dx

    return pl.pallas_call(
        body, name="inproj_bwd", grid=(seq // t,),
        in_specs=[_rows(t, D_MODEL), _full((1, D_MODEL)), _full((D_MODEL, IN_COLS)), _rows(t, HEAD_DIM), _rows(t, HEAD_DIM)]
        + [_rows(t, RET_WIDTH)] * 5 + [_rows(t, D_MODEL)],
        out_specs=[_rows(t, D_MODEL), _full((D_MODEL, IN_COLS)), _full((1, D_MODEL))],
        out_shape=[jax.ShapeDtypeStruct((seq, D_MODEL), F32), jax.ShapeDtypeStruct((D_MODEL, IN_COLS), F32),
                   jax.ShapeDtypeStruct((1, D_MODEL), F32)],
        scratch_shapes=[pltpu.VMEM((t, IN_COLS), BF16)],
        compiler_params=_params("arbitrary"),
    )(x, gain, w_in, cos2, sin2, dq, dk, dv, dgate, du, dres)


def _ret_consts():
    c = RET_CHUNK
    log_gamma = jnp.log(1.0 - jnp.exp(jnp.linspace(math.log(1.0 / 32), math.log(1.0 / 512), RET_HEADS))).astype(F32)
    idx = jnp.arange(c, dtype=F32)
    diff = idx[:, None] - idx[None, :]
    decay = jnp.where(diff[None] >= 0, jnp.exp(jnp.maximum(diff, 0.0)[None] * log_gamma[:, None, None]), 0.0)
    zeta = jnp.exp((c - 1 - idx)[None, :] * log_gamma[:, None])
    xi = jnp.exp((idx + 1.0)[None, :] * log_gamma[:, None])
    g_chunk = jnp.exp(c * log_gamma)
    wide = lambda rowvals: jnp.broadcast_to(rowvals[:, :, None], (RET_HEADS, c, c)).astype(F32)
    return decay.astype(F32), wide(xi), wide(zeta), jnp.broadcast_to(g_chunk[:, None, None], (RET_HEADS, c, c)).astype(F32)


def _ret_fwd(q, k, v, gate, gn_gain, consts):
    seq = q.shape[0]
    cps = RET_CHUNKS_PER_STEP
    t = cps * RET_CHUNK
    n_chunks = seq // RET_CHUNK
    dec, xi, zeta, gch = consts

    def body(q_ref, k_ref, v_ref, gate_ref, gain_ref, dec_ref, xi_ref, zeta_ref, gch_ref, y_ref, rprev_ref, state):
        @pl.when(pl.program_id(0) == 0)
        def _():
            state[...] = jnp.zeros_like(state)

        for cc in range(cps):
            r0 = cc * RET_CHUNK
            for hd in range(RET_HEADS):
                lo = hd * HEAD_DIM
                qh = q_ref[r0:r0 + RET_CHUNK, lo:lo + HEAD_DIM]
                kh = k_ref[r0:r0 + RET_CHUNK, lo:lo + HEAD_DIM]
                vh = v_ref[r0:r0 + RET_CHUNK, lo:lo + HEAD_DIM]
                s = _dot_nt(qh, kh) * dec_ref[hd]
                rp = state[hd]
                rpb = rp.astype(BF16)
                rprev_ref[cc, hd] = rpb
                qx = (qh.astype(F32) * xi_ref[hd]).astype(BF16)
                o = _dot(s.astype(BF16), vh) + _dot(qx, rpb)
                oc = o - jnp.mean(o, axis=-1, keepdims=True)
                on = oc * lax.rsqrt(jnp.mean(oc * oc, axis=-1, keepdims=True) + NORM_EPS)
                g = gate_ref[r0:r0 + RET_CHUNK, lo:lo + HEAD_DIM]
                y_ref[r0:r0 + RET_CHUNK, lo:lo + HEAD_DIM] = (g * _sigmoid(g) * (on * gain_ref[:, lo:lo + HEAD_DIM])).astype(BF16)
                vz = (vh.astype(F32) * zeta_ref[hd]).astype(BF16)
                state[hd] = gch_ref[hd] * rp + _dot_tn(kh, vz)

    cst = _full((RET_HEADS, RET_CHUNK, RET_CHUNK))
    return pl.pallas_call(
        body, name="ret_fwd", grid=(seq // t,),
        in_specs=[_rows(t, RET_WIDTH)] * 4 + [_full((1, RET_WIDTH)), cst, cst, cst, cst],
        out_specs=[_rows(t, RET_WIDTH), pl.BlockSpec((cps, RET_HEADS, HEAD_DIM, HEAD_DIM), lambda i: (i, 0, 0, 0))],
        out_shape=[jax.ShapeDtypeStruct((seq, RET_WIDTH), BF16),
                   jax.ShapeDtypeStruct((n_chunks, RET_HEADS, HEAD_DIM, HEAD_DIM), BF16)],
        scratch_shapes=[pltpu.VMEM((RET_HEADS, HEAD_DIM, HEAD_DIM), F32)],
        compiler_params=_params("arbitrary"),
    )(q, k, v, gate, gn_gain, dec, xi, zeta, gch)


def _ret_bwd(q, k, v, gate, gn_gain, consts, rprev, dy_ret):
    seq = q.shape[0]
    cps = RET_CHUNKS_PER_STEP
    t = cps * RET_CHUNK
    nt = seq // t
    dec, xi, zeta, gch = consts

    def body(q_ref, k_ref, v_ref, gate_ref, gain_ref, dec_ref, xi_ref, zeta_ref, gch_ref, rprev_ref, dyr_ref,
             dq_ref, dk_ref, dv_ref, dgate_ref, dgain_ref, dstate):
        @pl.when(pl.program_id(0) == 0)
        def _():
            dstate[...] = jnp.zeros_like(dstate)
            dgain_ref[...] = jnp.zeros_like(dgain_ref)

        for cc in reversed(range(cps)):
            rows = slice(cc * RET_CHUNK, (cc + 1) * RET_CHUNK)
            for hd in range(RET_HEADS):
                cols = slice(hd * HEAD_DIM, (hd + 1) * HEAD_DIM)
                qh = q_ref[rows, cols]
                kh = k_ref[rows, cols]
                vh = v_ref[rows, cols]
                dech = dec_ref[hd]
                xih = xi_ref[hd]
                zetah = zeta_ref[hd]
                rpb = rprev_ref[cc, hd]
                sdb = (_dot_nt(qh, kh) * dech).astype(BF16)
                qx = (qh.astype(F32) * xih).astype(BF16)
                o = _dot(sdb, vh) + _dot(qx, rpb)
                oc = o - jnp.mean(o, axis=-1, keepdims=True)
                rstd = lax.rsqrt(jnp.mean(oc * oc, axis=-1, keepdims=True) + NORM_EPS)
                on = oc * rstd
                g = gate_ref[rows, cols]
                sg = _sigmoid(g)
                gain_h = gain_ref[:, cols]
                dyr = dyr_ref[rows, cols]
                dgate_ref[rows, cols] = dyr * (on * gain_h) * (sg * (1.0 + g * (1.0 - sg)))
                dy = dyr * (g * sg)
                dgain_ref[:, cols] += jnp.sum(dy * on, axis=0, keepdims=True)
                don = dy * gain_h
                do = rstd * (don - jnp.mean(don, axis=-1, keepdims=True) - on * jnp.mean(don * on, axis=-1, keepdims=True))
                dob = do.astype(BF16)
                ds = (_dot_nt(dob, vh) * dech).astype(BF16)
                gh = dstate[hd]
                gb = gh.astype(BF16)
                vz = (vh.astype(F32) * zetah).astype(BF16)
                dq_ref[rows, cols] = _dot(ds, kh) + _dot_nt(dob, rpb) * xih
                dk_ref[rows, cols] = _dot_tn(ds, qh) + _dot_nt(vz, gb)
                dv_ref[rows, cols] = _dot_tn(sdb, dob) + _dot(kh, gb) * zetah
                dstate[hd] = _dot_tn(qx, dob) + gch_ref[hd] * gh

    cst = _full((RET_HEADS, RET_CHUNK, RET_CHUNK))
    rev = _rows_rev(t, RET_WIDTH, nt)
    act = jax.ShapeDtypeStruct((seq, RET_WIDTH), F32)
    return pl.pallas_call(
        body, name="ret_bwd", grid=(nt,),
        in_specs=[rev] * 4 + [_full((1, RET_WIDTH)), cst, cst, cst, cst,
                              pl.BlockSpec((cps, RET_HEADS, HEAD_DIM, HEAD_DIM), lambda i: (nt - 1 - i, 0, 0, 0)), rev],
        out_specs=[rev] * 4 + [_full((1, RET_WIDTH))],
        out_shape=[act, act, act, act, jax.ShapeDtypeStruct((1, RET_WIDTH), F32)],
        scratch_shapes=[pltpu.VMEM((RET_HEADS, HEAD_DIM, HEAD_DIM), F32)],
        compiler_params=_params("arbitrary"),
    )(q, k, v, gate, gn_gain, dec, xi, zeta, gch, rprev, dy_ret)


def _s5_discretise(lam_re, lam_im, log_dt, b_re, b_im):
    lam = lax.complex(jnp.minimum(lam_re, -1e-4), lam_im)
    dt = jnp.exp(log_dt)[:, None]
    lam_bar = jnp.exp(lam * dt)
    b_bar = ((lam_bar - 1.0) / lam)[:, :, None] * lax.complex(b_re, b_im)
    return jnp.real(lam_bar), jnp.imag(lam_bar), jnp.real(b_bar), jnp.imag(b_bar)


def _block_diag(blocks):
    g, a, b = blocks.shape
    eye = jnp.eye(g, dtype=blocks.dtype)
    return (blocks[:, :, None, :] * eye[:, None, :, None]).reshape(g * a, g * b)


def _outer_extract(acc):
    half = S5_STATE_TILES // 2
    a = acc.reshape(2, half // 2, 2, 2, S5_CHANNELS_PER_TILE, MXU_TILE)
    sel = jnp.stack([a[:, :, 0, 0], a[:, :, 1, 1]], axis=2)
    per = MXU_TILE // SSM_STATE
    blocks = sel.reshape(2, half, per, SSM_GROUP, per, SSM_STATE)
    eye = jnp.eye(per, dtype=acc.dtype)
    diag = jnp.sum(blocks * eye[None, None, :, None, :, None], axis=4)
    return diag.reshape(2, SSM_GROUPS, SSM_GROUP, SSM_STATE)


def _s5_tables(lbr, lbi):
    lr = lbr.reshape(1, N_STATES)
    li = lbi.reshape(1, N_STATES)
    pr, pi = [lr], [li]
    for _ in range(S5_STEPS - 1):
        pr, pi = pr + [pr[-1] * lr - pi[-1] * li], pi + [pr[-1] * li + pi[-1] * lr]
    bc = lambda a: jnp.broadcast_to(a, (SUBLANES, N_STATES))
    rep = lambda rows: jnp.concatenate([bc(r) for r in rows], axis=0)
    return bc(lr), bc(li), rep(pr), rep(pi)


def _cmul(ar, ai, br, bi):
    return ar * br - ai * bi, ar * bi + ai * br


def _seg(i):
    return pl.ds(pl.multiple_of(i * SUBLANES, SUBLANES), SUBLANES)


def _scan_order():
    r = jnp.arange(S5_TILE)
    token = (r % SUBLANES) * S5_STEPS + r // SUBLANES
    p = (token[:, None] == jnp.arange(S5_TILE)[None, :]).astype(BF16)
    return p, p.T


def _to_scan_order(p_ref, rows_bf16):
    return _dot(p_ref[...], rows_bf16).astype(BF16)


def _from_scan_order(pt_ref, rows):
    hi = rows.astype(BF16)
    rest = rows - hi.astype(F32)
    mid = rest.astype(BF16)
    lo = (rest - mid.astype(F32)).astype(BF16)
    pt = pt_ref[...]
    return _dot(pt, hi) + _dot(pt, mid) + _dot(pt, lo)


MXU_TILE = 256
S5_STATE_TILES = 2 * N_STATES // MXU_TILE
S5_CHANNELS_PER_TILE = MXU_TILE // SSM_STATE * SSM_GROUP


def _chan_tile(kk):
    return ((kk % (S5_STATE_TILES // 2)) * S5_CHANNELS_PER_TILE) // MXU_TILE


def _s5_input_states(ub, b_ref, xs):
    for kk in range(S5_STATE_TILES):
        ch = slice(_chan_tile(kk) * MXU_TILE, (_chan_tile(kk) + 1) * MXU_TILE)
        res = _dot(ub[:, ch], b_ref[ch, kk * MXU_TILE:(kk + 1) * MXU_TILE])
        xs[2 * kk] = res[:, :LANES]
        xs[2 * kk + 1] = res[:, LANES:]


def _s5_adjoint_inputs(dyb, c_ref, as_):
    for kk in range(S5_STATE_TILES):
        ch = slice(_chan_tile(kk) * MXU_TILE, (_chan_tile(kk) + 1) * MXU_TILE)
        res = _dot_nt(dyb[:, ch], c_ref[kk * MXU_TILE:(kk + 1) * MXU_TILE, ch])
        as_[2 * kk] = res[:, :LANES]
        as_[2 * kk + 1] = res[:, LANES:]


def _s5_states_to_channels(sb, m_ref, transposed):
    per = N_STATES // (SSM_WIDTH // MXU_TILE)
    out = []
    for n in range(SSM_WIDTH // MXU_TILE):
        ch = slice(n * MXU_TILE, (n + 1) * MXU_TILE)
        acc = None
        for part in range(2):
            st = slice(part * N_STATES + n * per, part * N_STATES + (n + 1) * per)
            term = _dot_nt(sb[:, st], m_ref[ch, st]) if transposed else _dot(sb[:, st], m_ref[st, ch])
            acc = term if acc is None else acc + term
        out.append(acc)
    return jnp.concatenate(out, axis=1)


def _s5_outer_acc(acc_ref, chan_b, states_ref):
    for kk in range(S5_STATE_TILES):
        lo = ((kk % (S5_STATE_TILES // 2)) * S5_CHANNELS_PER_TILE) // LANES * LANES
        acc_ref[kk] += _dot_tn(chan_b[:, lo:lo + LANES], states_ref[:, kk * MXU_TILE:(kk + 1) * MXU_TILE])


def _s5_scan_tile(xs, lr_ref, li_ref, pr_ref, pi_ref, carry_re, carry_im, conj, reverse):
    sgn = -1.0 if conj else 1.0
    seg_in_re, seg_in_im = [], []
    for grp in range(S5_COLBLK // S5_GROUP):
        blks = [grp * S5_GROUP + j for j in range(S5_GROUP)]
        lrs = [lr_ref[:, b * LANES:(b + 1) * LANES] for b in blks]
        lis = [sgn * li_ref[:, b * LANES:(b + 1) * LANES] for b in blks]

        def step(it, carry, blks=blks, lrs=lrs, lis=lis):
            i = (S5_STEPS - 1 - it) if reverse else it
            out_r, out_i = [], []
            for j, b in enumerate(blks):
                xr, xi = _cmul(lrs[j], lis[j], carry[j], carry[S5_GROUP + j])
                xr = xr + xs[b, _seg(i), :]
                xi = xi + xs[S5_COLBLK + b, _seg(i), :]
                xs[b, _seg(i), :] = xr
                xs[S5_COLBLK + b, _seg(i), :] = xi
                out_r.append(xr)
                out_i.append(xi)
            return tuple(out_r + out_i)

        zeros = tuple(jnp.zeros((SUBLANES, LANES), F32) for _ in range(2 * S5_GROUP))
        ends = lax.fori_loop(0, S5_STEPS, step, zeros, unroll=4)

        ins_r, ins_i = [], []
        for j, b in enumerate(blks):
            cols = slice(b * LANES, (b + 1) * LANES)
            pr = pr_ref[SUBLANES * S5_STEPS - 1:SUBLANES * S5_STEPS, cols]
            pi = sgn * pi_ref[SUBLANES * S5_STEPS - 1:SUBLANES * S5_STEPS, cols]
            cur_r = carry_re[:, cols]
            cur_i = carry_im[:, cols]
            row_id = lax.broadcasted_iota(jnp.int32, (SUBLANES, LANES), 0)
            in_r = jnp.zeros((SUBLANES, LANES), F32)
            in_i = jnp.zeros((SUBLANES, LANES), F32)
            order = range(SUBLANES - 1, -1, -1) if reverse else range(SUBLANES)
            for sgm in order:
                in_r = jnp.where(row_id == sgm, cur_r, in_r)
                in_i = jnp.where(row_id == sgm, cur_i, in_i)
                mr, mi = _cmul(pr, pi, cur_r, cur_i)
                cur_r = mr + ends[j][sgm:sgm + 1, :]
                cur_i = mi + ends[S5_GROUP + j][sgm:sgm + 1, :]
            carry_re[:, cols] = cur_r
            carry_im[:, cols] = cur_i
            ins_r.append(in_r)
            ins_i.append(in_i)

        def fix(it, c, blks=blks, ins_r=ins_r, ins_i=ins_i):
            pw = (S5_STEPS - 1 - it) if reverse else it
            for j, b in enumerate(blks):
                cols = slice(b * LANES, (b + 1) * LANES)
                prow = pl.ds(pl.multiple_of(pw * SUBLANES, SUBLANES), SUBLANES)
                fr, fi = _cmul(pr_ref[prow, cols], sgn * pi_ref[prow, cols], ins_r[j], ins_i[j])
                xs[b, _seg(it), :] = xs[b, _seg(it), :] + fr
                xs[S5_COLBLK + b, _seg(it), :] = xs[S5_COLBLK + b, _seg(it), :] + fi
            return c

        lax.fori_loop(0, S5_STEPS, fix, 0, unroll=4)
        seg_in_re += ins_r
        seg_in_im += ins_i
    return seg_in_re, seg_in_im


def _s5_pack(xs, dst):
    for b in range(2 * S5_COLBLK):
        dst[:, b * LANES:(b + 1) * LANES] = xs[b].astype(BF16)


def _s5_fwd(u, b_blk, c_blk, tables, d_skip):
    seq = u.shape[0]
    t = S5_TILE
    nt = seq // t
    lr8, li8, p_re, p_im = tables

    p, pt = _scan_order()

    def body(u_ref, p_ref, pt_ref, b_ref, c_ref, lr_ref, li_ref, pr_ref, pi_ref, d_ref, y_ref, start_ref,
             xs, xb, carry_re, carry_im):
        @pl.when(pl.program_id(0) == 0)
        def _():
            carry_re[...] = jnp.zeros_like(carry_re)
            carry_im[...] = jnp.zeros_like(carry_im)

        start_ref[0, 0:1, :] = carry_re[...]
        start_ref[0, 1:2, :] = carry_im[...]
        uv = u_ref[...]
        _s5_input_states(_to_scan_order(p_ref, uv.astype(BF16)), b_ref, xs)
        _s5_scan_tile(xs, lr_ref, li_ref, pr_ref, pi_ref, carry_re, carry_im, conj=False, reverse=False)
        _s5_pack(xs, xb)
        y_ref[...] = _from_scan_order(pt_ref, _s5_states_to_channels(xb, c_ref, False)) + d_ref[...] * uv

    return pl.pallas_call(
        body, name="s5_fwd", grid=(nt,),
        in_specs=[_rows(t, SSM_WIDTH), _full((t, t)), _full((t, t)),
                  _full((SSM_WIDTH, 2 * N_STATES)), _full((2 * N_STATES, SSM_WIDTH)),
                  _full((SUBLANES, N_STATES)), _full((SUBLANES, N_STATES)), _full((SUBLANES * S5_STEPS, N_STATES)),
                  _full((SUBLANES * S5_STEPS, N_STATES)), _full((1, SSM_WIDTH))],
        out_specs=[_rows(t, SSM_WIDTH), pl.BlockSpec((1, 2, N_STATES), lambda i: (i, 0, 0))],
        out_shape=[jax.ShapeDtypeStruct((seq, SSM_WIDTH), F32), jax.ShapeDtypeStruct((nt, 2, N_STATES), F32)],
        scratch_shapes=[pltpu.VMEM((2 * S5_COLBLK, t, LANES), F32), pltpu.VMEM((t, 2 * N_STATES), BF16),
                        pltpu.VMEM((1, N_STATES), F32), pltpu.VMEM((1, N_STATES), F32)],
        compiler_params=_params("arbitrary"),
    )(u, p, pt, b_blk, c_blk, lr8, li8, p_re, p_im, d_skip)


def _s5_bwd(u, dy, b_blk, c_blk, tables, d_skip, starts):
    seq = u.shape[0]
    t = S5_TILE
    nt = seq // t
    lr8, li8, p_re, p_im = tables
    p, pt = _scan_order()

    def body(u_ref, dy_ref, p_ref, pt_ref, b_ref, c_ref, lr_ref, li_ref, pr_ref, pi_ref, d_ref, start_ref,
             du_ref, dd_ref, dlam_ref, db_ref, dc_ref,
             xs, as_, xb, ab, carry_re, carry_im, acar_re, acar_im):
        @pl.when(pl.program_id(0) == 0)
        def _():
            acar_re[...] = jnp.zeros_like(acar_re)
            acar_im[...] = jnp.zeros_like(acar_im)
            dd_ref[...] = jnp.zeros_like(dd_ref)
            dlam_ref[...] = jnp.zeros_like(dlam_ref)
            db_ref[...] = jnp.zeros_like(db_ref)
            dc_ref[...] = jnp.zeros_like(dc_ref)

        uv = u_ref[...]
        dyv = dy_ref[...]
        ub = _to_scan_order(p_ref, uv.astype(BF16))
        dyb = _to_scan_order(p_ref, dyv.astype(BF16))
        carry_re[...] = start_ref[0, 0:1, :]
        carry_im[...] = start_ref[0, 1:2, :]
        _s5_input_states(ub, b_ref, xs)
        in_re, in_im = _s5_scan_tile(xs, lr_ref, li_ref, pr_ref, pi_ref, carry_re, carry_im, conj=False, reverse=False)
        _s5_adjoint_inputs(dyb, c_ref, as_)
        _s5_scan_tile(as_, lr_ref, li_ref, pr_ref, pi_ref, acar_re, acar_im, conj=True, reverse=True)
        for b in range(S5_COLBLK):
            def acc(i, c, b=b):
                xpr, xpi, sr, si = c
                ar = as_[b, _seg(i), :]
                ai = as_[S5_COLBLK + b, _seg(i), :]
                sr = sr + ar * xpr + ai * xpi
                si = si + ai * xpr - ar * xpi
                return xs[b, _seg(i), :], xs[S5_COLBLK + b, _seg(i), :], sr, si

            z = jnp.zeros((SUBLANES, LANES), F32)
            _, _, sr, si = lax.fori_loop(0, S5_STEPS, acc, (in_re[b], in_im[b], z, z), unroll=4)
            cols = slice(b * LANES, (b + 1) * LANES)
            dlam_ref[0, :, cols] += sr
            dlam_ref[1, :, cols] += si
        _s5_pack(xs, xb)
        _s5_pack(as_, ab)
        du_ref[...] = _from_scan_order(pt_ref, _s5_states_to_channels(ab, b_ref, True)) + d_ref[...] * dyv
        dd_ref[...] += jnp.sum(dyv * uv, axis=0, keepdims=True)
        _s5_outer_acc(db_ref, ub, ab)
        _s5_outer_acc(dc_ref, dyb, xb)

    rev = _rows_rev(t, SSM_WIDTH, nt)
    vec = lambda: pltpu.VMEM((1, N_STATES), F32)
    outer = (S5_STATE_TILES, LANES, MXU_TILE)
    return pl.pallas_call(
        body, name="s5_bwd", grid=(nt,),
        in_specs=[rev, rev, _full((t, t)), _full((t, t)), _full((SSM_WIDTH, 2 * N_STATES)), _full((2 * N_STATES, SSM_WIDTH)),
                  _full((SUBLANES, N_STATES)), _full((SUBLANES, N_STATES)), _full((SUBLANES * S5_STEPS, N_STATES)),
                  _full((SUBLANES * S5_STEPS, N_STATES)), _full((1, SSM_WIDTH)),
                  pl.BlockSpec((1, 2, N_STATES), lambda i: (nt - 1 - i, 0, 0))],
        out_specs=[rev, _full((1, SSM_WIDTH)), _full((2, SUBLANES, N_STATES)), _full(outer), _full(outer)],
        out_shape=[jax.ShapeDtypeStruct((seq, SSM_WIDTH), F32), jax.ShapeDtypeStruct((1, SSM_WIDTH), F32),
                   jax.ShapeDtypeStruct((2, SUBLANES, N_STATES), F32), jax.ShapeDtypeStruct(outer, F32),
                   jax.ShapeDtypeStruct(outer, F32)],
        scratch_shapes=[pltpu.VMEM((2 * S5_COLBLK, t, LANES), F32), pltpu.VMEM((2 * S5_COLBLK, t, LANES), F32),
                        pltpu.VMEM((t, 2 * N_STATES), BF16), pltpu.VMEM((t, 2 * N_STATES), BF16),
                        vec(), vec(), vec(), vec()],
        compiler_params=_params("arbitrary"),
    )(u, dy, p, pt, b_blk, c_blk, lr8, li8, p_re, p_im, d_skip, starts)


def _mix_fwd(y_ssm, y_ret, x, gain, w_glu, w_out):
    seq = x.shape[0]
    t = TOK_TILE

    def body(ys_ref, yr_ref, x_ref, g_ref, wg_ref, wo_ref, x1_ref, mix_ref):
        g0 = _gelu(ys_ref[...]).astype(BF16)
        z = _dot(g0, wg_ref[...])
        glu = (z[:, :SSM_WIDTH] * _sigmoid(z[:, SSM_WIDTH:])).astype(BF16)
        mix = _dot(yr_ref[...], wo_ref[:RET_WIDTH, :]) + _dot(glu, wo_ref[RET_WIDTH:, :])
        mix_ref[...] = mix
        n, _ = _rms(mix)
        x1_ref[...] = x_ref[...] + n * g_ref[...]

    act = jax.ShapeDtypeStruct((seq, D_MODEL), F32)
    return pl.pallas_call(
        body, name="mix_fwd", grid=(seq // t,),
        in_specs=[_rows(t, SSM_WIDTH), _rows(t, RET_WIDTH), _rows(t, D_MODEL), _full((1, D_MODEL)),
                  _full((SSM_WIDTH, 2 * SSM_WIDTH)), _full((D_MODEL, D_MODEL))],
        out_specs=[_rows(t, D_MODEL)] * 2, out_shape=[act, act],
        compiler_params=_params("parallel"),
    )(y_ssm, y_ret, x, gain, w_glu, w_out)


def _mix_bwd(dx1, mix, gain, y_ssm, y_ret, w_glu, w_out):
    seq = dx1.shape[0]
    t = TOK_TILE

    def body(dx1_ref, mix_ref, g_ref, ys_ref, yr_ref, wg_ref, wo_ref, dyr_ref, dys_ref, dwo_ref, dwg_ref, dg_ref):
        @pl.when(pl.program_id(0) == 0)
        def _():
            dwo_ref[...] = jnp.zeros_like(dwo_ref)
            dwg_ref[...] = jnp.zeros_like(dwg_ref)
            dg_ref[...] = jnp.zeros_like(dg_ref)

        n, r = _rms(mix_ref[...])
        dmix, dgs = _rms_bwd(n, r, g_ref[...], dx1_ref[...])
        dg_ref[...] += dgs
        dmb = dmix.astype(BF16)
        dcat = _dot_nt(dmb, wo_ref[...])
        dyr_ref[...] = dcat[:, :RET_WIDTH]
        dglu = dcat[:, RET_WIDTH:]
        ys = ys_ref[...]
        g0 = _gelu(ys).astype(BF16)
        z = _dot(g0, wg_ref[...])
        a = z[:, :SSM_WIDTH]
        sb = _sigmoid(z[:, SSM_WIDTH:])
        dwo_ref[:RET_WIDTH, :] += _dot_tn(yr_ref[...], dmb)
        dwo_ref[RET_WIDTH:, :] += _dot_tn((a * sb).astype(BF16), dmb)
        dz = jnp.concatenate([dglu * sb, dglu * a * sb * (1.0 - sb)], axis=1).astype(BF16)
        dwg_ref[...] += _dot_tn(g0, dz)
        dys_ref[...] = _dot_nt(dz, wg_ref[...]) * _gelu_grad(ys)

    half = jax.ShapeDtypeStruct((seq, RET_WIDTH), F32)
    return pl.pallas_call(
        body, name="mix_bwd", grid=(seq // t,),
        in_specs=[_rows(t, D_MODEL), _rows(t, D_MODEL), _full((1, D_MODEL)), _rows(t, SSM_WIDTH), _rows(t, RET_WIDTH),
                  _full((SSM_WIDTH, 2 * SSM_WIDTH)), _full((D_MODEL, D_MODEL))],
        out_specs=[_rows(t, RET_WIDTH), _rows(t, SSM_WIDTH), _full((D_MODEL, D_MODEL)), _full((SSM_WIDTH, 2 * SSM_WIDTH)),
                   _full((1, D_MODEL))],
        out_shape=[half, half, jax.ShapeDtypeStruct((D_MODEL, D_MODEL), F32),
                   jax.ShapeDtypeStruct((SSM_WIDTH, 2 * SSM_WIDTH), F32), jax.ShapeDtypeStruct((1, D_MODEL), F32)],
        compiler_params=_params("arbitrary"),
    )(dx1, mix, gain, y_ssm, y_ret, w_glu, w_out)


def _mlp_a(x1, target, gain_pre, gain_post, w1, w2):
    seq = x1.shape[0]
    t = MLP_TILE

    def body(x1_ref, tg_ref, gp_ref, gq_ref, w1_ref, w2_ref, df_ref, dx2_ref, dw2_ref, dgq_ref, sq_ref):
        @pl.when(pl.program_id(0) == 0)
        def _():
            dw2_ref[...] = jnp.zeros_like(dw2_ref)
            dgq_ref[...] = jnp.zeros_like(dgq_ref)
            sq_ref[...] = jnp.zeros_like(sq_ref)

        x1v = x1_ref[...]
        n3, _ = _rms(x1v)
        h = (n3 * gp_ref[...]).astype(BF16)
        rl = jnp.maximum(_dot(h, w1_ref[...]), 0.0)
        act = (rl * rl).astype(BF16)
        n4, r4 = _rms(_dot(act, w2_ref[...]))
        gq = gq_ref[...]
        err = x1v + n4 * gq - tg_ref[...]
        sq_ref[...] += jnp.sum(err * err, axis=0, keepdims=True)
        dx2 = err * (1.0 / D_MODEL)
        dx2_ref[...] = dx2
        dm, dgs = _rms_bwd(n4, r4, gq, dx2)
        dgq_ref[...] += dgs
        dmb = dm.astype(BF16)
        dw2_ref[...] += _dot_tn(act, dmb)
        df_ref[...] = (_dot_nt(dmb, w2_ref[...]) * (2.0 * rl)).astype(BF16)

    return pl.pallas_call(
        body, name="mlp_a", grid=(seq // t,),
        in_specs=[_rows(t, D_MODEL), _rows(t, D_MODEL), _full((1, D_MODEL)), _full((1, D_MODEL)),
                  _full((D_MODEL, D_FF)), _full((D_FF, D_MODEL))],
        out_specs=[_rows(t, D_FF), _rows(t, D_MODEL), _full((D_FF, D_MODEL)), _full((1, D_MODEL)), _full((1, D_MODEL))],
        out_shape=[jax.ShapeDtypeStruct((seq, D_FF), BF16), jax.ShapeDtypeStruct((seq, D_MODEL), F32),
                   jax.ShapeDtypeStruct((D_FF, D_MODEL), F32), jax.ShapeDtypeStruct((1, D_MODEL), F32),
                   jax.ShapeDtypeStruct((1, D_MODEL), F32)],
        compiler_params=_params("arbitrary"),
    )(x1, target, gain_pre, gain_post, w1, w2)


def _mlp_b(df, dx2, x1, gain_pre, w1):
    seq = x1.shape[0]
    t = MLP_TILE

    def body(df_ref, dx2_ref, x1_ref, gp_ref, w1_ref, dx1_ref, dw1_ref, dgp_ref):
        @pl.when(pl.program_id(0) == 0)
        def _():
            dw1_ref[...] = jnp.zeros_like(dw1_ref)
            dgp_ref[...] = jnp.zeros_like(dgp_ref)

        n3, r3 = _rms(x1_ref[...])
        gp = gp_ref[...]
        h = (n3 * gp).astype(BF16)
        dfv = df_ref[...]
        dw1_ref[...] += _dot_tn(h, dfv)
        dx, dgs = _rms_bwd(n3, r3, gp, _dot_nt(dfv, w1_ref[...]))
        dgp_ref[...] += dgs
        dx1_ref[...] = dx2_ref[...] + dx

    return pl.pallas_call(
        body, name="mlp_b", grid=(seq // t,),
        in_specs=[_rows(t, D_FF), _rows(t, D_MODEL), _rows(t, D_MODEL), _full((1, D_MODEL)), _full((D_MODEL, D_FF))],
        out_specs=[_rows(t, D_MODEL), _full((D_MODEL, D_FF)), _full((1, D_MODEL))],
        out_shape=[jax.ShapeDtypeStruct((seq, D_MODEL), F32), jax.ShapeDtypeStruct((D_MODEL, D_FF), F32),
                   jax.ShapeDtypeStruct((1, D_MODEL), F32)],
        compiler_params=_params("arbitrary"),
    )(df, dx2, x1, gain_pre, w1)


def _local_step(x, target, small, w_in, w_glu, w_out, w_ff1, w_ff2):
    seq = x.shape[0]
    half = HEAD_DIM // 2
    inv_freq = ROPE_BASE ** (-jnp.arange(half, dtype=F32) / half)
    ang = jnp.arange(seq, dtype=F32)[:, None] * inv_freq[None, :]
    cos, sin = jnp.cos(ang), jnp.sin(ang)
    cos2 = jnp.concatenate([cos, cos], axis=1)
    sin2 = jnp.concatenate([-sin, sin], axis=1)
    ret_consts = _ret_consts()

    def discretise(lam_re, lam_im, log_dt, b_re, b_im):
        return _s5_discretise(lam_re, lam_im, log_dt, b_re, b_im)

    (lbr, lbi, bbr, bbi), disc_vjp = jax.vjp(discretise, small["ssm_lambda_re"], small["ssm_lambda_im"], small["ssm_log_dt"],
                                              small["ssm_b_re"], small["ssm_b_im"])
    tables = _s5_tables(lbr, lbi)
    b_blk = jnp.concatenate([_block_diag(jnp.swapaxes(bbr, 1, 2)), _block_diag(jnp.swapaxes(bbi, 1, 2))], axis=1).astype(BF16)
    c_blk = jnp.concatenate([_block_diag(jnp.swapaxes(small["ssm_c_re"], 1, 2)),
                             -_block_diag(jnp.swapaxes(small["ssm_c_im"], 1, 2))], axis=0).astype(BF16)
    d_skip = small["ssm_d"].reshape(1, SSM_WIDTH)
    gn_gain = small["ret_gn_gain"].reshape(1, RET_WIDTH)
    g_mix_pre = small["norm_mix_pre"].reshape(1, D_MODEL)
    g_mix_post = small["norm_mix_post"].reshape(1, D_MODEL)
    g_mlp_pre = small["norm_mlp_pre"].reshape(1, D_MODEL)
    g_mlp_post = small["norm_mlp_post"].reshape(1, D_MODEL)

    q, k, v, gate, u = _inproj_fwd(x, g_mix_pre, w_in, cos2, sin2)
    y_ret, rprev = _ret_fwd(q, k, v, gate, gn_gain, ret_consts)
    y_ssm, starts = _s5_fwd(u, b_blk, c_blk, tables, d_skip)
    x1, mix = _mix_fwd(y_ssm, y_ret, x, g_mix_post, w_glu, w_out)
    df, dx2, d_ff2, dg_mlp_post, sq = _mlp_a(x1, target, g_mlp_pre, g_mlp_post, w_ff1, w_ff2)
    dx1, d_ff1, dg_mlp_pre = _mlp_b(df, dx2, x1, g_mlp_pre, w_ff1)
    dy_ret, dy_ssm, d_out, d_glu, dg_mix_post = _mix_bwd(dx1, mix, g_mix_post, y_ssm, y_ret, w_glu, w_out)
    du, dd, dlam8, db_dense, dc_dense = _s5_bwd(u, dy_ssm, b_blk, c_blk, tables, d_skip, starts)
    dq, dk, dv, dgate, dgn = _ret_bwd(q, k, v, gate, gn_gain, ret_consts, rprev, dy_ret)
    dx, d_in, dg_mix_pre = _inproj_bwd(x, g_mix_pre, w_in, cos2, sin2, dq, dk, dv, dgate, du, dx1)

    dlam = jnp.sum(dlam8, axis=1)
    dlbr = dlam[0].reshape(SSM_GROUPS, SSM_STATE)
    dlbi = dlam[1].reshape(SSM_GROUPS, SSM_STATE)
    db_parts = _outer_extract(db_dense)
    dc_parts = _outer_extract(dc_dense)
    d_lam_re, d_lam_im, d_log_dt, d_b_re, d_b_im = disc_vjp(
        (dlbr, dlbi, jnp.swapaxes(db_parts[0], 1, 2), jnp.swapaxes(db_parts[1], 1, 2)))
    d_c_re = dc_parts[0]
    d_c_im = -dc_parts[1]

    small_grads = {
        "norm_mix_pre": dg_mix_pre, "norm_mix_post": dg_mix_post, "ret_gn_gain": dgn,
        "ssm_lambda_re": d_lam_re, "ssm_lambda_im": d_lam_im, "ssm_log_dt": d_log_dt,
        "ssm_b_re": d_b_re, "ssm_b_im": d_b_im, "ssm_c_re": d_c_re, "ssm_c_im": d_c_im,
        "ssm_d": dd, "norm_mlp_pre": dg_mlp_pre, "norm_mlp_post": dg_mlp_post,
    }
    return sq, dx, (d_in, d_glu, d_out, d_ff1, d_ff2), small_grads


BIG = (
    ("w_in", D_MODEL, IN_COLS, 1),
    ("w_glu", SSM_WIDTH, 2 * SSM_WIDTH, 1),
    ("w_out", D_MODEL, D_MODEL, 0),
    ("w_ff1", D_MODEL, D_FF, 1),
    ("w_ff2", D_FF, D_MODEL, 0),
)
SMALL = (
    ("norm_mix_pre", (D_MODEL,)), ("norm_mix_post", (D_MODEL,)), ("ret_gn_gain", (RET_WIDTH,)),
    ("ssm_lambda_re", (SSM_GROUPS, SSM_STATE)), ("ssm_lambda_im", (SSM_GROUPS, SSM_STATE)), ("ssm_log_dt", (SSM_GROUPS,)),
    ("ssm_b_re", (SSM_GROUPS, SSM_STATE, SSM_GROUP)), ("ssm_b_im", (SSM_GROUPS, SSM_STATE, SSM_GROUP)),
    ("ssm_c_re", (SSM_GROUPS, SSM_GROUP, SSM_STATE)), ("ssm_c_im", (SSM_GROUPS, SSM_GROUP, SSM_STATE)),
    ("ssm_d", (SSM_WIDTH,)), ("norm_mlp_pre", (D_MODEL,)), ("norm_mlp_post", (D_MODEL,)),
)
SMALL_ROWS = 1152
EXCH_TILES = 8


def _pack_small(tree):
    rows = []
    for name, shape in SMALL:
        flat = tree[name].reshape(-1).astype(F32)
        n = -(-flat.shape[0] // LANES) * LANES
        rows.append(jnp.pad(flat, (0, n - flat.shape[0])).reshape(-1, LANES))
    used = sum(r.shape[0] for r in rows)
    rows.append(jnp.zeros((SMALL_ROWS - used, LANES), F32))
    return jnp.concatenate(rows, axis=0)


def _unpack_small(packed):
    out, row = {}, 0
    for name, shape in SMALL:
        size = math.prod(shape)
        n = -(-size // LANES)
        out[name] = packed[row:row + n].reshape(-1)[:size].reshape((1,) + shape)
        row += n
    return out


def _half_shape(r, c, axis):
    return (r // 2, c) if axis == 1 else (r, c // 2)


def _region_shape(r, c, axis):
    return (r // 2, c // N_CHIPS) if axis == 1 else (r // N_CHIPS, c // 2)


def _shard_shape(r, c, axis):
    return (r, c // N_CHIPS) if axis == 1 else (r // N_CHIPS, c)


def _ds(start, size):
    return pl.ds(pl.multiple_of(start * size, size), size)


def _region_of_full(ref, r, c, axis, shard, half):
    if axis == 1:
        return ref.at[_ds(half, r // 2), _ds(shard, c // N_CHIPS)]
    return ref.at[_ds(shard, r // N_CHIPS), _ds(half, c // 2)]


def _shard_of_full(ref, r, c, axis, shard):
    if axis == 1:
        return ref.at[:, _ds(shard, c // N_CHIPS)]
    return ref.at[_ds(shard, r // N_CHIPS), :]


def _half_of_full(ref, r, c, axis, half):
    if axis == 1:
        return ref.at[_ds(half, r // 2), :]
    return ref.at[:, _ds(half, c // 2)]


def _half_of_shard(ref, r, c, axis, half):
    if axis == 1:
        return ref.at[_ds(half, r // 2), :]
    return ref.at[:, _ds(half, c // 2)]


def _region_of_half(ref, r, c, axis, shard):
    if axis == 1:
        return ref.at[:, _ds(shard, c // N_CHIPS)]
    return ref.at[_ds(shard, r // N_CHIPS), :]


def _place():
    x, y, c = lax.axis_index("x"), lax.axis_index("y"), lax.axis_index("c")
    chips = [(1 - x, y), (x, 1 - y), (1 - x, 1 - y)]
    return x, y, c, chips


ANY = pl.BlockSpec(memory_space=pl.ANY)


def _remote(src, dst, send_sem, recv_sem, to):
    return pltpu.make_async_remote_copy(src_ref=src, dst_ref=dst, send_sem=send_sem, recv_sem=recv_sem,
                                        device_id=to, device_id_type=MESH)


def _gather_weights(shards):
    nw = len(BIG)

    def body(*refs):
        shard_refs, full_refs = refs[:nw], refs[nw:2 * nw]
        send_sems, recv_sems, pass_send, pass_recv, local_sems = refs[2 * nw:]
        x, y, c, chips = _place()
        mine = 2 * x + y
        local = []
        for w, (_, r, cc, axis) in enumerate(BIG):
            cp = pltpu.make_async_copy(shard_refs[w], _shard_of_full(full_refs[w], r, cc, axis, mine), local_sems.at[w])
            cp.start()
            local.append(cp)
        sent = []
        for w, (_, r, cc, axis) in enumerate(BIG):
            for k, (cx, cy) in enumerate(chips):
                cp = _remote(_half_of_shard(shard_refs[w], r, cc, axis, c), _region_of_full(full_refs[w], r, cc, axis, mine, c),
                             send_sems.at[3 * w + k], recv_sems.at[3 * w + k], (cx, cy, c))
                cp.start()
                sent.append(cp)
        passed = []
        for w, (_, r, cc, axis) in enumerate(BIG):
            for k, (cx, cy) in enumerate(chips):
                landed = _region_of_full(full_refs[w], r, cc, axis, 2 * cx + cy, c)
                _remote(landed, landed, send_sems.at[3 * w + k], recv_sems.at[3 * w + k], (cx, cy, c)).wait_recv()
                cp = _remote(landed, landed, pass_send.at[3 * w + k], pass_recv.at[3 * w + k], (x, y, 1 - c))
                cp.start()
                passed.append(cp)
        for w, (_, r, cc, axis) in enumerate(BIG):
            for k, (cx, cy) in enumerate(chips):
                other = _region_of_full(full_refs[w], r, cc, axis, 2 * cx + cy, 1 - c)
                _remote(other, other, pass_send.at[3 * w + k], pass_recv.at[3 * w + k], (x, y, 1 - c)).wait_recv()
        for cp in sent + passed:
            cp.wait_send()
        for cp in local:
            cp.wait()

    n = 3 * nw
    return pl.pallas_call(
        body, name="gather_weights",
        in_specs=[ANY] * nw, out_specs=[ANY] * nw,
        out_shape=[jax.ShapeDtypeStruct((r, cc), BF16) for _, r, cc, _ in BIG],
        scratch_shapes=[pltpu.SemaphoreType.DMA((n,)), pltpu.SemaphoreType.DMA((n,)), pltpu.SemaphoreType.DMA((n,)),
                        pltpu.SemaphoreType.DMA((n,)), pltpu.SemaphoreType.DMA((nw,))],
    )(*shards)


def _pair_exchange(grads, small):
    nw = len(BIG)

    def body(*refs):
        g_refs, got_refs = refs[:nw + 1], refs[nw + 1:2 * nw + 2]
        send_sems, recv_sems = refs[2 * nw + 2:]
        x, y, c, _ = _place()
        copies = []
        for w, (_, r, cc, axis) in enumerate(BIG):
            copies.append(_remote(_half_of_full(g_refs[w], r, cc, axis, 1 - c), got_refs[w], send_sems.at[w], recv_sems.at[w],
                                  (x, y, 1 - c)))
        copies.append(_remote(g_refs[nw].at[_ds(1 - c, SMALL_ROWS // 2), :], got_refs[nw], send_sems.at[nw], recv_sems.at[nw],
                              (x, y, 1 - c)))
        for cp in copies:
            cp.start()
        for cp in copies:
            cp.wait()

    return pl.pallas_call(
        body, name="pair_exchange",
        in_specs=[ANY] * (nw + 1), out_specs=[ANY] * (nw + 1),
        out_shape=[jax.ShapeDtypeStruct(_half_shape(r, cc, axis), F32) for _, r, cc, axis in BIG]
        + [jax.ShapeDtypeStruct((SMALL_ROWS // 2, LANES), F32)],
        scratch_shapes=[pltpu.SemaphoreType.DMA((nw + 1,)), pltpu.SemaphoreType.DMA((nw + 1,))],
    )(*grads, small)


def _tile_rows(shape):
    return (shape[0] // EXCH_TILES, shape[1])


def _pair_sum(ids, grads, small, got):
    nw = len(BIG)
    halves = [_half_shape(r, cc, axis) for _, r, cc, axis in BIG] + [(SMALL_ROWS // 2, LANES)]

    def body(ids_ref, *refs):
        n = nw + 1
        for j in range(n):
            total = refs[j][...] + refs[n + j][...]
            refs[2 * n + j][...] = total
            if j < nw:
                refs[3 * n + j][...] = total.astype(BF16)

    def mine_spec(j):
        tile = _tile_rows(halves[j])
        if j < nw and BIG[j][3] == 0:
            return pl.BlockSpec(tile, lambda i, ids: (i, ids[0]))
        return pl.BlockSpec(tile, lambda i, ids: (ids[0] * EXCH_TILES + i, 0))

    plain = lambda j: pl.BlockSpec(_tile_rows(halves[j]), lambda i, ids: (i, 0))
    n = nw + 1
    return pl.pallas_call(
        body, name="pair_sum",
        grid_spec=pltpu.PrefetchScalarGridSpec(
            num_scalar_prefetch=1, grid=(EXCH_TILES,),
            in_specs=[mine_spec(j) for j in range(n)] + [plain(j) for j in range(n)],
            out_specs=[plain(j) for j in range(n)] + [plain(j) for j in range(nw)]),
        out_shape=[jax.ShapeDtypeStruct(h, F32) for h in halves] + [jax.ShapeDtypeStruct(h, BF16) for h in halves[:nw]],
        compiler_params=pltpu.CompilerParams(dimension_semantics=("parallel",)),
    )(ids, *grads, small, *got)


def _chip_exchange(sums_bf16, small_sum):
    nw = len(BIG)

    def body(*refs):
        src_refs, got_refs = refs[:nw + 1], refs[nw + 1:2 * nw + 2]
        send_sems, recv_sems = refs[2 * nw + 2:]
        x, y, c, chips = _place()
        copies = []
        for w, (_, r, cc, axis) in enumerate(BIG):
            for k, (cx, cy) in enumerate(chips):
                copies.append(_remote(_region_of_half(src_refs[w], r, cc, axis, 2 * cx + cy), got_refs[w].at[k],
                                      send_sems.at[3 * w + k], recv_sems.at[3 * w + k], (cx, cy, c)))
        for k, (cx, cy) in enumerate(chips):
            copies.append(_remote(src_refs[nw], got_refs[nw].at[k], send_sems.at[3 * nw + k], recv_sems.at[3 * nw + k], (cx, cy, c)))
        for cp in copies:
            cp.start()
        for cp in copies:
            cp.wait()

    n = 3 * (nw + 1)
    return pl.pallas_call(
        body, name="chip_exchange",
        in_specs=[ANY] * (nw + 1), out_specs=[ANY] * (nw + 1),
        out_shape=[jax.ShapeDtypeStruct((3,) + _region_shape(r, cc, axis), BF16) for _, r, cc, axis in BIG]
        + [jax.ShapeDtypeStruct((3, SMALL_ROWS // 2, LANES), F32)],
        scratch_shapes=[pltpu.SemaphoreType.DMA((n,)), pltpu.SemaphoreType.DMA((n,))],
    )(*sums_bf16, small_sum)


def _chip_sum(ids, sums, got):
    nw = len(BIG)
    regions = [_region_shape(r, cc, axis) for _, r, cc, axis in BIG] + [(SMALL_ROWS // 2, LANES)]

    def body(ids_ref, *refs):
        n = nw + 1
        for j in range(n):
            own, arrived, out = refs[j], refs[n + j], refs[2 * n + j]
            if j < nw:
                out[...] = ((own[...] + arrived[0].astype(F32)) + arrived[1].astype(F32)) + arrived[2].astype(F32)
            else:
                out[...] = (own[...] + arrived[1]) + (arrived[0] + arrived[2])

    def own_spec(j):
        tile = _tile_rows(regions[j])
        if j == nw:
            return pl.BlockSpec(tile, lambda i, ids: (i, 0))
        if BIG[j][3] == 1:
            return pl.BlockSpec(tile, lambda i, ids: (i, ids[1]))
        return pl.BlockSpec(tile, lambda i, ids: (ids[1] * EXCH_TILES + i, 0))

    arrived_spec = lambda j: pl.BlockSpec((3,) + _tile_rows(regions[j]), lambda i, ids: (0, i, 0))
    plain = lambda j: pl.BlockSpec(_tile_rows(regions[j]), lambda i, ids: (i, 0))
    n = nw + 1
    return pl.pallas_call(
        body, name="chip_sum",
        grid_spec=pltpu.PrefetchScalarGridSpec(
            num_scalar_prefetch=1, grid=(EXCH_TILES,),
            in_specs=[own_spec(j) for j in range(n)] + [arrived_spec(j) for j in range(n)],
            out_specs=[plain(j) for j in range(n)]),
        out_shape=[jax.ShapeDtypeStruct(rg, F32) for rg in regions],
        compiler_params=pltpu.CompilerParams(dimension_semantics=("parallel",)),
    )(ids, *sums, *got)


def _pair_share(parts):
    nw = len(BIG)

    def body(*refs):
        part_refs, whole_refs = refs[:nw + 1], refs[nw + 1:2 * nw + 2]
        send_sems, recv_sems, local_sems = refs[2 * nw + 2:]
        x, y, c, _ = _place()

        def half_of_whole(w, half):
            if w == nw:
                return whole_refs[w].at[_ds(half, SMALL_ROWS // 2), :]
            _, r, cc, axis = BIG[w]
            return _half_of_shard(whole_refs[w], r, cc, axis, half)

        local, remote = [], []
        for w in range(nw + 1):
            lc = pltpu.make_async_copy(part_refs[w], half_of_whole(w, c), local_sems.at[w])
            lc.start()
            local.append(lc)
            rc = _remote(part_refs[w], half_of_whole(w, c), send_sems.at[w], recv_sems.at[w], (x, y, 1 - c))
            rc.start()
            remote.append(rc)
        for w in range(nw + 1):
            _remote(part_refs[w], half_of_whole(w, 1 - c), send_sems.at[w], recv_sems.at[w], (x, y, 1 - c)).wait_recv()
        for cp in remote:
            cp.wait_send()
        for cp in local:
            cp.wait()

    return pl.pallas_call(
        body, name="pair_share",
        in_specs=[ANY] * (nw + 1), out_specs=[ANY] * (nw + 1),
        out_shape=[jax.ShapeDtypeStruct(_shard_shape(r, cc, axis), F32) for _, r, cc, axis in BIG]
        + [jax.ShapeDtypeStruct((SMALL_ROWS, LANES), F32)],
        scratch_shapes=[pltpu.SemaphoreType.DMA((nw + 1,)), pltpu.SemaphoreType.DMA((nw + 1,)),
                        pltpu.SemaphoreType.DMA((nw + 1,))],
    )(*parts)


def _adamw(ws, gs, ms, vs):
    n = len(ws)
    c1 = 1.0 - ADAM_B1 ** ADAM_STEP
    c2 = 1.0 - ADAM_B2 ** ADAM_STEP

    def body(*refs):
        for j in range(n):
            w, g, m, v = (refs[k * n + j][...] for k in range(4))
            m = ADAM_B1 * m + (1.0 - ADAM_B1) * g
            v = ADAM_B2 * v + (1.0 - ADAM_B2) * (g * g)
            refs[4 * n + j][...] = -ADAM_LR * ((m / c1) / (jnp.sqrt(v / c2) + ADAM_EPS) + ADAM_WD * w)
            refs[5 * n + j][...] = m
            refs[6 * n + j][...] = v

    specs = [pl.BlockSpec(_tile_rows(w.shape), lambda i: (i, 0)) for w in ws]
    shapes = [jax.ShapeDtypeStruct(w.shape, F32) for w in ws]
    out = pl.pallas_call(
        body, name="adamw", grid=(EXCH_TILES,),
        in_specs=specs * 4, out_specs=specs * 3, out_shape=shapes * 3,
        compiler_params=pltpu.CompilerParams(dimension_semantics=("parallel",)),
    )(*ws, *gs, *ms, *vs)
    return out[:n], out[n:2 * n], out[2 * n:]


def kernel(x, norm_mix_pre, norm_mix_post, w_in, ret_gn_gain, ssm_lambda_re, ssm_lambda_im, ssm_log_dt, ssm_b_re, ssm_b_im, ssm_c_re, ssm_c_im, ssm_d, w_glu, w_out, norm_mlp_pre, norm_mlp_post, w_ff1, w_ff2, loss_target, m_norm_mix_pre, m_norm_mix_post, m_w_in, m_ret_gn_gain, m_ssm_lambda_re, m_ssm_lambda_im, m_ssm_log_dt, m_ssm_b_re, m_ssm_b_im, m_ssm_c_re, m_ssm_c_im, m_ssm_d, m_w_glu, m_w_out, m_norm_mlp_pre, m_norm_mlp_post, m_w_ff1, m_w_ff2, v_norm_mix_pre, v_norm_mix_post, v_w_in, v_ret_gn_gain, v_ssm_lambda_re, v_ssm_lambda_im, v_ssm_log_dt, v_ssm_b_re, v_ssm_b_im, v_ssm_c_re, v_ssm_c_im, v_ssm_d, v_w_glu, v_w_out, v_norm_mlp_pre, v_norm_mlp_post, v_w_ff1, v_w_ff2):
    given = dict(locals())
    order = ["norm_mix_pre", "norm_mix_post", "w_in", "ret_gn_gain", "ssm_lambda_re", "ssm_lambda_im", "ssm_log_dt",
             "ssm_b_re", "ssm_b_im", "ssm_c_re", "ssm_c_im", "ssm_d", "w_glu", "w_out", "norm_mlp_pre", "norm_mlp_post",
             "w_ff1", "w_ff2"]
    big_names = [name for name, _, _, _ in BIG]
    ids = jnp.stack([lax.axis_index("c"), 2 * lax.axis_index("x") + lax.axis_index("y")]).astype(jnp.int32)

    whole = _gather_weights([given[name][0].astype(BF16) for name in big_names])
    small_w = {name: given[name][0] for name, _ in SMALL}
    sq, dx, big_grads, small_grads = _local_step(x[0], loss_target[0], small_w, *whole)

    small_packed = _pack_small(small_grads)
    got = _pair_exchange(big_grads, small_packed)
    sums = _pair_sum(ids, big_grads, small_packed, got)
    nw = len(BIG)
    arrived = _chip_exchange(sums[nw + 1:], sums[nw])
    parts = _chip_sum(ids, sums[:nw + 1], arrived)
    reduced = _pair_share(parts)

    pack = lambda prefix: _pack_small({name: given[prefix + name][0] for name, _ in SMALL})
    ws = [given[name][0] for name in big_names] + [pack("")]
    ms = [given["m_" + name][0] for name in big_names] + [pack("m_")]
    vs = [given["v_" + name][0] for name in big_names] + [pack("v_")]
    deltas, new_m, new_v = _adamw(ws, list(reduced), ms, vs)

    def unpacked(arrays):
        tree = _unpack_small(arrays[nw])
        tree.update({name: arrays[j][None] for j, name in enumerate(big_names)})
        return [tree[name] for name in order]

    loss = lax.psum(0.5 * jnp.sum(sq) / D_MODEL, ("x", "y", "c"))
    return (loss, dx[None], *unpacked(list(reduced)), *unpacked(deltas), *unpacked(new_m), *unpacked(new_v))
```

```python
import functools
import math

import jax
import jax.numpy as jnp
from jax import lax
from jax.experimental import pallas as pl
from jax.experimental.pallas import tpu as pltpu

F32 = jnp.float32
BF16 = jnp.bfloat16

D_MODEL = 1024
RET_WIDTH = 512
RET_HEADS = 4
HEAD_DIM = 128
RET_CHUNK = 128
ROPE_BASE = 10000.0
SSM_WIDTH = 512
SSM_GROUP = 16
SSM_GROUPS = 32
SSM_STATE = 64
N_STATES = SSM_GROUPS * SSM_STATE
D_FF = 4096
IN_COLS = 4 * RET_WIDTH + SSM_WIDTH
NORM_EPS = 1e-6
K_SCALE = HEAD_DIM ** -0.5

ADAM_LR = 0.001
ADAM_B1 = 0.9
ADAM_B2 = 0.999
ADAM_EPS = 1e-08
ADAM_WD = 0.01
ADAM_STEP = 10

LANES = 128
SUBLANES = 8
VMEM_LIMIT = 56 * 2 ** 20

TOK_TILE = 512
MLP_TILE = 256
RET_CHUNKS_PER_STEP = 4
S5_TILE = 256
S5_STEPS = S5_TILE // SUBLANES
S5_COLBLK = N_STATES // LANES
S5_GROUP = 4

N_CHIPS = 4
MESH = pl.DeviceIdType.MESH


def _dot(a, b):
    return jnp.dot(a, b, preferred_element_type=F32)


def _dot_nt(a, b):
    return lax.dot_general(a, b, (((1,), (1,)), ((), ())), preferred_element_type=F32)


def _dot_tn(a, b):
    return lax.dot_general(a, b, (((0,), (0,)), ((), ())), preferred_element_type=F32)


def _sigmoid(x):
    return 1.0 / (1.0 + jnp.exp(-x))


_GELU_C = math.sqrt(2.0 / math.pi)


def _gelu(x):
    return 0.5 * x * (1.0 + jnp.tanh(_GELU_C * (x + 0.044715 * (x * x * x))))


def _gelu_grad(x):
    t = jnp.tanh(_GELU_C * (x + 0.044715 * (x * x * x)))
    return 0.5 * (1.0 + t) + 0.5 * x * (1.0 - t * t) * (_GELU_C * (1.0 + 3.0 * 0.044715 * (x * x)))


def _rms(x):
    r = lax.rsqrt(jnp.mean(x * x, axis=-1, keepdims=True) + NORM_EPS)
    return x * r, r


def _rms_bwd(n, r, gain, dy):
    dn = dy * gain
    dx = r * (dn - n * jnp.mean(dn * n, axis=-1, keepdims=True))
    return dx, jnp.sum(dy * n, axis=0, keepdims=True)


def _full(shape):
    nd = len(shape)
    return pl.BlockSpec(shape, lambda i, _n=nd: (0,) * _n, pipeline_mode=pl.Buffered(1))


def _rows(tile, width):
    return pl.BlockSpec((tile, width), lambda i: (i, 0))


def _rows_rev(tile, width, n):
    return pl.BlockSpec((tile, width), lambda i, _n=n: (_n - 1 - i, 0))


def _params(sem):
    return pltpu.CompilerParams(dimension_semantics=(sem,), vmem_limit_bytes=VMEM_LIMIT)


def _rope(t, cos2, sin2):
    return t * cos2 + pltpu.roll(t, HEAD_DIM // 2, 1) * sin2


def _rope_bwd(d, cos2, sin2):
    return d * cos2 + pltpu.roll(d * sin2, HEAD_DIM // 2, 1)


def _inproj_fwd(x, gain, w_in, cos2, sin2):
    seq = x.shape[0]

    def body(x_ref, g_ref, w_ref, c_ref, s_ref, q_ref, k_ref, v_ref, gate_ref, u_ref):
        n, _ = _rms(x_ref[...])
        h = (n * g_ref[...]).astype(BF16)
        proj = _dot(h, w_ref[...])
        c = c_ref[...]
        s = s_ref[...]
        for hd in range(RET_HEADS):
            lo = hd * HEAD_DIM
            q_ref[:, lo:lo + HEAD_DIM] = _rope(proj[:, lo:lo + HEAD_DIM], c, s).astype(BF16)
            kh = proj[:, RET_WIDTH + lo:RET_WIDTH + lo + HEAD_DIM]
            k_ref[:, lo:lo + HEAD_DIM] = (_rope(kh, c, s) * K_SCALE).astype(BF16)
        v_ref[...] = proj[:, 2 * RET_WIDTH:3 * RET_WIDTH].astype(BF16)
        gate_ref[...] = proj[:, 3 * RET_WIDTH:4 * RET_WIDTH]
        u_ref[...] = proj[:, 4 * RET_WIDTH:]

    t = TOK_TILE
    half = lambda dt: jax.ShapeDtypeStruct((seq, RET_WIDTH), dt)
    return pl.pallas_call(
        body, name="inproj_fwd", grid=(seq // t,),
        in_specs=[_rows(t, D_MODEL), _full((1, D_MODEL)), _full((D_MODEL, IN_COLS)), _rows(t, HEAD_DIM), _rows(t, HEAD_DIM)],
        out_specs=[_rows(t, RET_WIDTH)] * 5,
        out_shape=[half(BF16), half(BF16), half(BF16), half(F32), half(F32)],
        compiler_params=_params("parallel"),
    )(x, gain, w_in, cos2, sin2)


def _inproj_bwd(x, gain, w_in, cos2, sin2, dq, dk, dv, dgate, du, dres):
    seq = x.shape[0]
    t = TOK_TILE

    def body(x_ref, g_ref, w_ref, c_ref, s_ref, dq_ref, dk_ref, dv_ref, dgate_ref, du_ref, dres_ref,
             dx_ref, dw_ref, dg_ref, dproj):
        @pl.when(pl.program_id(0) == 0)
        def _():
            dw_ref[...] = jnp.zeros_like(dw_ref)
            dg_ref[...] = jnp.zeros_like(dg_ref)

        c = c_ref[...]
        s = s_ref[...]
        for hd in range(RET_HEADS):
            lo = hd * HEAD_DIM
            dproj[:, lo:lo + HEAD_DIM] = _rope_bwd(dq_ref[:, lo:lo + HEAD_DIM], c, s).astype(BF16)
            dproj[:, RET_WIDTH + lo:RET_WIDTH + lo + HEAD_DIM] = _rope_bwd(
                dk_ref[:, lo:lo + HEAD_DIM] * K_SCALE, c, s).astype(BF16)
        dproj[:, 2 * RET_WIDTH:3 * RET_WIDTH] = dv_ref[...].astype(BF16)
        dproj[:, 3 * RET_WIDTH:4 * RET_WIDTH] = dgate_ref[...].astype(BF16)
        dproj[:, 4 * RET_WIDTH:] = du_ref[...].astype(BF16)
        dp = dproj[...]
        n, r = _rms(x_ref[...])
        gain_v = g_ref[...]
        h = (n * gain_v).astype(BF16)
        dw_ref[...] += _dot_tn(h, dp)
        dh = _dot_nt(dp, w_ref[...])
        dx, dgs = _rms_bwd(n, r, gain_v, dh)
        dg_ref[...] += dgs
        dx_ref[...] = dres_ref[...] + dx

    return pl.pallas_call(
        body, name="inproj_bwd", grid=(seq // t,),
        in_specs=[_rows(t, D_MODEL), _full((1, D_MODEL)), _full((D_MODEL, IN_COLS)), _rows(t, HEAD_DIM), _rows(t, HEAD_DIM)]
        + [_rows(t, RET_WIDTH)] * 5 + [_rows(t, D_MODEL)],
        out_specs=[_rows(t, D_MODEL), _full((D_MODEL, IN_COLS)), _full((1, D_MODEL))],
        out_shape=[jax.ShapeDtypeStruct((seq, D_MODEL), F32), jax.ShapeDtypeStruct((D_MODEL, IN_COLS), F32),
                   jax.ShapeDtypeStruct((1, D_MODEL), F32)],
        scratch_shapes=[pltpu.VMEM((t, IN_COLS), BF16)],
        compiler_params=_params("arbitrary"),
    )(x, gain, w_in, cos2, sin2, dq, dk, dv, dgate, du, dres)


def _ret_consts():
    c = RET_CHUNK
    log_gamma = jnp.log(1.0 - jnp.exp(jnp.linspace(math.log(1.0 / 32), math.log(1.0 / 512), RET_HEADS))).astype(F32)
    idx = jnp.arange(c, dtype=F32)
    diff = idx[:, None] - idx[None, :]
    decay = jnp.where(diff[None] >= 0, jnp.exp(jnp.maximum(diff, 0.0)[None] * log_gamma[:, None, None]), 0.0)
    zeta = jnp.exp((c - 1 - idx)[None, :] * log_gamma[:, None])
    xi = jnp.exp((idx + 1.0)[None, :] * log_gamma[:, None])
    g_chunk = jnp.exp(c * log_gamma)
    wide = lambda rowvals: jnp.broadcast_to(rowvals[:, :, None], (RET_HEADS, c, c)).astype(F32)
    return decay.astype(F32), wide(xi), wide(zeta), jnp.broadcast_to(g_chunk[:, None, None], (RET_HEADS, c, c)).astype(F32)


def _bmm(a, b):
    return lax.dot_general(a, b, (((2,), (1,)), ((0,), (0,))), preferred_element_type=F32)


def _bmm_nt(a, b):
    return lax.dot_general(a, b, (((2,), (2,)), ((0,), (0,))), preferred_element_type=F32)


def _bmm_tn(a, b):
    return lax.dot_general(a, b, (((1,), (1,)), ((0,), (0,))), preferred_element_type=F32)


def _ret_blocks(ref):
    return jnp.stack([ref[cc * RET_CHUNK:(cc + 1) * RET_CHUNK, hd * HEAD_DIM:(hd + 1) * HEAD_DIM]
                      for cc in range(RET_CHUNKS_PER_STEP) for hd in range(RET_HEADS)])


def _ret_unblock(ref, blocks):
    for cc in range(RET_CHUNKS_PER_STEP):
        for hd in range(RET_HEADS):
            ref[cc * RET_CHUNK:(cc + 1) * RET_CHUNK, hd * HEAD_DIM:(hd + 1) * HEAD_DIM] = blocks[cc * RET_HEADS + hd].astype(ref.dtype)


def _per_chunk(ref):
    return jnp.concatenate([ref[...]] * RET_CHUNKS_PER_STEP, axis=0)


def _ret_gain(gain_ref):
    return jnp.stack([gain_ref[:, hd * HEAD_DIM:(hd + 1) * HEAD_DIM]
                      for _ in range(RET_CHUNKS_PER_STEP) for hd in range(RET_HEADS)])


def _ret_fwd(q, k, v, gate, gn_gain, consts):
    seq = q.shape[0]
    cps = RET_CHUNKS_PER_STEP
    t = cps * RET_CHUNK
    n_chunks = seq // RET_CHUNK
    dec, xi, zeta, gch = consts

    def body(q_ref, k_ref, v_ref, gate_ref, gain_ref, dec_ref, xi_ref, zeta_ref, gch_ref, y_ref, rprev_ref, state):
        @pl.when(pl.program_id(0) == 0)
        def _():
            state[...] = jnp.zeros_like(state)

        qb, kb, vb = _ret_blocks(q_ref), _ret_blocks(k_ref), _ret_blocks(v_ref)
        s = _bmm_nt(qb, kb) * _per_chunk(dec_ref)
        vz = (vb.astype(F32) * _per_chunk(zeta_ref)).astype(BF16)
        incr = _bmm_tn(kb, vz)
        gch_v = gch_ref[...]
        rp = state[...]
        before = []
        for cc in range(cps):
            before.append(rp.astype(BF16))
            rprev_ref[cc] = before[-1]
            rp = gch_v * rp + incr[cc * RET_HEADS:(cc + 1) * RET_HEADS]
        state[...] = rp
        qx = (qb.astype(F32) * _per_chunk(xi_ref)).astype(BF16)
        o = _bmm(s.astype(BF16), vb) + _bmm(qx, jnp.concatenate(before, axis=0))
        oc = o - jnp.mean(o, axis=-1, keepdims=True)
        on = oc * lax.rsqrt(jnp.mean(oc * oc, axis=-1, keepdims=True) + NORM_EPS)
        g = _ret_blocks(gate_ref)
        _ret_unblock(y_ref, g * _sigmoid(g) * (on * _ret_gain(gain_ref)))

    cst = _full((RET_HEADS, RET_CHUNK, RET_CHUNK))
    return pl.pallas_call(
        body, name="ret_fwd", grid=(seq // t,),
        in_specs=[_rows(t, RET_WIDTH)] * 4 + [_full((1, RET_WIDTH)), cst, cst, cst, cst],
        out_specs=[_rows(t, RET_WIDTH), pl.BlockSpec((cps, RET_HEADS, HEAD_DIM, HEAD_DIM), lambda i: (i, 0, 0, 0))],
        out_shape=[jax.ShapeDtypeStruct((seq, RET_WIDTH), BF16),
                   jax.ShapeDtypeStruct((n_chunks, RET_HEADS, HEAD_DIM, HEAD_DIM), BF16)],
        scratch_shapes=[pltpu.VMEM((RET_HEADS, HEAD_DIM, HEAD_DIM), F32)],
        compiler_params=_params("arbitrary"),
    )(q, k, v, gate, gn_gain, dec, xi, zeta, gch)


def _ret_bwd(q, k, v, gate, gn_gain, consts, rprev, dy_ret):
    seq = q.shape[0]
    cps = RET_CHUNKS_PER_STEP
    t = cps * RET_CHUNK
    nt = seq // t
    dec, xi, zeta, gch = consts

    def body(q_ref, k_ref, v_ref, gate_ref, gain_ref, dec_ref, xi_ref, zeta_ref, gch_ref, rprev_ref, dyr_ref,
             dq_ref, dk_ref, dv_ref, dgate_ref, dgain_ref, dstate):
        @pl.when(pl.program_id(0) == 0)
        def _():
            dstate[...] = jnp.zeros_like(dstate)
            dgain_ref[...] = jnp.zeros_like(dgain_ref)

        nb = cps * RET_HEADS
        qb, kb, vb = _ret_blocks(q_ref), _ret_blocks(k_ref), _ret_blocks(v_ref)
        dec_b, xi_b, zeta_b = _per_chunk(dec_ref), _per_chunk(xi_ref), _per_chunk(zeta_ref)
        rpb = rprev_ref[...].reshape(nb, HEAD_DIM, HEAD_DIM)
        sdb = (_bmm_nt(qb, kb) * dec_b).astype(BF16)
        qx = (qb.astype(F32) * xi_b).astype(BF16)
        o = _bmm(sdb, vb) + _bmm(qx, rpb)
        oc = o - jnp.mean(o, axis=-1, keepdims=True)
        rstd = lax.rsqrt(jnp.mean(oc * oc, axis=-1, keepdims=True) + NORM_EPS)
        on = oc * rstd
        g = _ret_blocks(gate_ref)
        sg = _sigmoid(g)
        gain_b = _ret_gain(gain_ref)
        dyr = _ret_blocks(dyr_ref)
        _ret_unblock(dgate_ref, dyr * (on * gain_b) * (sg * (1.0 + g * (1.0 - sg))))
        dy = dyr * (g * sg)
        dgain = jnp.sum(dy * on, axis=1, keepdims=True)
        for hd in range(RET_HEADS):
            part = dgain[hd]
            for cc in range(1, cps):
                part = part + dgain[cc * RET_HEADS + hd]
            dgain_ref[:, hd * HEAD_DIM:(hd + 1) * HEAD_DIM] += part
        don = dy * gain_b
        do = rstd * (don - jnp.mean(don, axis=-1, keepdims=True) - on * jnp.mean(don * on, axis=-1, keepdims=True))
        dob = do.astype(BF16)
        ds = (_bmm_nt(dob, vb) * dec_b).astype(BF16)
        _ret_unblock(dq_ref, _bmm(ds, kb) + _bmm_nt(dob, rpb) * xi_b)
        dk = _bmm_tn(ds, qb)
        dv = _bmm_tn(sdb, dob)
        dstate_local = _bmm_tn(qx, dob)
        vz = (vb.astype(F32) * zeta_b).astype(BF16)
        gch_v = gch_ref[...]
        zeta_v = zeta_ref[...]
        gh = dstate[...]
        dk_extra, dv_extra = [None] * cps, [None] * cps
        for cc in reversed(range(cps)):
            sl = slice(cc * RET_HEADS, (cc + 1) * RET_HEADS)
            gb = gh.astype(BF16)
            dk_extra[cc] = _bmm_nt(vz[sl], gb)
            dv_extra[cc] = _bmm(kb[sl], gb) * zeta_v
            gh = dstate_local[sl] + gch_v * gh
        dstate[...] = gh
        _ret_unblock(dk_ref, dk + jnp.concatenate(dk_extra, axis=0))
        _ret_unblock(dv_ref, dv + jnp.concatenate(dv_extra, axis=0))

    cst = _full((RET_HEADS, RET_CHUNK, RET_CHUNK))
    rev = _rows_rev(t, RET_WIDTH, nt)
    act = jax.ShapeDtypeStruct((seq, RET_WIDTH), F32)
    return pl.pallas_call(
        body, name="ret_bwd", grid=(nt,),
        in_specs=[rev] * 4 + [_full((1, RET_WIDTH)), cst, cst, cst, cst,
                              pl.BlockSpec((cps, RET_HEADS, HEAD_DIM, HEAD_DIM), lambda i: (nt - 1 - i, 0, 0, 0)), rev],
        out_specs=[rev] * 4 + [_full((1, RET_WIDTH))],
        out_shape=[act, act, act, act, jax.ShapeDtypeStruct((1, RET_WIDTH), F32)],
        scratch_shapes=[pltpu.VMEM((RET_HEADS, HEAD_DIM, HEAD_DIM), F32)],
        compiler_params=_params("arbitrary"),
    )(q, k, v, gate, gn_gain, dec, xi, zeta, gch, rprev, dy_ret)


MXU_TILE = 256
S5_STATE_TILES = 2 * N_STATES // MXU_TILE
S5_CHANNELS_PER_TILE = MXU_TILE // SSM_STATE * SSM_GROUP
S5_TILES_SHAPE = (S5_STATE_TILES, LANES, MXU_TILE)


def _chan_block(kk):
    return ((kk % (S5_STATE_TILES // 2)) * S5_CHANNELS_PER_TILE) // LANES * LANES


def _s5_tiles(blocks_re, blocks_im):
    per = MXU_TILE // SSM_STATE
    half = S5_STATE_TILES // 2
    eye = jnp.eye(per, dtype=F32)
    in_half = (jnp.arange(half) % (LANES // S5_CHANNELS_PER_TILE))[:, None] == jnp.arange(LANES // S5_CHANNELS_PER_TILE)[None, :]
    out = []
    for blk in (blocks_re, blocks_im):
        t = blk.reshape(half, per, SSM_GROUP, 1, SSM_STATE) * eye[None, :, None, :, None]
        t = t.reshape(half, 1, S5_CHANNELS_PER_TILE, MXU_TILE)
        out.append(jnp.where(in_half[:, :, None, None], t, 0.0).reshape(half, LANES, MXU_TILE))
    return jnp.concatenate(out, axis=0)


def _s5_discretise(lam_re, lam_im, log_dt, b_re, b_im, c_re, c_im):
    lam = lax.complex(jnp.minimum(lam_re, -1e-4), lam_im)
    dt = jnp.exp(log_dt)[:, None]
    lam_bar = jnp.exp(lam * dt)
    b_bar = ((lam_bar - 1.0) / lam)[:, :, None] * lax.complex(b_re, b_im)
    b_tiles = _s5_tiles(jnp.swapaxes(jnp.real(b_bar), 1, 2), jnp.swapaxes(jnp.imag(b_bar), 1, 2))
    return jnp.real(lam_bar), jnp.imag(lam_bar), b_tiles, _s5_tiles(c_re, -c_im)


def _s5_tables(lbr, lbi):
    lr = lbr.reshape(1, N_STATES)
    li = lbi.reshape(1, N_STATES)
    pr, pi = [lr], [li]
    for _ in range(S5_STEPS - 1):
        pr, pi = pr + [pr[-1] * lr - pi[-1] * li], pi + [pr[-1] * li + pi[-1] * lr]
    bc = lambda a: jnp.broadcast_to(a, (SUBLANES, N_STATES))
    rep = lambda rows: jnp.concatenate([bc(r) for r in rows], axis=0)
    return bc(lr), bc(li), rep(pr), rep(pi)


def _cmul(ar, ai, br, bi):
    return ar * br - ai * bi, ar * bi + ai * br


def _seg(i):
    return pl.ds(pl.multiple_of(i * SUBLANES, SUBLANES), SUBLANES)


def _scan_order():
    r = jnp.arange(S5_TILE)
    token = (r % SUBLANES) * S5_STEPS + r // SUBLANES
    p = (token[:, None] == jnp.arange(S5_TILE)[None, :]).astype(BF16)
    return p, p.T


def _to_scan_order(p_ref, rows_bf16):
    return _dot(p_ref[...], rows_bf16).astype(BF16)


def _from_scan_order(pt_ref, rows):
    hi = rows.astype(BF16)
    rest = rows - hi.astype(F32)
    mid = rest.astype(BF16)
    lo = (rest - mid.astype(F32)).astype(BF16)
    pt = pt_ref[...]
    return _dot(pt, hi) + _dot(pt, mid) + _dot(pt, lo)


def _s5_channels_to_states(chan_b, m_ref, xs):
    for kk in range(S5_STATE_TILES):
        lo = _chan_block(kk)
        res = _dot(chan_b[:, lo:lo + LANES], m_ref[kk])
        xs[2 * kk] = res[:, :LANES]
        xs[2 * kk + 1] = res[:, LANES:]


def _s5_states_to_channels(sb_ref, m_ref):
    out = []
    for lo in range(0, SSM_WIDTH, LANES):
        acc = None
        for kk in range(S5_STATE_TILES):
            if _chan_block(kk) == lo:
                term = _dot_nt(sb_ref[:, kk * MXU_TILE:(kk + 1) * MXU_TILE], m_ref[kk])
                acc = term if acc is None else acc + term
        out.append(acc)
    return jnp.concatenate(out, axis=1)


def _s5_outer_acc(acc_ref, chan_b, states_ref):
    for kk in range(S5_STATE_TILES):
        lo = _chan_block(kk)
        acc_ref[kk] += _dot_tn(chan_b[:, lo:lo + LANES], states_ref[:, kk * MXU_TILE:(kk + 1) * MXU_TILE])


def _s5_scan_tile(xs, lr_ref, li_ref, pr_ref, pi_ref, carry_re, carry_im, conj, reverse):
    sgn = -1.0 if conj else 1.0
    seg_in_re, seg_in_im = [], []
    for grp in range(S5_COLBLK // S5_GROUP):
        blks = [grp * S5_GROUP + j for j in range(S5_GROUP)]
        lrs = [lr_ref[:, b * LANES:(b + 1) * LANES] for b in blks]
        lis = [sgn * li_ref[:, b * LANES:(b + 1) * LANES] for b in blks]

        def step(it, carry, blks=blks, lrs=lrs, lis=lis):
            i = (S5_STEPS - 1 - it) if reverse else it
            out_r, out_i = [], []
            for j, b in enumerate(blks):
                xr, xi = _cmul(lrs[j], lis[j], carry[j], carry[S5_GROUP + j])
                xr = xr + xs[b, _seg(i), :]
                xi = xi + xs[S5_COLBLK + b, _seg(i), :]
                xs[b, _seg(i), :] = xr
                xs[S5_COLBLK + b, _seg(i), :] = xi
                out_r.append(xr)
                out_i.append(xi)
            return tuple(out_r + out_i)

        zeros = tuple(jnp.zeros((SUBLANES, LANES), F32) for _ in range(2 * S5_GROUP))
        ends = lax.fori_loop(0, S5_STEPS, step, zeros, unroll=4)

        ins_r, ins_i = [], []
        for j, b in enumerate(blks):
            cols = slice(b * LANES, (b + 1) * LANES)
            pr = pr_ref[SUBLANES * S5_STEPS - 1:SUBLANES * S5_STEPS, cols]
            pi = sgn * pi_ref[SUBLANES * S5_STEPS - 1:SUBLANES * S5_STEPS, cols]
            cur_r = carry_re[:, cols]
            cur_i = carry_im[:, cols]
            row_id = lax.broadcasted_iota(jnp.int32, (SUBLANES, LANES), 0)
            in_r = jnp.zeros((SUBLANES, LANES), F32)
            in_i = jnp.zeros((SUBLANES, LANES), F32)
            order = range(SUBLANES - 1, -1, -1) if reverse else range(SUBLANES)
            for sgm in order:
                in_r = jnp.where(row_id == sgm, cur_r, in_r)
                in_i = jnp.where(row_id == sgm, cur_i, in_i)
                mr, mi = _cmul(pr, pi, cur_r, cur_i)
                cur_r = mr + ends[j][sgm:sgm + 1, :]
                cur_i = mi + ends[S5_GROUP + j][sgm:sgm + 1, :]
            carry_re[:, cols] = cur_r
            carry_im[:, cols] = cur_i
            ins_r.append(in_r)
            ins_i.append(in_i)

        def fix(it, c, blks=blks, ins_r=ins_r, ins_i=ins_i):
            pw = (S5_STEPS - 1 - it) if reverse else it
            for j, b in enumerate(blks):
                cols = slice(b * LANES, (b + 1) * LANES)
                prow = pl.ds(pl.multiple_of(pw * SUBLANES, SUBLANES), SUBLANES)
                fr, fi = _cmul(pr_ref[prow, cols], sgn * pi_ref[prow, cols], ins_r[j], ins_i[j])
                xs[b, _seg(it), :] = xs[b, _seg(it), :] + fr
                xs[S5_COLBLK + b, _seg(it), :] = xs[S5_COLBLK + b, _seg(it), :] + fi
            return c

        lax.fori_loop(0, S5_STEPS, fix, 0, unroll=4)
        seg_in_re += ins_r
        seg_in_im += ins_i
    return seg_in_re, seg_in_im


def _s5_pack(xs, dst):
    for b in range(2 * S5_COLBLK):
        dst[:, b * LANES:(b + 1) * LANES] = xs[b].astype(BF16)


def _s5_fwd(u, b_blk, c_blk, tables, d_skip):
    seq = u.shape[0]
    t = S5_TILE
    nt = seq // t
    lr8, li8, p_re, p_im = tables

    p, pt = _scan_order()

    def body(u_ref, p_ref, pt_ref, b_ref, c_ref, lr_ref, li_ref, pr_ref, pi_ref, d_ref, y_ref, start_ref,
             xs, xb, carry_re, carry_im):
        @pl.when(pl.program_id(0) == 0)
        def _():
            carry_re[...] = jnp.zeros_like(carry_re)
            carry_im[...] = jnp.zeros_like(carry_im)

        start_ref[0, 0:1, :] = carry_re[...]
        start_ref[0, 1:2, :] = carry_im[...]
        uv = u_ref[...]
        _s5_channels_to_states(_to_scan_order(p_ref, uv.astype(BF16)), b_ref, xs)
        _s5_scan_tile(xs, lr_ref, li_ref, pr_ref, pi_ref, carry_re, carry_im, conj=False, reverse=False)
        _s5_pack(xs, xb)
        y_ref[...] = _from_scan_order(pt_ref, _s5_states_to_channels(xb, c_ref)) + d_ref[...] * uv

    return pl.pallas_call(
        body, name="s5_fwd", grid=(nt,),
        in_specs=[_rows(t, SSM_WIDTH), _full((t, t)), _full((t, t)),
                  _full(S5_TILES_SHAPE), _full(S5_TILES_SHAPE),
                  _full((SUBLANES, N_STATES)), _full((SUBLANES, N_STATES)), _full((SUBLANES * S5_STEPS, N_STATES)),
                  _full((SUBLANES * S5_STEPS, N_STATES)), _full((1, SSM_WIDTH))],
        out_specs=[_rows(t, SSM_WIDTH), pl.BlockSpec((1, 2, N_STATES), lambda i: (i, 0, 0))],
        out_shape=[jax.ShapeDtypeStruct((seq, SSM_WIDTH), F32), jax.ShapeDtypeStruct((nt, 2, N_STATES), F32)],
        scratch_shapes=[pltpu.VMEM((2 * S5_COLBLK, t, LANES), F32), pltpu.VMEM((t, 2 * N_STATES), BF16),
                        pltpu.VMEM((1, N_STATES), F32), pltpu.VMEM((1, N_STATES), F32)],
        compiler_params=_params("arbitrary"),
    )(u, p, pt, b_blk, c_blk, lr8, li8, p_re, p_im, d_skip)


def _s5_bwd(u, dy, b_blk, c_blk, tables, d_skip, starts):
    seq = u.shape[0]
    t = S5_TILE
    nt = seq // t
    lr8, li8, p_re, p_im = tables
    p, pt = _scan_order()

    def body(u_ref, dy_ref, p_ref, pt_ref, b_ref, c_ref, lr_ref, li_ref, pr_ref, pi_ref, d_ref, start_ref,
             du_ref, dd_ref, dlam_ref, db_ref, dc_ref,
             xs, as_, xb, ab, carry_re, carry_im, acar_re, acar_im):
        @pl.when(pl.program_id(0) == 0)
        def _():
            acar_re[...] = jnp.zeros_like(acar_re)
            acar_im[...] = jnp.zeros_like(acar_im)
            dd_ref[...] = jnp.zeros_like(dd_ref)
            dlam_ref[...] = jnp.zeros_like(dlam_ref)
            db_ref[...] = jnp.zeros_like(db_ref)
            dc_ref[...] = jnp.zeros_like(dc_ref)

        uv = u_ref[...]
        dyv = dy_ref[...]
        ub = _to_scan_order(p_ref, uv.astype(BF16))
        dyb = _to_scan_order(p_ref, dyv.astype(BF16))
        carry_re[...] = start_ref[0, 0:1, :]
        carry_im[...] = start_ref[0, 1:2, :]
        _s5_channels_to_states(ub, b_ref, xs)
        in_re, in_im = _s5_scan_tile(xs, lr_ref, li_ref, pr_ref, pi_ref, carry_re, carry_im, conj=False, reverse=False)
        _s5_channels_to_states(dyb, c_ref, as_)
        _s5_scan_tile(as_, lr_ref, li_ref, pr_ref, pi_ref, acar_re, acar_im, conj=True, reverse=True)
        for b in range(S5_COLBLK):
            def acc(i, c, b=b):
                xpr, xpi, sr, si = c
                ar = as_[b, _seg(i), :]
                ai = as_[S5_COLBLK + b, _seg(i), :]
                sr = sr + ar * xpr + ai * xpi
                si = si + ai * xpr - ar * xpi
                return xs[b, _seg(i), :], xs[S5_COLBLK + b, _seg(i), :], sr, si

            z = jnp.zeros((SUBLANES, LANES), F32)
            _, _, sr, si = lax.fori_loop(0, S5_STEPS, acc, (in_re[b], in_im[b], z, z), unroll=4)
            cols = slice(b * LANES, (b + 1) * LANES)
            dlam_ref[0, :, cols] += sr
            dlam_ref[1, :, cols] += si
        _s5_pack(xs, xb)
        _s5_pack(as_, ab)
        du_ref[...] = _from_scan_order(pt_ref, _s5_states_to_channels(ab, b_ref)) + d_ref[...] * dyv
        dd_ref[...] += jnp.sum(dyv * uv, axis=0, keepdims=True)
        _s5_outer_acc(db_ref, ub, ab)
        _s5_outer_acc(dc_ref, dyb, xb)

    rev = _rows_rev(t, SSM_WIDTH, nt)
    vec = lambda: pltpu.VMEM((1, N_STATES), F32)
    outer = S5_TILES_SHAPE
    return pl.pallas_call(
        body, name="s5_bwd", grid=(nt,),
        in_specs=[rev, rev, _full((t, t)), _full((t, t)), _full(S5_TILES_SHAPE), _full(S5_TILES_SHAPE),
                  _full((SUBLANES, N_STATES)), _full((SUBLANES, N_STATES)), _full((SUBLANES * S5_STEPS, N_STATES)),
                  _full((SUBLANES * S5_STEPS, N_STATES)), _full((1, SSM_WIDTH)),
                  pl.BlockSpec((1, 2, N_STATES), lambda i: (nt - 1 - i, 0, 0))],
        out_specs=[rev, _full((1, SSM_WIDTH)), _full((2, SUBLANES, N_STATES)), _full(outer), _full(outer)],
        out_shape=[jax.ShapeDtypeStruct((seq, SSM_WIDTH), F32), jax.ShapeDtypeStruct((1, SSM_WIDTH), F32),
                   jax.ShapeDtypeStruct((2, SUBLANES, N_STATES), F32), jax.ShapeDtypeStruct(outer, F32),
                   jax.ShapeDtypeStruct(outer, F32)],
        scratch_shapes=[pltpu.VMEM((2 * S5_COLBLK, t, LANES), F32), pltpu.VMEM((2 * S5_COLBLK, t, LANES), F32),
                        pltpu.VMEM((t, 2 * N_STATES), BF16), pltpu.VMEM((t, 2 * N_STATES), BF16),
                        vec(), vec(), vec(), vec()],
        compiler_params=_params("arbitrary"),
    )(u, dy, p, pt, b_blk, c_blk, lr8, li8, p_re, p_im, d_skip, starts)


def _mix_fwd(y_ssm, y_ret, x, gain, w_glu, w_out):
    seq = x.shape[0]
    t = TOK_TILE

    def body(ys_ref, yr_ref, x_ref, g_ref, wg_ref, wo_ref, x1_ref, mix_ref):
        g0 = _gelu(ys_ref[...]).astype(BF16)
        z = _dot(g0, wg_ref[...])
        glu = (z[:, :SSM_WIDTH] * _sigmoid(z[:, SSM_WIDTH:])).astype(BF16)
        mix = _dot(yr_ref[...], wo_ref[:RET_WIDTH, :]) + _dot(glu, wo_ref[RET_WIDTH:, :])
        mix_ref[...] = mix
        n, _ = _rms(mix)
        x1_ref[...] = x_ref[...] + n * g_ref[...]

    act = jax.ShapeDtypeStruct((seq, D_MODEL), F32)
    return pl.pallas_call(
        body, name="mix_fwd", grid=(seq // t,),
        in_specs=[_rows(t, SSM_WIDTH), _rows(t, RET_WIDTH), _rows(t, D_MODEL), _full((1, D_MODEL)),
                  _full((SSM_WIDTH, 2 * SSM_WIDTH)), _full((D_MODEL, D_MODEL))],
        out_specs=[_rows(t, D_MODEL)] * 2, out_shape=[act, act],
        compiler_params=_params("parallel"),
    )(y_ssm, y_ret, x, gain, w_glu, w_out)


def _mix_bwd(dx1, mix, gain, y_ssm, y_ret, w_glu, w_out):
    seq = dx1.shape[0]
    t = TOK_TILE

    def body(dx1_ref, mix_ref, g_ref, ys_ref, yr_ref, wg_ref, wo_ref, dyr_ref, dys_ref, dwo_ref, dwg_ref, dg_ref):
        @pl.when(pl.program_id(0) == 0)
        def _():
            dwo_ref[...] = jnp.zeros_like(dwo_ref)
            dwg_ref[...] = jnp.zeros_like(dwg_ref)
            dg_ref[...] = jnp.zeros_like(dg_ref)

        n, r = _rms(mix_ref[...])
        dmix, dgs = _rms_bwd(n, r, g_ref[...], dx1_ref[...])
        dg_ref[...] += dgs
        dmb = dmix.astype(BF16)
        dcat = _dot_nt(dmb, wo_ref[...])
        dyr_ref[...] = dcat[:, :RET_WIDTH]
        dglu = dcat[:, RET_WIDTH:]
        ys = ys_ref[...]
        g0 = _gelu(ys).astype(BF16)
        z = _dot(g0, wg_ref[...])
        a = z[:, :SSM_WIDTH]
        sb = _sigmoid(z[:, SSM_WIDTH:])
        dwo_ref[:RET_WIDTH, :] += _dot_tn(yr_ref[...], dmb)
        dwo_ref[RET_WIDTH:, :] += _dot_tn((a * sb).astype(BF16), dmb)
        dz = jnp.concatenate([dglu * sb, dglu * a * sb * (1.0 - sb)], axis=1).astype(BF16)
        dwg_ref[...] += _dot_tn(g0, dz)
        dys_ref[...] = _dot_nt(dz, wg_ref[...]) * _gelu_grad(ys)

    half = jax.ShapeDtypeStruct((seq, RET_WIDTH), F32)
    return pl.pallas_call(
        body, name="mix_bwd", grid=(seq // t,),
        in_specs=[_rows(t, D_MODEL), _rows(t, D_MODEL), _full((1, D_MODEL)), _rows(t, SSM_WIDTH), _rows(t, RET_WIDTH),
                  _full((SSM_WIDTH, 2 * SSM_WIDTH)), _full((D_MODEL, D_MODEL))],
        out_specs=[_rows(t, RET_WIDTH), _rows(t, SSM_WIDTH), _full((D_MODEL, D_MODEL)), _full((SSM_WIDTH, 2 * SSM_WIDTH)),
                   _full((1, D_MODEL))],
        out_shape=[half, half, jax.ShapeDtypeStruct((D_MODEL, D_MODEL), F32),
                   jax.ShapeDtypeStruct((SSM_WIDTH, 2 * SSM_WIDTH), F32), jax.ShapeDtypeStruct((1, D_MODEL), F32)],
        compiler_params=_params("arbitrary"),
    )(dx1, mix, gain, y_ssm, y_ret, w_glu, w_out)


def _mlp_a(x1, target, gain_pre, gain_post, w1, w2):
    seq = x1.shape[0]
    t = MLP_TILE

    def body(x1_ref, tg_ref, gp_ref, gq_ref, w1_ref, w2_ref, df_ref, dx2_ref, dw2_ref, dgq_ref, sq_ref):
        @pl.when(pl.program_id(0) == 0)
        def _():
            dw2_ref[...] = jnp.zeros_like(dw2_ref)
            dgq_ref[...] = jnp.zeros_like(dgq_ref)
            sq_ref[...] = jnp.zeros_like(sq_ref)

        x1v = x1_ref[...]
        n3, _ = _rms(x1v)
        h = (n3 * gp_ref[...]).astype(BF16)
        rl = jnp.maximum(_dot(h, w1_ref[...]), 0.0)
        act = (rl * rl).astype(BF16)
        n4, r4 = _rms(_dot(act, w2_ref[...]))
        gq = gq_ref[...]
        err = x1v + n4 * gq - tg_ref[...]
        sq_ref[...] += jnp.sum(err * err, axis=0, keepdims=True)
        dx2 = err * (1.0 / D_MODEL)
        dx2_ref[...] = dx2
        dm, dgs = _rms_bwd(n4, r4, gq, dx2)
        dgq_ref[...] += dgs
        dmb = dm.astype(BF16)
        dw2_ref[...] += _dot_tn(act, dmb)
        df_ref[...] = (_dot_nt(dmb, w2_ref[...]) * (2.0 * rl)).astype(BF16)

    return pl.pallas_call(
        body, name="mlp_a", grid=(seq // t,),
        in_specs=[_rows(t, D_MODEL), _rows(t, D_MODEL), _full((1, D_MODEL)), _full((1, D_MODEL)),
                  _full((D_MODEL, D_FF)), _full((D_FF, D_MODEL))],
        out_specs=[_rows(t, D_FF), _rows(t, D_MODEL), _full((D_FF, D_MODEL)), _full((1, D_MODEL)), _full((1, D_MODEL))],
        out_shape=[jax.ShapeDtypeStruct((seq, D_FF), BF16), jax.ShapeDtypeStruct((seq, D_MODEL), F32),
                   jax.ShapeDtypeStruct((D_FF, D_MODEL), F32), jax.ShapeDtypeStruct((1, D_MODEL), F32),
                   jax.ShapeDtypeStruct((1, D_MODEL), F32)],
        compiler_params=_params("arbitrary"),
    )(x1, target, gain_pre, gain_post, w1, w2)


def _mlp_b(df, dx2, x1, gain_pre, w1):
    seq = x1.shape[0]
    t = MLP_TILE

    def body(df_ref, dx2_ref, x1_ref, gp_ref, w1_ref, dx1_ref, dw1_ref, dgp_ref):
        @pl.when(pl.program_id(0) == 0)
        def _():
            dw1_ref[...] = jnp.zeros_like(dw1_ref)
            dgp_ref[...] = jnp.zeros_like(dgp_ref)

        n3, r3 = _rms(x1_ref[...])
        gp = gp_ref[...]
        h = (n3 * gp).astype(BF16)
        dfv = df_ref[...]
        dw1_ref[...] += _dot_tn(h, dfv)
        dx, dgs = _rms_bwd(n3, r3, gp, _dot_nt(dfv, w1_ref[...]))
        dgp_ref[...] += dgs
        dx1_ref[...] = dx2_ref[...] + dx

    return pl.pallas_call(
        body, name="mlp_b", grid=(seq // t,),
        in_specs=[_rows(t, D_FF), _rows(t, D_MODEL), _rows(t, D_MODEL), _full((1, D_MODEL)), _full((D_MODEL, D_FF))],
        out_specs=[_rows(t, D_MODEL), _full((D_MODEL, D_FF)), _full((1, D_MODEL))],
        out_shape=[jax.ShapeDtypeStruct((seq, D_MODEL), F32), jax.ShapeDtypeStruct((D_MODEL, D_FF), F32),
                   jax.ShapeDtypeStruct((1, D_MODEL), F32)],
        compiler_params=_params("arbitrary"),
    )(df, dx2, x1, gain_pre, w1)


def _local_step(x, target, small, w_in, w_glu, w_out, w_ff1, w_ff2):
    seq = x.shape[0]
    half = HEAD_DIM // 2
    inv_freq = ROPE_BASE ** (-jnp.arange(half, dtype=F32) / half)
    ang = jnp.arange(seq, dtype=F32)[:, None] * inv_freq[None, :]
    cos, sin = jnp.cos(ang), jnp.sin(ang)
    cos2 = jnp.concatenate([cos, cos], axis=1)
    sin2 = jnp.concatenate([-sin, sin], axis=1)
    ret_consts = _ret_consts()

    s5_names = ("ssm_lambda_re", "ssm_lambda_im", "ssm_log_dt", "ssm_b_re", "ssm_b_im", "ssm_c_re", "ssm_c_im")
    (lbr, lbi, b_tiles, c_tiles), disc_vjp = jax.vjp(_s5_discretise, *[small[name] for name in s5_names])
    tables = _s5_tables(lbr, lbi)
    b_blk = b_tiles.astype(BF16)
    c_blk = c_tiles.astype(BF16)
    d_skip = small["ssm_d"].reshape(1, SSM_WIDTH)
    gn_gain = small["ret_gn_gain"].reshape(1, RET_WIDTH)
    g_mix_pre = small["norm_mix_pre"].reshape(1, D_MODEL)
    g_mix_post = small["norm_mix_post"].reshape(1, D_MODEL)
    g_mlp_pre = small["norm_mlp_pre"].reshape(1, D_MODEL)
    g_mlp_post = small["norm_mlp_post"].reshape(1, D_MODEL)

    q, k, v, gate, u = _inproj_fwd(x, g_mix_pre, w_in, cos2, sin2)
    y_ret, rprev = _ret_fwd(q, k, v, gate, gn_gain, ret_consts)
    y_ssm, starts = _s5_fwd(u, b_blk, c_blk, tables, d_skip)
    x1, mix = _mix_fwd(y_ssm, y_ret, x, g_mix_post, w_glu, w_out)
    df, dx2, d_ff2, dg_mlp_post, sq = _mlp_a(x1, target, g_mlp_pre, g_mlp_post, w_ff1, w_ff2)
    dx1, d_ff1, dg_mlp_pre = _mlp_b(df, dx2, x1, g_mlp_pre, w_ff1)
    dy_ret, dy_ssm, d_out, d_glu, dg_mix_post = _mix_bwd(dx1, mix, g_mix_post, y_ssm, y_ret, w_glu, w_out)
    du, dd, dlam8, db_tiles, dc_tiles = _s5_bwd(u, dy_ssm, b_blk, c_blk, tables, d_skip, starts)
    dq, dk, dv, dgate, dgn = _ret_bwd(q, k, v, gate, gn_gain, ret_consts, rprev, dy_ret)
    dx, d_in, dg_mix_pre = _inproj_bwd(x, g_mix_pre, w_in, cos2, sin2, dq, dk, dv, dgate, du, dx1)

    dlam = jnp.sum(dlam8, axis=1)
    s5_grads = disc_vjp((dlam[0].reshape(SSM_GROUPS, SSM_STATE), dlam[1].reshape(SSM_GROUPS, SSM_STATE), db_tiles, dc_tiles))
    small_grads = {
        "norm_mix_pre": dg_mix_pre, "norm_mix_post": dg_mix_post, "ret_gn_gain": dgn, "ssm_d": dd,
        "norm_mlp_pre": dg_mlp_pre, "norm_mlp_post": dg_mlp_post, "loss_sum": 0.5 * jnp.sum(sq) / D_MODEL,
    }
    small_grads.update(dict(zip(s5_names, s5_grads)))
    return dx, (d_in, d_glu, d_out, d_ff1, d_ff2), small_grads


BIG = (
    ("w_in", D_MODEL, IN_COLS, 1),
    ("w_glu", SSM_WIDTH, 2 * SSM_WIDTH, 1),
    ("w_out", D_MODEL, D_MODEL, 0),
    ("w_ff1", D_MODEL, D_FF, 1),
    ("w_ff2", D_FF, D_MODEL, 0),
)
SMALL = (
    ("norm_mix_pre", (D_MODEL,)), ("norm_mix_post", (D_MODEL,)), ("ret_gn_gain", (RET_WIDTH,)),
    ("ssm_lambda_re", (SSM_GROUPS, SSM_STATE)), ("ssm_lambda_im", (SSM_GROUPS, SSM_STATE)), ("ssm_log_dt", (SSM_GROUPS,)),
    ("ssm_b_re", (SSM_GROUPS, SSM_STATE, SSM_GROUP)), ("ssm_b_im", (SSM_GROUPS, SSM_STATE, SSM_GROUP)),
    ("ssm_c_re", (SSM_GROUPS, SSM_GROUP, SSM_STATE)), ("ssm_c_im", (SSM_GROUPS, SSM_GROUP, SSM_STATE)),
    ("ssm_d", (SSM_WIDTH,)), ("norm_mlp_pre", (D_MODEL,)), ("norm_mlp_post", (D_MODEL,)),
)
SMALL_ROWS = 1152
EXCH_TILES = 8


PACKED = SMALL + (("loss_sum", (1,)),)


def _pack_small(tree):
    rows = []
    for name, shape in PACKED:
        size = math.prod(shape)
        flat = tree[name].reshape(-1).astype(F32) if name in tree else jnp.zeros((size,), F32)
        n = -(-size // LANES) * LANES
        rows.append(jnp.pad(flat, (0, n - size)).reshape(-1, LANES))
    used = sum(r.shape[0] for r in rows)
    rows.append(jnp.zeros((SMALL_ROWS - used, LANES), F32))
    return jnp.concatenate(rows, axis=0)


def _unpack_small(packed):
    out, row = {}, 0
    for name, shape in PACKED:
        size = math.prod(shape)
        n = -(-size // LANES)
        out[name] = packed[row:row + n].reshape(-1)[:size].reshape((1,) + shape)
        row += n
    return out


def _half_shape(r, c, axis):
    return (r // 2, c) if axis == 1 else (r, c // 2)


def _region_shape(r, c, axis):
    return (r // 2, c // N_CHIPS) if axis == 1 else (r // N_CHIPS, c // 2)


def _shard_shape(r, c, axis):
    return (r, c // N_CHIPS) if axis == 1 else (r // N_CHIPS, c)


def _ds(start, size):
    return pl.ds(pl.multiple_of(start * size, size), size)


def _region_of_full(ref, r, c, axis, shard, half):
    if axis == 1:
        return ref.at[_ds(half, r // 2), _ds(shard, c // N_CHIPS)]
    return ref.at[_ds(shard, r // N_CHIPS), _ds(half, c // 2)]


def _shard_of_full(ref, r, c, axis, shard):
    if axis == 1:
        return ref.at[:, _ds(shard, c // N_CHIPS)]
    return ref.at[_ds(shard, r // N_CHIPS), :]


def _half_of_full(ref, r, c, axis, half):
    if axis == 1:
        return ref.at[_ds(half, r // 2), :]
    return ref.at[:, _ds(half, c // 2)]


def _half_of_shard(ref, r, c, axis, half):
    if axis == 1:
        return ref.at[_ds(half, r // 2), :]
    return ref.at[:, _ds(half, c // 2)]


def _region_of_half(ref, r, c, axis, shard):
    if axis == 1:
        return ref.at[:, _ds(shard, c // N_CHIPS)]
    return ref.at[_ds(shard, r // N_CHIPS), :]


def _place():
    x, y, c = lax.axis_index("x"), lax.axis_index("y"), lax.axis_index("c")
    chips = [(1 - x, y), (x, 1 - y), (1 - x, 1 - y)]
    return x, y, c, chips


ANY = pl.BlockSpec(memory_space=pl.ANY)


def _remote(src, dst, send_sem, recv_sem, to):
    return pltpu.make_async_remote_copy(src_ref=src, dst_ref=dst, send_sem=send_sem, recv_sem=recv_sem,
                                        device_id=to, device_id_type=MESH)


def _gather_weights(shards):
    nw = len(BIG)

    def body(*refs):
        shard_refs, full_refs = refs[:nw], refs[nw:2 * nw]
        send_sems, recv_sems, pass_send, pass_recv, local_sems = refs[2 * nw:]
        x, y, c, chips = _place()
        mine = 2 * x + y
        local = []
        for w, (_, r, cc, axis) in enumerate(BIG):
            cp = pltpu.make_async_copy(shard_refs[w], _shard_of_full(full_refs[w], r, cc, axis, mine), local_sems.at[w])
            cp.start()
            local.append(cp)
        sent = []
        for w, (_, r, cc, axis) in enumerate(BIG):
            for k, (cx, cy) in enumerate(chips):
                cp = _remote(_half_of_shard(shard_refs[w], r, cc, axis, c), _region_of_full(full_refs[w], r, cc, axis, mine, c),
                             send_sems.at[3 * w + k], recv_sems.at[3 * w + k], (cx, cy, c))
                cp.start()
                sent.append(cp)
        passed = []
        for w, (_, r, cc, axis) in enumerate(BIG):
            for k, (cx, cy) in enumerate(chips):
                landed = _region_of_full(full_refs[w], r, cc, axis, 2 * cx + cy, c)
                _remote(landed, landed, send_sems.at[3 * w + k], recv_sems.at[3 * w + k], (cx, cy, c)).wait_recv()
                cp = _remote(landed, landed, pass_send.at[3 * w + k], pass_recv.at[3 * w + k], (x, y, 1 - c))
                cp.start()
                passed.append(cp)
        for w, (_, r, cc, axis) in enumerate(BIG):
            for k, (cx, cy) in enumerate(chips):
                other = _region_of_full(full_refs[w], r, cc, axis, 2 * cx + cy, 1 - c)
                _remote(other, other, pass_send.at[3 * w + k], pass_recv.at[3 * w + k], (x, y, 1 - c)).wait_recv()
        for cp in sent + passed:
            cp.wait_send()
        for cp in local:
            cp.wait()

    n = 3 * nw
    return pl.pallas_call(
        body, name="gather_weights",
        in_specs=[ANY] * nw, out_specs=[ANY] * nw,
        out_shape=[jax.ShapeDtypeStruct((r, cc), BF16) for _, r, cc, _ in BIG],
        scratch_shapes=[pltpu.SemaphoreType.DMA((n,)), pltpu.SemaphoreType.DMA((n,)), pltpu.SemaphoreType.DMA((n,)),
                        pltpu.SemaphoreType.DMA((n,)), pltpu.SemaphoreType.DMA((nw,))],
    )(*shards)


def _pair_exchange(grads, small):
    nw = len(BIG)

    def body(*refs):
        g_refs, got_refs = refs[:nw + 1], refs[nw + 1:2 * nw + 2]
        send_sems, recv_sems = refs[2 * nw + 2:]
        x, y, c, _ = _place()
        copies = []
        for w, (_, r, cc, axis) in enumerate(BIG):
            copies.append(_remote(_half_of_full(g_refs[w], r, cc, axis, 1 - c), got_refs[w], send_sems.at[w], recv_sems.at[w],
                                  (x, y, 1 - c)))
        copies.append(_remote(g_refs[nw].at[_ds(1 - c, SMALL_ROWS // 2), :], got_refs[nw], send_sems.at[nw], recv_sems.at[nw],
                              (x, y, 1 - c)))
        for cp in copies:
            cp.start()
        for cp in copies:
            cp.wait()

    return pl.pallas_call(
        body, name="pair_exchange",
        in_specs=[ANY] * (nw + 1), out_specs=[ANY] * (nw + 1),
        out_shape=[jax.ShapeDtypeStruct(_half_shape(r, cc, axis), F32) for _, r, cc, axis in BIG]
        + [jax.ShapeDtypeStruct((SMALL_ROWS // 2, LANES), F32)],
        scratch_shapes=[pltpu.SemaphoreType.DMA((nw + 1,)), pltpu.SemaphoreType.DMA((nw + 1,))],
    )(*grads, small)


def _tile_rows(shape):
    return (shape[0] // EXCH_TILES, shape[1])


def _pair_sum(ids, grads, small, got):
    nw = len(BIG)
    halves = [_half_shape(r, cc, axis) for _, r, cc, axis in BIG] + [(SMALL_ROWS // 2, LANES)]

    def body(ids_ref, *refs):
        n = nw + 1
        for j in range(n):
            total = refs[j][...] + refs[n + j][...]
            refs[2 * n + j][...] = total
            if j < nw:
                refs[3 * n + j][...] = total.astype(BF16)

    def mine_spec(j):
        tile = _tile_rows(halves[j])
        if j < nw and BIG[j][3] == 0:
            return pl.BlockSpec(tile, lambda i, ids: (i, ids[0]))
        return pl.BlockSpec(tile, lambda i, ids: (ids[0] * EXCH_TILES + i, 0))

    plain = lambda j: pl.BlockSpec(_tile_rows(halves[j]), lambda i, ids: (i, 0))
    n = nw + 1
    return pl.pallas_call(
        body, name="pair_sum",
        grid_spec=pltpu.PrefetchScalarGridSpec(
            num_scalar_prefetch=1, grid=(EXCH_TILES,),
            in_specs=[mine_spec(j) for j in range(n)] + [plain(j) for j in range(n)],
            out_specs=[plain(j) for j in range(n)] + [plain(j) for j in range(nw)]),
        out_shape=[jax.ShapeDtypeStruct(h, F32) for h in halves] + [jax.ShapeDtypeStruct(h, BF16) for h in halves[:nw]],
        compiler_params=pltpu.CompilerParams(dimension_semantics=("parallel",)),
    )(ids, *grads, small, *got)


def _chip_exchange(sums_bf16, small_sum):
    nw = len(BIG)

    def body(*refs):
        src_refs, got_refs = refs[:nw + 1], refs[nw + 1:2 * nw + 2]
        send_sems, recv_sems = refs[2 * nw + 2:]
        x, y, c, chips = _place()
        copies = []
        for w, (_, r, cc, axis) in enumerate(BIG):
            for k, (cx, cy) in enumerate(chips):
                copies.append(_remote(_region_of_half(src_refs[w], r, cc, axis, 2 * cx + cy), got_refs[w].at[k],
                                      send_sems.at[3 * w + k], recv_sems.at[3 * w + k], (cx, cy, c)))
        for k, (cx, cy) in enumerate(chips):
            copies.append(_remote(src_refs[nw], got_refs[nw].at[k], send_sems.at[3 * nw + k], recv_sems.at[3 * nw + k], (cx, cy, c)))
        for cp in copies:
            cp.start()
        for cp in copies:
            cp.wait()

    n = 3 * (nw + 1)
    return pl.pallas_call(
        body, name="chip_exchange",
        in_specs=[ANY] * (nw + 1), out_specs=[ANY] * (nw + 1),
        out_shape=[jax.ShapeDtypeStruct((3,) + _region_shape(r, cc, axis), BF16) for _, r, cc, axis in BIG]
        + [jax.ShapeDtypeStruct((3, SMALL_ROWS // 2, LANES), F32)],
        scratch_shapes=[pltpu.SemaphoreType.DMA((n,)), pltpu.SemaphoreType.DMA((n,))],
    )(*sums_bf16, small_sum)


def _chip_sum(ids, sums, got):
    nw = len(BIG)
    regions = [_region_shape(r, cc, axis) for _, r, cc, axis in BIG] + [(SMALL_ROWS // 2, LANES)]

    def body(ids_ref, *refs):
        n = nw + 1
        for j in range(n):
            own, arrived, out = refs[j], refs[n + j], refs[2 * n + j]
            if j < nw:
                out[...] = ((own[...] + arrived[0].astype(F32)) + arrived[1].astype(F32)) + arrived[2].astype(F32)
            else:
                out[...] = (own[...] + arrived[1]) + (arrived[0] + arrived[2])

    def own_spec(j):
        tile = _tile_rows(regions[j])
        if j == nw:
            return pl.BlockSpec(tile, lambda i, ids: (i, 0))
        if BIG[j][3] == 1:
            return pl.BlockSpec(tile, lambda i, ids: (i, ids[1]))
        return pl.BlockSpec(tile, lambda i, ids: (ids[1] * EXCH_TILES + i, 0))

    arrived_spec = lambda j: pl.BlockSpec((3,) + _tile_rows(regions[j]), lambda i, ids: (0, i, 0))
    plain = lambda j: pl.BlockSpec(_tile_rows(regions[j]), lambda i, ids: (i, 0))
    n = nw + 1
    return pl.pallas_call(
        body, name="chip_sum",
        grid_spec=pltpu.PrefetchScalarGridSpec(
            num_scalar_prefetch=1, grid=(EXCH_TILES,),
            in_specs=[own_spec(j) for j in range(n)] + [arrived_spec(j) for j in range(n)],
            out_specs=[plain(j) for j in range(n)]),
        out_shape=[jax.ShapeDtypeStruct(rg, F32) for rg in regions],
        compiler_params=pltpu.CompilerParams(dimension_semantics=("parallel",)),
    )(ids, *sums, *got)


def _pair_share(parts):
    nw = len(BIG)

    def body(*refs):
        part_refs, whole_refs = refs[:nw + 1], refs[nw + 1:2 * nw + 2]
        send_sems, recv_sems, local_sems = refs[2 * nw + 2:]
        x, y, c, _ = _place()

        def half_of_whole(w, half):
            if w == nw:
                return whole_refs[w].at[_ds(half, SMALL_ROWS // 2), :]
            _, r, cc, axis = BIG[w]
            return _half_of_shard(whole_refs[w], r, cc, axis, half)

        local, remote = [], []
        for w in range(nw + 1):
            lc = pltpu.make_async_copy(part_refs[w], half_of_whole(w, c), local_sems.at[w])
            lc.start()
            local.append(lc)
            rc = _remote(part_refs[w], half_of_whole(w, c), send_sems.at[w], recv_sems.at[w], (x, y, 1 - c))
            rc.start()
            remote.append(rc)
        for w in range(nw + 1):
            _remote(part_refs[w], half_of_whole(w, 1 - c), send_sems.at[w], recv_sems.at[w], (x, y, 1 - c)).wait_recv()
        for cp in remote:
            cp.wait_send()
        for cp in local:
            cp.wait()

    return pl.pallas_call(
        body, name="pair_share",
        in_specs=[ANY] * (nw + 1), out_specs=[ANY] * (nw + 1),
        out_shape=[jax.ShapeDtypeStruct(_shard_shape(r, cc, axis), F32) for _, r, cc, axis in BIG]
        + [jax.ShapeDtypeStruct((SMALL_ROWS, LANES), F32)],
        scratch_shapes=[pltpu.SemaphoreType.DMA((nw + 1,)), pltpu.SemaphoreType.DMA((nw + 1,)),
                        pltpu.SemaphoreType.DMA((nw + 1,))],
    )(*parts)


def _adamw(ws, gs, ms, vs):
    n = len(ws)
    c1 = 1.0 - ADAM_B1 ** ADAM_STEP
    c2 = 1.0 - ADAM_B2 ** ADAM_STEP

    def body(*refs):
        for j in range(n):
            w, g, m, v = (refs[k * n + j][...] for k in range(4))
            m = ADAM_B1 * m + (1.0 - ADAM_B1) * g
            v = ADAM_B2 * v + (1.0 - ADAM_B2) * (g * g)
            refs[4 * n + j][...] = -ADAM_LR * ((m / c1) / (jnp.sqrt(v / c2) + ADAM_EPS) + ADAM_WD * w)
            refs[5 * n + j][...] = m
            refs[6 * n + j][...] = v

    specs = [pl.BlockSpec(_tile_rows(w.shape), lambda i: (i, 0)) for w in ws]
    shapes = [jax.ShapeDtypeStruct(w.shape, F32) for w in ws]
    out = pl.pallas_call(
        body, name="adamw", grid=(EXCH_TILES,),
        in_specs=specs * 4, out_specs=specs * 3, out_shape=shapes * 3,
        compiler_params=pltpu.CompilerParams(dimension_semantics=("parallel",)),
    )(*ws, *gs, *ms, *vs)
    return out[:n], out[n:2 * n], out[2 * n:]


def kernel(x, norm_mix_pre, norm_mix_post, w_in, ret_gn_gain, ssm_lambda_re, ssm_lambda_im, ssm_log_dt, ssm_b_re, ssm_b_im, ssm_c_re, ssm_c_im, ssm_d, w_glu, w_out, norm_mlp_pre, norm_mlp_post, w_ff1, w_ff2, loss_target, m_norm_mix_pre, m_norm_mix_post, m_w_in, m_ret_gn_gain, m_ssm_lambda_re, m_ssm_lambda_im, m_ssm_log_dt, m_ssm_b_re, m_ssm_b_im, m_ssm_c_re, m_ssm_c_im, m_ssm_d, m_w_glu, m_w_out, m_norm_mlp_pre, m_norm_mlp_post, m_w_ff1, m_w_ff2, v_norm_mix_pre, v_norm_mix_post, v_w_in, v_ret_gn_gain, v_ssm_lambda_re, v_ssm_lambda_im, v_ssm_log_dt, v_ssm_b_re, v_ssm_b_im, v_ssm_c_re, v_ssm_c_im, v_ssm_d, v_w_glu, v_w_out, v_norm_mlp_pre, v_norm_mlp_post, v_w_ff1, v_w_ff2):
    given = dict(locals())
    order = ["norm_mix_pre", "norm_mix_post", "w_in", "ret_gn_gain", "ssm_lambda_re", "ssm_lambda_im", "ssm_log_dt",
             "ssm_b_re", "ssm_b_im", "ssm_c_re", "ssm_c_im", "ssm_d", "w_glu", "w_out", "norm_mlp_pre", "norm_mlp_post",
             "w_ff1", "w_ff2"]
    big_names = [name for name, _, _, _ in BIG]
    ids = jnp.stack([lax.axis_index("c"), 2 * lax.axis_index("x") + lax.axis_index("y")]).astype(jnp.int32)

    whole = _gather_weights([given[name][0].astype(BF16) for name in big_names])
    small_w = {name: given[name][0] for name, _ in SMALL}
    dx, big_grads, small_grads = _local_step(x[0], loss_target[0], small_w, *whole)

    small_packed = _pack_small(small_grads)
    got = _pair_exchange(big_grads, small_packed)
    sums = _pair_sum(ids, big_grads, small_packed, got)
    nw = len(BIG)
    arrived = _chip_exchange(sums[nw + 1:], sums[nw])
    parts = _chip_sum(ids, sums[:nw + 1], arrived)
    reduced = _pair_share(parts)

    pack = lambda prefix: _pack_small({name: given[prefix + name][0] for name, _ in SMALL})
    ws = [given[name][0] for name in big_names] + [pack("")]
    ms = [given["m_" + name][0] for name in big_names] + [pack("m_")]
    vs = [given["v_" + name][0] for name in big_names] + [pack("v_")]
    deltas, new_m, new_v = _adamw(ws, list(reduced), ms, vs)

    def unpacked(arrays):
        tree = _unpack_small(arrays[nw])
        tree.update({name: arrays[j][None] for j, name in enumerate(big_names)})
        return [tree[name] for name in order]

    loss = _unpack_small(reduced[nw])["loss_sum"].reshape(())
    return (loss, dx[None], *unpacked(list(reduced)), *unpacked(deltas), *unpacked(new_m), *unpacked(new_v))
```

```python
import functools
import math

import jax
import jax.numpy as jnp
from jax import lax
from jax.experimental import pallas as pl
from jax.experimental.pallas import tpu as pltpu

F32 = jnp.float32
BF16 = jnp.bfloat16

D_MODEL = 1024
RET_WIDTH = 512
RET_HEADS = 4
HEAD_DIM = 128
RET_CHUNK = 128
ROPE_BASE = 10000.0
SSM_WIDTH = 512
SSM_GROUP = 16
SSM_GROUPS = 32
SSM_STATE = 64
N_STATES = SSM_GROUPS * SSM_STATE
D_FF = 4096
IN_COLS = 4 * RET_WIDTH + SSM_WIDTH
NORM_EPS = 1e-6
K_SCALE = HEAD_DIM ** -0.5

ADAM_LR = 0.001
ADAM_B1 = 0.9
ADAM_B2 = 0.999
ADAM_EPS = 1e-08
ADAM_WD = 0.01
ADAM_STEP = 10

LANES = 128
SUBLANES = 8
VMEM_LIMIT = 56 * 2 ** 20

TOK_TILE = 512
MLP_TILE = 256
RET_CHUNKS_PER_STEP = 4
S5_TILE = 256
S5_STEPS = S5_TILE // SUBLANES
S5_COLBLK = N_STATES // LANES
S5_GROUP = 4

N_CHIPS = 4
MESH = pl.DeviceIdType.MESH


def _dot(a, b):
    return jnp.dot(a, b, preferred_element_type=F32)


def _dot_nt(a, b):
    return lax.dot_general(a, b, (((1,), (1,)), ((), ())), preferred_element_type=F32)


def _dot_tn(a, b):
    return lax.dot_general(a, b, (((0,), (0,)), ((), ())), preferred_element_type=F32)


def _sigmoid(x):
    return 1.0 / (1.0 + jnp.exp(-x))


_GELU_C = math.sqrt(2.0 / math.pi)


def _gelu(x):
    return 0.5 * x * (1.0 + jnp.tanh(_GELU_C * (x + 0.044715 * (x * x * x))))


def _gelu_grad(x):
    t = jnp.tanh(_GELU_C * (x + 0.044715 * (x * x * x)))
    return 0.5 * (1.0 + t) + 0.5 * x * (1.0 - t * t) * (_GELU_C * (1.0 + 3.0 * 0.044715 * (x * x)))


def _rms(x):
    r = lax.rsqrt(jnp.mean(x * x, axis=-1, keepdims=True) + NORM_EPS)
    return x * r, r


def _rms_bwd(n, r, gain, dy):
    dn = dy * gain
    dx = r * (dn - n * jnp.mean(dn * n, axis=-1, keepdims=True))
    return dx, jnp.sum(dy * n, axis=0, keepdims=True)


def _full(shape):
    nd = len(shape)
    return pl.BlockSpec(shape, lambda i, _n=nd: (0,) * _n, pipeline_mode=pl.Buffered(1))


def _rows(tile, width):
    return pl.BlockSpec((tile, width), lambda i: (i, 0))


def _rows_rev(tile, width, n):
    return pl.BlockSpec((tile, width), lambda i, _n=n: (_n - 1 - i, 0))


def _params(sem):
    return pltpu.CompilerParams(dimension_semantics=(sem,), vmem_limit_bytes=VMEM_LIMIT)


ANY = pl.BlockSpec(memory_space=pl.ANY)


class _Exchange:
    def __init__(self, operands, out_shapes, aliases, n, copies):
        self.operands, self.out_shapes, self.aliases, self.n, self.copies = list(operands), list(out_shapes), aliases, n, copies


def _call(body, name, grid, in_specs, out_specs, out_shape, scratch_shapes, args, exchange=None):
    if exchange is None:
        outs = pl.pallas_call(body, name=name, grid=grid, in_specs=in_specs, out_specs=out_specs, out_shape=out_shape,
                              scratch_shapes=scratch_shapes, compiler_params=_params("arbitrary"))(*args)
        return list(outs), []
    n_in, n_out, n_scr = len(in_specs), len(out_specs), len(scratch_shapes)
    k_in, k_out = len(exchange.operands), len(exchange.out_shapes)
    last = grid[0] - 1

    def hosted(*refs):
        own_in, rest = refs[:n_in], refs[n_in:]
        ex_in, rest = rest[:k_in], rest[k_in:]
        own_out, rest = rest[:n_out], rest[n_out:]
        ex_out, rest = rest[:k_out], rest[k_out:]
        own_scr, (send_sems, recv_sems) = rest[:n_scr], rest[n_scr:]

        @pl.when(pl.program_id(0) == 0)
        def _():
            for cp in exchange.copies(ex_in, ex_out, send_sems, recv_sems):
                cp.start()

        body(*own_in, *own_out, *own_scr)

        @pl.when(pl.program_id(0) == last)
        def _():
            for cp in exchange.copies(ex_in, ex_out, send_sems, recv_sems):
                cp.wait()

    outs = pl.pallas_call(
        hosted, name=name, grid=grid, in_specs=list(in_specs) + [ANY] * k_in, out_specs=list(out_specs) + [ANY] * k_out,
        out_shape=list(out_shape) + exchange.out_shapes,
        scratch_shapes=list(scratch_shapes) + [pltpu.SemaphoreType.DMA((exchange.n,)), pltpu.SemaphoreType.DMA((exchange.n,))],
        input_output_aliases={n_in + i: n_out + o for i, o in exchange.aliases.items()},
        compiler_params=_params("arbitrary"),
    )(*args, *exchange.operands)
    return list(outs[:n_out]), list(outs[n_out:])


def _rope(t, cos2, sin2):
    return t * cos2 + pltpu.roll(t, HEAD_DIM // 2, 1) * sin2


def _rope_bwd(d, cos2, sin2):
    return d * cos2 + pltpu.roll(d * sin2, HEAD_DIM // 2, 1)


def _inproj_fwd(x, gain, w_in, cos2, sin2, exchange=None):
    seq = x.shape[0]

    def body(x_ref, g_ref, w_ref, c_ref, s_ref, q_ref, k_ref, v_ref, gate_ref, u_ref):
        n, _ = _rms(x_ref[...])
        h = (n * g_ref[...]).astype(BF16)
        proj = _dot(h, w_ref[...])
        c = c_ref[...]
        s = s_ref[...]
        for hd in range(RET_HEADS):
            lo = hd * HEAD_DIM
            q_ref[:, lo:lo + HEAD_DIM] = _rope(proj[:, lo:lo + HEAD_DIM], c, s).astype(BF16)
            kh = proj[:, RET_WIDTH + lo:RET_WIDTH + lo + HEAD_DIM]
            k_ref[:, lo:lo + HEAD_DIM] = (_rope(kh, c, s) * K_SCALE).astype(BF16)
        v_ref[...] = proj[:, 2 * RET_WIDTH:3 * RET_WIDTH].astype(BF16)
        gate_ref[...] = proj[:, 3 * RET_WIDTH:4 * RET_WIDTH]
        u_ref[...] = proj[:, 4 * RET_WIDTH:]

    t = TOK_TILE
    half = lambda dt: jax.ShapeDtypeStruct((seq, RET_WIDTH), dt)
    return _call(
        body, "inproj_fwd", (seq // t,),
        [_rows(t, D_MODEL), _full((1, D_MODEL)), _full((D_MODEL, IN_COLS)), _rows(t, HEAD_DIM), _rows(t, HEAD_DIM)],
        [_rows(t, RET_WIDTH)] * 5, [half(BF16), half(BF16), half(BF16), half(F32), half(F32)], [],
        (x, gain, w_in, cos2, sin2), exchange)


def _inproj_bwd(x, gain, w_in, cos2, sin2, dq, dk, dv, dgate, du, dres):
    seq = x.shape[0]
    t = TOK_TILE

    def body(x_ref, g_ref, w_ref, c_ref, s_ref, dq_ref, dk_ref, dv_ref, dgate_ref, du_ref, dres_ref,
             dx_ref, dw_ref, dg_ref, dproj):
        @pl.when(pl.program_id(0) == 0)
        def _():
            dw_ref[...] = jnp.zeros_like(dw_ref)
            dg_ref[...] = jnp.zeros_like(dg_ref)

        c = c_ref[...]
        s = s_ref[...]
        for hd in range(RET_HEADS):
            lo = hd * HEAD_DIM
            dproj[:, lo:lo + HEAD_DIM] = _rope_bwd(dq_ref[:, lo:lo + HEAD_DIM], c, s).astype(BF16)
            dproj[:, RET_WIDTH + lo:RET_WIDTH + lo + HEAD_DIM] = _rope_bwd(
                dk_ref[:, lo:lo + HEAD_DIM] * K_SCALE, c, s).astype(BF16)
        dproj[:, 2 * RET_WIDTH:3 * RET_WIDTH] = dv_ref[...].astype(BF16)
        dproj[:, 3 * RET_WIDTH:4 * RET_WIDTH] = dgate_ref[...].astype(BF16)
        dproj[:, 4 * RET_WIDTH:] = du_ref[...].astype(BF16)
        dp = dproj[...]
        n, r = _rms(x_ref[...])
        gain_v = g_ref[...]
        h = (n * gain_v).astype(BF16)
        dw_ref[...] += _dot_tn(h, dp)
        dh = _dot_nt(dp, w_ref[...])
        dx, dgs = _rms_bwd(n, r, gain_v, dh)
        dg_ref[...] += dgs
        dx_ref[...] = dres_ref[...] + dx

    return pl.pallas_call(
        body, name="inproj_bwd", grid=(seq // t,),
        in_specs=[_rows(t, D_MODEL), _full((1, D_MODEL)), _full((D_MODEL, IN_COLS)), _rows(t, HEAD_DIM), _rows(t, HEAD_DIM)]
        + [_rows(t, RET_WIDTH)] * 5 + [_rows(t, D_MODEL)],
        out_specs=[_rows(t, D_MODEL), _full((D_MODEL, IN_COLS)), _full((1, D_MODEL))],
        out_shape=[jax.ShapeDtypeStruct((seq, D_MODEL), F32), jax.ShapeDtypeStruct((D_MODEL, IN_COLS), F32),
                   jax.ShapeDtypeStruct((1, D_MODEL), F32)],
        scratch_shapes=[pltpu.VMEM((t, IN_COLS), BF16)],
        compiler_params=_params("arbitrary"),
    )(x, gain, w_in, cos2, sin2, dq, dk, dv, dgate, du, dres)


def _ret_consts():
    c = RET_CHUNK
    log_gamma = jnp.log(1.0 - jnp.exp(jnp.linspace(math.log(1.0 / 32), math.log(1.0 / 512), RET_HEADS))).astype(F32)
    idx = jnp.arange(c, dtype=F32)
    diff = idx[:, None] - idx[None, :]
    decay = jnp.where(diff[None] >= 0, jnp.exp(jnp.maximum(diff, 0.0)[None] * log_gamma[:, None, None]), 0.0)
    zeta = jnp.exp((c - 1 - idx)[None, :] * log_gamma[:, None])
    xi = jnp.exp((idx + 1.0)[None, :] * log_gamma[:, None])
    g_chunk = jnp.exp(c * log_gamma)
    wide = lambda rowvals: jnp.broadcast_to(rowvals[:, :, None], (RET_HEADS, c, c)).astype(F32)
    return decay.astype(F32), wide(xi), wide(zeta), jnp.broadcast_to(g_chunk[:, None, None], (RET_HEADS, c, c)).astype(F32)


def _bmm(a, b):
    return lax.dot_general(a, b, (((2,), (1,)), ((0,), (0,))), preferred_element_type=F32)


def _bmm_nt(a, b):
    return lax.dot_general(a, b, (((2,), (2,)), ((0,), (0,))), preferred_element_type=F32)


def _bmm_tn(a, b):
    return lax.dot_general(a, b, (((1,), (1,)), ((0,), (0,))), preferred_element_type=F32)


def _ret_blocks(ref):
    return jnp.stack([ref[cc * RET_CHUNK:(cc + 1) * RET_CHUNK, hd * HEAD_DIM:(hd + 1) * HEAD_DIM]
                      for cc in range(RET_CHUNKS_PER_STEP) for hd in range(RET_HEADS)])


def _ret_unblock(ref, blocks):
    for cc in range(RET_CHUNKS_PER_STEP):
        for hd in range(RET_HEADS):
            ref[cc * RET_CHUNK:(cc + 1) * RET_CHUNK, hd * HEAD_DIM:(hd + 1) * HEAD_DIM] = blocks[cc * RET_HEADS + hd].astype(ref.dtype)


def _per_chunk(ref):
    return jnp.concatenate([ref[...]] * RET_CHUNKS_PER_STEP, axis=0)


def _ret_gain(gain_ref):
    return jnp.stack([gain_ref[:, hd * HEAD_DIM:(hd + 1) * HEAD_DIM]
                      for _ in range(RET_CHUNKS_PER_STEP) for hd in range(RET_HEADS)])


def _ret_fwd(q, k, v, gate, gn_gain, consts, exchange=None):
    seq = q.shape[0]
    cps = RET_CHUNKS_PER_STEP
    t = cps * RET_CHUNK
    n_chunks = seq // RET_CHUNK
    dec, xi, zeta, gch = consts

    def body(q_ref, k_ref, v_ref, gate_ref, gain_ref, dec_ref, xi_ref, zeta_ref, gch_ref, y_ref, rprev_ref, state):
        @pl.when(pl.program_id(0) == 0)
        def _():
            state[...] = jnp.zeros_like(state)

        qb, kb, vb = _ret_blocks(q_ref), _ret_blocks(k_ref), _ret_blocks(v_ref)
        s = _bmm_nt(qb, kb) * _per_chunk(dec_ref)
        vz = (vb.astype(F32) * _per_chunk(zeta_ref)).astype(BF16)
        incr = _bmm_tn(kb, vz)
        gch_v = gch_ref[...]
        rp = state[...]
        before = []
        for cc in range(cps):
            before.append(rp.astype(BF16))
            rprev_ref[cc] = before[-1]
            rp = gch_v * rp + incr[cc * RET_HEADS:(cc + 1) * RET_HEADS]
        state[...] = rp
        qx = (qb.astype(F32) * _per_chunk(xi_ref)).astype(BF16)
        o = _bmm(s.astype(BF16), vb) + _bmm(qx, jnp.concatenate(before, axis=0))
        oc = o - jnp.mean(o, axis=-1, keepdims=True)
        on = oc * lax.rsqrt(jnp.mean(oc * oc, axis=-1, keepdims=True) + NORM_EPS)
        g = _ret_blocks(gate_ref)
        _ret_unblock(y_ref, g * _sigmoid(g) * (on * _ret_gain(gain_ref)))

    cst = _full((RET_HEADS, RET_CHUNK, RET_CHUNK))
    return _call(
        body, "ret_fwd", (seq // t,),
        [_rows(t, RET_WIDTH)] * 4 + [_full((1, RET_WIDTH)), cst, cst, cst, cst],
        [_rows(t, RET_WIDTH), pl.BlockSpec((cps, RET_HEADS, HEAD_DIM, HEAD_DIM), lambda i: (i, 0, 0, 0))],
        [jax.ShapeDtypeStruct((seq, RET_WIDTH), BF16), jax.ShapeDtypeStruct((n_chunks, RET_HEADS, HEAD_DIM, HEAD_DIM), BF16)],
        [pltpu.VMEM((RET_HEADS, HEAD_DIM, HEAD_DIM), F32)],
        (q, k, v, gate, gn_gain, dec, xi, zeta, gch), exchange)


def _ret_bwd(q, k, v, gate, gn_gain, consts, rprev, dy_ret, exchange=None):
    seq = q.shape[0]
    cps = RET_CHUNKS_PER_STEP
    t = cps * RET_CHUNK
    nt = seq // t
    dec, xi, zeta, gch = consts

    def body(q_ref, k_ref, v_ref, gate_ref, gain_ref, dec_ref, xi_ref, zeta_ref, gch_ref, rprev_ref, dyr_ref,
             dq_ref, dk_ref, dv_ref, dgate_ref, dgain_ref, dstate):
        @pl.when(pl.program_id(0) == 0)
        def _():
            dstate[...] = jnp.zeros_like(dstate)
            dgain_ref[...] = jnp.zeros_like(dgain_ref)

        nb = cps * RET_HEADS
        qb, kb, vb = _ret_blocks(q_ref), _ret_blocks(k_ref), _ret_blocks(v_ref)
        dec_b, xi_b, zeta_b = _per_chunk(dec_ref), _per_chunk(xi_ref), _per_chunk(zeta_ref)
        rpb = rprev_ref[...].reshape(nb, HEAD_DIM, HEAD_DIM)
        sdb = (_bmm_nt(qb, kb) * dec_b).astype(BF16)
        qx = (qb.astype(F32) * xi_b).astype(BF16)
        o = _bmm(sdb, vb) + _bmm(qx, rpb)
        oc = o - jnp.mean(o, axis=-1, keepdims=True)
        rstd = lax.rsqrt(jnp.mean(oc * oc, axis=-1, keepdims=True) + NORM_EPS)
        on = oc * rstd
        g = _ret_blocks(gate_ref)
        sg = _sigmoid(g)
        gain_b = _ret_gain(gain_ref)
        dyr = _ret_blocks(dyr_ref)
        _ret_unblock(dgate_ref, dyr * (on * gain_b) * (sg * (1.0 + g * (1.0 - sg))))
        dy = dyr * (g * sg)
        dgain = jnp.sum(dy * on, axis=1, keepdims=True)
        for hd in range(RET_HEADS):
            part = dgain[hd]
            for cc in range(1, cps):
                part = part + dgain[cc * RET_HEADS + hd]
            dgain_ref[:, hd * HEAD_DIM:(hd + 1) * HEAD_DIM] += part
        don = dy * gain_b
        do = rstd * (don - jnp.mean(don, axis=-1, keepdims=True) - on * jnp.mean(don * on, axis=-1, keepdims=True))
        dob = do.astype(BF16)
        ds = (_bmm_nt(dob, vb) * dec_b).astype(BF16)
        _ret_unblock(dq_ref, _bmm(ds, kb) + _bmm_nt(dob, rpb) * xi_b)
        dk = _bmm_tn(ds, qb)
        dv = _bmm_tn(sdb, dob)
        dstate_local = _bmm_tn(qx, dob)
        vz = (vb.astype(F32) * zeta_b).astype(BF16)
        gch_v = gch_ref[...]
        zeta_v = zeta_ref[...]
        gh = dstate[...]
        dk_extra, dv_extra = [None] * cps, [None] * cps
        for cc in reversed(range(cps)):
            sl = slice(cc * RET_HEADS, (cc + 1) * RET_HEADS)
            gb = gh.astype(BF16)
            dk_extra[cc] = _bmm_nt(vz[sl], gb)
            dv_extra[cc] = _bmm(kb[sl], gb) * zeta_v
            gh = dstate_local[sl] + gch_v * gh
        dstate[...] = gh
        _ret_unblock(dk_ref, dk + jnp.concatenate(dk_extra, axis=0))
        _ret_unblock(dv_ref, dv + jnp.concatenate(dv_extra, axis=0))

    cst = _full((RET_HEADS, RET_CHUNK, RET_CHUNK))
    rev = _rows_rev(t, RET_WIDTH, nt)
    act = jax.ShapeDtypeStruct((seq, RET_WIDTH), F32)
    return _call(
        body, "ret_bwd", (nt,),
        [rev] * 4 + [_full((1, RET_WIDTH)), cst, cst, cst, cst,
                     pl.BlockSpec((cps, RET_HEADS, HEAD_DIM, HEAD_DIM), lambda i: (nt - 1 - i, 0, 0, 0)), rev],
        [rev] * 4 + [_full((1, RET_WIDTH))], [act, act, act, act, jax.ShapeDtypeStruct((1, RET_WIDTH), F32)],
        [pltpu.VMEM((RET_HEADS, HEAD_DIM, HEAD_DIM), F32)],
        (q, k, v, gate, gn_gain, dec, xi, zeta, gch, rprev, dy_ret), exchange)


MXU_TILE = 256
S5_STATE_TILES = 2 * N_STATES // MXU_TILE
S5_CHANNELS_PER_TILE = MXU_TILE // SSM_STATE * SSM_GROUP
S5_TILES_SHAPE = (S5_STATE_TILES, LANES, MXU_TILE)


def _chan_block(kk):
    return ((kk % (S5_STATE_TILES // 2)) * S5_CHANNELS_PER_TILE) // LANES * LANES


def _s5_tiles(blocks_re, blocks_im):
    per = MXU_TILE // SSM_STATE
    half = S5_STATE_TILES // 2
    eye = jnp.eye(per, dtype=F32)
    in_half = (jnp.arange(half) % (LANES // S5_CHANNELS_PER_TILE))[:, None] == jnp.arange(LANES // S5_CHANNELS_PER_TILE)[None, :]
    out = []
    for blk in (blocks_re, blocks_im):
        t = blk.reshape(half, per, SSM_GROUP, 1, SSM_STATE) * eye[None, :, None, :, None]
        t = t.reshape(half, 1, S5_CHANNELS_PER_TILE, MXU_TILE)
        out.append(jnp.where(in_half[:, :, None, None], t, 0.0).reshape(half, LANES, MXU_TILE))
    return jnp.concatenate(out, axis=0)


def _s5_discretise(lam_re, lam_im, log_dt, b_re, b_im, c_re, c_im):
    lam = lax.complex(jnp.minimum(lam_re, -1e-4), lam_im)
    dt = jnp.exp(log_dt)[:, None]
    lam_bar = jnp.exp(lam * dt)
    b_bar = ((lam_bar - 1.0) / lam)[:, :, None] * lax.complex(b_re, b_im)
    b_tiles = _s5_tiles(jnp.swapaxes(jnp.real(b_bar), 1, 2), jnp.swapaxes(jnp.imag(b_bar), 1, 2))
    return jnp.real(lam_bar), jnp.imag(lam_bar), b_tiles, _s5_tiles(c_re, -c_im)


def _s5_tables(lbr, lbi):
    lr = lbr.reshape(1, N_STATES)
    li = lbi.reshape(1, N_STATES)
    pr, pi = [lr], [li]
    for _ in range(S5_STEPS - 1):
        pr, pi = pr + [pr[-1] * lr - pi[-1] * li], pi + [pr[-1] * li + pi[-1] * lr]
    bc = lambda a: jnp.broadcast_to(a, (SUBLANES, N_STATES))
    rep = lambda rows: jnp.concatenate([bc(r) for r in rows], axis=0)
    return bc(lr), bc(li), rep(pr), rep(pi)


def _cmul(ar, ai, br, bi):
    return ar * br - ai * bi, ar * bi + ai * br


def _seg(i):
    return pl.ds(pl.multiple_of(i * SUBLANES, SUBLANES), SUBLANES)


def _scan_order():
    r = jnp.arange(S5_TILE)
    token = (r % SUBLANES) * S5_STEPS + r // SUBLANES
    p = (token[:, None] == jnp.arange(S5_TILE)[None, :]).astype(BF16)
    return p, p.T


def _to_scan_order(p_ref, rows_bf16):
    return _dot(p_ref[...], rows_bf16).astype(BF16)


def _from_scan_order(pt_ref, rows):
    hi = rows.astype(BF16)
    rest = rows - hi.astype(F32)
    mid = rest.astype(BF16)
    lo = (rest - mid.astype(F32)).astype(BF16)
    pt = pt_ref[...]
    return _dot(pt, hi) + _dot(pt, mid) + _dot(pt, lo)


def _s5_channels_to_states(chan_b, m_ref, xs):
    for kk in range(S5_STATE_TILES):
        lo = _chan_block(kk)
        res = _dot(chan_b[:, lo:lo + LANES], m_ref[kk])
        xs[2 * kk] = res[:, :LANES]
        xs[2 * kk + 1] = res[:, LANES:]


def _s5_states_to_channels(sb_ref, m_ref):
    out = []
    for lo in range(0, SSM_WIDTH, LANES):
        acc = None
        for kk in range(S5_STATE_TILES):
            if _chan_block(kk) == lo:
                term = _dot_nt(sb_ref[:, kk * MXU_TILE:(kk + 1) * MXU_TILE], m_ref[kk])
                acc = term if acc is None else acc + term
        out.append(acc)
    return jnp.concatenate(out, axis=1)


def _s5_outer_acc(acc_ref, chan_b, states_ref):
    for kk in range(S5_STATE_TILES):
        lo = _chan_block(kk)
        acc_ref[kk] += _dot_tn(chan_b[:, lo:lo + LANES], states_ref[:, kk * MXU_TILE:(kk + 1) * MXU_TILE])


def _s5_scan_tile(xs, lr_ref, li_ref, pr_ref, pi_ref, carry_re, carry_im, conj, reverse):
    sgn = -1.0 if conj else 1.0
    seg_in_re, seg_in_im = [], []
    for grp in range(S5_COLBLK // S5_GROUP):
        blks = [grp * S5_GROUP + j for j in range(S5_GROUP)]
        lrs = [lr_ref[:, b * LANES:(b + 1) * LANES] for b in blks]
        lis = [sgn * li_ref[:, b * LANES:(b + 1) * LANES] for b in blks]

        def step(it, carry, blks=blks, lrs=lrs, lis=lis):
            i = (S5_STEPS - 1 - it) if reverse else it
            out_r, out_i = [], []
            for j, b in enumerate(blks):
                xr, xi = _cmul(lrs[j], lis[j], carry[j], carry[S5_GROUP + j])
                xr = xr + xs[b, _seg(i), :]
                xi = xi + xs[S5_COLBLK + b, _seg(i), :]
                xs[b, _seg(i), :] = xr
                xs[S5_COLBLK + b, _seg(i), :] = xi
                out_r.append(xr)
                out_i.append(xi)
            return tuple(out_r + out_i)

        zeros = tuple(jnp.zeros((SUBLANES, LANES), F32) for _ in range(2 * S5_GROUP))
        ends = lax.fori_loop(0, S5_STEPS, step, zeros, unroll=4)

        ins_r, ins_i = [], []
        for j, b in enumerate(blks):
            cols = slice(b * LANES, (b + 1) * LANES)
            pr = pr_ref[SUBLANES * S5_STEPS - 1:SUBLANES * S5_STEPS, cols]
            pi = sgn * pi_ref[SUBLANES * S5_STEPS - 1:SUBLANES * S5_STEPS, cols]
            cur_r = carry_re[:, cols]
            cur_i = carry_im[:, cols]
            row_id = lax.broadcasted_iota(jnp.int32, (SUBLANES, LANES), 0)
            in_r = jnp.zeros((SUBLANES, LANES), F32)
            in_i = jnp.zeros((SUBLANES, LANES), F32)
            order = range(SUBLANES - 1, -1, -1) if reverse else range(SUBLANES)
            for sgm in order:
                in_r = jnp.where(row_id == sgm, cur_r, in_r)
                in_i = jnp.where(row_id == sgm, cur_i, in_i)
                mr, mi = _cmul(pr, pi, cur_r, cur_i)
                cur_r = mr + ends[j][sgm:sgm + 1, :]
                cur_i = mi + ends[S5_GROUP + j][sgm:sgm + 1, :]
            carry_re[:, cols] = cur_r
            carry_im[:, cols] = cur_i
            ins_r.append(in_r)
            ins_i.append(in_i)

        def fix(it, c, blks=blks, ins_r=ins_r, ins_i=ins_i):
            pw = (S5_STEPS - 1 - it) if reverse else it
            for j, b in enumerate(blks):
                cols = slice(b * LANES, (b + 1) * LANES)
                prow = pl.ds(pl.multiple_of(pw * SUBLANES, SUBLANES), SUBLANES)
                fr, fi = _cmul(pr_ref[prow, cols], sgn * pi_ref[prow, cols], ins_r[j], ins_i[j])
                xs[b, _seg(it), :] = xs[b, _seg(it), :] + fr
                xs[S5_COLBLK + b, _seg(it), :] = xs[S5_COLBLK + b, _seg(it), :] + fi
            return c

        lax.fori_loop(0, S5_STEPS, fix, 0, unroll=4)
        seg_in_re += ins_r
        seg_in_im += ins_i
    return seg_in_re, seg_in_im


def _s5_pack(xs, dst):
    for b in range(2 * S5_COLBLK):
        dst[:, b * LANES:(b + 1) * LANES] = xs[b].astype(BF16)


def _s5_fwd(u, b_blk, c_blk, tables, d_skip, exchange=None):
    seq = u.shape[0]
    t = S5_TILE
    nt = seq // t
    lr8, li8, p_re, p_im = tables

    p, pt = _scan_order()

    def body(u_ref, p_ref, pt_ref, b_ref, c_ref, lr_ref, li_ref, pr_ref, pi_ref, d_ref, y_ref, start_ref,
             xs, xb, carry_re, carry_im):
        @pl.when(pl.program_id(0) == 0)
        def _():
            carry_re[...] = jnp.zeros_like(carry_re)
            carry_im[...] = jnp.zeros_like(carry_im)

        start_ref[0, 0:1, :] = carry_re[...]
        start_ref[0, 1:2, :] = carry_im[...]
        uv = u_ref[...]
        _s5_channels_to_states(_to_scan_order(p_ref, uv.astype(BF16)), b_ref, xs)
        _s5_scan_tile(xs, lr_ref, li_ref, pr_ref, pi_ref, carry_re, carry_im, conj=False, reverse=False)
        _s5_pack(xs, xb)
        y_ref[...] = _from_scan_order(pt_ref, _s5_states_to_channels(xb, c_ref)) + d_ref[...] * uv

    return _call(
        body, "s5_fwd", (nt,),
        [_rows(t, SSM_WIDTH), _full((t, t)), _full((t, t)), _full(S5_TILES_SHAPE), _full(S5_TILES_SHAPE),
         _full((SUBLANES, N_STATES)), _full((SUBLANES, N_STATES)), _full((SUBLANES * S5_STEPS, N_STATES)),
         _full((SUBLANES * S5_STEPS, N_STATES)), _full((1, SSM_WIDTH))],
        [_rows(t, SSM_WIDTH), pl.BlockSpec((1, 2, N_STATES), lambda i: (i, 0, 0))],
        [jax.ShapeDtypeStruct((seq, SSM_WIDTH), F32), jax.ShapeDtypeStruct((nt, 2, N_STATES), F32)],
        [pltpu.VMEM((2 * S5_COLBLK, t, LANES), F32), pltpu.VMEM((t, 2 * N_STATES), BF16),
         pltpu.VMEM((1, N_STATES), F32), pltpu.VMEM((1, N_STATES), F32)],
        (u, p, pt, b_blk, c_blk, lr8, li8, p_re, p_im, d_skip), exchange)


def _s5_bwd(u, dy, b_blk, c_blk, tables, d_skip, starts, exchange=None):
    seq = u.shape[0]
    t = S5_TILE
    nt = seq // t
    lr8, li8, p_re, p_im = tables
    p, pt = _scan_order()

    def body(u_ref, dy_ref, p_ref, pt_ref, b_ref, c_ref, lr_ref, li_ref, pr_ref, pi_ref, d_ref, start_ref,
             du_ref, dd_ref, dlam_ref, db_ref, dc_ref,
             xs, as_, xb, ab, carry_re, carry_im, acar_re, acar_im):
        @pl.when(pl.program_id(0) == 0)
        def _():
            acar_re[...] = jnp.zeros_like(acar_re)
            acar_im[...] = jnp.zeros_like(acar_im)
            dd_ref[...] = jnp.zeros_like(dd_ref)
            dlam_ref[...] = jnp.zeros_like(dlam_ref)
            db_ref[...] = jnp.zeros_like(db_ref)
            dc_ref[...] = jnp.zeros_like(dc_ref)

        uv = u_ref[...]
        dyv = dy_ref[...]
        ub = _to_scan_order(p_ref, uv.astype(BF16))
        dyb = _to_scan_order(p_ref, dyv.astype(BF16))
        carry_re[...] = start_ref[0, 0:1, :]
        carry_im[...] = start_ref[0, 1:2, :]
        _s5_channels_to_states(ub, b_ref, xs)
        in_re, in_im = _s5_scan_tile(xs, lr_ref, li_ref, pr_ref, pi_ref, carry_re, carry_im, conj=False, reverse=False)
        _s5_channels_to_states(dyb, c_ref, as_)
        _s5_scan_tile(as_, lr_ref, li_ref, pr_ref, pi_ref, acar_re, acar_im, conj=True, reverse=True)
        for b in range(S5_COLBLK):
            def acc(i, c, b=b):
                xpr, xpi, sr, si = c
                ar = as_[b, _seg(i), :]
                ai = as_[S5_COLBLK + b, _seg(i), :]
                sr = sr + ar * xpr + ai * xpi
                si = si + ai * xpr - ar * xpi
                return xs[b, _seg(i), :], xs[S5_COLBLK + b, _seg(i), :], sr, si

            z = jnp.zeros((SUBLANES, LANES), F32)
            _, _, sr, si = lax.fori_loop(0, S5_STEPS, acc, (in_re[b], in_im[b], z, z), unroll=4)
            cols = slice(b * LANES, (b + 1) * LANES)
            dlam_ref[0, :, cols] += sr
            dlam_ref[1, :, cols] += si
        _s5_pack(xs, xb)
        _s5_pack(as_, ab)
        du_ref[...] = _from_scan_order(pt_ref, _s5_states_to_channels(ab, b_ref)) + d_ref[...] * dyv
        dd_ref[...] += jnp.sum(dyv * uv, axis=0, keepdims=True)
        _s5_outer_acc(db_ref, ub, ab)
        _s5_outer_acc(dc_ref, dyb, xb)

    rev = _rows_rev(t, SSM_WIDTH, nt)
    vec = lambda: pltpu.VMEM((1, N_STATES), F32)
    outer = S5_TILES_SHAPE
    return _call(
        body, "s5_bwd", (nt,),
        [rev, rev, _full((t, t)), _full((t, t)), _full(S5_TILES_SHAPE), _full(S5_TILES_SHAPE),
         _full((SUBLANES, N_STATES)), _full((SUBLANES, N_STATES)), _full((SUBLANES * S5_STEPS, N_STATES)),
         _full((SUBLANES * S5_STEPS, N_STATES)), _full((1, SSM_WIDTH)),
         pl.BlockSpec((1, 2, N_STATES), lambda i: (nt - 1 - i, 0, 0))],
        [rev, _full((1, SSM_WIDTH)), _full((2, SUBLANES, N_STATES)), _full(outer), _full(outer)],
        [jax.ShapeDtypeStruct((seq, SSM_WIDTH), F32), jax.ShapeDtypeStruct((1, SSM_WIDTH), F32),
         jax.ShapeDtypeStruct((2, SUBLANES, N_STATES), F32), jax.ShapeDtypeStruct(outer, F32), jax.ShapeDtypeStruct(outer, F32)],
        [pltpu.VMEM((2 * S5_COLBLK, t, LANES), F32), pltpu.VMEM((2 * S5_COLBLK, t, LANES), F32),
         pltpu.VMEM((t, 2 * N_STATES), BF16), pltpu.VMEM((t, 2 * N_STATES), BF16), vec(), vec(), vec(), vec()],
        (u, dy, p, pt, b_blk, c_blk, lr8, li8, p_re, p_im, d_skip, starts), exchange)


def _mix_fwd(y_ssm, y_ret, x, gain, w_glu, w_out):
    seq = x.shape[0]
    t = TOK_TILE

    def body(ys_ref, yr_ref, x_ref, g_ref, wg_ref, wo_ref, x1_ref, mix_ref):
        g0 = _gelu(ys_ref[...]).astype(BF16)
        z = _dot(g0, wg_ref[...])
        glu = (z[:, :SSM_WIDTH] * _sigmoid(z[:, SSM_WIDTH:])).astype(BF16)
        mix = _dot(yr_ref[...], wo_ref[:RET_WIDTH, :]) + _dot(glu, wo_ref[RET_WIDTH:, :])
        mix_ref[...] = mix
        n, _ = _rms(mix)
        x1_ref[...] = x_ref[...] + n * g_ref[...]

    act = jax.ShapeDtypeStruct((seq, D_MODEL), F32)
    return pl.pallas_call(
        body, name="mix_fwd", grid=(seq // t,),
        in_specs=[_rows(t, SSM_WIDTH), _rows(t, RET_WIDTH), _rows(t, D_MODEL), _full((1, D_MODEL)),
                  _full((SSM_WIDTH, 2 * SSM_WIDTH)), _full((D_MODEL, D_MODEL))],
        out_specs=[_rows(t, D_MODEL)] * 2, out_shape=[act, act],
        compiler_params=_params("parallel"),
    )(y_ssm, y_ret, x, gain, w_glu, w_out)


def _mix_bwd(dx1, mix, gain, y_ssm, y_ret, w_glu, w_out, exchange=None):
    seq = dx1.shape[0]
    t = TOK_TILE

    def body(dx1_ref, mix_ref, g_ref, ys_ref, yr_ref, wg_ref, wo_ref, dyr_ref, dys_ref, dwo_ref, dwg_ref, dg_ref):
        @pl.when(pl.program_id(0) == 0)
        def _():
            dwo_ref[...] = jnp.zeros_like(dwo_ref)
            dwg_ref[...] = jnp.zeros_like(dwg_ref)
            dg_ref[...] = jnp.zeros_like(dg_ref)

        n, r = _rms(mix_ref[...])
        dmix, dgs = _rms_bwd(n, r, g_ref[...], dx1_ref[...])
        dg_ref[...] += dgs
        dmb = dmix.astype(BF16)
        dcat = _dot_nt(dmb, wo_ref[...])
        dyr_ref[...] = dcat[:, :RET_WIDTH]
        dglu = dcat[:, RET_WIDTH:]
        ys = ys_ref[...]
        g0 = _gelu(ys).astype(BF16)
        z = _dot(g0, wg_ref[...])
        a = z[:, :SSM_WIDTH]
        sb = _sigmoid(z[:, SSM_WIDTH:])
        dwo_ref[:RET_WIDTH, :] += _dot_tn(yr_ref[...], dmb)
        dwo_ref[RET_WIDTH:, :] += _dot_tn((a * sb).astype(BF16), dmb)
        dz = jnp.concatenate([dglu * sb, dglu * a * sb * (1.0 - sb)], axis=1).astype(BF16)
        dwg_ref[...] += _dot_tn(g0, dz)
        dys_ref[...] = _dot_nt(dz, wg_ref[...]) * _gelu_grad(ys)

    half = jax.ShapeDtypeStruct((seq, RET_WIDTH), F32)
    return _call(
        body, "mix_bwd", (seq // t,),
        [_rows(t, D_MODEL), _rows(t, D_MODEL), _full((1, D_MODEL)), _rows(t, SSM_WIDTH), _rows(t, RET_WIDTH),
         _full((SSM_WIDTH, 2 * SSM_WIDTH)), _full((D_MODEL, D_MODEL))],
        [_rows(t, RET_WIDTH), _rows(t, SSM_WIDTH), _full((D_MODEL, D_MODEL)), _full((SSM_WIDTH, 2 * SSM_WIDTH)),
         _full((1, D_MODEL))],
        [half, half, jax.ShapeDtypeStruct((D_MODEL, D_MODEL), F32), jax.ShapeDtypeStruct((SSM_WIDTH, 2 * SSM_WIDTH), F32),
         jax.ShapeDtypeStruct((1, D_MODEL), F32)],
        [], (dx1, mix, gain, y_ssm, y_ret, w_glu, w_out), exchange)


def _mlp_a(x1, target, gain_pre, gain_post, w1, w2):
    seq = x1.shape[0]
    t = MLP_TILE

    def body(x1_ref, tg_ref, gp_ref, gq_ref, w1_ref, w2_ref, df_ref, dx2_ref, dw2_ref, dgq_ref, sq_ref):
        @pl.when(pl.program_id(0) == 0)
        def _():
            dw2_ref[...] = jnp.zeros_like(dw2_ref)
            dgq_ref[...] = jnp.zeros_like(dgq_ref)
            sq_ref[...] = jnp.zeros_like(sq_ref)

        x1v = x1_ref[...]
        n3, _ = _rms(x1v)
        h = (n3 * gp_ref[...]).astype(BF16)
        rl = jnp.maximum(_dot(h, w1_ref[...]), 0.0)
        act = (rl * rl).astype(BF16)
        n4, r4 = _rms(_dot(act, w2_ref[...]))
        gq = gq_ref[...]
        err = x1v + n4 * gq - tg_ref[...]
        sq_ref[...] += jnp.sum(err * err, axis=0, keepdims=True)
        dx2 = err * (1.0 / D_MODEL)
        dx2_ref[...] = dx2
        dm, dgs = _rms_bwd(n4, r4, gq, dx2)
        dgq_ref[...] += dgs
        dmb = dm.astype(BF16)
        dw2_ref[...] += _dot_tn(act, dmb)
        df_ref[...] = (_dot_nt(dmb, w2_ref[...]) * (2.0 * rl)).astype(BF16)

    return pl.pallas_call(
        body, name="mlp_a", grid=(seq // t,),
        in_specs=[_rows(t, D_MODEL), _rows(t, D_MODEL), _full((1, D_MODEL)), _full((1, D_MODEL)),
                  _full((D_MODEL, D_FF)), _full((D_FF, D_MODEL))],
        out_specs=[_rows(t, D_FF), _rows(t, D_MODEL), _full((D_FF, D_MODEL)), _full((1, D_MODEL)), _full((1, D_MODEL))],
        out_shape=[jax.ShapeDtypeStruct((seq, D_FF), BF16), jax.ShapeDtypeStruct((seq, D_MODEL), F32),
                   jax.ShapeDtypeStruct((D_FF, D_MODEL), F32), jax.ShapeDtypeStruct((1, D_MODEL), F32),
                   jax.ShapeDtypeStruct((1, D_MODEL), F32)],
        compiler_params=_params("arbitrary"),
    )(x1, target, gain_pre, gain_post, w1, w2)


def _mlp_b(df, dx2, x1, gain_pre, w1):
    seq = x1.shape[0]
    t = MLP_TILE

    def body(df_ref, dx2_ref, x1_ref, gp_ref, w1_ref, dx1_ref, dw1_ref, dgp_ref):
        @pl.when(pl.program_id(0) == 0)
        def _():
            dw1_ref[...] = jnp.zeros_like(dw1_ref)
            dgp_ref[...] = jnp.zeros_like(dgp_ref)

        n3, r3 = _rms(x1_ref[...])
        gp = gp_ref[...]
        h = (n3 * gp).astype(BF16)
        dfv = df_ref[...]
        dw1_ref[...] += _dot_tn(h, dfv)
        dx, dgs = _rms_bwd(n3, r3, gp, _dot_nt(dfv, w1_ref[...]))
        dgp_ref[...] += dgs
        dx1_ref[...] = dx2_ref[...] + dx

    return pl.pallas_call(
        body, name="mlp_b", grid=(seq // t,),
        in_specs=[_rows(t, D_FF), _rows(t, D_MODEL), _rows(t, D_MODEL), _full((1, D_MODEL)), _full((D_MODEL, D_FF))],
        out_specs=[_rows(t, D_MODEL), _full((D_MODEL, D_FF)), _full((1, D_MODEL))],
        out_shape=[jax.ShapeDtypeStruct((seq, D_MODEL), F32), jax.ShapeDtypeStruct((D_MODEL, D_FF), F32),
                   jax.ShapeDtypeStruct((1, D_MODEL), F32)],
        compiler_params=_params("arbitrary"),
    )(df, dx2, x1, gain_pre, w1)


def _local_step(x, target, small, weights, ids=None):
    seq = x.shape[0]
    half = HEAD_DIM // 2
    inv_freq = ROPE_BASE ** (-jnp.arange(half, dtype=F32) / half)
    ang = jnp.arange(seq, dtype=F32)[:, None] * inv_freq[None, :]
    cos, sin = jnp.cos(ang), jnp.sin(ang)
    cos2 = jnp.concatenate([cos, cos], axis=1)
    sin2 = jnp.concatenate([-sin, sin], axis=1)
    ret_consts = _ret_consts()

    s5_names = ("ssm_lambda_re", "ssm_lambda_im", "ssm_log_dt", "ssm_b_re", "ssm_b_im", "ssm_c_re", "ssm_c_im")
    (lbr, lbi, b_tiles, c_tiles), disc_vjp = jax.vjp(_s5_discretise, *[small[name] for name in s5_names])
    tables = _s5_tables(lbr, lbi)
    b_blk = b_tiles.astype(BF16)
    c_blk = c_tiles.astype(BF16)
    d_skip = small["ssm_d"].reshape(1, SSM_WIDTH)
    gn_gain = small["ret_gn_gain"].reshape(1, RET_WIDTH)
    g_mix_pre = small["norm_mix_pre"].reshape(1, D_MODEL)
    g_mix_post = small["norm_mix_post"].reshape(1, D_MODEL)
    g_mlp_pre = small["norm_mlp_pre"].reshape(1, D_MODEL)
    g_mlp_post = small["norm_mlp_post"].reshape(1, D_MODEL)

    dist = ids is not None
    w_in, w_glu, w_out, w_ff1, w_ff2 = weights
    (q, k, v, gate, u), got = _inproj_fwd(x, g_mix_pre, w_in, cos2, sin2, _x_gather_send((1, 2), [w_glu, w_out]) if dist else None)
    if dist:
        w_glu, w_out = got
    (y_ssm, starts), got = _s5_fwd(u, b_blk, c_blk, tables, d_skip, _x_gather_send(LATE, [w_ff1, w_ff2]) if dist else None)
    if dist:
        w_ff1, w_ff2 = got
    (y_ret, rprev), got = _ret_fwd(q, k, v, gate, gn_gain, ret_consts,
                                   _x_gather_pass((1, 2) + LATE, [w_glu, w_out, w_ff1, w_ff2]) if dist else None)
    if dist:
        w_glu, w_out, w_ff1, w_ff2 = got
    x1, mix = _mix_fwd(y_ssm, y_ret, x, g_mix_post, w_glu, w_out)
    df, dx2, d_ff2, dg_mlp_post, sq = _mlp_a(x1, target, g_mlp_pre, g_mlp_post, w_ff1, w_ff2)
    dx1, d_ff1, dg_mlp_pre = _mlp_b(df, dx2, x1, g_mlp_pre, w_ff1)
    (dy_ret, dy_ssm, d_out, d_glu, dg_mix_post), got = _mix_bwd(dx1, mix, g_mix_post, y_ssm, y_ret, w_glu, w_out,
                                                               _x_pair(LATE, [d_ff1, d_ff2]) if dist else None)
    if dist:
        sums, sums_bf16 = _pair_sum(ids, LATE, [d_ff1, d_ff2], got)
    (du, dd, dlam8, db_tiles, dc_tiles), got = _s5_bwd(u, dy_ssm, b_blk, c_blk, tables, d_skip, starts,
                                                       _x_chip(LATE, sums_bf16) if dist else None)
    if dist:
        parts = _chip_sum(ids, LATE, sums, got)
    (dq, dk, dv, dgate, dgn), late_grads = _ret_bwd(q, k, v, gate, gn_gain, ret_consts, rprev, dy_ret,
                                                    _x_share(LATE, parts) if dist else None)
    dx, d_in, dg_mix_pre = _inproj_bwd(x, g_mix_pre, w_in, cos2, sin2, dq, dk, dv, dgate, du, dx1)

    dlam = jnp.sum(dlam8, axis=1)
    s5_grads = disc_vjp((dlam[0].reshape(SSM_GROUPS, SSM_STATE), dlam[1].reshape(SSM_GROUPS, SSM_STATE), db_tiles, dc_tiles))
    small_grads = {
        "norm_mix_pre": dg_mix_pre, "norm_mix_post": dg_mix_post, "ret_gn_gain": dgn, "ssm_d": dd,
        "norm_mlp_pre": dg_mlp_pre, "norm_mlp_post": dg_mlp_post, "loss_sum": 0.5 * jnp.sum(sq) / D_MODEL,
    }
    small_grads.update(dict(zip(s5_names, s5_grads)))
    if dist:
        return dx, (d_in, d_glu, d_out), late_grads, small_grads
    return dx, (d_in, d_glu, d_out, d_ff1, d_ff2), small_grads


BIG = (
    ("w_in", D_MODEL, IN_COLS, 1),
    ("w_glu", SSM_WIDTH, 2 * SSM_WIDTH, 1),
    ("w_out", D_MODEL, D_MODEL, 0),
    ("w_ff1", D_MODEL, D_FF, 1),
    ("w_ff2", D_FF, D_MODEL, 0),
)
SMALL = (
    ("norm_mix_pre", (D_MODEL,)), ("norm_mix_post", (D_MODEL,)), ("ret_gn_gain", (RET_WIDTH,)),
    ("ssm_lambda_re", (SSM_GROUPS, SSM_STATE)), ("ssm_lambda_im", (SSM_GROUPS, SSM_STATE)), ("ssm_log_dt", (SSM_GROUPS,)),
    ("ssm_b_re", (SSM_GROUPS, SSM_STATE, SSM_GROUP)), ("ssm_b_im", (SSM_GROUPS, SSM_STATE, SSM_GROUP)),
    ("ssm_c_re", (SSM_GROUPS, SSM_GROUP, SSM_STATE)), ("ssm_c_im", (SSM_GROUPS, SSM_GROUP, SSM_STATE)),
    ("ssm_d", (SSM_WIDTH,)), ("norm_mlp_pre", (D_MODEL,)), ("norm_mlp_post", (D_MODEL,)),
)
SMALL_ROWS = 1152
EXCH_TILES = 8


PACKED = SMALL + (("loss_sum", (1,)),)


def _pack_small(tree):
    rows = []
    for name, shape in PACKED:
        size = math.prod(shape)
        flat = tree[name].reshape(-1).astype(F32) if name in tree else jnp.zeros((size,), F32)
        n = -(-size // LANES) * LANES
        rows.append(jnp.pad(flat, (0, n - size)).reshape(-1, LANES))
    used = sum(r.shape[0] for r in rows)
    rows.append(jnp.zeros((SMALL_ROWS - used, LANES), F32))
    return jnp.concatenate(rows, axis=0)


def _unpack_small(packed):
    out, row = {}, 0
    for name, shape in PACKED:
        size = math.prod(shape)
        n = -(-size // LANES)
        out[name] = packed[row:row + n].reshape(-1)[:size].reshape((1,) + shape)
        row += n
    return out


def _half_shape(r, c, axis):
    return (r // 2, c) if axis == 1 else (r, c // 2)


def _region_shape(r, c, axis):
    return (r // 2, c // N_CHIPS) if axis == 1 else (r // N_CHIPS, c // 2)


def _shard_shape(r, c, axis):
    return (r, c // N_CHIPS) if axis == 1 else (r // N_CHIPS, c)


def _ds(start, size):
    return pl.ds(pl.multiple_of(start * size, size), size)


def _region_of_full(ref, r, c, axis, shard, half):
    if axis == 1:
        return ref.at[_ds(half, r // 2), _ds(shard, c // N_CHIPS)]
    return ref.at[_ds(shard, r // N_CHIPS), _ds(half, c // 2)]


def _shard_of_full(ref, r, c, axis, shard):
    if axis == 1:
        return ref.at[:, _ds(shard, c // N_CHIPS)]
    return ref.at[_ds(shard, r // N_CHIPS), :]


def _half_of_full(ref, r, c, axis, half):
    if axis == 1:
        return ref.at[_ds(half, r // 2), :]
    return ref.at[:, _ds(half, c // 2)]


def _half_of_shard(ref, r, c, axis, half):
    if axis == 1:
        return ref.at[_ds(half, r // 2), :]
    return ref.at[:, _ds(half, c // 2)]


def _region_of_half(ref, r, c, axis, shard):
    if axis == 1:
        return ref.at[:, _ds(shard, c // N_CHIPS)]
    return ref.at[_ds(shard, r // N_CHIPS), :]


def _place():
    x, y, c = lax.axis_index("x"), lax.axis_index("y"), lax.axis_index("c")
    chips = [(1 - x, y), (x, 1 - y), (1 - x, 1 - y)]
    return x, y, c, chips


LATE = (3, 4)
EARLY = (0, 1, 2)
SMALL_HALF = (SMALL_ROWS // 2, LANES)


def _remote(src, dst, send_sem, recv_sem, to):
    return pltpu.make_async_remote_copy(src_ref=src, dst_ref=dst, send_sem=send_sem, recv_sem=recv_sem,
                                        device_id=to, device_id_type=MESH)


def _same(arrays):
    return [jax.ShapeDtypeStruct(a.shape, a.dtype) for a in arrays]


def _x_gather_send(ws, fulls):
    def copies(ins, outs, send_sems, recv_sems):
        x, y, c, chips = _place()
        out = []
        for j, w in enumerate(ws):
            _, r, cc, axis = BIG[w]
            mine = _region_of_full(outs[j], r, cc, axis, 2 * x + y, c)
            out += [_remote(mine, mine, send_sems.at[3 * j + k], recv_sems.at[3 * j + k], (cx, cy, c))
                    for k, (cx, cy) in enumerate(chips)]
        return out

    return _Exchange(fulls, _same(fulls), {j: j for j in range(len(ws))}, 3 * len(ws), copies)


def _x_gather_pass(ws, fulls):
    def copies(ins, outs, send_sems, recv_sems):
        x, y, c, chips = _place()
        out = []
        for j, w in enumerate(ws):
            _, r, cc, axis = BIG[w]
            for k, (cx, cy) in enumerate(chips):
                landed = _region_of_full(outs[j], r, cc, axis, 2 * cx + cy, c)
                out.append(_remote(landed, landed, send_sems.at[3 * j + k], recv_sems.at[3 * j + k], (x, y, 1 - c)))
        return out

    return _Exchange(fulls, _same(fulls), {j: j for j in range(len(ws))}, 3 * len(ws), copies)


def _x_pair(ws, grads, small=None):
    def copies(ins, outs, send_sems, recv_sems):
        x, y, c, _ = _place()
        out = []
        for j, w in enumerate(ws):
            _, r, cc, axis = BIG[w]
            out.append(_remote(_half_of_full(ins[j], r, cc, axis, 1 - c), outs[j], send_sems.at[j], recv_sems.at[j], (x, y, 1 - c)))
        if small is not None:
            j = len(ws)
            out.append(_remote(ins[j].at[_ds(1 - c, SMALL_ROWS // 2), :], outs[j], send_sems.at[j], recv_sems.at[j], (x, y, 1 - c)))
        return out

    shapes = [jax.ShapeDtypeStruct(_half_shape(*BIG[w][1:]), F32) for w in ws]
    extra = [] if small is None else [small]
    return _Exchange(list(grads) + extra, shapes + [jax.ShapeDtypeStruct(SMALL_HALF, F32)] * len(extra), {},
                     len(ws) + len(extra), copies)


def _x_chip(ws, sums_bf16, small_sum=None):
    def copies(ins, outs, send_sems, recv_sems):
        x, y, c, chips = _place()
        out = []
        for j, w in enumerate(ws):
            _, r, cc, axis = BIG[w]
            out += [_remote(_region_of_half(ins[j], r, cc, axis, 2 * cx + cy), outs[j].at[k],
                            send_sems.at[3 * j + k], recv_sems.at[3 * j + k], (cx, cy, c)) for k, (cx, cy) in enumerate(chips)]
        if small_sum is not None:
            j = len(ws)
            out += [_remote(ins[j], outs[j].at[k], send_sems.at[3 * j + k], recv_sems.at[3 * j + k], (cx, cy, c))
                    for k, (cx, cy) in enumerate(chips)]
        return out

    shapes = [jax.ShapeDtypeStruct((3,) + _region_shape(*BIG[w][1:]), BF16) for w in ws]
    extra = [] if small_sum is None else [small_sum]
    return _Exchange(list(sums_bf16) + extra, shapes + [jax.ShapeDtypeStruct((3,) + SMALL_HALF, F32)] * len(extra), {},
                     3 * (len(ws) + len(extra)), copies)


def _x_share(ws, shards, small=None):
    def copies(ins, outs, send_sems, recv_sems):
        x, y, c, _ = _place()
        out = []
        for j, w in enumerate(ws):
            _, r, cc, axis = BIG[w]
            mine = _half_of_shard(outs[j], r, cc, axis, c)
            out.append(_remote(mine, mine, send_sems.at[j], recv_sems.at[j], (x, y, 1 - c)))
        if small is not None:
            j = len(ws)
            mine = outs[j].at[_ds(c, SMALL_ROWS // 2), :]
            out.append(_remote(mine, mine, send_sems.at[j], recv_sems.at[j], (x, y, 1 - c)))
        return out

    arrays = list(shards) + ([] if small is None else [small])
    return _Exchange(arrays, _same(arrays), {j: j for j in range(len(arrays))}, len(arrays), copies)


def _run_exchange(name, exchange):
    k_in, k_out = len(exchange.operands), len(exchange.out_shapes)

    def body(*refs):
        copies = exchange.copies(refs[:k_in], refs[k_in:k_in + k_out], refs[-2], refs[-1])
        for cp in copies:
            cp.start()
        for cp in copies:
            cp.wait()

    return list(pl.pallas_call(
        body, name=name, in_specs=[ANY] * k_in, out_specs=[ANY] * k_out, out_shape=exchange.out_shapes,
        scratch_shapes=[pltpu.SemaphoreType.DMA((exchange.n,)), pltpu.SemaphoreType.DMA((exchange.n,))],
        input_output_aliases=dict(exchange.aliases),
    )(*exchange.operands))


def _tile_rows(shape):
    return (shape[0] // EXCH_TILES, shape[1])


def _cast_place(ids, shards):
    def body(ids_ref, *refs):
        for j in range(len(BIG)):
            refs[len(BIG) + j][...] = refs[j][...].astype(BF16)

    def out_spec(w):
        _, r, cc, axis = BIG[w]
        tile = _tile_rows(_shard_shape(r, cc, axis))
        if axis == 1:
            return pl.BlockSpec(tile, lambda i, ids: (i, ids[1]))
        return pl.BlockSpec(tile, lambda i, ids: (ids[1] * EXCH_TILES + i, 0))

    return list(pl.pallas_call(
        body, name="cast_place",
        grid_spec=pltpu.PrefetchScalarGridSpec(
            num_scalar_prefetch=1, grid=(EXCH_TILES,),
            in_specs=[pl.BlockSpec(_tile_rows(_shard_shape(r, cc, axis)), lambda i, ids: (i, 0)) for _, r, cc, axis in BIG],
            out_specs=[out_spec(w) for w in range(len(BIG))]),
        out_shape=[jax.ShapeDtypeStruct((r, cc), BF16) for _, r, cc, _ in BIG],
        compiler_params=pltpu.CompilerParams(dimension_semantics=("parallel",)),
    )(ids, *shards))


def _pair_sum(ids, ws, grads, got, small=None):
    nb = len(ws)
    n = nb + (small is not None)
    halves = [_half_shape(*BIG[w][1:]) for w in ws] + [SMALL_HALF] * (n - nb)

    def body(ids_ref, *refs):
        for j in range(n):
            total = refs[j][...] + refs[n + j][...]
            refs[2 * n + j][...] = total
            if j < nb:
                refs[3 * n + j][...] = total.astype(BF16)

    def mine_spec(j):
        tile = _tile_rows(halves[j])
        if j < nb and BIG[ws[j]][3] == 0:
            return pl.BlockSpec(tile, lambda i, ids: (i, ids[0]))
        return pl.BlockSpec(tile, lambda i, ids: (ids[0] * EXCH_TILES + i, 0))

    plain = lambda j: pl.BlockSpec(_tile_rows(halves[j]), lambda i, ids: (i, 0))
    outs = pl.pallas_call(
        body, name="pair_sum",
        grid_spec=pltpu.PrefetchScalarGridSpec(
            num_scalar_prefetch=1, grid=(EXCH_TILES,),
            in_specs=[mine_spec(j) for j in range(n)] + [plain(j) for j in range(n)],
            out_specs=[plain(j) for j in range(n)] + [plain(j) for j in range(nb)]),
        out_shape=[jax.ShapeDtypeStruct(h, F32) for h in halves] + [jax.ShapeDtypeStruct(h, BF16) for h in halves[:nb]],
        compiler_params=pltpu.CompilerParams(dimension_semantics=("parallel",)),
    )(ids, *grads, *([] if small is None else [small]), *got)
    return list(outs[:n]), list(outs[n:])


def _chip_sum(ids, ws, sums, arrived):
    nb, n = len(ws), len(sums)
    regions = [_region_shape(*BIG[w][1:]) for w in ws] + [SMALL_HALF] * (n - nb)

    def body(ids_ref, *refs):
        for j in range(n):
            own, got, out = refs[j], refs[n + j], refs[2 * n + j]
            if j < nb:
                out[...] = ((own[...] + got[0].astype(F32)) + got[1].astype(F32)) + got[2].astype(F32)
            else:
                out[...] = (own[...] + got[1]) + (got[0] + got[2])

    def own_spec(j):
        tile = _tile_rows(regions[j])
        if j >= nb:
            return pl.BlockSpec(tile, lambda i, ids: (i, 0))
        if BIG[ws[j]][3] == 1:
            return pl.BlockSpec(tile, lambda i, ids: (i, ids[1]))
        return pl.BlockSpec(tile, lambda i, ids: (ids[1] * EXCH_TILES + i, 0))

    def out_spec(j):
        tile = _tile_rows(regions[j])
        if j < nb and BIG[ws[j]][3] == 0:
            return pl.BlockSpec(tile, lambda i, ids: (i, ids[0]))
        return pl.BlockSpec(tile, lambda i, ids: (ids[0] * EXCH_TILES + i, 0))

    got_spec = lambda j: pl.BlockSpec((3,) + _tile_rows(regions[j]), lambda i, ids: (0, i, 0))
    return list(pl.pallas_call(
        body, name="chip_sum",
        grid_spec=pltpu.PrefetchScalarGridSpec(
            num_scalar_prefetch=1, grid=(EXCH_TILES,),
            in_specs=[own_spec(j) for j in range(n)] + [got_spec(j) for j in range(n)],
            out_specs=[out_spec(j) for j in range(n)]),
        out_shape=[jax.ShapeDtypeStruct(_shard_shape(*BIG[w][1:]), F32) for w in ws]
        + [jax.ShapeDtypeStruct((SMALL_ROWS, LANES), F32)] * (n - nb),
        compiler_params=pltpu.CompilerParams(dimension_semantics=("parallel",)),
    )(ids, *sums, *arrived))


def _adamw(ws, gs, ms, vs):
    n = len(ws)
    c1 = 1.0 - ADAM_B1 ** ADAM_STEP
    c2 = 1.0 - ADAM_B2 ** ADAM_STEP

    def body(*refs):
        for j in range(n):
            w, g, m, v = (refs[k * n + j][...] for k in range(4))
            m = ADAM_B1 * m + (1.0 - ADAM_B1) * g
            v = ADAM_B2 * v + (1.0 - ADAM_B2) * (g * g)
            refs[4 * n + j][...] = -ADAM_LR * ((m / c1) / (jnp.sqrt(v / c2) + ADAM_EPS) + ADAM_WD * w)
            refs[5 * n + j][...] = m
            refs[6 * n + j][...] = v

    specs = [pl.BlockSpec(_tile_rows(w.shape), lambda i: (i, 0)) for w in ws]
    shapes = [jax.ShapeDtypeStruct(w.shape, F32) for w in ws]
    out = pl.pallas_call(
        body, name="adamw", grid=(EXCH_TILES,),
        in_specs=specs * 4, out_specs=specs * 3, out_shape=shapes * 3,
        compiler_params=pltpu.CompilerParams(dimension_semantics=("parallel",)),
    )(*ws, *gs, *ms, *vs)
    return out[:n], out[n:2 * n], out[2 * n:]


def kernel(x, norm_mix_pre, norm_mix_post, w_in, ret_gn_gain, ssm_lambda_re, ssm_lambda_im, ssm_log_dt, ssm_b_re, ssm_b_im, ssm_c_re, ssm_c_im, ssm_d, w_glu, w_out, norm_mlp_pre, norm_mlp_post, w_ff1, w_ff2, loss_target, m_norm_mix_pre, m_norm_mix_post, m_w_in, m_ret_gn_gain, m_ssm_lambda_re, m_ssm_lambda_im, m_ssm_log_dt, m_ssm_b_re, m_ssm_b_im, m_ssm_c_re, m_ssm_c_im, m_ssm_d, m_w_glu, m_w_out, m_norm_mlp_pre, m_norm_mlp_post, m_w_ff1, m_w_ff2, v_norm_mix_pre, v_norm_mix_post, v_w_in, v_ret_gn_gain, v_ssm_lambda_re, v_ssm_lambda_im, v_ssm_log_dt, v_ssm_b_re, v_ssm_b_im, v_ssm_c_re, v_ssm_c_im, v_ssm_d, v_w_glu, v_w_out, v_norm_mlp_pre, v_norm_mlp_post, v_w_ff1, v_w_ff2):
    given = dict(locals())
    order = ["norm_mix_pre", "norm_mix_post", "w_in", "ret_gn_gain", "ssm_lambda_re", "ssm_lambda_im", "ssm_log_dt",
             "ssm_b_re", "ssm_b_im", "ssm_c_re", "ssm_c_im", "ssm_d", "w_glu", "w_out", "norm_mlp_pre", "norm_mlp_post",
             "w_ff1", "w_ff2"]
    big_names = [name for name, _, _, _ in BIG]
    ids = jnp.stack([lax.axis_index("c"), 2 * lax.axis_index("x") + lax.axis_index("y")]).astype(jnp.int32)

    weights = _cast_place(ids, [given[name][0] for name in big_names])
    w_in = _run_exchange("gather_w_in_send", _x_gather_send((0,), weights[:1]))
    w_in = _run_exchange("gather_w_in_pass", _x_gather_pass((0,), w_in))
    small_w = {name: given[name][0] for name, _ in SMALL}
    dx, early_grads, late_grads, small_grads = _local_step(x[0], loss_target[0], small_w, w_in + weights[1:], ids)

    small_packed = _pack_small(small_grads)
    got = _run_exchange("pair_exchange", _x_pair(EARLY, early_grads, small_packed))
    sums, sums_bf16 = _pair_sum(ids, EARLY, early_grads, got, small_packed)
    arrived = _run_exchange("chip_exchange", _x_chip(EARLY, sums_bf16, sums[-1]))
    parts = _chip_sum(ids, EARLY, sums, arrived)
    shared = _run_exchange("pair_share", _x_share(EARLY, parts[:-1], parts[-1]))
    nw = len(BIG)
    reduced = shared[:len(EARLY)] + late_grads + shared[len(EARLY):]

    pack = lambda prefix: _pack_small({name: given[prefix + name][0] for name, _ in SMALL})
    ws = [given[name][0] for name in big_names] + [pack("")]
    ms = [given["m_" + name][0] for name in big_names] + [pack("m_")]
    vs = [given["v_" + name][0] for name in big_names] + [pack("v_")]
    deltas, new_m, new_v = _adamw(ws, list(reduced), ms, vs)

    def unpacked(arrays):
        tree = _unpack_small(arrays[nw])
        tree.update({name: arrays[j][None] for j, name in enumerate(big_names)})
        return [tree[name] for name in order]

    loss = _unpack_small(reduced[nw])["loss_sum"].reshape(())
    return (loss, dx[None], *unpacked(list(reduced)), *unpacked(deltas), *unpacked(new_m), *unpacked(new_v))
```

```python
import functools
import math

import jax
import jax.numpy as jnp
from jax import lax
from jax.experimental import pallas as pl
from jax.experimental.pallas import tpu as pltpu

F32 = jnp.float32
BF16 = jnp.bfloat16

D_MODEL = 1024
RET_WIDTH = 512
RET_HEADS = 4
HEAD_DIM = 128
RET_CHUNK = 128
ROPE_BASE = 10000.0
SSM_WIDTH = 512
SSM_GROUP = 16
SSM_GROUPS = 32
SSM_STATE = 64
N_STATES = SSM_GROUPS * SSM_STATE
D_FF = 4096
IN_COLS = 4 * RET_WIDTH + SSM_WIDTH
NORM_EPS = 1e-6
K_SCALE = HEAD_DIM ** -0.5

ADAM_LR = 0.001
ADAM_B1 = 0.9
ADAM_B2 = 0.999
ADAM_EPS = 1e-08
ADAM_WD = 0.01
ADAM_STEP = 10

LANES = 128
SUBLANES = 8
VMEM_LIMIT = 56 * 2 ** 20

TOK_TILE = 512
MLP_TILE = 256
RET_CHUNKS_PER_STEP = 4
S5_TILE = 256
S5_STEPS = S5_TILE // SUBLANES
S5_COLBLK = N_STATES // LANES
S5_GROUP = 8

N_CHIPS = 4
MESH = pl.DeviceIdType.MESH


def _dot(a, b):
    return jnp.dot(a, b, preferred_element_type=F32)


def _dot_nt(a, b):
    return lax.dot_general(a, b, (((1,), (1,)), ((), ())), preferred_element_type=F32)


def _dot_tn(a, b):
    return lax.dot_general(a, b, (((0,), (0,)), ((), ())), preferred_element_type=F32)


def _sigmoid(x):
    return 1.0 / (1.0 + jnp.exp(-x))


_GELU_C = math.sqrt(2.0 / math.pi)


def _gelu(x):
    return 0.5 * x * (1.0 + jnp.tanh(_GELU_C * (x + 0.044715 * (x * x * x))))


def _gelu_grad(x):
    t = jnp.tanh(_GELU_C * (x + 0.044715 * (x * x * x)))
    return 0.5 * (1.0 + t) + 0.5 * x * (1.0 - t * t) * (_GELU_C * (1.0 + 3.0 * 0.044715 * (x * x)))


def _rms(x):
    r = lax.rsqrt(jnp.mean(x * x, axis=-1, keepdims=True) + NORM_EPS)
    return x * r, r


def _rms_bwd(n, r, gain, dy):
    dn = dy * gain
    dx = r * (dn - n * jnp.mean(dn * n, axis=-1, keepdims=True))
    return dx, jnp.sum(dy * n, axis=0, keepdims=True)


def _full(shape):
    nd = len(shape)
    return pl.BlockSpec(shape, lambda i, _n=nd: (0,) * _n, pipeline_mode=pl.Buffered(1))


def _rows(tile, width):
    return pl.BlockSpec((tile, width), lambda i: (i, 0))


def _rows_rev(tile, width, n):
    return pl.BlockSpec((tile, width), lambda i, _n=n: (_n - 1 - i, 0))


def _params(sem):
    return pltpu.CompilerParams(dimension_semantics=(sem,), vmem_limit_bytes=VMEM_LIMIT)


ANY = pl.BlockSpec(memory_space=pl.ANY)


class _Exchange:
    def __init__(self, operands, out_shapes, aliases, n, copies):
        self.operands, self.out_shapes, self.aliases, self.n, self.copies = list(operands), list(out_shapes), aliases, n, copies


def _call(body, name, grid, in_specs, out_specs, out_shape, scratch_shapes, args, exchange=None):
    if exchange is None:
        outs = pl.pallas_call(body, name=name, grid=grid, in_specs=in_specs, out_specs=out_specs, out_shape=out_shape,
                              scratch_shapes=scratch_shapes, compiler_params=_params("arbitrary"))(*args)
        return list(outs), []
    n_in, n_out, n_scr = len(in_specs), len(out_specs), len(scratch_shapes)
    k_in, k_out = len(exchange.operands), len(exchange.out_shapes)
    last = grid[0] - 1

    def hosted(*refs):
        own_in, rest = refs[:n_in], refs[n_in:]
        ex_in, rest = rest[:k_in], rest[k_in:]
        own_out, rest = rest[:n_out], rest[n_out:]
        ex_out, rest = rest[:k_out], rest[k_out:]
        own_scr, (send_sems, recv_sems) = rest[:n_scr], rest[n_scr:]

        @pl.when(pl.program_id(0) == 0)
        def _():
            for cp in exchange.copies(ex_in, ex_out, send_sems, recv_sems):
                cp.start()

        body(*own_in, *own_out, *own_scr)

        @pl.when(pl.program_id(0) == last)
        def _():
            for cp in exchange.copies(ex_in, ex_out, send_sems, recv_sems):
                cp.wait()

    outs = pl.pallas_call(
        hosted, name=name, grid=grid, in_specs=list(in_specs) + [ANY] * k_in, out_specs=list(out_specs) + [ANY] * k_out,
        out_shape=list(out_shape) + exchange.out_shapes,
        scratch_shapes=list(scratch_shapes) + [pltpu.SemaphoreType.DMA((exchange.n,)), pltpu.SemaphoreType.DMA((exchange.n,))],
        input_output_aliases={n_in + i: n_out + o for i, o in exchange.aliases.items()},
        compiler_params=_params("arbitrary"),
    )(*args, *exchange.operands)
    return list(outs[:n_out]), list(outs[n_out:])


def _rope(t, cos2, sin2):
    return t * cos2 + pltpu.roll(t, HEAD_DIM // 2, 1) * sin2


def _rope_bwd(d, cos2, sin2):
    return d * cos2 + pltpu.roll(d * sin2, HEAD_DIM // 2, 1)


def _inproj_fwd(x, gain, w_in, cos2, sin2, exchange=None):
    seq = x.shape[0]

    def body(x_ref, g_ref, w_ref, c_ref, s_ref, q_ref, k_ref, v_ref, gate_ref, u_ref):
        n, _ = _rms(x_ref[...])
        h = (n * g_ref[...]).astype(BF16)
        proj = _dot(h, w_ref[...])
        c = c_ref[...]
        s = s_ref[...]
        for hd in range(RET_HEADS):
            lo = hd * HEAD_DIM
            q_ref[:, lo:lo + HEAD_DIM] = _rope(proj[:, lo:lo + HEAD_DIM], c, s).astype(BF16)
            kh = proj[:, RET_WIDTH + lo:RET_WIDTH + lo + HEAD_DIM]
            k_ref[:, lo:lo + HEAD_DIM] = (_rope(kh, c, s) * K_SCALE).astype(BF16)
        v_ref[...] = proj[:, 2 * RET_WIDTH:3 * RET_WIDTH].astype(BF16)
        gate_ref[...] = proj[:, 3 * RET_WIDTH:4 * RET_WIDTH]
        u_ref[...] = proj[:, 4 * RET_WIDTH:]

    t = TOK_TILE
    half = lambda dt: jax.ShapeDtypeStruct((seq, RET_WIDTH), dt)
    return _call(
        body, "inproj_fwd", (seq // t,),
        [_rows(t, D_MODEL), _full((1, D_MODEL)), _full((D_MODEL, IN_COLS)), _rows(t, HEAD_DIM), _rows(t, HEAD_DIM)],
        [_rows(t, RET_WIDTH)] * 5, [half(BF16), half(BF16), half(BF16), half(F32), half(F32)], [],
        (x, gain, w_in, cos2, sin2), exchange)


def _inproj_bwd(x, gain, w_in, cos2, sin2, dq, dk, dv, dgate, du, dres):
    seq = x.shape[0]
    t = TOK_TILE

    def body(x_ref, g_ref, w_ref, c_ref, s_ref, dq_ref, dk_ref, dv_ref, dgate_ref, du_ref, dres_ref,
             dx_ref, dw_ref, dg_ref, dproj):
        @pl.when(pl.program_id(0) == 0)
        def _():
            dw_ref[...] = jnp.zeros_like(dw_ref)
            dg_ref[...] = jnp.zeros_like(dg_ref)

        c = c_ref[...]
        s = s_ref[...]
        for hd in range(RET_HEADS):
            lo = hd * HEAD_DIM
            dproj[:, lo:lo + HEAD_DIM] = _rope_bwd(dq_ref[:, lo:lo + HEAD_DIM], c, s).astype(BF16)
            dproj[:, RET_WIDTH + lo:RET_WIDTH + lo + HEAD_DIM] = _rope_bwd(
                dk_ref[:, lo:lo + HEAD_DIM] * K_SCALE, c, s).astype(BF16)
        dproj[:, 2 * RET_WIDTH:3 * RET_WIDTH] = dv_ref[...].astype(BF16)
        dproj[:, 3 * RET_WIDTH:4 * RET_WIDTH] = dgate_ref[...].astype(BF16)
        dproj[:, 4 * RET_WIDTH:] = du_ref[...].astype(BF16)
        dp = dproj[...]
        n, r = _rms(x_ref[...])
        gain_v = g_ref[...]
        h = (n * gain_v).astype(BF16)
        dw_ref[...] += _dot_tn(h, dp)
        dh = _dot_nt(dp, w_ref[...])
        dx, dgs = _rms_bwd(n, r, gain_v, dh)
        dg_ref[...] += dgs
        dx_ref[...] = dres_ref[...] + dx

    return pl.pallas_call(
        body, name="inproj_bwd", grid=(seq // t,),
        in_specs=[_rows(t, D_MODEL), _full((1, D_MODEL)), _full((D_MODEL, IN_COLS)), _rows(t, HEAD_DIM), _rows(t, HEAD_DIM)]
        + [_rows(t, RET_WIDTH)] * 5 + [_rows(t, D_MODEL)],
        out_specs=[_rows(t, D_MODEL), _full((D_MODEL, IN_COLS)), _full((1, D_MODEL))],
        out_shape=[jax.ShapeDtypeStruct((seq, D_MODEL), F32), jax.ShapeDtypeStruct((D_MODEL, IN_COLS), F32),
                   jax.ShapeDtypeStruct((1, D_MODEL), F32)],
        scratch_shapes=[pltpu.VMEM((t, IN_COLS), BF16)],
        compiler_params=_params("arbitrary"),
    )(x, gain, w_in, cos2, sin2, dq, dk, dv, dgate, du, dres)


def _ret_consts():
    c = RET_CHUNK
    log_gamma = jnp.log(1.0 - jnp.exp(jnp.linspace(math.log(1.0 / 32), math.log(1.0 / 512), RET_HEADS))).astype(F32)
    idx = jnp.arange(c, dtype=F32)
    diff = idx[:, None] - idx[None, :]
    decay = jnp.where(diff[None] >= 0, jnp.exp(jnp.maximum(diff, 0.0)[None] * log_gamma[:, None, None]), 0.0)
    zeta = jnp.exp((c - 1 - idx)[None, :] * log_gamma[:, None])
    xi = jnp.exp((idx + 1.0)[None, :] * log_gamma[:, None])
    g_chunk = jnp.exp(c * log_gamma)
    wide = lambda rowvals: jnp.broadcast_to(rowvals[:, :, None], (RET_HEADS, c, c)).astype(F32)
    return decay.astype(F32), wide(xi), wide(zeta), jnp.broadcast_to(g_chunk[:, None, None], (RET_HEADS, c, c)).astype(F32)


def _bmm(a, b):
    return lax.dot_general(a, b, (((2,), (1,)), ((0,), (0,))), preferred_element_type=F32)


def _bmm_nt(a, b):
    return lax.dot_general(a, b, (((2,), (2,)), ((0,), (0,))), preferred_element_type=F32)


def _bmm_tn(a, b):
    return lax.dot_general(a, b, (((1,), (1,)), ((0,), (0,))), preferred_element_type=F32)


def _ret_blocks(ref):
    return jnp.stack([ref[cc * RET_CHUNK:(cc + 1) * RET_CHUNK, hd * HEAD_DIM:(hd + 1) * HEAD_DIM]
                      for cc in range(RET_CHUNKS_PER_STEP) for hd in range(RET_HEADS)])


def _ret_unblock(ref, blocks):
    for cc in range(RET_CHUNKS_PER_STEP):
        for hd in range(RET_HEADS):
            ref[cc * RET_CHUNK:(cc + 1) * RET_CHUNK, hd * HEAD_DIM:(hd + 1) * HEAD_DIM] = blocks[cc * RET_HEADS + hd].astype(ref.dtype)


def _per_chunk(ref):
    return jnp.concatenate([ref[...]] * RET_CHUNKS_PER_STEP, axis=0)


def _ret_gain(gain_ref):
    return jnp.stack([gain_ref[:, hd * HEAD_DIM:(hd + 1) * HEAD_DIM]
                      for _ in range(RET_CHUNKS_PER_STEP) for hd in range(RET_HEADS)])


def _ret_fwd(q, k, v, gate, gn_gain, consts, exchange=None):
    seq = q.shape[0]
    cps = RET_CHUNKS_PER_STEP
    t = cps * RET_CHUNK
    n_chunks = seq // RET_CHUNK
    dec, xi, zeta, gch = consts

    def body(q_ref, k_ref, v_ref, gate_ref, gain_ref, dec_ref, xi_ref, zeta_ref, gch_ref, y_ref, rprev_ref, state):
        @pl.when(pl.program_id(0) == 0)
        def _():
            state[...] = jnp.zeros_like(state)

        qb, kb, vb = _ret_blocks(q_ref), _ret_blocks(k_ref), _ret_blocks(v_ref)
        s = _bmm_nt(qb, kb) * _per_chunk(dec_ref)
        vz = (vb.astype(F32) * _per_chunk(zeta_ref)).astype(BF16)
        incr = _bmm_tn(kb, vz)
        gch_v = gch_ref[...]
        rp = state[...]
        before = []
        for cc in range(cps):
            before.append(rp.astype(BF16))
            rprev_ref[cc] = before[-1]
            rp = gch_v * rp + incr[cc * RET_HEADS:(cc + 1) * RET_HEADS]
        state[...] = rp
        qx = (qb.astype(F32) * _per_chunk(xi_ref)).astype(BF16)
        o = _bmm(s.astype(BF16), vb) + _bmm(qx, jnp.concatenate(before, axis=0))
        oc = o - jnp.mean(o, axis=-1, keepdims=True)
        on = oc * lax.rsqrt(jnp.mean(oc * oc, axis=-1, keepdims=True) + NORM_EPS)
        g = _ret_blocks(gate_ref)
        _ret_unblock(y_ref, g * _sigmoid(g) * (on * _ret_gain(gain_ref)))

    cst = _full((RET_HEADS, RET_CHUNK, RET_CHUNK))
    return _call(
        body, "ret_fwd", (seq // t,),
        [_rows(t, RET_WIDTH)] * 4 + [_full((1, RET_WIDTH)), cst, cst, cst, cst],
        [_rows(t, RET_WIDTH), pl.BlockSpec((cps, RET_HEADS, HEAD_DIM, HEAD_DIM), lambda i: (i, 0, 0, 0))],
        [jax.ShapeDtypeStruct((seq, RET_WIDTH), BF16), jax.ShapeDtypeStruct((n_chunks, RET_HEADS, HEAD_DIM, HEAD_DIM), BF16)],
        [pltpu.VMEM((RET_HEADS, HEAD_DIM, HEAD_DIM), F32)],
        (q, k, v, gate, gn_gain, dec, xi, zeta, gch), exchange)


def _ret_bwd(q, k, v, gate, gn_gain, consts, rprev, dy_ret, exchange=None):
    seq = q.shape[0]
    cps = RET_CHUNKS_PER_STEP
    t = cps * RET_CHUNK
    nt = seq // t
    dec, xi, zeta, gch = consts

    def body(q_ref, k_ref, v_ref, gate_ref, gain_ref, dec_ref, xi_ref, zeta_ref, gch_ref, rprev_ref, dyr_ref,
             dq_ref, dk_ref, dv_ref, dgate_ref, dgain_ref, dstate):
        @pl.when(pl.program_id(0) == 0)
        def _():
            dstate[...] = jnp.zeros_like(dstate)
            dgain_ref[...] = jnp.zeros_like(dgain_ref)

        nb = cps * RET_HEADS
        qb, kb, vb = _ret_blocks(q_ref), _ret_blocks(k_ref), _ret_blocks(v_ref)
        dec_b, xi_b, zeta_b = _per_chunk(dec_ref), _per_chunk(xi_ref), _per_chunk(zeta_ref)
        rpb = rprev_ref[...].reshape(nb, HEAD_DIM, HEAD_DIM)
        sdb = (_bmm_nt(qb, kb) * dec_b).astype(BF16)
        qx = (qb.astype(F32) * xi_b).astype(BF16)
        o = _bmm(sdb, vb) + _bmm(qx, rpb)
        oc = o - jnp.mean(o, axis=-1, keepdims=True)
        rstd = lax.rsqrt(jnp.mean(oc * oc, axis=-1, keepdims=True) + NORM_EPS)
        on = oc * rstd
        g = _ret_blocks(gate_ref)
        sg = _sigmoid(g)
        gain_b = _ret_gain(gain_ref)
        dyr = _ret_blocks(dyr_ref)
        _ret_unblock(dgate_ref, dyr * (on * gain_b) * (sg * (1.0 + g * (1.0 - sg))))
        dy = dyr * (g * sg)
        dgain = jnp.sum(dy * on, axis=1, keepdims=True)
        for hd in range(RET_HEADS):
            part = dgain[hd]
            for cc in range(1, cps):
                part = part + dgain[cc * RET_HEADS + hd]
            dgain_ref[:, hd * HEAD_DIM:(hd + 1) * HEAD_DIM] += part
        don = dy * gain_b
        do = rstd * (don - jnp.mean(don, axis=-1, keepdims=True) - on * jnp.mean(don * on, axis=-1, keepdims=True))
        dob = do.astype(BF16)
        ds = (_bmm_nt(dob, vb) * dec_b).astype(BF16)
        _ret_unblock(dq_ref, _bmm(ds, kb) + _bmm_nt(dob, rpb) * xi_b)
        dk = _bmm_tn(ds, qb)
        dv = _bmm_tn(sdb, dob)
        dstate_local = _bmm_tn(qx, dob)
        vz = (vb.astype(F32) * zeta_b).astype(BF16)
        gch_v = gch_ref[...]
        zeta_v = zeta_ref[...]
        gh = dstate[...]
        dk_extra, dv_extra = [None] * cps, [None] * cps
        for cc in reversed(range(cps)):
            sl = slice(cc * RET_HEADS, (cc + 1) * RET_HEADS)
            gb = gh.astype(BF16)
            dk_extra[cc] = _bmm_nt(vz[sl], gb)
            dv_extra[cc] = _bmm(kb[sl], gb) * zeta_v
            gh = dstate_local[sl] + gch_v * gh
        dstate[...] = gh
        _ret_unblock(dk_ref, dk + jnp.concatenate(dk_extra, axis=0))
        _ret_unblock(dv_ref, dv + jnp.concatenate(dv_extra, axis=0))

    cst = _full((RET_HEADS, RET_CHUNK, RET_CHUNK))
    rev = _rows_rev(t, RET_WIDTH, nt)
    act = jax.ShapeDtypeStruct((seq, RET_WIDTH), F32)
    return _call(
        body, "ret_bwd", (nt,),
        [rev] * 4 + [_full((1, RET_WIDTH)), cst, cst, cst, cst,
                     pl.BlockSpec((cps, RET_HEADS, HEAD_DIM, HEAD_DIM), lambda i: (nt - 1 - i, 0, 0, 0)), rev],
        [rev] * 4 + [_full((1, RET_WIDTH))], [act, act, act, act, jax.ShapeDtypeStruct((1, RET_WIDTH), F32)],
        [pltpu.VMEM((RET_HEADS, HEAD_DIM, HEAD_DIM), F32)],
        (q, k, v, gate, gn_gain, dec, xi, zeta, gch, rprev, dy_ret), exchange)


MXU_TILE = 256
S5_STATE_TILES = 2 * N_STATES // MXU_TILE
S5_CHANNELS_PER_TILE = MXU_TILE // SSM_STATE * SSM_GROUP
S5_TILES_SHAPE = (S5_STATE_TILES, LANES, MXU_TILE)


def _chan_block(kk):
    return ((kk % (S5_STATE_TILES // 2)) * S5_CHANNELS_PER_TILE) // LANES * LANES


def _s5_tiles(blocks_re, blocks_im):
    per = MXU_TILE // SSM_STATE
    half = S5_STATE_TILES // 2
    eye = jnp.eye(per, dtype=F32)
    in_half = (jnp.arange(half) % (LANES // S5_CHANNELS_PER_TILE))[:, None] == jnp.arange(LANES // S5_CHANNELS_PER_TILE)[None, :]
    out = []
    for blk in (blocks_re, blocks_im):
        t = blk.reshape(half, per, SSM_GROUP, 1, SSM_STATE) * eye[None, :, None, :, None]
        t = t.reshape(half, 1, S5_CHANNELS_PER_TILE, MXU_TILE)
        out.append(jnp.where(in_half[:, :, None, None], t, 0.0).reshape(half, LANES, MXU_TILE))
    return jnp.concatenate(out, axis=0)


def _s5_discretise(lam_re, lam_im, log_dt, b_re, b_im, c_re, c_im):
    lam = lax.complex(jnp.minimum(lam_re, -1e-4), lam_im)
    dt = jnp.exp(log_dt)[:, None]
    lam_bar = jnp.exp(lam * dt)
    b_bar = ((lam_bar - 1.0) / lam)[:, :, None] * lax.complex(b_re, b_im)
    b_tiles = _s5_tiles(jnp.swapaxes(jnp.real(b_bar), 1, 2), jnp.swapaxes(jnp.imag(b_bar), 1, 2))
    return jnp.real(lam_bar), jnp.imag(lam_bar), b_tiles, _s5_tiles(c_re, -c_im)


def _s5_tables(lbr, lbi):
    lr = lbr.reshape(1, N_STATES)
    li = lbi.reshape(1, N_STATES)
    pr, pi = [lr], [li]
    for _ in range(S5_STEPS - 1):
        pr, pi = pr + [pr[-1] * lr - pi[-1] * li], pi + [pr[-1] * li + pi[-1] * lr]
    bc = lambda a: jnp.broadcast_to(a, (SUBLANES, N_STATES))
    rep = lambda rows: jnp.concatenate([bc(r) for r in rows], axis=0)
    return bc(lr), bc(li), rep(pr), rep(pi)


def _cmul(ar, ai, br, bi):
    return ar * br - ai * bi, ar * bi + ai * br


def _seg(i):
    return pl.ds(pl.multiple_of(i * SUBLANES, SUBLANES), SUBLANES)


def _scan_order():
    r = jnp.arange(S5_TILE)
    token = (r % SUBLANES) * S5_STEPS + r // SUBLANES
    p = (token[:, None] == jnp.arange(S5_TILE)[None, :]).astype(BF16)
    return p, p.T


def _to_scan_order(p_ref, rows_bf16):
    return _dot(p_ref[...], rows_bf16).astype(BF16)


def _from_scan_order(pt_ref, rows):
    hi = rows.astype(BF16)
    rest = rows - hi.astype(F32)
    mid = rest.astype(BF16)
    lo = (rest - mid.astype(F32)).astype(BF16)
    pt = pt_ref[...]
    return _dot(pt, hi) + _dot(pt, mid) + _dot(pt, lo)


def _s5_channels_to_states(chan_b, m_ref, xs):
    for kk in range(S5_STATE_TILES):
        lo = _chan_block(kk)
        res = _dot(chan_b[:, lo:lo + LANES], m_ref[kk])
        xs[2 * kk] = res[:, :LANES]
        xs[2 * kk + 1] = res[:, LANES:]


def _s5_states_to_channels(sb_ref, m_ref):
    out = []
    for lo in range(0, SSM_WIDTH, LANES):
        acc = None
        for kk in range(S5_STATE_TILES):
            if _chan_block(kk) == lo:
                term = _dot_nt(sb_ref[:, kk * MXU_TILE:(kk + 1) * MXU_TILE], m_ref[kk])
                acc = term if acc is None else acc + term
        out.append(acc)
    return jnp.concatenate(out, axis=1)


def _s5_outer_acc(acc_ref, chan_b, states_ref):
    for kk in range(S5_STATE_TILES):
        lo = _chan_block(kk)
        acc_ref[kk] += _dot_tn(chan_b[:, lo:lo + LANES], states_ref[:, kk * MXU_TILE:(kk + 1) * MXU_TILE])


def _s5_scan_tile(xs, lr_ref, li_ref, pr_ref, pi_ref, carry_re, carry_im, conj, reverse, prev_of=None):
    sgn = -1.0 if conj else 1.0
    seg_in_re, seg_in_im, sums_re, sums_im = [], [], [], []
    for grp in range(S5_COLBLK // S5_GROUP):
        blks = [grp * S5_GROUP + j for j in range(S5_GROUP)]
        lrs = [lr_ref[:, b * LANES:(b + 1) * LANES] for b in blks]
        lis = [sgn * li_ref[:, b * LANES:(b + 1) * LANES] for b in blks]

        def step(it, carry, blks=blks, lrs=lrs, lis=lis):
            i = (S5_STEPS - 1 - it) if reverse else it
            out_r, out_i = [], []
            for j, b in enumerate(blks):
                xr, xi = _cmul(lrs[j], lis[j], carry[j], carry[S5_GROUP + j])
                xr = xr + xs[b, _seg(i), :]
                xi = xi + xs[S5_COLBLK + b, _seg(i), :]
                xs[b, _seg(i), :] = xr
                xs[S5_COLBLK + b, _seg(i), :] = xi
                out_r.append(xr)
                out_i.append(xi)
            return tuple(out_r + out_i)

        zeros = tuple(jnp.zeros((SUBLANES, LANES), F32) for _ in range(2 * S5_GROUP))
        ends = lax.fori_loop(0, S5_STEPS, step, zeros, unroll=4)

        ins_r, ins_i = [], []
        for j, b in enumerate(blks):
            cols = slice(b * LANES, (b + 1) * LANES)
            pr = pr_ref[SUBLANES * S5_STEPS - 1:SUBLANES * S5_STEPS, cols]
            pi = sgn * pi_ref[SUBLANES * S5_STEPS - 1:SUBLANES * S5_STEPS, cols]
            cur_r = carry_re[:, cols]
            cur_i = carry_im[:, cols]
            row_id = lax.broadcasted_iota(jnp.int32, (SUBLANES, LANES), 0)
            in_r = jnp.zeros((SUBLANES, LANES), F32)
            in_i = jnp.zeros((SUBLANES, LANES), F32)
            order = range(SUBLANES - 1, -1, -1) if reverse else range(SUBLANES)
            for sgm in order:
                in_r = jnp.where(row_id == sgm, cur_r, in_r)
                in_i = jnp.where(row_id == sgm, cur_i, in_i)
                mr, mi = _cmul(pr, pi, cur_r, cur_i)
                cur_r = mr + ends[j][sgm:sgm + 1, :]
                cur_i = mi + ends[S5_GROUP + j][sgm:sgm + 1, :]
            carry_re[:, cols] = cur_r
            carry_im[:, cols] = cur_i
            ins_r.append(in_r)
            ins_i.append(in_i)

        def fix(it, c, blks=blks, ins_r=ins_r, ins_i=ins_i):
            pw = (S5_STEPS - 1 - it) if reverse else it
            out = []
            for j, b in enumerate(blks):
                cols = slice(b * LANES, (b + 1) * LANES)
                prow = pl.ds(pl.multiple_of(pw * SUBLANES, SUBLANES), SUBLANES)
                fr, fi = _cmul(pr_ref[prow, cols], sgn * pi_ref[prow, cols], ins_r[j], ins_i[j])
                vr = xs[b, _seg(it), :] + fr
                vi = xs[S5_COLBLK + b, _seg(it), :] + fi
                xs[b, _seg(it), :] = vr
                xs[S5_COLBLK + b, _seg(it), :] = vi
                if prev_of is not None:
                    opr, opi, sr, si = c[4 * j:4 * j + 4]
                    out += [prev_of[0][b, _seg(it), :], prev_of[0][S5_COLBLK + b, _seg(it), :],
                            sr + vr * opr + vi * opi, si + vi * opr - vr * opi]
            return tuple(out)

        init = ()
        if prev_of is not None:
            zero = jnp.zeros((SUBLANES, LANES), F32)
            for b in blks:
                init += (prev_of[1][b], prev_of[2][b], zero, zero)
        done = lax.fori_loop(0, S5_STEPS, fix, init, unroll=4)
        for j in range(len(blks) if prev_of is not None else 0):
            sums_re.append(done[4 * j + 2])
            sums_im.append(done[4 * j + 3])
        seg_in_re += ins_r
        seg_in_im += ins_i
    return seg_in_re, seg_in_im, sums_re, sums_im


def _s5_pack(xs, dst):
    for b in range(2 * S5_COLBLK):
        dst[:, b * LANES:(b + 1) * LANES] = xs[b].astype(BF16)


def _s5_fwd(u, b_blk, c_blk, tables, d_skip, exchange=None):
    seq = u.shape[0]
    t = S5_TILE
    nt = seq // t
    lr8, li8, p_re, p_im = tables

    p, pt = _scan_order()

    def body(u_ref, p_ref, pt_ref, b_ref, c_ref, lr_ref, li_ref, pr_ref, pi_ref, d_ref, y_ref, start_ref,
             xs, xb, carry_re, carry_im):
        @pl.when(pl.program_id(0) == 0)
        def _():
            carry_re[...] = jnp.zeros_like(carry_re)
            carry_im[...] = jnp.zeros_like(carry_im)

        start_ref[0, 0:1, :] = carry_re[...]
        start_ref[0, 1:2, :] = carry_im[...]
        uv = u_ref[...]
        _s5_channels_to_states(_to_scan_order(p_ref, uv.astype(BF16)), b_ref, xs)
        _s5_scan_tile(xs, lr_ref, li_ref, pr_ref, pi_ref, carry_re, carry_im, conj=False, reverse=False)
        _s5_pack(xs, xb)
        y_ref[...] = _from_scan_order(pt_ref, _s5_states_to_channels(xb, c_ref)) + d_ref[...] * uv

    return _call(
        body, "s5_fwd", (nt,),
        [_rows(t, SSM_WIDTH), _full((t, t)), _full((t, t)), _full(S5_TILES_SHAPE), _full(S5_TILES_SHAPE),
         _full((SUBLANES, N_STATES)), _full((SUBLANES, N_STATES)), _full((SUBLANES * S5_STEPS, N_STATES)),
         _full((SUBLANES * S5_STEPS, N_STATES)), _full((1, SSM_WIDTH))],
        [_rows(t, SSM_WIDTH), pl.BlockSpec((1, 2, N_STATES), lambda i: (i, 0, 0))],
        [jax.ShapeDtypeStruct((seq, SSM_WIDTH), F32), jax.ShapeDtypeStruct((nt, 2, N_STATES), F32)],
        [pltpu.VMEM((2 * S5_COLBLK, t, LANES), F32), pltpu.VMEM((t, 2 * N_STATES), BF16),
         pltpu.VMEM((1, N_STATES), F32), pltpu.VMEM((1, N_STATES), F32)],
        (u, p, pt, b_blk, c_blk, lr8, li8, p_re, p_im, d_skip), exchange)


def _s5_bwd(u, dy, b_blk, c_blk, tables, d_skip, starts, exchange=None):
    seq = u.shape[0]
    t = S5_TILE
    nt = seq // t
    lr8, li8, p_re, p_im = tables
    p, pt = _scan_order()

    def body(u_ref, dy_ref, p_ref, pt_ref, b_ref, c_ref, lr_ref, li_ref, pr_ref, pi_ref, d_ref, start_ref,
             du_ref, dd_ref, dlam_ref, db_ref, dc_ref,
             xs, as_, xb, ab, carry_re, carry_im, acar_re, acar_im):
        @pl.when(pl.program_id(0) == 0)
        def _():
            acar_re[...] = jnp.zeros_like(acar_re)
            acar_im[...] = jnp.zeros_like(acar_im)
            dd_ref[...] = jnp.zeros_like(dd_ref)
            dlam_ref[...] = jnp.zeros_like(dlam_ref)
            db_ref[...] = jnp.zeros_like(db_ref)
            dc_ref[...] = jnp.zeros_like(dc_ref)

        uv = u_ref[...]
        dyv = dy_ref[...]
        ub = _to_scan_order(p_ref, uv.astype(BF16))
        dyb = _to_scan_order(p_ref, dyv.astype(BF16))
        carry_re[...] = start_ref[0, 0:1, :]
        carry_im[...] = start_ref[0, 1:2, :]
        _s5_channels_to_states(ub, b_ref, xs)
        in_re, in_im, _, _ = _s5_scan_tile(xs, lr_ref, li_ref, pr_ref, pi_ref, carry_re, carry_im, conj=False, reverse=False)
        _s5_channels_to_states(dyb, c_ref, as_)
        _, _, dl_re, dl_im = _s5_scan_tile(as_, lr_ref, li_ref, pr_ref, pi_ref, acar_re, acar_im, conj=True, reverse=True,
                                           prev_of=(xs, in_re, in_im))
        for b in range(S5_COLBLK):
            cols = slice(b * LANES, (b + 1) * LANES)
            dlam_ref[0, :, cols] += dl_re[b]
            dlam_ref[1, :, cols] += dl_im[b]
        _s5_pack(xs, xb)
        _s5_pack(as_, ab)
        du_ref[...] = _from_scan_order(pt_ref, _s5_states_to_channels(ab, b_ref)) + d_ref[...] * dyv
        dd_ref[...] += jnp.sum(dyv * uv, axis=0, keepdims=True)
        _s5_outer_acc(db_ref, ub, ab)
        _s5_outer_acc(dc_ref, dyb, xb)

    rev = _rows_rev(t, SSM_WIDTH, nt)
    vec = lambda: pltpu.VMEM((1, N_STATES), F32)
    outer = S5_TILES_SHAPE
    return _call(
        body, "s5_bwd", (nt,),
        [rev, rev, _full((t, t)), _full((t, t)), _full(S5_TILES_SHAPE), _full(S5_TILES_SHAPE),
         _full((SUBLANES, N_STATES)), _full((SUBLANES, N_STATES)), _full((SUBLANES * S5_STEPS, N_STATES)),
         _full((SUBLANES * S5_STEPS, N_STATES)), _full((1, SSM_WIDTH)),
         pl.BlockSpec((1, 2, N_STATES), lambda i: (nt - 1 - i, 0, 0))],
        [rev, _full((1, SSM_WIDTH)), _full((2, SUBLANES, N_STATES)), _full(outer), _full(outer)],
        [jax.ShapeDtypeStruct((seq, SSM_WIDTH), F32), jax.ShapeDtypeStruct((1, SSM_WIDTH), F32),
         jax.ShapeDtypeStruct((2, SUBLANES, N_STATES), F32), jax.ShapeDtypeStruct(outer, F32), jax.ShapeDtypeStruct(outer, F32)],
        [pltpu.VMEM((2 * S5_COLBLK, t, LANES), F32), pltpu.VMEM((2 * S5_COLBLK, t, LANES), F32),
         pltpu.VMEM((t, 2 * N_STATES), BF16), pltpu.VMEM((t, 2 * N_STATES), BF16), vec(), vec(), vec(), vec()],
        (u, dy, p, pt, b_blk, c_blk, lr8, li8, p_re, p_im, d_skip, starts), exchange)


def _mix_fwd(y_ssm, y_ret, x, gain, w_glu, w_out):
    seq = x.shape[0]
    t = TOK_TILE

    def body(ys_ref, yr_ref, x_ref, g_ref, wg_ref, wo_ref, x1_ref, mix_ref):
        g0 = _gelu(ys_ref[...]).astype(BF16)
        z = _dot(g0, wg_ref[...])
        glu = (z[:, :SSM_WIDTH] * _sigmoid(z[:, SSM_WIDTH:])).astype(BF16)
        mix = _dot(yr_ref[...], wo_ref[:RET_WIDTH, :]) + _dot(glu, wo_ref[RET_WIDTH:, :])
        mix_ref[...] = mix
        n, _ = _rms(mix)
        x1_ref[...] = x_ref[...] + n * g_ref[...]

    act = jax.ShapeDtypeStruct((seq, D_MODEL), F32)
    return pl.pallas_call(
        body, name="mix_fwd", grid=(seq // t,),
        in_specs=[_rows(t, SSM_WIDTH), _rows(t, RET_WIDTH), _rows(t, D_MODEL), _full((1, D_MODEL)),
                  _full((SSM_WIDTH, 2 * SSM_WIDTH)), _full((D_MODEL, D_MODEL))],
        out_specs=[_rows(t, D_MODEL)] * 2, out_shape=[act, act],
        compiler_params=_params("parallel"),
    )(y_ssm, y_ret, x, gain, w_glu, w_out)


def _mix_bwd(dx1, mix, gain, y_ssm, y_ret, w_glu, w_out, exchange=None):
    seq = dx1.shape[0]
    t = TOK_TILE

    def body(dx1_ref, mix_ref, g_ref, ys_ref, yr_ref, wg_ref, wo_ref, dyr_ref, dys_ref, dwo_ref, dwg_ref, dg_ref):
        @pl.when(pl.program_id(0) == 0)
        def _():
            dwo_ref[...] = jnp.zeros_like(dwo_ref)
            dwg_ref[...] = jnp.zeros_like(dwg_ref)
            dg_ref[...] = jnp.zeros_like(dg_ref)

        n, r = _rms(mix_ref[...])
        dmix, dgs = _rms_bwd(n, r, g_ref[...], dx1_ref[...])
        dg_ref[...] += dgs
        dmb = dmix.astype(BF16)
        dcat = _dot_nt(dmb, wo_ref[...])
        dyr_ref[...] = dcat[:, :RET_WIDTH]
        dglu = dcat[:, RET_WIDTH:]
        ys = ys_ref[...]
        g0 = _gelu(ys).astype(BF16)
        z = _dot(g0, wg_ref[...])
        a = z[:, :SSM_WIDTH]
        sb = _sigmoid(z[:, SSM_WIDTH:])
        dwo_ref[:RET_WIDTH, :] += _dot_tn(yr_ref[...], dmb)
        dwo_ref[RET_WIDTH:, :] += _dot_tn((a * sb).astype(BF16), dmb)
        dz = jnp.concatenate([dglu * sb, dglu * a * sb * (1.0 - sb)], axis=1).astype(BF16)
        dwg_ref[...] += _dot_tn(g0, dz)
        dys_ref[...] = _dot_nt(dz, wg_ref[...]) * _gelu_grad(ys)

    half = jax.ShapeDtypeStruct((seq, RET_WIDTH), F32)
    return _call(
        body, "mix_bwd", (seq // t,),
        [_rows(t, D_MODEL), _rows(t, D_MODEL), _full((1, D_MODEL)), _rows(t, SSM_WIDTH), _rows(t, RET_WIDTH),
         _full((SSM_WIDTH, 2 * SSM_WIDTH)), _full((D_MODEL, D_MODEL))],
        [_rows(t, RET_WIDTH), _rows(t, SSM_WIDTH), _full((D_MODEL, D_MODEL)), _full((SSM_WIDTH, 2 * SSM_WIDTH)),
         _full((1, D_MODEL))],
        [half, half, jax.ShapeDtypeStruct((D_MODEL, D_MODEL), F32), jax.ShapeDtypeStruct((SSM_WIDTH, 2 * SSM_WIDTH), F32),
         jax.ShapeDtypeStruct((1, D_MODEL), F32)],
        [], (dx1, mix, gain, y_ssm, y_ret, w_glu, w_out), exchange)


def _mlp_a(x1, target, gain_pre, gain_post, w1, w2):
    seq = x1.shape[0]
    t = MLP_TILE

    def body(x1_ref, tg_ref, gp_ref, gq_ref, w1_ref, w2_ref, df_ref, dx2_ref, dw2_ref, dgq_ref, sq_ref):
        @pl.when(pl.program_id(0) == 0)
        def _():
            dw2_ref[...] = jnp.zeros_like(dw2_ref)
            dgq_ref[...] = jnp.zeros_like(dgq_ref)
            sq_ref[...] = jnp.zeros_like(sq_ref)

        x1v = x1_ref[...]
        n3, _ = _rms(x1v)
        h = (n3 * gp_ref[...]).astype(BF16)
        rl = jnp.maximum(_dot(h, w1_ref[...]), 0.0)
        act = (rl * rl).astype(BF16)
        n4, r4 = _rms(_dot(act, w2_ref[...]))
        gq = gq_ref[...]
        err = x1v + n4 * gq - tg_ref[...]
        sq_ref[...] += jnp.sum(err * err, axis=0, keepdims=True)
        dx2 = err * (1.0 / D_MODEL)
        dx2_ref[...] = dx2
        dm, dgs = _rms_bwd(n4, r4, gq, dx2)
        dgq_ref[...] += dgs
        dmb = dm.astype(BF16)
        dw2_ref[...] += _dot_tn(act, dmb)
        df_ref[...] = (_dot_nt(dmb, w2_ref[...]) * (2.0 * rl)).astype(BF16)

    return pl.pallas_call(
        body, name="mlp_a", grid=(seq // t,),
        in_specs=[_rows(t, D_MODEL), _rows(t, D_MODEL), _full((1, D_MODEL)), _full((1, D_MODEL)),
                  _full((D_MODEL, D_FF)), _full((D_FF, D_MODEL))],
        out_specs=[_rows(t, D_FF), _rows(t, D_MODEL), _full((D_FF, D_MODEL)), _full((1, D_MODEL)), _full((1, D_MODEL))],
        out_shape=[jax.ShapeDtypeStruct((seq, D_FF), BF16), jax.ShapeDtypeStruct((seq, D_MODEL), F32),
                   jax.ShapeDtypeStruct((D_FF, D_MODEL), F32), jax.ShapeDtypeStruct((1, D_MODEL), F32),
                   jax.ShapeDtypeStruct((1, D_MODEL), F32)],
        compiler_params=_params("arbitrary"),
    )(x1, target, gain_pre, gain_post, w1, w2)


def _mlp_b(df, dx2, x1, gain_pre, w1):
    seq = x1.shape[0]
    t = TOK_TILE

    def body(df_ref, dx2_ref, x1_ref, gp_ref, w1_ref, dx1_ref, dw1_ref, dgp_ref):
        @pl.when(pl.program_id(0) == 0)
        def _():
            dw1_ref[...] = jnp.zeros_like(dw1_ref)
            dgp_ref[...] = jnp.zeros_like(dgp_ref)

        n3, r3 = _rms(x1_ref[...])
        gp = gp_ref[...]
        h = (n3 * gp).astype(BF16)
        dfv = df_ref[...]
        dw1_ref[...] += _dot_tn(h, dfv)
        dx, dgs = _rms_bwd(n3, r3, gp, _dot_nt(dfv, w1_ref[...]))
        dgp_ref[...] += dgs
        dx1_ref[...] = dx2_ref[...] + dx

    return pl.pallas_call(
        body, name="mlp_b", grid=(seq // t,),
        in_specs=[_rows(t, D_FF), _rows(t, D_MODEL), _rows(t, D_MODEL), _full((1, D_MODEL)), _full((D_MODEL, D_FF))],
        out_specs=[_rows(t, D_MODEL), _full((D_MODEL, D_FF)), _full((1, D_MODEL))],
        out_shape=[jax.ShapeDtypeStruct((seq, D_MODEL), F32), jax.ShapeDtypeStruct((D_MODEL, D_FF), F32),
                   jax.ShapeDtypeStruct((1, D_MODEL), F32)],
        compiler_params=_params("arbitrary"),
    )(df, dx2, x1, gain_pre, w1)


def _local_step(x, target, small, weights, ids=None):
    seq = x.shape[0]
    half = HEAD_DIM // 2
    inv_freq = ROPE_BASE ** (-jnp.arange(half, dtype=F32) / half)
    ang = jnp.arange(seq, dtype=F32)[:, None] * inv_freq[None, :]
    cos, sin = jnp.cos(ang), jnp.sin(ang)
    cos2 = jnp.concatenate([cos, cos], axis=1)
    sin2 = jnp.concatenate([-sin, sin], axis=1)
    ret_consts = _ret_consts()

    s5_names = ("ssm_lambda_re", "ssm_lambda_im", "ssm_log_dt", "ssm_b_re", "ssm_b_im", "ssm_c_re", "ssm_c_im")
    (lbr, lbi, b_tiles, c_tiles), disc_vjp = jax.vjp(_s5_discretise, *[small[name] for name in s5_names])
    tables = _s5_tables(lbr, lbi)
    b_blk = b_tiles.astype(BF16)
    c_blk = c_tiles.astype(BF16)
    d_skip = small["ssm_d"].reshape(1, SSM_WIDTH)
    gn_gain = small["ret_gn_gain"].reshape(1, RET_WIDTH)
    g_mix_pre = small["norm_mix_pre"].reshape(1, D_MODEL)
    g_mix_post = small["norm_mix_post"].reshape(1, D_MODEL)
    g_mlp_pre = small["norm_mlp_pre"].reshape(1, D_MODEL)
    g_mlp_post = small["norm_mlp_post"].reshape(1, D_MODEL)

    dist = ids is not None
    w_in, w_glu, w_out, w_ff1, w_ff2 = weights
    (q, k, v, gate, u), got = _inproj_fwd(x, g_mix_pre, w_in, cos2, sin2, _x_gather_send((1, 2), [w_glu, w_out]) if dist else None)
    if dist:
        w_glu, w_out = got
    (y_ssm, starts), got = _s5_fwd(u, b_blk, c_blk, tables, d_skip, _x_gather_send(LATE, [w_ff1, w_ff2]) if dist else None)
    if dist:
        w_ff1, w_ff2 = got
    (y_ret, rprev), got = _ret_fwd(q, k, v, gate, gn_gain, ret_consts,
                                   _x_gather_pass((1, 2) + LATE, [w_glu, w_out, w_ff1, w_ff2]) if dist else None)
    if dist:
        w_glu, w_out, w_ff1, w_ff2 = got
    x1, mix = _mix_fwd(y_ssm, y_ret, x, g_mix_post, w_glu, w_out)
    df, dx2, d_ff2, dg_mlp_post, sq = _mlp_a(x1, target, g_mlp_pre, g_mlp_post, w_ff1, w_ff2)
    dx1, d_ff1, dg_mlp_pre = _mlp_b(df, dx2, x1, g_mlp_pre, w_ff1)
    (dy_ret, dy_ssm, d_out, d_glu, dg_mix_post), got = _mix_bwd(dx1, mix, g_mix_post, y_ssm, y_ret, w_glu, w_out,
                                                               _x_pair(LATE, [d_ff1, d_ff2]) if dist else None)
    if dist:
        sums, sums_bf16 = _pair_sum(ids, LATE, [d_ff1, d_ff2], got)
    (du, dd, dlam8, db_tiles, dc_tiles), got = _s5_bwd(u, dy_ssm, b_blk, c_blk, tables, d_skip, starts,
                                                       _x_chip(LATE, sums_bf16) if dist else None)
    if dist:
        parts = _chip_sum(ids, LATE, sums, got)
    (dq, dk, dv, dgate, dgn), late_grads = _ret_bwd(q, k, v, gate, gn_gain, ret_consts, rprev, dy_ret,
                                                    _x_share(LATE, parts) if dist else None)
    dx, d_in, dg_mix_pre = _inproj_bwd(x, g_mix_pre, w_in, cos2, sin2, dq, dk, dv, dgate, du, dx1)

    dlam = jnp.sum(dlam8, axis=1)
    s5_grads = disc_vjp((dlam[0].reshape(SSM_GROUPS, SSM_STATE), dlam[1].reshape(SSM_GROUPS, SSM_STATE), db_tiles, dc_tiles))
    small_grads = {
        "norm_mix_pre": dg_mix_pre, "norm_mix_post": dg_mix_post, "ret_gn_gain": dgn, "ssm_d": dd,
        "norm_mlp_pre": dg_mlp_pre, "norm_mlp_post": dg_mlp_post, "loss_sum": 0.5 * jnp.sum(sq) / D_MODEL,
    }
    small_grads.update(dict(zip(s5_names, s5_grads)))
    if dist:
        return dx, (d_in, d_glu, d_out), late_grads, small_grads
    return dx, (d_in, d_glu, d_out, d_ff1, d_ff2), small_grads


BIG = (
    ("w_in", D_MODEL, IN_COLS, 1),
    ("w_glu", SSM_WIDTH, 2 * SSM_WIDTH, 1),
    ("w_out", D_MODEL, D_MODEL, 0),
    ("w_ff1", D_MODEL, D_FF, 1),
    ("w_ff2", D_FF, D_MODEL, 0),
)
SMALL = (
    ("norm_mix_pre", (D_MODEL,)), ("norm_mix_post", (D_MODEL,)), ("ret_gn_gain", (RET_WIDTH,)),
    ("ssm_lambda_re", (SSM_GROUPS, SSM_STATE)), ("ssm_lambda_im", (SSM_GROUPS, SSM_STATE)), ("ssm_log_dt", (SSM_GROUPS,)),
    ("ssm_b_re", (SSM_GROUPS, SSM_STATE, SSM_GROUP)), ("ssm_b_im", (SSM_GROUPS, SSM_STATE, SSM_GROUP)),
    ("ssm_c_re", (SSM_GROUPS, SSM_GROUP, SSM_STATE)), ("ssm_c_im", (SSM_GROUPS, SSM_GROUP, SSM_STATE)),
    ("ssm_d", (SSM_WIDTH,)), ("norm_mlp_pre", (D_MODEL,)), ("norm_mlp_post", (D_MODEL,)),
)
SMALL_ROWS = 1152
EXCH_TILES = 8


PACKED = SMALL + (("loss_sum", (1,)),)


def _packed_rows(shape):
    return -(-math.prod(shape) // (SUBLANES * LANES)) * SUBLANES


def _pack_small(tree):
    rows = []
    for name, shape in PACKED:
        size = math.prod(shape)
        flat = tree[name].reshape(-1).astype(F32) if name in tree else jnp.zeros((size,), F32)
        rows.append(jnp.pad(flat, (0, _packed_rows(shape) * LANES - size)).reshape(-1, LANES))
    used = sum(r.shape[0] for r in rows)
    rows.append(jnp.zeros((SMALL_ROWS - used, LANES), F32))
    return jnp.concatenate(rows, axis=0)


def _unpack_small(packed):
    out, row = {}, 0
    for name, shape in PACKED:
        size = math.prod(shape)
        out[name] = packed[row:row + _packed_rows(shape)].reshape(-1)[:size].reshape((1,) + shape)
        row += _packed_rows(shape)
    return out


def _half_shape(r, c, axis):
    return (r // 2, c) if axis == 1 else (r, c // 2)


def _region_shape(r, c, axis):
    return (r // 2, c // N_CHIPS) if axis == 1 else (r // N_CHIPS, c // 2)


def _shard_shape(r, c, axis):
    return (r, c // N_CHIPS) if axis == 1 else (r // N_CHIPS, c)


def _ds(start, size):
    return pl.ds(pl.multiple_of(start * size, size), size)


def _region_of_full(ref, r, c, axis, shard, half):
    if axis == 1:
        return ref.at[_ds(half, r // 2), _ds(shard, c // N_CHIPS)]
    return ref.at[_ds(shard, r // N_CHIPS), _ds(half, c // 2)]


def _half_of_full(ref, r, c, axis, half):
    if axis == 1:
        return ref.at[_ds(half, r // 2), :]
    return ref.at[:, _ds(half, c // 2)]


def _half_of_shard(ref, r, c, axis, half):
    if axis == 1:
        return ref.at[_ds(half, r // 2), :]
    return ref.at[:, _ds(half, c // 2)]


def _region_of_half(ref, r, c, axis, shard):
    if axis == 1:
        return ref.at[:, _ds(shard, c // N_CHIPS)]
    return ref.at[_ds(shard, r // N_CHIPS), :]


def _place():
    x, y, c = lax.axis_index("x"), lax.axis_index("y"), lax.axis_index("c")
    chips = [(1 - x, y), (x, 1 - y), (1 - x, 1 - y)]
    return x, y, c, chips


LATE = (3, 4)
EARLY = (0, 1, 2)
SMALL_HALF = (SMALL_ROWS // 2, LANES)


def _remote(src, dst, send_sem, recv_sem, to):
    return pltpu.make_async_remote_copy(src_ref=src, dst_ref=dst, send_sem=send_sem, recv_sem=recv_sem,
                                        device_id=to, device_id_type=MESH)


def _same(arrays):
    return [jax.ShapeDtypeStruct(a.shape, a.dtype) for a in arrays]


def _x_gather_send(ws, fulls):
    def copies(ins, outs, send_sems, recv_sems):
        x, y, c, chips = _place()
        out = []
        for j, w in enumerate(ws):
            _, r, cc, axis = BIG[w]
            mine = _region_of_full(outs[j], r, cc, axis, 2 * x + y, c)
            out += [_remote(mine, mine, send_sems.at[3 * j + k], recv_sems.at[3 * j + k], (cx, cy, c))
                    for k, (cx, cy) in enumerate(chips)]
        return out

    return _Exchange(fulls, _same(fulls), {j: j for j in range(len(ws))}, 3 * len(ws), copies)


def _x_gather_pass(ws, fulls):
    def copies(ins, outs, send_sems, recv_sems):
        x, y, c, chips = _place()
        out = []
        for j, w in enumerate(ws):
            _, r, cc, axis = BIG[w]
            for k, (cx, cy) in enumerate(chips):
                landed = _region_of_full(outs[j], r, cc, axis, 2 * cx + cy, c)
                out.append(_remote(landed, landed, send_sems.at[3 * j + k], recv_sems.at[3 * j + k], (x, y, 1 - c)))
        return out

    return _Exchange(fulls, _same(fulls), {j: j for j in range(len(ws))}, 3 * len(ws), copies)


def _x_pair(ws, grads, small=None):
    def copies(ins, outs, send_sems, recv_sems):
        x, y, c, _ = _place()
        out = []
        for j, w in enumerate(ws):
            _, r, cc, axis = BIG[w]
            out.append(_remote(_half_of_full(ins[j], r, cc, axis, 1 - c), outs[j], send_sems.at[j], recv_sems.at[j], (x, y, 1 - c)))
        if small is not None:
            j = len(ws)
            out.append(_remote(ins[j].at[_ds(1 - c, SMALL_ROWS // 2), :], outs[j], send_sems.at[j], recv_sems.at[j], (x, y, 1 - c)))
        return out

    shapes = [jax.ShapeDtypeStruct(_half_shape(*BIG[w][1:]), F32) for w in ws]
    extra = [] if small is None else [small]
    return _Exchange(list(grads) + extra, shapes + [jax.ShapeDtypeStruct(SMALL_HALF, F32)] * len(extra), {},
                     len(ws) + len(extra), copies)


def _x_chip(ws, sums_bf16, small_sum=None):
    def copies(ins, outs, send_sems, recv_sems):
        x, y, c, chips = _place()
        out = []
        for j, w in enumerate(ws):
            _, r, cc, axis = BIG[w]
            out += [_remote(_region_of_half(ins[j], r, cc, axis, 2 * cx + cy), outs[j].at[k],
                            send_sems.at[3 * j + k], recv_sems.at[3 * j + k], (cx, cy, c)) for k, (cx, cy) in enumerate(chips)]
        if small_sum is not None:
            j = len(ws)
            out += [_remote(ins[j], outs[j].at[k], send_sems.at[3 * j + k], recv_sems.at[3 * j + k], (cx, cy, c))
                    for k, (cx, cy) in enumerate(chips)]
        return out

    shapes = [jax.ShapeDtypeStruct((3,) + _region_shape(*BIG[w][1:]), BF16) for w in ws]
    extra = [] if small_sum is None else [small_sum]
    return _Exchange(list(sums_bf16) + extra, shapes + [jax.ShapeDtypeStruct((3,) + SMALL_HALF, F32)] * len(extra), {},
                     3 * (len(ws) + len(extra)), copies)


def _x_share(ws, shards, small=None):
    def copies(ins, outs, send_sems, recv_sems):
        x, y, c, _ = _place()
        out = []
        for j, w in enumerate(ws):
            _, r, cc, axis = BIG[w]
            mine = _half_of_shard(outs[j], r, cc, axis, c)
            out.append(_remote(mine, mine, send_sems.at[j], recv_sems.at[j], (x, y, 1 - c)))
        if small is not None:
            j = len(ws)
            mine = outs[j].at[_ds(c, SMALL_ROWS // 2), :]
            out.append(_remote(mine, mine, send_sems.at[j], recv_sems.at[j], (x, y, 1 - c)))
        return out

    arrays = list(shards) + ([] if small is None else [small])
    return _Exchange(arrays, _same(arrays), {j: j for j in range(len(arrays))}, len(arrays), copies)


def _run_exchange(name, exchange):
    k_in, k_out = len(exchange.operands), len(exchange.out_shapes)

    def body(*refs):
        copies = exchange.copies(refs[:k_in], refs[k_in:k_in + k_out], refs[-2], refs[-1])
        for cp in copies:
            cp.start()
        for cp in copies:
            cp.wait()

    return list(pl.pallas_call(
        body, name=name, in_specs=[ANY] * k_in, out_specs=[ANY] * k_out, out_shape=exchange.out_shapes,
        scratch_shapes=[pltpu.SemaphoreType.DMA((exchange.n,)), pltpu.SemaphoreType.DMA((exchange.n,))],
        input_output_aliases=dict(exchange.aliases),
    )(*exchange.operands))


def _tile_rows(shape):
    return (shape[0] // EXCH_TILES, shape[1])


def _cast_place(ids, shards):
    def body(ids_ref, *refs):
        for j in range(len(BIG)):
            refs[len(BIG) + j][...] = refs[j][...].astype(BF16)

    def out_spec(w):
        _, r, cc, axis = BIG[w]
        tile = _tile_rows(_shard_shape(r, cc, axis))
        if axis == 1:
            return pl.BlockSpec(tile, lambda i, ids: (i, ids[1]))
        return pl.BlockSpec(tile, lambda i, ids: (ids[1] * EXCH_TILES + i, 0))

    return list(pl.pallas_call(
        body, name="cast_place",
        grid_spec=pltpu.PrefetchScalarGridSpec(
            num_scalar_prefetch=1, grid=(EXCH_TILES,),
            in_specs=[pl.BlockSpec(_tile_rows(_shard_shape(r, cc, axis)), lambda i, ids: (i, 0)) for _, r, cc, axis in BIG],
            out_specs=[out_spec(w) for w in range(len(BIG))]),
        out_shape=[jax.ShapeDtypeStruct((r, cc), BF16) for _, r, cc, _ in BIG],
        compiler_params=pltpu.CompilerParams(dimension_semantics=("parallel",)),
    )(ids, *shards))


def _pair_sum(ids, ws, grads, got, small=None):
    nb = len(ws)
    n = nb + (small is not None)
    halves = [_half_shape(*BIG[w][1:]) for w in ws] + [SMALL_HALF] * (n - nb)

    def body(ids_ref, *refs):
        for j in range(n):
            total = refs[j][...] + refs[n + j][...]
            refs[2 * n + j][...] = total
            if j < nb:
                refs[3 * n + j][...] = total.astype(BF16)

    def mine_spec(j):
        tile = _tile_rows(halves[j])
        if j < nb and BIG[ws[j]][3] == 0:
            return pl.BlockSpec(tile, lambda i, ids: (i, ids[0]))
        return pl.BlockSpec(tile, lambda i, ids: (ids[0] * EXCH_TILES + i, 0))

    plain = lambda j: pl.BlockSpec(_tile_rows(halves[j]), lambda i, ids: (i, 0))
    outs = pl.pallas_call(
        body, name="pair_sum",
        grid_spec=pltpu.PrefetchScalarGridSpec(
            num_scalar_prefetch=1, grid=(EXCH_TILES,),
            in_specs=[mine_spec(j) for j in range(n)] + [plain(j) for j in range(n)],
            out_specs=[plain(j) for j in range(n)] + [plain(j) for j in range(nb)]),
        out_shape=[jax.ShapeDtypeStruct(h, F32) for h in halves] + [jax.ShapeDtypeStruct(h, BF16) for h in halves[:nb]],
        compiler_params=pltpu.CompilerParams(dimension_semantics=("parallel",)),
    )(ids, *grads, *([] if small is None else [small]), *got)
    return list(outs[:n]), list(outs[n:])


def _chip_sum(ids, ws, sums, arrived):
    nb, n = len(ws), len(sums)
    regions = [_region_shape(*BIG[w][1:]) for w in ws] + [SMALL_HALF] * (n - nb)

    def body(ids_ref, *refs):
        for j in range(n):
            own, got, out = refs[j], refs[n + j], refs[2 * n + j]
            if j < nb:
                out[...] = ((own[...] + got[0].astype(F32)) + got[1].astype(F32)) + got[2].astype(F32)
            else:
                out[...] = (own[...] + got[1]) + (got[0] + got[2])

    def own_spec(j):
        tile = _tile_rows(regions[j])
        if j >= nb:
            return pl.BlockSpec(tile, lambda i, ids: (i, 0))
        if BIG[ws[j]][3] == 1:
            return pl.BlockSpec(tile, lambda i, ids: (i, ids[1]))
        return pl.BlockSpec(tile, lambda i, ids: (ids[1] * EXCH_TILES + i, 0))

    def out_spec(j):
        tile = _tile_rows(regions[j])
        if j < nb and BIG[ws[j]][3] == 0:
            return pl.BlockSpec(tile, lambda i, ids: (i, ids[0]))
        return pl.BlockSpec(tile, lambda i, ids: (ids[0] * EXCH_TILES + i, 0))

    got_spec = lambda j: pl.BlockSpec((3,) + _tile_rows(regions[j]), lambda i, ids: (0, i, 0))
    return list(pl.pallas_call(
        body, name="chip_sum",
        grid_spec=pltpu.PrefetchScalarGridSpec(
            num_scalar_prefetch=1, grid=(EXCH_TILES,),
            in_specs=[own_spec(j) for j in range(n)] + [got_spec(j) for j in range(n)],
            out_specs=[out_spec(j) for j in range(n)]),
        out_shape=[jax.ShapeDtypeStruct(_shard_shape(*BIG[w][1:]), F32) for w in ws]
        + [jax.ShapeDtypeStruct((SMALL_ROWS, LANES), F32)] * (n - nb),
        compiler_params=pltpu.CompilerParams(dimension_semantics=("parallel",)),
    )(ids, *sums, *arrived))


def _adamw(ws, gs, ms, vs):
    n = len(ws)
    c1 = 1.0 - ADAM_B1 ** ADAM_STEP
    c2 = 1.0 - ADAM_B2 ** ADAM_STEP

    def body(*refs):
        for j in range(n):
            w, g, m, v = (refs[k * n + j][...] for k in range(4))
            m = ADAM_B1 * m + (1.0 - ADAM_B1) * g
            v = ADAM_B2 * v + (1.0 - ADAM_B2) * (g * g)
            refs[4 * n + j][...] = -ADAM_LR * ((m / c1) / (jnp.sqrt(v / c2) + ADAM_EPS) + ADAM_WD * w)
            refs[5 * n + j][...] = m
            refs[6 * n + j][...] = v

    specs = [pl.BlockSpec(_tile_rows(w.shape), lambda i: (i, 0)) for w in ws]
    shapes = [jax.ShapeDtypeStruct(w.shape, F32) for w in ws]
    out = pl.pallas_call(
        body, name="adamw", grid=(EXCH_TILES,),
        in_specs=specs * 4, out_specs=specs * 3, out_shape=shapes * 3,
        compiler_params=pltpu.CompilerParams(dimension_semantics=("parallel",)),
    )(*ws, *gs, *ms, *vs)
    return out[:n], out[n:2 * n], out[2 * n:]


def kernel(x, norm_mix_pre, norm_mix_post, w_in, ret_gn_gain, ssm_lambda_re, ssm_lambda_im, ssm_log_dt, ssm_b_re, ssm_b_im, ssm_c_re, ssm_c_im, ssm_d, w_glu, w_out, norm_mlp_pre, norm_mlp_post, w_ff1, w_ff2, loss_target, m_norm_mix_pre, m_norm_mix_post, m_w_in, m_ret_gn_gain, m_ssm_lambda_re, m_ssm_lambda_im, m_ssm_log_dt, m_ssm_b_re, m_ssm_b_im, m_ssm_c_re, m_ssm_c_im, m_ssm_d, m_w_glu, m_w_out, m_norm_mlp_pre, m_norm_mlp_post, m_w_ff1, m_w_ff2, v_norm_mix_pre, v_norm_mix_post, v_w_in, v_ret_gn_gain, v_ssm_lambda_re, v_ssm_lambda_im, v_ssm_log_dt, v_ssm_b_re, v_ssm_b_im, v_ssm_c_re, v_ssm_c_im, v_ssm_d, v_w_glu, v_w_out, v_norm_mlp_pre, v_norm_mlp_post, v_w_ff1, v_w_ff2):
    given = dict(locals())
    order = ["norm_mix_pre", "norm_mix_post", "w_in", "ret_gn_gain", "ssm_lambda_re", "ssm_lambda_im", "ssm_log_dt",
             "ssm_b_re", "ssm_b_im", "ssm_c_re", "ssm_c_im", "ssm_d", "w_glu", "w_out", "norm_mlp_pre", "norm_mlp_post",
             "w_ff1", "w_ff2"]
    big_names = [name for name, _, _, _ in BIG]
    ids = jnp.stack([lax.axis_index("c"), 2 * lax.axis_index("x") + lax.axis_index("y")]).astype(jnp.int32)

    weights = _cast_place(ids, [given[name][0] for name in big_names])
    w_in = _run_exchange("gather_w_in_send", _x_gather_send((0,), weights[:1]))
    w_in = _run_exchange("gather_w_in_pass", _x_gather_pass((0,), w_in))
    small_w = {name: given[name][0] for name, _ in SMALL}
    dx, early_grads, late_grads, small_grads = _local_step(x[0], loss_target[0], small_w, w_in + weights[1:], ids)

    small_packed = _pack_small(small_grads)
    got = _run_exchange("pair_exchange", _x_pair(EARLY, early_grads, small_packed))
    sums, sums_bf16 = _pair_sum(ids, EARLY, early_grads, got, small_packed)
    arrived = _run_exchange("chip_exchange", _x_chip(EARLY, sums_bf16, sums[-1]))
    parts = _chip_sum(ids, EARLY, sums, arrived)
    shared = _run_exchange("pair_share", _x_share(EARLY, parts[:-1], parts[-1]))
    nw = len(BIG)
    reduced = shared[:len(EARLY)] + late_grads + shared[len(EARLY):]

    pack = lambda prefix: _pack_small({name: given[prefix + name][0] for name, _ in SMALL})
    ws = [given[name][0] for name in big_names] + [pack("")]
    ms = [given["m_" + name][0] for name in big_names] + [pack("m_")]
    vs = [given["v_" + name][0] for name in big_names] + [pack("v_")]
    deltas, new_m, new_v = _adamw(ws, list(reduced), ms, vs)

    def unpacked(arrays):
        tree = _unpack_small(arrays[nw])
        tree.update({name: arrays[j][None] for j, name in enumerate(big_names)})
        return [tree[name] for name in order]

    loss = _unpack_small(reduced[nw])["loss_sum"].reshape(())
    return (loss, dx[None], *unpacked(list(reduced)), *unpacked(deltas), *unpacked(new_m), *unpacked(new_v))
```

```python
import functools
import math

import jax
import jax.numpy as jnp
import numpy as np
from jax import lax
from jax.experimental import pallas as pl
from jax.experimental.pallas import tpu as pltpu

F32 = jnp.float32
BF16 = jnp.bfloat16

D_MODEL = 1024
RET_WIDTH = 512
RET_HEADS = 4
HEAD_DIM = 128
RET_CHUNK = 128
ROPE_BASE = 10000.0
SSM_WIDTH = 512
SSM_GROUP = 16
SSM_GROUPS = 32
SSM_STATE = 64
N_STATES = SSM_GROUPS * SSM_STATE
D_FF = 4096
IN_COLS = 4 * RET_WIDTH + SSM_WIDTH
NORM_EPS = 1e-6
K_SCALE = HEAD_DIM ** -0.5

ADAM_LR = 0.001
ADAM_B1 = 0.9
ADAM_B2 = 0.999
ADAM_EPS = 1e-08
ADAM_WD = 0.01
ADAM_STEP = 10

LANES = 128
SUBLANES = 8
VMEM_LIMIT = 56 * 2 ** 20

TOK_TILE = 512
MLP_TILE = 256
RET_CHUNKS_PER_STEP = 4
S5_TILE = 256
S5_STEPS = S5_TILE // SUBLANES
S5_COLBLK = N_STATES // LANES
S5_GROUP = 8

N_CHIPS = 4
MESH = pl.DeviceIdType.MESH


def _dot(a, b):
    return jnp.dot(a, b, preferred_element_type=F32)


def _dot_nt(a, b):
    return lax.dot_general(a, b, (((1,), (1,)), ((), ())), preferred_element_type=F32)


def _dot_tn(a, b):
    return lax.dot_general(a, b, (((0,), (0,)), ((), ())), preferred_element_type=F32)


def _sigmoid(x):
    return 1.0 / (1.0 + jnp.exp(-x))


_GELU_C = math.sqrt(2.0 / math.pi)


def _gelu(x):
    return 0.5 * x * (1.0 + jnp.tanh(_GELU_C * (x + 0.044715 * (x * x * x))))


def _gelu_grad(x):
    t = jnp.tanh(_GELU_C * (x + 0.044715 * (x * x * x)))
    return 0.5 * (1.0 + t) + 0.5 * x * (1.0 - t * t) * (_GELU_C * (1.0 + 3.0 * 0.044715 * (x * x)))


def _rms(x):
    r = lax.rsqrt(jnp.mean(x * x, axis=-1, keepdims=True) + NORM_EPS)
    return x * r, r


def _rms_bwd(n, r, gain, dy):
    dn = dy * gain
    dx = r * (dn - n * jnp.mean(dn * n, axis=-1, keepdims=True))
    return dx, jnp.sum(dy * n, axis=0, keepdims=True)


def _full(shape):
    nd = len(shape)
    return pl.BlockSpec(shape, lambda i, _n=nd: (0,) * _n, pipeline_mode=pl.Buffered(1))


def _rows(tile, width):
    return pl.BlockSpec((tile, width), lambda i: (i, 0))


def _rows_rev(tile, width, n):
    return pl.BlockSpec((tile, width), lambda i, _n=n: (_n - 1 - i, 0))


def _params(sem):
    return pltpu.CompilerParams(dimension_semantics=(sem,), vmem_limit_bytes=VMEM_LIMIT)


ANY = pl.BlockSpec(memory_space=pl.ANY)


class _Exchange:
    def __init__(self, operands, out_shapes, aliases, n, copies):
        self.operands, self.out_shapes, self.aliases, self.n, self.copies = list(operands), list(out_shapes), aliases, n, copies


def _call(body, name, grid, in_specs, out_specs, out_shape, scratch_shapes, args, exchange=None):
    if exchange is None:
        outs = pl.pallas_call(body, name=name, grid=grid, in_specs=in_specs, out_specs=out_specs, out_shape=out_shape,
                              scratch_shapes=scratch_shapes, compiler_params=_params("arbitrary"))(*args)
        return list(outs), []
    n_in, n_out, n_scr = len(in_specs), len(out_specs), len(scratch_shapes)
    k_in, k_out = len(exchange.operands), len(exchange.out_shapes)
    last = grid[0] - 1

    def hosted(*refs):
        own_in, rest = refs[:n_in], refs[n_in:]
        ex_in, rest = rest[:k_in], rest[k_in:]
        own_out, rest = rest[:n_out], rest[n_out:]
        ex_out, rest = rest[:k_out], rest[k_out:]
        own_scr, (send_sems, recv_sems) = rest[:n_scr], rest[n_scr:]

        @pl.when(pl.program_id(0) == 0)
        def _():
            for cp in exchange.copies(ex_in, ex_out, send_sems, recv_sems):
                cp.start()

        body(*own_in, *own_out, *own_scr)

        @pl.when(pl.program_id(0) == last)
        def _():
            for cp in exchange.copies(ex_in, ex_out, send_sems, recv_sems):
                cp.wait()

    outs = pl.pallas_call(
        hosted, name=name, grid=grid, in_specs=list(in_specs) + [ANY] * k_in, out_specs=list(out_specs) + [ANY] * k_out,
        out_shape=list(out_shape) + exchange.out_shapes,
        scratch_shapes=list(scratch_shapes) + [pltpu.SemaphoreType.DMA((exchange.n,)), pltpu.SemaphoreType.DMA((exchange.n,))],
        input_output_aliases={n_in + i: n_out + o for i, o in exchange.aliases.items()},
        compiler_params=_params("arbitrary"),
    )(*args, *exchange.operands)
    return list(outs[:n_out]), list(outs[n_out:])


def _rope(t, cos2, sin2):
    return t * cos2 + pltpu.roll(t, HEAD_DIM // 2, 1) * sin2


def _rope_bwd(d, cos2, sin2):
    return d * cos2 + pltpu.roll(d * sin2, HEAD_DIM // 2, 1)


def _inproj_fwd(x, gain, w_in, cos2, sin2, exchange=None):
    seq = x.shape[0]

    def body(x_ref, g_ref, w_ref, c_ref, s_ref, q_ref, k_ref, v_ref, gate_ref, u_ref):
        n, _ = _rms(x_ref[...])
        h = (n * g_ref[...]).astype(BF16)
        proj = _dot(h, w_ref[...])
        c = c_ref[...]
        s = s_ref[...]
        for hd in range(RET_HEADS):
            lo = hd * HEAD_DIM
            q_ref[:, lo:lo + HEAD_DIM] = _rope(proj[:, lo:lo + HEAD_DIM], c, s).astype(BF16)
            kh = proj[:, RET_WIDTH + lo:RET_WIDTH + lo + HEAD_DIM]
            k_ref[:, lo:lo + HEAD_DIM] = (_rope(kh, c, s) * K_SCALE).astype(BF16)
        v_ref[...] = proj[:, 2 * RET_WIDTH:3 * RET_WIDTH].astype(BF16)
        gate_ref[...] = proj[:, 3 * RET_WIDTH:4 * RET_WIDTH]
        u_ref[...] = proj[:, 4 * RET_WIDTH:]

    t = TOK_TILE
    half = lambda dt: jax.ShapeDtypeStruct((seq, RET_WIDTH), dt)
    return _call(
        body, "inproj_fwd", (seq // t,),
        [_rows(t, D_MODEL), _full((1, D_MODEL)), _full((D_MODEL, IN_COLS)), _rows(t, HEAD_DIM), _rows(t, HEAD_DIM)],
        [_rows(t, RET_WIDTH)] * 5, [half(BF16), half(BF16), half(BF16), half(F32), half(F32)], [],
        (x, gain, w_in, cos2, sin2), exchange)


def _inproj_bwd(x, gain, w_in, cos2, sin2, dq, dk, dv, dgate, du, dres, exchange=None):
    seq = x.shape[0]
    t = TOK_TILE

    def body(x_ref, g_ref, w_ref, c_ref, s_ref, dq_ref, dk_ref, dv_ref, dgate_ref, du_ref, dres_ref,
             dx_ref, dw_ref, dg_ref, dproj):
        @pl.when(pl.program_id(0) == 0)
        def _():
            dw_ref[...] = jnp.zeros_like(dw_ref)
            dg_ref[...] = jnp.zeros_like(dg_ref)

        c = c_ref[...]
        s = s_ref[...]
        for hd in range(RET_HEADS):
            lo = hd * HEAD_DIM
            dproj[:, lo:lo + HEAD_DIM] = _rope_bwd(dq_ref[:, lo:lo + HEAD_DIM], c, s).astype(BF16)
            dproj[:, RET_WIDTH + lo:RET_WIDTH + lo + HEAD_DIM] = _rope_bwd(
                dk_ref[:, lo:lo + HEAD_DIM] * K_SCALE, c, s).astype(BF16)
        dproj[:, 2 * RET_WIDTH:3 * RET_WIDTH] = dv_ref[...].astype(BF16)
        dproj[:, 3 * RET_WIDTH:4 * RET_WIDTH] = dgate_ref[...].astype(BF16)
        dproj[:, 4 * RET_WIDTH:] = du_ref[...].astype(BF16)
        dp = dproj[...]
        n, r = _rms(x_ref[...])
        gain_v = g_ref[...]
        h = (n * gain_v).astype(BF16)
        dw_ref[...] += _dot_tn(h, dp)
        dh = _dot_nt(dp, w_ref[...])
        dx, dgs = _rms_bwd(n, r, gain_v, dh)
        dg_ref[...] += dgs
        dx_ref[...] = dres_ref[...] + dx

    return _call(
        body, "inproj_bwd", (seq // t,),
        [_rows(t, D_MODEL), _full((1, D_MODEL)), _full((D_MODEL, IN_COLS)), _rows(t, HEAD_DIM), _rows(t, HEAD_DIM)]
        + [_rows(t, RET_WIDTH)] * 5 + [_rows(t, D_MODEL)],
        [_rows(t, D_MODEL), _full((D_MODEL, IN_COLS)), _full((1, D_MODEL))],
        [jax.ShapeDtypeStruct((seq, D_MODEL), F32), jax.ShapeDtypeStruct((D_MODEL, IN_COLS), F32),
         jax.ShapeDtypeStruct((1, D_MODEL), F32)],
        [pltpu.VMEM((t, IN_COLS), BF16)],
        (x, gain, w_in, cos2, sin2, dq, dk, dv, dgate, du, dres), exchange)


def _ret_consts():
    c = RET_CHUNK
    f32 = np.float32
    log_gamma = np.log(f32(1.0) - np.exp(np.linspace(math.log(1.0 / 32), math.log(1.0 / 512), RET_HEADS, dtype=f32))).astype(f32)
    idx = np.arange(c, dtype=f32)
    diff = idx[:, None] - idx[None, :]
    decay = np.where(diff[None] >= 0, np.exp(np.maximum(diff, f32(0.0))[None] * log_gamma[:, None, None]), f32(0.0))
    zeta = np.exp((c - 1 - idx)[None, :] * log_gamma[:, None])
    xi = np.exp((idx + f32(1.0))[None, :] * log_gamma[:, None])
    g_chunk = np.exp(f32(c) * log_gamma)
    wide = lambda rowvals: jnp.asarray(np.broadcast_to(rowvals[:, :, None], (RET_HEADS, c, c)).astype(f32))
    return (jnp.asarray(decay.astype(f32)), wide(xi), wide(zeta),
            jnp.asarray(np.broadcast_to(g_chunk[:, None, None], (RET_HEADS, c, c)).astype(f32)))


def _rope_tables(seq):
    f32 = np.float32
    half = HEAD_DIM // 2
    inv_freq = np.power(f32(ROPE_BASE), -np.arange(half, dtype=f32) / f32(half)).astype(f32)
    ang = (np.arange(seq, dtype=f32)[:, None] * inv_freq[None, :]).astype(f32)
    cos, sin = np.cos(ang).astype(f32), np.sin(ang).astype(f32)
    return jnp.asarray(np.concatenate([cos, cos], axis=1)), jnp.asarray(np.concatenate([-sin, sin], axis=1))


def _bmm(a, b):
    return lax.dot_general(a, b, (((2,), (1,)), ((0,), (0,))), preferred_element_type=F32)


def _bmm_nt(a, b):
    return lax.dot_general(a, b, (((2,), (2,)), ((0,), (0,))), preferred_element_type=F32)


def _bmm_tn(a, b):
    return lax.dot_general(a, b, (((1,), (1,)), ((0,), (0,))), preferred_element_type=F32)


def _ret_blocks(ref):
    return jnp.stack([ref[cc * RET_CHUNK:(cc + 1) * RET_CHUNK, hd * HEAD_DIM:(hd + 1) * HEAD_DIM]
                      for cc in range(RET_CHUNKS_PER_STEP) for hd in range(RET_HEADS)])


def _ret_unblock(ref, blocks):
    for cc in range(RET_CHUNKS_PER_STEP):
        for hd in range(RET_HEADS):
            ref[cc * RET_CHUNK:(cc + 1) * RET_CHUNK, hd * HEAD_DIM:(hd + 1) * HEAD_DIM] = blocks[cc * RET_HEADS + hd].astype(ref.dtype)


def _per_chunk(ref):
    return jnp.concatenate([ref[...]] * RET_CHUNKS_PER_STEP, axis=0)


def _ret_gain(gain_ref):
    return jnp.stack([gain_ref[:, hd * HEAD_DIM:(hd + 1) * HEAD_DIM]
                      for _ in range(RET_CHUNKS_PER_STEP) for hd in range(RET_HEADS)])


def _ret_fwd(q, k, v, gate, gn_gain, consts, exchange=None):
    seq = q.shape[0]
    cps = RET_CHUNKS_PER_STEP
    t = cps * RET_CHUNK
    n_chunks = seq // RET_CHUNK
    dec, xi, zeta, gch = consts

    def body(q_ref, k_ref, v_ref, gate_ref, gain_ref, dec_ref, xi_ref, zeta_ref, gch_ref, y_ref, rprev_ref, state):
        @pl.when(pl.program_id(0) == 0)
        def _():
            state[...] = jnp.zeros_like(state)

        qb, kb, vb = _ret_blocks(q_ref), _ret_blocks(k_ref), _ret_blocks(v_ref)
        s = _bmm_nt(qb, kb) * _per_chunk(dec_ref)
        vz = (vb.astype(F32) * _per_chunk(zeta_ref)).astype(BF16)
        incr = _bmm_tn(kb, vz)
        gch_v = gch_ref[...]
        rp = state[...]
        before = []
        for cc in range(cps):
            before.append(rp.astype(BF16))
            rprev_ref[cc] = before[-1]
            rp = gch_v * rp + incr[cc * RET_HEADS:(cc + 1) * RET_HEADS]
        state[...] = rp
        qx = (qb.astype(F32) * _per_chunk(xi_ref)).astype(BF16)
        o = _bmm(s.astype(BF16), vb) + _bmm(qx, jnp.concatenate(before, axis=0))
        oc = o - jnp.mean(o, axis=-1, keepdims=True)
        on = oc * lax.rsqrt(jnp.mean(oc * oc, axis=-1, keepdims=True) + NORM_EPS)
        g = _ret_blocks(gate_ref)
        _ret_unblock(y_ref, g * _sigmoid(g) * (on * _ret_gain(gain_ref)))

    cst = _full((RET_HEADS, RET_CHUNK, RET_CHUNK))
    return _call(
        body, "ret_fwd", (seq // t,),
        [_rows(t, RET_WIDTH)] * 4 + [_full((1, RET_WIDTH)), cst, cst, cst, cst],
        [_rows(t, RET_WIDTH), pl.BlockSpec((cps, RET_HEADS, HEAD_DIM, HEAD_DIM), lambda i: (i, 0, 0, 0))],
        [jax.ShapeDtypeStruct((seq, RET_WIDTH), BF16), jax.ShapeDtypeStruct((n_chunks, RET_HEADS, HEAD_DIM, HEAD_DIM), BF16)],
        [pltpu.VMEM((RET_HEADS, HEAD_DIM, HEAD_DIM), F32)],
        (q, k, v, gate, gn_gain, dec, xi, zeta, gch), exchange)


def _ret_bwd(q, k, v, gate, gn_gain, consts, rprev, dy_ret, exchange=None):
    seq = q.shape[0]
    cps = RET_CHUNKS_PER_STEP
    t = cps * RET_CHUNK
    nt = seq // t
    dec, xi, zeta, gch = consts

    def body(q_ref, k_ref, v_ref, gate_ref, gain_ref, dec_ref, xi_ref, zeta_ref, gch_ref, rprev_ref, dyr_ref,
             dq_ref, dk_ref, dv_ref, dgate_ref, dgain_ref, dstate):
        @pl.when(pl.program_id(0) == 0)
        def _():
            dstate[...] = jnp.zeros_like(dstate)
            dgain_ref[...] = jnp.zeros_like(dgain_ref)

        nb = cps * RET_HEADS
        qb, kb, vb = _ret_blocks(q_ref), _ret_blocks(k_ref), _ret_blocks(v_ref)
        dec_b, xi_b, zeta_b = _per_chunk(dec_ref), _per_chunk(xi_ref), _per_chunk(zeta_ref)
        rpb = rprev_ref[...].reshape(nb, HEAD_DIM, HEAD_DIM)
        sdb = (_bmm_nt(qb, kb) * dec_b).astype(BF16)
        qx = (qb.astype(F32) * xi_b).astype(BF16)
        o = _bmm(sdb, vb) + _bmm(qx, rpb)
        oc = o - jnp.mean(o, axis=-1, keepdims=True)
        rstd = lax.rsqrt(jnp.mean(oc * oc, axis=-1, keepdims=True) + NORM_EPS)
        on = oc * rstd
        g = _ret_blocks(gate_ref)
        sg = _sigmoid(g)
        gain_b = _ret_gain(gain_ref)
        dyr = _ret_blocks(dyr_ref)
        _ret_unblock(dgate_ref, dyr * (on * gain_b) * (sg * (1.0 + g * (1.0 - sg))))
        dy = dyr * (g * sg)
        dgain = jnp.sum(dy * on, axis=1, keepdims=True)
        for hd in range(RET_HEADS):
            part = dgain[hd]
            for cc in range(1, cps):
                part = part + dgain[cc * RET_HEADS + hd]
            dgain_ref[:, hd * HEAD_DIM:(hd + 1) * HEAD_DIM] += part
        don = dy * gain_b
        do = rstd * (don - jnp.mean(don, axis=-1, keepdims=True) - on * jnp.mean(don * on, axis=-1, keepdims=True))
        dob = do.astype(BF16)
        ds = (_bmm_nt(dob, vb) * dec_b).astype(BF16)
        _ret_unblock(dq_ref, _bmm(ds, kb) + _bmm_nt(dob, rpb) * xi_b)
        dk = _bmm_tn(ds, qb)
        dv = _bmm_tn(sdb, dob)
        dstate_local = _bmm_tn(qx, dob)
        vz = (vb.astype(F32) * zeta_b).astype(BF16)
        gch_v = gch_ref[...]
        zeta_v = zeta_ref[...]
        gh = dstate[...]
        dk_extra, dv_extra = [None] * cps, [None] * cps
        for cc in reversed(range(cps)):
            sl = slice(cc * RET_HEADS, (cc + 1) * RET_HEADS)
            gb = gh.astype(BF16)
            dk_extra[cc] = _bmm_nt(vz[sl], gb)
            dv_extra[cc] = _bmm(kb[sl], gb) * zeta_v
            gh = dstate_local[sl] + gch_v * gh
        dstate[...] = gh
        _ret_unblock(dk_ref, dk + jnp.concatenate(dk_extra, axis=0))
        _ret_unblock(dv_ref, dv + jnp.concatenate(dv_extra, axis=0))

    cst = _full((RET_HEADS, RET_CHUNK, RET_CHUNK))
    rev = _rows_rev(t, RET_WIDTH, nt)
    act = jax.ShapeDtypeStruct((seq, RET_WIDTH), F32)
    return _call(
        body, "ret_bwd", (nt,),
        [rev] * 4 + [_full((1, RET_WIDTH)), cst, cst, cst, cst,
                     pl.BlockSpec((cps, RET_HEADS, HEAD_DIM, HEAD_DIM), lambda i: (nt - 1 - i, 0, 0, 0)), rev],
        [rev] * 4 + [_full((1, RET_WIDTH))], [act, act, act, act, jax.ShapeDtypeStruct((1, RET_WIDTH), F32)],
        [pltpu.VMEM((RET_HEADS, HEAD_DIM, HEAD_DIM), F32)],
        (q, k, v, gate, gn_gain, dec, xi, zeta, gch, rprev, dy_ret), exchange)


MXU_TILE = 256
S5_STATE_TILES = 2 * N_STATES // MXU_TILE
S5_CHANNELS_PER_TILE = MXU_TILE // SSM_STATE * SSM_GROUP
S5_TILES_SHAPE = (S5_STATE_TILES, LANES, MXU_TILE)


def _chan_block(kk):
    return ((kk % (S5_STATE_TILES // 2)) * S5_CHANNELS_PER_TILE) // LANES * LANES


def _s5_tiles(blocks_re, blocks_im):
    per = MXU_TILE // SSM_STATE
    half = S5_STATE_TILES // 2
    eye = jnp.eye(per, dtype=F32)
    in_half = (jnp.arange(half) % (LANES // S5_CHANNELS_PER_TILE))[:, None] == jnp.arange(LANES // S5_CHANNELS_PER_TILE)[None, :]
    out = []
    for blk in (blocks_re, blocks_im):
        t = blk.reshape(half, per, SSM_GROUP, 1, SSM_STATE) * eye[None, :, None, :, None]
        t = t.reshape(half, 1, S5_CHANNELS_PER_TILE, MXU_TILE)
        out.append(jnp.where(in_half[:, :, None, None], t, 0.0).reshape(half, LANES, MXU_TILE))
    return jnp.concatenate(out, axis=0)


def _s5_discretise(lam_re, lam_im, log_dt, b_re, b_im, c_re, c_im):
    lam = lax.complex(jnp.minimum(lam_re, -1e-4), lam_im)
    dt = jnp.exp(log_dt)[:, None]
    lam_bar = jnp.exp(lam * dt)
    b_bar = ((lam_bar - 1.0) / lam)[:, :, None] * lax.complex(b_re, b_im)
    b_tiles = _s5_tiles(jnp.swapaxes(jnp.real(b_bar), 1, 2), jnp.swapaxes(jnp.imag(b_bar), 1, 2))
    return jnp.real(lam_bar), jnp.imag(lam_bar), b_tiles, _s5_tiles(c_re, -c_im)


def _s5_tables(lbr, lbi):
    lr = lbr.reshape(1, N_STATES)
    li = lbi.reshape(1, N_STATES)
    pr, pi = lr, li
    while pr.shape[0] < S5_STEPS:
        top_r, top_i = pr[-1:], pi[-1:]
        pr, pi = (jnp.concatenate([pr, pr * top_r - pi * top_i], axis=0), jnp.concatenate([pi, pr * top_i + pi * top_r], axis=0))
    bc = lambda a: jnp.broadcast_to(a, (SUBLANES, N_STATES))
    rep = lambda rows: jnp.broadcast_to(rows[:, None, :], (S5_STEPS, SUBLANES, N_STATES)).reshape(S5_STEPS * SUBLANES, N_STATES)
    return bc(lr), bc(li), rep(pr), rep(pi)


def _cmul(ar, ai, br, bi):
    return ar * br - ai * bi, ar * bi + ai * br


def _seg(i):
    return pl.ds(pl.multiple_of(i * SUBLANES, SUBLANES), SUBLANES)


def _scan_order():
    r = np.arange(S5_TILE)
    token = (r % SUBLANES) * S5_STEPS + r // SUBLANES
    p = (token[:, None] == np.arange(S5_TILE)[None, :]).astype(np.float32)
    return jnp.asarray(p, dtype=BF16), jnp.asarray(p.T, dtype=BF16)


def _to_scan_order(p_ref, rows_bf16):
    return _dot(p_ref[...], rows_bf16).astype(BF16)


def _from_scan_order(pt_ref, rows):
    hi = rows.astype(BF16)
    rest = rows - hi.astype(F32)
    mid = rest.astype(BF16)
    lo = (rest - mid.astype(F32)).astype(BF16)
    pt = pt_ref[...]
    return _dot(pt, hi) + _dot(pt, mid) + _dot(pt, lo)


def _s5_channels_to_states(chan_b, m_ref, xs):
    for kk in range(S5_STATE_TILES):
        lo = _chan_block(kk)
        res = _dot(chan_b[:, lo:lo + LANES], m_ref[kk])
        xs[2 * kk] = res[:, :LANES]
        xs[2 * kk + 1] = res[:, LANES:]


def _s5_states_to_channels(sb_ref, m_ref):
    out = []
    for lo in range(0, SSM_WIDTH, LANES):
        acc = None
        for kk in range(S5_STATE_TILES):
            if _chan_block(kk) == lo:
                term = _dot_nt(sb_ref[:, kk * MXU_TILE:(kk + 1) * MXU_TILE], m_ref[kk])
                acc = term if acc is None else acc + term
        out.append(acc)
    return jnp.concatenate(out, axis=1)


def _s5_outer_acc(acc_ref, chan_b, states_ref):
    for kk in range(S5_STATE_TILES):
        lo = _chan_block(kk)
        acc_ref[kk] += _dot_tn(chan_b[:, lo:lo + LANES], states_ref[:, kk * MXU_TILE:(kk + 1) * MXU_TILE])


def _s5_scan_tile(xs, lr_ref, li_ref, pr_ref, pi_ref, carry_re, carry_im, conj, reverse, prev_of=None):
    sgn = -1.0 if conj else 1.0
    seg_in_re, seg_in_im, sums_re, sums_im = [], [], [], []
    for grp in range(S5_COLBLK // S5_GROUP):
        blks = [grp * S5_GROUP + j for j in range(S5_GROUP)]
        lrs = [lr_ref[:, b * LANES:(b + 1) * LANES] for b in blks]
        lis = [sgn * li_ref[:, b * LANES:(b + 1) * LANES] for b in blks]

        def step(it, carry, blks=blks, lrs=lrs, lis=lis):
            i = (S5_STEPS - 1 - it) if reverse else it
            out_r, out_i = [], []
            for j, b in enumerate(blks):
                xr, xi = _cmul(lrs[j], lis[j], carry[j], carry[S5_GROUP + j])
                xr = xr + xs[b, _seg(i), :]
                xi = xi + xs[S5_COLBLK + b, _seg(i), :]
                xs[b, _seg(i), :] = xr
                xs[S5_COLBLK + b, _seg(i), :] = xi
                out_r.append(xr)
                out_i.append(xi)
            return tuple(out_r + out_i)

        zeros = tuple(jnp.zeros((SUBLANES, LANES), F32) for _ in range(2 * S5_GROUP))
        ends = lax.fori_loop(0, S5_STEPS, step, zeros, unroll=4)

        ins_r, ins_i = [], []
        for j, b in enumerate(blks):
            cols = slice(b * LANES, (b + 1) * LANES)
            pr = pr_ref[SUBLANES * S5_STEPS - 1:SUBLANES * S5_STEPS, cols]
            pi = sgn * pi_ref[SUBLANES * S5_STEPS - 1:SUBLANES * S5_STEPS, cols]
            cur_r = carry_re[:, cols]
            cur_i = carry_im[:, cols]
            row_id = lax.broadcasted_iota(jnp.int32, (SUBLANES, LANES), 0)
            in_r = jnp.zeros((SUBLANES, LANES), F32)
            in_i = jnp.zeros((SUBLANES, LANES), F32)
            order = range(SUBLANES - 1, -1, -1) if reverse else range(SUBLANES)
            for sgm in order:
                in_r = jnp.where(row_id == sgm, cur_r, in_r)
                in_i = jnp.where(row_id == sgm, cur_i, in_i)
                mr, mi = _cmul(pr, pi, cur_r, cur_i)
                cur_r = mr + ends[j][sgm:sgm + 1, :]
                cur_i = mi + ends[S5_GROUP + j][sgm:sgm + 1, :]
            carry_re[:, cols] = cur_r
            carry_im[:, cols] = cur_i
            ins_r.append(in_r)
            ins_i.append(in_i)

        def fix(it, c, blks=blks, ins_r=ins_r, ins_i=ins_i):
            pw = (S5_STEPS - 1 - it) if reverse else it
            out = []
            for j, b in enumerate(blks):
                cols = slice(b * LANES, (b + 1) * LANES)
                prow = pl.ds(pl.multiple_of(pw * SUBLANES, SUBLANES), SUBLANES)
                fr, fi = _cmul(pr_ref[prow, cols], sgn * pi_ref[prow, cols], ins_r[j], ins_i[j])
                vr = xs[b, _seg(it), :] + fr
                vi = xs[S5_COLBLK + b, _seg(it), :] + fi
                xs[b, _seg(it), :] = vr
                xs[S5_COLBLK + b, _seg(it), :] = vi
                if prev_of is not None:
                    opr, opi, sr, si = c[4 * j:4 * j + 4]
                    out += [prev_of[0][b, _seg(it), :], prev_of[0][S5_COLBLK + b, _seg(it), :],
                            sr + vr * opr + vi * opi, si + vi * opr - vr * opi]
            return tuple(out)

        init = ()
        if prev_of is not None:
            zero = jnp.zeros((SUBLANES, LANES), F32)
            for b in blks:
                init += (prev_of[1][b], prev_of[2][b], zero, zero)
        done = lax.fori_loop(0, S5_STEPS, fix, init, unroll=4)
        for j in range(len(blks) if prev_of is not None else 0):
            sums_re.append(done[4 * j + 2])
            sums_im.append(done[4 * j + 3])
        seg_in_re += ins_r
        seg_in_im += ins_i
    return seg_in_re, seg_in_im, sums_re, sums_im


def _s5_pack(xs, dst):
    for b in range(2 * S5_COLBLK):
        dst[:, b * LANES:(b + 1) * LANES] = xs[b].astype(BF16)


def _s5_fwd(u, b_blk, c_blk, tables, d_skip, exchange=None):
    seq = u.shape[0]
    t = S5_TILE
    nt = seq // t
    lr8, li8, p_re, p_im = tables

    p, pt = _scan_order()

    def body(u_ref, p_ref, pt_ref, b_ref, c_ref, lr_ref, li_ref, pr_ref, pi_ref, d_ref, y_ref, start_ref,
             xs, xb, carry_re, carry_im):
        @pl.when(pl.program_id(0) == 0)
        def _():
            carry_re[...] = jnp.zeros_like(carry_re)
            carry_im[...] = jnp.zeros_like(carry_im)

        start_ref[0, 0:1, :] = carry_re[...]
        start_ref[0, 1:2, :] = carry_im[...]
        uv = u_ref[...]
        _s5_channels_to_states(_to_scan_order(p_ref, uv.astype(BF16)), b_ref, xs)
        _s5_scan_tile(xs, lr_ref, li_ref, pr_ref, pi_ref, carry_re, carry_im, conj=False, reverse=False)
        _s5_pack(xs, xb)
        y_ref[...] = _from_scan_order(pt_ref, _s5_states_to_channels(xb, c_ref)) + d_ref[...] * uv

    return _call(
        body, "s5_fwd", (nt,),
        [_rows(t, SSM_WIDTH), _full((t, t)), _full((t, t)), _full(S5_TILES_SHAPE), _full(S5_TILES_SHAPE),
         _full((SUBLANES, N_STATES)), _full((SUBLANES, N_STATES)), _full((SUBLANES * S5_STEPS, N_STATES)),
         _full((SUBLANES * S5_STEPS, N_STATES)), _full((1, SSM_WIDTH))],
        [_rows(t, SSM_WIDTH), pl.BlockSpec((1, 2, N_STATES), lambda i: (i, 0, 0))],
        [jax.ShapeDtypeStruct((seq, SSM_WIDTH), F32), jax.ShapeDtypeStruct((nt, 2, N_STATES), F32)],
        [pltpu.VMEM((2 * S5_COLBLK, t, LANES), F32), pltpu.VMEM((t, 2 * N_STATES), BF16),
         pltpu.VMEM((1, N_STATES), F32), pltpu.VMEM((1, N_STATES), F32)],
        (u, p, pt, b_blk, c_blk, lr8, li8, p_re, p_im, d_skip), exchange)


def _s5_bwd(u, dy, b_blk, c_blk, tables, d_skip, starts, exchange=None):
    seq = u.shape[0]
    t = S5_TILE
    nt = seq // t
    lr8, li8, p_re, p_im = tables
    p, pt = _scan_order()

    def body(u_ref, dy_ref, p_ref, pt_ref, b_ref, c_ref, lr_ref, li_ref, pr_ref, pi_ref, d_ref, start_ref,
             du_ref, dd_ref, dlam_ref, db_ref, dc_ref,
             xs, as_, xb, ab, carry_re, carry_im, acar_re, acar_im):
        @pl.when(pl.program_id(0) == 0)
        def _():
            acar_re[...] = jnp.zeros_like(acar_re)
            acar_im[...] = jnp.zeros_like(acar_im)
            dd_ref[...] = jnp.zeros_like(dd_ref)
            dlam_ref[...] = jnp.zeros_like(dlam_ref)
            db_ref[...] = jnp.zeros_like(db_ref)
            dc_ref[...] = jnp.zeros_like(dc_ref)

        uv = u_ref[...]
        dyv = dy_ref[...]
        ub = _to_scan_order(p_ref, uv.astype(BF16))
        dyb = _to_scan_order(p_ref, dyv.astype(BF16))
        carry_re[...] = start_ref[0, 0:1, :]
        carry_im[...] = start_ref[0, 1:2, :]
        _s5_channels_to_states(ub, b_ref, xs)
        in_re, in_im, _, _ = _s5_scan_tile(xs, lr_ref, li_ref, pr_ref, pi_ref, carry_re, carry_im, conj=False, reverse=False)
        _s5_channels_to_states(dyb, c_ref, as_)
        _, _, dl_re, dl_im = _s5_scan_tile(as_, lr_ref, li_ref, pr_ref, pi_ref, acar_re, acar_im, conj=True, reverse=True,
                                           prev_of=(xs, in_re, in_im))
        for b in range(S5_COLBLK):
            cols = slice(b * LANES, (b + 1) * LANES)
            dlam_ref[0, :, cols] += dl_re[b]
            dlam_ref[1, :, cols] += dl_im[b]
        _s5_pack(xs, xb)
        _s5_pack(as_, ab)
        du_ref[...] = _from_scan_order(pt_ref, _s5_states_to_channels(ab, b_ref)) + d_ref[...] * dyv
        dd_ref[...] += jnp.sum(dyv * uv, axis=0, keepdims=True)
        _s5_outer_acc(db_ref, ub, ab)
        _s5_outer_acc(dc_ref, dyb, xb)

    rev = _rows_rev(t, SSM_WIDTH, nt)
    vec = lambda: pltpu.VMEM((1, N_STATES), F32)
    outer = S5_TILES_SHAPE
    return _call(
        body, "s5_bwd", (nt,),
        [rev, rev, _full((t, t)), _full((t, t)), _full(S5_TILES_SHAPE), _full(S5_TILES_SHAPE),
         _full((SUBLANES, N_STATES)), _full((SUBLANES, N_STATES)), _full((SUBLANES * S5_STEPS, N_STATES)),
         _full((SUBLANES * S5_STEPS, N_STATES)), _full((1, SSM_WIDTH)),
         pl.BlockSpec((1, 2, N_STATES), lambda i: (nt - 1 - i, 0, 0))],
        [rev, _full((1, SSM_WIDTH)), _full((2, SUBLANES, N_STATES)), _full(outer), _full(outer)],
        [jax.ShapeDtypeStruct((seq, SSM_WIDTH), F32), jax.ShapeDtypeStruct((1, SSM_WIDTH), F32),
         jax.ShapeDtypeStruct((2, SUBLANES, N_STATES), F32), jax.ShapeDtypeStruct(outer, F32), jax.ShapeDtypeStruct(outer, F32)],
        [pltpu.VMEM((2 * S5_COLBLK, t, LANES), F32), pltpu.VMEM((2 * S5_COLBLK, t, LANES), F32),
         pltpu.VMEM((t, 2 * N_STATES), BF16), pltpu.VMEM((t, 2 * N_STATES), BF16), vec(), vec(), vec(), vec()],
        (u, dy, p, pt, b_blk, c_blk, lr8, li8, p_re, p_im, d_skip, starts), exchange)


def _mix_fwd(y_ssm, y_ret, x, gain, w_glu, w_out):
    seq = x.shape[0]
    t = TOK_TILE

    def body(ys_ref, yr_ref, x_ref, g_ref, wg_ref, wo_ref, x1_ref, mix_ref):
        g0 = _gelu(ys_ref[...]).astype(BF16)
        z = _dot(g0, wg_ref[...])
        glu = (z[:, :SSM_WIDTH] * _sigmoid(z[:, SSM_WIDTH:])).astype(BF16)
        mix = _dot(yr_ref[...], wo_ref[:RET_WIDTH, :]) + _dot(glu, wo_ref[RET_WIDTH:, :])
        mix_ref[...] = mix
        n, _ = _rms(mix)
        x1_ref[...] = x_ref[...] + n * g_ref[...]

    act = jax.ShapeDtypeStruct((seq, D_MODEL), F32)
    return pl.pallas_call(
        body, name="mix_fwd", grid=(seq // t,),
        in_specs=[_rows(t, SSM_WIDTH), _rows(t, RET_WIDTH), _rows(t, D_MODEL), _full((1, D_MODEL)),
                  _full((SSM_WIDTH, 2 * SSM_WIDTH)), _full((D_MODEL, D_MODEL))],
        out_specs=[_rows(t, D_MODEL)] * 2, out_shape=[act, act],
        compiler_params=_params("parallel"),
    )(y_ssm, y_ret, x, gain, w_glu, w_out)


def _mix_bwd(dx1, mix, gain, y_ssm, y_ret, w_glu, w_out, exchange=None):
    seq = dx1.shape[0]
    t = TOK_TILE

    def body(dx1_ref, mix_ref, g_ref, ys_ref, yr_ref, wg_ref, wo_ref, dyr_ref, dys_ref, dwo_ref, dwg_ref, dg_ref):
        @pl.when(pl.program_id(0) == 0)
        def _():
            dwo_ref[...] = jnp.zeros_like(dwo_ref)
            dwg_ref[...] = jnp.zeros_like(dwg_ref)
            dg_ref[...] = jnp.zeros_like(dg_ref)

        n, r = _rms(mix_ref[...])
        dmix, dgs = _rms_bwd(n, r, g_ref[...], dx1_ref[...])
        dg_ref[...] += dgs
        dmb = dmix.astype(BF16)
        dcat = _dot_nt(dmb, wo_ref[...])
        dyr_ref[...] = dcat[:, :RET_WIDTH]
        dglu = dcat[:, RET_WIDTH:]
        ys = ys_ref[...]
        g0 = _gelu(ys).astype(BF16)
        z = _dot(g0, wg_ref[...])
        a = z[:, :SSM_WIDTH]
        sb = _sigmoid(z[:, SSM_WIDTH:])
        dwo_ref[:RET_WIDTH, :] += _dot_tn(yr_ref[...], dmb)
        dwo_ref[RET_WIDTH:, :] += _dot_tn((a * sb).astype(BF16), dmb)
        dz = jnp.concatenate([dglu * sb, dglu * a * sb * (1.0 - sb)], axis=1).astype(BF16)
        dwg_ref[...] += _dot_tn(g0, dz)
        dys_ref[...] = _dot_nt(dz, wg_ref[...]) * _gelu_grad(ys)

    half = jax.ShapeDtypeStruct((seq, RET_WIDTH), F32)
    return _call(
        body, "mix_bwd", (seq // t,),
        [_rows(t, D_MODEL), _rows(t, D_MODEL), _full((1, D_MODEL)), _rows(t, SSM_WIDTH), _rows(t, RET_WIDTH),
         _full((SSM_WIDTH, 2 * SSM_WIDTH)), _full((D_MODEL, D_MODEL))],
        [_rows(t, RET_WIDTH), _rows(t, SSM_WIDTH), _full((D_MODEL, D_MODEL)), _full((SSM_WIDTH, 2 * SSM_WIDTH)),
         _full((1, D_MODEL))],
        [half, half, jax.ShapeDtypeStruct((D_MODEL, D_MODEL), F32), jax.ShapeDtypeStruct((SSM_WIDTH, 2 * SSM_WIDTH), F32),
         jax.ShapeDtypeStruct((1, D_MODEL), F32)],
        [], (dx1, mix, gain, y_ssm, y_ret, w_glu, w_out), exchange)


def _mlp_a(x1, target, gain_pre, gain_post, w1, w2):
    seq = x1.shape[0]
    t = MLP_TILE

    def body(x1_ref, tg_ref, gp_ref, gq_ref, w1_ref, w2_ref, df_ref, dx2_ref, dw2_ref, dgq_ref, sq_ref):
        @pl.when(pl.program_id(0) == 0)
        def _():
            dw2_ref[...] = jnp.zeros_like(dw2_ref)
            dgq_ref[...] = jnp.zeros_like(dgq_ref)
            sq_ref[...] = jnp.zeros_like(sq_ref)

        x1v = x1_ref[...]
        n3, _ = _rms(x1v)
        h = (n3 * gp_ref[...]).astype(BF16)
        rl = jnp.maximum(_dot(h, w1_ref[...]), 0.0)
        act = (rl * rl).astype(BF16)
        n4, r4 = _rms(_dot(act, w2_ref[...]))
        gq = gq_ref[...]
        err = x1v + n4 * gq - tg_ref[...]
        sq_ref[...] += jnp.sum(err * err, axis=0, keepdims=True)
        dx2 = err * (1.0 / D_MODEL)
        dx2_ref[...] = dx2
        dm, dgs = _rms_bwd(n4, r4, gq, dx2)
        dgq_ref[...] += dgs
        dmb = dm.astype(BF16)
        dw2_ref[...] += _dot_tn(act, dmb)
        df_ref[...] = (_dot_nt(dmb, w2_ref[...]) * (2.0 * rl)).astype(BF16)

    return pl.pallas_call(
        body, name="mlp_a", grid=(seq // t,),
        in_specs=[_rows(t, D_MODEL), _rows(t, D_MODEL), _full((1, D_MODEL)), _full((1, D_MODEL)),
                  _full((D_MODEL, D_FF)), _full((D_FF, D_MODEL))],
        out_specs=[_rows(t, D_FF), _rows(t, D_MODEL), _full((D_FF, D_MODEL)), _full((1, D_MODEL)), _full((1, D_MODEL))],
        out_shape=[jax.ShapeDtypeStruct((seq, D_FF), BF16), jax.ShapeDtypeStruct((seq, D_MODEL), F32),
                   jax.ShapeDtypeStruct((D_FF, D_MODEL), F32), jax.ShapeDtypeStruct((1, D_MODEL), F32),
                   jax.ShapeDtypeStruct((1, D_MODEL), F32)],
        compiler_params=_params("arbitrary"),
    )(x1, target, gain_pre, gain_post, w1, w2)


def _mlp_b(df, dx2, x1, gain_pre, w1):
    seq = x1.shape[0]
    t = TOK_TILE

    def body(df_ref, dx2_ref, x1_ref, gp_ref, w1_ref, dx1_ref, dw1_ref, dgp_ref):
        @pl.when(pl.program_id(0) == 0)
        def _():
            dw1_ref[...] = jnp.zeros_like(dw1_ref)
            dgp_ref[...] = jnp.zeros_like(dgp_ref)

        n3, r3 = _rms(x1_ref[...])
        gp = gp_ref[...]
        h = (n3 * gp).astype(BF16)
        dfv = df_ref[...]
        dw1_ref[...] += _dot_tn(h, dfv)
        dx, dgs = _rms_bwd(n3, r3, gp, _dot_nt(dfv, w1_ref[...]))
        dgp_ref[...] += dgs
        dx1_ref[...] = dx2_ref[...] + dx

    return pl.pallas_call(
        body, name="mlp_b", grid=(seq // t,),
        in_specs=[_rows(t, D_FF), _rows(t, D_MODEL), _rows(t, D_MODEL), _full((1, D_MODEL)), _full((D_MODEL, D_FF))],
        out_specs=[_rows(t, D_MODEL), _full((D_MODEL, D_FF)), _full((1, D_MODEL))],
        out_shape=[jax.ShapeDtypeStruct((seq, D_MODEL), F32), jax.ShapeDtypeStruct((D_MODEL, D_FF), F32),
                   jax.ShapeDtypeStruct((1, D_MODEL), F32)],
        compiler_params=_params("arbitrary"),
    )(df, dx2, x1, gain_pre, w1)


def _local_step(x, target, small, weights, ids=None):
    cos2, sin2 = _rope_tables(x.shape[0])
    ret_consts = _ret_consts()

    s5_names = ("ssm_lambda_re", "ssm_lambda_im", "ssm_log_dt", "ssm_b_re", "ssm_b_im", "ssm_c_re", "ssm_c_im")
    (lbr, lbi, b_tiles, c_tiles), disc_vjp = jax.vjp(_s5_discretise, *[small[name] for name in s5_names])
    tables = _s5_tables(lbr, lbi)
    b_blk = b_tiles.astype(BF16)
    c_blk = c_tiles.astype(BF16)
    d_skip = small["ssm_d"].reshape(1, SSM_WIDTH)
    gn_gain = small["ret_gn_gain"].reshape(1, RET_WIDTH)
    g_mix_pre = small["norm_mix_pre"].reshape(1, D_MODEL)
    g_mix_post = small["norm_mix_post"].reshape(1, D_MODEL)
    g_mlp_pre = small["norm_mlp_pre"].reshape(1, D_MODEL)
    g_mlp_post = small["norm_mlp_post"].reshape(1, D_MODEL)

    dist = ids is not None
    w_in, w_glu, w_out, w_ff1, w_ff2 = weights
    (q, k, v, gate, u), got = _inproj_fwd(x, g_mix_pre, w_in, cos2, sin2, _x_gather_send((1, 2), [w_glu, w_out]) if dist else None)
    if dist:
        w_glu, w_out = got
    (y_ssm, starts), got = _s5_fwd(u, b_blk, c_blk, tables, d_skip, _x_gather_send(LATE, [w_ff1, w_ff2]) if dist else None)
    if dist:
        w_ff1, w_ff2 = got
    (y_ret, rprev), got = _ret_fwd(q, k, v, gate, gn_gain, ret_consts,
                                   _x_gather_pass((1, 2) + LATE, [w_glu, w_out, w_ff1, w_ff2]) if dist else None)
    if dist:
        w_glu, w_out, w_ff1, w_ff2 = got
    x1, mix = _mix_fwd(y_ssm, y_ret, x, g_mix_post, w_glu, w_out)
    df, dx2, d_ff2, dg_mlp_post, sq = _mlp_a(x1, target, g_mlp_pre, g_mlp_post, w_ff1, w_ff2)
    dx1, d_ff1, dg_mlp_pre = _mlp_b(df, dx2, x1, g_mlp_pre, w_ff1)
    (dy_ret, dy_ssm, d_out, d_glu, dg_mix_post), got = _mix_bwd(dx1, mix, g_mix_post, y_ssm, y_ret, w_glu, w_out,
                                                               _x_pair(LATE, [d_ff1, d_ff2]) if dist else None)
    if dist:
        sums, sums_bf16 = _pair_sum(ids, LATE, [d_ff1, d_ff2], got)
    (du, dd, dlam8, db_tiles, dc_tiles), got = _s5_bwd(
        u, dy_ssm, b_blk, c_blk, tables, d_skip, starts,
        _x_join(_x_chip(LATE, sums_bf16), _x_pair(MID, [d_glu, d_out])) if dist else None)
    if dist:
        parts = _chip_sum(ids, LATE, sums, got[:len(LATE)])
        sums, sums_bf16 = _pair_sum(ids, MID, [d_glu, d_out], got[len(LATE):])
    (dq, dk, dv, dgate, dgn), got = _ret_bwd(q, k, v, gate, gn_gain, ret_consts, rprev, dy_ret,
                                             _x_join(_x_share(LATE, parts), _x_chip(MID, sums_bf16)) if dist else None)
    if dist:
        late_grads = got[:len(LATE)]
        parts = _chip_sum(ids, MID, sums, got[len(LATE):])
    (dx, d_in, dg_mix_pre), mid_grads = _inproj_bwd(x, g_mix_pre, w_in, cos2, sin2, dq, dk, dv, dgate, du, dx1,
                                                    _x_share(MID, parts) if dist else None)

    dlam = jnp.sum(dlam8, axis=1)
    s5_grads = disc_vjp((dlam[0].reshape(SSM_GROUPS, SSM_STATE), dlam[1].reshape(SSM_GROUPS, SSM_STATE), db_tiles, dc_tiles))
    small_grads = {
        "norm_mix_pre": dg_mix_pre, "norm_mix_post": dg_mix_post, "ret_gn_gain": dgn, "ssm_d": dd,
        "norm_mlp_pre": dg_mlp_pre, "norm_mlp_post": dg_mlp_post, "loss_sum": 0.5 * jnp.sum(sq) / D_MODEL,
    }
    small_grads.update(dict(zip(s5_names, s5_grads)))
    if dist:
        return dx, d_in, mid_grads, late_grads, small_grads
    return dx, (d_in, d_glu, d_out, d_ff1, d_ff2), small_grads


BIG = (
    ("w_in", D_MODEL, IN_COLS, 1),
    ("w_glu", SSM_WIDTH, 2 * SSM_WIDTH, 1),
    ("w_out", D_MODEL, D_MODEL, 0),
    ("w_ff1", D_MODEL, D_FF, 1),
    ("w_ff2", D_FF, D_MODEL, 0),
)
SMALL = (
    ("norm_mix_pre", (D_MODEL,)), ("norm_mix_post", (D_MODEL,)), ("ret_gn_gain", (RET_WIDTH,)),
    ("ssm_lambda_re", (SSM_GROUPS, SSM_STATE)), ("ssm_lambda_im", (SSM_GROUPS, SSM_STATE)), ("ssm_log_dt", (SSM_GROUPS,)),
    ("ssm_b_re", (SSM_GROUPS, SSM_STATE, SSM_GROUP)), ("ssm_b_im", (SSM_GROUPS, SSM_STATE, SSM_GROUP)),
    ("ssm_c_re", (SSM_GROUPS, SSM_GROUP, SSM_STATE)), ("ssm_c_im", (SSM_GROUPS, SSM_GROUP, SSM_STATE)),
    ("ssm_d", (SSM_WIDTH,)), ("norm_mlp_pre", (D_MODEL,)), ("norm_mlp_post", (D_MODEL,)),
)
SMALL_ROWS = 1152
EXCH_TILES = 8


PACKED = SMALL + (("loss_sum", (1,)),)


def _packed_rows(shape):
    return -(-math.prod(shape) // (SUBLANES * LANES)) * SUBLANES


def _pack_small(tree):
    rows = []
    for name, shape in PACKED:
        size = math.prod(shape)
        flat = tree[name].reshape(-1).astype(F32) if name in tree else jnp.zeros((size,), F32)
        rows.append(jnp.pad(flat, (0, _packed_rows(shape) * LANES - size)).reshape(-1, LANES))
    used = sum(r.shape[0] for r in rows)
    rows.append(jnp.zeros((SMALL_ROWS - used, LANES), F32))
    return jnp.concatenate(rows, axis=0)


def _unpack_small(packed):
    out, row = {}, 0
    for name, shape in PACKED:
        size = math.prod(shape)
        out[name] = packed[row:row + _packed_rows(shape)].reshape(-1)[:size].reshape((1,) + shape)
        row += _packed_rows(shape)
    return out


def _half_shape(r, c, axis):
    return (r // 2, c) if axis == 1 else (r, c // 2)


def _region_shape(r, c, axis):
    return (r // 2, c // N_CHIPS) if axis == 1 else (r // N_CHIPS, c // 2)


def _shard_shape(r, c, axis):
    return (r, c // N_CHIPS) if axis == 1 else (r // N_CHIPS, c)


def _ds(start, size):
    return pl.ds(pl.multiple_of(start * size, size), size)


def _region_of_full(ref, r, c, axis, shard, half):
    if axis == 1:
        return ref.at[_ds(half, r // 2), _ds(shard, c // N_CHIPS)]
    return ref.at[_ds(shard, r // N_CHIPS), _ds(half, c // 2)]


def _half_of_full(ref, r, c, axis, half):
    if axis == 1:
        return ref.at[_ds(half, r // 2), :]
    return ref.at[:, _ds(half, c // 2)]


def _half_of_shard(ref, r, c, axis, half):
    if axis == 1:
        return ref.at[_ds(half, r // 2), :]
    return ref.at[:, _ds(half, c // 2)]


def _region_of_half(ref, r, c, axis, shard):
    if axis == 1:
        return ref.at[:, _ds(shard, c // N_CHIPS)]
    return ref.at[_ds(shard, r // N_CHIPS), :]


def _place():
    x, y, c = lax.axis_index("x"), lax.axis_index("y"), lax.axis_index("c")
    chips = [(1 - x, y), (x, 1 - y), (1 - x, 1 - y)]
    return x, y, c, chips


LATE = (3, 4)
MID = (1, 2)
FIRST = (0,)
SMALL_HALF = (SMALL_ROWS // 2, LANES)


def _remote(src, dst, send_sem, recv_sem, to):
    return pltpu.make_async_remote_copy(src_ref=src, dst_ref=dst, send_sem=send_sem, recv_sem=recv_sem,
                                        device_id=to, device_id_type=MESH)


def _same(arrays):
    return [jax.ShapeDtypeStruct(a.shape, a.dtype) for a in arrays]


def _x_gather_send(ws, fulls):
    def copies(ins, outs, send_sems, recv_sems):
        x, y, c, chips = _place()
        out = []
        for j, w in enumerate(ws):
            _, r, cc, axis = BIG[w]
            mine = _region_of_full(outs[j], r, cc, axis, 2 * x + y, c)
            out += [_remote(mine, mine, send_sems.at[3 * j + k], recv_sems.at[3 * j + k], (cx, cy, c))
                    for k, (cx, cy) in enumerate(chips)]
        return out

    return _Exchange(fulls, _same(fulls), {j: j for j in range(len(ws))}, 3 * len(ws), copies)


def _x_gather_pass(ws, fulls):
    def copies(ins, outs, send_sems, recv_sems):
        x, y, c, chips = _place()
        out = []
        for j, w in enumerate(ws):
            _, r, cc, axis = BIG[w]
            for k, (cx, cy) in enumerate(chips):
                landed = _region_of_full(outs[j], r, cc, axis, 2 * cx + cy, c)
                out.append(_remote(landed, landed, send_sems.at[3 * j + k], recv_sems.at[3 * j + k], (x, y, 1 - c)))
        return out

    return _Exchange(fulls, _same(fulls), {j: j for j in range(len(ws))}, 3 * len(ws), copies)


def _x_pair(ws, grads, small=None):
    def copies(ins, outs, send_sems, recv_sems):
        x, y, c, _ = _place()
        out = []
        for j, w in enumerate(ws):
            _, r, cc, axis = BIG[w]
            out.append(_remote(_half_of_full(ins[j], r, cc, axis, 1 - c), outs[j], send_sems.at[j], recv_sems.at[j], (x, y, 1 - c)))
        if small is not None:
            j = len(ws)
            out.append(_remote(ins[j].at[_ds(1 - c, SMALL_ROWS // 2), :], outs[j], send_sems.at[j], recv_sems.at[j], (x, y, 1 - c)))
        return out

    shapes = [jax.ShapeDtypeStruct(_half_shape(*BIG[w][1:]), F32) for w in ws]
    extra = [] if small is None else [small]
    return _Exchange(list(grads) + extra, shapes + [jax.ShapeDtypeStruct(SMALL_HALF, F32)] * len(extra), {},
                     len(ws) + len(extra), copies)


def _x_chip(ws, sums_bf16, small_sum=None):
    def copies(ins, outs, send_sems, recv_sems):
        x, y, c, chips = _place()
        out = []
        for j, w in enumerate(ws):
            _, r, cc, axis = BIG[w]
            out += [_remote(_region_of_half(ins[j], r, cc, axis, 2 * cx + cy), outs[j].at[k],
                            send_sems.at[3 * j + k], recv_sems.at[3 * j + k], (cx, cy, c)) for k, (cx, cy) in enumerate(chips)]
        if small_sum is not None:
            j = len(ws)
            out += [_remote(ins[j], outs[j].at[k], send_sems.at[3 * j + k], recv_sems.at[3 * j + k], (cx, cy, c))
                    for k, (cx, cy) in enumerate(chips)]
        return out

    shapes = [jax.ShapeDtypeStruct((3,) + _region_shape(*BIG[w][1:]), BF16) for w in ws]
    extra = [] if small_sum is None else [small_sum]
    return _Exchange(list(sums_bf16) + extra, shapes + [jax.ShapeDtypeStruct((3,) + SMALL_HALF, F32)] * len(extra), {},
                     3 * (len(ws) + len(extra)), copies)


def _x_share(ws, shards, small=None):
    def copies(ins, outs, send_sems, recv_sems):
        x, y, c, _ = _place()
        out = []
        for j, w in enumerate(ws):
            _, r, cc, axis = BIG[w]
            mine = _half_of_shard(outs[j], r, cc, axis, c)
            out.append(_remote(mine, mine, send_sems.at[j], recv_sems.at[j], (x, y, 1 - c)))
        if small is not None:
            j = len(ws)
            mine = outs[j].at[_ds(c, SMALL_ROWS // 2), :]
            out.append(_remote(mine, mine, send_sems.at[j], recv_sems.at[j], (x, y, 1 - c)))
        return out

    arrays = list(shards) + ([] if small is None else [small])
    return _Exchange(arrays, _same(arrays), {j: j for j in range(len(arrays))}, len(arrays), copies)


class _Offset:
    def __init__(self, sems, base):
        self._sems, self._base = sems, base

    @property
    def at(self):
        return self

    def __getitem__(self, i):
        return self._sems.at[self._base + i]


def _x_join(a, b):
    ka_in, ka_out = len(a.operands), len(a.out_shapes)

    def copies(ins, outs, send_sems, recv_sems):
        return (a.copies(ins[:ka_in], outs[:ka_out], send_sems, recv_sems)
                + b.copies(ins[ka_in:], outs[ka_out:], _Offset(send_sems, a.n), _Offset(recv_sems, a.n)))

    aliases = dict(a.aliases)
    aliases.update({ka_in + i: ka_out + o for i, o in b.aliases.items()})
    return _Exchange(a.operands + b.operands, a.out_shapes + b.out_shapes, aliases, a.n + b.n, copies)


def _run_exchange(name, exchange):
    k_in, k_out = len(exchange.operands), len(exchange.out_shapes)

    def body(*refs):
        copies = exchange.copies(refs[:k_in], refs[k_in:k_in + k_out], refs[-2], refs[-1])
        for cp in copies:
            cp.start()
        for cp in copies:
            cp.wait()

    return list(pl.pallas_call(
        body, name=name, in_specs=[ANY] * k_in, out_specs=[ANY] * k_out, out_shape=exchange.out_shapes,
        scratch_shapes=[pltpu.SemaphoreType.DMA((exchange.n,)), pltpu.SemaphoreType.DMA((exchange.n,))],
        input_output_aliases=dict(exchange.aliases),
    )(*exchange.operands))


def _tile_rows(shape):
    return (shape[0] // EXCH_TILES, shape[1])


def _cast_place(ids, shards):
    def body(ids_ref, *refs):
        for j in range(len(BIG)):
            refs[len(BIG) + j][...] = refs[j][...].astype(BF16)

    def out_spec(w):
        _, r, cc, axis = BIG[w]
        tile = _tile_rows(_shard_shape(r, cc, axis))
        if axis == 1:
            return pl.BlockSpec(tile, lambda i, ids: (i, ids[1]))
        return pl.BlockSpec(tile, lambda i, ids: (ids[1] * EXCH_TILES + i, 0))

    return list(pl.pallas_call(
        body, name="cast_place",
        grid_spec=pltpu.PrefetchScalarGridSpec(
            num_scalar_prefetch=1, grid=(EXCH_TILES,),
            in_specs=[pl.BlockSpec(_tile_rows(_shard_shape(r, cc, axis)), lambda i, ids: (i, 0)) for _, r, cc, axis in BIG],
            out_specs=[out_spec(w) for w in range(len(BIG))]),
        out_shape=[jax.ShapeDtypeStruct((r, cc), BF16) for _, r, cc, _ in BIG],
        compiler_params=pltpu.CompilerParams(dimension_semantics=("parallel",)),
    )(ids, *shards))


def _pair_sum(ids, ws, grads, got, small=None):
    nb = len(ws)
    n = nb + (small is not None)
    halves = [_half_shape(*BIG[w][1:]) for w in ws] + [SMALL_HALF] * (n - nb)

    def body(ids_ref, *refs):
        for j in range(n):
            total = refs[j][...] + refs[n + j][...]
            refs[2 * n + j][...] = total
            if j < nb:
                refs[3 * n + j][...] = total.astype(BF16)

    def mine_spec(j):
        tile = _tile_rows(halves[j])
        if j < nb and BIG[ws[j]][3] == 0:
            return pl.BlockSpec(tile, lambda i, ids: (i, ids[0]))
        return pl.BlockSpec(tile, lambda i, ids: (ids[0] * EXCH_TILES + i, 0))

    plain = lambda j: pl.BlockSpec(_tile_rows(halves[j]), lambda i, ids: (i, 0))
    outs = pl.pallas_call(
        body, name="pair_sum",
        grid_spec=pltpu.PrefetchScalarGridSpec(
            num_scalar_prefetch=1, grid=(EXCH_TILES,),
            in_specs=[mine_spec(j) for j in range(n)] + [plain(j) for j in range(n)],
            out_specs=[plain(j) for j in range(n)] + [plain(j) for j in range(nb)]),
        out_shape=[jax.ShapeDtypeStruct(h, F32) for h in halves] + [jax.ShapeDtypeStruct(h, BF16) for h in halves[:nb]],
        compiler_params=pltpu.CompilerParams(dimension_semantics=("parallel",)),
    )(ids, *grads, *([] if small is None else [small]), *got)
    return list(outs[:n]), list(outs[n:])


def _chip_sum(ids, ws, sums, arrived):
    nb, n = len(ws), len(sums)
    regions = [_region_shape(*BIG[w][1:]) for w in ws] + [SMALL_HALF] * (n - nb)

    def body(ids_ref, *refs):
        for j in range(n):
            own, got, out = refs[j], refs[n + j], refs[2 * n + j]
            if j < nb:
                out[...] = ((own[...] + got[0].astype(F32)) + got[1].astype(F32)) + got[2].astype(F32)
            else:
                out[...] = (own[...] + got[1]) + (got[0] + got[2])

    def own_spec(j):
        tile = _tile_rows(regions[j])
        if j >= nb:
            return pl.BlockSpec(tile, lambda i, ids: (i, 0))
        if BIG[ws[j]][3] == 1:
            return pl.BlockSpec(tile, lambda i, ids: (i, ids[1]))
        return pl.BlockSpec(tile, lambda i, ids: (ids[1] * EXCH_TILES + i, 0))

    def out_spec(j):
        tile = _tile_rows(regions[j])
        if j < nb and BIG[ws[j]][3] == 0:
            return pl.BlockSpec(tile, lambda i, ids: (i, ids[0]))
        return pl.BlockSpec(tile, lambda i, ids: (ids[0] * EXCH_TILES + i, 0))

    got_spec = lambda j: pl.BlockSpec((3,) + _tile_rows(regions[j]), lambda i, ids: (0, i, 0))
    return list(pl.pallas_call(
        body, name="chip_sum",
        grid_spec=pltpu.PrefetchScalarGridSpec(
            num_scalar_prefetch=1, grid=(EXCH_TILES,),
            in_specs=[own_spec(j) for j in range(n)] + [got_spec(j) for j in range(n)],
            out_specs=[out_spec(j) for j in range(n)]),
        out_shape=[jax.ShapeDtypeStruct(_shard_shape(*BIG[w][1:]), F32) for w in ws]
        + [jax.ShapeDtypeStruct((SMALL_ROWS, LANES), F32)] * (n - nb),
        compiler_params=pltpu.CompilerParams(dimension_semantics=("parallel",)),
    )(ids, *sums, *arrived))


def _adamw(name, ws, gs, ms, vs, exchange=None):
    n = len(ws)
    c1 = 1.0 - ADAM_B1 ** ADAM_STEP
    c2 = 1.0 - ADAM_B2 ** ADAM_STEP

    def body(*refs):
        for j in range(n):
            w, g, m, v = (refs[k * n + j][...] for k in range(4))
            m = ADAM_B1 * m + (1.0 - ADAM_B1) * g
            v = ADAM_B2 * v + (1.0 - ADAM_B2) * (g * g)
            refs[4 * n + j][...] = -ADAM_LR * ((m / c1) / (jnp.sqrt(v / c2) + ADAM_EPS) + ADAM_WD * w)
            refs[5 * n + j][...] = m
            refs[6 * n + j][...] = v

    specs = [pl.BlockSpec(_tile_rows(w.shape), lambda i: (i, 0)) for w in ws]
    shapes = [jax.ShapeDtypeStruct(w.shape, F32) for w in ws]
    out, got = _call(body, name, (EXCH_TILES,), specs * 4, specs * 3, shapes * 3, [], (*ws, *gs, *ms, *vs), exchange)
    return (out[:n], out[n:2 * n], out[2 * n:]), got


def kernel(x, norm_mix_pre, norm_mix_post, w_in, ret_gn_gain, ssm_lambda_re, ssm_lambda_im, ssm_log_dt, ssm_b_re, ssm_b_im, ssm_c_re, ssm_c_im, ssm_d, w_glu, w_out, norm_mlp_pre, norm_mlp_post, w_ff1, w_ff2, loss_target, m_norm_mix_pre, m_norm_mix_post, m_w_in, m_ret_gn_gain, m_ssm_lambda_re, m_ssm_lambda_im, m_ssm_log_dt, m_ssm_b_re, m_ssm_b_im, m_ssm_c_re, m_ssm_c_im, m_ssm_d, m_w_glu, m_w_out, m_norm_mlp_pre, m_norm_mlp_post, m_w_ff1, m_w_ff2, v_norm_mix_pre, v_norm_mix_post, v_w_in, v_ret_gn_gain, v_ssm_lambda_re, v_ssm_lambda_im, v_ssm_log_dt, v_ssm_b_re, v_ssm_b_im, v_ssm_c_re, v_ssm_c_im, v_ssm_d, v_w_glu, v_w_out, v_norm_mlp_pre, v_norm_mlp_post, v_w_ff1, v_w_ff2):
    given = dict(locals())
    order = ["norm_mix_pre", "norm_mix_post", "w_in", "ret_gn_gain", "ssm_lambda_re", "ssm_lambda_im", "ssm_log_dt",
             "ssm_b_re", "ssm_b_im", "ssm_c_re", "ssm_c_im", "ssm_d", "w_glu", "w_out", "norm_mlp_pre", "norm_mlp_post",
             "w_ff1", "w_ff2"]
    big_names = [name for name, _, _, _ in BIG]
    ids = jnp.stack([lax.axis_index("c"), 2 * lax.axis_index("x") + lax.axis_index("y")]).astype(jnp.int32)

    weights = _cast_place(ids, [given[name][0] for name in big_names])
    w_in = _run_exchange("gather_w_in_send", _x_gather_send((0,), weights[:1]))
    w_in = _run_exchange("gather_w_in_pass", _x_gather_pass((0,), w_in))
    small_w = {name: given[name][0] for name, _ in SMALL}
    dx, d_in, mid_grads, late_grads, small_grads = _local_step(x[0], loss_target[0], small_w, w_in + weights[1:], ids)

    small_packed = _pack_small(small_grads)
    got = _run_exchange("pair_exchange", _x_pair(FIRST, [d_in], small_packed))
    sums, sums_bf16 = _pair_sum(ids, FIRST, [d_in], got, small_packed)
    done = list(MID + LATE)
    state = lambda prefix: [given[prefix + big_names[w]][0] for w in done]
    stepped, arrived = _adamw("adamw_done", state(""), mid_grads + late_grads, state("m_"), state("v_"), _x_chip(FIRST, sums_bf16, sums[-1]))
    parts = _chip_sum(ids, FIRST, sums, arrived)
    g_in, g_small = _run_exchange("pair_share", _x_share(FIRST, parts[:-1], parts[-1]))
    pack = lambda prefix: _pack_small({name: given[prefix + name][0] for name, _ in SMALL})
    last, _ = _adamw("adamw_last", [given["w_in"][0], pack("")], [g_in, g_small], [given["m_w_in"][0], pack("m_")],
                     [given["v_w_in"][0], pack("v_")])
    nw = len(BIG)
    reduced = [g_in] + mid_grads + late_grads + [g_small]
    deltas, new_m, new_v = ([last[k][0]] + list(stepped[k]) + [last[k][1]] for k in range(3))

    def unpacked(arrays):
        tree = _unpack_small(arrays[nw])
        tree.update({name: arrays[j][None] for j, name in enumerate(big_names)})
        return [tree[name] for name in order]

    loss = _unpack_small(reduced[nw])["loss_sum"].reshape(())
    return (loss, dx[None], *unpacked(list(reduced)), *unpacked(deltas), *unpacked(new_m), *unpacked(new_v))
```

```python
import functools
import math

import jax
import jax.numpy as jnp
import numpy as np
from jax import lax
from jax.experimental import pallas as pl
from jax.experimental.pallas import tpu as pltpu

F32 = jnp.float32
BF16 = jnp.bfloat16

D_MODEL = 1024
RET_WIDTH = 512
RET_HEADS = 4
HEAD_DIM = 128
RET_CHUNK = 128
ROPE_BASE = 10000.0
SSM_WIDTH = 512
SSM_GROUP = 16
SSM_GROUPS = 32
SSM_STATE = 64
N_STATES = SSM_GROUPS * SSM_STATE
D_FF = 4096
IN_COLS = 4 * RET_WIDTH + SSM_WIDTH
NORM_EPS = 1e-6
K_SCALE = HEAD_DIM ** -0.5

ADAM_LR = 0.001
ADAM_B1 = 0.9
ADAM_B2 = 0.999
ADAM_EPS = 1e-08
ADAM_WD = 0.01
ADAM_STEP = 10

LANES = 128
SUBLANES = 8
VMEM_LIMIT = 56 * 2 ** 20

TOK_TILE = 512
MLP_TILE = 256
RET_CHUNKS_PER_STEP = 4
S5_TILE = 256
S5_STEPS = S5_TILE // SUBLANES
S5_COLBLK = N_STATES // LANES
S5_GROUP = 8

N_CHIPS = 4
MESH = pl.DeviceIdType.MESH


def _dot(a, b):
    return jnp.dot(a, b, preferred_element_type=F32)


def _dot_nt(a, b):
    return lax.dot_general(a, b, (((1,), (1,)), ((), ())), preferred_element_type=F32)


def _dot_tn(a, b):
    return lax.dot_general(a, b, (((0,), (0,)), ((), ())), preferred_element_type=F32)


def _sigmoid(x):
    return 1.0 / (1.0 + jnp.exp(-x))


_GELU_C = math.sqrt(2.0 / math.pi)


def _gelu(x):
    return 0.5 * x * (1.0 + jnp.tanh(_GELU_C * (x + 0.044715 * (x * x * x))))


def _gelu_grad(x):
    t = jnp.tanh(_GELU_C * (x + 0.044715 * (x * x * x)))
    return 0.5 * (1.0 + t) + 0.5 * x * (1.0 - t * t) * (_GELU_C * (1.0 + 3.0 * 0.044715 * (x * x)))


def _rms(x):
    r = lax.rsqrt(jnp.mean(x * x, axis=-1, keepdims=True) + NORM_EPS)
    return x * r, r


def _rms_bwd(n, r, gain, dy):
    dn = dy * gain
    dx = r * (dn - n * jnp.mean(dn * n, axis=-1, keepdims=True))
    return dx, jnp.sum(dy * n, axis=0, keepdims=True)


def _full(shape):
    nd = len(shape)
    return pl.BlockSpec(shape, lambda i, _n=nd: (0,) * _n, pipeline_mode=pl.Buffered(1))


def _rows(tile, width):
    return pl.BlockSpec((tile, width), lambda i: (i, 0))


def _rows_rev(tile, width, n):
    return pl.BlockSpec((tile, width), lambda i, _n=n: (_n - 1 - i, 0))


def _params(sem):
    return pltpu.CompilerParams(dimension_semantics=(sem,), vmem_limit_bytes=VMEM_LIMIT)


ANY = pl.BlockSpec(memory_space=pl.ANY)


class _Exchange:
    def __init__(self, operands, out_shapes, aliases, n, copies):
        self.operands, self.out_shapes, self.aliases, self.n, self.copies = list(operands), list(out_shapes), aliases, n, copies


def _call(body, name, grid, in_specs, out_specs, out_shape, scratch_shapes, args, exchange=None):
    if exchange is None:
        outs = pl.pallas_call(body, name=name, grid=grid, in_specs=in_specs, out_specs=out_specs, out_shape=out_shape,
                              scratch_shapes=scratch_shapes, compiler_params=_params("arbitrary"))(*args)
        return list(outs), []
    n_in, n_out, n_scr = len(in_specs), len(out_specs), len(scratch_shapes)
    k_in, k_out = len(exchange.operands), len(exchange.out_shapes)
    last = grid[0] - 1

    def hosted(*refs):
        own_in, rest = refs[:n_in], refs[n_in:]
        ex_in, rest = rest[:k_in], rest[k_in:]
        own_out, rest = rest[:n_out], rest[n_out:]
        ex_out, rest = rest[:k_out], rest[k_out:]
        own_scr, (send_sems, recv_sems) = rest[:n_scr], rest[n_scr:]

        @pl.when(pl.program_id(0) == 0)
        def _():
            for cp in exchange.copies(ex_in, ex_out, send_sems, recv_sems):
                cp.start()

        body(*own_in, *own_out, *own_scr)

        @pl.when(pl.program_id(0) == last)
        def _():
            for cp in exchange.copies(ex_in, ex_out, send_sems, recv_sems):
                cp.wait()

    outs = pl.pallas_call(
        hosted, name=name, grid=grid, in_specs=list(in_specs) + [ANY] * k_in, out_specs=list(out_specs) + [ANY] * k_out,
        out_shape=list(out_shape) + exchange.out_shapes,
        scratch_shapes=list(scratch_shapes) + [pltpu.SemaphoreType.DMA((exchange.n,)), pltpu.SemaphoreType.DMA((exchange.n,))],
        input_output_aliases={n_in + i: n_out + o for i, o in exchange.aliases.items()},
        compiler_params=_params("arbitrary"),
    )(*args, *exchange.operands)
    return list(outs[:n_out]), list(outs[n_out:])


def _rope(t, cos2, sin2):
    return t * cos2 + pltpu.roll(t, HEAD_DIM // 2, 1) * sin2


def _rope_bwd(d, cos2, sin2):
    return d * cos2 + pltpu.roll(d * sin2, HEAD_DIM // 2, d.ndim - 1)


def _inproj_fwd(x, gain, w_in, cos2, sin2, exchange=None):
    seq = x.shape[0]

    def body(x_ref, g_ref, w_ref, c_ref, s_ref, q_ref, k_ref, v_ref, gate_ref, u_ref):
        n, _ = _rms(x_ref[...])
        h = (n * g_ref[...]).astype(BF16)
        proj = _dot(h, w_ref[...])
        c = c_ref[...]
        s = s_ref[...]
        for hd in range(RET_HEADS):
            lo = hd * HEAD_DIM
            q_ref[:, lo:lo + HEAD_DIM] = _rope(proj[:, lo:lo + HEAD_DIM], c, s).astype(BF16)
            kh = proj[:, RET_WIDTH + lo:RET_WIDTH + lo + HEAD_DIM]
            k_ref[:, lo:lo + HEAD_DIM] = (_rope(kh, c, s) * K_SCALE).astype(BF16)
        v_ref[...] = proj[:, 2 * RET_WIDTH:3 * RET_WIDTH].astype(BF16)
        gate_ref[...] = proj[:, 3 * RET_WIDTH:4 * RET_WIDTH]
        u_ref[...] = proj[:, 4 * RET_WIDTH:]

    t = TOK_TILE
    half = lambda dt: jax.ShapeDtypeStruct((seq, RET_WIDTH), dt)
    return _call(
        body, "inproj_fwd", (seq // t,),
        [_rows(t, D_MODEL), _full((1, D_MODEL)), _full((D_MODEL, IN_COLS)), _rows(t, HEAD_DIM), _rows(t, HEAD_DIM)],
        [_rows(t, RET_WIDTH)] * 5, [half(BF16), half(BF16), half(BF16), half(F32), half(F32)], [],
        (x, gain, w_in, cos2, sin2), exchange)


def _inproj_bwd(x, gain, w_in, dpieces, dres, exchange=None):
    seq = x.shape[0]
    t = TOK_TILE
    n_pieces = len(dpieces)

    def body(x_ref, g_ref, w_ref, *refs):
        piece_refs, (dres_ref, dx_ref, dw_ref, dg_ref) = refs[:n_pieces], refs[n_pieces:]

        @pl.when(pl.program_id(0) == 0)
        def _():
            dw_ref[...] = jnp.zeros_like(dw_ref)
            dg_ref[...] = jnp.zeros_like(dg_ref)

        n, r = _rms(x_ref[...])
        gain_v = g_ref[...]
        h = (n * gain_v).astype(BF16)
        dh = None
        for j, piece in enumerate(piece_refs):
            cols = slice(j * RET_WIDTH, (j + 1) * RET_WIDTH)
            dp = piece[...]
            dw_ref[:, cols] += _dot_tn(h, dp)
            term = _dot_nt(dp, w_ref[:, cols])
            dh = term if dh is None else dh + term
        dx, dgs = _rms_bwd(n, r, gain_v, dh)
        dg_ref[...] += dgs
        dx_ref[...] = dres_ref[...] + dx

    return _call(
        body, "inproj_bwd", (seq // t,),
        [_rows(t, D_MODEL), _full((1, D_MODEL)), _full((D_MODEL, IN_COLS))] + [_rows(t, RET_WIDTH)] * n_pieces + [_rows(t, D_MODEL)],
        [_rows(t, D_MODEL), _full((D_MODEL, IN_COLS)), _full((1, D_MODEL))],
        [jax.ShapeDtypeStruct((seq, D_MODEL), F32), jax.ShapeDtypeStruct((D_MODEL, IN_COLS), F32),
         jax.ShapeDtypeStruct((1, D_MODEL), F32)],
        [], (x, gain, w_in, *dpieces, dres), exchange)


def _ret_consts():
    c = RET_CHUNK
    f32 = np.float32
    log_gamma = np.log(f32(1.0) - np.exp(np.linspace(math.log(1.0 / 32), math.log(1.0 / 512), RET_HEADS, dtype=f32))).astype(f32)
    idx = np.arange(c, dtype=f32)
    diff = idx[:, None] - idx[None, :]
    decay = np.where(diff[None] >= 0, np.exp(np.maximum(diff, f32(0.0))[None] * log_gamma[:, None, None]), f32(0.0))
    zeta = np.exp((c - 1 - idx)[None, :] * log_gamma[:, None])
    xi = np.exp((idx + f32(1.0))[None, :] * log_gamma[:, None])
    g_chunk = np.exp(f32(c) * log_gamma)
    wide = lambda rowvals: jnp.asarray(np.broadcast_to(rowvals[:, :, None], (RET_HEADS, c, c)).astype(f32))
    return (jnp.asarray(decay.astype(f32)), wide(xi), wide(zeta),
            jnp.asarray(np.broadcast_to(g_chunk[:, None, None], (RET_HEADS, c, c)).astype(f32)))


def _rope_tables(seq):
    f32 = np.float32
    half = HEAD_DIM // 2
    inv_freq = np.power(f32(ROPE_BASE), -np.arange(half, dtype=f32) / f32(half)).astype(f32)
    ang = (np.arange(seq, dtype=f32)[:, None] * inv_freq[None, :]).astype(f32)
    cos, sin = np.cos(ang).astype(f32), np.sin(ang).astype(f32)
    return jnp.asarray(np.concatenate([cos, cos], axis=1)), jnp.asarray(np.concatenate([-sin, sin], axis=1))


def _bmm(a, b):
    return lax.dot_general(a, b, (((2,), (1,)), ((0,), (0,))), preferred_element_type=F32)


def _bmm_nt(a, b):
    return lax.dot_general(a, b, (((2,), (2,)), ((0,), (0,))), preferred_element_type=F32)


def _bmm_tn(a, b):
    return lax.dot_general(a, b, (((1,), (1,)), ((0,), (0,))), preferred_element_type=F32)


def _ret_blocks(ref):
    return jnp.stack([ref[cc * RET_CHUNK:(cc + 1) * RET_CHUNK, hd * HEAD_DIM:(hd + 1) * HEAD_DIM]
                      for cc in range(RET_CHUNKS_PER_STEP) for hd in range(RET_HEADS)])


def _ret_unblock(ref, blocks):
    for cc in range(RET_CHUNKS_PER_STEP):
        for hd in range(RET_HEADS):
            ref[cc * RET_CHUNK:(cc + 1) * RET_CHUNK, hd * HEAD_DIM:(hd + 1) * HEAD_DIM] = blocks[cc * RET_HEADS + hd].astype(ref.dtype)


def _rope_blocks(ref):
    return jnp.stack([ref[cc * RET_CHUNK:(cc + 1) * RET_CHUNK, :] for cc in range(RET_CHUNKS_PER_STEP) for _ in range(RET_HEADS)])


def _per_chunk(ref):
    return jnp.concatenate([ref[...]] * RET_CHUNKS_PER_STEP, axis=0)


def _ret_gain(gain_ref):
    return jnp.stack([gain_ref[:, hd * HEAD_DIM:(hd + 1) * HEAD_DIM]
                      for _ in range(RET_CHUNKS_PER_STEP) for hd in range(RET_HEADS)])


def _ret_fwd(q, k, v, gate, gn_gain, consts, exchange=None):
    seq = q.shape[0]
    cps = RET_CHUNKS_PER_STEP
    t = cps * RET_CHUNK
    n_chunks = seq // RET_CHUNK
    dec, xi, zeta, gch = consts

    def body(q_ref, k_ref, v_ref, gate_ref, gain_ref, dec_ref, xi_ref, zeta_ref, gch_ref, y_ref, rprev_ref, state):
        @pl.when(pl.program_id(0) == 0)
        def _():
            state[...] = jnp.zeros_like(state)

        qb, kb, vb = _ret_blocks(q_ref), _ret_blocks(k_ref), _ret_blocks(v_ref)
        s = _bmm_nt(qb, kb) * _per_chunk(dec_ref)
        vz = (vb.astype(F32) * _per_chunk(zeta_ref)).astype(BF16)
        incr = _bmm_tn(kb, vz)
        gch_v = gch_ref[...]
        rp = state[...]
        before = []
        for cc in range(cps):
            before.append(rp.astype(BF16))
            rprev_ref[cc] = before[-1]
            rp = gch_v * rp + incr[cc * RET_HEADS:(cc + 1) * RET_HEADS]
        state[...] = rp
        qx = (qb.astype(F32) * _per_chunk(xi_ref)).astype(BF16)
        o = _bmm(s.astype(BF16), vb) + _bmm(qx, jnp.concatenate(before, axis=0))
        oc = o - jnp.mean(o, axis=-1, keepdims=True)
        on = oc * lax.rsqrt(jnp.mean(oc * oc, axis=-1, keepdims=True) + NORM_EPS)
        g = _ret_blocks(gate_ref)
        _ret_unblock(y_ref, g * _sigmoid(g) * (on * _ret_gain(gain_ref)))

    cst = _full((RET_HEADS, RET_CHUNK, RET_CHUNK))
    return _call(
        body, "ret_fwd", (seq // t,),
        [_rows(t, RET_WIDTH)] * 4 + [_full((1, RET_WIDTH)), cst, cst, cst, cst],
        [_rows(t, RET_WIDTH), pl.BlockSpec((cps, RET_HEADS, HEAD_DIM, HEAD_DIM), lambda i: (i, 0, 0, 0))],
        [jax.ShapeDtypeStruct((seq, RET_WIDTH), BF16), jax.ShapeDtypeStruct((n_chunks, RET_HEADS, HEAD_DIM, HEAD_DIM), BF16)],
        [pltpu.VMEM((RET_HEADS, HEAD_DIM, HEAD_DIM), F32)],
        (q, k, v, gate, gn_gain, dec, xi, zeta, gch), exchange)


def _ret_bwd(q, k, v, gate, gn_gain, consts, rprev, dy_ret, cos2, sin2, exchange=None):
    seq = q.shape[0]
    cps = RET_CHUNKS_PER_STEP
    t = cps * RET_CHUNK
    nt = seq // t
    dec, xi, zeta, gch = consts

    def body(q_ref, k_ref, v_ref, gate_ref, gain_ref, dec_ref, xi_ref, zeta_ref, gch_ref, rprev_ref, dyr_ref, cos_ref, sin_ref,
             dq_ref, dk_ref, dv_ref, dgate_ref, dgain_ref, dstate):
        @pl.when(pl.program_id(0) == 0)
        def _():
            dstate[...] = jnp.zeros_like(dstate)
            dgain_ref[...] = jnp.zeros_like(dgain_ref)

        nb = cps * RET_HEADS
        qb, kb, vb = _ret_blocks(q_ref), _ret_blocks(k_ref), _ret_blocks(v_ref)
        dec_b, xi_b, zeta_b = _per_chunk(dec_ref), _per_chunk(xi_ref), _per_chunk(zeta_ref)
        rpb = rprev_ref[...].reshape(nb, HEAD_DIM, HEAD_DIM)
        sdb = (_bmm_nt(qb, kb) * dec_b).astype(BF16)
        qx = (qb.astype(F32) * xi_b).astype(BF16)
        o = _bmm(sdb, vb) + _bmm(qx, rpb)
        oc = o - jnp.mean(o, axis=-1, keepdims=True)
        rstd = lax.rsqrt(jnp.mean(oc * oc, axis=-1, keepdims=True) + NORM_EPS)
        on = oc * rstd
        g = _ret_blocks(gate_ref)
        sg = _sigmoid(g)
        gain_b = _ret_gain(gain_ref)
        dyr = _ret_blocks(dyr_ref)
        _ret_unblock(dgate_ref, dyr * (on * gain_b) * (sg * (1.0 + g * (1.0 - sg))))
        dy = dyr * (g * sg)
        dgain = jnp.sum(dy * on, axis=1, keepdims=True)
        for hd in range(RET_HEADS):
            part = dgain[hd]
            for cc in range(1, cps):
                part = part + dgain[cc * RET_HEADS + hd]
            dgain_ref[:, hd * HEAD_DIM:(hd + 1) * HEAD_DIM] += part
        don = dy * gain_b
        do = rstd * (don - jnp.mean(don, axis=-1, keepdims=True) - on * jnp.mean(don * on, axis=-1, keepdims=True))
        dob = do.astype(BF16)
        ds = (_bmm_nt(dob, vb) * dec_b).astype(BF16)
        dq = _bmm(ds, kb) + _bmm_nt(dob, rpb) * xi_b
        dk = _bmm_tn(ds, qb)
        dv = _bmm_tn(sdb, dob)
        dstate_local = _bmm_tn(qx, dob)
        vz = (vb.astype(F32) * zeta_b).astype(BF16)
        gch_v = gch_ref[...]
        zeta_v = zeta_ref[...]
        gh = dstate[...]
        dk_extra, dv_extra = [None] * cps, [None] * cps
        for cc in reversed(range(cps)):
            sl = slice(cc * RET_HEADS, (cc + 1) * RET_HEADS)
            gb = gh.astype(BF16)
            dk_extra[cc] = _bmm_nt(vz[sl], gb)
            dv_extra[cc] = _bmm(kb[sl], gb) * zeta_v
            gh = dstate_local[sl] + gch_v * gh
        dstate[...] = gh
        cos_b, sin_b = _rope_blocks(cos_ref), _rope_blocks(sin_ref)
        _ret_unblock(dq_ref, _rope_bwd(dq, cos_b, sin_b))
        _ret_unblock(dk_ref, _rope_bwd((dk + jnp.concatenate(dk_extra, axis=0)) * K_SCALE, cos_b, sin_b))
        _ret_unblock(dv_ref, dv + jnp.concatenate(dv_extra, axis=0))

    cst = _full((RET_HEADS, RET_CHUNK, RET_CHUNK))
    rev = _rows_rev(t, RET_WIDTH, nt)
    act = jax.ShapeDtypeStruct((seq, RET_WIDTH), BF16)
    return _call(
        body, "ret_bwd", (nt,),
        [rev] * 4 + [_full((1, RET_WIDTH)), cst, cst, cst, cst,
                     pl.BlockSpec((cps, RET_HEADS, HEAD_DIM, HEAD_DIM), lambda i: (nt - 1 - i, 0, 0, 0)), rev,
                     _rows_rev(t, HEAD_DIM, nt), _rows_rev(t, HEAD_DIM, nt)],
        [rev] * 4 + [_full((1, RET_WIDTH))], [act, act, act, act, jax.ShapeDtypeStruct((1, RET_WIDTH), F32)],
        [pltpu.VMEM((RET_HEADS, HEAD_DIM, HEAD_DIM), F32)],
        (q, k, v, gate, gn_gain, dec, xi, zeta, gch, rprev, dy_ret, cos2, sin2), exchange)


MXU_TILE = 256
S5_STATE_TILES = 2 * N_STATES // MXU_TILE
S5_CHANNELS_PER_TILE = MXU_TILE // SSM_STATE * SSM_GROUP
S5_TILES_SHAPE = (S5_STATE_TILES, LANES, MXU_TILE)


def _chan_block(kk):
    return ((kk % (S5_STATE_TILES // 2)) * S5_CHANNELS_PER_TILE) // LANES * LANES


def _s5_tiles(blocks_re, blocks_im):
    per = MXU_TILE // SSM_STATE
    half = S5_STATE_TILES // 2
    eye = jnp.eye(per, dtype=F32)
    in_half = (jnp.arange(half) % (LANES // S5_CHANNELS_PER_TILE))[:, None] == jnp.arange(LANES // S5_CHANNELS_PER_TILE)[None, :]
    out = []
    for blk in (blocks_re, blocks_im):
        t = blk.reshape(half, per, SSM_GROUP, 1, SSM_STATE) * eye[None, :, None, :, None]
        t = t.reshape(half, 1, S5_CHANNELS_PER_TILE, MXU_TILE)
        out.append(jnp.where(in_half[:, :, None, None], t, 0.0).reshape(half, LANES, MXU_TILE))
    return jnp.concatenate(out, axis=0)


def _s5_discretise(lam_re, lam_im, log_dt, b_re, b_im, c_re, c_im):
    lam = lax.complex(jnp.minimum(lam_re, -1e-4), lam_im)
    dt = jnp.exp(log_dt)[:, None]
    lam_bar = jnp.exp(lam * dt)
    b_bar = ((lam_bar - 1.0) / lam)[:, :, None] * lax.complex(b_re, b_im)
    b_tiles = _s5_tiles(jnp.swapaxes(jnp.real(b_bar), 1, 2), jnp.swapaxes(jnp.imag(b_bar), 1, 2))
    return jnp.real(lam_bar), jnp.imag(lam_bar), b_tiles, _s5_tiles(c_re, -c_im)


def _s5_tables(lbr, lbi):
    lr = lbr.reshape(1, N_STATES)
    li = lbi.reshape(1, N_STATES)
    pr, pi = lr, li
    while pr.shape[0] < S5_STEPS:
        top_r, top_i = pr[-1:], pi[-1:]
        pr, pi = (jnp.concatenate([pr, pr * top_r - pi * top_i], axis=0), jnp.concatenate([pi, pr * top_i + pi * top_r], axis=0))
    bc = lambda a: jnp.broadcast_to(a, (SUBLANES, N_STATES))
    rep = lambda rows: jnp.broadcast_to(rows[:, None, :], (S5_STEPS, SUBLANES, N_STATES)).reshape(S5_STEPS * SUBLANES, N_STATES)
    return bc(lr), bc(li), rep(pr), rep(pi)


def _cmul(ar, ai, br, bi):
    return ar * br - ai * bi, ar * bi + ai * br


def _seg(i):
    return pl.ds(pl.multiple_of(i * SUBLANES, SUBLANES), SUBLANES)


def _scan_order():
    r = np.arange(S5_TILE)
    token = (r % SUBLANES) * S5_STEPS + r // SUBLANES
    p = (token[:, None] == np.arange(S5_TILE)[None, :]).astype(np.float32)
    return jnp.asarray(p, dtype=BF16), jnp.asarray(p.T, dtype=BF16)


def _to_scan_order(p_ref, rows_bf16):
    return _dot(p_ref[...], rows_bf16).astype(BF16)


def _from_scan_order(pt_ref, rows):
    hi = rows.astype(BF16)
    rest = rows - hi.astype(F32)
    mid = rest.astype(BF16)
    lo = (rest - mid.astype(F32)).astype(BF16)
    pt = pt_ref[...]
    return _dot(pt, hi) + _dot(pt, mid) + _dot(pt, lo)


def _s5_channels_to_states(chan_b, m_ref, xs):
    for kk in range(S5_STATE_TILES):
        lo = _chan_block(kk)
        res = _dot(chan_b[:, lo:lo + LANES], m_ref[kk])
        xs[2 * kk] = res[:, :LANES]
        xs[2 * kk + 1] = res[:, LANES:]


def _s5_states_to_channels(sb_ref, m_ref):
    out = []
    for lo in range(0, SSM_WIDTH, LANES):
        acc = None
        for kk in range(S5_STATE_TILES):
            if _chan_block(kk) == lo:
                term = _dot_nt(sb_ref[:, kk * MXU_TILE:(kk + 1) * MXU_TILE], m_ref[kk])
                acc = term if acc is None else acc + term
        out.append(acc)
    return jnp.concatenate(out, axis=1)


def _s5_outer_acc(acc_ref, chan_b, states_ref):
    for kk in range(S5_STATE_TILES):
        lo = _chan_block(kk)
        acc_ref[kk] += _dot_tn(chan_b[:, lo:lo + LANES], states_ref[:, kk * MXU_TILE:(kk + 1) * MXU_TILE])


def _s5_scan_tile(xs, lr_ref, li_ref, pr_ref, pi_ref, carry_re, carry_im, conj, reverse, prev_of=None):
    sgn = -1.0 if conj else 1.0
    seg_in_re, seg_in_im, sums_re, sums_im = [], [], [], []
    group = S5_GROUP if prev_of is None else S5_GROUP // 2
    for grp in range(S5_COLBLK // group):
        blks = [grp * group + j for j in range(group)]
        lrs = [lr_ref[:, b * LANES:(b + 1) * LANES] for b in blks]
        lis = [sgn * li_ref[:, b * LANES:(b + 1) * LANES] for b in blks]

        def step(it, carry, blks=blks, lrs=lrs, lis=lis):
            i = (S5_STEPS - 1 - it) if reverse else it
            out_r, out_i = [], []
            for j, b in enumerate(blks):
                xr, xi = _cmul(lrs[j], lis[j], carry[j], carry[group + j])
                xr = xr + xs[b, _seg(i), :]
                xi = xi + xs[S5_COLBLK + b, _seg(i), :]
                xs[b, _seg(i), :] = xr
                xs[S5_COLBLK + b, _seg(i), :] = xi
                out_r.append(xr)
                out_i.append(xi)
            return tuple(out_r + out_i)

        zeros = tuple(jnp.zeros((SUBLANES, LANES), F32) for _ in range(2 * group))
        ends = lax.fori_loop(0, S5_STEPS, step, zeros, unroll=4)

        ins_r, ins_i = [], []
        for j, b in enumerate(blks):
            cols = slice(b * LANES, (b + 1) * LANES)
            pr = pr_ref[SUBLANES * S5_STEPS - 1:SUBLANES * S5_STEPS, cols]
            pi = sgn * pi_ref[SUBLANES * S5_STEPS - 1:SUBLANES * S5_STEPS, cols]
            cur_r = carry_re[:, cols]
            cur_i = carry_im[:, cols]
            row_id = lax.broadcasted_iota(jnp.int32, (SUBLANES, LANES), 0)
            in_r = jnp.zeros((SUBLANES, LANES), F32)
            in_i = jnp.zeros((SUBLANES, LANES), F32)
            order = range(SUBLANES - 1, -1, -1) if reverse else range(SUBLANES)
            for sgm in order:
                in_r = jnp.where(row_id == sgm, cur_r, in_r)
                in_i = jnp.where(row_id == sgm, cur_i, in_i)
                mr, mi = _cmul(pr, pi, cur_r, cur_i)
                cur_r = mr + ends[j][sgm:sgm + 1, :]
                cur_i = mi + ends[group + j][sgm:sgm + 1, :]
            carry_re[:, cols] = cur_r
            carry_im[:, cols] = cur_i
            ins_r.append(in_r)
            ins_i.append(in_i)

        def fix(it, c, blks=blks, ins_r=ins_r, ins_i=ins_i):
            pw = (S5_STEPS - 1 - it) if reverse else it
            out = []
            for j, b in enumerate(blks):
                cols = slice(b * LANES, (b + 1) * LANES)
                prow = pl.ds(pl.multiple_of(pw * SUBLANES, SUBLANES), SUBLANES)
                fr, fi = _cmul(pr_ref[prow, cols], sgn * pi_ref[prow, cols], ins_r[j], ins_i[j])
                vr = xs[b, _seg(it), :] + fr
                vi = xs[S5_COLBLK + b, _seg(it), :] + fi
                xs[b, _seg(it), :] = vr
                xs[S5_COLBLK + b, _seg(it), :] = vi
                if prev_of is not None:
                    opr, opi, sr, si = c[4 * j:4 * j + 4]
                    out += [prev_of[0][b, _seg(it), :], prev_of[0][S5_COLBLK + b, _seg(it), :],
                            sr + vr * opr + vi * opi, si + vi * opr - vr * opi]
            return tuple(out)

        init = ()
        if prev_of is not None:
            zero = jnp.zeros((SUBLANES, LANES), F32)
            for b in blks:
                init += (prev_of[1][b], prev_of[2][b], zero, zero)
        done = lax.fori_loop(0, S5_STEPS, fix, init, unroll=4)
        for j in range(len(blks) if prev_of is not None else 0):
            sums_re.append(done[4 * j + 2])
            sums_im.append(done[4 * j + 3])
        seg_in_re += ins_r
        seg_in_im += ins_i
    return seg_in_re, seg_in_im, sums_re, sums_im


def _s5_pack(xs, dst):
    for b in range(2 * S5_COLBLK):
        dst[:, b * LANES:(b + 1) * LANES] = xs[b].astype(BF16)


def _s5_fwd(u, b_blk, c_blk, tables, d_skip, exchange=None):
    seq = u.shape[0]
    t = S5_TILE
    nt = seq // t
    lr8, li8, p_re, p_im = tables

    p, pt = _scan_order()

    def body(u_ref, p_ref, pt_ref, b_ref, c_ref, lr_ref, li_ref, pr_ref, pi_ref, d_ref, y_ref, start_ref,
             xs, xb, carry_re, carry_im):
        @pl.when(pl.program_id(0) == 0)
        def _():
            carry_re[...] = jnp.zeros_like(carry_re)
            carry_im[...] = jnp.zeros_like(carry_im)

        start_ref[0, 0:1, :] = carry_re[...]
        start_ref[0, 1:2, :] = carry_im[...]
        uv = u_ref[...]
        _s5_channels_to_states(_to_scan_order(p_ref, uv.astype(BF16)), b_ref, xs)
        _s5_scan_tile(xs, lr_ref, li_ref, pr_ref, pi_ref, carry_re, carry_im, conj=False, reverse=False)
        _s5_pack(xs, xb)
        y_ref[...] = _from_scan_order(pt_ref, _s5_states_to_channels(xb, c_ref)) + d_ref[...] * uv

    return _call(
        body, "s5_fwd", (nt,),
        [_rows(t, SSM_WIDTH), _full((t, t)), _full((t, t)), _full(S5_TILES_SHAPE), _full(S5_TILES_SHAPE),
         _full((SUBLANES, N_STATES)), _full((SUBLANES, N_STATES)), _full((SUBLANES * S5_STEPS, N_STATES)),
         _full((SUBLANES * S5_STEPS, N_STATES)), _full((1, SSM_WIDTH))],
        [_rows(t, SSM_WIDTH), pl.BlockSpec((1, 2, N_STATES), lambda i: (i, 0, 0))],
        [jax.ShapeDtypeStruct((seq, SSM_WIDTH), F32), jax.ShapeDtypeStruct((nt, 2, N_STATES), F32)],
        [pltpu.VMEM((2 * S5_COLBLK, t, LANES), F32), pltpu.VMEM((t, 2 * N_STATES), BF16),
         pltpu.VMEM((1, N_STATES), F32), pltpu.VMEM((1, N_STATES), F32)],
        (u, p, pt, b_blk, c_blk, lr8, li8, p_re, p_im, d_skip), exchange)


def _s5_bwd(u, dy, b_blk, c_blk, tables, d_skip, starts, exchange=None):
    seq = u.shape[0]
    t = S5_TILE
    nt = seq // t
    lr8, li8, p_re, p_im = tables
    p, pt = _scan_order()

    def body(u_ref, dy_ref, p_ref, pt_ref, b_ref, c_ref, lr_ref, li_ref, pr_ref, pi_ref, d_ref, start_ref,
             du_ref, dd_ref, dlam_ref, db_ref, dc_ref,
             xs, as_, xb, ab, carry_re, carry_im, acar_re, acar_im):
        @pl.when(pl.program_id(0) == 0)
        def _():
            acar_re[...] = jnp.zeros_like(acar_re)
            acar_im[...] = jnp.zeros_like(acar_im)
            dd_ref[...] = jnp.zeros_like(dd_ref)
            dlam_ref[...] = jnp.zeros_like(dlam_ref)
            db_ref[...] = jnp.zeros_like(db_ref)
            dc_ref[...] = jnp.zeros_like(dc_ref)

        uv = u_ref[...]
        dyv = dy_ref[...]
        ub = _to_scan_order(p_ref, uv.astype(BF16))
        dyb = _to_scan_order(p_ref, dyv.astype(BF16))
        carry_re[...] = start_ref[0, 0:1, :]
        carry_im[...] = start_ref[0, 1:2, :]
        _s5_channels_to_states(ub, b_ref, xs)
        in_re, in_im, _, _ = _s5_scan_tile(xs, lr_ref, li_ref, pr_ref, pi_ref, carry_re, carry_im, conj=False, reverse=False)
        _s5_channels_to_states(dyb, c_ref, as_)
        _, _, dl_re, dl_im = _s5_scan_tile(as_, lr_ref, li_ref, pr_ref, pi_ref, acar_re, acar_im, conj=True, reverse=True,
                                           prev_of=(xs, in_re, in_im))
        for b in range(S5_COLBLK):
            cols = slice(b * LANES, (b + 1) * LANES)
            dlam_ref[0, :, cols] += dl_re[b]
            dlam_ref[1, :, cols] += dl_im[b]
        _s5_pack(xs, xb)
        _s5_pack(as_, ab)
        du_ref[...] = (_from_scan_order(pt_ref, _s5_states_to_channels(ab, b_ref)) + d_ref[...] * dyv).astype(BF16)
        dd_ref[...] += jnp.sum(dyv * uv, axis=0, keepdims=True)
        _s5_outer_acc(db_ref, ub, ab)
        _s5_outer_acc(dc_ref, dyb, xb)

    rev = _rows_rev(t, SSM_WIDTH, nt)
    vec = lambda: pltpu.VMEM((1, N_STATES), F32)
    outer = S5_TILES_SHAPE
    return _call(
        body, "s5_bwd", (nt,),
        [rev, rev, _full((t, t)), _full((t, t)), _full(S5_TILES_SHAPE), _full(S5_TILES_SHAPE),
         _full((SUBLANES, N_STATES)), _full((SUBLANES, N_STATES)), _full((SUBLANES * S5_STEPS, N_STATES)),
         _full((SUBLANES * S5_STEPS, N_STATES)), _full((1, SSM_WIDTH)),
         pl.BlockSpec((1, 2, N_STATES), lambda i: (nt - 1 - i, 0, 0))],
        [rev, _full((1, SSM_WIDTH)), _full((2, SUBLANES, N_STATES)), _full(outer), _full(outer)],
        [jax.ShapeDtypeStruct((seq, SSM_WIDTH), BF16), jax.ShapeDtypeStruct((1, SSM_WIDTH), F32),
         jax.ShapeDtypeStruct((2, SUBLANES, N_STATES), F32), jax.ShapeDtypeStruct(outer, F32), jax.ShapeDtypeStruct(outer, F32)],
        [pltpu.VMEM((2 * S5_COLBLK, t, LANES), F32), pltpu.VMEM((2 * S5_COLBLK, t, LANES), F32),
         pltpu.VMEM((t, 2 * N_STATES), BF16), pltpu.VMEM((t, 2 * N_STATES), BF16), vec(), vec(), vec(), vec()],
        (u, dy, p, pt, b_blk, c_blk, lr8, li8, p_re, p_im, d_skip, starts), exchange)


def _mix_fwd(y_ssm, y_ret, x, gain, w_glu, w_out):
    seq = x.shape[0]
    t = TOK_TILE

    def body(ys_ref, yr_ref, x_ref, g_ref, wg_ref, wo_ref, x1_ref, mix_ref):
        g0 = _gelu(ys_ref[...]).astype(BF16)
        z = _dot(g0, wg_ref[...])
        glu = (z[:, :SSM_WIDTH] * _sigmoid(z[:, SSM_WIDTH:])).astype(BF16)
        mix = _dot(yr_ref[...], wo_ref[:RET_WIDTH, :]) + _dot(glu, wo_ref[RET_WIDTH:, :])
        mix_ref[...] = mix
        n, _ = _rms(mix)
        x1_ref[...] = x_ref[...] + n * g_ref[...]

    act = jax.ShapeDtypeStruct((seq, D_MODEL), F32)
    return pl.pallas_call(
        body, name="mix_fwd", grid=(seq // t,),
        in_specs=[_rows(t, SSM_WIDTH), _rows(t, RET_WIDTH), _rows(t, D_MODEL), _full((1, D_MODEL)),
                  _full((SSM_WIDTH, 2 * SSM_WIDTH)), _full((D_MODEL, D_MODEL))],
        out_specs=[_rows(t, D_MODEL)] * 2, out_shape=[act, act],
        compiler_params=_params("parallel"),
    )(y_ssm, y_ret, x, gain, w_glu, w_out)


def _mix_bwd(dx1, mix, gain, y_ssm, y_ret, w_glu, w_out, exchange=None):
    seq = dx1.shape[0]
    t = TOK_TILE

    def body(dx1_ref, mix_ref, g_ref, ys_ref, yr_ref, wg_ref, wo_ref, dyr_ref, dys_ref, dwo_ref, dwg_ref, dg_ref):
        @pl.when(pl.program_id(0) == 0)
        def _():
            dwo_ref[...] = jnp.zeros_like(dwo_ref)
            dwg_ref[...] = jnp.zeros_like(dwg_ref)
            dg_ref[...] = jnp.zeros_like(dg_ref)

        n, r = _rms(mix_ref[...])
        dmix, dgs = _rms_bwd(n, r, g_ref[...], dx1_ref[...])
        dg_ref[...] += dgs
        dmb = dmix.astype(BF16)
        dcat = _dot_nt(dmb, wo_ref[...])
        dyr_ref[...] = dcat[:, :RET_WIDTH]
        dglu = dcat[:, RET_WIDTH:]
        ys = ys_ref[...]
        g0 = _gelu(ys).astype(BF16)
        z = _dot(g0, wg_ref[...])
        a = z[:, :SSM_WIDTH]
        sb = _sigmoid(z[:, SSM_WIDTH:])
        dwo_ref[:RET_WIDTH, :] += _dot_tn(yr_ref[...], dmb)
        dwo_ref[RET_WIDTH:, :] += _dot_tn((a * sb).astype(BF16), dmb)
        dz = jnp.concatenate([dglu * sb, dglu * a * sb * (1.0 - sb)], axis=1).astype(BF16)
        dwg_ref[...] += _dot_tn(g0, dz)
        dys_ref[...] = _dot_nt(dz, wg_ref[...]) * _gelu_grad(ys)

    half = jax.ShapeDtypeStruct((seq, RET_WIDTH), F32)
    return _call(
        body, "mix_bwd", (seq // t,),
        [_rows(t, D_MODEL), _rows(t, D_MODEL), _full((1, D_MODEL)), _rows(t, SSM_WIDTH), _rows(t, RET_WIDTH),
         _full((SSM_WIDTH, 2 * SSM_WIDTH)), _full((D_MODEL, D_MODEL))],
        [_rows(t, RET_WIDTH), _rows(t, SSM_WIDTH), _full((D_MODEL, D_MODEL)), _full((SSM_WIDTH, 2 * SSM_WIDTH)),
         _full((1, D_MODEL))],
        [half, half, jax.ShapeDtypeStruct((D_MODEL, D_MODEL), F32), jax.ShapeDtypeStruct((SSM_WIDTH, 2 * SSM_WIDTH), F32),
         jax.ShapeDtypeStruct((1, D_MODEL), F32)],
        [], (dx1, mix, gain, y_ssm, y_ret, w_glu, w_out), exchange)


def _mlp_a(x1, target, gain_pre, gain_post, w1, w2):
    seq = x1.shape[0]
    t = MLP_TILE

    def body(x1_ref, tg_ref, gp_ref, gq_ref, w1_ref, w2_ref, df_ref, dx2_ref, dw2_ref, dgq_ref, sq_ref):
        @pl.when(pl.program_id(0) == 0)
        def _():
            dw2_ref[...] = jnp.zeros_like(dw2_ref)
            dgq_ref[...] = jnp.zeros_like(dgq_ref)
            sq_ref[...] = jnp.zeros_like(sq_ref)

        x1v = x1_ref[...]
        n3, _ = _rms(x1v)
        h = (n3 * gp_ref[...]).astype(BF16)
        rl = jnp.maximum(_dot(h, w1_ref[...]), 0.0)
        act = (rl * rl).astype(BF16)
        n4, r4 = _rms(_dot(act, w2_ref[...]))
        gq = gq_ref[...]
        err = x1v + n4 * gq - tg_ref[...]
        sq_ref[...] += jnp.sum(err * err, axis=0, keepdims=True)
        dx2 = err * (1.0 / D_MODEL)
        dx2_ref[...] = dx2
        dm, dgs = _rms_bwd(n4, r4, gq, dx2)
        dgq_ref[...] += dgs
        dmb = dm.astype(BF16)
        dw2_ref[...] += _dot_tn(act, dmb)
        df_ref[...] = (_dot_nt(dmb, w2_ref[...]) * (2.0 * rl)).astype(BF16)

    return pl.pallas_call(
        body, name="mlp_a", grid=(seq // t,),
        in_specs=[_rows(t, D_MODEL), _rows(t, D_MODEL), _full((1, D_MODEL)), _full((1, D_MODEL)),
                  _full((D_MODEL, D_FF)), _full((D_FF, D_MODEL))],
        out_specs=[_rows(t, D_FF), _rows(t, D_MODEL), _full((D_FF, D_MODEL)), _full((1, D_MODEL)), _full((1, D_MODEL))],
        out_shape=[jax.ShapeDtypeStruct((seq, D_FF), BF16), jax.ShapeDtypeStruct((seq, D_MODEL), F32),
                   jax.ShapeDtypeStruct((D_FF, D_MODEL), F32), jax.ShapeDtypeStruct((1, D_MODEL), F32),
                   jax.ShapeDtypeStruct((1, D_MODEL), F32)],
        compiler_params=_params("arbitrary"),
    )(x1, target, gain_pre, gain_post, w1, w2)


def _mlp_b(df, dx2, x1, gain_pre, w1):
    seq = x1.shape[0]
    t = TOK_TILE

    def body(df_ref, dx2_ref, x1_ref, gp_ref, w1_ref, dx1_ref, dw1_ref, dgp_ref):
        @pl.when(pl.program_id(0) == 0)
        def _():
            dw1_ref[...] = jnp.zeros_like(dw1_ref)
            dgp_ref[...] = jnp.zeros_like(dgp_ref)

        n3, r3 = _rms(x1_ref[...])
        gp = gp_ref[...]
        h = (n3 * gp).astype(BF16)
        dfv = df_ref[...]
        dw1_ref[...] += _dot_tn(h, dfv)
        dx, dgs = _rms_bwd(n3, r3, gp, _dot_nt(dfv, w1_ref[...]))
        dgp_ref[...] += dgs
        dx1_ref[...] = dx2_ref[...] + dx

    return pl.pallas_call(
        body, name="mlp_b", grid=(seq // t,),
        in_specs=[_rows(t, D_FF), _rows(t, D_MODEL), _rows(t, D_MODEL), _full((1, D_MODEL)), _full((D_MODEL, D_FF))],
        out_specs=[_rows(t, D_MODEL), _full((D_MODEL, D_FF)), _full((1, D_MODEL))],
        out_shape=[jax.ShapeDtypeStruct((seq, D_MODEL), F32), jax.ShapeDtypeStruct((D_MODEL, D_FF), F32),
                   jax.ShapeDtypeStruct((1, D_MODEL), F32)],
        compiler_params=_params("arbitrary"),
    )(df, dx2, x1, gain_pre, w1)


def _local_step(x, target, small, weights, ids=None):
    cos2, sin2 = _rope_tables(x.shape[0])
    ret_consts = _ret_consts()

    s5_names = ("ssm_lambda_re", "ssm_lambda_im", "ssm_log_dt", "ssm_b_re", "ssm_b_im", "ssm_c_re", "ssm_c_im")
    (lbr, lbi, b_tiles, c_tiles), disc_vjp = jax.vjp(_s5_discretise, *[small[name] for name in s5_names])
    tables = _s5_tables(lbr, lbi)
    b_blk = b_tiles.astype(BF16)
    c_blk = c_tiles.astype(BF16)
    d_skip = small["ssm_d"].reshape(1, SSM_WIDTH)
    gn_gain = small["ret_gn_gain"].reshape(1, RET_WIDTH)
    g_mix_pre = small["norm_mix_pre"].reshape(1, D_MODEL)
    g_mix_post = small["norm_mix_post"].reshape(1, D_MODEL)
    g_mlp_pre = small["norm_mlp_pre"].reshape(1, D_MODEL)
    g_mlp_post = small["norm_mlp_post"].reshape(1, D_MODEL)

    dist = ids is not None
    w_in, w_glu, w_out, w_ff1, w_ff2 = weights
    (q, k, v, gate, u), got = _inproj_fwd(x, g_mix_pre, w_in, cos2, sin2, _x_gather_send((1, 2), [w_glu, w_out]) if dist else None)
    if dist:
        w_glu, w_out = got
    (y_ssm, starts), got = _s5_fwd(u, b_blk, c_blk, tables, d_skip, _x_gather_send(LATE, [w_ff1, w_ff2]) if dist else None)
    if dist:
        w_ff1, w_ff2 = got
    (y_ret, rprev), got = _ret_fwd(q, k, v, gate, gn_gain, ret_consts,
                                   _x_gather_pass((1, 2) + LATE, [w_glu, w_out, w_ff1, w_ff2]) if dist else None)
    if dist:
        w_glu, w_out, w_ff1, w_ff2 = got
    x1, mix = _mix_fwd(y_ssm, y_ret, x, g_mix_post, w_glu, w_out)
    df, dx2, d_ff2, dg_mlp_post, sq = _mlp_a(x1, target, g_mlp_pre, g_mlp_post, w_ff1, w_ff2)
    dx1, d_ff1, dg_mlp_pre = _mlp_b(df, dx2, x1, g_mlp_pre, w_ff1)
    (dy_ret, dy_ssm, d_out, d_glu, dg_mix_post), got = _mix_bwd(dx1, mix, g_mix_post, y_ssm, y_ret, w_glu, w_out,
                                                               _x_pair(LATE, [d_ff1, d_ff2]) if dist else None)
    if dist:
        sums, sums_bf16 = _pair_sum(ids, LATE, [d_ff1, d_ff2], got)
    (du, dd, dlam8, db_tiles, dc_tiles), got = _s5_bwd(
        u, dy_ssm, b_blk, c_blk, tables, d_skip, starts,
        _x_join(_x_chip(LATE, sums_bf16), _x_pair(MID, [d_glu, d_out])) if dist else None)
    if dist:
        parts = _chip_sum(ids, LATE, sums, got[:len(LATE)])
        sums, sums_bf16 = _pair_sum(ids, MID, [d_glu, d_out], got[len(LATE):])
    (dq, dk, dv, dgate, dgn), got = _ret_bwd(q, k, v, gate, gn_gain, ret_consts, rprev, dy_ret, cos2, sin2,
                                             _x_join(_x_share(LATE, parts), _x_chip(MID, sums_bf16)) if dist else None)
    if dist:
        late_grads = got[:len(LATE)]
        parts = _chip_sum(ids, MID, sums, got[len(LATE):])
        mid_parts = parts
    (dx, d_in, dg_mix_pre), _ = _inproj_bwd(x, g_mix_pre, w_in, [dq, dk, dv, dgate, du], dx1)

    dlam = jnp.sum(dlam8, axis=1)
    s5_grads = disc_vjp((dlam[0].reshape(SSM_GROUPS, SSM_STATE), dlam[1].reshape(SSM_GROUPS, SSM_STATE), db_tiles, dc_tiles))
    small_grads = {
        "norm_mix_pre": dg_mix_pre, "norm_mix_post": dg_mix_post, "ret_gn_gain": dgn, "ssm_d": dd,
        "norm_mlp_pre": dg_mlp_pre, "norm_mlp_post": dg_mlp_post, "loss_sum": 0.5 * jnp.sum(sq) / D_MODEL,
    }
    small_grads.update(dict(zip(s5_names, s5_grads)))
    if dist:
        return dx, d_in, mid_parts, late_grads, small_grads
    return dx, (d_in, d_glu, d_out, d_ff1, d_ff2), small_grads


BIG = (
    ("w_in", D_MODEL, IN_COLS, 1),
    ("w_glu", SSM_WIDTH, 2 * SSM_WIDTH, 1),
    ("w_out", D_MODEL, D_MODEL, 0),
    ("w_ff1", D_MODEL, D_FF, 1),
    ("w_ff2", D_FF, D_MODEL, 0),
)
SMALL = (
    ("norm_mix_pre", (D_MODEL,)), ("norm_mix_post", (D_MODEL,)), ("ret_gn_gain", (RET_WIDTH,)),
    ("ssm_lambda_re", (SSM_GROUPS, SSM_STATE)), ("ssm_lambda_im", (SSM_GROUPS, SSM_STATE)), ("ssm_log_dt", (SSM_GROUPS,)),
    ("ssm_b_re", (SSM_GROUPS, SSM_STATE, SSM_GROUP)), ("ssm_b_im", (SSM_GROUPS, SSM_STATE, SSM_GROUP)),
    ("ssm_c_re", (SSM_GROUPS, SSM_GROUP, SSM_STATE)), ("ssm_c_im", (SSM_GROUPS, SSM_GROUP, SSM_STATE)),
    ("ssm_d", (SSM_WIDTH,)), ("norm_mlp_pre", (D_MODEL,)), ("norm_mlp_post", (D_MODEL,)),
)
SMALL_ROWS = 1152
EXCH_TILES = 8


PACKED = SMALL + (("loss_sum", (1,)),)


def _packed_rows(shape):
    return -(-math.prod(shape) // (SUBLANES * LANES)) * SUBLANES


def _pack_small(tree):
    rows = []
    for name, shape in PACKED:
        size = math.prod(shape)
        flat = tree[name].reshape(-1).astype(F32) if name in tree else jnp.zeros((size,), F32)
        rows.append(jnp.pad(flat, (0, _packed_rows(shape) * LANES - size)).reshape(-1, LANES))
    used = sum(r.shape[0] for r in rows)
    rows.append(jnp.zeros((SMALL_ROWS - used, LANES), F32))
    return jnp.concatenate(rows, axis=0)


def _unpack_small(packed):
    out, row = {}, 0
    for name, shape in PACKED:
        size = math.prod(shape)
        out[name] = packed[row:row + _packed_rows(shape)].reshape(-1)[:size].reshape((1,) + shape)
        row += _packed_rows(shape)
    return out


def _half_shape(r, c, axis):
    return (r // 2, c) if axis == 1 else (r, c // 2)


def _region_shape(r, c, axis):
    return (r // 2, c // N_CHIPS) if axis == 1 else (r // N_CHIPS, c // 2)


def _shard_shape(r, c, axis):
    return (r, c // N_CHIPS) if axis == 1 else (r // N_CHIPS, c)


def _ds(start, size):
    return pl.ds(pl.multiple_of(start * size, size), size)


def _region_of_full(ref, r, c, axis, shard, half):
    if axis == 1:
        return ref.at[_ds(half, r // 2), _ds(shard, c // N_CHIPS)]
    return ref.at[_ds(shard, r // N_CHIPS), _ds(half, c // 2)]


def _half_of_full(ref, r, c, axis, half):
    if axis == 1:
        return ref.at[_ds(half, r // 2), :]
    return ref.at[:, _ds(half, c // 2)]


def _half_of_shard(ref, r, c, axis, half):
    if axis == 1:
        return ref.at[_ds(half, r // 2), :]
    return ref.at[:, _ds(half, c // 2)]


def _region_of_half(ref, r, c, axis, shard):
    if axis == 1:
        return ref.at[:, _ds(shard, c // N_CHIPS)]
    return ref.at[_ds(shard, r // N_CHIPS), :]


def _place():
    x, y, c = lax.axis_index("x"), lax.axis_index("y"), lax.axis_index("c")
    chips = [(1 - x, y), (x, 1 - y), (1 - x, 1 - y)]
    return x, y, c, chips


LATE = (3, 4)
MID = (1, 2)
FIRST = (0,)
SMALL_HALF = (SMALL_ROWS // 2, LANES)


def _remote(src, dst, send_sem, recv_sem, to):
    return pltpu.make_async_remote_copy(src_ref=src, dst_ref=dst, send_sem=send_sem, recv_sem=recv_sem,
                                        device_id=to, device_id_type=MESH)


def _same(arrays):
    return [jax.ShapeDtypeStruct(a.shape, a.dtype) for a in arrays]


def _x_gather_send(ws, fulls):
    def copies(ins, outs, send_sems, recv_sems):
        x, y, c, chips = _place()
        out = []
        for j, w in enumerate(ws):
            _, r, cc, axis = BIG[w]
            mine = _region_of_full(outs[j], r, cc, axis, 2 * x + y, c)
            out += [_remote(mine, mine, send_sems.at[3 * j + k], recv_sems.at[3 * j + k], (cx, cy, c))
                    for k, (cx, cy) in enumerate(chips)]
        return out

    return _Exchange(fulls, _same(fulls), {j: j for j in range(len(ws))}, 3 * len(ws), copies)


def _x_gather_pass(ws, fulls):
    def copies(ins, outs, send_sems, recv_sems):
        x, y, c, chips = _place()
        out = []
        for j, w in enumerate(ws):
            _, r, cc, axis = BIG[w]
            for k, (cx, cy) in enumerate(chips):
                landed = _region_of_full(outs[j], r, cc, axis, 2 * cx + cy, c)
                out.append(_remote(landed, landed, send_sems.at[3 * j + k], recv_sems.at[3 * j + k], (x, y, 1 - c)))
        return out

    return _Exchange(fulls, _same(fulls), {j: j for j in range(len(ws))}, 3 * len(ws), copies)


def _x_pair(ws, grads, small=None):
    def copies(ins, outs, send_sems, recv_sems):
        x, y, c, _ = _place()
        out = []
        for j, w in enumerate(ws):
            _, r, cc, axis = BIG[w]
            out.append(_remote(_half_of_full(ins[j], r, cc, axis, 1 - c), outs[j], send_sems.at[j], recv_sems.at[j], (x, y, 1 - c)))
        if small is not None:
            j = len(ws)
            out.append(_remote(ins[j].at[_ds(1 - c, SMALL_ROWS // 2), :], outs[j], send_sems.at[j], recv_sems.at[j], (x, y, 1 - c)))
        return out

    shapes = [jax.ShapeDtypeStruct(_half_shape(*BIG[w][1:]), F32) for w in ws]
    extra = [] if small is None else [small]
    return _Exchange(list(grads) + extra, shapes + [jax.ShapeDtypeStruct(SMALL_HALF, F32)] * len(extra), {},
                     len(ws) + len(extra), copies)


def _x_chip(ws, sums_bf16, small_sum=None):
    def copies(ins, outs, send_sems, recv_sems):
        x, y, c, chips = _place()
        out = []
        for j, w in enumerate(ws):
            _, r, cc, axis = BIG[w]
            out += [_remote(_region_of_half(ins[j], r, cc, axis, 2 * cx + cy), outs[j].at[k],
                            send_sems.at[3 * j + k], recv_sems.at[3 * j + k], (cx, cy, c)) for k, (cx, cy) in enumerate(chips)]
        if small_sum is not None:
            j = len(ws)
            out += [_remote(ins[j], outs[j].at[k], send_sems.at[3 * j + k], recv_sems.at[3 * j + k], (cx, cy, c))
                    for k, (cx, cy) in enumerate(chips)]
        return out

    shapes = [jax.ShapeDtypeStruct((3,) + _region_shape(*BIG[w][1:]), BF16) for w in ws]
    extra = [] if small_sum is None else [small_sum]
    return _Exchange(list(sums_bf16) + extra, shapes + [jax.ShapeDtypeStruct((3,) + SMALL_HALF, F32)] * len(extra), {},
                     3 * (len(ws) + len(extra)), copies)


def _x_share(ws, shards, small=None):
    def copies(ins, outs, send_sems, recv_sems):
        x, y, c, _ = _place()
        out = []
        for j, w in enumerate(ws):
            _, r, cc, axis = BIG[w]
            mine = _half_of_shard(outs[j], r, cc, axis, c)
            out.append(_remote(mine, mine, send_sems.at[j], recv_sems.at[j], (x, y, 1 - c)))
        if small is not None:
            j = len(ws)
            mine = outs[j].at[_ds(c, SMALL_ROWS // 2), :]
            out.append(_remote(mine, mine, send_sems.at[j], recv_sems.at[j], (x, y, 1 - c)))
        return out

    arrays = list(shards) + ([] if small is None else [small])
    return _Exchange(arrays, _same(arrays), {j: j for j in range(len(arrays))}, len(arrays), copies)


class _Offset:
    def __init__(self, sems, base):
        self._sems, self._base = sems, base

    @property
    def at(self):
        return self

    def __getitem__(self, i):
        return self._sems.at[self._base + i]


def _x_join(a, b):
    ka_in, ka_out = len(a.operands), len(a.out_shapes)

    def copies(ins, outs, send_sems, recv_sems):
        return (a.copies(ins[:ka_in], outs[:ka_out], send_sems, recv_sems)
                + b.copies(ins[ka_in:], outs[ka_out:], _Offset(send_sems, a.n), _Offset(recv_sems, a.n)))

    aliases = dict(a.aliases)
    aliases.update({ka_in + i: ka_out + o for i, o in b.aliases.items()})
    return _Exchange(a.operands + b.operands, a.out_shapes + b.out_shapes, aliases, a.n + b.n, copies)


def _run_exchange(name, exchange):
    k_in, k_out = len(exchange.operands), len(exchange.out_shapes)

    def body(*refs):
        copies = exchange.copies(refs[:k_in], refs[k_in:k_in + k_out], refs[-2], refs[-1])
        for cp in copies:
            cp.start()
        for cp in copies:
            cp.wait()

    return list(pl.pallas_call(
        body, name=name, in_specs=[ANY] * k_in, out_specs=[ANY] * k_out, out_shape=exchange.out_shapes,
        scratch_shapes=[pltpu.SemaphoreType.DMA((exchange.n,)), pltpu.SemaphoreType.DMA((exchange.n,))],
        input_output_aliases=dict(exchange.aliases),
    )(*exchange.operands))


def _tile_rows(shape):
    return (shape[0] // EXCH_TILES, shape[1])


def _cast_place(ids, shards):
    def body(ids_ref, *refs):
        for j in range(len(BIG)):
            refs[len(BIG) + j][...] = refs[j][...].astype(BF16)

    def out_spec(w):
        _, r, cc, axis = BIG[w]
        tile = _tile_rows(_shard_shape(r, cc, axis))
        if axis == 1:
            return pl.BlockSpec(tile, lambda i, ids: (i, ids[1]))
        return pl.BlockSpec(tile, lambda i, ids: (ids[1] * EXCH_TILES + i, 0))

    return list(pl.pallas_call(
        body, name="cast_place",
        grid_spec=pltpu.PrefetchScalarGridSpec(
            num_scalar_prefetch=1, grid=(EXCH_TILES,),
            in_specs=[pl.BlockSpec(_tile_rows(_shard_shape(r, cc, axis)), lambda i, ids: (i, 0)) for _, r, cc, axis in BIG],
            out_specs=[out_spec(w) for w in range(len(BIG))]),
        out_shape=[jax.ShapeDtypeStruct((r, cc), BF16) for _, r, cc, _ in BIG],
        compiler_params=pltpu.CompilerParams(dimension_semantics=("parallel",)),
    )(ids, *shards))


def _pair_sum(ids, ws, grads, got, small=None):
    nb = len(ws)
    n = nb + (small is not None)
    halves = [_half_shape(*BIG[w][1:]) for w in ws] + [SMALL_HALF] * (n - nb)

    def body(ids_ref, *refs):
        for j in range(n):
            total = refs[j][...] + refs[n + j][...]
            refs[2 * n + j][...] = total
            if j < nb:
                refs[3 * n + j][...] = total.astype(BF16)

    def mine_spec(j):
        tile = _tile_rows(halves[j])
        if j < nb and BIG[ws[j]][3] == 0:
            return pl.BlockSpec(tile, lambda i, ids: (i, ids[0]))
        return pl.BlockSpec(tile, lambda i, ids: (ids[0] * EXCH_TILES + i, 0))

    plain = lambda j: pl.BlockSpec(_tile_rows(halves[j]), lambda i, ids: (i, 0))
    outs = pl.pallas_call(
        body, name="pair_sum",
        grid_spec=pltpu.PrefetchScalarGridSpec(
            num_scalar_prefetch=1, grid=(EXCH_TILES,),
            in_specs=[mine_spec(j) for j in range(n)] + [plain(j) for j in range(n)],
            out_specs=[plain(j) for j in range(n)] + [plain(j) for j in range(nb)]),
        out_shape=[jax.ShapeDtypeStruct(h, F32) for h in halves] + [jax.ShapeDtypeStruct(h, BF16) for h in halves[:nb]],
        compiler_params=pltpu.CompilerParams(dimension_semantics=("parallel",)),
    )(ids, *grads, *([] if small is None else [small]), *got)
    return list(outs[:n]), list(outs[n:])


def _chip_sum(ids, ws, sums, arrived):
    nb, n = len(ws), len(sums)
    regions = [_region_shape(*BIG[w][1:]) for w in ws] + [SMALL_HALF] * (n - nb)

    def body(ids_ref, *refs):
        for j in range(n):
            own, got, out = refs[j], refs[n + j], refs[2 * n + j]
            if j < nb:
                out[...] = ((own[...] + got[0].astype(F32)) + got[1].astype(F32)) + got[2].astype(F32)
            else:
                out[...] = (own[...] + got[1]) + (got[0] + got[2])

    def own_spec(j):
        tile = _tile_rows(regions[j])
        if j >= nb:
            return pl.BlockSpec(tile, lambda i, ids: (i, 0))
        if BIG[ws[j]][3] == 1:
            return pl.BlockSpec(tile, lambda i, ids: (i, ids[1]))
        return pl.BlockSpec(tile, lambda i, ids: (ids[1] * EXCH_TILES + i, 0))

    def out_spec(j):
        tile = _tile_rows(regions[j])
        if j < nb and BIG[ws[j]][3] == 0:
            return pl.BlockSpec(tile, lambda i, ids: (i, ids[0]))
        return pl.BlockSpec(tile, lambda i, ids: (ids[0] * EXCH_TILES + i, 0))

    got_spec = lambda j: pl.BlockSpec((3,) + _tile_rows(regions[j]), lambda i, ids: (0, i, 0))
    return list(pl.pallas_call(
        body, name="chip_sum",
        grid_spec=pltpu.PrefetchScalarGridSpec(
            num_scalar_prefetch=1, grid=(EXCH_TILES,),
            in_specs=[own_spec(j) for j in range(n)] + [got_spec(j) for j in range(n)],
            out_specs=[out_spec(j) for j in range(n)]),
        out_shape=[jax.ShapeDtypeStruct(_shard_shape(*BIG[w][1:]), F32) for w in ws]
        + [jax.ShapeDtypeStruct((SMALL_ROWS, LANES), F32)] * (n - nb),
        compiler_params=pltpu.CompilerParams(dimension_semantics=("parallel",)),
    )(ids, *sums, *arrived))


def _adamw(name, ws, gs, ms, vs, exchange=None):
    n = len(ws)
    c1 = 1.0 - ADAM_B1 ** ADAM_STEP
    c2 = 1.0 - ADAM_B2 ** ADAM_STEP

    def body(*refs):
        for j in range(n):
            w, g, m, v = (refs[k * n + j][...] for k in range(4))
            m = ADAM_B1 * m + (1.0 - ADAM_B1) * g
            v = ADAM_B2 * v + (1.0 - ADAM_B2) * (g * g)
            refs[4 * n + j][...] = -ADAM_LR * ((m / c1) / (jnp.sqrt(v / c2) + ADAM_EPS) + ADAM_WD * w)
            refs[5 * n + j][...] = m
            refs[6 * n + j][...] = v

    specs = [pl.BlockSpec(_tile_rows(w.shape), lambda i: (i, 0)) for w in ws]
    shapes = [jax.ShapeDtypeStruct(w.shape, F32) for w in ws]
    out, got = _call(body, name, (EXCH_TILES,), specs * 4, specs * 3, shapes * 3, [], (*ws, *gs, *ms, *vs), exchange)
    return (out[:n], out[n:2 * n], out[2 * n:]), got


def kernel(x, norm_mix_pre, norm_mix_post, w_in, ret_gn_gain, ssm_lambda_re, ssm_lambda_im, ssm_log_dt, ssm_b_re, ssm_b_im, ssm_c_re, ssm_c_im, ssm_d, w_glu, w_out, norm_mlp_pre, norm_mlp_post, w_ff1, w_ff2, loss_target, m_norm_mix_pre, m_norm_mix_post, m_w_in, m_ret_gn_gain, m_ssm_lambda_re, m_ssm_lambda_im, m_ssm_log_dt, m_ssm_b_re, m_ssm_b_im, m_ssm_c_re, m_ssm_c_im, m_ssm_d, m_w_glu, m_w_out, m_norm_mlp_pre, m_norm_mlp_post, m_w_ff1, m_w_ff2, v_norm_mix_pre, v_norm_mix_post, v_w_in, v_ret_gn_gain, v_ssm_lambda_re, v_ssm_lambda_im, v_ssm_log_dt, v_ssm_b_re, v_ssm_b_im, v_ssm_c_re, v_ssm_c_im, v_ssm_d, v_w_glu, v_w_out, v_norm_mlp_pre, v_norm_mlp_post, v_w_ff1, v_w_ff2):
    given = dict(locals())
    order = ["norm_mix_pre", "norm_mix_post", "w_in", "ret_gn_gain", "ssm_lambda_re", "ssm_lambda_im", "ssm_log_dt",
             "ssm_b_re", "ssm_b_im", "ssm_c_re", "ssm_c_im", "ssm_d", "w_glu", "w_out", "norm_mlp_pre", "norm_mlp_post",
             "w_ff1", "w_ff2"]
    big_names = [name for name, _, _, _ in BIG]
    ids = jnp.stack([lax.axis_index("c"), 2 * lax.axis_index("x") + lax.axis_index("y")]).astype(jnp.int32)

    weights = _cast_place(ids, [given[name][0] for name in big_names])
    w_in = _run_exchange("gather_w_in_send", _x_gather_send((0,), weights[:1]))
    w_in = _run_exchange("gather_w_in_pass", _x_gather_pass((0,), w_in))
    small_w = {name: given[name][0] for name, _ in SMALL}
    dx, d_in, mid_parts, late_grads, small_grads = _local_step(x[0], loss_target[0], small_w, w_in + weights[1:], ids)

    small_packed = _pack_small(small_grads)
    got = _run_exchange("pair_exchange", _x_join(_x_share(MID, mid_parts), _x_pair(FIRST, [d_in], small_packed)))
    mid_grads, got = got[:len(MID)], got[len(MID):]
    sums, sums_bf16 = _pair_sum(ids, FIRST, [d_in], got, small_packed)
    arrived = _run_exchange("chip_exchange", _x_chip(FIRST, sums_bf16, sums[-1]))
    parts = _chip_sum(ids, FIRST, sums, arrived)
    g_in, g_small = _run_exchange("pair_share", _x_share(FIRST, parts[:-1], parts[-1]))
    nw = len(BIG)
    reduced = [g_in] + mid_grads + late_grads + [g_small]
    pack = lambda prefix: _pack_small({name: given[prefix + name][0] for name, _ in SMALL})
    state = lambda prefix: [given[prefix + name][0] for name in big_names] + [pack(prefix)]
    (deltas, new_m, new_v), _ = _adamw("adamw", state(""), reduced, state("m_"), state("v_"))

    def unpacked(arrays):
        tree = _unpack_small(arrays[nw])
        tree.update({name: arrays[j][None] for j, name in enumerate(big_names)})
        return [tree[name] for name in order]

    loss = _unpack_small(reduced[nw])["loss_sum"].reshape(())
    return (loss, dx[None], *unpacked(list(reduced)), *unpacked(deltas), *unpacked(new_m), *unpacked(new_v))
```

```python
import functools
import math

import jax
import jax.numpy as jnp
import numpy as np
from jax import lax
from jax.experimental import pallas as pl
from jax.experimental.pallas import tpu as pltpu

F32 = jnp.float32
BF16 = jnp.bfloat16

D_MODEL = 1024
RET_WIDTH = 512
RET_HEADS = 4
HEAD_DIM = 128
RET_CHUNK = 128
ROPE_BASE = 10000.0
SSM_WIDTH = 512
SSM_GROUP = 16
SSM_GROUPS = 32
SSM_STATE = 64
N_STATES = SSM_GROUPS * SSM_STATE
D_FF = 4096
IN_COLS = 4 * RET_WIDTH + SSM_WIDTH
NORM_EPS = 1e-6
K_SCALE = HEAD_DIM ** -0.5

ADAM_LR = 0.001
ADAM_B1 = 0.9
ADAM_B2 = 0.999
ADAM_EPS = 1e-08
ADAM_WD = 0.01
ADAM_STEP = 10

LANES = 128
SUBLANES = 8
VMEM_LIMIT = 56 * 2 ** 20

TOK_TILE = 512
MLP_TILE = 256
RET_CHUNKS_PER_STEP = 4
S5_TILE = 256
S5_STEPS = S5_TILE // SUBLANES
S5_COLBLK = N_STATES // LANES
S5_GROUP = 4

N_CHIPS = 4
MESH = pl.DeviceIdType.MESH


def _dot(a, b):
    return jnp.dot(a, b, preferred_element_type=F32)


def _dot_nt(a, b):
    return lax.dot_general(a, b, (((1,), (1,)), ((), ())), preferred_element_type=F32)


def _dot_tn(a, b):
    return lax.dot_general(a, b, (((0,), (0,)), ((), ())), preferred_element_type=F32)


def _sigmoid(x):
    return 1.0 / (1.0 + jnp.exp(-x))


_GELU_C = math.sqrt(2.0 / math.pi)


def _gelu(x):
    return 0.5 * x * (1.0 + jnp.tanh(_GELU_C * (x + 0.044715 * (x * x * x))))


def _gelu_grad(x):
    t = jnp.tanh(_GELU_C * (x + 0.044715 * (x * x * x)))
    return 0.5 * (1.0 + t) + 0.5 * x * (1.0 - t * t) * (_GELU_C * (1.0 + 3.0 * 0.044715 * (x * x)))


def _rms(x):
    r = lax.rsqrt(jnp.mean(x * x, axis=-1, keepdims=True) + NORM_EPS)
    return x * r, r


def _rms_bwd(n, r, gain, dy):
    dn = dy * gain
    dx = r * (dn - n * jnp.mean(dn * n, axis=-1, keepdims=True))
    return dx, jnp.sum(dy * n, axis=0, keepdims=True)


def _full(shape):
    nd = len(shape)
    return pl.BlockSpec(shape, lambda i, _n=nd: (0,) * _n, pipeline_mode=pl.Buffered(1))


def _rows(tile, width):
    return pl.BlockSpec((tile, width), lambda i: (i, 0))


def _rows_rev(tile, width, n):
    return pl.BlockSpec((tile, width), lambda i, _n=n: (_n - 1 - i, 0))


def _params(sem):
    return pltpu.CompilerParams(dimension_semantics=(sem,), vmem_limit_bytes=VMEM_LIMIT)


ANY = pl.BlockSpec(memory_space=pl.ANY)


class _Exchange:
    def __init__(self, operands, out_shapes, aliases, n, copies):
        self.operands, self.out_shapes, self.aliases, self.n, self.copies = list(operands), list(out_shapes), aliases, n, copies


def _call(body, name, grid, in_specs, out_specs, out_shape, scratch_shapes, args, exchange=None):
    if exchange is None:
        outs = pl.pallas_call(body, name=name, grid=grid, in_specs=in_specs, out_specs=out_specs, out_shape=out_shape,
                              scratch_shapes=scratch_shapes, compiler_params=_params("arbitrary"))(*args)
        return list(outs), []
    n_in, n_out, n_scr = len(in_specs), len(out_specs), len(scratch_shapes)
    k_in, k_out = len(exchange.operands), len(exchange.out_shapes)
    last = grid[0] - 1

    def hosted(*refs):
        own_in, rest = refs[:n_in], refs[n_in:]
        ex_in, rest = rest[:k_in], rest[k_in:]
        own_out, rest = rest[:n_out], rest[n_out:]
        ex_out, rest = rest[:k_out], rest[k_out:]
        own_scr, (send_sems, recv_sems) = rest[:n_scr], rest[n_scr:]

        @pl.when(pl.program_id(0) == 0)
        def _():
            for cp in exchange.copies(ex_in, ex_out, send_sems, recv_sems):
                cp.start()

        body(*own_in, *own_out, *own_scr)

        @pl.when(pl.program_id(0) == last)
        def _():
            for cp in exchange.copies(ex_in, ex_out, send_sems, recv_sems):
                cp.wait()

    outs = pl.pallas_call(
        hosted, name=name, grid=grid, in_specs=list(in_specs) + [ANY] * k_in, out_specs=list(out_specs) + [ANY] * k_out,
        out_shape=list(out_shape) + exchange.out_shapes,
        scratch_shapes=list(scratch_shapes) + [pltpu.SemaphoreType.DMA((exchange.n,)), pltpu.SemaphoreType.DMA((exchange.n,))],
        input_output_aliases={n_in + i: n_out + o for i, o in exchange.aliases.items()},
        compiler_params=_params("arbitrary"),
    )(*args, *exchange.operands)
    return list(outs[:n_out]), list(outs[n_out:])


def _rope(t, cos2, sin2):
    return t * cos2 + pltpu.roll(t, HEAD_DIM // 2, 1) * sin2


def _rope_bwd(d, cos2, sin2):
    return d * cos2 + pltpu.roll(d * sin2, HEAD_DIM // 2, d.ndim - 1)


def _inproj_fwd(x, gain, w_in, cos2, sin2, exchange=None):
    seq = x.shape[0]

    def body(x_ref, g_ref, w_ref, c_ref, s_ref, q_ref, k_ref, v_ref, gate_ref, u_ref):
        n, _ = _rms(x_ref[...])
        h = (n * g_ref[...]).astype(BF16)
        proj = _dot(h, w_ref[...])
        c = c_ref[...]
        s = s_ref[...]
        for hd in range(RET_HEADS):
            lo = hd * HEAD_DIM
            q_ref[:, lo:lo + HEAD_DIM] = _rope(proj[:, lo:lo + HEAD_DIM], c, s).astype(BF16)
            kh = proj[:, RET_WIDTH + lo:RET_WIDTH + lo + HEAD_DIM]
            k_ref[:, lo:lo + HEAD_DIM] = (_rope(kh, c, s) * K_SCALE).astype(BF16)
        v_ref[...] = proj[:, 2 * RET_WIDTH:3 * RET_WIDTH].astype(BF16)
        gate_ref[...] = proj[:, 3 * RET_WIDTH:4 * RET_WIDTH]
        u_ref[...] = proj[:, 4 * RET_WIDTH:]

    t = TOK_TILE
    half = lambda dt: jax.ShapeDtypeStruct((seq, RET_WIDTH), dt)
    return _call(
        body, "inproj_fwd", (seq // t,),
        [_rows(t, D_MODEL), _full((1, D_MODEL)), _full((D_MODEL, IN_COLS)), _rows(t, HEAD_DIM), _rows(t, HEAD_DIM)],
        [_rows(t, RET_WIDTH)] * 5, [half(BF16), half(BF16), half(BF16), half(F32), half(F32)], [],
        (x, gain, w_in, cos2, sin2), exchange)


def _inproj_bwd(x, gain, w_in, dpieces, dres, exchange=None):
    seq = x.shape[0]
    t = TOK_TILE
    n_pieces = len(dpieces)

    def body(x_ref, g_ref, w_ref, *refs):
        piece_refs, (dres_ref, dx_ref, dw_ref, dg_ref) = refs[:n_pieces], refs[n_pieces:]

        @pl.when(pl.program_id(0) == 0)
        def _():
            dw_ref[...] = jnp.zeros_like(dw_ref)
            dg_ref[...] = jnp.zeros_like(dg_ref)

        n, r = _rms(x_ref[...])
        gain_v = g_ref[...]
        h = (n * gain_v).astype(BF16)
        dh = None
        for j, piece in enumerate(piece_refs):
            cols = slice(j * RET_WIDTH, (j + 1) * RET_WIDTH)
            dp = piece[...]
            dw_ref[:, cols] += _dot_tn(h, dp)
            term = _dot_nt(dp, w_ref[:, cols])
            dh = term if dh is None else dh + term
        dx, dgs = _rms_bwd(n, r, gain_v, dh)
        dg_ref[...] += dgs
        dx_ref[...] = dres_ref[...] + dx

    return _call(
        body, "inproj_bwd", (seq // t,),
        [_rows(t, D_MODEL), _full((1, D_MODEL)), _full((D_MODEL, IN_COLS))] + [_rows(t, RET_WIDTH)] * n_pieces + [_rows(t, D_MODEL)],
        [_rows(t, D_MODEL), _full((D_MODEL, IN_COLS)), _full((1, D_MODEL))],
        [jax.ShapeDtypeStruct((seq, D_MODEL), F32), jax.ShapeDtypeStruct((D_MODEL, IN_COLS), F32),
         jax.ShapeDtypeStruct((1, D_MODEL), F32)],
        [], (x, gain, w_in, *dpieces, dres), exchange)


def _ret_consts():
    c = RET_CHUNK
    f32 = np.float32
    log_gamma = np.log(f32(1.0) - np.exp(np.linspace(math.log(1.0 / 32), math.log(1.0 / 512), RET_HEADS, dtype=f32))).astype(f32)
    idx = np.arange(c, dtype=f32)
    diff = idx[:, None] - idx[None, :]
    decay = np.where(diff[None] >= 0, np.exp(np.maximum(diff, f32(0.0))[None] * log_gamma[:, None, None]), f32(0.0))
    zeta = np.exp((c - 1 - idx)[None, :] * log_gamma[:, None])
    xi = np.exp((idx + f32(1.0))[None, :] * log_gamma[:, None])
    g_chunk = np.exp(f32(c) * log_gamma)
    wide = lambda rowvals: jnp.asarray(np.broadcast_to(rowvals[:, :, None], (RET_HEADS, c, c)).astype(f32))
    return (jnp.asarray(decay.astype(f32)), wide(xi), wide(zeta),
            jnp.asarray(np.broadcast_to(g_chunk[:, None, None], (RET_HEADS, c, c)).astype(f32)))


def _rope_tables(seq):
    f32 = np.float32
    half = HEAD_DIM // 2
    inv_freq = np.power(f32(ROPE_BASE), -np.arange(half, dtype=f32) / f32(half)).astype(f32)
    ang = (np.arange(seq, dtype=f32)[:, None] * inv_freq[None, :]).astype(f32)
    cos, sin = np.cos(ang).astype(f32), np.sin(ang).astype(f32)
    return jnp.asarray(np.concatenate([cos, cos], axis=1)), jnp.asarray(np.concatenate([-sin, sin], axis=1))


def _bmm(a, b):
    return lax.dot_general(a, b, (((2,), (1,)), ((0,), (0,))), preferred_element_type=F32)


def _bmm_nt(a, b):
    return lax.dot_general(a, b, (((2,), (2,)), ((0,), (0,))), preferred_element_type=F32)


def _bmm_tn(a, b):
    return lax.dot_general(a, b, (((1,), (1,)), ((0,), (0,))), preferred_element_type=F32)


def _ret_blocks(ref):
    return jnp.stack([ref[cc * RET_CHUNK:(cc + 1) * RET_CHUNK, hd * HEAD_DIM:(hd + 1) * HEAD_DIM]
                      for cc in range(RET_CHUNKS_PER_STEP) for hd in range(RET_HEADS)])


def _ret_unblock(ref, blocks):
    for cc in range(RET_CHUNKS_PER_STEP):
        for hd in range(RET_HEADS):
            ref[cc * RET_CHUNK:(cc + 1) * RET_CHUNK, hd * HEAD_DIM:(hd + 1) * HEAD_DIM] = blocks[cc * RET_HEADS + hd].astype(ref.dtype)


def _rope_blocks(ref):
    return jnp.stack([ref[cc * RET_CHUNK:(cc + 1) * RET_CHUNK, :] for cc in range(RET_CHUNKS_PER_STEP) for _ in range(RET_HEADS)])


def _per_chunk(ref):
    return jnp.concatenate([ref[...]] * RET_CHUNKS_PER_STEP, axis=0)


def _ret_gain(gain_ref):
    return jnp.stack([gain_ref[:, hd * HEAD_DIM:(hd + 1) * HEAD_DIM]
                      for _ in range(RET_CHUNKS_PER_STEP) for hd in range(RET_HEADS)])


def _ret_fwd(q, k, v, gate, gn_gain, consts, exchange=None):
    seq = q.shape[0]
    cps = RET_CHUNKS_PER_STEP
    t = cps * RET_CHUNK
    n_chunks = seq // RET_CHUNK
    dec, xi, zeta, gch = consts

    def body(q_ref, k_ref, v_ref, gate_ref, gain_ref, dec_ref, xi_ref, zeta_ref, gch_ref, y_ref, rprev_ref, state):
        @pl.when(pl.program_id(0) == 0)
        def _():
            state[...] = jnp.zeros_like(state)

        qb, kb, vb = _ret_blocks(q_ref), _ret_blocks(k_ref), _ret_blocks(v_ref)
        s = _bmm_nt(qb, kb) * _per_chunk(dec_ref)
        vz = (vb.astype(F32) * _per_chunk(zeta_ref)).astype(BF16)
        incr = _bmm_tn(kb, vz)
        gch_v = gch_ref[...]
        rp = state[...]
        before = []
        for cc in range(cps):
            before.append(rp.astype(BF16))
            rprev_ref[cc] = before[-1]
            rp = gch_v * rp + incr[cc * RET_HEADS:(cc + 1) * RET_HEADS]
        state[...] = rp
        qx = (qb.astype(F32) * _per_chunk(xi_ref)).astype(BF16)
        o = _bmm(s.astype(BF16), vb) + _bmm(qx, jnp.concatenate(before, axis=0))
        oc = o - jnp.mean(o, axis=-1, keepdims=True)
        on = oc * lax.rsqrt(jnp.mean(oc * oc, axis=-1, keepdims=True) + NORM_EPS)
        g = _ret_blocks(gate_ref)
        _ret_unblock(y_ref, g * _sigmoid(g) * (on * _ret_gain(gain_ref)))

    cst = _full((RET_HEADS, RET_CHUNK, RET_CHUNK))
    return _call(
        body, "ret_fwd", (seq // t,),
        [_rows(t, RET_WIDTH)] * 4 + [_full((1, RET_WIDTH)), cst, cst, cst, cst],
        [_rows(t, RET_WIDTH), pl.BlockSpec((cps, RET_HEADS, HEAD_DIM, HEAD_DIM), lambda i: (i, 0, 0, 0))],
        [jax.ShapeDtypeStruct((seq, RET_WIDTH), BF16), jax.ShapeDtypeStruct((n_chunks, RET_HEADS, HEAD_DIM, HEAD_DIM), BF16)],
        [pltpu.VMEM((RET_HEADS, HEAD_DIM, HEAD_DIM), F32)],
        (q, k, v, gate, gn_gain, dec, xi, zeta, gch), exchange)


def _ret_bwd(q, k, v, gate, gn_gain, consts, rprev, dy_ret, cos2, sin2, exchange=None):
    seq = q.shape[0]
    cps = RET_CHUNKS_PER_STEP
    t = cps * RET_CHUNK
    nt = seq // t
    dec, xi, zeta, gch = consts

    def body(q_ref, k_ref, v_ref, gate_ref, gain_ref, dec_ref, xi_ref, zeta_ref, gch_ref, rprev_ref, dyr_ref, cos_ref, sin_ref,
             dq_ref, dk_ref, dv_ref, dgate_ref, dgain_ref, dstate):
        @pl.when(pl.program_id(0) == 0)
        def _():
            dstate[...] = jnp.zeros_like(dstate)
            dgain_ref[...] = jnp.zeros_like(dgain_ref)

        nb = cps * RET_HEADS
        qb, kb, vb = _ret_blocks(q_ref), _ret_blocks(k_ref), _ret_blocks(v_ref)
        dec_b, xi_b, zeta_b = _per_chunk(dec_ref), _per_chunk(xi_ref), _per_chunk(zeta_ref)
        rpb = rprev_ref[...].reshape(nb, HEAD_DIM, HEAD_DIM)
        sdb = (_bmm_nt(qb, kb) * dec_b).astype(BF16)
        qx = (qb.astype(F32) * xi_b).astype(BF16)
        o = _bmm(sdb, vb) + _bmm(qx, rpb)
        oc = o - jnp.mean(o, axis=-1, keepdims=True)
        rstd = lax.rsqrt(jnp.mean(oc * oc, axis=-1, keepdims=True) + NORM_EPS)
        on = oc * rstd
        g = _ret_blocks(gate_ref)
        sg = _sigmoid(g)
        gain_b = _ret_gain(gain_ref)
        dyr = _ret_blocks(dyr_ref)
        _ret_unblock(dgate_ref, dyr * (on * gain_b) * (sg * (1.0 + g * (1.0 - sg))))
        dy = dyr * (g * sg)
        dgain = jnp.sum(dy * on, axis=1, keepdims=True)
        for hd in range(RET_HEADS):
            part = dgain[hd]
            for cc in range(1, cps):
                part = part + dgain[cc * RET_HEADS + hd]
            dgain_ref[:, hd * HEAD_DIM:(hd + 1) * HEAD_DIM] += part
        don = dy * gain_b
        do = rstd * (don - jnp.mean(don, axis=-1, keepdims=True) - on * jnp.mean(don * on, axis=-1, keepdims=True))
        dob = do.astype(BF16)
        ds = (_bmm_nt(dob, vb) * dec_b).astype(BF16)
        dq = _bmm(ds, kb) + _bmm_nt(dob, rpb) * xi_b
        dk = _bmm_tn(ds, qb)
        dv = _bmm_tn(sdb, dob)
        dstate_local = _bmm_tn(qx, dob)
        vz = (vb.astype(F32) * zeta_b).astype(BF16)
        gch_v = gch_ref[...]
        zeta_v = zeta_ref[...]
        gh = dstate[...]
        dk_extra, dv_extra = [None] * cps, [None] * cps
        for cc in reversed(range(cps)):
            sl = slice(cc * RET_HEADS, (cc + 1) * RET_HEADS)
            gb = gh.astype(BF16)
            dk_extra[cc] = _bmm_nt(vz[sl], gb)
            dv_extra[cc] = _bmm(kb[sl], gb) * zeta_v
            gh = dstate_local[sl] + gch_v * gh
        dstate[...] = gh
        cos_b, sin_b = _rope_blocks(cos_ref), _rope_blocks(sin_ref)
        _ret_unblock(dq_ref, _rope_bwd(dq, cos_b, sin_b))
        _ret_unblock(dk_ref, _rope_bwd((dk + jnp.concatenate(dk_extra, axis=0)) * K_SCALE, cos_b, sin_b))
        _ret_unblock(dv_ref, dv + jnp.concatenate(dv_extra, axis=0))

    cst = _full((RET_HEADS, RET_CHUNK, RET_CHUNK))
    rev = _rows_rev(t, RET_WIDTH, nt)
    act = jax.ShapeDtypeStruct((seq, RET_WIDTH), BF16)
    return _call(
        body, "ret_bwd", (nt,),
        [rev] * 4 + [_full((1, RET_WIDTH)), cst, cst, cst, cst,
                     pl.BlockSpec((cps, RET_HEADS, HEAD_DIM, HEAD_DIM), lambda i: (nt - 1 - i, 0, 0, 0)), rev,
                     _rows_rev(t, HEAD_DIM, nt), _rows_rev(t, HEAD_DIM, nt)],
        [rev] * 4 + [_full((1, RET_WIDTH))], [act, act, act, act, jax.ShapeDtypeStruct((1, RET_WIDTH), F32)],
        [pltpu.VMEM((RET_HEADS, HEAD_DIM, HEAD_DIM), F32)],
        (q, k, v, gate, gn_gain, dec, xi, zeta, gch, rprev, dy_ret, cos2, sin2), exchange)


MXU_TILE = 256
S5_STATE_TILES = 2 * N_STATES // MXU_TILE
S5_CHANNELS_PER_TILE = MXU_TILE // SSM_STATE * SSM_GROUP
S5_TILES_SHAPE = (S5_STATE_TILES, LANES, MXU_TILE)


def _chan_block(kk):
    return ((kk % (S5_STATE_TILES // 2)) * S5_CHANNELS_PER_TILE) // LANES * LANES


def _s5_tiles(blocks_re, blocks_im):
    per = MXU_TILE // SSM_STATE
    half = S5_STATE_TILES // 2
    eye = jnp.eye(per, dtype=F32)
    in_half = (jnp.arange(half) % (LANES // S5_CHANNELS_PER_TILE))[:, None] == jnp.arange(LANES // S5_CHANNELS_PER_TILE)[None, :]
    out = []
    for blk in (blocks_re, blocks_im):
        t = blk.reshape(half, per, SSM_GROUP, 1, SSM_STATE) * eye[None, :, None, :, None]
        t = t.reshape(half, 1, S5_CHANNELS_PER_TILE, MXU_TILE)
        out.append(jnp.where(in_half[:, :, None, None], t, 0.0).reshape(half, LANES, MXU_TILE))
    return jnp.concatenate(out, axis=0)


def _s5_discretise(lam_re, lam_im, log_dt, b_re, b_im, c_re, c_im):
    lam = lax.complex(jnp.minimum(lam_re, -1e-4), lam_im)
    dt = jnp.exp(log_dt)[:, None]
    lam_bar = jnp.exp(lam * dt)
    b_bar = ((lam_bar - 1.0) / lam)[:, :, None] * lax.complex(b_re, b_im)
    b_tiles = _s5_tiles(jnp.swapaxes(jnp.real(b_bar), 1, 2), jnp.swapaxes(jnp.imag(b_bar), 1, 2))
    return jnp.real(lam_bar), jnp.imag(lam_bar), b_tiles, _s5_tiles(c_re, -c_im)


def _s5_tables(lbr, lbi):
    lr = lbr.reshape(1, N_STATES)
    li = lbi.reshape(1, N_STATES)
    pr, pi = lr, li
    while pr.shape[0] < S5_STEPS:
        top_r, top_i = pr[-1:], pi[-1:]
        pr, pi = (jnp.concatenate([pr, pr * top_r - pi * top_i], axis=0), jnp.concatenate([pi, pr * top_i + pi * top_r], axis=0))
    bc = lambda a: jnp.broadcast_to(a, (SUBLANES, N_STATES))
    rep = lambda rows: jnp.broadcast_to(rows[:, None, :], (S5_STEPS, SUBLANES, N_STATES)).reshape(S5_STEPS * SUBLANES, N_STATES)
    return bc(lr), bc(li), rep(pr), rep(pi)


def _cmul(ar, ai, br, bi):
    return ar * br - ai * bi, ar * bi + ai * br


def _seg(i):
    return pl.ds(pl.multiple_of(i * SUBLANES, SUBLANES), SUBLANES)


def _scan_order():
    r = np.arange(S5_TILE)
    token = (r % SUBLANES) * S5_STEPS + r // SUBLANES
    p = (token[:, None] == np.arange(S5_TILE)[None, :]).astype(np.float32)
    return jnp.asarray(p, dtype=BF16), jnp.asarray(p.T, dtype=BF16)


def _to_scan_order(p_ref, rows_bf16):
    return _dot(p_ref[...], rows_bf16).astype(BF16)


def _from_scan_order(pt_ref, rows):
    hi = rows.astype(BF16)
    rest = rows - hi.astype(F32)
    mid = rest.astype(BF16)
    lo = (rest - mid.astype(F32)).astype(BF16)
    pt = pt_ref[...]
    return _dot(pt, hi) + _dot(pt, mid) + _dot(pt, lo)


def _s5_channels_to_states(chan_b, m_ref, xs):
    for kk in range(S5_STATE_TILES):
        lo = _chan_block(kk)
        res = _dot(chan_b[:, lo:lo + LANES], m_ref[kk])
        xs[2 * kk] = res[:, :LANES]
        xs[2 * kk + 1] = res[:, LANES:]


def _s5_states_to_channels(sb_ref, m_ref):
    out = []
    for lo in range(0, SSM_WIDTH, LANES):
        acc = None
        for kk in range(S5_STATE_TILES):
            if _chan_block(kk) == lo:
                term = _dot_nt(sb_ref[:, kk * MXU_TILE:(kk + 1) * MXU_TILE], m_ref[kk])
                acc = term if acc is None else acc + term
        out.append(acc)
    return jnp.concatenate(out, axis=1)


def _s5_outer_acc(acc_ref, chan_b, states_ref):
    for kk in range(S5_STATE_TILES):
        lo = _chan_block(kk)
        acc_ref[kk] += _dot_tn(chan_b[:, lo:lo + LANES], states_ref[:, kk * MXU_TILE:(kk + 1) * MXU_TILE])


def _s5_scan_tile(xs, lr_ref, li_ref, pr_ref, pi_ref, carry_re, carry_im, conj, reverse, prev_of=None):
    sgn = -1.0 if conj else 1.0
    seg_in_re, seg_in_im, sums_re, sums_im = [], [], [], []
    group = S5_GROUP
    for grp in range(S5_COLBLK // group):
        blks = [grp * group + j for j in range(group)]
        lrs = [lr_ref[:, b * LANES:(b + 1) * LANES] for b in blks]
        lis = [sgn * li_ref[:, b * LANES:(b + 1) * LANES] for b in blks]

        def step(it, carry, blks=blks, lrs=lrs, lis=lis):
            i = (S5_STEPS - 1 - it) if reverse else it
            out_r, out_i = [], []
            for j, b in enumerate(blks):
                xr, xi = _cmul(lrs[j], lis[j], carry[j], carry[group + j])
                xr = xr + xs[b, _seg(i), :]
                xi = xi + xs[S5_COLBLK + b, _seg(i), :]
                xs[b, _seg(i), :] = xr
                xs[S5_COLBLK + b, _seg(i), :] = xi
                out_r.append(xr)
                out_i.append(xi)
            return tuple(out_r + out_i)

        zeros = tuple(jnp.zeros((SUBLANES, LANES), F32) for _ in range(2 * group))
        ends = lax.fori_loop(0, S5_STEPS, step, zeros, unroll=True)

        ins_r, ins_i = [], []
        for j, b in enumerate(blks):
            cols = slice(b * LANES, (b + 1) * LANES)
            pr = pr_ref[SUBLANES * S5_STEPS - 1:SUBLANES * S5_STEPS, cols]
            pi = sgn * pi_ref[SUBLANES * S5_STEPS - 1:SUBLANES * S5_STEPS, cols]
            cur_r = carry_re[:, cols]
            cur_i = carry_im[:, cols]
            row_id = lax.broadcasted_iota(jnp.int32, (SUBLANES, LANES), 0)
            in_r = jnp.zeros((SUBLANES, LANES), F32)
            in_i = jnp.zeros((SUBLANES, LANES), F32)
            order = range(SUBLANES - 1, -1, -1) if reverse else range(SUBLANES)
            for sgm in order:
                in_r = jnp.where(row_id == sgm, cur_r, in_r)
                in_i = jnp.where(row_id == sgm, cur_i, in_i)
                mr, mi = _cmul(pr, pi, cur_r, cur_i)
                cur_r = mr + ends[j][sgm:sgm + 1, :]
                cur_i = mi + ends[group + j][sgm:sgm + 1, :]
            carry_re[:, cols] = cur_r
            carry_im[:, cols] = cur_i
            ins_r.append(in_r)
            ins_i.append(in_i)

        def fix(it, c, blks=blks, ins_r=ins_r, ins_i=ins_i):
            pw = (S5_STEPS - 1 - it) if reverse else it
            out = []
            for j, b in enumerate(blks):
                cols = slice(b * LANES, (b + 1) * LANES)
                prow = pl.ds(pl.multiple_of(pw * SUBLANES, SUBLANES), SUBLANES)
                fr, fi = _cmul(pr_ref[prow, cols], sgn * pi_ref[prow, cols], ins_r[j], ins_i[j])
                vr = xs[b, _seg(it), :] + fr
                vi = xs[S5_COLBLK + b, _seg(it), :] + fi
                xs[b, _seg(it), :] = vr
                xs[S5_COLBLK + b, _seg(it), :] = vi
                if prev_of is not None:
                    opr, opi, sr, si = c[4 * j:4 * j + 4]
                    out += [prev_of[0][b, _seg(it), :], prev_of[0][S5_COLBLK + b, _seg(it), :],
                            sr + vr * opr + vi * opi, si + vi * opr - vr * opi]
            return tuple(out)

        init = ()
        if prev_of is not None:
            zero = jnp.zeros((SUBLANES, LANES), F32)
            for b in blks:
                init += (prev_of[1][b], prev_of[2][b], zero, zero)
        done = lax.fori_loop(0, S5_STEPS, fix, init, unroll=True)
        for j in range(len(blks) if prev_of is not None else 0):
            sums_re.append(done[4 * j + 2])
            sums_im.append(done[4 * j + 3])
        seg_in_re += ins_r
        seg_in_im += ins_i
    return seg_in_re, seg_in_im, sums_re, sums_im


def _s5_pack(xs, dst):
    for b in range(2 * S5_COLBLK):
        dst[:, b * LANES:(b + 1) * LANES] = xs[b].astype(BF16)


def _s5_fwd(u, b_blk, c_blk, tables, d_skip, exchange=None):
    seq = u.shape[0]
    t = S5_TILE
    nt = seq // t
    lr8, li8, p_re, p_im = tables

    p, pt = _scan_order()

    def body(u_ref, p_ref, pt_ref, b_ref, c_ref, lr_ref, li_ref, pr_ref, pi_ref, d_ref, y_ref, start_ref,
             xs, xb, carry_re, carry_im):
        @pl.when(pl.program_id(0) == 0)
        def _():
            carry_re[...] = jnp.zeros_like(carry_re)
            carry_im[...] = jnp.zeros_like(carry_im)

        start_ref[0, 0:1, :] = carry_re[...]
        start_ref[0, 1:2, :] = carry_im[...]
        uv = u_ref[...]
        _s5_channels_to_states(_to_scan_order(p_ref, uv.astype(BF16)), b_ref, xs)
        _s5_scan_tile(xs, lr_ref, li_ref, pr_ref, pi_ref, carry_re, carry_im, conj=False, reverse=False)
        _s5_pack(xs, xb)
        y_ref[...] = _from_scan_order(pt_ref, _s5_states_to_channels(xb, c_ref)) + d_ref[...] * uv

    return _call(
        body, "s5_fwd", (nt,),
        [_rows(t, SSM_WIDTH), _full((t, t)), _full((t, t)), _full(S5_TILES_SHAPE), _full(S5_TILES_SHAPE),
         _full((SUBLANES, N_STATES)), _full((SUBLANES, N_STATES)), _full((SUBLANES * S5_STEPS, N_STATES)),
         _full((SUBLANES * S5_STEPS, N_STATES)), _full((1, SSM_WIDTH))],
        [_rows(t, SSM_WIDTH), pl.BlockSpec((1, 2, N_STATES), lambda i: (i, 0, 0))],
        [jax.ShapeDtypeStruct((seq, SSM_WIDTH), F32), jax.ShapeDtypeStruct((nt, 2, N_STATES), F32)],
        [pltpu.VMEM((2 * S5_COLBLK, t, LANES), F32), pltpu.VMEM((t, 2 * N_STATES), BF16),
         pltpu.VMEM((1, N_STATES), F32), pltpu.VMEM((1, N_STATES), F32)],
        (u, p, pt, b_blk, c_blk, lr8, li8, p_re, p_im, d_skip), exchange)


def _s5_bwd(u, dy, b_blk, c_blk, tables, d_skip, starts, exchange=None):
    seq = u.shape[0]
    t = S5_TILE
    nt = seq // t
    lr8, li8, p_re, p_im = tables
    p, pt = _scan_order()

    def body(u_ref, dy_ref, p_ref, pt_ref, b_ref, c_ref, lr_ref, li_ref, pr_ref, pi_ref, d_ref, start_ref,
             du_ref, dd_ref, dlam_ref, db_ref, dc_ref,
             xs, as_, xb, ab, carry_re, carry_im, acar_re, acar_im):
        @pl.when(pl.program_id(0) == 0)
        def _():
            acar_re[...] = jnp.zeros_like(acar_re)
            acar_im[...] = jnp.zeros_like(acar_im)
            dd_ref[...] = jnp.zeros_like(dd_ref)
            dlam_ref[...] = jnp.zeros_like(dlam_ref)
            db_ref[...] = jnp.zeros_like(db_ref)
            dc_ref[...] = jnp.zeros_like(dc_ref)

        uv = u_ref[...]
        dyv = dy_ref[...]
        ub = _to_scan_order(p_ref, uv.astype(BF16))
        dyb = _to_scan_order(p_ref, dyv.astype(BF16))
        carry_re[...] = start_ref[0, 0:1, :]
        carry_im[...] = start_ref[0, 1:2, :]
        _s5_channels_to_states(ub, b_ref, xs)
        in_re, in_im, _, _ = _s5_scan_tile(xs, lr_ref, li_ref, pr_ref, pi_ref, carry_re, carry_im, conj=False, reverse=False)
        _s5_channels_to_states(dyb, c_ref, as_)
        _, _, dl_re, dl_im = _s5_scan_tile(as_, lr_ref, li_ref, pr_ref, pi_ref, acar_re, acar_im, conj=True, reverse=True,
                                           prev_of=(xs, in_re, in_im))
        for b in range(S5_COLBLK):
            cols = slice(b * LANES, (b + 1) * LANES)
            dlam_ref[0, :, cols] += dl_re[b]
            dlam_ref[1, :, cols] += dl_im[b]
        _s5_pack(xs, xb)
        _s5_pack(as_, ab)
        du_ref[...] = (_from_scan_order(pt_ref, _s5_states_to_channels(ab, b_ref)) + d_ref[...] * dyv).astype(BF16)
        dd_ref[...] += jnp.sum(dyv * uv, axis=0, keepdims=True)
        _s5_outer_acc(db_ref, ub, ab)
        _s5_outer_acc(dc_ref, dyb, xb)

    rev = _rows_rev(t, SSM_WIDTH, nt)
    vec = lambda: pltpu.VMEM((1, N_STATES), F32)
    outer = S5_TILES_SHAPE
    return _call(
        body, "s5_bwd", (nt,),
        [rev, rev, _full((t, t)), _full((t, t)), _full(S5_TILES_SHAPE), _full(S5_TILES_SHAPE),
         _full((SUBLANES, N_STATES)), _full((SUBLANES, N_STATES)), _full((SUBLANES * S5_STEPS, N_STATES)),
         _full((SUBLANES * S5_STEPS, N_STATES)), _full((1, SSM_WIDTH)),
         pl.BlockSpec((1, 2, N_STATES), lambda i: (nt - 1 - i, 0, 0))],
        [rev, _full((1, SSM_WIDTH)), _full((2, SUBLANES, N_STATES)), _full(outer), _full(outer)],
        [jax.ShapeDtypeStruct((seq, SSM_WIDTH), BF16), jax.ShapeDtypeStruct((1, SSM_WIDTH), F32),
         jax.ShapeDtypeStruct((2, SUBLANES, N_STATES), F32), jax.ShapeDtypeStruct(outer, F32), jax.ShapeDtypeStruct(outer, F32)],
        [pltpu.VMEM((2 * S5_COLBLK, t, LANES), F32), pltpu.VMEM((2 * S5_COLBLK, t, LANES), F32),
         pltpu.VMEM((t, 2 * N_STATES), BF16), pltpu.VMEM((t, 2 * N_STATES), BF16), vec(), vec(), vec(), vec()],
        (u, dy, p, pt, b_blk, c_blk, lr8, li8, p_re, p_im, d_skip, starts), exchange)


def _mix_fwd(y_ssm, y_ret, x, gain, w_glu, w_out):
    seq = x.shape[0]
    t = TOK_TILE

    def body(ys_ref, yr_ref, x_ref, g_ref, wg_ref, wo_ref, x1_ref, mix_ref):
        g0 = _gelu(ys_ref[...]).astype(BF16)
        z = _dot(g0, wg_ref[...])
        glu = (z[:, :SSM_WIDTH] * _sigmoid(z[:, SSM_WIDTH:])).astype(BF16)
        mix = _dot(yr_ref[...], wo_ref[:RET_WIDTH, :]) + _dot(glu, wo_ref[RET_WIDTH:, :])
        mix_ref[...] = mix
        n, _ = _rms(mix)
        x1_ref[...] = x_ref[...] + n * g_ref[...]

    act = jax.ShapeDtypeStruct((seq, D_MODEL), F32)
    return pl.pallas_call(
        body, name="mix_fwd", grid=(seq // t,),
        in_specs=[_rows(t, SSM_WIDTH), _rows(t, RET_WIDTH), _rows(t, D_MODEL), _full((1, D_MODEL)),
                  _full((SSM_WIDTH, 2 * SSM_WIDTH)), _full((D_MODEL, D_MODEL))],
        out_specs=[_rows(t, D_MODEL)] * 2, out_shape=[act, act],
        compiler_params=_params("parallel"),
    )(y_ssm, y_ret, x, gain, w_glu, w_out)


def _mix_bwd(dx1, mix, gain, y_ssm, y_ret, w_glu, w_out, exchange=None):
    seq = dx1.shape[0]
    t = TOK_TILE

    def body(dx1_ref, mix_ref, g_ref, ys_ref, yr_ref, wg_ref, wo_ref, dyr_ref, dys_ref, dwo_ref, dwg_ref, dg_ref):
        @pl.when(pl.program_id(0) == 0)
        def _():
            dwo_ref[...] = jnp.zeros_like(dwo_ref)
            dwg_ref[...] = jnp.zeros_like(dwg_ref)
            dg_ref[...] = jnp.zeros_like(dg_ref)

        n, r = _rms(mix_ref[...])
        dmix, dgs = _rms_bwd(n, r, g_ref[...], dx1_ref[...])
        dg_ref[...] += dgs
        dmb = dmix.astype(BF16)
        dcat = _dot_nt(dmb, wo_ref[...])
        dyr_ref[...] = dcat[:, :RET_WIDTH]
        dglu = dcat[:, RET_WIDTH:]
        ys = ys_ref[...]
        g0 = _gelu(ys).astype(BF16)
        z = _dot(g0, wg_ref[...])
        a = z[:, :SSM_WIDTH]
        sb = _sigmoid(z[:, SSM_WIDTH:])
        dwo_ref[:RET_WIDTH, :] += _dot_tn(yr_ref[...], dmb)
        dwo_ref[RET_WIDTH:, :] += _dot_tn((a * sb).astype(BF16), dmb)
        dz = jnp.concatenate([dglu * sb, dglu * a * sb * (1.0 - sb)], axis=1).astype(BF16)
        dwg_ref[...] += _dot_tn(g0, dz)
        dys_ref[...] = _dot_nt(dz, wg_ref[...]) * _gelu_grad(ys)

    half = jax.ShapeDtypeStruct((seq, RET_WIDTH), F32)
    return _call(
        body, "mix_bwd", (seq // t,),
        [_rows(t, D_MODEL), _rows(t, D_MODEL), _full((1, D_MODEL)), _rows(t, SSM_WIDTH), _rows(t, RET_WIDTH),
         _full((SSM_WIDTH, 2 * SSM_WIDTH)), _full((D_MODEL, D_MODEL))],
        [_rows(t, RET_WIDTH), _rows(t, SSM_WIDTH), _full((D_MODEL, D_MODEL)), _full((SSM_WIDTH, 2 * SSM_WIDTH)),
         _full((1, D_MODEL))],
        [half, half, jax.ShapeDtypeStruct((D_MODEL, D_MODEL), F32), jax.ShapeDtypeStruct((SSM_WIDTH, 2 * SSM_WIDTH), F32),
         jax.ShapeDtypeStruct((1, D_MODEL), F32)],
        [], (dx1, mix, gain, y_ssm, y_ret, w_glu, w_out), exchange)


def _mlp_a(x1, target, gain_pre, gain_post, w1, w2):
    seq = x1.shape[0]
    t = MLP_TILE

    def body(x1_ref, tg_ref, gp_ref, gq_ref, w1_ref, w2_ref, df_ref, dx2_ref, dw2_ref, dgq_ref, sq_ref):
        @pl.when(pl.program_id(0) == 0)
        def _():
            dw2_ref[...] = jnp.zeros_like(dw2_ref)
            dgq_ref[...] = jnp.zeros_like(dgq_ref)
            sq_ref[...] = jnp.zeros_like(sq_ref)

        x1v = x1_ref[...]
        n3, _ = _rms(x1v)
        h = (n3 * gp_ref[...]).astype(BF16)
        rl = jnp.maximum(_dot(h, w1_ref[...]), 0.0)
        act = (rl * rl).astype(BF16)
        n4, r4 = _rms(_dot(act, w2_ref[...]))
        gq = gq_ref[...]
        err = x1v + n4 * gq - tg_ref[...]
        sq_ref[...] += jnp.sum(err * err, axis=0, keepdims=True)
        dx2 = err * (1.0 / D_MODEL)
        dx2_ref[...] = dx2
        dm, dgs = _rms_bwd(n4, r4, gq, dx2)
        dgq_ref[...] += dgs
        dmb = dm.astype(BF16)
        dw2_ref[...] += _dot_tn(act, dmb)
        df_ref[...] = (_dot_nt(dmb, w2_ref[...]) * (2.0 * rl)).astype(BF16)

    return pl.pallas_call(
        body, name="mlp_a", grid=(seq // t,),
        in_specs=[_rows(t, D_MODEL), _rows(t, D_MODEL), _full((1, D_MODEL)), _full((1, D_MODEL)),
                  _full((D_MODEL, D_FF)), _full((D_FF, D_MODEL))],
        out_specs=[_rows(t, D_FF), _rows(t, D_MODEL), _full((D_FF, D_MODEL)), _full((1, D_MODEL)), _full((1, D_MODEL))],
        out_shape=[jax.ShapeDtypeStruct((seq, D_FF), BF16), jax.ShapeDtypeStruct((seq, D_MODEL), F32),
                   jax.ShapeDtypeStruct((D_FF, D_MODEL), F32), jax.ShapeDtypeStruct((1, D_MODEL), F32),
                   jax.ShapeDtypeStruct((1, D_MODEL), F32)],
        compiler_params=_params("arbitrary"),
    )(x1, target, gain_pre, gain_post, w1, w2)


def _mlp_b(df, dx2, x1, gain_pre, w1):
    seq = x1.shape[0]
    t = TOK_TILE

    def body(df_ref, dx2_ref, x1_ref, gp_ref, w1_ref, dx1_ref, dw1_ref, dgp_ref):
        @pl.when(pl.program_id(0) == 0)
        def _():
            dw1_ref[...] = jnp.zeros_like(dw1_ref)
            dgp_ref[...] = jnp.zeros_like(dgp_ref)

        n3, r3 = _rms(x1_ref[...])
        gp = gp_ref[...]
        h = (n3 * gp).astype(BF16)
        dfv = df_ref[...]
        dw1_ref[...] += _dot_tn(h, dfv)
        dx, dgs = _rms_bwd(n3, r3, gp, _dot_nt(dfv, w1_ref[...]))
        dgp_ref[...] += dgs
        dx1_ref[...] = dx2_ref[...] + dx

    return pl.pallas_call(
        body, name="mlp_b", grid=(seq // t,),
        in_specs=[_rows(t, D_FF), _rows(t, D_MODEL), _rows(t, D_MODEL), _full((1, D_MODEL)), _full((D_MODEL, D_FF))],
        out_specs=[_rows(t, D_MODEL), _full((D_MODEL, D_FF)), _full((1, D_MODEL))],
        out_shape=[jax.ShapeDtypeStruct((seq, D_MODEL), F32), jax.ShapeDtypeStruct((D_MODEL, D_FF), F32),
                   jax.ShapeDtypeStruct((1, D_MODEL), F32)],
        compiler_params=_params("arbitrary"),
    )(df, dx2, x1, gain_pre, w1)


def _local_step(x, target, small, weights, ids=None):
    cos2, sin2 = _rope_tables(x.shape[0])
    ret_consts = _ret_consts()

    s5_names = ("ssm_lambda_re", "ssm_lambda_im", "ssm_log_dt", "ssm_b_re", "ssm_b_im", "ssm_c_re", "ssm_c_im")
    (lbr, lbi, b_tiles, c_tiles), disc_vjp = jax.vjp(_s5_discretise, *[small[name] for name in s5_names])
    tables = _s5_tables(lbr, lbi)
    b_blk = b_tiles.astype(BF16)
    c_blk = c_tiles.astype(BF16)
    d_skip = small["ssm_d"].reshape(1, SSM_WIDTH)
    gn_gain = small["ret_gn_gain"].reshape(1, RET_WIDTH)
    g_mix_pre = small["norm_mix_pre"].reshape(1, D_MODEL)
    g_mix_post = small["norm_mix_post"].reshape(1, D_MODEL)
    g_mlp_pre = small["norm_mlp_pre"].reshape(1, D_MODEL)
    g_mlp_post = small["norm_mlp_post"].reshape(1, D_MODEL)

    dist = ids is not None
    w_in, w_glu, w_out, w_ff1, w_ff2 = weights
    (q, k, v, gate, u), got = _inproj_fwd(x, g_mix_pre, w_in, cos2, sin2, _x_gather_send((1, 2), [w_glu, w_out]) if dist else None)
    if dist:
        w_glu, w_out = got
    (y_ssm, starts), got = _s5_fwd(u, b_blk, c_blk, tables, d_skip, _x_gather_send(LATE, [w_ff1, w_ff2]) if dist else None)
    if dist:
        w_ff1, w_ff2 = got
    (y_ret, rprev), got = _ret_fwd(q, k, v, gate, gn_gain, ret_consts,
                                   _x_gather_pass((1, 2) + LATE, [w_glu, w_out, w_ff1, w_ff2]) if dist else None)
    if dist:
        w_glu, w_out, w_ff1, w_ff2 = got
    x1, mix = _mix_fwd(y_ssm, y_ret, x, g_mix_post, w_glu, w_out)
    df, dx2, d_ff2, dg_mlp_post, sq = _mlp_a(x1, target, g_mlp_pre, g_mlp_post, w_ff1, w_ff2)
    dx1, d_ff1, dg_mlp_pre = _mlp_b(df, dx2, x1, g_mlp_pre, w_ff1)
    (dy_ret, dy_ssm, d_out, d_glu, dg_mix_post), got = _mix_bwd(dx1, mix, g_mix_post, y_ssm, y_ret, w_glu, w_out,
                                                               _x_pair(LATE, [d_ff1, d_ff2]) if dist else None)
    if dist:
        sums, sums_bf16 = _pair_sum(ids, LATE, [d_ff1, d_ff2], got)
    (du, dd, dlam8, db_tiles, dc_tiles), got = _s5_bwd(
        u, dy_ssm, b_blk, c_blk, tables, d_skip, starts,
        _x_join(_x_chip(LATE, sums_bf16), _x_pair(MID, [d_glu, d_out])) if dist else None)
    if dist:
        parts = _chip_sum(ids, LATE, sums, got[:len(LATE)])
        sums, sums_bf16 = _pair_sum(ids, MID, [d_glu, d_out], got[len(LATE):])
    (dq, dk, dv, dgate, dgn), got = _ret_bwd(q, k, v, gate, gn_gain, ret_consts, rprev, dy_ret, cos2, sin2,
                                             _x_join(_x_share(LATE, parts), _x_chip(MID, sums_bf16)) if dist else None)
    if dist:
        late_grads = got[:len(LATE)]
        parts = _chip_sum(ids, MID, sums, got[len(LATE):])
        mid_parts = parts
    (dx, d_in, dg_mix_pre), _ = _inproj_bwd(x, g_mix_pre, w_in, [dq, dk, dv, dgate, du], dx1)

    dlam = jnp.sum(dlam8, axis=1)
    s5_grads = disc_vjp((dlam[0].reshape(SSM_GROUPS, SSM_STATE), dlam[1].reshape(SSM_GROUPS, SSM_STATE), db_tiles, dc_tiles))
    small_grads = {
        "norm_mix_pre": dg_mix_pre, "norm_mix_post": dg_mix_post, "ret_gn_gain": dgn, "ssm_d": dd,
        "norm_mlp_pre": dg_mlp_pre, "norm_mlp_post": dg_mlp_post, "loss_sum": 0.5 * jnp.sum(sq) / D_MODEL,
    }
    small_grads.update(dict(zip(s5_names, s5_grads)))
    if dist:
        return dx, d_in, mid_parts, late_grads, small_grads
    return dx, (d_in, d_glu, d_out, d_ff1, d_ff2), small_grads


BIG = (
    ("w_in", D_MODEL, IN_COLS, 1),
    ("w_glu", SSM_WIDTH, 2 * SSM_WIDTH, 1),
    ("w_out", D_MODEL, D_MODEL, 0),
    ("w_ff1", D_MODEL, D_FF, 1),
    ("w_ff2", D_FF, D_MODEL, 0),
)
SMALL = (
    ("norm_mix_pre", (D_MODEL,)), ("norm_mix_post", (D_MODEL,)), ("ret_gn_gain", (RET_WIDTH,)),
    ("ssm_lambda_re", (SSM_GROUPS, SSM_STATE)), ("ssm_lambda_im", (SSM_GROUPS, SSM_STATE)), ("ssm_log_dt", (SSM_GROUPS,)),
    ("ssm_b_re", (SSM_GROUPS, SSM_STATE, SSM_GROUP)), ("ssm_b_im", (SSM_GROUPS, SSM_STATE, SSM_GROUP)),
    ("ssm_c_re", (SSM_GROUPS, SSM_GROUP, SSM_STATE)), ("ssm_c_im", (SSM_GROUPS, SSM_GROUP, SSM_STATE)),
    ("ssm_d", (SSM_WIDTH,)), ("norm_mlp_pre", (D_MODEL,)), ("norm_mlp_post", (D_MODEL,)),
)
SMALL_ROWS = 1152
EXCH_TILES = 8


PACKED = SMALL + (("loss_sum", (1,)),)


def _packed_rows(shape):
    return -(-math.prod(shape) // (SUBLANES * LANES)) * SUBLANES


def _pack_small(tree):
    rows = []
    for name, shape in PACKED:
        size = math.prod(shape)
        flat = tree[name].reshape(-1).astype(F32) if name in tree else jnp.zeros((size,), F32)
        rows.append(jnp.pad(flat, (0, _packed_rows(shape) * LANES - size)).reshape(-1, LANES))
    used = sum(r.shape[0] for r in rows)
    rows.append(jnp.zeros((SMALL_ROWS - used, LANES), F32))
    return jnp.concatenate(rows, axis=0)


def _unpack_small(packed):
    out, row = {}, 0
    for name, shape in PACKED:
        size = math.prod(shape)
        out[name] = packed[row:row + _packed_rows(shape)].reshape(-1)[:size].reshape((1,) + shape)
        row += _packed_rows(shape)
    return out


def _half_shape(r, c, axis):
    return (r // 2, c) if axis == 1 else (r, c // 2)


def _region_shape(r, c, axis):
    return (r // 2, c // N_CHIPS) if axis == 1 else (r // N_CHIPS, c // 2)


def _shard_shape(r, c, axis):
    return (r, c // N_CHIPS) if axis == 1 else (r // N_CHIPS, c)


def _ds(start, size):
    return pl.ds(pl.multiple_of(start * size, size), size)


def _region_of_full(ref, r, c, axis, shard, half):
    if axis == 1:
        return ref.at[_ds(half, r // 2), _ds(shard, c // N_CHIPS)]
    return ref.at[_ds(shard, r // N_CHIPS), _ds(half, c // 2)]


def _half_of_full(ref, r, c, axis, half):
    if axis == 1:
        return ref.at[_ds(half, r // 2), :]
    return ref.at[:, _ds(half, c // 2)]


def _half_of_shard(ref, r, c, axis, half):
    if axis == 1:
        return ref.at[_ds(half, r // 2), :]
    return ref.at[:, _ds(half, c // 2)]


def _region_of_half(ref, r, c, axis, shard):
    if axis == 1:
        return ref.at[:, _ds(shard, c // N_CHIPS)]
    return ref.at[_ds(shard, r // N_CHIPS), :]


def _place():
    x, y, c = lax.axis_index("x"), lax.axis_index("y"), lax.axis_index("c")
    chips = [(1 - x, y), (x, 1 - y), (1 - x, 1 - y)]
    return x, y, c, chips


LATE = (3, 4)
MID = (1, 2)
FIRST = (0,)
SMALL_HALF = (SMALL_ROWS // 2, LANES)


def _remote(src, dst, send_sem, recv_sem, to):
    return pltpu.make_async_remote_copy(src_ref=src, dst_ref=dst, send_sem=send_sem, recv_sem=recv_sem,
                                        device_id=to, device_id_type=MESH)


def _same(arrays):
    return [jax.ShapeDtypeStruct(a.shape, a.dtype) for a in arrays]


def _x_gather_send(ws, fulls):
    def copies(ins, outs, send_sems, recv_sems):
        x, y, c, chips = _place()
        out = []
        for j, w in enumerate(ws):
            _, r, cc, axis = BIG[w]
            mine = _region_of_full(outs[j], r, cc, axis, 2 * x + y, c)
            out += [_remote(mine, mine, send_sems.at[3 * j + k], recv_sems.at[3 * j + k], (cx, cy, c))
                    for k, (cx, cy) in enumerate(chips)]
        return out

    return _Exchange(fulls, _same(fulls), {j: j for j in range(len(ws))}, 3 * len(ws), copies)


def _x_gather_pass(ws, fulls):
    def copies(ins, outs, send_sems, recv_sems):
        x, y, c, chips = _place()
        out = []
        for j, w in enumerate(ws):
            _, r, cc, axis = BIG[w]
            for k, (cx, cy) in enumerate(chips):
                landed = _region_of_full(outs[j], r, cc, axis, 2 * cx + cy, c)
                out.append(_remote(landed, landed, send_sems.at[3 * j + k], recv_sems.at[3 * j + k], (x, y, 1 - c)))
        return out

    return _Exchange(fulls, _same(fulls), {j: j for j in range(len(ws))}, 3 * len(ws), copies)


def _x_pair(ws, grads, small=None):
    def copies(ins, outs, send_sems, recv_sems):
        x, y, c, _ = _place()
        out = []
        for j, w in enumerate(ws):
            _, r, cc, axis = BIG[w]
            out.append(_remote(_half_of_full(ins[j], r, cc, axis, 1 - c), outs[j], send_sems.at[j], recv_sems.at[j], (x, y, 1 - c)))
        if small is not None:
            j = len(ws)
            out.append(_remote(ins[j].at[_ds(1 - c, SMALL_ROWS // 2), :], outs[j], send_sems.at[j], recv_sems.at[j], (x, y, 1 - c)))
        return out

    shapes = [jax.ShapeDtypeStruct(_half_shape(*BIG[w][1:]), F32) for w in ws]
    extra = [] if small is None else [small]
    return _Exchange(list(grads) + extra, shapes + [jax.ShapeDtypeStruct(SMALL_HALF, F32)] * len(extra), {},
                     len(ws) + len(extra), copies)


def _x_chip(ws, sums_bf16, small_sum=None):
    def copies(ins, outs, send_sems, recv_sems):
        x, y, c, chips = _place()
        out = []
        for j, w in enumerate(ws):
            _, r, cc, axis = BIG[w]
            out += [_remote(_region_of_half(ins[j], r, cc, axis, 2 * cx + cy), outs[j].at[k],
                            send_sems.at[3 * j + k], recv_sems.at[3 * j + k], (cx, cy, c)) for k, (cx, cy) in enumerate(chips)]
        if small_sum is not None:
            j = len(ws)
            out += [_remote(ins[j], outs[j].at[k], send_sems.at[3 * j + k], recv_sems.at[3 * j + k], (cx, cy, c))
                    for k, (cx, cy) in enumerate(chips)]
        return out

    shapes = [jax.ShapeDtypeStruct((3,) + _region_shape(*BIG[w][1:]), BF16) for w in ws]
    extra = [] if small_sum is None else [small_sum]
    return _Exchange(list(sums_bf16) + extra, shapes + [jax.ShapeDtypeStruct((3,) + SMALL_HALF, F32)] * len(extra), {},
                     3 * (len(ws) + len(extra)), copies)


def _x_share(ws, shards, small=None):
    def copies(ins, outs, send_sems, recv_sems):
        x, y, c, _ = _place()
        out = []
        for j, w in enumerate(ws):
            _, r, cc, axis = BIG[w]
            mine = _half_of_shard(outs[j], r, cc, axis, c)
            out.append(_remote(mine, mine, send_sems.at[j], recv_sems.at[j], (x, y, 1 - c)))
        if small is not None:
            j = len(ws)
            mine = outs[j].at[_ds(c, SMALL_ROWS // 2), :]
            out.append(_remote(mine, mine, send_sems.at[j], recv_sems.at[j], (x, y, 1 - c)))
        return out

    arrays = list(shards) + ([] if small is None else [small])
    return _Exchange(arrays, _same(arrays), {j: j for j in range(len(arrays))}, len(arrays), copies)


class _Offset:
    def __init__(self, sems, base):
        self._sems, self._base = sems, base

    @property
    def at(self):
        return self

    def __getitem__(self, i):
        return self._sems.at[self._base + i]


def _x_join(a, b):
    ka_in, ka_out = len(a.operands), len(a.out_shapes)

    def copies(ins, outs, send_sems, recv_sems):
        return (a.copies(ins[:ka_in], outs[:ka_out], send_sems, recv_sems)
                + b.copies(ins[ka_in:], outs[ka_out:], _Offset(send_sems, a.n), _Offset(recv_sems, a.n)))

    aliases = dict(a.aliases)
    aliases.update({ka_in + i: ka_out + o for i, o in b.aliases.items()})
    return _Exchange(a.operands + b.operands, a.out_shapes + b.out_shapes, aliases, a.n + b.n, copies)


def _run_exchange(name, exchange):
    k_in, k_out = len(exchange.operands), len(exchange.out_shapes)

    def body(*refs):
        copies = exchange.copies(refs[:k_in], refs[k_in:k_in + k_out], refs[-2], refs[-1])
        for cp in copies:
            cp.start()
        for cp in copies:
            cp.wait()

    return list(pl.pallas_call(
        body, name=name, in_specs=[ANY] * k_in, out_specs=[ANY] * k_out, out_shape=exchange.out_shapes,
        scratch_shapes=[pltpu.SemaphoreType.DMA((exchange.n,)), pltpu.SemaphoreType.DMA((exchange.n,))],
        input_output_aliases=dict(exchange.aliases),
    )(*exchange.operands))


def _tile_rows(shape):
    return (shape[0] // EXCH_TILES, shape[1])


def _cast_place(ids, shards):
    def body(ids_ref, *refs):
        for j in range(len(BIG)):
            refs[len(BIG) + j][...] = refs[j][...].astype(BF16)

    def out_spec(w):
        _, r, cc, axis = BIG[w]
        tile = _tile_rows(_shard_shape(r, cc, axis))
        if axis == 1:
            return pl.BlockSpec(tile, lambda i, ids: (i, ids[1]))
        return pl.BlockSpec(tile, lambda i, ids: (ids[1] * EXCH_TILES + i, 0))

    return list(pl.pallas_call(
        body, name="cast_place",
        grid_spec=pltpu.PrefetchScalarGridSpec(
            num_scalar_prefetch=1, grid=(EXCH_TILES,),
            in_specs=[pl.BlockSpec(_tile_rows(_shard_shape(r, cc, axis)), lambda i, ids: (i, 0)) for _, r, cc, axis in BIG],
            out_specs=[out_spec(w) for w in range(len(BIG))]),
        out_shape=[jax.ShapeDtypeStruct((r, cc), BF16) for _, r, cc, _ in BIG],
        compiler_params=pltpu.CompilerParams(dimension_semantics=("parallel",)),
    )(ids, *shards))


def _pair_sum(ids, ws, grads, got, small=None):
    nb = len(ws)
    n = nb + (small is not None)
    halves = [_half_shape(*BIG[w][1:]) for w in ws] + [SMALL_HALF] * (n - nb)

    def body(ids_ref, *refs):
        for j in range(n):
            total = refs[j][...] + refs[n + j][...]
            refs[2 * n + j][...] = total
            if j < nb:
                refs[3 * n + j][...] = total.astype(BF16)

    def mine_spec(j):
        tile = _tile_rows(halves[j])
        if j < nb and BIG[ws[j]][3] == 0:
            return pl.BlockSpec(tile, lambda i, ids: (i, ids[0]))
        return pl.BlockSpec(tile, lambda i, ids: (ids[0] * EXCH_TILES + i, 0))

    plain = lambda j: pl.BlockSpec(_tile_rows(halves[j]), lambda i, ids: (i, 0))
    outs = pl.pallas_call(
        body, name="pair_sum",
        grid_spec=pltpu.PrefetchScalarGridSpec(
            num_scalar_prefetch=1, grid=(EXCH_TILES,),
            in_specs=[mine_spec(j) for j in range(n)] + [plain(j) for j in range(n)],
            out_specs=[plain(j) for j in range(n)] + [plain(j) for j in range(nb)]),
        out_shape=[jax.ShapeDtypeStruct(h, F32) for h in halves] + [jax.ShapeDtypeStruct(h, BF16) for h in halves[:nb]],
        compiler_params=pltpu.CompilerParams(dimension_semantics=("parallel",)),
    )(ids, *grads, *([] if small is None else [small]), *got)
    return list(outs[:n]), list(outs[n:])


def _chip_sum(ids, ws, sums, arrived):
    nb, n = len(ws), len(sums)
    regions = [_region_shape(*BIG[w][1:]) for w in ws] + [SMALL_HALF] * (n - nb)

    def body(ids_ref, *refs):
        for j in range(n):
            own, got, out = refs[j], refs[n + j], refs[2 * n + j]
            if j < nb:
                out[...] = ((own[...] + got[0].astype(F32)) + got[1].astype(F32)) + got[2].astype(F32)
            else:
                out[...] = (own[...] + got[1]) + (got[0] + got[2])

    def own_spec(j):
        tile = _tile_rows(regions[j])
        if j >= nb:
            return pl.BlockSpec(tile, lambda i, ids: (i, 0))
        if BIG[ws[j]][3] == 1:
            return pl.BlockSpec(tile, lambda i, ids: (i, ids[1]))
        return pl.BlockSpec(tile, lambda i, ids: (ids[1] * EXCH_TILES + i, 0))

    def out_spec(j):
        tile = _tile_rows(regions[j])
        if j < nb and BIG[ws[j]][3] == 0:
            return pl.BlockSpec(tile, lambda i, ids: (i, ids[0]))
        return pl.BlockSpec(tile, lambda i, ids: (ids[0] * EXCH_TILES + i, 0))

    got_spec = lambda j: pl.BlockSpec((3,) + _tile_rows(regions[j]), lambda i, ids: (0, i, 0))
    return list(pl.pallas_call(
        body, name="chip_sum",
        grid_spec=pltpu.PrefetchScalarGridSpec(
            num_scalar_prefetch=1, grid=(EXCH_TILES,),
            in_specs=[own_spec(j) for j in range(n)] + [got_spec(j) for j in range(n)],
            out_specs=[out_spec(j) for j in range(n)]),
        out_shape=[jax.ShapeDtypeStruct(_shard_shape(*BIG[w][1:]), F32) for w in ws]
        + [jax.ShapeDtypeStruct((SMALL_ROWS, LANES), F32)] * (n - nb),
        compiler_params=pltpu.CompilerParams(dimension_semantics=("parallel",)),
    )(ids, *sums, *arrived))


def _adamw(name, ws, gs, ms, vs, exchange=None):
    n = len(ws)
    c1 = 1.0 - ADAM_B1 ** ADAM_STEP
    c2 = 1.0 - ADAM_B2 ** ADAM_STEP

    def body(*refs):
        for j in range(n):
            w, g, m, v = (refs[k * n + j][...] for k in range(4))
            m = ADAM_B1 * m + (1.0 - ADAM_B1) * g
            v = ADAM_B2 * v + (1.0 - ADAM_B2) * (g * g)
            refs[4 * n + j][...] = -ADAM_LR * ((m / c1) / (jnp.sqrt(v / c2) + ADAM_EPS) + ADAM_WD * w)
            refs[5 * n + j][...] = m
            refs[6 * n + j][...] = v

    specs = [pl.BlockSpec(_tile_rows(w.shape), lambda i: (i, 0)) for w in ws]
    shapes = [jax.ShapeDtypeStruct(w.shape, F32) for w in ws]
    out, got = _call(body, name, (EXCH_TILES,), specs * 4, specs * 3, shapes * 3, [], (*ws, *gs, *ms, *vs), exchange)
    return (out[:n], out[n:2 * n], out[2 * n:]), got


def kernel(x, norm_mix_pre, norm_mix_post, w_in, ret_gn_gain, ssm_lambda_re, ssm_lambda_im, ssm_log_dt, ssm_b_re, ssm_b_im, ssm_c_re, ssm_c_im, ssm_d, w_glu, w_out, norm_mlp_pre, norm_mlp_post, w_ff1, w_ff2, loss_target, m_norm_mix_pre, m_norm_mix_post, m_w_in, m_ret_gn_gain, m_ssm_lambda_re, m_ssm_lambda_im, m_ssm_log_dt, m_ssm_b_re, m_ssm_b_im, m_ssm_c_re, m_ssm_c_im, m_ssm_d, m_w_glu, m_w_out, m_norm_mlp_pre, m_norm_mlp_post, m_w_ff1, m_w_ff2, v_norm_mix_pre, v_norm_mix_post, v_w_in, v_ret_gn_gain, v_ssm_lambda_re, v_ssm_lambda_im, v_ssm_log_dt, v_ssm_b_re, v_ssm_b_im, v_ssm_c_re, v_ssm_c_im, v_ssm_d, v_w_glu, v_w_out, v_norm_mlp_pre, v_norm_mlp_post, v_w_ff1, v_w_ff2):
    given = dict(locals())
    order = ["norm_mix_pre", "norm_mix_post", "w_in", "ret_gn_gain", "ssm_lambda_re", "ssm_lambda_im", "ssm_log_dt",
             "ssm_b_re", "ssm_b_im", "ssm_c_re", "ssm_c_im", "ssm_d", "w_glu", "w_out", "norm_mlp_pre", "norm_mlp_post",
             "w_ff1", "w_ff2"]
    big_names = [name for name, _, _, _ in BIG]
    ids = jnp.stack([lax.axis_index("c"), 2 * lax.axis_index("x") + lax.axis_index("y")]).astype(jnp.int32)

    weights = _cast_place(ids, [given[name][0] for name in big_names])
    w_in = _run_exchange("gather_w_in_send", _x_gather_send((0,), weights[:1]))
    w_in = _run_exchange("gather_w_in_pass", _x_gather_pass((0,), w_in))
    small_w = {name: given[name][0] for name, _ in SMALL}
    dx, d_in, mid_parts, late_grads, small_grads = _local_step(x[0], loss_target[0], small_w, w_in + weights[1:], ids)

    small_packed = _pack_small(small_grads)
    got = _run_exchange("pair_exchange", _x_join(_x_share(MID, mid_parts), _x_pair(FIRST, [d_in], small_packed)))
    mid_grads, got = got[:len(MID)], got[len(MID):]
    sums, sums_bf16 = _pair_sum(ids, FIRST, [d_in], got, small_packed)
    arrived = _run_exchange("chip_exchange", _x_chip(FIRST, sums_bf16, sums[-1]))
    parts = _chip_sum(ids, FIRST, sums, arrived)
    g_in, g_small = _run_exchange("pair_share", _x_share(FIRST, parts[:-1], parts[-1]))
    nw = len(BIG)
    reduced = [g_in] + mid_grads + late_grads + [g_small]
    pack = lambda prefix: _pack_small({name: given[prefix + name][0] for name, _ in SMALL})
    state = lambda prefix: [given[prefix + name][0] for name in big_names] + [pack(prefix)]
    (deltas, new_m, new_v), _ = _adamw("adamw", state(""), reduced, state("m_"), state("v_"))

    def unpacked(arrays):
        tree = _unpack_small(arrays[nw])
        tree.update({name: arrays[j][None] for j, name in enumerate(big_names)})
        return [tree[name] for name in order]

    loss = _unpack_small(reduced[nw])["loss_sum"].reshape(())
    return (loss, dx[None], *unpacked(list(reduced)), *unpacked(deltas), *unpacked(new_m), *unpacked(new_v))
```

```python
import functools
import math

import jax
import jax.numpy as jnp
import numpy as np
from jax import lax
from jax.experimental import pallas as pl
from jax.experimental.pallas import tpu as pltpu

F32 = jnp.float32
BF16 = jnp.bfloat16

D_MODEL = 1024
RET_WIDTH = 512
RET_HEADS = 4
HEAD_DIM = 128
RET_CHUNK = 128
ROPE_BASE = 10000.0
SSM_WIDTH = 512
SSM_GROUP = 16
SSM_GROUPS = 32
SSM_STATE = 64
N_STATES = SSM_GROUPS * SSM_STATE
D_FF = 4096
IN_COLS = 4 * RET_WIDTH + SSM_WIDTH
NORM_EPS = 1e-6
K_SCALE = HEAD_DIM ** -0.5

ADAM_LR = 0.001
ADAM_B1 = 0.9
ADAM_B2 = 0.999
ADAM_EPS = 1e-08
ADAM_WD = 0.01
ADAM_STEP = 10

LANES = 128
SUBLANES = 8
VMEM_LIMIT = 56 * 2 ** 20

TOK_TILE = 512
MLP_TILE = 256
RET_CHUNKS_PER_STEP = 4
S5_TILE = 256
S5_STEPS = S5_TILE // SUBLANES
S5_COLBLK = N_STATES // LANES
S5_GROUP = 4

N_CHIPS = 4
MESH = pl.DeviceIdType.MESH


def _dot(a, b):
    return jnp.dot(a, b, preferred_element_type=F32)


def _dot_nt(a, b):
    return lax.dot_general(a, b, (((1,), (1,)), ((), ())), preferred_element_type=F32)


def _dot_tn(a, b):
    return lax.dot_general(a, b, (((0,), (0,)), ((), ())), preferred_element_type=F32)


def _sigmoid(x):
    return 1.0 / (1.0 + jnp.exp(-x))


_GELU_C = math.sqrt(2.0 / math.pi)


def _gelu(x):
    return 0.5 * x * (1.0 + jnp.tanh(_GELU_C * (x + 0.044715 * (x * x * x))))


def _gelu_grad(x):
    t = jnp.tanh(_GELU_C * (x + 0.044715 * (x * x * x)))
    return 0.5 * (1.0 + t) + 0.5 * x * (1.0 - t * t) * (_GELU_C * (1.0 + 3.0 * 0.044715 * (x * x)))


def _rms(x):
    r = lax.rsqrt(jnp.mean(x * x, axis=-1, keepdims=True) + NORM_EPS)
    return x * r, r


def _rms_bwd(n, r, gain, dy):
    dn = dy * gain
    dx = r * (dn - n * jnp.mean(dn * n, axis=-1, keepdims=True))
    return dx, jnp.sum(dy * n, axis=0, keepdims=True)


def _full(shape):
    nd = len(shape)
    return pl.BlockSpec(shape, lambda i, _n=nd: (0,) * _n, pipeline_mode=pl.Buffered(1))


def _rows(tile, width):
    return pl.BlockSpec((tile, width), lambda i: (i, 0))


def _rows_rev(tile, width, n):
    return pl.BlockSpec((tile, width), lambda i, _n=n: (_n - 1 - i, 0))


def _params(sem):
    return pltpu.CompilerParams(dimension_semantics=(sem,), vmem_limit_bytes=VMEM_LIMIT)


ANY = pl.BlockSpec(memory_space=pl.ANY)


class _Exchange:
    def __init__(self, operands, out_shapes, aliases, n, copies):
        self.operands, self.out_shapes, self.aliases, self.n, self.copies = list(operands), list(out_shapes), aliases, n, copies


def _call(body, name, grid, in_specs, out_specs, out_shape, scratch_shapes, args, exchange=None):
    if exchange is None:
        outs = pl.pallas_call(body, name=name, grid=grid, in_specs=in_specs, out_specs=out_specs, out_shape=out_shape,
                              scratch_shapes=scratch_shapes, compiler_params=_params("arbitrary"))(*args)
        return list(outs), []
    n_in, n_out, n_scr = len(in_specs), len(out_specs), len(scratch_shapes)
    k_in, k_out = len(exchange.operands), len(exchange.out_shapes)
    last = grid[0] - 1

    def hosted(*refs):
        own_in, rest = refs[:n_in], refs[n_in:]
        ex_in, rest = rest[:k_in], rest[k_in:]
        own_out, rest = rest[:n_out], rest[n_out:]
        ex_out, rest = rest[:k_out], rest[k_out:]
        own_scr, (send_sems, recv_sems) = rest[:n_scr], rest[n_scr:]

        @pl.when(pl.program_id(0) == 0)
        def _():
            for cp in exchange.copies(ex_in, ex_out, send_sems, recv_sems):
                cp.start()

        body(*own_in, *own_out, *own_scr)

        @pl.when(pl.program_id(0) == last)
        def _():
            for cp in exchange.copies(ex_in, ex_out, send_sems, recv_sems):
                cp.wait()

    outs = pl.pallas_call(
        hosted, name=name, grid=grid, in_specs=list(in_specs) + [ANY] * k_in, out_specs=list(out_specs) + [ANY] * k_out,
        out_shape=list(out_shape) + exchange.out_shapes,
        scratch_shapes=list(scratch_shapes) + [pltpu.SemaphoreType.DMA((exchange.n,)), pltpu.SemaphoreType.DMA((exchange.n,))],
        input_output_aliases={n_in + i: n_out + o for i, o in exchange.aliases.items()},
        compiler_params=_params("arbitrary"),
    )(*args, *exchange.operands)
    return list(outs[:n_out]), list(outs[n_out:])


def _rope(t, cos2, sin2):
    return t * cos2 + pltpu.roll(t, HEAD_DIM // 2, 1) * sin2


def _rope_bwd(d, cos2, sin2):
    return d * cos2 + pltpu.roll(d * sin2, HEAD_DIM // 2, d.ndim - 1)


def _inproj_fwd(x, gain, w_in, cos2, sin2, exchange=None):
    seq = x.shape[0]

    def body(x_ref, g_ref, w_ref, c_ref, s_ref, q_ref, k_ref, v_ref, gate_ref, u_ref):
        n, _ = _rms(x_ref[...])
        h = (n * g_ref[...]).astype(BF16)
        proj = _dot(h, w_ref[...])
        c = c_ref[...]
        s = s_ref[...]
        for hd in range(RET_HEADS):
            lo = hd * HEAD_DIM
            q_ref[:, lo:lo + HEAD_DIM] = _rope(proj[:, lo:lo + HEAD_DIM], c, s).astype(BF16)
            kh = proj[:, RET_WIDTH + lo:RET_WIDTH + lo + HEAD_DIM]
            k_ref[:, lo:lo + HEAD_DIM] = (_rope(kh, c, s) * K_SCALE).astype(BF16)
        v_ref[...] = proj[:, 2 * RET_WIDTH:3 * RET_WIDTH].astype(BF16)
        gate_ref[...] = proj[:, 3 * RET_WIDTH:4 * RET_WIDTH]
        u_ref[...] = proj[:, 4 * RET_WIDTH:]

    t = TOK_TILE
    half = lambda dt: jax.ShapeDtypeStruct((seq, RET_WIDTH), dt)
    return _call(
        body, "inproj_fwd", (seq // t,),
        [_rows(t, D_MODEL), _full((1, D_MODEL)), _full((D_MODEL, IN_COLS)), _rows(t, HEAD_DIM), _rows(t, HEAD_DIM)],
        [_rows(t, RET_WIDTH)] * 5, [half(BF16), half(BF16), half(BF16), half(F32), half(F32)], [],
        (x, gain, w_in, cos2, sin2), exchange)


def _inproj_bwd(x, gain, w_in, dpieces, dres, exchange=None):
    seq = x.shape[0]
    t = TOK_TILE
    n_pieces = len(dpieces)

    def body(x_ref, g_ref, w_ref, *refs):
        piece_refs, (dres_ref, dx_ref, dw_ref, dg_ref) = refs[:n_pieces], refs[n_pieces:]

        @pl.when(pl.program_id(0) == 0)
        def _():
            dw_ref[...] = jnp.zeros_like(dw_ref)
            dg_ref[...] = jnp.zeros_like(dg_ref)

        n, r = _rms(x_ref[...])
        gain_v = g_ref[...]
        h = (n * gain_v).astype(BF16)
        dh = None
        for j, piece in enumerate(piece_refs):
            cols = slice(j * RET_WIDTH, (j + 1) * RET_WIDTH)
            dp = piece[...]
            dw_ref[:, cols] += _dot_tn(h, dp)
            term = _dot_nt(dp, w_ref[:, cols])
            dh = term if dh is None else dh + term
        dx, dgs = _rms_bwd(n, r, gain_v, dh)
        dg_ref[...] += dgs
        dx_ref[...] = dres_ref[...] + dx

    return _call(
        body, "inproj_bwd", (seq // t,),
        [_rows(t, D_MODEL), _full((1, D_MODEL)), _full((D_MODEL, IN_COLS))] + [_rows(t, RET_WIDTH)] * n_pieces + [_rows(t, D_MODEL)],
        [_rows(t, D_MODEL), _full((D_MODEL, IN_COLS)), _full((1, D_MODEL))],
        [jax.ShapeDtypeStruct((seq, D_MODEL), F32), jax.ShapeDtypeStruct((D_MODEL, IN_COLS), F32),
         jax.ShapeDtypeStruct((1, D_MODEL), F32)],
        [], (x, gain, w_in, *dpieces, dres), exchange)


def _ret_consts():
    c = RET_CHUNK
    f32 = np.float32
    log_gamma = np.log(f32(1.0) - np.exp(np.linspace(math.log(1.0 / 32), math.log(1.0 / 512), RET_HEADS, dtype=f32))).astype(f32)
    idx = np.arange(c, dtype=f32)
    diff = idx[:, None] - idx[None, :]
    decay = np.where(diff[None] >= 0, np.exp(np.maximum(diff, f32(0.0))[None] * log_gamma[:, None, None]), f32(0.0))
    zeta = np.exp((c - 1 - idx)[None, :] * log_gamma[:, None])
    xi = np.exp((idx + f32(1.0))[None, :] * log_gamma[:, None])
    g_chunk = np.exp(f32(c) * log_gamma)
    wide = lambda rowvals: jnp.asarray(np.broadcast_to(rowvals[:, :, None], (RET_HEADS, c, c)).astype(f32))
    return (jnp.asarray(decay.astype(f32)), wide(xi), wide(zeta),
            jnp.asarray(np.broadcast_to(g_chunk[:, None, None], (RET_HEADS, c, c)).astype(f32)))


def _rope_tables(seq):
    f32 = np.float32
    half = HEAD_DIM // 2
    inv_freq = np.power(f32(ROPE_BASE), -np.arange(half, dtype=f32) / f32(half)).astype(f32)
    ang = (np.arange(seq, dtype=f32)[:, None] * inv_freq[None, :]).astype(f32)
    cos, sin = np.cos(ang).astype(f32), np.sin(ang).astype(f32)
    return jnp.asarray(np.concatenate([cos, cos], axis=1)), jnp.asarray(np.concatenate([-sin, sin], axis=1))


def _bmm(a, b):
    return lax.dot_general(a, b, (((2,), (1,)), ((0,), (0,))), preferred_element_type=F32)


def _bmm_nt(a, b):
    return lax.dot_general(a, b, (((2,), (2,)), ((0,), (0,))), preferred_element_type=F32)


def _bmm_tn(a, b):
    return lax.dot_general(a, b, (((1,), (1,)), ((0,), (0,))), preferred_element_type=F32)


def _ret_blocks(ref):
    return jnp.stack([ref[cc * RET_CHUNK:(cc + 1) * RET_CHUNK, hd * HEAD_DIM:(hd + 1) * HEAD_DIM]
                      for cc in range(RET_CHUNKS_PER_STEP) for hd in range(RET_HEADS)])


def _ret_unblock(ref, blocks):
    for cc in range(RET_CHUNKS_PER_STEP):
        for hd in range(RET_HEADS):
            ref[cc * RET_CHUNK:(cc + 1) * RET_CHUNK, hd * HEAD_DIM:(hd + 1) * HEAD_DIM] = blocks[cc * RET_HEADS + hd].astype(ref.dtype)


def _rope_blocks(ref):
    return jnp.stack([ref[cc * RET_CHUNK:(cc + 1) * RET_CHUNK, :] for cc in range(RET_CHUNKS_PER_STEP) for _ in range(RET_HEADS)])


def _per_chunk(ref):
    return jnp.concatenate([ref[...]] * RET_CHUNKS_PER_STEP, axis=0)


def _ret_gain(gain_ref):
    return jnp.stack([gain_ref[:, hd * HEAD_DIM:(hd + 1) * HEAD_DIM]
                      for _ in range(RET_CHUNKS_PER_STEP) for hd in range(RET_HEADS)])


def _ret_fwd(q, k, v, gate, gn_gain, consts, exchange=None):
    seq = q.shape[0]
    cps = RET_CHUNKS_PER_STEP
    t = cps * RET_CHUNK
    n_chunks = seq // RET_CHUNK
    dec, xi, zeta, gch = consts

    def body(q_ref, k_ref, v_ref, gate_ref, gain_ref, dec_ref, xi_ref, zeta_ref, gch_ref, y_ref, rprev_ref, state):
        @pl.when(pl.program_id(0) == 0)
        def _():
            state[...] = jnp.zeros_like(state)

        qb, kb, vb = _ret_blocks(q_ref), _ret_blocks(k_ref), _ret_blocks(v_ref)
        s = _bmm_nt(qb, kb) * _per_chunk(dec_ref)
        vz = (vb.astype(F32) * _per_chunk(zeta_ref)).astype(BF16)
        incr = _bmm_tn(kb, vz)
        gch_v = gch_ref[...]
        rp = state[...]
        before = []
        for cc in range(cps):
            before.append(rp.astype(BF16))
            rprev_ref[cc] = before[-1]
            rp = gch_v * rp + incr[cc * RET_HEADS:(cc + 1) * RET_HEADS]
        state[...] = rp
        qx = (qb.astype(F32) * _per_chunk(xi_ref)).astype(BF16)
        o = _bmm(s.astype(BF16), vb) + _bmm(qx, jnp.concatenate(before, axis=0))
        oc = o - jnp.mean(o, axis=-1, keepdims=True)
        on = oc * lax.rsqrt(jnp.mean(oc * oc, axis=-1, keepdims=True) + NORM_EPS)
        g = _ret_blocks(gate_ref)
        _ret_unblock(y_ref, g * _sigmoid(g) * (on * _ret_gain(gain_ref)))

    cst = _full((RET_HEADS, RET_CHUNK, RET_CHUNK))
    return _call(
        body, "ret_fwd", (seq // t,),
        [_rows(t, RET_WIDTH)] * 4 + [_full((1, RET_WIDTH)), cst, cst, cst, cst],
        [_rows(t, RET_WIDTH), pl.BlockSpec((cps, RET_HEADS, HEAD_DIM, HEAD_DIM), lambda i: (i, 0, 0, 0))],
        [jax.ShapeDtypeStruct((seq, RET_WIDTH), BF16), jax.ShapeDtypeStruct((n_chunks, RET_HEADS, HEAD_DIM, HEAD_DIM), BF16)],
        [pltpu.VMEM((RET_HEADS, HEAD_DIM, HEAD_DIM), F32)],
        (q, k, v, gate, gn_gain, dec, xi, zeta, gch), exchange)


def _ret_bwd(q, k, v, gate, gn_gain, consts, rprev, dy_ret, cos2, sin2, exchange=None):
    seq = q.shape[0]
    cps = RET_CHUNKS_PER_STEP
    t = cps * RET_CHUNK
    nt = seq // t
    dec, xi, zeta, gch = consts

    def body(q_ref, k_ref, v_ref, gate_ref, gain_ref, dec_ref, xi_ref, zeta_ref, gch_ref, rprev_ref, dyr_ref, cos_ref, sin_ref,
             dq_ref, dk_ref, dv_ref, dgate_ref, dgain_ref, dstate):
        @pl.when(pl.program_id(0) == 0)
        def _():
            dstate[...] = jnp.zeros_like(dstate)
            dgain_ref[...] = jnp.zeros_like(dgain_ref)

        nb = cps * RET_HEADS
        qb, kb, vb = _ret_blocks(q_ref), _ret_blocks(k_ref), _ret_blocks(v_ref)
        dec_b, xi_b, zeta_b = _per_chunk(dec_ref), _per_chunk(xi_ref), _per_chunk(zeta_ref)
        rpb = rprev_ref[...].reshape(nb, HEAD_DIM, HEAD_DIM)
        sdb = (_bmm_nt(qb, kb) * dec_b).astype(BF16)
        qx = (qb.astype(F32) * xi_b).astype(BF16)
        o = _bmm(sdb, vb) + _bmm(qx, rpb)
        oc = o - jnp.mean(o, axis=-1, keepdims=True)
        rstd = lax.rsqrt(jnp.mean(oc * oc, axis=-1, keepdims=True) + NORM_EPS)
        on = oc * rstd
        g = _ret_blocks(gate_ref)
        sg = _sigmoid(g)
        gain_b = _ret_gain(gain_ref)
        dyr = _ret_blocks(dyr_ref)
        _ret_unblock(dgate_ref, dyr * (on * gain_b) * (sg * (1.0 + g * (1.0 - sg))))
        dy = dyr * (g * sg)
        dgain = jnp.sum(dy * on, axis=1, keepdims=True)
        for hd in range(RET_HEADS):
            part = dgain[hd]
            for cc in range(1, cps):
                part = part + dgain[cc * RET_HEADS + hd]
            dgain_ref[:, hd * HEAD_DIM:(hd + 1) * HEAD_DIM] += part
        don = dy * gain_b
        do = rstd * (don - jnp.mean(don, axis=-1, keepdims=True) - on * jnp.mean(don * on, axis=-1, keepdims=True))
        dob = do.astype(BF16)
        ds = (_bmm_nt(dob, vb) * dec_b).astype(BF16)
        dq = _bmm(ds, kb) + _bmm_nt(dob, rpb) * xi_b
        dk = _bmm_tn(ds, qb)
        dv = _bmm_tn(sdb, dob)
        dstate_local = _bmm_tn(qx, dob)
        vz = (vb.astype(F32) * zeta_b).astype(BF16)
        gch_v = gch_ref[...]
        zeta_v = zeta_ref[...]
        gh = dstate[...]
        dk_extra, dv_extra = [None] * cps, [None] * cps
        for cc in reversed(range(cps)):
            sl = slice(cc * RET_HEADS, (cc + 1) * RET_HEADS)
            gb = gh.astype(BF16)
            dk_extra[cc] = _bmm_nt(vz[sl], gb)
            dv_extra[cc] = _bmm(kb[sl], gb) * zeta_v
            gh = dstate_local[sl] + gch_v * gh
        dstate[...] = gh
        cos_b, sin_b = _rope_blocks(cos_ref), _rope_blocks(sin_ref)
        _ret_unblock(dq_ref, _rope_bwd(dq, cos_b, sin_b))
        _ret_unblock(dk_ref, _rope_bwd((dk + jnp.concatenate(dk_extra, axis=0)) * K_SCALE, cos_b, sin_b))
        _ret_unblock(dv_ref, dv + jnp.concatenate(dv_extra, axis=0))

    cst = _full((RET_HEADS, RET_CHUNK, RET_CHUNK))
    rev = _rows_rev(t, RET_WIDTH, nt)
    act = jax.ShapeDtypeStruct((seq, RET_WIDTH), BF16)
    return _call(
        body, "ret_bwd", (nt,),
        [rev] * 4 + [_full((1, RET_WIDTH)), cst, cst, cst, cst,
                     pl.BlockSpec((cps, RET_HEADS, HEAD_DIM, HEAD_DIM), lambda i: (nt - 1 - i, 0, 0, 0)), rev,
                     _rows_rev(t, HEAD_DIM, nt), _rows_rev(t, HEAD_DIM, nt)],
        [rev] * 4 + [_full((1, RET_WIDTH))], [act, act, act, act, jax.ShapeDtypeStruct((1, RET_WIDTH), F32)],
        [pltpu.VMEM((RET_HEADS, HEAD_DIM, HEAD_DIM), F32)],
        (q, k, v, gate, gn_gain, dec, xi, zeta, gch, rprev, dy_ret, cos2, sin2), exchange)


MXU_TILE = 256
S5_STATE_TILES = 2 * N_STATES // MXU_TILE
S5_CHANNELS_PER_TILE = MXU_TILE // SSM_STATE * SSM_GROUP
S5_TILES_SHAPE = (S5_STATE_TILES, LANES, MXU_TILE)


def _chan_block(kk):
    return ((kk % (S5_STATE_TILES // 2)) * S5_CHANNELS_PER_TILE) // LANES * LANES


def _s5_tiles(blocks_re, blocks_im):
    per = MXU_TILE // SSM_STATE
    half = S5_STATE_TILES // 2
    eye = jnp.eye(per, dtype=F32)
    in_half = (jnp.arange(half) % (LANES // S5_CHANNELS_PER_TILE))[:, None] == jnp.arange(LANES // S5_CHANNELS_PER_TILE)[None, :]
    out = []
    for blk in (blocks_re, blocks_im):
        t = blk.reshape(half, per, SSM_GROUP, 1, SSM_STATE) * eye[None, :, None, :, None]
        t = t.reshape(half, 1, S5_CHANNELS_PER_TILE, MXU_TILE)
        out.append(jnp.where(in_half[:, :, None, None], t, 0.0).reshape(half, LANES, MXU_TILE))
    return jnp.concatenate(out, axis=0)


def _s5_discretise(lam_re, lam_im, log_dt, b_re, b_im, c_re, c_im):
    lam = lax.complex(jnp.minimum(lam_re, -1e-4), lam_im)
    dt = jnp.exp(log_dt)[:, None]
    lam_bar = jnp.exp(lam * dt)
    b_bar = ((lam_bar - 1.0) / lam)[:, :, None] * lax.complex(b_re, b_im)
    b_tiles = _s5_tiles(jnp.swapaxes(jnp.real(b_bar), 1, 2), jnp.swapaxes(jnp.imag(b_bar), 1, 2))
    return jnp.real(lam_bar), jnp.imag(lam_bar), b_tiles, _s5_tiles(c_re, -c_im)


def _s5_tables(lbr, lbi):
    lr = lbr.reshape(1, N_STATES)
    li = lbi.reshape(1, N_STATES)
    pr, pi = lr, li
    while pr.shape[0] < S5_STEPS:
        top_r, top_i = pr[-1:], pi[-1:]
        pr, pi = (jnp.concatenate([pr, pr * top_r - pi * top_i], axis=0), jnp.concatenate([pi, pr * top_i + pi * top_r], axis=0))
    bc = lambda a: jnp.broadcast_to(a, (SUBLANES, N_STATES))
    rep = lambda rows: jnp.broadcast_to(rows[:, None, :], (S5_STEPS, SUBLANES, N_STATES)).reshape(S5_STEPS * SUBLANES, N_STATES)
    return bc(lr), bc(li), rep(pr), rep(pi)


def _cmul(ar, ai, br, bi):
    return ar * br - ai * bi, ar * bi + ai * br


def _seg(i):
    return pl.ds(pl.multiple_of(i * SUBLANES, SUBLANES), SUBLANES)


def _scan_order():
    r = np.arange(S5_TILE)
    token = (r % SUBLANES) * S5_STEPS + r // SUBLANES
    p = (token[:, None] == np.arange(S5_TILE)[None, :]).astype(np.float32)
    return jnp.asarray(p, dtype=BF16), jnp.asarray(p.T, dtype=BF16)


def _to_scan_order(p_ref, rows_bf16):
    return _dot(p_ref[...], rows_bf16).astype(BF16)


def _from_scan_order(pt_ref, rows):
    hi = rows.astype(BF16)
    rest = rows - hi.astype(F32)
    mid = rest.astype(BF16)
    lo = (rest - mid.astype(F32)).astype(BF16)
    pt = pt_ref[...]
    return _dot(pt, hi) + _dot(pt, mid) + _dot(pt, lo)


def _s5_channels_to_states(chan_b, m_ref, xs):
    for kk in range(S5_STATE_TILES):
        lo = _chan_block(kk)
        res = _dot(chan_b[:, lo:lo + LANES], m_ref[kk])
        xs[2 * kk] = res[:, :LANES]
        xs[2 * kk + 1] = res[:, LANES:]


def _s5_states_to_channels(sb_ref, m_ref):
    out = []
    for lo in range(0, SSM_WIDTH, LANES):
        acc = None
        for kk in range(S5_STATE_TILES):
            if _chan_block(kk) == lo:
                term = _dot_nt(sb_ref[:, kk * MXU_TILE:(kk + 1) * MXU_TILE], m_ref[kk])
                acc = term if acc is None else acc + term
        out.append(acc)
    return jnp.concatenate(out, axis=1)


def _s5_outer_acc(acc_ref, chan_b, states_ref):
    for kk in range(S5_STATE_TILES):
        lo = _chan_block(kk)
        acc_ref[kk] += _dot_tn(chan_b[:, lo:lo + LANES], states_ref[:, kk * MXU_TILE:(kk + 1) * MXU_TILE])


def _s5_scan_tile(xs, lr_ref, li_ref, pr_ref, pi_ref, carry_re, carry_im, conj, reverse, prev_of=None):
    sgn = -1.0 if conj else 1.0
    seg_in_re, seg_in_im, sums_re, sums_im = [], [], [], []
    group = S5_GROUP
    for grp in range(S5_COLBLK // group):
        blks = [grp * group + j for j in range(group)]
        lrs = [lr_ref[:, b * LANES:(b + 1) * LANES] for b in blks]
        lis = [sgn * li_ref[:, b * LANES:(b + 1) * LANES] for b in blks]

        def step(it, carry, blks=blks, lrs=lrs, lis=lis):
            i = (S5_STEPS - 1 - it) if reverse else it
            out_r, out_i = [], []
            for j, b in enumerate(blks):
                xr, xi = _cmul(lrs[j], lis[j], carry[j], carry[group + j])
                xr = xr + xs[b, _seg(i), :]
                xi = xi + xs[S5_COLBLK + b, _seg(i), :]
                xs[b, _seg(i), :] = xr
                xs[S5_COLBLK + b, _seg(i), :] = xi
                out_r.append(xr)
                out_i.append(xi)
            return tuple(out_r + out_i)

        zeros = tuple(jnp.zeros((SUBLANES, LANES), F32) for _ in range(2 * group))
        ends = lax.fori_loop(0, S5_STEPS, step, zeros, unroll=True)

        ins_r, ins_i = [], []
        for j, b in enumerate(blks):
            cols = slice(b * LANES, (b + 1) * LANES)
            pr = pr_ref[SUBLANES * S5_STEPS - 1:SUBLANES * S5_STEPS, cols]
            pi = sgn * pi_ref[SUBLANES * S5_STEPS - 1:SUBLANES * S5_STEPS, cols]
            cur_r = carry_re[:, cols]
            cur_i = carry_im[:, cols]
            row_id = lax.broadcasted_iota(jnp.int32, (SUBLANES, LANES), 0)
            in_r = jnp.zeros((SUBLANES, LANES), F32)
            in_i = jnp.zeros((SUBLANES, LANES), F32)
            order = range(SUBLANES - 1, -1, -1) if reverse else range(SUBLANES)
            for sgm in order:
                in_r = jnp.where(row_id == sgm, cur_r, in_r)
                in_i = jnp.where(row_id == sgm, cur_i, in_i)
                mr, mi = _cmul(pr, pi, cur_r, cur_i)
                cur_r = mr + ends[j][sgm:sgm + 1, :]
                cur_i = mi + ends[group + j][sgm:sgm + 1, :]
            carry_re[:, cols] = cur_r
            carry_im[:, cols] = cur_i
            ins_r.append(in_r)
            ins_i.append(in_i)

        def fix(it, c, blks=blks, ins_r=ins_r, ins_i=ins_i):
            pw = (S5_STEPS - 1 - it) if reverse else it
            out = []
            for j, b in enumerate(blks):
                cols = slice(b * LANES, (b + 1) * LANES)
                prow = pl.ds(pl.multiple_of(pw * SUBLANES, SUBLANES), SUBLANES)
                fr, fi = _cmul(pr_ref[prow, cols], sgn * pi_ref[prow, cols], ins_r[j], ins_i[j])
                vr = xs[b, _seg(it), :] + fr
                vi = xs[S5_COLBLK + b, _seg(it), :] + fi
                xs[b, _seg(it), :] = vr
                xs[S5_COLBLK + b, _seg(it), :] = vi
                if prev_of is not None:
                    opr, opi, sr, si = c[4 * j:4 * j + 4]
                    out += [prev_of[0][b, _seg(it), :], prev_of[0][S5_COLBLK + b, _seg(it), :],
                            sr + vr * opr + vi * opi, si + vi * opr - vr * opi]
            return tuple(out)

        init = ()
        if prev_of is not None:
            zero = jnp.zeros((SUBLANES, LANES), F32)
            for b in blks:
                init += (prev_of[1][b], prev_of[2][b], zero, zero)
        done = lax.fori_loop(0, S5_STEPS, fix, init, unroll=True)
        for j in range(len(blks) if prev_of is not None else 0):
            sums_re.append(done[4 * j + 2])
            sums_im.append(done[4 * j + 3])
        seg_in_re += ins_r
        seg_in_im += ins_i
    return seg_in_re, seg_in_im, sums_re, sums_im


def _s5_pack(xs, dst):
    for b in range(2 * S5_COLBLK):
        dst[:, b * LANES:(b + 1) * LANES] = xs[b].astype(BF16)


def _s5_fwd(u, b_blk, c_blk, tables, d_skip, exchange=None):
    seq = u.shape[0]
    t = S5_TILE
    nt = seq // t
    lr8, li8, p_re, p_im = tables

    p, pt = _scan_order()

    def body(u_ref, p_ref, pt_ref, b_ref, c_ref, lr_ref, li_ref, pr_ref, pi_ref, d_ref, y_ref, start_ref,
             xs, xb, carry_re, carry_im):
        @pl.when(pl.program_id(0) == 0)
        def _():
            carry_re[...] = jnp.zeros_like(carry_re)
            carry_im[...] = jnp.zeros_like(carry_im)

        start_ref[0, 0:1, :] = carry_re[...]
        start_ref[0, 1:2, :] = carry_im[...]
        uv = u_ref[...]
        _s5_channels_to_states(_to_scan_order(p_ref, uv.astype(BF16)), b_ref, xs)
        _s5_scan_tile(xs, lr_ref, li_ref, pr_ref, pi_ref, carry_re, carry_im, conj=False, reverse=False)
        _s5_pack(xs, xb)
        y_ref[...] = _from_scan_order(pt_ref, _s5_states_to_channels(xb, c_ref)) + d_ref[...] * uv

    return _call(
        body, "s5_fwd", (nt,),
        [_rows(t, SSM_WIDTH), _full((t, t)), _full((t, t)), _full(S5_TILES_SHAPE), _full(S5_TILES_SHAPE),
         _full((SUBLANES, N_STATES)), _full((SUBLANES, N_STATES)), _full((SUBLANES * S5_STEPS, N_STATES)),
         _full((SUBLANES * S5_STEPS, N_STATES)), _full((1, SSM_WIDTH))],
        [_rows(t, SSM_WIDTH), pl.BlockSpec((1, 2, N_STATES), lambda i: (i, 0, 0))],
        [jax.ShapeDtypeStruct((seq, SSM_WIDTH), F32), jax.ShapeDtypeStruct((nt, 2, N_STATES), F32)],
        [pltpu.VMEM((2 * S5_COLBLK, t, LANES), F32), pltpu.VMEM((t, 2 * N_STATES), BF16),
         pltpu.VMEM((1, N_STATES), F32), pltpu.VMEM((1, N_STATES), F32)],
        (u, p, pt, b_blk, c_blk, lr8, li8, p_re, p_im, d_skip), exchange)


def _s5_bwd(u, dy, b_blk, c_blk, tables, d_skip, starts, exchange=None):
    seq = u.shape[0]
    t = S5_TILE
    nt = seq // t
    lr8, li8, p_re, p_im = tables
    p, pt = _scan_order()

    def body(u_ref, dy_ref, p_ref, pt_ref, b_ref, c_ref, lr_ref, li_ref, pr_ref, pi_ref, d_ref, start_ref,
             du_ref, dd_ref, dlam_ref, db_ref, dc_ref,
             xs, as_, xb, ab, carry_re, carry_im, acar_re, acar_im):
        @pl.when(pl.program_id(0) == 0)
        def _():
            acar_re[...] = jnp.zeros_like(acar_re)
            acar_im[...] = jnp.zeros_like(acar_im)
            dd_ref[...] = jnp.zeros_like(dd_ref)
            dlam_ref[...] = jnp.zeros_like(dlam_ref)
            db_ref[...] = jnp.zeros_like(db_ref)
            dc_ref[...] = jnp.zeros_like(dc_ref)

        uv = u_ref[...]
        dyv = dy_ref[...]
        ub = _to_scan_order(p_ref, uv.astype(BF16))
        dyb = _to_scan_order(p_ref, dyv.astype(BF16))
        carry_re[...] = start_ref[0, 0:1, :]
        carry_im[...] = start_ref[0, 1:2, :]
        _s5_channels_to_states(ub, b_ref, xs)
        in_re, in_im, _, _ = _s5_scan_tile(xs, lr_ref, li_ref, pr_ref, pi_ref, carry_re, carry_im, conj=False, reverse=False)
        _s5_channels_to_states(dyb, c_ref, as_)
        _, _, dl_re, dl_im = _s5_scan_tile(as_, lr_ref, li_ref, pr_ref, pi_ref, acar_re, acar_im, conj=True, reverse=True,
                                           prev_of=(xs, in_re, in_im))
        for b in range(S5_COLBLK):
            cols = slice(b * LANES, (b + 1) * LANES)
            dlam_ref[0, :, cols] += dl_re[b]
            dlam_ref[1, :, cols] += dl_im[b]
        _s5_pack(xs, xb)
        _s5_pack(as_, ab)
        du_ref[...] = (_from_scan_order(pt_ref, _s5_states_to_channels(ab, b_ref)) + d_ref[...] * dyv).astype(BF16)
        dd_ref[...] += jnp.sum(dyv * uv, axis=0, keepdims=True)
        _s5_outer_acc(db_ref, ub, ab)
        _s5_outer_acc(dc_ref, dyb, xb)

    rev = _rows_rev(t, SSM_WIDTH, nt)
    vec = lambda: pltpu.VMEM((1, N_STATES), F32)
    outer = S5_TILES_SHAPE
    return _call(
        body, "s5_bwd", (nt,),
        [rev, rev, _full((t, t)), _full((t, t)), _full(S5_TILES_SHAPE), _full(S5_TILES_SHAPE),
         _full((SUBLANES, N_STATES)), _full((SUBLANES, N_STATES)), _full((SUBLANES * S5_STEPS, N_STATES)),
         _full((SUBLANES * S5_STEPS, N_STATES)), _full((1, SSM_WIDTH)),
         pl.BlockSpec((1, 2, N_STATES), lambda i: (nt - 1 - i, 0, 0))],
        [rev, _full((1, SSM_WIDTH)), _full((2, SUBLANES, N_STATES)), _full(outer), _full(outer)],
        [jax.ShapeDtypeStruct((seq, SSM_WIDTH), BF16), jax.ShapeDtypeStruct((1, SSM_WIDTH), F32),
         jax.ShapeDtypeStruct((2, SUBLANES, N_STATES), F32), jax.ShapeDtypeStruct(outer, F32), jax.ShapeDtypeStruct(outer, F32)],
        [pltpu.VMEM((2 * S5_COLBLK, t, LANES), F32), pltpu.VMEM((2 * S5_COLBLK, t, LANES), F32),
         pltpu.VMEM((t, 2 * N_STATES), BF16), pltpu.VMEM((t, 2 * N_STATES), BF16), vec(), vec(), vec(), vec()],
        (u, dy, p, pt, b_blk, c_blk, lr8, li8, p_re, p_im, d_skip, starts), exchange)


def _mix_fwd(y_ssm, y_ret, x, gain, w_glu, w_out, exchange=None):
    seq = x.shape[0]
    t = TOK_TILE

    def body(ys_ref, yr_ref, x_ref, g_ref, wg_ref, wo_ref, x1_ref, mix_ref):
        g0 = _gelu(ys_ref[...]).astype(BF16)
        z = _dot(g0, wg_ref[...])
        glu = (z[:, :SSM_WIDTH] * _sigmoid(z[:, SSM_WIDTH:])).astype(BF16)
        mix = _dot(yr_ref[...], wo_ref[:RET_WIDTH, :]) + _dot(glu, wo_ref[RET_WIDTH:, :])
        mix_ref[...] = mix
        n, _ = _rms(mix)
        x1_ref[...] = x_ref[...] + n * g_ref[...]

    act = jax.ShapeDtypeStruct((seq, D_MODEL), F32)
    return _call(
        body, "mix_fwd", (seq // t,),
        [_rows(t, SSM_WIDTH), _rows(t, RET_WIDTH), _rows(t, D_MODEL), _full((1, D_MODEL)),
         _full((SSM_WIDTH, 2 * SSM_WIDTH)), _full((D_MODEL, D_MODEL))],
        [_rows(t, D_MODEL)] * 2, [act, act], [], (y_ssm, y_ret, x, gain, w_glu, w_out), exchange)


def _mix_bwd(dx1, mix, gain, y_ssm, y_ret, w_glu, w_out, exchange=None):
    seq = dx1.shape[0]
    t = TOK_TILE

    def body(dx1_ref, mix_ref, g_ref, ys_ref, yr_ref, wg_ref, wo_ref, dyr_ref, dys_ref, dwo_ref, dwg_ref, dg_ref):
        @pl.when(pl.program_id(0) == 0)
        def _():
            dwo_ref[...] = jnp.zeros_like(dwo_ref)
            dwg_ref[...] = jnp.zeros_like(dwg_ref)
            dg_ref[...] = jnp.zeros_like(dg_ref)

        n, r = _rms(mix_ref[...])
        dmix, dgs = _rms_bwd(n, r, g_ref[...], dx1_ref[...])
        dg_ref[...] += dgs
        dmb = dmix.astype(BF16)
        dcat = _dot_nt(dmb, wo_ref[...])
        dyr_ref[...] = dcat[:, :RET_WIDTH]
        dglu = dcat[:, RET_WIDTH:]
        ys = ys_ref[...]
        g0 = _gelu(ys).astype(BF16)
        z = _dot(g0, wg_ref[...])
        a = z[:, :SSM_WIDTH]
        sb = _sigmoid(z[:, SSM_WIDTH:])
        dwo_ref[:RET_WIDTH, :] += _dot_tn(yr_ref[...], dmb)
        dwo_ref[RET_WIDTH:, :] += _dot_tn((a * sb).astype(BF16), dmb)
        dz = jnp.concatenate([dglu * sb, dglu * a * sb * (1.0 - sb)], axis=1).astype(BF16)
        dwg_ref[...] += _dot_tn(g0, dz)
        dys_ref[...] = _dot_nt(dz, wg_ref[...]) * _gelu_grad(ys)

    half = jax.ShapeDtypeStruct((seq, RET_WIDTH), F32)
    return _call(
        body, "mix_bwd", (seq // t,),
        [_rows(t, D_MODEL), _rows(t, D_MODEL), _full((1, D_MODEL)), _rows(t, SSM_WIDTH), _rows(t, RET_WIDTH),
         _full((SSM_WIDTH, 2 * SSM_WIDTH)), _full((D_MODEL, D_MODEL))],
        [_rows(t, RET_WIDTH), _rows(t, SSM_WIDTH), _full((D_MODEL, D_MODEL)), _full((SSM_WIDTH, 2 * SSM_WIDTH)),
         _full((1, D_MODEL))],
        [half, half, jax.ShapeDtypeStruct((D_MODEL, D_MODEL), F32), jax.ShapeDtypeStruct((SSM_WIDTH, 2 * SSM_WIDTH), F32),
         jax.ShapeDtypeStruct((1, D_MODEL), F32)],
        [], (dx1, mix, gain, y_ssm, y_ret, w_glu, w_out), exchange)


def _mlp_a(x1, target, gain_pre, gain_post, w1, w2):
    seq = x1.shape[0]
    t = MLP_TILE

    def body(x1_ref, tg_ref, gp_ref, gq_ref, w1_ref, w2_ref, df_ref, dx2_ref, dw2_ref, dgq_ref, sq_ref):
        @pl.when(pl.program_id(0) == 0)
        def _():
            dw2_ref[...] = jnp.zeros_like(dw2_ref)
            dgq_ref[...] = jnp.zeros_like(dgq_ref)
            sq_ref[...] = jnp.zeros_like(sq_ref)

        x1v = x1_ref[...]
        n3, _ = _rms(x1v)
        h = (n3 * gp_ref[...]).astype(BF16)
        rl = jnp.maximum(_dot(h, w1_ref[...]), 0.0)
        act = (rl * rl).astype(BF16)
        n4, r4 = _rms(_dot(act, w2_ref[...]))
        gq = gq_ref[...]
        err = x1v + n4 * gq - tg_ref[...]
        sq_ref[...] += jnp.sum(err * err, axis=0, keepdims=True)
        dx2 = err * (1.0 / D_MODEL)
        dx2_ref[...] = dx2
        dm, dgs = _rms_bwd(n4, r4, gq, dx2)
        dgq_ref[...] += dgs
        dmb = dm.astype(BF16)
        dw2_ref[...] += _dot_tn(act, dmb)
        df_ref[...] = (_dot_nt(dmb, w2_ref[...]) * (2.0 * rl)).astype(BF16)

    return pl.pallas_call(
        body, name="mlp_a", grid=(seq // t,),
        in_specs=[_rows(t, D_MODEL), _rows(t, D_MODEL), _full((1, D_MODEL)), _full((1, D_MODEL)),
                  _full((D_MODEL, D_FF)), _full((D_FF, D_MODEL))],
        out_specs=[_rows(t, D_FF), _rows(t, D_MODEL), _full((D_FF, D_MODEL)), _full((1, D_MODEL)), _full((1, D_MODEL))],
        out_shape=[jax.ShapeDtypeStruct((seq, D_FF), BF16), jax.ShapeDtypeStruct((seq, D_MODEL), F32),
                   jax.ShapeDtypeStruct((D_FF, D_MODEL), F32), jax.ShapeDtypeStruct((1, D_MODEL), F32),
                   jax.ShapeDtypeStruct((1, D_MODEL), F32)],
        compiler_params=_params("arbitrary"),
    )(x1, target, gain_pre, gain_post, w1, w2)


def _mlp_b(df, dx2, x1, gain_pre, w1):
    seq = x1.shape[0]
    t = TOK_TILE

    def body(df_ref, dx2_ref, x1_ref, gp_ref, w1_ref, dx1_ref, dw1_ref, dgp_ref):
        @pl.when(pl.program_id(0) == 0)
        def _():
            dw1_ref[...] = jnp.zeros_like(dw1_ref)
            dgp_ref[...] = jnp.zeros_like(dgp_ref)

        n3, r3 = _rms(x1_ref[...])
        gp = gp_ref[...]
        h = (n3 * gp).astype(BF16)
        dfv = df_ref[...]
        dw1_ref[...] += _dot_tn(h, dfv)
        dx, dgs = _rms_bwd(n3, r3, gp, _dot_nt(dfv, w1_ref[...]))
        dgp_ref[...] += dgs
        dx1_ref[...] = dx2_ref[...] + dx

    return pl.pallas_call(
        body, name="mlp_b", grid=(seq // t,),
        in_specs=[_rows(t, D_FF), _rows(t, D_MODEL), _rows(t, D_MODEL), _full((1, D_MODEL)), _full((D_MODEL, D_FF))],
        out_specs=[_rows(t, D_MODEL), _full((D_MODEL, D_FF)), _full((1, D_MODEL))],
        out_shape=[jax.ShapeDtypeStruct((seq, D_MODEL), F32), jax.ShapeDtypeStruct((D_MODEL, D_FF), F32),
                   jax.ShapeDtypeStruct((1, D_MODEL), F32)],
        compiler_params=_params("arbitrary"),
    )(df, dx2, x1, gain_pre, w1)


def _local_step(x, target, small, weights, ids=None):
    cos2, sin2 = _rope_tables(x.shape[0])
    ret_consts = _ret_consts()

    s5_names = ("ssm_lambda_re", "ssm_lambda_im", "ssm_log_dt", "ssm_b_re", "ssm_b_im", "ssm_c_re", "ssm_c_im")
    (lbr, lbi, b_tiles, c_tiles), disc_vjp = jax.vjp(_s5_discretise, *[small[name] for name in s5_names])
    tables = _s5_tables(lbr, lbi)
    b_blk = b_tiles.astype(BF16)
    c_blk = c_tiles.astype(BF16)
    d_skip = small["ssm_d"].reshape(1, SSM_WIDTH)
    gn_gain = small["ret_gn_gain"].reshape(1, RET_WIDTH)
    g_mix_pre = small["norm_mix_pre"].reshape(1, D_MODEL)
    g_mix_post = small["norm_mix_post"].reshape(1, D_MODEL)
    g_mlp_pre = small["norm_mlp_pre"].reshape(1, D_MODEL)
    g_mlp_post = small["norm_mlp_post"].reshape(1, D_MODEL)

    dist = ids is not None
    w_in, w_glu, w_out, w_ff1, w_ff2 = weights
    (q, k, v, gate, u), got = _inproj_fwd(x, g_mix_pre, w_in, cos2, sin2, _x_gather_send((1, 2), [w_glu, w_out]) if dist else None)
    if dist:
        w_glu, w_out = got
    (y_ssm, starts), got = _s5_fwd(u, b_blk, c_blk, tables, d_skip,
                                   _x_join(_x_gather_send((3,), [w_ff1]), _x_gather_send((4,), [w_ff2], (0, 2))) if dist else None)
    if dist:
        w_ff1, w_ff2 = got
    (y_ret, rprev), got = _ret_fwd(q, k, v, gate, gn_gain, ret_consts,
                                   _x_join(_x_gather_pass((1, 2, 3), [w_glu, w_out, w_ff1]),
                                           _x_gather_send((4,), [w_ff2], (1, 2))) if dist else None)
    if dist:
        w_glu, w_out, w_ff1, w_ff2 = got
    (x1, mix), got = _mix_fwd(y_ssm, y_ret, x, g_mix_post, w_glu, w_out, _x_gather_pass((4,), [w_ff2]) if dist else None)
    if dist:
        w_ff2, = got
    df, dx2, d_ff2, dg_mlp_post, sq = _mlp_a(x1, target, g_mlp_pre, g_mlp_post, w_ff1, w_ff2)
    dx1, d_ff1, dg_mlp_pre = _mlp_b(df, dx2, x1, g_mlp_pre, w_ff1)
    (dy_ret, dy_ssm, d_out, d_glu, dg_mix_post), got = _mix_bwd(dx1, mix, g_mix_post, y_ssm, y_ret, w_glu, w_out,
                                                               _x_pair(LATE, [d_ff1, d_ff2]) if dist else None)
    if dist:
        sums, sums_bf16 = _pair_sum(ids, LATE, [d_ff1, d_ff2], got)
    (du, dd, dlam8, db_tiles, dc_tiles), got = _s5_bwd(
        u, dy_ssm, b_blk, c_blk, tables, d_skip, starts,
        _x_join(_x_chip(LATE, sums_bf16), _x_pair(MID, [d_glu, d_out])) if dist else None)
    if dist:
        parts = _chip_sum(ids, LATE, sums, got[:len(LATE)])
        sums, sums_bf16 = _pair_sum(ids, MID, [d_glu, d_out], got[len(LATE):])
    (dq, dk, dv, dgate, dgn), got = _ret_bwd(q, k, v, gate, gn_gain, ret_consts, rprev, dy_ret, cos2, sin2,
                                             _x_join(_x_share(LATE, parts), _x_chip(MID, sums_bf16)) if dist else None)
    if dist:
        late_grads = got[:len(LATE)]
        parts = _chip_sum(ids, MID, sums, got[len(LATE):])
        mid_parts = parts
    (dx, d_in, dg_mix_pre), _ = _inproj_bwd(x, g_mix_pre, w_in, [dq, dk, dv, dgate, du], dx1)

    dlam = jnp.sum(dlam8, axis=1)
    s5_grads = disc_vjp((dlam[0].reshape(SSM_GROUPS, SSM_STATE), dlam[1].reshape(SSM_GROUPS, SSM_STATE), db_tiles, dc_tiles))
    small_grads = {
        "norm_mix_pre": dg_mix_pre, "norm_mix_post": dg_mix_post, "ret_gn_gain": dgn, "ssm_d": dd,
        "norm_mlp_pre": dg_mlp_pre, "norm_mlp_post": dg_mlp_post, "loss_sum": 0.5 * jnp.sum(sq) / D_MODEL,
    }
    small_grads.update(dict(zip(s5_names, s5_grads)))
    if dist:
        return dx, d_in, mid_parts, late_grads, small_grads
    return dx, (d_in, d_glu, d_out, d_ff1, d_ff2), small_grads


BIG = (
    ("w_in", D_MODEL, IN_COLS, 1),
    ("w_glu", SSM_WIDTH, 2 * SSM_WIDTH, 1),
    ("w_out", D_MODEL, D_MODEL, 0),
    ("w_ff1", D_MODEL, D_FF, 1),
    ("w_ff2", D_FF, D_MODEL, 0),
)
SMALL = (
    ("norm_mix_pre", (D_MODEL,)), ("norm_mix_post", (D_MODEL,)), ("ret_gn_gain", (RET_WIDTH,)),
    ("ssm_lambda_re", (SSM_GROUPS, SSM_STATE)), ("ssm_lambda_im", (SSM_GROUPS, SSM_STATE)), ("ssm_log_dt", (SSM_GROUPS,)),
    ("ssm_b_re", (SSM_GROUPS, SSM_STATE, SSM_GROUP)), ("ssm_b_im", (SSM_GROUPS, SSM_STATE, SSM_GROUP)),
    ("ssm_c_re", (SSM_GROUPS, SSM_GROUP, SSM_STATE)), ("ssm_c_im", (SSM_GROUPS, SSM_GROUP, SSM_STATE)),
    ("ssm_d", (SSM_WIDTH,)), ("norm_mlp_pre", (D_MODEL,)), ("norm_mlp_post", (D_MODEL,)),
)
SMALL_ROWS = 1152
EXCH_TILES = 8


PACKED = SMALL + (("loss_sum", (1,)),)


def _packed_rows(shape):
    return -(-math.prod(shape) // (SUBLANES * LANES)) * SUBLANES


def _pack_small(tree):
    rows = []
    for name, shape in PACKED:
        size = math.prod(shape)
        flat = tree[name].reshape(-1).astype(F32) if name in tree else jnp.zeros((size,), F32)
        rows.append(jnp.pad(flat, (0, _packed_rows(shape) * LANES - size)).reshape(-1, LANES))
    used = sum(r.shape[0] for r in rows)
    rows.append(jnp.zeros((SMALL_ROWS - used, LANES), F32))
    return jnp.concatenate(rows, axis=0)


def _unpack_small(packed):
    out, row = {}, 0
    for name, shape in PACKED:
        size = math.prod(shape)
        out[name] = packed[row:row + _packed_rows(shape)].reshape(-1)[:size].reshape((1,) + shape)
        row += _packed_rows(shape)
    return out


def _half_shape(r, c, axis):
    return (r // 2, c) if axis == 1 else (r, c // 2)


def _region_shape(r, c, axis):
    return (r // 2, c // N_CHIPS) if axis == 1 else (r // N_CHIPS, c // 2)


def _shard_shape(r, c, axis):
    return (r, c // N_CHIPS) if axis == 1 else (r // N_CHIPS, c)


def _ds(start, size):
    return pl.ds(pl.multiple_of(start * size, size), size)


def _region_of_full(ref, r, c, axis, shard, half):
    if axis == 1:
        return ref.at[_ds(half, r // 2), _ds(shard, c // N_CHIPS)]
    return ref.at[_ds(shard, r // N_CHIPS), _ds(half, c // 2)]


def _half_of_full(ref, r, c, axis, half):
    if axis == 1:
        return ref.at[_ds(half, r // 2), :]
    return ref.at[:, _ds(half, c // 2)]


def _half_of_shard(ref, r, c, axis, half):
    if axis == 1:
        return ref.at[_ds(half, r // 2), :]
    return ref.at[:, _ds(half, c // 2)]


def _region_of_half(ref, r, c, axis, shard):
    if axis == 1:
        return ref.at[:, _ds(shard, c // N_CHIPS)]
    return ref.at[_ds(shard, r // N_CHIPS), :]


def _place():
    x, y, c = lax.axis_index("x"), lax.axis_index("y"), lax.axis_index("c")
    chips = [(1 - x, y), (x, 1 - y), (1 - x, 1 - y)]
    return x, y, c, chips


LATE = (3, 4)
MID = (1, 2)
FIRST = (0,)
SMALL_HALF = (SMALL_ROWS // 2, LANES)


def _remote(src, dst, send_sem, recv_sem, to):
    return pltpu.make_async_remote_copy(src_ref=src, dst_ref=dst, send_sem=send_sem, recv_sem=recv_sem,
                                        device_id=to, device_id_type=MESH)


def _same(arrays):
    return [jax.ShapeDtypeStruct(a.shape, a.dtype) for a in arrays]


def _x_gather_send(ws, fulls, part=(0, 1)):
    def copies(ins, outs, send_sems, recv_sems):
        x, y, c, chips = _place()
        out = []
        for j, w in enumerate(ws):
            _, r, cc, axis = BIG[w]
            mine = _region_of_full(outs[j], r, cc, axis, 2 * x + y, c)
            rows = _region_shape(r, cc, axis)[0] // part[1]
            mine = mine.at[pl.ds(part[0] * rows, rows), :]
            out += [_remote(mine, mine, send_sems.at[3 * j + k], recv_sems.at[3 * j + k], (cx, cy, c))
                    for k, (cx, cy) in enumerate(chips)]
        return out

    return _Exchange(fulls, _same(fulls), {j: j for j in range(len(ws))}, 3 * len(ws), copies)


def _x_gather_pass(ws, fulls):
    def copies(ins, outs, send_sems, recv_sems):
        x, y, c, chips = _place()
        out = []
        for j, w in enumerate(ws):
            _, r, cc, axis = BIG[w]
            for k, (cx, cy) in enumerate(chips):
                landed = _region_of_full(outs[j], r, cc, axis, 2 * cx + cy, c)
                out.append(_remote(landed, landed, send_sems.at[3 * j + k], recv_sems.at[3 * j + k], (x, y, 1 - c)))
        return out

    return _Exchange(fulls, _same(fulls), {j: j for j in range(len(ws))}, 3 * len(ws), copies)


def _x_pair(ws, grads, small=None):
    def copies(ins, outs, send_sems, recv_sems):
        x, y, c, _ = _place()
        out = []
        for j, w in enumerate(ws):
            _, r, cc, axis = BIG[w]
            out.append(_remote(_half_of_full(ins[j], r, cc, axis, 1 - c), outs[j], send_sems.at[j], recv_sems.at[j], (x, y, 1 - c)))
        if small is not None:
            j = len(ws)
            out.append(_remote(ins[j].at[_ds(1 - c, SMALL_ROWS // 2), :], outs[j], send_sems.at[j], recv_sems.at[j], (x, y, 1 - c)))
        return out

    shapes = [jax.ShapeDtypeStruct(_half_shape(*BIG[w][1:]), F32) for w in ws]
    extra = [] if small is None else [small]
    return _Exchange(list(grads) + extra, shapes + [jax.ShapeDtypeStruct(SMALL_HALF, F32)] * len(extra), {},
                     len(ws) + len(extra), copies)


def _x_chip(ws, sums_bf16, small_sum=None):
    def copies(ins, outs, send_sems, recv_sems):
        x, y, c, chips = _place()
        out = []
        for j, w in enumerate(ws):
            _, r, cc, axis = BIG[w]
            out += [_remote(_region_of_half(ins[j], r, cc, axis, 2 * cx + cy), outs[j].at[k],
                            send_sems.at[3 * j + k], recv_sems.at[3 * j + k], (cx, cy, c)) for k, (cx, cy) in enumerate(chips)]
        if small_sum is not None:
            j = len(ws)
            out += [_remote(ins[j], outs[j].at[k], send_sems.at[3 * j + k], recv_sems.at[3 * j + k], (cx, cy, c))
                    for k, (cx, cy) in enumerate(chips)]
        return out

    shapes = [jax.ShapeDtypeStruct((3,) + _region_shape(*BIG[w][1:]), BF16) for w in ws]
    extra = [] if small_sum is None else [small_sum]
    return _Exchange(list(sums_bf16) + extra, shapes + [jax.ShapeDtypeStruct((3,) + SMALL_HALF, F32)] * len(extra), {},
                     3 * (len(ws) + len(extra)), copies)


def _x_share(ws, shards, small=None):
    def copies(ins, outs, send_sems, recv_sems):
        x, y, c, _ = _place()
        out = []
        for j, w in enumerate(ws):
            _, r, cc, axis = BIG[w]
            mine = _half_of_shard(outs[j], r, cc, axis, c)
            out.append(_remote(mine, mine, send_sems.at[j], recv_sems.at[j], (x, y, 1 - c)))
        if small is not None:
            j = len(ws)
            mine = outs[j].at[_ds(c, SMALL_ROWS // 2), :]
            out.append(_remote(mine, mine, send_sems.at[j], recv_sems.at[j], (x, y, 1 - c)))
        return out

    arrays = list(shards) + ([] if small is None else [small])
    return _Exchange(arrays, _same(arrays), {j: j for j in range(len(arrays))}, len(arrays), copies)


class _Offset:
    def __init__(self, sems, base):
        self._sems, self._base = sems, base

    @property
    def at(self):
        return self

    def __getitem__(self, i):
        return self._sems.at[self._base + i]


def _x_join(a, b):
    ka_in, ka_out = len(a.operands), len(a.out_shapes)

    def copies(ins, outs, send_sems, recv_sems):
        return (a.copies(ins[:ka_in], outs[:ka_out], send_sems, recv_sems)
                + b.copies(ins[ka_in:], outs[ka_out:], _Offset(send_sems, a.n), _Offset(recv_sems, a.n)))

    aliases = dict(a.aliases)
    aliases.update({ka_in + i: ka_out + o for i, o in b.aliases.items()})
    return _Exchange(a.operands + b.operands, a.out_shapes + b.out_shapes, aliases, a.n + b.n, copies)


def _run_exchange(name, exchange):
    k_in, k_out = len(exchange.operands), len(exchange.out_shapes)

    def body(*refs):
        copies = exchange.copies(refs[:k_in], refs[k_in:k_in + k_out], refs[-2], refs[-1])
        for cp in copies:
            cp.start()
        for cp in copies:
            cp.wait()

    return list(pl.pallas_call(
        body, name=name, in_specs=[ANY] * k_in, out_specs=[ANY] * k_out, out_shape=exchange.out_shapes,
        scratch_shapes=[pltpu.SemaphoreType.DMA((exchange.n,)), pltpu.SemaphoreType.DMA((exchange.n,))],
        input_output_aliases=dict(exchange.aliases),
    )(*exchange.operands))


def _tile_rows(shape):
    return (shape[0] // EXCH_TILES, shape[1])


def _cast_place(ids, shards):
    def body(ids_ref, *refs):
        for j in range(len(BIG)):
            refs[len(BIG) + j][...] = refs[j][...].astype(BF16)

    def out_spec(w):
        _, r, cc, axis = BIG[w]
        tile = _tile_rows(_shard_shape(r, cc, axis))
        if axis == 1:
            return pl.BlockSpec(tile, lambda i, ids: (i, ids[1]))
        return pl.BlockSpec(tile, lambda i, ids: (ids[1] * EXCH_TILES + i, 0))

    return list(pl.pallas_call(
        body, name="cast_place",
        grid_spec=pltpu.PrefetchScalarGridSpec(
            num_scalar_prefetch=1, grid=(EXCH_TILES,),
            in_specs=[pl.BlockSpec(_tile_rows(_shard_shape(r, cc, axis)), lambda i, ids: (i, 0)) for _, r, cc, axis in BIG],
            out_specs=[out_spec(w) for w in range(len(BIG))]),
        out_shape=[jax.ShapeDtypeStruct((r, cc), BF16) for _, r, cc, _ in BIG],
        compiler_params=pltpu.CompilerParams(dimension_semantics=("parallel",)),
    )(ids, *shards))


def _pair_sum(ids, ws, grads, got, small=None):
    nb = len(ws)
    n = nb + (small is not None)
    halves = [_half_shape(*BIG[w][1:]) for w in ws] + [SMALL_HALF] * (n - nb)

    def body(ids_ref, *refs):
        for j in range(n):
            total = refs[j][...] + refs[n + j][...]
            refs[2 * n + j][...] = total
            if j < nb:
                refs[3 * n + j][...] = total.astype(BF16)

    def mine_spec(j):
        tile = _tile_rows(halves[j])
        if j < nb and BIG[ws[j]][3] == 0:
            return pl.BlockSpec(tile, lambda i, ids: (i, ids[0]))
        return pl.BlockSpec(tile, lambda i, ids: (ids[0] * EXCH_TILES + i, 0))

    plain = lambda j: pl.BlockSpec(_tile_rows(halves[j]), lambda i, ids: (i, 0))
    outs = pl.pallas_call(
        body, name="pair_sum",
        grid_spec=pltpu.PrefetchScalarGridSpec(
            num_scalar_prefetch=1, grid=(EXCH_TILES,),
            in_specs=[mine_spec(j) for j in range(n)] + [plain(j) for j in range(n)],
            out_specs=[plain(j) for j in range(n)] + [plain(j) for j in range(nb)]),
        out_shape=[jax.ShapeDtypeStruct(h, F32) for h in halves] + [jax.ShapeDtypeStruct(h, BF16) for h in halves[:nb]],
        compiler_params=pltpu.CompilerParams(dimension_semantics=("parallel",)),
    )(ids, *grads, *([] if small is None else [small]), *got)
    return list(outs[:n]), list(outs[n:])


def _chip_sum(ids, ws, sums, arrived):
    nb, n = len(ws), len(sums)
    regions = [_region_shape(*BIG[w][1:]) for w in ws] + [SMALL_HALF] * (n - nb)

    def body(ids_ref, *refs):
        for j in range(n):
            own, got, out = refs[j], refs[n + j], refs[2 * n + j]
            if j < nb:
                out[...] = ((own[...] + got[0].astype(F32)) + got[1].astype(F32)) + got[2].astype(F32)
            else:
                out[...] = (own[...] + got[1]) + (got[0] + got[2])

    def own_spec(j):
        tile = _tile_rows(regions[j])
        if j >= nb:
            return pl.BlockSpec(tile, lambda i, ids: (i, 0))
        if BIG[ws[j]][3] == 1:
            return pl.BlockSpec(tile, lambda i, ids: (i, ids[1]))
        return pl.BlockSpec(tile, lambda i, ids: (ids[1] * EXCH_TILES + i, 0))

    def out_spec(j):
        tile = _tile_rows(regions[j])
        if j < nb and BIG[ws[j]][3] == 0:
            return pl.BlockSpec(tile, lambda i, ids: (i, ids[0]))
        return pl.BlockSpec(tile, lambda i, ids: (ids[0] * EXCH_TILES + i, 0))

    got_spec = lambda j: pl.BlockSpec((3,) + _tile_rows(regions[j]), lambda i, ids: (0, i, 0))
    return list(pl.pallas_call(
        body, name="chip_sum",
        grid_spec=pltpu.PrefetchScalarGridSpec(
            num_scalar_prefetch=1, grid=(EXCH_TILES,),
            in_specs=[own_spec(j) for j in range(n)] + [got_spec(j) for j in range(n)],
            out_specs=[out_spec(j) for j in range(n)]),
        out_shape=[jax.ShapeDtypeStruct(_shard_shape(*BIG[w][1:]), F32) for w in ws]
        + [jax.ShapeDtypeStruct((SMALL_ROWS, LANES), F32)] * (n - nb),
        compiler_params=pltpu.CompilerParams(dimension_semantics=("parallel",)),
    )(ids, *sums, *arrived))


def _adamw_body(n):
    c1 = 1.0 - ADAM_B1 ** ADAM_STEP
    c2 = 1.0 - ADAM_B2 ** ADAM_STEP

    def body(*refs):
        for j in range(n):
            w, g, m, v = (refs[k * n + j][...] for k in range(4))
            m = ADAM_B1 * m + (1.0 - ADAM_B1) * g
            v = ADAM_B2 * v + (1.0 - ADAM_B2) * (g * g)
            refs[4 * n + j][...] = -ADAM_LR * ((m / c1) / (jnp.sqrt(v / c2) + ADAM_EPS) + ADAM_WD * w)
            refs[5 * n + j][...] = m
            refs[6 * n + j][...] = v

    return body


def _adamw(name, ws, gs, ms, vs):
    n = len(ws)
    specs = [pl.BlockSpec(_tile_rows(w.shape), lambda i: (i, 0)) for w in ws]
    shapes = [jax.ShapeDtypeStruct(w.shape, F32) for w in ws]
    out, _ = _call(_adamw_body(n), name, (EXCH_TILES,), specs * 4, specs * 3, shapes * 3, [], (*ws, *gs, *ms, *vs))
    return out[:n], out[n:2 * n], out[2 * n:]


def _adamw_whole(name, ws, gs, ms, vs):
    n = len(ws)
    out = pl.pallas_call(_adamw_body(n), name=name, out_shape=[jax.ShapeDtypeStruct(w.shape, F32) for w in ws] * 3)(*ws, *gs, *ms, *vs)
    return out[:n], out[n:2 * n], out[2 * n:]


def kernel(x, norm_mix_pre, norm_mix_post, w_in, ret_gn_gain, ssm_lambda_re, ssm_lambda_im, ssm_log_dt, ssm_b_re, ssm_b_im, ssm_c_re, ssm_c_im, ssm_d, w_glu, w_out, norm_mlp_pre, norm_mlp_post, w_ff1, w_ff2, loss_target, m_norm_mix_pre, m_norm_mix_post, m_w_in, m_ret_gn_gain, m_ssm_lambda_re, m_ssm_lambda_im, m_ssm_log_dt, m_ssm_b_re, m_ssm_b_im, m_ssm_c_re, m_ssm_c_im, m_ssm_d, m_w_glu, m_w_out, m_norm_mlp_pre, m_norm_mlp_post, m_w_ff1, m_w_ff2, v_norm_mix_pre, v_norm_mix_post, v_w_in, v_ret_gn_gain, v_ssm_lambda_re, v_ssm_lambda_im, v_ssm_log_dt, v_ssm_b_re, v_ssm_b_im, v_ssm_c_re, v_ssm_c_im, v_ssm_d, v_w_glu, v_w_out, v_norm_mlp_pre, v_norm_mlp_post, v_w_ff1, v_w_ff2):
    given = dict(locals())
    order = ["norm_mix_pre", "norm_mix_post", "w_in", "ret_gn_gain", "ssm_lambda_re", "ssm_lambda_im", "ssm_log_dt",
             "ssm_b_re", "ssm_b_im", "ssm_c_re", "ssm_c_im", "ssm_d", "w_glu", "w_out", "norm_mlp_pre", "norm_mlp_post",
             "w_ff1", "w_ff2"]
    big_names = [name for name, _, _, _ in BIG]
    ids = jnp.stack([lax.axis_index("c"), 2 * lax.axis_index("x") + lax.axis_index("y")]).astype(jnp.int32)

    weights = _cast_place(ids, [given[name][0] for name in big_names])
    w_in = _run_exchange("gather_w_in_send", _x_gather_send((0,), weights[:1]))
    w_in = _run_exchange("gather_w_in_pass", _x_gather_pass((0,), w_in))
    small_w = {name: given[name][0] for name, _ in SMALL}
    dx, d_in, mid_parts, late_grads, small_grads = _local_step(x[0], loss_target[0], small_w, w_in + weights[1:], ids)

    small_packed = _pack_small(small_grads)
    got = _run_exchange("pair_exchange", _x_join(_x_share(MID, mid_parts), _x_pair(FIRST, [d_in], small_packed)))
    mid_grads, got = got[:len(MID)], got[len(MID):]
    sums, sums_bf16 = _pair_sum(ids, FIRST, [d_in], got, small_packed)
    arrived = _run_exchange("chip_exchange", _x_chip(FIRST, sums_bf16, sums[-1]))
    parts = _chip_sum(ids, FIRST, sums, arrived)
    g_in, g_small = _run_exchange("pair_share", _x_share(FIRST, parts[:-1], parts[-1]))
    big_grads = [g_in] + mid_grads + late_grads
    big = lambda prefix: [given[prefix + name][0] for name in big_names]
    big_out = _adamw("adamw_big", big(""), big_grads, big("m_"), big("v_"))
    small_names = [name for name, _ in SMALL]
    small_tree = _unpack_small(g_small)
    small_grads = [small_tree[name] for name in small_names]
    small = lambda prefix: [given[prefix + name] for name in small_names]
    small_out = _adamw_whole("adamw_small", small(""), small_grads, small("m_"), small("v_"))

    def in_order(big_arrays, small_arrays):
        tree = {name: a[None] for name, a in zip(big_names, big_arrays)}
        tree.update(zip(small_names, small_arrays))
        return [tree[name] for name in order]

    loss = small_tree["loss_sum"].reshape(())
    return (loss, dx[None], *in_order(big_grads, small_grads), *in_order(big_out[0], small_out[0]),
            *in_order(big_out[1], small_out[1]), *in_order(big_out[2], small_out[2]))
```

```python
import functools
import math

import jax
import jax.numpy as jnp
import numpy as np
from jax import lax
from jax.experimental import pallas as pl
from jax.experimental.pallas import tpu as pltpu

F32 = jnp.float32
BF16 = jnp.bfloat16

D_MODEL = 1024
RET_WIDTH = 512
RET_HEADS = 4
HEAD_DIM = 128
RET_CHUNK = 128
ROPE_BASE = 10000.0
SSM_WIDTH = 512
SSM_GROUP = 16
SSM_GROUPS = 32
SSM_STATE = 64
N_STATES = SSM_GROUPS * SSM_STATE
D_FF = 4096
IN_COLS = 4 * RET_WIDTH + SSM_WIDTH
NORM_EPS = 1e-6
K_SCALE = HEAD_DIM ** -0.5

ADAM_LR = 0.001
ADAM_B1 = 0.9
ADAM_B2 = 0.999
ADAM_EPS = 1e-08
ADAM_WD = 0.01
ADAM_STEP = 10

LANES = 128
SUBLANES = 8
VMEM_LIMIT = 56 * 2 ** 20

TOK_TILE = 512
MLP_TILE = 256
RET_CHUNKS_PER_STEP = 4
S5_TILE = 256
S5_STEPS = S5_TILE // SUBLANES
S5_COLBLK = N_STATES // LANES
S5_GROUP = 4

N_CHIPS = 4
MESH = pl.DeviceIdType.MESH


def _dot(a, b):
    return jnp.dot(a, b, preferred_element_type=F32)


def _dot_nt(a, b):
    return lax.dot_general(a, b, (((1,), (1,)), ((), ())), preferred_element_type=F32)


def _dot_tn(a, b):
    return lax.dot_general(a, b, (((0,), (0,)), ((), ())), preferred_element_type=F32)


def _sigmoid(x):
    return 1.0 / (1.0 + jnp.exp(-x))


_GELU_C = math.sqrt(2.0 / math.pi)


def _gelu(x):
    return 0.5 * x * (1.0 + jnp.tanh(_GELU_C * (x + 0.044715 * (x * x * x))))


def _gelu_grad(x):
    t = jnp.tanh(_GELU_C * (x + 0.044715 * (x * x * x)))
    return 0.5 * (1.0 + t) + 0.5 * x * (1.0 - t * t) * (_GELU_C * (1.0 + 3.0 * 0.044715 * (x * x)))


def _rms(x):
    r = lax.rsqrt(jnp.mean(x * x, axis=-1, keepdims=True) + NORM_EPS)
    return x * r, r


def _rms_bwd(n, r, gain, dy):
    dn = dy * gain
    dx = r * (dn - n * jnp.mean(dn * n, axis=-1, keepdims=True))
    return dx, jnp.sum(dy * n, axis=0, keepdims=True)


def _full(shape):
    nd = len(shape)
    return pl.BlockSpec(shape, lambda i, _n=nd: (0,) * _n, pipeline_mode=pl.Buffered(1))


def _rows(tile, width):
    return pl.BlockSpec((tile, width), lambda i: (i, 0))


def _rows_rev(tile, width, n):
    return pl.BlockSpec((tile, width), lambda i, _n=n: (_n - 1 - i, 0))


def _params(sem):
    return pltpu.CompilerParams(dimension_semantics=(sem,), vmem_limit_bytes=VMEM_LIMIT)


ANY = pl.BlockSpec(memory_space=pl.ANY)


class _Exchange:
    def __init__(self, operands, out_shapes, aliases, n, copies):
        self.operands, self.out_shapes, self.aliases, self.n, self.copies = list(operands), list(out_shapes), aliases, n, copies


def _call(body, name, grid, in_specs, out_specs, out_shape, scratch_shapes, args, exchange=None):
    if exchange is None:
        outs = pl.pallas_call(body, name=name, grid=grid, in_specs=in_specs, out_specs=out_specs, out_shape=out_shape,
                              scratch_shapes=scratch_shapes, compiler_params=_params("arbitrary"))(*args)
        return list(outs), []
    n_in, n_out, n_scr = len(in_specs), len(out_specs), len(scratch_shapes)
    k_in, k_out = len(exchange.operands), len(exchange.out_shapes)
    last = grid[0] - 1

    def hosted(*refs):
        own_in, rest = refs[:n_in], refs[n_in:]
        ex_in, rest = rest[:k_in], rest[k_in:]
        own_out, rest = rest[:n_out], rest[n_out:]
        ex_out, rest = rest[:k_out], rest[k_out:]
        own_scr, (send_sems, recv_sems) = rest[:n_scr], rest[n_scr:]

        @pl.when(pl.program_id(0) == 0)
        def _():
            for cp in exchange.copies(ex_in, ex_out, send_sems, recv_sems):
                cp.start()

        body(*own_in, *own_out, *own_scr)

        @pl.when(pl.program_id(0) == last)
        def _():
            for cp in exchange.copies(ex_in, ex_out, send_sems, recv_sems):
                cp.wait()

    outs = pl.pallas_call(
        hosted, name=name, grid=grid, in_specs=list(in_specs) + [ANY] * k_in, out_specs=list(out_specs) + [ANY] * k_out,
        out_shape=list(out_shape) + exchange.out_shapes,
        scratch_shapes=list(scratch_shapes) + [pltpu.SemaphoreType.DMA((exchange.n,)), pltpu.SemaphoreType.DMA((exchange.n,))],
        input_output_aliases={n_in + i: n_out + o for i, o in exchange.aliases.items()},
        compiler_params=_params("arbitrary"),
    )(*args, *exchange.operands)
    return list(outs[:n_out]), list(outs[n_out:])


def _rope(t, cos2, sin2):
    return t * cos2 + pltpu.roll(t, HEAD_DIM // 2, 1) * sin2


def _rope_bwd(d, cos2, sin2):
    return d * cos2 + pltpu.roll(d * sin2, HEAD_DIM // 2, d.ndim - 1)


def _inproj_fwd(x, gain, w_in, cos2, sin2, exchange=None):
    seq = x.shape[0]

    def body(x_ref, g_ref, w_ref, c_ref, s_ref, q_ref, k_ref, v_ref, gate_ref, u_ref):
        n, _ = _rms(x_ref[...])
        h = (n * g_ref[...]).astype(BF16)
        proj = _dot(h, w_ref[...])
        c = c_ref[...]
        s = s_ref[...]
        for hd in range(RET_HEADS):
            lo = hd * HEAD_DIM
            q_ref[:, lo:lo + HEAD_DIM] = _rope(proj[:, lo:lo + HEAD_DIM], c, s).astype(BF16)
            kh = proj[:, RET_WIDTH + lo:RET_WIDTH + lo + HEAD_DIM]
            k_ref[:, lo:lo + HEAD_DIM] = (_rope(kh, c, s) * K_SCALE).astype(BF16)
        v_ref[...] = proj[:, 2 * RET_WIDTH:3 * RET_WIDTH].astype(BF16)
        gate_ref[...] = proj[:, 3 * RET_WIDTH:4 * RET_WIDTH]
        u_ref[...] = proj[:, 4 * RET_WIDTH:]

    t = TOK_TILE
    half = lambda dt: jax.ShapeDtypeStruct((seq, RET_WIDTH), dt)
    return _call(
        body, "inproj_fwd", (seq // t,),
        [_rows(t, D_MODEL), _full((1, D_MODEL)), _full((D_MODEL, IN_COLS)), _rows(t, HEAD_DIM), _rows(t, HEAD_DIM)],
        [_rows(t, RET_WIDTH)] * 5, [half(BF16), half(BF16), half(BF16), half(F32), half(F32)], [],
        (x, gain, w_in, cos2, sin2), exchange)


def _inproj_bwd(x, gain, w_in, dpieces, dres, adam=()):
    seq = x.shape[0]
    t = TOK_TILE
    steps = seq // t
    n_pieces = len(dpieces)
    n_adam = len(adam[0]) if adam else 0
    adam_step = _adamw_body(n_adam)

    def body(x_ref, g_ref, w_ref, *refs):
        piece_refs, (dres_ref, *adam_in), (dx_ref, dw_ref, dg_ref, *adam_out) = (
            refs[:n_pieces], refs[n_pieces:n_pieces + 1 + 4 * n_adam], refs[n_pieces + 1 + 4 * n_adam:])

        @pl.when(pl.program_id(0) == 0)
        def _():
            dw_ref[...] = jnp.zeros_like(dw_ref)
            dg_ref[...] = jnp.zeros_like(dg_ref)

        adam_step(*adam_in, *adam_out)

        n, r = _rms(x_ref[...])
        gain_v = g_ref[...]
        h = (n * gain_v).astype(BF16)
        dh = None
        for j, piece in enumerate(piece_refs):
            cols = slice(j * RET_WIDTH, (j + 1) * RET_WIDTH)
            dp = piece[...]
            dw_ref[:, cols] += _dot_tn(h, dp)
            term = _dot_nt(dp, w_ref[:, cols])
            dh = term if dh is None else dh + term
        dx, dgs = _rms_bwd(n, r, gain_v, dh)
        dg_ref[...] += dgs
        dx_ref[...] = dres_ref[...] + dx

    adam_arrays = [a for group in adam for a in group]
    adam_spec = lambda a: pl.BlockSpec((a.shape[0] // steps, a.shape[1]), lambda i: (i, 0))
    outs, _ = _call(
        body, "inproj_bwd", (steps,),
        [_rows(t, D_MODEL), _full((1, D_MODEL)), _full((D_MODEL, IN_COLS))] + [_rows(t, RET_WIDTH)] * n_pieces + [_rows(t, D_MODEL)]
        + [adam_spec(a) for a in adam_arrays],
        [_rows(t, D_MODEL), _full((D_MODEL, IN_COLS)), _full((1, D_MODEL))] + [adam_spec(a) for a in adam_arrays[:n_adam]] * 3,
        [jax.ShapeDtypeStruct((seq, D_MODEL), F32), jax.ShapeDtypeStruct((D_MODEL, IN_COLS), F32),
         jax.ShapeDtypeStruct((1, D_MODEL), F32)] + [jax.ShapeDtypeStruct(a.shape, F32) for a in adam_arrays[:n_adam]] * 3,
        [], (x, gain, w_in, *dpieces, dres, *adam_arrays))
    stepped = outs[3:]
    return outs[:3], (stepped[:n_adam], stepped[n_adam:2 * n_adam], stepped[2 * n_adam:])


def _ret_consts():
    c = RET_CHUNK
    f32 = np.float32
    log_gamma = np.log(f32(1.0) - np.exp(np.linspace(math.log(1.0 / 32), math.log(1.0 / 512), RET_HEADS, dtype=f32))).astype(f32)
    idx = np.arange(c, dtype=f32)
    diff = idx[:, None] - idx[None, :]
    decay = np.where(diff[None] >= 0, np.exp(np.maximum(diff, f32(0.0))[None] * log_gamma[:, None, None]), f32(0.0))
    zeta = np.exp((c - 1 - idx)[None, :] * log_gamma[:, None])
    xi = np.exp((idx + f32(1.0))[None, :] * log_gamma[:, None])
    g_chunk = np.exp(f32(c) * log_gamma)
    wide = lambda rowvals: jnp.asarray(np.broadcast_to(rowvals[:, :, None], (RET_HEADS, c, c)).astype(f32))
    return (jnp.asarray(decay.astype(f32)), wide(xi), wide(zeta),
            jnp.asarray(np.broadcast_to(g_chunk[:, None, None], (RET_HEADS, c, c)).astype(f32)))


def _rope_tables(seq):
    f32 = np.float32
    half = HEAD_DIM // 2
    inv_freq = np.power(f32(ROPE_BASE), -np.arange(half, dtype=f32) / f32(half)).astype(f32)
    ang = (np.arange(seq, dtype=f32)[:, None] * inv_freq[None, :]).astype(f32)
    cos, sin = np.cos(ang).astype(f32), np.sin(ang).astype(f32)
    return jnp.asarray(np.concatenate([cos, cos], axis=1)), jnp.asarray(np.concatenate([-sin, sin], axis=1))


def _bmm(a, b):
    return lax.dot_general(a, b, (((2,), (1,)), ((0,), (0,))), preferred_element_type=F32)


def _bmm_nt(a, b):
    return lax.dot_general(a, b, (((2,), (2,)), ((0,), (0,))), preferred_element_type=F32)


def _bmm_tn(a, b):
    return lax.dot_general(a, b, (((1,), (1,)), ((0,), (0,))), preferred_element_type=F32)


def _ret_blocks(ref):
    return jnp.stack([ref[cc * RET_CHUNK:(cc + 1) * RET_CHUNK, hd * HEAD_DIM:(hd + 1) * HEAD_DIM]
                      for cc in range(RET_CHUNKS_PER_STEP) for hd in range(RET_HEADS)])


def _ret_unblock(ref, blocks):
    for cc in range(RET_CHUNKS_PER_STEP):
        for hd in range(RET_HEADS):
            ref[cc * RET_CHUNK:(cc + 1) * RET_CHUNK, hd * HEAD_DIM:(hd + 1) * HEAD_DIM] = blocks[cc * RET_HEADS + hd].astype(ref.dtype)


def _rope_blocks(ref):
    return jnp.stack([ref[cc * RET_CHUNK:(cc + 1) * RET_CHUNK, :] for cc in range(RET_CHUNKS_PER_STEP) for _ in range(RET_HEADS)])


def _per_chunk(ref):
    return jnp.concatenate([ref[...]] * RET_CHUNKS_PER_STEP, axis=0)


def _ret_gain(gain_ref):
    return jnp.stack([gain_ref[:, hd * HEAD_DIM:(hd + 1) * HEAD_DIM]
                      for _ in range(RET_CHUNKS_PER_STEP) for hd in range(RET_HEADS)])


def _ret_fwd(q, k, v, gate, gn_gain, consts, exchange=None):
    seq = q.shape[0]
    cps = RET_CHUNKS_PER_STEP
    t = cps * RET_CHUNK
    n_chunks = seq // RET_CHUNK
    dec, xi, zeta, gch = consts

    def body(q_ref, k_ref, v_ref, gate_ref, gain_ref, dec_ref, xi_ref, zeta_ref, gch_ref, y_ref, rprev_ref, state):
        @pl.when(pl.program_id(0) == 0)
        def _():
            state[...] = jnp.zeros_like(state)

        qb, kb, vb = _ret_blocks(q_ref), _ret_blocks(k_ref), _ret_blocks(v_ref)
        s = _bmm_nt(qb, kb) * _per_chunk(dec_ref)
        vz = (vb.astype(F32) * _per_chunk(zeta_ref)).astype(BF16)
        incr = _bmm_tn(kb, vz)
        gch_v = gch_ref[...]
        rp = state[...]
        before = []
        for cc in range(cps):
            before.append(rp.astype(BF16))
            rprev_ref[cc] = before[-1]
            rp = gch_v * rp + incr[cc * RET_HEADS:(cc + 1) * RET_HEADS]
        state[...] = rp
        qx = (qb.astype(F32) * _per_chunk(xi_ref)).astype(BF16)
        o = _bmm(s.astype(BF16), vb) + _bmm(qx, jnp.concatenate(before, axis=0))
        oc = o - jnp.mean(o, axis=-1, keepdims=True)
        on = oc * lax.rsqrt(jnp.mean(oc * oc, axis=-1, keepdims=True) + NORM_EPS)
        g = _ret_blocks(gate_ref)
        _ret_unblock(y_ref, g * _sigmoid(g) * (on * _ret_gain(gain_ref)))

    cst = _full((RET_HEADS, RET_CHUNK, RET_CHUNK))
    return _call(
        body, "ret_fwd", (seq // t,),
        [_rows(t, RET_WIDTH)] * 4 + [_full((1, RET_WIDTH)), cst, cst, cst, cst],
        [_rows(t, RET_WIDTH), pl.BlockSpec((cps, RET_HEADS, HEAD_DIM, HEAD_DIM), lambda i: (i, 0, 0, 0))],
        [jax.ShapeDtypeStruct((seq, RET_WIDTH), BF16), jax.ShapeDtypeStruct((n_chunks, RET_HEADS, HEAD_DIM, HEAD_DIM), BF16)],
        [pltpu.VMEM((RET_HEADS, HEAD_DIM, HEAD_DIM), F32)],
        (q, k, v, gate, gn_gain, dec, xi, zeta, gch), exchange)


def _ret_bwd(q, k, v, gate, gn_gain, consts, rprev, dy_ret, cos2, sin2, exchange=None):
    seq = q.shape[0]
    cps = RET_CHUNKS_PER_STEP
    t = cps * RET_CHUNK
    nt = seq // t
    dec, xi, zeta, gch = consts

    def body(q_ref, k_ref, v_ref, gate_ref, gain_ref, dec_ref, xi_ref, zeta_ref, gch_ref, rprev_ref, dyr_ref, cos_ref, sin_ref,
             dq_ref, dk_ref, dv_ref, dgate_ref, dgain_ref, dstate):
        @pl.when(pl.program_id(0) == 0)
        def _():
            dstate[...] = jnp.zeros_like(dstate)
            dgain_ref[...] = jnp.zeros_like(dgain_ref)

        nb = cps * RET_HEADS
        qb, kb, vb = _ret_blocks(q_ref), _ret_blocks(k_ref), _ret_blocks(v_ref)
        dec_b, xi_b, zeta_b = _per_chunk(dec_ref), _per_chunk(xi_ref), _per_chunk(zeta_ref)
        rpb = rprev_ref[...].reshape(nb, HEAD_DIM, HEAD_DIM)
        sdb = (_bmm_nt(qb, kb) * dec_b).astype(BF16)
        qx = (qb.astype(F32) * xi_b).astype(BF16)
        o = _bmm(sdb, vb) + _bmm(qx, rpb)
        oc = o - jnp.mean(o, axis=-1, keepdims=True)
        rstd = lax.rsqrt(jnp.mean(oc * oc, axis=-1, keepdims=True) + NORM_EPS)
        on = oc * rstd
        g = _ret_blocks(gate_ref)
        sg = _sigmoid(g)
        gain_b = _ret_gain(gain_ref)
        dyr = _ret_blocks(dyr_ref)
        _ret_unblock(dgate_ref, dyr * (on * gain_b) * (sg * (1.0 + g * (1.0 - sg))))
        dy = dyr * (g * sg)
        dgain = jnp.sum(dy * on, axis=1, keepdims=True)
        for hd in range(RET_HEADS):
            part = dgain[hd]
            for cc in range(1, cps):
                part = part + dgain[cc * RET_HEADS + hd]
            dgain_ref[:, hd * HEAD_DIM:(hd + 1) * HEAD_DIM] += part
        don = dy * gain_b
        do = rstd * (don - jnp.mean(don, axis=-1, keepdims=True) - on * jnp.mean(don * on, axis=-1, keepdims=True))
        dob = do.astype(BF16)
        ds = (_bmm_nt(dob, vb) * dec_b).astype(BF16)
        dq = _bmm(ds, kb) + _bmm_nt(dob, rpb) * xi_b
        dk = _bmm_tn(ds, qb)
        dv = _bmm_tn(sdb, dob)
        dstate_local = _bmm_tn(qx, dob)
        vz = (vb.astype(F32) * zeta_b).astype(BF16)
        gch_v = gch_ref[...]
        zeta_v = zeta_ref[...]
        gh = dstate[...]
        dk_extra, dv_extra = [None] * cps, [None] * cps
        for cc in reversed(range(cps)):
            sl = slice(cc * RET_HEADS, (cc + 1) * RET_HEADS)
            gb = gh.astype(BF16)
            dk_extra[cc] = _bmm_nt(vz[sl], gb)
            dv_extra[cc] = _bmm(kb[sl], gb) * zeta_v
            gh = dstate_local[sl] + gch_v * gh
        dstate[...] = gh
        cos_b, sin_b = _rope_blocks(cos_ref), _rope_blocks(sin_ref)
        _ret_unblock(dq_ref, _rope_bwd(dq, cos_b, sin_b))
        _ret_unblock(dk_ref, _rope_bwd((dk + jnp.concatenate(dk_extra, axis=0)) * K_SCALE, cos_b, sin_b))
        _ret_unblock(dv_ref, dv + jnp.concatenate(dv_extra, axis=0))

    cst = _full((RET_HEADS, RET_CHUNK, RET_CHUNK))
    rev = _rows_rev(t, RET_WIDTH, nt)
    act = jax.ShapeDtypeStruct((seq, RET_WIDTH), BF16)
    return _call(
        body, "ret_bwd", (nt,),
        [rev] * 4 + [_full((1, RET_WIDTH)), cst, cst, cst, cst,
                     pl.BlockSpec((cps, RET_HEADS, HEAD_DIM, HEAD_DIM), lambda i: (nt - 1 - i, 0, 0, 0)), rev,
                     _rows_rev(t, HEAD_DIM, nt), _rows_rev(t, HEAD_DIM, nt)],
        [rev] * 4 + [_full((1, RET_WIDTH))], [act, act, act, act, jax.ShapeDtypeStruct((1, RET_WIDTH), F32)],
        [pltpu.VMEM((RET_HEADS, HEAD_DIM, HEAD_DIM), F32)],
        (q, k, v, gate, gn_gain, dec, xi, zeta, gch, rprev, dy_ret, cos2, sin2), exchange)


MXU_TILE = 256
S5_STATE_TILES = 2 * N_STATES // MXU_TILE
S5_CHANNELS_PER_TILE = MXU_TILE // SSM_STATE * SSM_GROUP
S5_TILES_SHAPE = (S5_STATE_TILES, LANES, MXU_TILE)


def _chan_block(kk):
    return ((kk % (S5_STATE_TILES // 2)) * S5_CHANNELS_PER_TILE) // LANES * LANES


def _s5_tiles(blocks_re, blocks_im):
    per = MXU_TILE // SSM_STATE
    half = S5_STATE_TILES // 2
    eye = jnp.eye(per, dtype=F32)
    in_half = (jnp.arange(half) % (LANES // S5_CHANNELS_PER_TILE))[:, None] == jnp.arange(LANES // S5_CHANNELS_PER_TILE)[None, :]
    out = []
    for blk in (blocks_re, blocks_im):
        t = blk.reshape(half, per, SSM_GROUP, 1, SSM_STATE) * eye[None, :, None, :, None]
        t = t.reshape(half, 1, S5_CHANNELS_PER_TILE, MXU_TILE)
        out.append(jnp.where(in_half[:, :, None, None], t, 0.0).reshape(half, LANES, MXU_TILE))
    return jnp.concatenate(out, axis=0)


def _s5_discretise(lam_re, lam_im, log_dt, b_re, b_im, c_re, c_im):
    lam = lax.complex(jnp.minimum(lam_re, -1e-4), lam_im)
    dt = jnp.exp(log_dt)[:, None]
    lam_bar = jnp.exp(lam * dt)
    b_bar = ((lam_bar - 1.0) / lam)[:, :, None] * lax.complex(b_re, b_im)
    b_tiles = _s5_tiles(jnp.swapaxes(jnp.real(b_bar), 1, 2), jnp.swapaxes(jnp.imag(b_bar), 1, 2))
    return jnp.real(lam_bar), jnp.imag(lam_bar), b_tiles, _s5_tiles(c_re, -c_im)


def _s5_tables(lbr, lbi):
    lr = lbr.reshape(1, N_STATES)
    li = lbi.reshape(1, N_STATES)
    pr, pi = lr, li
    while pr.shape[0] < S5_STEPS:
        top_r, top_i = pr[-1:], pi[-1:]
        pr, pi = (jnp.concatenate([pr, pr * top_r - pi * top_i], axis=0), jnp.concatenate([pi, pr * top_i + pi * top_r], axis=0))
    bc = lambda a: jnp.broadcast_to(a, (SUBLANES, N_STATES))
    rep = lambda rows: jnp.broadcast_to(rows[:, None, :], (S5_STEPS, SUBLANES, N_STATES)).reshape(S5_STEPS * SUBLANES, N_STATES)
    return bc(lr), bc(li), rep(pr), rep(pi)


def _cmul(ar, ai, br, bi):
    return ar * br - ai * bi, ar * bi + ai * br


def _seg(i):
    return pl.ds(pl.multiple_of(i * SUBLANES, SUBLANES), SUBLANES)


def _scan_order():
    r = np.arange(S5_TILE)
    token = (r % SUBLANES) * S5_STEPS + r // SUBLANES
    p = (token[:, None] == np.arange(S5_TILE)[None, :]).astype(np.float32)
    return jnp.asarray(p, dtype=BF16), jnp.asarray(p.T, dtype=BF16)


def _to_scan_order(p_ref, rows_bf16):
    return _dot(p_ref[...], rows_bf16).astype(BF16)


def _from_scan_order(pt_ref, rows):
    hi = rows.astype(BF16)
    rest = rows - hi.astype(F32)
    mid = rest.astype(BF16)
    lo = (rest - mid.astype(F32)).astype(BF16)
    pt = pt_ref[...]
    return _dot(pt, hi) + _dot(pt, mid) + _dot(pt, lo)


def _s5_channels_to_states(chan_b, m_ref, xs):
    for kk in range(S5_STATE_TILES):
        lo = _chan_block(kk)
        res = _dot(chan_b[:, lo:lo + LANES], m_ref[kk])
        xs[2 * kk] = res[:, :LANES]
        xs[2 * kk + 1] = res[:, LANES:]


def _s5_states_to_channels(sb_ref, m_ref):
    out = []
    for lo in range(0, SSM_WIDTH, LANES):
        acc = None
        for kk in range(S5_STATE_TILES):
            if _chan_block(kk) == lo:
                term = _dot_nt(sb_ref[:, kk * MXU_TILE:(kk + 1) * MXU_TILE], m_ref[kk])
                acc = term if acc is None else acc + term
        out.append(acc)
    return jnp.concatenate(out, axis=1)


def _s5_outer_acc(acc_ref, chan_b, states_ref):
    for kk in range(S5_STATE_TILES):
        lo = _chan_block(kk)
        acc_ref[kk] += _dot_tn(chan_b[:, lo:lo + LANES], states_ref[:, kk * MXU_TILE:(kk + 1) * MXU_TILE])


def _s5_scan_tile(xs, lr_ref, li_ref, pr_ref, pi_ref, carry_re, carry_im, conj, reverse, prev_of=None):
    sgn = -1.0 if conj else 1.0
    seg_in_re, seg_in_im, sums_re, sums_im = [], [], [], []
    group = S5_GROUP
    for grp in range(S5_COLBLK // group):
        blks = [grp * group + j for j in range(group)]
        lrs = [lr_ref[:, b * LANES:(b + 1) * LANES] for b in blks]
        lis = [sgn * li_ref[:, b * LANES:(b + 1) * LANES] for b in blks]

        def step(it, carry, blks=blks, lrs=lrs, lis=lis):
            i = (S5_STEPS - 1 - it) if reverse else it
            out_r, out_i = [], []
            for j, b in enumerate(blks):
                xr, xi = _cmul(lrs[j], lis[j], carry[j], carry[group + j])
                xr = xr + xs[b, _seg(i), :]
                xi = xi + xs[S5_COLBLK + b, _seg(i), :]
                xs[b, _seg(i), :] = xr
                xs[S5_COLBLK + b, _seg(i), :] = xi
                out_r.append(xr)
                out_i.append(xi)
            return tuple(out_r + out_i)

        zeros = tuple(jnp.zeros((SUBLANES, LANES), F32) for _ in range(2 * group))
        ends = lax.fori_loop(0, S5_STEPS, step, zeros, unroll=True)

        ins_r, ins_i = [], []
        for j, b in enumerate(blks):
            cols = slice(b * LANES, (b + 1) * LANES)
            pr = pr_ref[SUBLANES * S5_STEPS - 1:SUBLANES * S5_STEPS, cols]
            pi = sgn * pi_ref[SUBLANES * S5_STEPS - 1:SUBLANES * S5_STEPS, cols]
            cur_r = carry_re[:, cols]
            cur_i = carry_im[:, cols]
            row_id = lax.broadcasted_iota(jnp.int32, (SUBLANES, LANES), 0)
            in_r = jnp.zeros((SUBLANES, LANES), F32)
            in_i = jnp.zeros((SUBLANES, LANES), F32)
            order = range(SUBLANES - 1, -1, -1) if reverse else range(SUBLANES)
            for sgm in order:
                in_r = jnp.where(row_id == sgm, cur_r, in_r)
                in_i = jnp.where(row_id == sgm, cur_i, in_i)
                mr, mi = _cmul(pr, pi, cur_r, cur_i)
                cur_r = mr + ends[j][sgm:sgm + 1, :]
                cur_i = mi + ends[group + j][sgm:sgm + 1, :]
            carry_re[:, cols] = cur_r
            carry_im[:, cols] = cur_i
            ins_r.append(in_r)
            ins_i.append(in_i)

        def fix(it, c, blks=blks, ins_r=ins_r, ins_i=ins_i):
            pw = (S5_STEPS - 1 - it) if reverse else it
            out = []
            for j, b in enumerate(blks):
                cols = slice(b * LANES, (b + 1) * LANES)
                prow = pl.ds(pl.multiple_of(pw * SUBLANES, SUBLANES), SUBLANES)
                fr, fi = _cmul(pr_ref[prow, cols], sgn * pi_ref[prow, cols], ins_r[j], ins_i[j])
                vr = xs[b, _seg(it), :] + fr
                vi = xs[S5_COLBLK + b, _seg(it), :] + fi
                xs[b, _seg(it), :] = vr
                xs[S5_COLBLK + b, _seg(it), :] = vi
                if prev_of is not None:
                    opr, opi, sr, si = c[4 * j:4 * j + 4]
                    out += [prev_of[0][b, _seg(it), :], prev_of[0][S5_COLBLK + b, _seg(it), :],
                            sr + vr * opr + vi * opi, si + vi * opr - vr * opi]
            return tuple(out)

        init = ()
        if prev_of is not None:
            zero = jnp.zeros((SUBLANES, LANES), F32)
            for b in blks:
                init += (prev_of[1][b], prev_of[2][b], zero, zero)
        done = lax.fori_loop(0, S5_STEPS, fix, init, unroll=True)
        for j in range(len(blks) if prev_of is not None else 0):
            sums_re.append(done[4 * j + 2])
            sums_im.append(done[4 * j + 3])
        seg_in_re += ins_r
        seg_in_im += ins_i
    return seg_in_re, seg_in_im, sums_re, sums_im


def _s5_pack(xs, dst):
    for b in range(2 * S5_COLBLK):
        dst[:, b * LANES:(b + 1) * LANES] = xs[b].astype(BF16)


def _s5_fwd(u, b_blk, c_blk, tables, d_skip, exchange=None):
    seq = u.shape[0]
    t = S5_TILE
    nt = seq // t
    lr8, li8, p_re, p_im = tables

    p, pt = _scan_order()

    def body(u_ref, p_ref, pt_ref, b_ref, c_ref, lr_ref, li_ref, pr_ref, pi_ref, d_ref, y_ref, start_ref,
             xs, xb, carry_re, carry_im):
        @pl.when(pl.program_id(0) == 0)
        def _():
            carry_re[...] = jnp.zeros_like(carry_re)
            carry_im[...] = jnp.zeros_like(carry_im)

        start_ref[0, 0:1, :] = carry_re[...]
        start_ref[0, 1:2, :] = carry_im[...]
        uv = u_ref[...]
        _s5_channels_to_states(_to_scan_order(p_ref, uv.astype(BF16)), b_ref, xs)
        _s5_scan_tile(xs, lr_ref, li_ref, pr_ref, pi_ref, carry_re, carry_im, conj=False, reverse=False)
        _s5_pack(xs, xb)
        y_ref[...] = _from_scan_order(pt_ref, _s5_states_to_channels(xb, c_ref)) + d_ref[...] * uv

    return _call(
        body, "s5_fwd", (nt,),
        [_rows(t, SSM_WIDTH), _full((t, t)), _full((t, t)), _full(S5_TILES_SHAPE), _full(S5_TILES_SHAPE),
         _full((SUBLANES, N_STATES)), _full((SUBLANES, N_STATES)), _full((SUBLANES * S5_STEPS, N_STATES)),
         _full((SUBLANES * S5_STEPS, N_STATES)), _full((1, SSM_WIDTH))],
        [_rows(t, SSM_WIDTH), pl.BlockSpec((1, 2, N_STATES), lambda i: (i, 0, 0))],
        [jax.ShapeDtypeStruct((seq, SSM_WIDTH), F32), jax.ShapeDtypeStruct((nt, 2, N_STATES), F32)],
        [pltpu.VMEM((2 * S5_COLBLK, t, LANES), F32), pltpu.VMEM((t, 2 * N_STATES), BF16),
         pltpu.VMEM((1, N_STATES), F32), pltpu.VMEM((1, N_STATES), F32)],
        (u, p, pt, b_blk, c_blk, lr8, li8, p_re, p_im, d_skip), exchange)


def _s5_bwd(u, dy, b_blk, c_blk, tables, d_skip, starts, exchange=None):
    seq = u.shape[0]
    t = S5_TILE
    nt = seq // t
    lr8, li8, p_re, p_im = tables
    p, pt = _scan_order()

    def body(u_ref, dy_ref, p_ref, pt_ref, b_ref, c_ref, lr_ref, li_ref, pr_ref, pi_ref, d_ref, start_ref,
             du_ref, dd_ref, dlam_ref, db_ref, dc_ref,
             xs, as_, xb, ab, carry_re, carry_im, acar_re, acar_im):
        @pl.when(pl.program_id(0) == 0)
        def _():
            acar_re[...] = jnp.zeros_like(acar_re)
            acar_im[...] = jnp.zeros_like(acar_im)
            dd_ref[...] = jnp.zeros_like(dd_ref)
            dlam_ref[...] = jnp.zeros_like(dlam_ref)
            db_ref[...] = jnp.zeros_like(db_ref)
            dc_ref[...] = jnp.zeros_like(dc_ref)

        uv = u_ref[...]
        dyv = dy_ref[...]
        ub = _to_scan_order(p_ref, uv.astype(BF16))
        dyb = _to_scan_order(p_ref, dyv.astype(BF16))
        carry_re[...] = start_ref[0, 0:1, :]
        carry_im[...] = start_ref[0, 1:2, :]
        _s5_channels_to_states(ub, b_ref, xs)
        in_re, in_im, _, _ = _s5_scan_tile(xs, lr_ref, li_ref, pr_ref, pi_ref, carry_re, carry_im, conj=False, reverse=False)
        _s5_channels_to_states(dyb, c_ref, as_)
        _, _, dl_re, dl_im = _s5_scan_tile(as_, lr_ref, li_ref, pr_ref, pi_ref, acar_re, acar_im, conj=True, reverse=True,
                                           prev_of=(xs, in_re, in_im))
        for b in range(S5_COLBLK):
            cols = slice(b * LANES, (b + 1) * LANES)
            dlam_ref[0, :, cols] += dl_re[b]
            dlam_ref[1, :, cols] += dl_im[b]
        _s5_pack(xs, xb)
        _s5_pack(as_, ab)
        du_ref[...] = (_from_scan_order(pt_ref, _s5_states_to_channels(ab, b_ref)) + d_ref[...] * dyv).astype(BF16)
        dd_ref[...] += jnp.sum(dyv * uv, axis=0, keepdims=True)
        _s5_outer_acc(db_ref, ub, ab)
        _s5_outer_acc(dc_ref, dyb, xb)

    rev = _rows_rev(t, SSM_WIDTH, nt)
    vec = lambda: pltpu.VMEM((1, N_STATES), F32)
    outer = S5_TILES_SHAPE
    return _call(
        body, "s5_bwd", (nt,),
        [rev, rev, _full((t, t)), _full((t, t)), _full(S5_TILES_SHAPE), _full(S5_TILES_SHAPE),
         _full((SUBLANES, N_STATES)), _full((SUBLANES, N_STATES)), _full((SUBLANES * S5_STEPS, N_STATES)),
         _full((SUBLANES * S5_STEPS, N_STATES)), _full((1, SSM_WIDTH)),
         pl.BlockSpec((1, 2, N_STATES), lambda i: (nt - 1 - i, 0, 0))],
        [rev, _full((1, SSM_WIDTH)), _full((2, SUBLANES, N_STATES)), _full(outer), _full(outer)],
        [jax.ShapeDtypeStruct((seq, SSM_WIDTH), BF16), jax.ShapeDtypeStruct((1, SSM_WIDTH), F32),
         jax.ShapeDtypeStruct((2, SUBLANES, N_STATES), F32), jax.ShapeDtypeStruct(outer, F32), jax.ShapeDtypeStruct(outer, F32)],
        [pltpu.VMEM((2 * S5_COLBLK, t, LANES), F32), pltpu.VMEM((2 * S5_COLBLK, t, LANES), F32),
         pltpu.VMEM((t, 2 * N_STATES), BF16), pltpu.VMEM((t, 2 * N_STATES), BF16), vec(), vec(), vec(), vec()],
        (u, dy, p, pt, b_blk, c_blk, lr8, li8, p_re, p_im, d_skip, starts), exchange)


def _mix_fwd(y_ssm, y_ret, x, gain, w_glu, w_out, exchange=None):
    seq = x.shape[0]
    t = TOK_TILE

    def body(ys_ref, yr_ref, x_ref, g_ref, wg_ref, wo_ref, x1_ref, mix_ref):
        g0 = _gelu(ys_ref[...]).astype(BF16)
        z = _dot(g0, wg_ref[...])
        glu = (z[:, :SSM_WIDTH] * _sigmoid(z[:, SSM_WIDTH:])).astype(BF16)
        mix = _dot(yr_ref[...], wo_ref[:RET_WIDTH, :]) + _dot(glu, wo_ref[RET_WIDTH:, :])
        mix_ref[...] = mix
        n, _ = _rms(mix)
        x1_ref[...] = x_ref[...] + n * g_ref[...]

    act = jax.ShapeDtypeStruct((seq, D_MODEL), F32)
    return _call(
        body, "mix_fwd", (seq // t,),
        [_rows(t, SSM_WIDTH), _rows(t, RET_WIDTH), _rows(t, D_MODEL), _full((1, D_MODEL)),
         _full((SSM_WIDTH, 2 * SSM_WIDTH)), _full((D_MODEL, D_MODEL))],
        [_rows(t, D_MODEL)] * 2, [act, act], [], (y_ssm, y_ret, x, gain, w_glu, w_out), exchange)


def _mix_bwd(dx1, mix, gain, y_ssm, y_ret, w_glu, w_out, exchange=None):
    seq = dx1.shape[0]
    t = TOK_TILE

    def body(dx1_ref, mix_ref, g_ref, ys_ref, yr_ref, wg_ref, wo_ref, dyr_ref, dys_ref, dwo_ref, dwg_ref, dg_ref):
        @pl.when(pl.program_id(0) == 0)
        def _():
            dwo_ref[...] = jnp.zeros_like(dwo_ref)
            dwg_ref[...] = jnp.zeros_like(dwg_ref)
            dg_ref[...] = jnp.zeros_like(dg_ref)

        n, r = _rms(mix_ref[...])
        dmix, dgs = _rms_bwd(n, r, g_ref[...], dx1_ref[...])
        dg_ref[...] += dgs
        dmb = dmix.astype(BF16)
        dcat = _dot_nt(dmb, wo_ref[...])
        dyr_ref[...] = dcat[:, :RET_WIDTH]
        dglu = dcat[:, RET_WIDTH:]
        ys = ys_ref[...]
        g0 = _gelu(ys).astype(BF16)
        z = _dot(g0, wg_ref[...])
        a = z[:, :SSM_WIDTH]
        sb = _sigmoid(z[:, SSM_WIDTH:])
        dwo_ref[:RET_WIDTH, :] += _dot_tn(yr_ref[...], dmb)
        dwo_ref[RET_WIDTH:, :] += _dot_tn((a * sb).astype(BF16), dmb)
        dz = jnp.concatenate([dglu * sb, dglu * a * sb * (1.0 - sb)], axis=1).astype(BF16)
        dwg_ref[...] += _dot_tn(g0, dz)
        dys_ref[...] = _dot_nt(dz, wg_ref[...]) * _gelu_grad(ys)

    half = jax.ShapeDtypeStruct((seq, RET_WIDTH), F32)
    return _call(
        body, "mix_bwd", (seq // t,),
        [_rows(t, D_MODEL), _rows(t, D_MODEL), _full((1, D_MODEL)), _rows(t, SSM_WIDTH), _rows(t, RET_WIDTH),
         _full((SSM_WIDTH, 2 * SSM_WIDTH)), _full((D_MODEL, D_MODEL))],
        [_rows(t, RET_WIDTH), _rows(t, SSM_WIDTH), _full((D_MODEL, D_MODEL)), _full((SSM_WIDTH, 2 * SSM_WIDTH)),
         _full((1, D_MODEL))],
        [half, half, jax.ShapeDtypeStruct((D_MODEL, D_MODEL), F32), jax.ShapeDtypeStruct((SSM_WIDTH, 2 * SSM_WIDTH), F32),
         jax.ShapeDtypeStruct((1, D_MODEL), F32)],
        [], (dx1, mix, gain, y_ssm, y_ret, w_glu, w_out), exchange)


def _mlp_a(x1, target, gain_pre, gain_post, w1, w2):
    seq = x1.shape[0]
    t = MLP_TILE

    def body(x1_ref, tg_ref, gp_ref, gq_ref, w1_ref, w2_ref, df_ref, dx2_ref, dw2_ref, dgq_ref, sq_ref):
        @pl.when(pl.program_id(0) == 0)
        def _():
            dw2_ref[...] = jnp.zeros_like(dw2_ref)
            dgq_ref[...] = jnp.zeros_like(dgq_ref)
            sq_ref[...] = jnp.zeros_like(sq_ref)

        x1v = x1_ref[...]
        n3, _ = _rms(x1v)
        h = (n3 * gp_ref[...]).astype(BF16)
        rl = jnp.maximum(_dot(h, w1_ref[...]), 0.0)
        act = (rl * rl).astype(BF16)
        n4, r4 = _rms(_dot(act, w2_ref[...]))
        gq = gq_ref[...]
        err = x1v + n4 * gq - tg_ref[...]
        sq_ref[...] += jnp.sum(err * err, axis=0, keepdims=True)
        dx2 = err * (1.0 / D_MODEL)
        dx2_ref[...] = dx2
        dm, dgs = _rms_bwd(n4, r4, gq, dx2)
        dgq_ref[...] += dgs
        dmb = dm.astype(BF16)
        dw2_ref[...] += _dot_tn(act, dmb)
        df_ref[...] = (_dot_nt(dmb, w2_ref[...]) * (2.0 * rl)).astype(BF16)

    return pl.pallas_call(
        body, name="mlp_a", grid=(seq // t,),
        in_specs=[_rows(t, D_MODEL), _rows(t, D_MODEL), _full((1, D_MODEL)), _full((1, D_MODEL)),
                  _full((D_MODEL, D_FF)), _full((D_FF, D_MODEL))],
        out_specs=[_rows(t, D_FF), _rows(t, D_MODEL), _full((D_FF, D_MODEL)), _full((1, D_MODEL)), _full((1, D_MODEL))],
        out_shape=[jax.ShapeDtypeStruct((seq, D_FF), BF16), jax.ShapeDtypeStruct((seq, D_MODEL), F32),
                   jax.ShapeDtypeStruct((D_FF, D_MODEL), F32), jax.ShapeDtypeStruct((1, D_MODEL), F32),
                   jax.ShapeDtypeStruct((1, D_MODEL), F32)],
        compiler_params=_params("arbitrary"),
    )(x1, target, gain_pre, gain_post, w1, w2)


def _mlp_b(df, dx2, x1, gain_pre, w1):
    seq = x1.shape[0]
    t = TOK_TILE

    def body(df_ref, dx2_ref, x1_ref, gp_ref, w1_ref, dx1_ref, dw1_ref, dgp_ref):
        @pl.when(pl.program_id(0) == 0)
        def _():
            dw1_ref[...] = jnp.zeros_like(dw1_ref)
            dgp_ref[...] = jnp.zeros_like(dgp_ref)

        n3, r3 = _rms(x1_ref[...])
        gp = gp_ref[...]
        h = (n3 * gp).astype(BF16)
        dfv = df_ref[...]
        dw1_ref[...] += _dot_tn(h, dfv)
        dx, dgs = _rms_bwd(n3, r3, gp, _dot_nt(dfv, w1_ref[...]))
        dgp_ref[...] += dgs
        dx1_ref[...] = dx2_ref[...] + dx

    return pl.pallas_call(
        body, name="mlp_b", grid=(seq // t,),
        in_specs=[_rows(t, D_FF), _rows(t, D_MODEL), _rows(t, D_MODEL), _full((1, D_MODEL)), _full((D_MODEL, D_FF))],
        out_specs=[_rows(t, D_MODEL), _full((D_MODEL, D_FF)), _full((1, D_MODEL))],
        out_shape=[jax.ShapeDtypeStruct((seq, D_MODEL), F32), jax.ShapeDtypeStruct((D_MODEL, D_FF), F32),
                   jax.ShapeDtypeStruct((1, D_MODEL), F32)],
        compiler_params=_params("arbitrary"),
    )(df, dx2, x1, gain_pre, w1)


def _local_step(x, target, small, weights, ids=None, late_state=None):
    cos2, sin2 = _rope_tables(x.shape[0])
    ret_consts = _ret_consts()

    s5_names = ("ssm_lambda_re", "ssm_lambda_im", "ssm_log_dt", "ssm_b_re", "ssm_b_im", "ssm_c_re", "ssm_c_im")
    (lbr, lbi, b_tiles, c_tiles), disc_vjp = jax.vjp(_s5_discretise, *[small[name] for name in s5_names])
    tables = _s5_tables(lbr, lbi)
    b_blk = b_tiles.astype(BF16)
    c_blk = c_tiles.astype(BF16)
    d_skip = small["ssm_d"].reshape(1, SSM_WIDTH)
    gn_gain = small["ret_gn_gain"].reshape(1, RET_WIDTH)
    g_mix_pre = small["norm_mix_pre"].reshape(1, D_MODEL)
    g_mix_post = small["norm_mix_post"].reshape(1, D_MODEL)
    g_mlp_pre = small["norm_mlp_pre"].reshape(1, D_MODEL)
    g_mlp_post = small["norm_mlp_post"].reshape(1, D_MODEL)

    dist = ids is not None
    w_in, w_glu, w_out, w_ff1, w_ff2 = weights
    (q, k, v, gate, u), got = _inproj_fwd(x, g_mix_pre, w_in, cos2, sin2, _x_gather_send((1, 2), [w_glu, w_out]) if dist else None)
    if dist:
        w_glu, w_out = got
    (y_ssm, starts), got = _s5_fwd(u, b_blk, c_blk, tables, d_skip,
                                   _x_join(_x_gather_send((3,), [w_ff1]), _x_gather_send((4,), [w_ff2], (0, 2))) if dist else None)
    if dist:
        w_ff1, w_ff2 = got
    (y_ret, rprev), got = _ret_fwd(q, k, v, gate, gn_gain, ret_consts,
                                   _x_join(_x_gather_pass((1, 2, 3), [w_glu, w_out, w_ff1]),
                                           _x_gather_send((4,), [w_ff2], (1, 2))) if dist else None)
    if dist:
        w_glu, w_out, w_ff1, w_ff2 = got
    (x1, mix), got = _mix_fwd(y_ssm, y_ret, x, g_mix_post, w_glu, w_out, _x_gather_pass((4,), [w_ff2]) if dist else None)
    if dist:
        w_ff2, = got
    df, dx2, d_ff2, dg_mlp_post, sq = _mlp_a(x1, target, g_mlp_pre, g_mlp_post, w_ff1, w_ff2)
    dx1, d_ff1, dg_mlp_pre = _mlp_b(df, dx2, x1, g_mlp_pre, w_ff1)
    (dy_ret, dy_ssm, d_out, d_glu, dg_mix_post), got = _mix_bwd(dx1, mix, g_mix_post, y_ssm, y_ret, w_glu, w_out,
                                                               _x_pair(LATE, [d_ff1, d_ff2]) if dist else None)
    if dist:
        sums, sums_bf16 = _pair_sum(ids, LATE, [d_ff1, d_ff2], got)
    (du, dd, dlam8, db_tiles, dc_tiles), got = _s5_bwd(
        u, dy_ssm, b_blk, c_blk, tables, d_skip, starts,
        _x_join(_x_chip(LATE, sums_bf16), _x_pair(MID, [d_glu, d_out])) if dist else None)
    if dist:
        parts = _chip_sum(ids, LATE, sums, got[:len(LATE)])
        sums, sums_bf16 = _pair_sum(ids, MID, [d_glu, d_out], got[len(LATE):])
    (dq, dk, dv, dgate, dgn), got = _ret_bwd(q, k, v, gate, gn_gain, ret_consts, rprev, dy_ret, cos2, sin2,
                                             _x_join(_x_share(LATE, parts), _x_chip(MID, sums_bf16)) if dist else None)
    if dist:
        late_grads = got[:len(LATE)]
        parts = _chip_sum(ids, MID, sums, got[len(LATE):])
        mid_parts = parts
    adam = (late_state[0], late_grads, late_state[1], late_state[2]) if dist else ()
    (dx, d_in, dg_mix_pre), late_stepped = _inproj_bwd(x, g_mix_pre, w_in, [dq, dk, dv, dgate, du], dx1, adam)

    dlam = jnp.sum(dlam8, axis=1)
    s5_grads = disc_vjp((dlam[0].reshape(SSM_GROUPS, SSM_STATE), dlam[1].reshape(SSM_GROUPS, SSM_STATE), db_tiles, dc_tiles))
    small_grads = {
        "norm_mix_pre": dg_mix_pre, "norm_mix_post": dg_mix_post, "ret_gn_gain": dgn, "ssm_d": dd,
        "norm_mlp_pre": dg_mlp_pre, "norm_mlp_post": dg_mlp_post, "loss_sum": 0.5 * jnp.sum(sq) / D_MODEL,
    }
    small_grads.update(dict(zip(s5_names, s5_grads)))
    if dist:
        return dx, d_in, mid_parts, late_grads, late_stepped, small_grads
    return dx, (d_in, d_glu, d_out, d_ff1, d_ff2), small_grads


BIG = (
    ("w_in", D_MODEL, IN_COLS, 1),
    ("w_glu", SSM_WIDTH, 2 * SSM_WIDTH, 1),
    ("w_out", D_MODEL, D_MODEL, 0),
    ("w_ff1", D_MODEL, D_FF, 1),
    ("w_ff2", D_FF, D_MODEL, 0),
)
SMALL = (
    ("norm_mix_pre", (D_MODEL,)), ("norm_mix_post", (D_MODEL,)), ("ret_gn_gain", (RET_WIDTH,)),
    ("ssm_lambda_re", (SSM_GROUPS, SSM_STATE)), ("ssm_lambda_im", (SSM_GROUPS, SSM_STATE)), ("ssm_log_dt", (SSM_GROUPS,)),
    ("ssm_b_re", (SSM_GROUPS, SSM_STATE, SSM_GROUP)), ("ssm_b_im", (SSM_GROUPS, SSM_STATE, SSM_GROUP)),
    ("ssm_c_re", (SSM_GROUPS, SSM_GROUP, SSM_STATE)), ("ssm_c_im", (SSM_GROUPS, SSM_GROUP, SSM_STATE)),
    ("ssm_d", (SSM_WIDTH,)), ("norm_mlp_pre", (D_MODEL,)), ("norm_mlp_post", (D_MODEL,)),
)
SMALL_ROWS = 1152
EXCH_TILES = 8


PACKED = SMALL + (("loss_sum", (1,)),)


def _packed_rows(shape):
    return -(-math.prod(shape) // (SUBLANES * LANES)) * SUBLANES


def _pack_small(tree):
    rows = []
    for name, shape in PACKED:
        size = math.prod(shape)
        flat = tree[name].reshape(-1).astype(F32) if name in tree else jnp.zeros((size,), F32)
        rows.append(jnp.pad(flat, (0, _packed_rows(shape) * LANES - size)).reshape(-1, LANES))
    used = sum(r.shape[0] for r in rows)
    rows.append(jnp.zeros((SMALL_ROWS - used, LANES), F32))
    return jnp.concatenate(rows, axis=0)


def _unpack_small(packed):
    out, row = {}, 0
    for name, shape in PACKED:
        size = math.prod(shape)
        out[name] = packed[row:row + _packed_rows(shape)].reshape(-1)[:size].reshape((1,) + shape)
        row += _packed_rows(shape)
    return out


def _half_shape(r, c, axis):
    return (r // 2, c) if axis == 1 else (r, c // 2)


def _region_shape(r, c, axis):
    return (r // 2, c // N_CHIPS) if axis == 1 else (r // N_CHIPS, c // 2)


def _shard_shape(r, c, axis):
    return (r, c // N_CHIPS) if axis == 1 else (r // N_CHIPS, c)


def _ds(start, size):
    return pl.ds(pl.multiple_of(start * size, size), size)


def _region_of_full(ref, r, c, axis, shard, half):
    if axis == 1:
        return ref.at[_ds(half, r // 2), _ds(shard, c // N_CHIPS)]
    return ref.at[_ds(shard, r // N_CHIPS), _ds(half, c // 2)]


def _half_of_full(ref, r, c, axis, half):
    if axis == 1:
        return ref.at[_ds(half, r // 2), :]
    return ref.at[:, _ds(half, c // 2)]


def _half_of_shard(ref, r, c, axis, half):
    if axis == 1:
        return ref.at[_ds(half, r // 2), :]
    return ref.at[:, _ds(half, c // 2)]


def _region_of_half(ref, r, c, axis, shard):
    if axis == 1:
        return ref.at[:, _ds(shard, c // N_CHIPS)]
    return ref.at[_ds(shard, r // N_CHIPS), :]


def _place():
    x, y, c = lax.axis_index("x"), lax.axis_index("y"), lax.axis_index("c")
    chips = [(1 - x, y), (x, 1 - y), (1 - x, 1 - y)]
    return x, y, c, chips


LATE = (3, 4)
MID = (1, 2)
FIRST = (0,)
SMALL_HALF = (SMALL_ROWS // 2, LANES)


def _remote(src, dst, send_sem, recv_sem, to):
    return pltpu.make_async_remote_copy(src_ref=src, dst_ref=dst, send_sem=send_sem, recv_sem=recv_sem,
                                        device_id=to, device_id_type=MESH)


def _same(arrays):
    return [jax.ShapeDtypeStruct(a.shape, a.dtype) for a in arrays]


def _x_gather_send(ws, fulls, part=(0, 1)):
    def copies(ins, outs, send_sems, recv_sems):
        x, y, c, chips = _place()
        out = []
        for j, w in enumerate(ws):
            _, r, cc, axis = BIG[w]
            mine = _region_of_full(outs[j], r, cc, axis, 2 * x + y, c)
            rows = _region_shape(r, cc, axis)[0] // part[1]
            mine = mine.at[pl.ds(part[0] * rows, rows), :]
            out += [_remote(mine, mine, send_sems.at[3 * j + k], recv_sems.at[3 * j + k], (cx, cy, c))
                    for k, (cx, cy) in enumerate(chips)]
        return out

    return _Exchange(fulls, _same(fulls), {j: j for j in range(len(ws))}, 3 * len(ws), copies)


def _x_gather_pass(ws, fulls):
    def copies(ins, outs, send_sems, recv_sems):
        x, y, c, chips = _place()
        out = []
        for j, w in enumerate(ws):
            _, r, cc, axis = BIG[w]
            for k, (cx, cy) in enumerate(chips):
                landed = _region_of_full(outs[j], r, cc, axis, 2 * cx + cy, c)
                out.append(_remote(landed, landed, send_sems.at[3 * j + k], recv_sems.at[3 * j + k], (x, y, 1 - c)))
        return out

    return _Exchange(fulls, _same(fulls), {j: j for j in range(len(ws))}, 3 * len(ws), copies)


def _x_pair(ws, grads, small=None):
    def copies(ins, outs, send_sems, recv_sems):
        x, y, c, _ = _place()
        out = []
        for j, w in enumerate(ws):
            _, r, cc, axis = BIG[w]
            out.append(_remote(_half_of_full(ins[j], r, cc, axis, 1 - c), outs[j], send_sems.at[j], recv_sems.at[j], (x, y, 1 - c)))
        if small is not None:
            j = len(ws)
            out.append(_remote(ins[j].at[_ds(1 - c, SMALL_ROWS // 2), :], outs[j], send_sems.at[j], recv_sems.at[j], (x, y, 1 - c)))
        return out

    shapes = [jax.ShapeDtypeStruct(_half_shape(*BIG[w][1:]), F32) for w in ws]
    extra = [] if small is None else [small]
    return _Exchange(list(grads) + extra, shapes + [jax.ShapeDtypeStruct(SMALL_HALF, F32)] * len(extra), {},
                     len(ws) + len(extra), copies)


def _x_chip(ws, sums_bf16, small_sum=None):
    def copies(ins, outs, send_sems, recv_sems):
        x, y, c, chips = _place()
        out = []
        for j, w in enumerate(ws):
            _, r, cc, axis = BIG[w]
            out += [_remote(_region_of_half(ins[j], r, cc, axis, 2 * cx + cy), outs[j].at[k],
                            send_sems.at[3 * j + k], recv_sems.at[3 * j + k], (cx, cy, c)) for k, (cx, cy) in enumerate(chips)]
        if small_sum is not None:
            j = len(ws)
            out += [_remote(ins[j], outs[j].at[k], send_sems.at[3 * j + k], recv_sems.at[3 * j + k], (cx, cy, c))
                    for k, (cx, cy) in enumerate(chips)]
        return out

    shapes = [jax.ShapeDtypeStruct((3,) + _region_shape(*BIG[w][1:]), BF16) for w in ws]
    extra = [] if small_sum is None else [small_sum]
    return _Exchange(list(sums_bf16) + extra, shapes + [jax.ShapeDtypeStruct((3,) + SMALL_HALF, F32)] * len(extra), {},
                     3 * (len(ws) + len(extra)), copies)


def _x_share(ws, shards, small=None):
    def copies(ins, outs, send_sems, recv_sems):
        x, y, c, _ = _place()
        out = []
        for j, w in enumerate(ws):
            _, r, cc, axis = BIG[w]
            mine = _half_of_shard(outs[j], r, cc, axis, c)
            out.append(_remote(mine, mine, send_sems.at[j], recv_sems.at[j], (x, y, 1 - c)))
        if small is not None:
            j = len(ws)
            mine = outs[j].at[_ds(c, SMALL_ROWS // 2), :]
            out.append(_remote(mine, mine, send_sems.at[j], recv_sems.at[j], (x, y, 1 - c)))
        return out

    arrays = list(shards) + ([] if small is None else [small])
    return _Exchange(arrays, _same(arrays), {j: j for j in range(len(arrays))}, len(arrays), copies)


class _Offset:
    def __init__(self, sems, base):
        self._sems, self._base = sems, base

    @property
    def at(self):
        return self

    def __getitem__(self, i):
        return self._sems.at[self._base + i]


def _x_join(a, b):
    ka_in, ka_out = len(a.operands), len(a.out_shapes)

    def copies(ins, outs, send_sems, recv_sems):
        return (a.copies(ins[:ka_in], outs[:ka_out], send_sems, recv_sems)
                + b.copies(ins[ka_in:], outs[ka_out:], _Offset(send_sems, a.n), _Offset(recv_sems, a.n)))

    aliases = dict(a.aliases)
    aliases.update({ka_in + i: ka_out + o for i, o in b.aliases.items()})
    return _Exchange(a.operands + b.operands, a.out_shapes + b.out_shapes, aliases, a.n + b.n, copies)


def _run_exchange(name, exchange):
    k_in, k_out = len(exchange.operands), len(exchange.out_shapes)

    def body(*refs):
        copies = exchange.copies(refs[:k_in], refs[k_in:k_in + k_out], refs[-2], refs[-1])
        for cp in copies:
            cp.start()
        for cp in copies:
            cp.wait()

    return list(pl.pallas_call(
        body, name=name, in_specs=[ANY] * k_in, out_specs=[ANY] * k_out, out_shape=exchange.out_shapes,
        scratch_shapes=[pltpu.SemaphoreType.DMA((exchange.n,)), pltpu.SemaphoreType.DMA((exchange.n,))],
        input_output_aliases=dict(exchange.aliases),
    )(*exchange.operands))


def _tile_rows(shape):
    return (shape[0] // EXCH_TILES, shape[1])


def _cast_place(ids, shards):
    def body(ids_ref, *refs):
        for j in range(len(BIG)):
            refs[len(BIG) + j][...] = refs[j][...].astype(BF16)

    def out_spec(w):
        _, r, cc, axis = BIG[w]
        tile = _tile_rows(_shard_shape(r, cc, axis))
        if axis == 1:
            return pl.BlockSpec(tile, lambda i, ids: (i, ids[1]))
        return pl.BlockSpec(tile, lambda i, ids: (ids[1] * EXCH_TILES + i, 0))

    return list(pl.pallas_call(
        body, name="cast_place",
        grid_spec=pltpu.PrefetchScalarGridSpec(
            num_scalar_prefetch=1, grid=(EXCH_TILES,),
            in_specs=[pl.BlockSpec(_tile_rows(_shard_shape(r, cc, axis)), lambda i, ids: (i, 0)) for _, r, cc, axis in BIG],
            out_specs=[out_spec(w) for w in range(len(BIG))]),
        out_shape=[jax.ShapeDtypeStruct((r, cc), BF16) for _, r, cc, _ in BIG],
        compiler_params=pltpu.CompilerParams(dimension_semantics=("parallel",)),
    )(ids, *shards))


def _pair_sum(ids, ws, grads, got, small=None):
    nb = len(ws)
    n = nb + (small is not None)
    halves = [_half_shape(*BIG[w][1:]) for w in ws] + [SMALL_HALF] * (n - nb)

    def body(ids_ref, *refs):
        for j in range(n):
            total = refs[j][...] + refs[n + j][...]
            refs[2 * n + j][...] = total
            if j < nb:
                refs[3 * n + j][...] = total.astype(BF16)

    def mine_spec(j):
        tile = _tile_rows(halves[j])
        if j < nb and BIG[ws[j]][3] == 0:
            return pl.BlockSpec(tile, lambda i, ids: (i, ids[0]))
        return pl.BlockSpec(tile, lambda i, ids: (ids[0] * EXCH_TILES + i, 0))

    plain = lambda j: pl.BlockSpec(_tile_rows(halves[j]), lambda i, ids: (i, 0))
    outs = pl.pallas_call(
        body, name="pair_sum",
        grid_spec=pltpu.PrefetchScalarGridSpec(
            num_scalar_prefetch=1, grid=(EXCH_TILES,),
            in_specs=[mine_spec(j) for j in range(n)] + [plain(j) for j in range(n)],
            out_specs=[plain(j) for j in range(n)] + [plain(j) for j in range(nb)]),
        out_shape=[jax.ShapeDtypeStruct(h, F32) for h in halves] + [jax.ShapeDtypeStruct(h, BF16) for h in halves[:nb]],
        compiler_params=pltpu.CompilerParams(dimension_semantics=("parallel",)),
    )(ids, *grads, *([] if small is None else [small]), *got)
    return list(outs[:n]), list(outs[n:])


def _chip_sum(ids, ws, sums, arrived):
    nb, n = len(ws), len(sums)
    regions = [_region_shape(*BIG[w][1:]) for w in ws] + [SMALL_HALF] * (n - nb)

    def body(ids_ref, *refs):
        for j in range(n):
            own, got, out = refs[j], refs[n + j], refs[2 * n + j]
            if j < nb:
                out[...] = ((own[...] + got[0].astype(F32)) + got[1].astype(F32)) + got[2].astype(F32)
            else:
                out[...] = (own[...] + got[1]) + (got[0] + got[2])

    def own_spec(j):
        tile = _tile_rows(regions[j])
        if j >= nb:
            return pl.BlockSpec(tile, lambda i, ids: (i, 0))
        if BIG[ws[j]][3] == 1:
            return pl.BlockSpec(tile, lambda i, ids: (i, ids[1]))
        return pl.BlockSpec(tile, lambda i, ids: (ids[1] * EXCH_TILES + i, 0))

    def out_spec(j):
        tile = _tile_rows(regions[j])
        if j < nb and BIG[ws[j]][3] == 0:
            return pl.BlockSpec(tile, lambda i, ids: (i, ids[0]))
        return pl.BlockSpec(tile, lambda i, ids: (ids[0] * EXCH_TILES + i, 0))

    got_spec = lambda j: pl.BlockSpec((3,) + _tile_rows(regions[j]), lambda i, ids: (0, i, 0))
    return list(pl.pallas_call(
        body, name="chip_sum",
        grid_spec=pltpu.PrefetchScalarGridSpec(
            num_scalar_prefetch=1, grid=(EXCH_TILES,),
            in_specs=[own_spec(j) for j in range(n)] + [got_spec(j) for j in range(n)],
            out_specs=[out_spec(j) for j in range(n)]),
        out_shape=[jax.ShapeDtypeStruct(_shard_shape(*BIG[w][1:]), F32) for w in ws]
        + [jax.ShapeDtypeStruct((SMALL_ROWS, LANES), F32)] * (n - nb),
        compiler_params=pltpu.CompilerParams(dimension_semantics=("parallel",)),
    )(ids, *sums, *arrived))


def _adamw_body(n):
    c1 = 1.0 - ADAM_B1 ** ADAM_STEP
    c2 = 1.0 - ADAM_B2 ** ADAM_STEP

    def body(*refs):
        for j in range(n):
            w, g, m, v = (refs[k * n + j][...] for k in range(4))
            m = ADAM_B1 * m + (1.0 - ADAM_B1) * g
            v = ADAM_B2 * v + (1.0 - ADAM_B2) * (g * g)
            refs[4 * n + j][...] = -ADAM_LR * ((m / c1) / (jnp.sqrt(v / c2) + ADAM_EPS) + ADAM_WD * w)
            refs[5 * n + j][...] = m
            refs[6 * n + j][...] = v

    return body


def _adamw(name, ws, gs, ms, vs):
    n = len(ws)
    specs = [pl.BlockSpec(_tile_rows(w.shape), lambda i: (i, 0)) for w in ws]
    shapes = [jax.ShapeDtypeStruct(w.shape, F32) for w in ws]
    out, _ = _call(_adamw_body(n), name, (EXCH_TILES,), specs * 4, specs * 3, shapes * 3, [], (*ws, *gs, *ms, *vs))
    return out[:n], out[n:2 * n], out[2 * n:]


def _adamw_whole(name, ws, gs, ms, vs):
    n = len(ws)
    out = pl.pallas_call(_adamw_body(n), name=name, out_shape=[jax.ShapeDtypeStruct(w.shape, F32) for w in ws] * 3)(*ws, *gs, *ms, *vs)
    return out[:n], out[n:2 * n], out[2 * n:]


def kernel(x, norm_mix_pre, norm_mix_post, w_in, ret_gn_gain, ssm_lambda_re, ssm_lambda_im, ssm_log_dt, ssm_b_re, ssm_b_im, ssm_c_re, ssm_c_im, ssm_d, w_glu, w_out, norm_mlp_pre, norm_mlp_post, w_ff1, w_ff2, loss_target, m_norm_mix_pre, m_norm_mix_post, m_w_in, m_ret_gn_gain, m_ssm_lambda_re, m_ssm_lambda_im, m_ssm_log_dt, m_ssm_b_re, m_ssm_b_im, m_ssm_c_re, m_ssm_c_im, m_ssm_d, m_w_glu, m_w_out, m_norm_mlp_pre, m_norm_mlp_post, m_w_ff1, m_w_ff2, v_norm_mix_pre, v_norm_mix_post, v_w_in, v_ret_gn_gain, v_ssm_lambda_re, v_ssm_lambda_im, v_ssm_log_dt, v_ssm_b_re, v_ssm_b_im, v_ssm_c_re, v_ssm_c_im, v_ssm_d, v_w_glu, v_w_out, v_norm_mlp_pre, v_norm_mlp_post, v_w_ff1, v_w_ff2):
    given = dict(locals())
    order = ["norm_mix_pre", "norm_mix_post", "w_in", "ret_gn_gain", "ssm_lambda_re", "ssm_lambda_im", "ssm_log_dt",
             "ssm_b_re", "ssm_b_im", "ssm_c_re", "ssm_c_im", "ssm_d", "w_glu", "w_out", "norm_mlp_pre", "norm_mlp_post",
             "w_ff1", "w_ff2"]
    big_names = [name for name, _, _, _ in BIG]
    ids = jnp.stack([lax.axis_index("c"), 2 * lax.axis_index("x") + lax.axis_index("y")]).astype(jnp.int32)

    weights = _cast_place(ids, [given[name][0] for name in big_names])
    w_in = _run_exchange("gather_w_in_send", _x_gather_send((0,), weights[:1]))
    w_in = _run_exchange("gather_w_in_pass", _x_gather_pass((0,), w_in))
    small_w = {name: given[name][0] for name, _ in SMALL}
    late_state = [[given[prefix + big_names[w]][0] for w in LATE] for prefix in ("", "m_", "v_")]
    dx, d_in, mid_parts, late_grads, late_stepped, small_grads = _local_step(
        x[0], loss_target[0], small_w, w_in + weights[1:], ids, late_state)

    small_packed = _pack_small(small_grads)
    got = _run_exchange("pair_exchange", _x_join(_x_share(MID, mid_parts), _x_pair(FIRST, [d_in], small_packed)))
    mid_grads, got = got[:len(MID)], got[len(MID):]
    sums, sums_bf16 = _pair_sum(ids, FIRST, [d_in], got, small_packed)
    arrived = _run_exchange("chip_exchange", _x_chip(FIRST, sums_bf16, sums[-1]))
    parts = _chip_sum(ids, FIRST, sums, arrived)
    g_in, g_small = _run_exchange("pair_share", _x_share(FIRST, parts[:-1], parts[-1]))
    big_grads = [g_in] + mid_grads + late_grads
    rest = lambda prefix: [given[prefix + big_names[w]][0] for w in FIRST + MID]
    rest_out = _adamw("adamw_big", rest(""), big_grads[:len(FIRST + MID)], rest("m_"), rest("v_"))
    big_out = [list(rest_out[k]) + list(late_stepped[k]) for k in range(3)]
    small_names = [name for name, _ in SMALL]
    small_tree = _unpack_small(g_small)
    small_grads = [small_tree[name] for name in small_names]
    small = lambda prefix: [given[prefix + name] for name in small_names]
    small_out = _adamw_whole("adamw_small", small(""), small_grads, small("m_"), small("v_"))

    def in_order(big_arrays, small_arrays):
        tree = {name: a[None] for name, a in zip(big_names, big_arrays)}
        tree.update(zip(small_names, small_arrays))
        return [tree[name] for name in order]

    loss = small_tree["loss_sum"].reshape(())
    return (loss, dx[None], *in_order(big_grads, small_grads), *in_order(big_out[0], small_out[0]),
            *in_order(big_out[1], small_out[1]), *in_order(big_out[2], small_out[2]))
```

```python
import functools
import math

import jax
import jax.numpy as jnp
import numpy as np
from jax import lax
from jax.experimental import pallas as pl
from jax.experimental.pallas import tpu as pltpu

F32 = jnp.float32
BF16 = jnp.bfloat16

D_MODEL = 1024
RET_WIDTH = 512
RET_HEADS = 4
HEAD_DIM = 128
RET_CHUNK = 128
ROPE_BASE = 10000.0
SSM_WIDTH = 512
SSM_GROUP = 16
SSM_GROUPS = 32
SSM_STATE = 64
N_STATES = SSM_GROUPS * SSM_STATE
D_FF = 4096
IN_COLS = 4 * RET_WIDTH + SSM_WIDTH
NORM_EPS = 1e-6
K_SCALE = HEAD_DIM ** -0.5

ADAM_LR = 0.001
ADAM_B1 = 0.9
ADAM_B2 = 0.999
ADAM_EPS = 1e-08
ADAM_WD = 0.01
ADAM_STEP = 10

LANES = 128
SUBLANES = 8
VMEM_LIMIT = 56 * 2 ** 20

TOK_TILE = 512
MLP_TILE = 256
RET_CHUNKS_PER_STEP = 4
S5_TILE = 256
S5_STEPS = S5_TILE // SUBLANES
S5_COLBLK = N_STATES // LANES
S5_GROUP = 4

N_CHIPS = 4
MESH = pl.DeviceIdType.MESH


def _dot(a, b):
    return jnp.dot(a, b, preferred_element_type=F32)


def _dot_nt(a, b):
    return lax.dot_general(a, b, (((1,), (1,)), ((), ())), preferred_element_type=F32)


def _dot_tn(a, b):
    return lax.dot_general(a, b, (((0,), (0,)), ((), ())), preferred_element_type=F32)


def _sigmoid(x):
    return 1.0 / (1.0 + jnp.exp(-x))


_GELU_C = math.sqrt(2.0 / math.pi)


def _gelu(x):
    return 0.5 * x * (1.0 + jnp.tanh(_GELU_C * (x + 0.044715 * (x * x * x))))


def _gelu_grad(x):
    t = jnp.tanh(_GELU_C * (x + 0.044715 * (x * x * x)))
    return 0.5 * (1.0 + t) + 0.5 * x * (1.0 - t * t) * (_GELU_C * (1.0 + 3.0 * 0.044715 * (x * x)))


def _rms(x):
    r = lax.rsqrt(jnp.mean(x * x, axis=-1, keepdims=True) + NORM_EPS)
    return x * r, r


def _rms_bwd(n, r, gain, dy):
    dn = dy * gain
    dx = r * (dn - n * jnp.mean(dn * n, axis=-1, keepdims=True))
    return dx, jnp.sum(dy * n, axis=0, keepdims=True)


def _full(shape):
    nd = len(shape)
    return pl.BlockSpec(shape, lambda i, _n=nd: (0,) * _n, pipeline_mode=pl.Buffered(1))


def _rows(tile, width):
    return pl.BlockSpec((tile, width), lambda i: (i, 0))


def _rows_rev(tile, width, n):
    return pl.BlockSpec((tile, width), lambda i, _n=n: (_n - 1 - i, 0))


def _params(sem):
    return pltpu.CompilerParams(dimension_semantics=(sem,), vmem_limit_bytes=VMEM_LIMIT)


ANY = pl.BlockSpec(memory_space=pl.ANY)


class _Exchange:
    def __init__(self, operands, out_shapes, aliases, n, copies):
        self.operands, self.out_shapes, self.aliases, self.n, self.copies = list(operands), list(out_shapes), aliases, n, copies


def _call(body, name, grid, in_specs, out_specs, out_shape, scratch_shapes, args, exchange=None):
    if exchange is None:
        outs = pl.pallas_call(body, name=name, grid=grid, in_specs=in_specs, out_specs=out_specs, out_shape=out_shape,
                              scratch_shapes=scratch_shapes, compiler_params=_params("arbitrary"))(*args)
        return list(outs), []
    n_in, n_out, n_scr = len(in_specs), len(out_specs), len(scratch_shapes)
    k_in, k_out = len(exchange.operands), len(exchange.out_shapes)
    last = grid[0] - 1

    def hosted(*refs):
        own_in, rest = refs[:n_in], refs[n_in:]
        ex_in, rest = rest[:k_in], rest[k_in:]
        own_out, rest = rest[:n_out], rest[n_out:]
        ex_out, rest = rest[:k_out], rest[k_out:]
        own_scr, (send_sems, recv_sems) = rest[:n_scr], rest[n_scr:]

        @pl.when(pl.program_id(0) == 0)
        def _():
            for cp in exchange.copies(ex_in, ex_out, send_sems, recv_sems):
                cp.start()

        body(*own_in, *own_out, *own_scr)

        @pl.when(pl.program_id(0) == last)
        def _():
            for cp in exchange.copies(ex_in, ex_out, send_sems, recv_sems):
                cp.wait()

    outs = pl.pallas_call(
        hosted, name=name, grid=grid, in_specs=list(in_specs) + [ANY] * k_in, out_specs=list(out_specs) + [ANY] * k_out,
        out_shape=list(out_shape) + exchange.out_shapes,
        scratch_shapes=list(scratch_shapes) + [pltpu.SemaphoreType.DMA((exchange.n,)), pltpu.SemaphoreType.DMA((exchange.n,))],
        input_output_aliases={n_in + i: n_out + o for i, o in exchange.aliases.items()},
        compiler_params=_params("arbitrary"),
    )(*args, *exchange.operands)
    return list(outs[:n_out]), list(outs[n_out:])


def _rope(t, cos2, sin2):
    return t * cos2 + pltpu.roll(t, HEAD_DIM // 2, 1) * sin2


def _rope_bwd(d, cos2, sin2):
    return d * cos2 + pltpu.roll(d * sin2, HEAD_DIM // 2, d.ndim - 1)


def _inproj_fwd(x, gain, w_in, cos2, sin2, exchange=None):
    seq = x.shape[0]

    def body(x_ref, g_ref, w_ref, c_ref, s_ref, q_ref, k_ref, v_ref, gate_ref, u_ref):
        n, _ = _rms(x_ref[...])
        h = (n * g_ref[...]).astype(BF16)
        proj = _dot(h, w_ref[...])
        c = c_ref[...]
        s = s_ref[...]
        for hd in range(RET_HEADS):
            lo = hd * HEAD_DIM
            q_ref[:, lo:lo + HEAD_DIM] = _rope(proj[:, lo:lo + HEAD_DIM], c, s).astype(BF16)
            kh = proj[:, RET_WIDTH + lo:RET_WIDTH + lo + HEAD_DIM]
            k_ref[:, lo:lo + HEAD_DIM] = (_rope(kh, c, s) * K_SCALE).astype(BF16)
        v_ref[...] = proj[:, 2 * RET_WIDTH:3 * RET_WIDTH].astype(BF16)
        gate_ref[...] = proj[:, 3 * RET_WIDTH:4 * RET_WIDTH]
        u_ref[...] = proj[:, 4 * RET_WIDTH:]

    t = TOK_TILE
    half = lambda dt: jax.ShapeDtypeStruct((seq, RET_WIDTH), dt)
    return _call(
        body, "inproj_fwd", (seq // t,),
        [_rows(t, D_MODEL), _full((1, D_MODEL)), _full((D_MODEL, IN_COLS)), _rows(t, HEAD_DIM), _rows(t, HEAD_DIM)],
        [_rows(t, RET_WIDTH)] * 5, [half(BF16), half(BF16), half(BF16), half(F32), half(F32)], [],
        (x, gain, w_in, cos2, sin2), exchange)


def _inproj_bwd(x, gain, w_in, dpieces, dres, adam=()):
    seq = x.shape[0]
    t = TOK_TILE
    steps = seq // t
    n_pieces = len(dpieces)
    n_adam = len(adam[0]) if adam else 0
    adam_step = _adamw_body(n_adam)

    def body(x_ref, g_ref, w_ref, *refs):
        piece_refs, (dres_ref, *adam_in), (dx_ref, dw_ref, dg_ref, *adam_out) = (
            refs[:n_pieces], refs[n_pieces:n_pieces + 1 + 4 * n_adam], refs[n_pieces + 1 + 4 * n_adam:])

        @pl.when(pl.program_id(0) == 0)
        def _():
            dw_ref[...] = jnp.zeros_like(dw_ref)
            dg_ref[...] = jnp.zeros_like(dg_ref)

        adam_step(*adam_in, *adam_out)

        n, r = _rms(x_ref[...])
        gain_v = g_ref[...]
        h = (n * gain_v).astype(BF16)
        dh = None
        for j, piece in enumerate(piece_refs):
            cols = slice(j * RET_WIDTH, (j + 1) * RET_WIDTH)
            dp = piece[...]
            dw_ref[:, cols] += _dot_tn(h, dp)
            term = _dot_nt(dp, w_ref[:, cols])
            dh = term if dh is None else dh + term
        dx, dgs = _rms_bwd(n, r, gain_v, dh)
        dg_ref[...] += dgs
        dx_ref[...] = dres_ref[...] + dx

    adam_arrays = [a for group in adam for a in group]
    adam_spec = lambda a: pl.BlockSpec((a.shape[0] // steps, a.shape[1]), lambda i: (i, 0))
    outs, _ = _call(
        body, "inproj_bwd", (steps,),
        [_rows(t, D_MODEL), _full((1, D_MODEL)), _full((D_MODEL, IN_COLS))] + [_rows(t, RET_WIDTH)] * n_pieces + [_rows(t, D_MODEL)]
        + [adam_spec(a) for a in adam_arrays],
        [_rows(t, D_MODEL), _full((D_MODEL, IN_COLS)), _full((1, D_MODEL))] + [adam_spec(a) for a in adam_arrays[:n_adam]] * 3,
        [jax.ShapeDtypeStruct((seq, D_MODEL), F32), jax.ShapeDtypeStruct((D_MODEL, IN_COLS), F32),
         jax.ShapeDtypeStruct((1, D_MODEL), F32)] + [jax.ShapeDtypeStruct(a.shape, F32) for a in adam_arrays[:n_adam]] * 3,
        [], (x, gain, w_in, *dpieces, dres, *adam_arrays))
    stepped = outs[3:]
    return outs[:3], (stepped[:n_adam], stepped[n_adam:2 * n_adam], stepped[2 * n_adam:])


def _ret_consts():
    c = RET_CHUNK
    f32 = np.float32
    log_gamma = np.log(f32(1.0) - np.exp(np.linspace(math.log(1.0 / 32), math.log(1.0 / 512), RET_HEADS, dtype=f32))).astype(f32)
    idx = np.arange(c, dtype=f32)
    diff = idx[:, None] - idx[None, :]
    decay = np.where(diff[None] >= 0, np.exp(np.maximum(diff, f32(0.0))[None] * log_gamma[:, None, None]), f32(0.0))
    zeta = np.exp((c - 1 - idx)[None, :] * log_gamma[:, None])
    xi = np.exp((idx + f32(1.0))[None, :] * log_gamma[:, None])
    g_chunk = np.exp(f32(c) * log_gamma)
    wide = lambda rowvals: jnp.asarray(np.broadcast_to(rowvals[:, :, None], (RET_HEADS, c, c)).astype(f32))
    return (jnp.asarray(decay.astype(f32)), wide(xi), wide(zeta),
            jnp.asarray(np.broadcast_to(g_chunk[:, None, None], (RET_HEADS, c, c)).astype(f32)))


def _rope_tables(seq):
    f32 = np.float32
    half = HEAD_DIM // 2
    inv_freq = np.power(f32(ROPE_BASE), -np.arange(half, dtype=f32) / f32(half)).astype(f32)
    ang = (np.arange(seq, dtype=f32)[:, None] * inv_freq[None, :]).astype(f32)
    cos, sin = np.cos(ang).astype(f32), np.sin(ang).astype(f32)
    return jnp.asarray(np.concatenate([cos, cos], axis=1)), jnp.asarray(np.concatenate([-sin, sin], axis=1))


def _bmm(a, b):
    return lax.dot_general(a, b, (((2,), (1,)), ((0,), (0,))), preferred_element_type=F32)


def _bmm_nt(a, b):
    return lax.dot_general(a, b, (((2,), (2,)), ((0,), (0,))), preferred_element_type=F32)


def _bmm_tn(a, b):
    return lax.dot_general(a, b, (((1,), (1,)), ((0,), (0,))), preferred_element_type=F32)


def _ret_blocks(ref):
    return jnp.stack([ref[cc * RET_CHUNK:(cc + 1) * RET_CHUNK, hd * HEAD_DIM:(hd + 1) * HEAD_DIM]
                      for cc in range(RET_CHUNKS_PER_STEP) for hd in range(RET_HEADS)])


def _ret_unblock(ref, blocks):
    for cc in range(RET_CHUNKS_PER_STEP):
        for hd in range(RET_HEADS):
            ref[cc * RET_CHUNK:(cc + 1) * RET_CHUNK, hd * HEAD_DIM:(hd + 1) * HEAD_DIM] = blocks[cc * RET_HEADS + hd].astype(ref.dtype)


def _rope_blocks(ref):
    return jnp.stack([ref[cc * RET_CHUNK:(cc + 1) * RET_CHUNK, :] for cc in range(RET_CHUNKS_PER_STEP) for _ in range(RET_HEADS)])


def _per_chunk(ref):
    return jnp.concatenate([ref[...]] * RET_CHUNKS_PER_STEP, axis=0)


def _ret_gain(gain_ref):
    return jnp.stack([gain_ref[:, hd * HEAD_DIM:(hd + 1) * HEAD_DIM]
                      for _ in range(RET_CHUNKS_PER_STEP) for hd in range(RET_HEADS)])


def _ret_fwd(q, k, v, gate, gn_gain, consts, exchange=None):
    seq = q.shape[0]
    cps = RET_CHUNKS_PER_STEP
    t = cps * RET_CHUNK
    n_chunks = seq // RET_CHUNK
    dec, xi, zeta, gch = consts

    def body(q_ref, k_ref, v_ref, gate_ref, gain_ref, dec_ref, xi_ref, zeta_ref, gch_ref, y_ref, rprev_ref, state):
        @pl.when(pl.program_id(0) == 0)
        def _():
            state[...] = jnp.zeros_like(state)

        qb, kb, vb = _ret_blocks(q_ref), _ret_blocks(k_ref), _ret_blocks(v_ref)
        s = _bmm_nt(qb, kb) * _per_chunk(dec_ref)
        vz = (vb.astype(F32) * _per_chunk(zeta_ref)).astype(BF16)
        incr = _bmm_tn(kb, vz)
        gch_v = gch_ref[...]
        rp = state[...]
        before = []
        for cc in range(cps):
            before.append(rp.astype(BF16))
            rprev_ref[cc] = before[-1]
            rp = gch_v * rp + incr[cc * RET_HEADS:(cc + 1) * RET_HEADS]
        state[...] = rp
        qx = (qb.astype(F32) * _per_chunk(xi_ref)).astype(BF16)
        o = _bmm(s.astype(BF16), vb) + _bmm(qx, jnp.concatenate(before, axis=0))
        oc = o - jnp.mean(o, axis=-1, keepdims=True)
        on = oc * lax.rsqrt(jnp.mean(oc * oc, axis=-1, keepdims=True) + NORM_EPS)
        g = _ret_blocks(gate_ref)
        _ret_unblock(y_ref, g * _sigmoid(g) * (on * _ret_gain(gain_ref)))

    cst = _full((RET_HEADS, RET_CHUNK, RET_CHUNK))
    return _call(
        body, "ret_fwd", (seq // t,),
        [_rows(t, RET_WIDTH)] * 4 + [_full((1, RET_WIDTH)), cst, cst, cst, cst],
        [_rows(t, RET_WIDTH), pl.BlockSpec((cps, RET_HEADS, HEAD_DIM, HEAD_DIM), lambda i: (i, 0, 0, 0))],
        [jax.ShapeDtypeStruct((seq, RET_WIDTH), BF16), jax.ShapeDtypeStruct((n_chunks, RET_HEADS, HEAD_DIM, HEAD_DIM), BF16)],
        [pltpu.VMEM((RET_HEADS, HEAD_DIM, HEAD_DIM), F32)],
        (q, k, v, gate, gn_gain, dec, xi, zeta, gch), exchange)


def _ret_bwd(q, k, v, gate, gn_gain, consts, rprev, dy_ret, cos2, sin2, exchange=None):
    seq = q.shape[0]
    cps = RET_CHUNKS_PER_STEP
    t = cps * RET_CHUNK
    nt = seq // t
    dec, xi, zeta, gch = consts

    def body(q_ref, k_ref, v_ref, gate_ref, gain_ref, dec_ref, xi_ref, zeta_ref, gch_ref, rprev_ref, dyr_ref, cos_ref, sin_ref,
             dq_ref, dk_ref, dv_ref, dgate_ref, dgain_ref, dstate):
        @pl.when(pl.program_id(0) == 0)
        def _():
            dstate[...] = jnp.zeros_like(dstate)
            dgain_ref[...] = jnp.zeros_like(dgain_ref)

        nb = cps * RET_HEADS
        qb, kb, vb = _ret_blocks(q_ref), _ret_blocks(k_ref), _ret_blocks(v_ref)
        dec_b, xi_b, zeta_b = _per_chunk(dec_ref), _per_chunk(xi_ref), _per_chunk(zeta_ref)
        rpb = rprev_ref[...].reshape(nb, HEAD_DIM, HEAD_DIM)
        sdb = (_bmm_nt(qb, kb) * dec_b).astype(BF16)
        qx = (qb.astype(F32) * xi_b).astype(BF16)
        o = _bmm(sdb, vb) + _bmm(qx, rpb)
        oc = o - jnp.mean(o, axis=-1, keepdims=True)
        rstd = lax.rsqrt(jnp.mean(oc * oc, axis=-1, keepdims=True) + NORM_EPS)
        on = oc * rstd
        g = _ret_blocks(gate_ref)
        sg = _sigmoid(g)
        gain_b = _ret_gain(gain_ref)
        dyr = _ret_blocks(dyr_ref)
        _ret_unblock(dgate_ref, dyr * (on * gain_b) * (sg * (1.0 + g * (1.0 - sg))))
        dy = dyr * (g * sg)
        dgain = jnp.sum(dy * on, axis=1, keepdims=True)
        for hd in range(RET_HEADS):
            part = dgain[hd]
            for cc in range(1, cps):
                part = part + dgain[cc * RET_HEADS + hd]
            dgain_ref[:, hd * HEAD_DIM:(hd + 1) * HEAD_DIM] += part
        don = dy * gain_b
        do = rstd * (don - jnp.mean(don, axis=-1, keepdims=True) - on * jnp.mean(don * on, axis=-1, keepdims=True))
        dob = do.astype(BF16)
        ds = (_bmm_nt(dob, vb) * dec_b).astype(BF16)
        dq = _bmm(ds, kb) + _bmm_nt(dob, rpb) * xi_b
        dk = _bmm_tn(ds, qb)
        dv = _bmm_tn(sdb, dob)
        dstate_local = _bmm_tn(qx, dob)
        vz = (vb.astype(F32) * zeta_b).astype(BF16)
        gch_v = gch_ref[...]
        zeta_v = zeta_ref[...]
        gh = dstate[...]
        dk_extra, dv_extra = [None] * cps, [None] * cps
        for cc in reversed(range(cps)):
            sl = slice(cc * RET_HEADS, (cc + 1) * RET_HEADS)
            gb = gh.astype(BF16)
            dk_extra[cc] = _bmm_nt(vz[sl], gb)
            dv_extra[cc] = _bmm(kb[sl], gb) * zeta_v
            gh = dstate_local[sl] + gch_v * gh
        dstate[...] = gh
        cos_b, sin_b = _rope_blocks(cos_ref), _rope_blocks(sin_ref)
        _ret_unblock(dq_ref, _rope_bwd(dq, cos_b, sin_b))
        _ret_unblock(dk_ref, _rope_bwd((dk + jnp.concatenate(dk_extra, axis=0)) * K_SCALE, cos_b, sin_b))
        _ret_unblock(dv_ref, dv + jnp.concatenate(dv_extra, axis=0))

    cst = _full((RET_HEADS, RET_CHUNK, RET_CHUNK))
    rev = _rows_rev(t, RET_WIDTH, nt)
    act = jax.ShapeDtypeStruct((seq, RET_WIDTH), BF16)
    return _call(
        body, "ret_bwd", (nt,),
        [rev] * 4 + [_full((1, RET_WIDTH)), cst, cst, cst, cst,
                     pl.BlockSpec((cps, RET_HEADS, HEAD_DIM, HEAD_DIM), lambda i: (nt - 1 - i, 0, 0, 0)), rev,
                     _rows_rev(t, HEAD_DIM, nt), _rows_rev(t, HEAD_DIM, nt)],
        [rev] * 4 + [_full((1, RET_WIDTH))], [act, act, act, act, jax.ShapeDtypeStruct((1, RET_WIDTH), F32)],
        [pltpu.VMEM((RET_HEADS, HEAD_DIM, HEAD_DIM), F32)],
        (q, k, v, gate, gn_gain, dec, xi, zeta, gch, rprev, dy_ret, cos2, sin2), exchange)


MXU_TILE = 256
S5_STATE_TILES = 2 * N_STATES // MXU_TILE
S5_CHANNELS_PER_TILE = MXU_TILE // SSM_STATE * SSM_GROUP
S5_TILES_SHAPE = (S5_STATE_TILES, LANES, MXU_TILE)


def _chan_block(kk):
    return ((kk % (S5_STATE_TILES // 2)) * S5_CHANNELS_PER_TILE) // LANES * LANES


def _s5_tiles(blocks_re, blocks_im):
    per = MXU_TILE // SSM_STATE
    half = S5_STATE_TILES // 2
    eye = jnp.eye(per, dtype=F32)
    in_half = (jnp.arange(half) % (LANES // S5_CHANNELS_PER_TILE))[:, None] == jnp.arange(LANES // S5_CHANNELS_PER_TILE)[None, :]
    out = []
    for blk in (blocks_re, blocks_im):
        t = blk.reshape(half, per, SSM_GROUP, 1, SSM_STATE) * eye[None, :, None, :, None]
        t = t.reshape(half, 1, S5_CHANNELS_PER_TILE, MXU_TILE)
        out.append(jnp.where(in_half[:, :, None, None], t, 0.0).reshape(half, LANES, MXU_TILE))
    return jnp.concatenate(out, axis=0)


def _s5_discretise(lam_re, lam_im, log_dt, b_re, b_im, c_re, c_im):
    lam = lax.complex(jnp.minimum(lam_re, -1e-4), lam_im)
    dt = jnp.exp(log_dt)[:, None]
    lam_bar = jnp.exp(lam * dt)
    b_bar = ((lam_bar - 1.0) / lam)[:, :, None] * lax.complex(b_re, b_im)
    b_tiles = _s5_tiles(jnp.swapaxes(jnp.real(b_bar), 1, 2), jnp.swapaxes(jnp.imag(b_bar), 1, 2))
    return jnp.real(lam_bar), jnp.imag(lam_bar), b_tiles, _s5_tiles(c_re, -c_im)


def _s5_tables(lbr, lbi):
    lr = lbr.reshape(1, N_STATES)
    li = lbi.reshape(1, N_STATES)
    pr, pi = lr, li
    while pr.shape[0] < S5_STEPS:
        top_r, top_i = pr[-1:], pi[-1:]
        pr, pi = (jnp.concatenate([pr, pr * top_r - pi * top_i], axis=0), jnp.concatenate([pi, pr * top_i + pi * top_r], axis=0))
    bc = lambda a: jnp.broadcast_to(a, (SUBLANES, N_STATES))
    rep = lambda rows: jnp.broadcast_to(rows[:, None, :], (S5_STEPS, SUBLANES, N_STATES)).reshape(S5_STEPS * SUBLANES, N_STATES)
    return bc(lr), bc(li), rep(pr), rep(pi)


def _cmul(ar, ai, br, bi):
    return ar * br - ai * bi, ar * bi + ai * br


def _seg(i):
    return pl.ds(pl.multiple_of(i * SUBLANES, SUBLANES), SUBLANES)


def _scan_order():
    r = np.arange(S5_TILE)
    token = (r % SUBLANES) * S5_STEPS + r // SUBLANES
    p = (token[:, None] == np.arange(S5_TILE)[None, :]).astype(np.float32)
    return jnp.asarray(p, dtype=BF16), jnp.asarray(p.T, dtype=BF16)


def _to_scan_order(p_ref, rows_bf16):
    return _dot(p_ref[...], rows_bf16).astype(BF16)


def _from_scan_order(pt_ref, rows):
    hi = rows.astype(BF16)
    rest = rows - hi.astype(F32)
    mid = rest.astype(BF16)
    lo = (rest - mid.astype(F32)).astype(BF16)
    pt = pt_ref[...]
    return _dot(pt, hi) + _dot(pt, mid) + _dot(pt, lo)


def _s5_channels_to_states(chan_b, m_ref, xs):
    for kk in range(S5_STATE_TILES):
        lo = _chan_block(kk)
        res = _dot(chan_b[:, lo:lo + LANES], m_ref[kk])
        xs[2 * kk] = res[:, :LANES]
        xs[2 * kk + 1] = res[:, LANES:]


def _s5_states_to_channels(sb_ref, m_ref):
    out = []
    for lo in range(0, SSM_WIDTH, LANES):
        acc = None
        for kk in range(S5_STATE_TILES):
            if _chan_block(kk) == lo:
                term = _dot_nt(sb_ref[:, kk * MXU_TILE:(kk + 1) * MXU_TILE], m_ref[kk])
                acc = term if acc is None else acc + term
        out.append(acc)
    return jnp.concatenate(out, axis=1)


def _s5_outer_acc(acc_ref, chan_b, states_ref):
    for kk in range(S5_STATE_TILES):
        lo = _chan_block(kk)
        acc_ref[kk] += _dot_tn(chan_b[:, lo:lo + LANES], states_ref[:, kk * MXU_TILE:(kk + 1) * MXU_TILE])


def _s5_scan_tile(xs, lr_ref, li_ref, pr_ref, pi_ref, carry_re, carry_im, conj, reverse, prev_of=None):
    sgn = -1.0 if conj else 1.0
    seg_in_re, seg_in_im, sums_re, sums_im = [], [], [], []
    group = S5_GROUP
    for grp in range(S5_COLBLK // group):
        blks = [grp * group + j for j in range(group)]
        lrs = [lr_ref[:, b * LANES:(b + 1) * LANES] for b in blks]
        lis = [sgn * li_ref[:, b * LANES:(b + 1) * LANES] for b in blks]

        def step(it, carry, blks=blks, lrs=lrs, lis=lis):
            i = (S5_STEPS - 1 - it) if reverse else it
            out_r, out_i = [], []
            for j, b in enumerate(blks):
                xr, xi = _cmul(lrs[j], lis[j], carry[j], carry[group + j])
                xr = xr + xs[b, _seg(i), :]
                xi = xi + xs[S5_COLBLK + b, _seg(i), :]
                xs[b, _seg(i), :] = xr
                xs[S5_COLBLK + b, _seg(i), :] = xi
                out_r.append(xr)
                out_i.append(xi)
            return tuple(out_r + out_i)

        zeros = tuple(jnp.zeros((SUBLANES, LANES), F32) for _ in range(2 * group))
        ends = lax.fori_loop(0, S5_STEPS, step, zeros, unroll=True)

        ins_r, ins_i = [], []
        for j, b in enumerate(blks):
            cols = slice(b * LANES, (b + 1) * LANES)
            pr = pr_ref[SUBLANES * S5_STEPS - 1:SUBLANES * S5_STEPS, cols]
            pi = sgn * pi_ref[SUBLANES * S5_STEPS - 1:SUBLANES * S5_STEPS, cols]
            cur_r = carry_re[:, cols]
            cur_i = carry_im[:, cols]
            row_id = lax.broadcasted_iota(jnp.int32, (SUBLANES, LANES), 0)
            in_r = jnp.zeros((SUBLANES, LANES), F32)
            in_i = jnp.zeros((SUBLANES, LANES), F32)
            order = range(SUBLANES - 1, -1, -1) if reverse else range(SUBLANES)
            for sgm in order:
                in_r = jnp.where(row_id == sgm, cur_r, in_r)
                in_i = jnp.where(row_id == sgm, cur_i, in_i)
                mr, mi = _cmul(pr, pi, cur_r, cur_i)
                cur_r = mr + ends[j][sgm:sgm + 1, :]
                cur_i = mi + ends[group + j][sgm:sgm + 1, :]
            carry_re[:, cols] = cur_r
            carry_im[:, cols] = cur_i
            ins_r.append(in_r)
            ins_i.append(in_i)

        def fix(it, c, blks=blks, ins_r=ins_r, ins_i=ins_i):
            pw = (S5_STEPS - 1 - it) if reverse else it
            out = []
            for j, b in enumerate(blks):
                cols = slice(b * LANES, (b + 1) * LANES)
                prow = pl.ds(pl.multiple_of(pw * SUBLANES, SUBLANES), SUBLANES)
                fr, fi = _cmul(pr_ref[prow, cols], sgn * pi_ref[prow, cols], ins_r[j], ins_i[j])
                vr = xs[b, _seg(it), :] + fr
                vi = xs[S5_COLBLK + b, _seg(it), :] + fi
                xs[b, _seg(it), :] = vr
                xs[S5_COLBLK + b, _seg(it), :] = vi
                if prev_of is not None:
                    opr, opi, sr, si = c[4 * j:4 * j + 4]
                    out += [prev_of[0][b, _seg(it), :], prev_of[0][S5_COLBLK + b, _seg(it), :],
                            sr + vr * opr + vi * opi, si + vi * opr - vr * opi]
            return tuple(out)

        init = ()
        if prev_of is not None:
            zero = jnp.zeros((SUBLANES, LANES), F32)
            for b in blks:
                init += (prev_of[1][b], prev_of[2][b], zero, zero)
        done = lax.fori_loop(0, S5_STEPS, fix, init, unroll=True)
        for j in range(len(blks) if prev_of is not None else 0):
            sums_re.append(done[4 * j + 2])
            sums_im.append(done[4 * j + 3])
        seg_in_re += ins_r
        seg_in_im += ins_i
    return seg_in_re, seg_in_im, sums_re, sums_im


def _s5_pack(xs, dst):
    for b in range(2 * S5_COLBLK):
        dst[:, b * LANES:(b + 1) * LANES] = xs[b].astype(BF16)


def _s5_fwd(u, b_blk, c_blk, tables, d_skip, exchange=None):
    seq = u.shape[0]
    t = S5_TILE
    nt = seq // t
    lr8, li8, p_re, p_im = tables

    p, pt = _scan_order()

    def body(u_ref, p_ref, pt_ref, b_ref, c_ref, lr_ref, li_ref, pr_ref, pi_ref, d_ref, y_ref, start_ref,
             xs, xb, carry_re, carry_im):
        @pl.when(pl.program_id(0) == 0)
        def _():
            carry_re[...] = jnp.zeros_like(carry_re)
            carry_im[...] = jnp.zeros_like(carry_im)

        start_ref[0, 0:1, :] = carry_re[...]
        start_ref[0, 1:2, :] = carry_im[...]
        uv = u_ref[...]
        _s5_channels_to_states(_to_scan_order(p_ref, uv.astype(BF16)), b_ref, xs)
        _s5_scan_tile(xs, lr_ref, li_ref, pr_ref, pi_ref, carry_re, carry_im, conj=False, reverse=False)
        _s5_pack(xs, xb)
        y_ref[...] = _from_scan_order(pt_ref, _s5_states_to_channels(xb, c_ref)) + d_ref[...] * uv

    return _call(
        body, "s5_fwd", (nt,),
        [_rows(t, SSM_WIDTH), _full((t, t)), _full((t, t)), _full(S5_TILES_SHAPE), _full(S5_TILES_SHAPE),
         _full((SUBLANES, N_STATES)), _full((SUBLANES, N_STATES)), _full((SUBLANES * S5_STEPS, N_STATES)),
         _full((SUBLANES * S5_STEPS, N_STATES)), _full((1, SSM_WIDTH))],
        [_rows(t, SSM_WIDTH), pl.BlockSpec((1, 2, N_STATES), lambda i: (i, 0, 0))],
        [jax.ShapeDtypeStruct((seq, SSM_WIDTH), F32), jax.ShapeDtypeStruct((nt, 2, N_STATES), F32)],
        [pltpu.VMEM((2 * S5_COLBLK, t, LANES), F32), pltpu.VMEM((t, 2 * N_STATES), BF16),
         pltpu.VMEM((1, N_STATES), F32), pltpu.VMEM((1, N_STATES), F32)],
        (u, p, pt, b_blk, c_blk, lr8, li8, p_re, p_im, d_skip), exchange)


def _s5_bwd(u, dy, b_blk, c_blk, tables, d_skip, starts, exchange=None):
    seq = u.shape[0]
    t = S5_TILE
    nt = seq // t
    lr8, li8, p_re, p_im = tables
    p, pt = _scan_order()

    def body(u_ref, dy_ref, p_ref, pt_ref, b_ref, c_ref, lr_ref, li_ref, pr_ref, pi_ref, d_ref, start_ref,
             du_ref, dd_ref, dlam_ref, db_ref, dc_ref,
             xs, as_, xb, ab, carry_re, carry_im, acar_re, acar_im):
        @pl.when(pl.program_id(0) == 0)
        def _():
            acar_re[...] = jnp.zeros_like(acar_re)
            acar_im[...] = jnp.zeros_like(acar_im)
            dd_ref[...] = jnp.zeros_like(dd_ref)
            dlam_ref[...] = jnp.zeros_like(dlam_ref)
            db_ref[...] = jnp.zeros_like(db_ref)
            dc_ref[...] = jnp.zeros_like(dc_ref)

        uv = u_ref[...]
        dyv = dy_ref[...]
        ub = _to_scan_order(p_ref, uv.astype(BF16))
        dyb = _to_scan_order(p_ref, dyv.astype(BF16))
        carry_re[...] = start_ref[0, 0:1, :]
        carry_im[...] = start_ref[0, 1:2, :]
        _s5_channels_to_states(ub, b_ref, xs)
        in_re, in_im, _, _ = _s5_scan_tile(xs, lr_ref, li_ref, pr_ref, pi_ref, carry_re, carry_im, conj=False, reverse=False)
        _s5_channels_to_states(dyb, c_ref, as_)
        _, _, dl_re, dl_im = _s5_scan_tile(as_, lr_ref, li_ref, pr_ref, pi_ref, acar_re, acar_im, conj=True, reverse=True,
                                           prev_of=(xs, in_re, in_im))
        for b in range(S5_COLBLK):
            cols = slice(b * LANES, (b + 1) * LANES)
            dlam_ref[0, :, cols] += dl_re[b]
            dlam_ref[1, :, cols] += dl_im[b]
        _s5_pack(xs, xb)
        _s5_pack(as_, ab)
        du_ref[...] = (_from_scan_order(pt_ref, _s5_states_to_channels(ab, b_ref)) + d_ref[...] * dyv).astype(BF16)
        dd_ref[...] += jnp.sum(dyv * uv, axis=0, keepdims=True)
        _s5_outer_acc(db_ref, ub, ab)
        _s5_outer_acc(dc_ref, dyb, xb)

    rev = _rows_rev(t, SSM_WIDTH, nt)
    vec = lambda: pltpu.VMEM((1, N_STATES), F32)
    outer = S5_TILES_SHAPE
    return _call(
        body, "s5_bwd", (nt,),
        [rev, rev, _full((t, t)), _full((t, t)), _full(S5_TILES_SHAPE), _full(S5_TILES_SHAPE),
         _full((SUBLANES, N_STATES)), _full((SUBLANES, N_STATES)), _full((SUBLANES * S5_STEPS, N_STATES)),
         _full((SUBLANES * S5_STEPS, N_STATES)), _full((1, SSM_WIDTH)),
         pl.BlockSpec((1, 2, N_STATES), lambda i: (nt - 1 - i, 0, 0))],
        [rev, _full((1, SSM_WIDTH)), _full((2, SUBLANES, N_STATES)), _full(outer), _full(outer)],
        [jax.ShapeDtypeStruct((seq, SSM_WIDTH), BF16), jax.ShapeDtypeStruct((1, SSM_WIDTH), F32),
         jax.ShapeDtypeStruct((2, SUBLANES, N_STATES), F32), jax.ShapeDtypeStruct(outer, F32), jax.ShapeDtypeStruct(outer, F32)],
        [pltpu.VMEM((2 * S5_COLBLK, t, LANES), F32), pltpu.VMEM((2 * S5_COLBLK, t, LANES), F32),
         pltpu.VMEM((t, 2 * N_STATES), BF16), pltpu.VMEM((t, 2 * N_STATES), BF16), vec(), vec(), vec(), vec()],
        (u, dy, p, pt, b_blk, c_blk, lr8, li8, p_re, p_im, d_skip, starts), exchange)


def _mix_fwd(y_ssm, y_ret, x, gain, w_glu, w_out, exchange=None):
    seq = x.shape[0]
    t = TOK_TILE

    def body(ys_ref, yr_ref, x_ref, g_ref, wg_ref, wo_ref, x1_ref, mix_ref):
        g0 = _gelu(ys_ref[...]).astype(BF16)
        z = _dot(g0, wg_ref[...])
        glu = (z[:, :SSM_WIDTH] * _sigmoid(z[:, SSM_WIDTH:])).astype(BF16)
        mix = _dot(yr_ref[...], wo_ref[:RET_WIDTH, :]) + _dot(glu, wo_ref[RET_WIDTH:, :])
        mix_ref[...] = mix
        n, _ = _rms(mix)
        x1_ref[...] = x_ref[...] + n * g_ref[...]

    act = jax.ShapeDtypeStruct((seq, D_MODEL), F32)
    return _call(
        body, "mix_fwd", (seq // t,),
        [_rows(t, SSM_WIDTH), _rows(t, RET_WIDTH), _rows(t, D_MODEL), _full((1, D_MODEL)),
         _full((SSM_WIDTH, 2 * SSM_WIDTH)), _full((D_MODEL, D_MODEL))],
        [_rows(t, D_MODEL)] * 2, [act, act], [], (y_ssm, y_ret, x, gain, w_glu, w_out), exchange)


def _mix_bwd(dx1, mix, gain, y_ssm, y_ret, w_glu, w_out, exchange=None):
    seq = dx1.shape[0]
    t = TOK_TILE

    def body(dx1_ref, mix_ref, g_ref, ys_ref, yr_ref, wg_ref, wo_ref, dyr_ref, dys_ref, dwo_ref, dwg_ref, dg_ref):
        @pl.when(pl.program_id(0) == 0)
        def _():
            dwo_ref[...] = jnp.zeros_like(dwo_ref)
            dwg_ref[...] = jnp.zeros_like(dwg_ref)
            dg_ref[...] = jnp.zeros_like(dg_ref)

        n, r = _rms(mix_ref[...])
        dmix, dgs = _rms_bwd(n, r, g_ref[...], dx1_ref[...])
        dg_ref[...] += dgs
        dmb = dmix.astype(BF16)
        dcat = _dot_nt(dmb, wo_ref[...])
        dyr_ref[...] = dcat[:, :RET_WIDTH]
        dglu = dcat[:, RET_WIDTH:]
        ys = ys_ref[...]
        g0 = _gelu(ys).astype(BF16)
        z = _dot(g0, wg_ref[...])
        a = z[:, :SSM_WIDTH]
        sb = _sigmoid(z[:, SSM_WIDTH:])
        dwo_ref[:RET_WIDTH, :] += _dot_tn(yr_ref[...], dmb)
        dwo_ref[RET_WIDTH:, :] += _dot_tn((a * sb).astype(BF16), dmb)
        dz = jnp.concatenate([dglu * sb, dglu * a * sb * (1.0 - sb)], axis=1).astype(BF16)
        dwg_ref[...] += _dot_tn(g0, dz)
        dys_ref[...] = _dot_nt(dz, wg_ref[...]) * _gelu_grad(ys)

    half = jax.ShapeDtypeStruct((seq, RET_WIDTH), F32)
    return _call(
        body, "mix_bwd", (seq // t,),
        [_rows(t, D_MODEL), _rows(t, D_MODEL), _full((1, D_MODEL)), _rows(t, SSM_WIDTH), _rows(t, RET_WIDTH),
         _full((SSM_WIDTH, 2 * SSM_WIDTH)), _full((D_MODEL, D_MODEL))],
        [_rows(t, RET_WIDTH), _rows(t, SSM_WIDTH), _full((D_MODEL, D_MODEL)), _full((SSM_WIDTH, 2 * SSM_WIDTH)),
         _full((1, D_MODEL))],
        [half, half, jax.ShapeDtypeStruct((D_MODEL, D_MODEL), F32), jax.ShapeDtypeStruct((SSM_WIDTH, 2 * SSM_WIDTH), F32),
         jax.ShapeDtypeStruct((1, D_MODEL), F32)],
        [], (dx1, mix, gain, y_ssm, y_ret, w_glu, w_out), exchange)


def _mlp_a(x1, target, gain_pre, gain_post, w1, w2):
    seq = x1.shape[0]
    t = MLP_TILE

    def body(x1_ref, tg_ref, gp_ref, gq_ref, w1_ref, w2_ref, df_ref, dx2_ref, dw2_ref, dgq_ref, sq_ref):
        @pl.when(pl.program_id(0) == 0)
        def _():
            dw2_ref[...] = jnp.zeros_like(dw2_ref)
            dgq_ref[...] = jnp.zeros_like(dgq_ref)
            sq_ref[...] = jnp.zeros_like(sq_ref)

        x1v = x1_ref[...]
        n3, _ = _rms(x1v)
        h = (n3 * gp_ref[...]).astype(BF16)
        rl = jnp.maximum(_dot(h, w1_ref[...]), 0.0)
        act = (rl * rl).astype(BF16)
        n4, r4 = _rms(_dot(act, w2_ref[...]))
        gq = gq_ref[...]
        err = x1v + n4 * gq - tg_ref[...]
        sq_ref[...] += jnp.sum(err * err, axis=0, keepdims=True)
        dx2 = err * (1.0 / D_MODEL)
        dx2_ref[...] = dx2
        dm, dgs = _rms_bwd(n4, r4, gq, dx2)
        dgq_ref[...] += dgs
        dmb = dm.astype(BF16)
        dw2_ref[...] += _dot_tn(act, dmb)
        df_ref[...] = (_dot_nt(dmb, w2_ref[...]) * (2.0 * rl)).astype(BF16)

    return pl.pallas_call(
        body, name="mlp_a", grid=(seq // t,),
        in_specs=[_rows(t, D_MODEL), _rows(t, D_MODEL), _full((1, D_MODEL)), _full((1, D_MODEL)),
                  _full((D_MODEL, D_FF)), _full((D_FF, D_MODEL))],
        out_specs=[_rows(t, D_FF), _rows(t, D_MODEL), _full((D_FF, D_MODEL)), _full((1, D_MODEL)), _full((1, D_MODEL))],
        out_shape=[jax.ShapeDtypeStruct((seq, D_FF), BF16), jax.ShapeDtypeStruct((seq, D_MODEL), F32),
                   jax.ShapeDtypeStruct((D_FF, D_MODEL), F32), jax.ShapeDtypeStruct((1, D_MODEL), F32),
                   jax.ShapeDtypeStruct((1, D_MODEL), F32)],
        compiler_params=_params("arbitrary"),
    )(x1, target, gain_pre, gain_post, w1, w2)


def _mlp_b(df, dx2, x1, gain_pre, w1):
    seq = x1.shape[0]
    t = TOK_TILE

    def body(df_ref, dx2_ref, x1_ref, gp_ref, w1_ref, dx1_ref, dw1_ref, dgp_ref):
        @pl.when(pl.program_id(0) == 0)
        def _():
            dw1_ref[...] = jnp.zeros_like(dw1_ref)
            dgp_ref[...] = jnp.zeros_like(dgp_ref)

        n3, r3 = _rms(x1_ref[...])
        gp = gp_ref[...]
        h = (n3 * gp).astype(BF16)
        dfv = df_ref[...]
        dw1_ref[...] += _dot_tn(h, dfv)
        dx, dgs = _rms_bwd(n3, r3, gp, _dot_nt(dfv, w1_ref[...]))
        dgp_ref[...] += dgs
        dx1_ref[...] = dx2_ref[...] + dx

    return pl.pallas_call(
        body, name="mlp_b", grid=(seq // t,),
        in_specs=[_rows(t, D_FF), _rows(t, D_MODEL), _rows(t, D_MODEL), _full((1, D_MODEL)), _full((D_MODEL, D_FF))],
        out_specs=[_rows(t, D_MODEL), _full((D_MODEL, D_FF)), _full((1, D_MODEL))],
        out_shape=[jax.ShapeDtypeStruct((seq, D_MODEL), F32), jax.ShapeDtypeStruct((D_MODEL, D_FF), F32),
                   jax.ShapeDtypeStruct((1, D_MODEL), F32)],
        compiler_params=_params("arbitrary"),
    )(df, dx2, x1, gain_pre, w1)


def _local_step(x, target, small, weights, ids=None, late_state=None):
    cos2, sin2 = _rope_tables(x.shape[0])
    ret_consts = _ret_consts()

    s5_names = ("ssm_lambda_re", "ssm_lambda_im", "ssm_log_dt", "ssm_b_re", "ssm_b_im", "ssm_c_re", "ssm_c_im")
    (lbr, lbi, b_tiles, c_tiles), disc_vjp = jax.vjp(_s5_discretise, *[small[name] for name in s5_names])
    tables = _s5_tables(lbr, lbi)
    b_blk = b_tiles.astype(BF16)
    c_blk = c_tiles.astype(BF16)
    d_skip = small["ssm_d"].reshape(1, SSM_WIDTH)
    gn_gain = small["ret_gn_gain"].reshape(1, RET_WIDTH)
    g_mix_pre = small["norm_mix_pre"].reshape(1, D_MODEL)
    g_mix_post = small["norm_mix_post"].reshape(1, D_MODEL)
    g_mlp_pre = small["norm_mlp_pre"].reshape(1, D_MODEL)
    g_mlp_post = small["norm_mlp_post"].reshape(1, D_MODEL)

    dist = ids is not None
    w_in, w_glu, w_out, w_ff1, w_ff2 = weights
    (q, k, v, gate, u), got = _inproj_fwd(x, g_mix_pre, w_in, cos2, sin2, _x_gather_send((1, 2), [w_glu, w_out]) if dist else None)
    if dist:
        w_glu, w_out = got
    (y_ssm, starts), got = _s5_fwd(u, b_blk, c_blk, tables, d_skip, _x_gather_send(LATE, [w_ff1, w_ff2]) if dist else None)
    if dist:
        w_ff1, w_ff2 = got
    (y_ret, rprev), got = _ret_fwd(q, k, v, gate, gn_gain, ret_consts,
                                   _x_gather_pass(MID + LATE, [w_glu, w_out, w_ff1, w_ff2]) if dist else None)
    if dist:
        w_glu, w_out, w_ff1, w_ff2 = got
    (x1, mix), _ = _mix_fwd(y_ssm, y_ret, x, g_mix_post, w_glu, w_out)
    df, dx2, d_ff2, dg_mlp_post, sq = _mlp_a(x1, target, g_mlp_pre, g_mlp_post, w_ff1, w_ff2)
    dx1, d_ff1, dg_mlp_pre = _mlp_b(df, dx2, x1, g_mlp_pre, w_ff1)
    (dy_ret, dy_ssm, d_out, d_glu, dg_mix_post), got = _mix_bwd(dx1, mix, g_mix_post, y_ssm, y_ret, w_glu, w_out,
                                                               _x_pair(LATE, [d_ff1, d_ff2]) if dist else None)
    if dist:
        sums, sums_bf16 = _pair_sum(ids, LATE, [d_ff1, d_ff2], got)
    (du, dd, dlam8, db_tiles, dc_tiles), got = _s5_bwd(
        u, dy_ssm, b_blk, c_blk, tables, d_skip, starts,
        _x_join(_x_chip(LATE, sums_bf16), _x_pair(MID, [d_glu, d_out])) if dist else None)
    if dist:
        parts = _chip_sum(ids, LATE, sums, got[:len(LATE)])
        sums, sums_bf16 = _pair_sum(ids, MID, [d_glu, d_out], got[len(LATE):])
    (dq, dk, dv, dgate, dgn), got = _ret_bwd(q, k, v, gate, gn_gain, ret_consts, rprev, dy_ret, cos2, sin2,
                                             _x_join(_x_share(LATE, parts), _x_chip(MID, sums_bf16)) if dist else None)
    if dist:
        late_grads = got[:len(LATE)]
        parts = _chip_sum(ids, MID, sums, got[len(LATE):])
        mid_parts = parts
    adam = (late_state[0], late_grads, late_state[1], late_state[2]) if dist else ()
    (dx, d_in, dg_mix_pre), late_stepped = _inproj_bwd(x, g_mix_pre, w_in, [dq, dk, dv, dgate, du], dx1, adam)

    dlam = jnp.sum(dlam8, axis=1)
    s5_grads = disc_vjp((dlam[0].reshape(SSM_GROUPS, SSM_STATE), dlam[1].reshape(SSM_GROUPS, SSM_STATE), db_tiles, dc_tiles))
    small_grads = {
        "norm_mix_pre": dg_mix_pre, "norm_mix_post": dg_mix_post, "ret_gn_gain": dgn, "ssm_d": dd,
        "norm_mlp_pre": dg_mlp_pre, "norm_mlp_post": dg_mlp_post, "loss_sum": 0.5 * jnp.sum(sq) / D_MODEL,
    }
    small_grads.update(dict(zip(s5_names, s5_grads)))
    if dist:
        return dx, d_in, mid_parts, late_grads, late_stepped, small_grads
    return dx, (d_in, d_glu, d_out, d_ff1, d_ff2), small_grads


BIG = (
    ("w_in", D_MODEL, IN_COLS, 1),
    ("w_glu", SSM_WIDTH, 2 * SSM_WIDTH, 1),
    ("w_out", D_MODEL, D_MODEL, 0),
    ("w_ff1", D_MODEL, D_FF, 1),
    ("w_ff2", D_FF, D_MODEL, 0),
)
SMALL = (
    ("norm_mix_pre", (D_MODEL,)), ("norm_mix_post", (D_MODEL,)), ("ret_gn_gain", (RET_WIDTH,)),
    ("ssm_lambda_re", (SSM_GROUPS, SSM_STATE)), ("ssm_lambda_im", (SSM_GROUPS, SSM_STATE)), ("ssm_log_dt", (SSM_GROUPS,)),
    ("ssm_b_re", (SSM_GROUPS, SSM_STATE, SSM_GROUP)), ("ssm_b_im", (SSM_GROUPS, SSM_STATE, SSM_GROUP)),
    ("ssm_c_re", (SSM_GROUPS, SSM_GROUP, SSM_STATE)), ("ssm_c_im", (SSM_GROUPS, SSM_GROUP, SSM_STATE)),
    ("ssm_d", (SSM_WIDTH,)), ("norm_mlp_pre", (D_MODEL,)), ("norm_mlp_post", (D_MODEL,)),
)
SMALL_ROWS = 1152
EXCH_TILES = 8


PACKED = SMALL + (("loss_sum", (1,)),)


def _packed_rows(shape):
    return -(-math.prod(shape) // (SUBLANES * LANES)) * SUBLANES


def _pack_small(tree):
    rows = []
    for name, shape in PACKED:
        size = math.prod(shape)
        flat = tree[name].reshape(-1).astype(F32) if name in tree else jnp.zeros((size,), F32)
        rows.append(jnp.pad(flat, (0, _packed_rows(shape) * LANES - size)).reshape(-1, LANES))
    used = sum(r.shape[0] for r in rows)
    rows.append(jnp.zeros((SMALL_ROWS - used, LANES), F32))
    return jnp.concatenate(rows, axis=0)


def _unpack_small(packed):
    out, row = {}, 0
    for name, shape in PACKED:
        size = math.prod(shape)
        out[name] = packed[row:row + _packed_rows(shape)].reshape(-1)[:size].reshape((1,) + shape)
        row += _packed_rows(shape)
    return out


def _half_shape(r, c, axis):
    return (r // 2, c) if axis == 1 else (r, c // 2)


def _region_shape(r, c, axis):
    return (r // 2, c // N_CHIPS) if axis == 1 else (r // N_CHIPS, c // 2)


def _shard_shape(r, c, axis):
    return (r, c // N_CHIPS) if axis == 1 else (r // N_CHIPS, c)


def _ds(start, size):
    return pl.ds(pl.multiple_of(start * size, size), size)


def _region_of_full(ref, r, c, axis, shard, half):
    if axis == 1:
        return ref.at[_ds(half, r // 2), _ds(shard, c // N_CHIPS)]
    return ref.at[_ds(shard, r // N_CHIPS), _ds(half, c // 2)]


def _half_of_full(ref, r, c, axis, half):
    if axis == 1:
        return ref.at[_ds(half, r // 2), :]
    return ref.at[:, _ds(half, c // 2)]


def _half_of_shard(ref, r, c, axis, half):
    if axis == 1:
        return ref.at[_ds(half, r // 2), :]
    return ref.at[:, _ds(half, c // 2)]


def _region_of_half(ref, r, c, axis, shard):
    if axis == 1:
        return ref.at[:, _ds(shard, c // N_CHIPS)]
    return ref.at[_ds(shard, r // N_CHIPS), :]


def _place():
    x, y, c = lax.axis_index("x"), lax.axis_index("y"), lax.axis_index("c")
    chips = [(1 - x, y), (x, 1 - y), (1 - x, 1 - y)]
    return x, y, c, chips


LATE = (3, 4)
MID = (1, 2)
FIRST = (0,)
SMALL_HALF = (SMALL_ROWS // 2, LANES)


def _remote(src, dst, send_sem, recv_sem, to):
    return pltpu.make_async_remote_copy(src_ref=src, dst_ref=dst, send_sem=send_sem, recv_sem=recv_sem,
                                        device_id=to, device_id_type=MESH)


def _same(arrays):
    return [jax.ShapeDtypeStruct(a.shape, a.dtype) for a in arrays]


def _x_gather_send(ws, fulls):
    def copies(ins, outs, send_sems, recv_sems):
        x, y, c, chips = _place()
        out = []
        for j, w in enumerate(ws):
            _, r, cc, axis = BIG[w]
            mine = _region_of_full(outs[j], r, cc, axis, 2 * x + y, c)
            out += [_remote(mine, mine, send_sems.at[3 * j + k], recv_sems.at[3 * j + k], (cx, cy, c))
                    for k, (cx, cy) in enumerate(chips)]
        return out

    return _Exchange(fulls, _same(fulls), {j: j for j in range(len(ws))}, 3 * len(ws), copies)


def _x_gather_pass(ws, fulls):
    def copies(ins, outs, send_sems, recv_sems):
        x, y, c, chips = _place()
        out = []
        for j, w in enumerate(ws):
            _, r, cc, axis = BIG[w]
            for k, (cx, cy) in enumerate(chips):
                landed = _region_of_full(outs[j], r, cc, axis, 2 * cx + cy, c)
                out.append(_remote(landed, landed, send_sems.at[3 * j + k], recv_sems.at[3 * j + k], (x, y, 1 - c)))
        return out

    return _Exchange(fulls, _same(fulls), {j: j for j in range(len(ws))}, 3 * len(ws), copies)


def _x_pair(ws, grads, small=None):
    def copies(ins, outs, send_sems, recv_sems):
        x, y, c, _ = _place()
        out = []
        for j, w in enumerate(ws):
            _, r, cc, axis = BIG[w]
            out.append(_remote(_half_of_full(ins[j], r, cc, axis, 1 - c), outs[j], send_sems.at[j], recv_sems.at[j], (x, y, 1 - c)))
        if small is not None:
            j = len(ws)
            out.append(_remote(ins[j].at[_ds(1 - c, SMALL_ROWS // 2), :], outs[j], send_sems.at[j], recv_sems.at[j], (x, y, 1 - c)))
        return out

    shapes = [jax.ShapeDtypeStruct(_half_shape(*BIG[w][1:]), F32) for w in ws]
    extra = [] if small is None else [small]
    return _Exchange(list(grads) + extra, shapes + [jax.ShapeDtypeStruct(SMALL_HALF, F32)] * len(extra), {},
                     len(ws) + len(extra), copies)


def _x_chip(ws, sums_bf16, small_sum=None):
    def copies(ins, outs, send_sems, recv_sems):
        x, y, c, chips = _place()
        out = []
        for j, w in enumerate(ws):
            _, r, cc, axis = BIG[w]
            out += [_remote(_region_of_half(ins[j], r, cc, axis, 2 * cx + cy), outs[j].at[k],
                            send_sems.at[3 * j + k], recv_sems.at[3 * j + k], (cx, cy, c)) for k, (cx, cy) in enumerate(chips)]
        if small_sum is not None:
            j = len(ws)
            out += [_remote(ins[j], outs[j].at[k], send_sems.at[3 * j + k], recv_sems.at[3 * j + k], (cx, cy, c))
                    for k, (cx, cy) in enumerate(chips)]
        return out

    shapes = [jax.ShapeDtypeStruct((3,) + _region_shape(*BIG[w][1:]), BF16) for w in ws]
    extra = [] if small_sum is None else [small_sum]
    return _Exchange(list(sums_bf16) + extra, shapes + [jax.ShapeDtypeStruct((3,) + SMALL_HALF, F32)] * len(extra), {},
                     3 * (len(ws) + len(extra)), copies)


def _x_share(ws, shards, small=None):
    def copies(ins, outs, send_sems, recv_sems):
        x, y, c, _ = _place()
        out = []
        for j, w in enumerate(ws):
            _, r, cc, axis = BIG[w]
            mine = _half_of_shard(outs[j], r, cc, axis, c)
            out.append(_remote(mine, mine, send_sems.at[j], recv_sems.at[j], (x, y, 1 - c)))
        if small is not None:
            j = len(ws)
            mine = outs[j].at[_ds(c, SMALL_ROWS // 2), :]
            out.append(_remote(mine, mine, send_sems.at[j], recv_sems.at[j], (x, y, 1 - c)))
        return out

    arrays = list(shards) + ([] if small is None else [small])
    return _Exchange(arrays, _same(arrays), {j: j for j in range(len(arrays))}, len(arrays), copies)


class _Offset:
    def __init__(self, sems, base):
        self._sems, self._base = sems, base

    @property
    def at(self):
        return self

    def __getitem__(self, i):
        return self._sems.at[self._base + i]


def _x_join(a, b):
    ka_in, ka_out = len(a.operands), len(a.out_shapes)

    def copies(ins, outs, send_sems, recv_sems):
        return (a.copies(ins[:ka_in], outs[:ka_out], send_sems, recv_sems)
                + b.copies(ins[ka_in:], outs[ka_out:], _Offset(send_sems, a.n), _Offset(recv_sems, a.n)))

    aliases = dict(a.aliases)
    aliases.update({ka_in + i: ka_out + o for i, o in b.aliases.items()})
    return _Exchange(a.operands + b.operands, a.out_shapes + b.out_shapes, aliases, a.n + b.n, copies)


def _run_exchange(name, exchange):
    k_in, k_out = len(exchange.operands), len(exchange.out_shapes)

    def body(*refs):
        copies = exchange.copies(refs[:k_in], refs[k_in:k_in + k_out], refs[-2], refs[-1])
        for cp in copies:
            cp.start()
        for cp in copies:
            cp.wait()

    return list(pl.pallas_call(
        body, name=name, in_specs=[ANY] * k_in, out_specs=[ANY] * k_out, out_shape=exchange.out_shapes,
        scratch_shapes=[pltpu.SemaphoreType.DMA((exchange.n,)), pltpu.SemaphoreType.DMA((exchange.n,))],
        input_output_aliases=dict(exchange.aliases),
    )(*exchange.operands))


def _gather_now(name, ws, fulls):
    send, onward = _x_gather_send(ws, fulls), _x_gather_pass(ws, fulls)
    k = len(fulls)

    def body(*refs):
        ins, outs, (send_a, recv_a, send_b, recv_b) = refs[:k], refs[k:2 * k], refs[2 * k:]
        first = send.copies(ins, outs, send_a, recv_a)
        second = onward.copies(ins, outs, send_b, recv_b)
        for cp in first:
            cp.start()
        for landed, cp in zip(first, second):
            landed.wait()
            cp.start()
        for cp in second:
            cp.wait()

    sems = lambda n: pltpu.SemaphoreType.DMA((n,))
    return list(pl.pallas_call(
        body, name=name, in_specs=[ANY] * k, out_specs=[ANY] * k, out_shape=send.out_shapes,
        scratch_shapes=[sems(send.n), sems(send.n), sems(onward.n), sems(onward.n)],
        input_output_aliases={j: j for j in range(k)},
    )(*fulls))


def _tile_rows(shape):
    return (shape[0] // EXCH_TILES, shape[1])


def _cast_place(ids, shards):
    def body(ids_ref, *refs):
        for j in range(len(BIG)):
            refs[len(BIG) + j][...] = refs[j][...].astype(BF16)

    def out_spec(w):
        _, r, cc, axis = BIG[w]
        tile = _tile_rows(_shard_shape(r, cc, axis))
        if axis == 1:
            return pl.BlockSpec(tile, lambda i, ids: (i, ids[1]))
        return pl.BlockSpec(tile, lambda i, ids: (ids[1] * EXCH_TILES + i, 0))

    return list(pl.pallas_call(
        body, name="cast_place",
        grid_spec=pltpu.PrefetchScalarGridSpec(
            num_scalar_prefetch=1, grid=(EXCH_TILES,),
            in_specs=[pl.BlockSpec(_tile_rows(_shard_shape(r, cc, axis)), lambda i, ids: (i, 0)) for _, r, cc, axis in BIG],
            out_specs=[out_spec(w) for w in range(len(BIG))]),
        out_shape=[jax.ShapeDtypeStruct((r, cc), BF16) for _, r, cc, _ in BIG],
        compiler_params=pltpu.CompilerParams(dimension_semantics=("parallel",)),
    )(ids, *shards))


def _pair_sum(ids, ws, grads, got, small=None):
    nb = len(ws)
    n = nb + (small is not None)
    halves = [_half_shape(*BIG[w][1:]) for w in ws] + [SMALL_HALF] * (n - nb)

    def body(ids_ref, *refs):
        for j in range(n):
            total = refs[j][...] + refs[n + j][...]
            refs[2 * n + j][...] = total
            if j < nb:
                refs[3 * n + j][...] = total.astype(BF16)

    def mine_spec(j):
        tile = _tile_rows(halves[j])
        if j < nb and BIG[ws[j]][3] == 0:
            return pl.BlockSpec(tile, lambda i, ids: (i, ids[0]))
        return pl.BlockSpec(tile, lambda i, ids: (ids[0] * EXCH_TILES + i, 0))

    plain = lambda j: pl.BlockSpec(_tile_rows(halves[j]), lambda i, ids: (i, 0))
    outs = pl.pallas_call(
        body, name="pair_sum",
        grid_spec=pltpu.PrefetchScalarGridSpec(
            num_scalar_prefetch=1, grid=(EXCH_TILES,),
            in_specs=[mine_spec(j) for j in range(n)] + [plain(j) for j in range(n)],
            out_specs=[plain(j) for j in range(n)] + [plain(j) for j in range(nb)]),
        out_shape=[jax.ShapeDtypeStruct(h, F32) for h in halves] + [jax.ShapeDtypeStruct(h, BF16) for h in halves[:nb]],
        compiler_params=pltpu.CompilerParams(dimension_semantics=("parallel",)),
    )(ids, *grads, *([] if small is None else [small]), *got)
    return list(outs[:n]), list(outs[n:])


def _chip_sum(ids, ws, sums, arrived):
    nb, n = len(ws), len(sums)
    regions = [_region_shape(*BIG[w][1:]) for w in ws] + [SMALL_HALF] * (n - nb)

    def body(ids_ref, *refs):
        for j in range(n):
            own, got, out = refs[j], refs[n + j], refs[2 * n + j]
            if j < nb:
                out[...] = ((own[...] + got[0].astype(F32)) + got[1].astype(F32)) + got[2].astype(F32)
            else:
                out[...] = (own[...] + got[1]) + (got[0] + got[2])

    def own_spec(j):
        tile = _tile_rows(regions[j])
        if j >= nb:
            return pl.BlockSpec(tile, lambda i, ids: (i, 0))
        if BIG[ws[j]][3] == 1:
            return pl.BlockSpec(tile, lambda i, ids: (i, ids[1]))
        return pl.BlockSpec(tile, lambda i, ids: (ids[1] * EXCH_TILES + i, 0))

    def out_spec(j):
        tile = _tile_rows(regions[j])
        if j < nb and BIG[ws[j]][3] == 0:
            return pl.BlockSpec(tile, lambda i, ids: (i, ids[0]))
        return pl.BlockSpec(tile, lambda i, ids: (ids[0] * EXCH_TILES + i, 0))

    got_spec = lambda j: pl.BlockSpec((3,) + _tile_rows(regions[j]), lambda i, ids: (0, i, 0))
    return list(pl.pallas_call(
        body, name="chip_sum",
        grid_spec=pltpu.PrefetchScalarGridSpec(
            num_scalar_prefetch=1, grid=(EXCH_TILES,),
            in_specs=[own_spec(j) for j in range(n)] + [got_spec(j) for j in range(n)],
            out_specs=[out_spec(j) for j in range(n)]),
        out_shape=[jax.ShapeDtypeStruct(_shard_shape(*BIG[w][1:]), F32) for w in ws]
        + [jax.ShapeDtypeStruct((SMALL_ROWS, LANES), F32)] * (n - nb),
        compiler_params=pltpu.CompilerParams(dimension_semantics=("parallel",)),
    )(ids, *sums, *arrived))


def _adamw_body(n):
    c1 = 1.0 - ADAM_B1 ** ADAM_STEP
    c2 = 1.0 - ADAM_B2 ** ADAM_STEP

    def body(*refs):
        for j in range(n):
            w, g, m, v = (refs[k * n + j][...] for k in range(4))
            m = ADAM_B1 * m + (1.0 - ADAM_B1) * g
            v = ADAM_B2 * v + (1.0 - ADAM_B2) * (g * g)
            refs[4 * n + j][...] = -ADAM_LR * ((m / c1) / (jnp.sqrt(v / c2) + ADAM_EPS) + ADAM_WD * w)
            refs[5 * n + j][...] = m
            refs[6 * n + j][...] = v

    return body


def _adamw(name, ws, gs, ms, vs):
    n = len(ws)
    specs = [pl.BlockSpec(_tile_rows(w.shape), lambda i: (i, 0)) for w in ws]
    shapes = [jax.ShapeDtypeStruct(w.shape, F32) for w in ws]
    out, _ = _call(_adamw_body(n), name, (EXCH_TILES,), specs * 4, specs * 3, shapes * 3, [], (*ws, *gs, *ms, *vs))
    return out[:n], out[n:2 * n], out[2 * n:]


def _adamw_whole(name, ws, gs, ms, vs):
    n = len(ws)
    out = pl.pallas_call(_adamw_body(n), name=name, out_shape=[jax.ShapeDtypeStruct(w.shape, F32) for w in ws] * 3)(*ws, *gs, *ms, *vs)
    return out[:n], out[n:2 * n], out[2 * n:]


def kernel(x, norm_mix_pre, norm_mix_post, w_in, ret_gn_gain, ssm_lambda_re, ssm_lambda_im, ssm_log_dt, ssm_b_re, ssm_b_im, ssm_c_re, ssm_c_im, ssm_d, w_glu, w_out, norm_mlp_pre, norm_mlp_post, w_ff1, w_ff2, loss_target, m_norm_mix_pre, m_norm_mix_post, m_w_in, m_ret_gn_gain, m_ssm_lambda_re, m_ssm_lambda_im, m_ssm_log_dt, m_ssm_b_re, m_ssm_b_im, m_ssm_c_re, m_ssm_c_im, m_ssm_d, m_w_glu, m_w_out, m_norm_mlp_pre, m_norm_mlp_post, m_w_ff1, m_w_ff2, v_norm_mix_pre, v_norm_mix_post, v_w_in, v_ret_gn_gain, v_ssm_lambda_re, v_ssm_lambda_im, v_ssm_log_dt, v_ssm_b_re, v_ssm_b_im, v_ssm_c_re, v_ssm_c_im, v_ssm_d, v_w_glu, v_w_out, v_norm_mlp_pre, v_norm_mlp_post, v_w_ff1, v_w_ff2):
    given = dict(locals())
    order = ["norm_mix_pre", "norm_mix_post", "w_in", "ret_gn_gain", "ssm_lambda_re", "ssm_lambda_im", "ssm_log_dt",
             "ssm_b_re", "ssm_b_im", "ssm_c_re", "ssm_c_im", "ssm_d", "w_glu", "w_out", "norm_mlp_pre", "norm_mlp_post",
             "w_ff1", "w_ff2"]
    big_names = [name for name, _, _, _ in BIG]
    ids = jnp.stack([lax.axis_index("c"), 2 * lax.axis_index("x") + lax.axis_index("y")]).astype(jnp.int32)

    weights = _cast_place(ids, [given[name][0] for name in big_names])
    w_in = _gather_now("gather_w_in", FIRST, weights[:1])
    small_w = {name: given[name][0] for name, _ in SMALL}
    late_state = [[given[prefix + big_names[w]][0] for w in LATE] for prefix in ("", "m_", "v_")]
    dx, d_in, mid_parts, late_grads, late_stepped, small_grads = _local_step(
        x[0], loss_target[0], small_w, w_in + weights[1:], ids, late_state)

    small_packed = _pack_small(small_grads)
    got = _run_exchange("pair_exchange", _x_join(_x_share(MID, mid_parts), _x_pair(FIRST, [d_in], small_packed)))
    mid_grads, got = got[:len(MID)], got[len(MID):]
    sums, sums_bf16 = _pair_sum(ids, FIRST, [d_in], got, small_packed)
    arrived = _run_exchange("chip_exchange", _x_chip(FIRST, sums_bf16, sums[-1]))
    parts = _chip_sum(ids, FIRST, sums, arrived)
    g_in, g_small = _run_exchange("pair_share", _x_share(FIRST, parts[:-1], parts[-1]))
    big_grads = [g_in] + mid_grads + late_grads
    rest = lambda prefix: [given[prefix + big_names[w]][0] for w in FIRST + MID]
    rest_out = _adamw("adamw_big", rest(""), big_grads[:len(FIRST + MID)], rest("m_"), rest("v_"))
    big_out = [list(rest_out[k]) + list(late_stepped[k]) for k in range(3)]
    small_names = [name for name, _ in SMALL]
    small_tree = _unpack_small(g_small)
    small_grads = [small_tree[name] for name in small_names]
    small = lambda prefix: [given[prefix + name] for name in small_names]
    small_out = _adamw_whole("adamw_small", small(""), small_grads, small("m_"), small("v_"))

    def in_order(big_arrays, small_arrays):
        tree = {name: a[None] for name, a in zip(big_names, big_arrays)}
        tree.update(zip(small_names, small_arrays))
        return [tree[name] for name in order]

    loss = small_tree["loss_sum"].reshape(())
    return (loss, dx[None], *in_order(big_grads, small_grads), *in_order(big_out[0], small_out[0]),
            *in_order(big_out[1], small_out[1]), *in_order(big_out[2], small_out[2]))
```

```python
import functools
import math

import jax
import jax.numpy as jnp
import numpy as np
from jax import lax
from jax.experimental import pallas as pl
from jax.experimental.pallas import tpu as pltpu

F32 = jnp.float32
BF16 = jnp.bfloat16

D_MODEL = 1024
RET_WIDTH = 512
RET_HEADS = 4
HEAD_DIM = 128
RET_CHUNK = 128
ROPE_BASE = 10000.0
SSM_WIDTH = 512
SSM_GROUP = 16
SSM_GROUPS = 32
SSM_STATE = 64
N_STATES = SSM_GROUPS * SSM_STATE
D_FF = 4096
IN_COLS = 4 * RET_WIDTH + SSM_WIDTH
NORM_EPS = 1e-6
K_SCALE = HEAD_DIM ** -0.5

ADAM_LR = 0.001
ADAM_B1 = 0.9
ADAM_B2 = 0.999
ADAM_EPS = 1e-08
ADAM_WD = 0.01
ADAM_STEP = 10

LANES = 128
SUBLANES = 8
VMEM_LIMIT = 56 * 2 ** 20

TOK_TILE = 512
MLP_TILE = 256
RET_CHUNKS_PER_STEP = 4
S5_TILE = 256
S5_STEPS = S5_TILE // SUBLANES
S5_COLBLK = N_STATES // LANES
S5_GROUP = 4

N_CHIPS = 4
MESH = pl.DeviceIdType.MESH


def _dot(a, b):
    return jnp.dot(a, b, preferred_element_type=F32)


def _dot_nt(a, b):
    return lax.dot_general(a, b, (((1,), (1,)), ((), ())), preferred_element_type=F32)


def _dot_tn(a, b):
    return lax.dot_general(a, b, (((0,), (0,)), ((), ())), preferred_element_type=F32)


def _sigmoid(x):
    return 1.0 / (1.0 + jnp.exp(-x))


_GELU_C = math.sqrt(2.0 / math.pi)


def _gelu(x):
    return 0.5 * x * (1.0 + jnp.tanh(_GELU_C * (x + 0.044715 * (x * x * x))))


def _gelu_grad(x):
    t = jnp.tanh(_GELU_C * (x + 0.044715 * (x * x * x)))
    return 0.5 * (1.0 + t) + 0.5 * x * (1.0 - t * t) * (_GELU_C * (1.0 + 3.0 * 0.044715 * (x * x)))


def _rms(x):
    r = lax.rsqrt(jnp.mean(x * x, axis=-1, keepdims=True) + NORM_EPS)
    return x * r, r


def _rms_bwd(n, r, gain, dy):
    dn = dy * gain
    dx = r * (dn - n * jnp.mean(dn * n, axis=-1, keepdims=True))
    return dx, jnp.sum(dy * n, axis=0, keepdims=True)


def _full(shape):
    nd = len(shape)
    return pl.BlockSpec(shape, lambda i, _n=nd: (0,) * _n, pipeline_mode=pl.Buffered(1))


def _rows(tile, width):
    return pl.BlockSpec((tile, width), lambda i: (i, 0))


def _rows_rev(tile, width, n):
    return pl.BlockSpec((tile, width), lambda i, _n=n: (_n - 1 - i, 0))


def _params(sem):
    return pltpu.CompilerParams(dimension_semantics=(sem,), vmem_limit_bytes=VMEM_LIMIT)


ANY = pl.BlockSpec(memory_space=pl.ANY)


class _Exchange:
    def __init__(self, operands, out_shapes, aliases, n, copies):
        self.operands, self.out_shapes, self.aliases, self.n, self.copies = list(operands), list(out_shapes), aliases, n, copies


def _call(body, name, grid, in_specs, out_specs, out_shape, scratch_shapes, args, exchange=None):
    if exchange is None:
        outs = pl.pallas_call(body, name=name, grid=grid, in_specs=in_specs, out_specs=out_specs, out_shape=out_shape,
                              scratch_shapes=scratch_shapes, compiler_params=_params("arbitrary"))(*args)
        return list(outs), []
    n_in, n_out, n_scr = len(in_specs), len(out_specs), len(scratch_shapes)
    k_in, k_out = len(exchange.operands), len(exchange.out_shapes)
    last = grid[0] - 1

    def hosted(*refs):
        own_in, rest = refs[:n_in], refs[n_in:]
        ex_in, rest = rest[:k_in], rest[k_in:]
        own_out, rest = rest[:n_out], rest[n_out:]
        ex_out, rest = rest[:k_out], rest[k_out:]
        own_scr, (send_sems, recv_sems) = rest[:n_scr], rest[n_scr:]

        @pl.when(pl.program_id(0) == 0)
        def _():
            for cp in exchange.copies(ex_in, ex_out, send_sems, recv_sems):
                cp.start()

        body(*own_in, *own_out, *own_scr)

        @pl.when(pl.program_id(0) == last)
        def _():
            for cp in exchange.copies(ex_in, ex_out, send_sems, recv_sems):
                cp.wait()

    outs = pl.pallas_call(
        hosted, name=name, grid=grid, in_specs=list(in_specs) + [ANY] * k_in, out_specs=list(out_specs) + [ANY] * k_out,
        out_shape=list(out_shape) + exchange.out_shapes,
        scratch_shapes=list(scratch_shapes) + [pltpu.SemaphoreType.DMA((exchange.n,)), pltpu.SemaphoreType.DMA((exchange.n,))],
        input_output_aliases={n_in + i: n_out + o for i, o in exchange.aliases.items()},
        compiler_params=_params("arbitrary"),
    )(*args, *exchange.operands)
    return list(outs[:n_out]), list(outs[n_out:])


def _rope(t, cos2, sin2):
    return t * cos2 + pltpu.roll(t, HEAD_DIM // 2, 1) * sin2


def _rope_bwd(d, cos2, sin2):
    return d * cos2 + pltpu.roll(d * sin2, HEAD_DIM // 2, d.ndim - 1)


def _inproj_fwd(x, gain, w_in, cos2, sin2, exchange=None):
    seq = x.shape[0]

    def body(x_ref, g_ref, w_ref, c_ref, s_ref, q_ref, k_ref, v_ref, gate_ref, u_ref):
        n, _ = _rms(x_ref[...])
        h = (n * g_ref[...]).astype(BF16)
        proj = _dot(h, w_ref[...])
        c = c_ref[...]
        s = s_ref[...]
        for hd in range(RET_HEADS):
            lo = hd * HEAD_DIM
            q_ref[:, lo:lo + HEAD_DIM] = _rope(proj[:, lo:lo + HEAD_DIM], c, s).astype(BF16)
            kh = proj[:, RET_WIDTH + lo:RET_WIDTH + lo + HEAD_DIM]
            k_ref[:, lo:lo + HEAD_DIM] = (_rope(kh, c, s) * K_SCALE).astype(BF16)
        v_ref[...] = proj[:, 2 * RET_WIDTH:3 * RET_WIDTH].astype(BF16)
        gate_ref[...] = proj[:, 3 * RET_WIDTH:4 * RET_WIDTH]
        u_ref[...] = proj[:, 4 * RET_WIDTH:]

    t = TOK_TILE
    half = lambda dt: jax.ShapeDtypeStruct((seq, RET_WIDTH), dt)
    return _call(
        body, "inproj_fwd", (seq // t,),
        [_rows(t, D_MODEL), _full((1, D_MODEL)), _full((D_MODEL, IN_COLS)), _rows(t, HEAD_DIM), _rows(t, HEAD_DIM)],
        [_rows(t, RET_WIDTH)] * 5, [half(BF16), half(BF16), half(BF16), half(F32), half(F32)], [],
        (x, gain, w_in, cos2, sin2), exchange)


def _inproj_bwd(x, gain, w_in, dpieces, dres, adam=()):
    seq = x.shape[0]
    t = TOK_TILE
    steps = seq // t
    n_pieces = len(dpieces)
    n_adam = len(adam[0]) if adam else 0
    adam_step = _adamw_body(n_adam)

    def body(x_ref, g_ref, w_ref, *refs):
        piece_refs, (dres_ref, *adam_in), (dx_ref, dw_ref, dg_ref, *adam_out) = (
            refs[:n_pieces], refs[n_pieces:n_pieces + 1 + 4 * n_adam], refs[n_pieces + 1 + 4 * n_adam:])

        @pl.when(pl.program_id(0) == 0)
        def _():
            dw_ref[...] = jnp.zeros_like(dw_ref)
            dg_ref[...] = jnp.zeros_like(dg_ref)

        adam_step(*adam_in, *adam_out)

        n, r = _rms(x_ref[...])
        gain_v = g_ref[...]
        h = (n * gain_v).astype(BF16)
        dh = None
        for j, piece in enumerate(piece_refs):
            cols = slice(j * RET_WIDTH, (j + 1) * RET_WIDTH)
            dp = piece[...]
            dw_ref[:, cols] += _dot_tn(h, dp)
            term = _dot_nt(dp, w_ref[:, cols])
            dh = term if dh is None else dh + term
        dx, dgs = _rms_bwd(n, r, gain_v, dh)
        dg_ref[...] += dgs
        dx_ref[...] = dres_ref[...] + dx

    adam_arrays = [a for group in adam for a in group]
    adam_spec = lambda a: pl.BlockSpec((a.shape[0] // steps, a.shape[1]), lambda i: (i, 0))
    outs, _ = _call(
        body, "inproj_bwd", (steps,),
        [_rows(t, D_MODEL), _full((1, D_MODEL)), _full((D_MODEL, IN_COLS))] + [_rows(t, RET_WIDTH)] * n_pieces + [_rows(t, D_MODEL)]
        + [adam_spec(a) for a in adam_arrays],
        [_rows(t, D_MODEL), _full((D_MODEL, IN_COLS)), _full((1, D_MODEL))] + [adam_spec(a) for a in adam_arrays[:n_adam]] * 3,
        [jax.ShapeDtypeStruct((seq, D_MODEL), F32), jax.ShapeDtypeStruct((D_MODEL, IN_COLS), F32),
         jax.ShapeDtypeStruct((1, D_MODEL), F32)] + [jax.ShapeDtypeStruct(a.shape, F32) for a in adam_arrays[:n_adam]] * 3,
        [], (x, gain, w_in, *dpieces, dres, *adam_arrays))
    stepped = outs[3:]
    return outs[:3], (stepped[:n_adam], stepped[n_adam:2 * n_adam], stepped[2 * n_adam:])


def _ret_consts():
    c = RET_CHUNK
    f32 = np.float32
    log_gamma = np.log(f32(1.0) - np.exp(np.linspace(math.log(1.0 / 32), math.log(1.0 / 512), RET_HEADS, dtype=f32))).astype(f32)
    idx = np.arange(c, dtype=f32)
    diff = idx[:, None] - idx[None, :]
    decay = np.where(diff[None] >= 0, np.exp(np.maximum(diff, f32(0.0))[None] * log_gamma[:, None, None]), f32(0.0))
    zeta = np.exp((c - 1 - idx)[None, :] * log_gamma[:, None])
    xi = np.exp((idx + f32(1.0))[None, :] * log_gamma[:, None])
    g_chunk = np.exp(f32(c) * log_gamma)
    wide = lambda rowvals: jnp.asarray(np.broadcast_to(rowvals[:, :, None], (RET_HEADS, c, c)).astype(f32))
    return (jnp.asarray(decay.astype(f32)), wide(xi), wide(zeta),
            jnp.asarray(np.broadcast_to(g_chunk[:, None, None], (RET_HEADS, c, c)).astype(f32)))


def _rope_tables(seq):
    f32 = np.float32
    half = HEAD_DIM // 2
    inv_freq = np.power(f32(ROPE_BASE), -np.arange(half, dtype=f32) / f32(half)).astype(f32)
    ang = (np.arange(seq, dtype=f32)[:, None] * inv_freq[None, :]).astype(f32)
    cos, sin = np.cos(ang).astype(f32), np.sin(ang).astype(f32)
    return jnp.asarray(np.concatenate([cos, cos], axis=1)), jnp.asarray(np.concatenate([-sin, sin], axis=1))


def _bmm(a, b):
    return lax.dot_general(a, b, (((2,), (1,)), ((0,), (0,))), preferred_element_type=F32)


def _bmm_nt(a, b):
    return lax.dot_general(a, b, (((2,), (2,)), ((0,), (0,))), preferred_element_type=F32)


def _bmm_tn(a, b):
    return lax.dot_general(a, b, (((1,), (1,)), ((0,), (0,))), preferred_element_type=F32)


def _ret_blocks(ref):
    return jnp.stack([ref[cc * RET_CHUNK:(cc + 1) * RET_CHUNK, hd * HEAD_DIM:(hd + 1) * HEAD_DIM]
                      for cc in range(RET_CHUNKS_PER_STEP) for hd in range(RET_HEADS)])


def _ret_unblock(ref, blocks):
    for cc in range(RET_CHUNKS_PER_STEP):
        for hd in range(RET_HEADS):
            ref[cc * RET_CHUNK:(cc + 1) * RET_CHUNK, hd * HEAD_DIM:(hd + 1) * HEAD_DIM] = blocks[cc * RET_HEADS + hd].astype(ref.dtype)


def _rope_blocks(ref):
    return jnp.stack([ref[cc * RET_CHUNK:(cc + 1) * RET_CHUNK, :] for cc in range(RET_CHUNKS_PER_STEP) for _ in range(RET_HEADS)])


def _per_chunk(ref):
    return jnp.concatenate([ref[...]] * RET_CHUNKS_PER_STEP, axis=0)


def _ret_gain(gain_ref):
    return jnp.stack([gain_ref[:, hd * HEAD_DIM:(hd + 1) * HEAD_DIM]
                      for _ in range(RET_CHUNKS_PER_STEP) for hd in range(RET_HEADS)])


def _ret_fwd(q, k, v, gate, gn_gain, consts, exchange=None):
    seq = q.shape[0]
    cps = RET_CHUNKS_PER_STEP
    t = cps * RET_CHUNK
    n_chunks = seq // RET_CHUNK
    dec, xi, zeta, gch = consts

    def body(q_ref, k_ref, v_ref, gate_ref, gain_ref, dec_ref, xi_ref, zeta_ref, gch_ref, y_ref, rprev_ref, state):
        @pl.when(pl.program_id(0) == 0)
        def _():
            state[...] = jnp.zeros_like(state)

        qb, kb, vb = _ret_blocks(q_ref), _ret_blocks(k_ref), _ret_blocks(v_ref)
        s = _bmm_nt(qb, kb) * _per_chunk(dec_ref)
        vz = (vb.astype(F32) * _per_chunk(zeta_ref)).astype(BF16)
        incr = _bmm_tn(kb, vz)
        gch_v = gch_ref[...]
        rp = state[...]
        before = []
        for cc in range(cps):
            before.append(rp.astype(BF16))
            rprev_ref[cc] = before[-1]
            rp = gch_v * rp + incr[cc * RET_HEADS:(cc + 1) * RET_HEADS]
        state[...] = rp
        qx = (qb.astype(F32) * _per_chunk(xi_ref)).astype(BF16)
        o = _bmm(s.astype(BF16), vb) + _bmm(qx, jnp.concatenate(before, axis=0))
        oc = o - jnp.mean(o, axis=-1, keepdims=True)
        on = oc * lax.rsqrt(jnp.mean(oc * oc, axis=-1, keepdims=True) + NORM_EPS)
        g = _ret_blocks(gate_ref)
        _ret_unblock(y_ref, g * _sigmoid(g) * (on * _ret_gain(gain_ref)))

    cst = _full((RET_HEADS, RET_CHUNK, RET_CHUNK))
    return _call(
        body, "ret_fwd", (seq // t,),
        [_rows(t, RET_WIDTH)] * 4 + [_full((1, RET_WIDTH)), cst, cst, cst, cst],
        [_rows(t, RET_WIDTH), pl.BlockSpec((cps, RET_HEADS, HEAD_DIM, HEAD_DIM), lambda i: (i, 0, 0, 0))],
        [jax.ShapeDtypeStruct((seq, RET_WIDTH), BF16), jax.ShapeDtypeStruct((n_chunks, RET_HEADS, HEAD_DIM, HEAD_DIM), BF16)],
        [pltpu.VMEM((RET_HEADS, HEAD_DIM, HEAD_DIM), F32)],
        (q, k, v, gate, gn_gain, dec, xi, zeta, gch), exchange)


def _ret_bwd(q, k, v, gate, gn_gain, consts, rprev, dy_ret, cos2, sin2, exchange=None):
    seq = q.shape[0]
    cps = RET_CHUNKS_PER_STEP
    t = cps * RET_CHUNK
    nt = seq // t
    dec, xi, zeta, gch = consts

    def body(q_ref, k_ref, v_ref, gate_ref, gain_ref, dec_ref, xi_ref, zeta_ref, gch_ref, rprev_ref, dyr_ref, cos_ref, sin_ref,
             dq_ref, dk_ref, dv_ref, dgate_ref, dgain_ref, dstate):
        @pl.when(pl.program_id(0) == 0)
        def _():
            dstate[...] = jnp.zeros_like(dstate)
            dgain_ref[...] = jnp.zeros_like(dgain_ref)

        nb = cps * RET_HEADS
        qb, kb, vb = _ret_blocks(q_ref), _ret_blocks(k_ref), _ret_blocks(v_ref)
        dec_b, xi_b, zeta_b = _per_chunk(dec_ref), _per_chunk(xi_ref), _per_chunk(zeta_ref)
        rpb = rprev_ref[...].reshape(nb, HEAD_DIM, HEAD_DIM)
        sdb = (_bmm_nt(qb, kb) * dec_b).astype(BF16)
        qx = (qb.astype(F32) * xi_b).astype(BF16)
        o = _bmm(sdb, vb) + _bmm(qx, rpb)
        oc = o - jnp.mean(o, axis=-1, keepdims=True)
        rstd = lax.rsqrt(jnp.mean(oc * oc, axis=-1, keepdims=True) + NORM_EPS)
        on = oc * rstd
        g = _ret_blocks(gate_ref)
        sg = _sigmoid(g)
        gain_b = _ret_gain(gain_ref)
        dyr = _ret_blocks(dyr_ref)
        _ret_unblock(dgate_ref, dyr * (on * gain_b) * (sg * (1.0 + g * (1.0 - sg))))
        dy = dyr * (g * sg)
        dgain = jnp.sum(dy * on, axis=1, keepdims=True)
        for hd in range(RET_HEADS):
            part = dgain[hd]
            for cc in range(1, cps):
                part = part + dgain[cc * RET_HEADS + hd]
            dgain_ref[:, hd * HEAD_DIM:(hd + 1) * HEAD_DIM] += part
        don = dy * gain_b
        do = rstd * (don - jnp.mean(don, axis=-1, keepdims=True) - on * jnp.mean(don * on, axis=-1, keepdims=True))
        dob = do.astype(BF16)
        ds = (_bmm_nt(dob, vb) * dec_b).astype(BF16)
        dq = _bmm(ds, kb) + _bmm_nt(dob, rpb) * xi_b
        dk = _bmm_tn(ds, qb)
        dv = _bmm_tn(sdb, dob)
        dstate_local = _bmm_tn(qx, dob)
        vz = (vb.astype(F32) * zeta_b).astype(BF16)
        gch_v = gch_ref[...]
        zeta_v = zeta_ref[...]
        gh = dstate[...]
        dk_extra, dv_extra = [None] * cps, [None] * cps
        for cc in reversed(range(cps)):
            sl = slice(cc * RET_HEADS, (cc + 1) * RET_HEADS)
            gb = gh.astype(BF16)
            dk_extra[cc] = _bmm_nt(vz[sl], gb)
            dv_extra[cc] = _bmm(kb[sl], gb) * zeta_v
            gh = dstate_local[sl] + gch_v * gh
        dstate[...] = gh
        cos_b, sin_b = _rope_blocks(cos_ref), _rope_blocks(sin_ref)
        _ret_unblock(dq_ref, _rope_bwd(dq, cos_b, sin_b))
        _ret_unblock(dk_ref, _rope_bwd((dk + jnp.concatenate(dk_extra, axis=0)) * K_SCALE, cos_b, sin_b))
        _ret_unblock(dv_ref, dv + jnp.concatenate(dv_extra, axis=0))

    cst = _full((RET_HEADS, RET_CHUNK, RET_CHUNK))
    rev = _rows_rev(t, RET_WIDTH, nt)
    act = jax.ShapeDtypeStruct((seq, RET_WIDTH), BF16)
    return _call(
        body, "ret_bwd", (nt,),
        [rev] * 4 + [_full((1, RET_WIDTH)), cst, cst, cst, cst,
                     pl.BlockSpec((cps, RET_HEADS, HEAD_DIM, HEAD_DIM), lambda i: (nt - 1 - i, 0, 0, 0)), rev,
                     _rows_rev(t, HEAD_DIM, nt), _rows_rev(t, HEAD_DIM, nt)],
        [rev] * 4 + [_full((1, RET_WIDTH))], [act, act, act, act, jax.ShapeDtypeStruct((1, RET_WIDTH), F32)],
        [pltpu.VMEM((RET_HEADS, HEAD_DIM, HEAD_DIM), F32)],
        (q, k, v, gate, gn_gain, dec, xi, zeta, gch, rprev, dy_ret, cos2, sin2), exchange)


MXU_TILE = 256
S5_STATE_TILES = 2 * N_STATES // MXU_TILE
S5_CHANNELS_PER_TILE = MXU_TILE // SSM_STATE * SSM_GROUP
S5_TILES_SHAPE = (S5_STATE_TILES, LANES, MXU_TILE)


def _chan_block(kk):
    return ((kk % (S5_STATE_TILES // 2)) * S5_CHANNELS_PER_TILE) // LANES * LANES


def _s5_tiles(blocks_re, blocks_im):
    per = MXU_TILE // SSM_STATE
    half = S5_STATE_TILES // 2
    eye = jnp.eye(per, dtype=F32)
    in_half = (jnp.arange(half) % (LANES // S5_CHANNELS_PER_TILE))[:, None] == jnp.arange(LANES // S5_CHANNELS_PER_TILE)[None, :]
    out = []
    for blk in (blocks_re, blocks_im):
        t = blk.reshape(half, per, SSM_GROUP, 1, SSM_STATE) * eye[None, :, None, :, None]
        t = t.reshape(half, 1, S5_CHANNELS_PER_TILE, MXU_TILE)
        out.append(jnp.where(in_half[:, :, None, None], t, 0.0).reshape(half, LANES, MXU_TILE))
    return jnp.concatenate(out, axis=0)


def _s5_discretise(lam_re, lam_im, log_dt, b_re, b_im, c_re, c_im):
    lam = lax.complex(jnp.minimum(lam_re, -1e-4), lam_im)
    dt = jnp.exp(log_dt)[:, None]
    lam_bar = jnp.exp(lam * dt)
    b_bar = ((lam_bar - 1.0) / lam)[:, :, None] * lax.complex(b_re, b_im)
    b_tiles = _s5_tiles(jnp.swapaxes(jnp.real(b_bar), 1, 2), jnp.swapaxes(jnp.imag(b_bar), 1, 2))
    return jnp.real(lam_bar), jnp.imag(lam_bar), b_tiles, _s5_tiles(c_re, -c_im)


def _s5_tables(lbr, lbi):
    lr = lbr.reshape(1, N_STATES)
    li = lbi.reshape(1, N_STATES)
    pr, pi = lr, li
    while pr.shape[0] < S5_STEPS:
        top_r, top_i = pr[-1:], pi[-1:]
        pr, pi = (jnp.concatenate([pr, pr * top_r - pi * top_i], axis=0), jnp.concatenate([pi, pr * top_i + pi * top_r], axis=0))
    bc = lambda a: jnp.broadcast_to(a, (SUBLANES, N_STATES))
    rep = lambda rows: jnp.broadcast_to(rows[:, None, :], (S5_STEPS, SUBLANES, N_STATES)).reshape(S5_STEPS * SUBLANES, N_STATES)
    return bc(lr), bc(li), rep(pr), rep(pi)


def _cmul(ar, ai, br, bi):
    return ar * br - ai * bi, ar * bi + ai * br


def _seg(i):
    return pl.ds(pl.multiple_of(i * SUBLANES, SUBLANES), SUBLANES)


def _scan_order():
    r = np.arange(S5_TILE)
    token = (r % SUBLANES) * S5_STEPS + r // SUBLANES
    p = (token[:, None] == np.arange(S5_TILE)[None, :]).astype(np.float32)
    return jnp.asarray(p, dtype=BF16), jnp.asarray(p.T, dtype=BF16)


def _to_scan_order(p_ref, rows_bf16):
    return _dot(p_ref[...], rows_bf16).astype(BF16)


def _from_scan_order(pt_ref, rows, pieces=3):
    pt = pt_ref[...]
    out = None
    for _ in range(pieces):
        piece = rows.astype(BF16)
        rows = rows - piece.astype(F32)
        term = _dot(pt, piece)
        out = term if out is None else out + term
    return out


def _s5_channels_to_states(chan_b, m_ref, xs):
    for kk in range(S5_STATE_TILES):
        lo = _chan_block(kk)
        res = _dot(chan_b[:, lo:lo + LANES], m_ref[kk])
        xs[2 * kk] = res[:, :LANES]
        xs[2 * kk + 1] = res[:, LANES:]


def _s5_states_to_channels(sb_ref, m_ref):
    out = []
    for lo in range(0, SSM_WIDTH, LANES):
        acc = None
        for kk in range(S5_STATE_TILES):
            if _chan_block(kk) == lo:
                term = _dot_nt(sb_ref[:, kk * MXU_TILE:(kk + 1) * MXU_TILE], m_ref[kk])
                acc = term if acc is None else acc + term
        out.append(acc)
    return jnp.concatenate(out, axis=1)


def _s5_outer_acc(acc_ref, chan_b, states_ref):
    for kk in range(S5_STATE_TILES):
        lo = _chan_block(kk)
        acc_ref[kk] += _dot_tn(chan_b[:, lo:lo + LANES], states_ref[:, kk * MXU_TILE:(kk + 1) * MXU_TILE])


def _s5_scan_tile(xs, lr_ref, li_ref, pr_ref, pi_ref, carry_re, carry_im, conj, reverse, prev_of=None):
    sgn = -1.0 if conj else 1.0
    seg_in_re, seg_in_im, sums_re, sums_im = [], [], [], []
    group = S5_GROUP
    for grp in range(S5_COLBLK // group):
        blks = [grp * group + j for j in range(group)]
        lrs = [lr_ref[:, b * LANES:(b + 1) * LANES] for b in blks]
        lis = [sgn * li_ref[:, b * LANES:(b + 1) * LANES] for b in blks]

        def step(it, carry, blks=blks, lrs=lrs, lis=lis):
            i = (S5_STEPS - 1 - it) if reverse else it
            out_r, out_i = [], []
            for j, b in enumerate(blks):
                xr, xi = _cmul(lrs[j], lis[j], carry[j], carry[group + j])
                xr = xr + xs[b, _seg(i), :]
                xi = xi + xs[S5_COLBLK + b, _seg(i), :]
                xs[b, _seg(i), :] = xr
                xs[S5_COLBLK + b, _seg(i), :] = xi
                out_r.append(xr)
                out_i.append(xi)
            return tuple(out_r + out_i)

        zeros = tuple(jnp.zeros((SUBLANES, LANES), F32) for _ in range(2 * group))
        ends = lax.fori_loop(0, S5_STEPS, step, zeros, unroll=True)

        ins_r, ins_i = [], []
        for j, b in enumerate(blks):
            cols = slice(b * LANES, (b + 1) * LANES)
            pr = pr_ref[SUBLANES * S5_STEPS - 1:SUBLANES * S5_STEPS, cols]
            pi = sgn * pi_ref[SUBLANES * S5_STEPS - 1:SUBLANES * S5_STEPS, cols]
            cur_r = carry_re[:, cols]
            cur_i = carry_im[:, cols]
            row_id = lax.broadcasted_iota(jnp.int32, (SUBLANES, LANES), 0)
            in_r = jnp.zeros((SUBLANES, LANES), F32)
            in_i = jnp.zeros((SUBLANES, LANES), F32)
            order = range(SUBLANES - 1, -1, -1) if reverse else range(SUBLANES)
            for sgm in order:
                in_r = jnp.where(row_id == sgm, cur_r, in_r)
                in_i = jnp.where(row_id == sgm, cur_i, in_i)
                mr, mi = _cmul(pr, pi, cur_r, cur_i)
                cur_r = mr + ends[j][sgm:sgm + 1, :]
                cur_i = mi + ends[group + j][sgm:sgm + 1, :]
            carry_re[:, cols] = cur_r
            carry_im[:, cols] = cur_i
            ins_r.append(in_r)
            ins_i.append(in_i)

        def fix(it, c, blks=blks, ins_r=ins_r, ins_i=ins_i):
            pw = (S5_STEPS - 1 - it) if reverse else it
            out = []
            for j, b in enumerate(blks):
                cols = slice(b * LANES, (b + 1) * LANES)
                prow = pl.ds(pl.multiple_of(pw * SUBLANES, SUBLANES), SUBLANES)
                fr, fi = _cmul(pr_ref[prow, cols], sgn * pi_ref[prow, cols], ins_r[j], ins_i[j])
                vr = xs[b, _seg(it), :] + fr
                vi = xs[S5_COLBLK + b, _seg(it), :] + fi
                xs[b, _seg(it), :] = vr
                xs[S5_COLBLK + b, _seg(it), :] = vi
                if prev_of is not None:
                    opr, opi, sr, si = c[4 * j:4 * j + 4]
                    out += [prev_of[0][b, _seg(it), :], prev_of[0][S5_COLBLK + b, _seg(it), :],
                            sr + vr * opr + vi * opi, si + vi * opr - vr * opi]
            return tuple(out)

        init = ()
        if prev_of is not None:
            zero = jnp.zeros((SUBLANES, LANES), F32)
            for b in blks:
                init += (prev_of[1][b], prev_of[2][b], zero, zero)
        done = lax.fori_loop(0, S5_STEPS, fix, init, unroll=True)
        for j in range(len(blks) if prev_of is not None else 0):
            sums_re.append(done[4 * j + 2])
            sums_im.append(done[4 * j + 3])
        seg_in_re += ins_r
        seg_in_im += ins_i
    return seg_in_re, seg_in_im, sums_re, sums_im


def _s5_pack(xs, dst):
    for b in range(2 * S5_COLBLK):
        dst[:, b * LANES:(b + 1) * LANES] = xs[b].astype(BF16)


def _s5_fwd(u, b_blk, c_blk, tables, d_skip, exchange=None):
    seq = u.shape[0]
    t = S5_TILE
    nt = seq // t
    lr8, li8, p_re, p_im = tables

    p, pt = _scan_order()

    def body(u_ref, p_ref, pt_ref, b_ref, c_ref, lr_ref, li_ref, pr_ref, pi_ref, d_ref, y_ref, start_ref,
             xs, xb, carry_re, carry_im):
        @pl.when(pl.program_id(0) == 0)
        def _():
            carry_re[...] = jnp.zeros_like(carry_re)
            carry_im[...] = jnp.zeros_like(carry_im)

        start_ref[0, 0:1, :] = carry_re[...]
        start_ref[0, 1:2, :] = carry_im[...]
        uv = u_ref[...]
        _s5_channels_to_states(_to_scan_order(p_ref, uv.astype(BF16)), b_ref, xs)
        _s5_scan_tile(xs, lr_ref, li_ref, pr_ref, pi_ref, carry_re, carry_im, conj=False, reverse=False)
        _s5_pack(xs, xb)
        y_ref[...] = _from_scan_order(pt_ref, _s5_states_to_channels(xb, c_ref)) + d_ref[...] * uv

    return _call(
        body, "s5_fwd", (nt,),
        [_rows(t, SSM_WIDTH), _full((t, t)), _full((t, t)), _full(S5_TILES_SHAPE), _full(S5_TILES_SHAPE),
         _full((SUBLANES, N_STATES)), _full((SUBLANES, N_STATES)), _full((SUBLANES * S5_STEPS, N_STATES)),
         _full((SUBLANES * S5_STEPS, N_STATES)), _full((1, SSM_WIDTH))],
        [_rows(t, SSM_WIDTH), pl.BlockSpec((1, 2, N_STATES), lambda i: (i, 0, 0))],
        [jax.ShapeDtypeStruct((seq, SSM_WIDTH), F32), jax.ShapeDtypeStruct((nt, 2, N_STATES), F32)],
        [pltpu.VMEM((2 * S5_COLBLK, t, LANES), F32), pltpu.VMEM((t, 2 * N_STATES), BF16),
         pltpu.VMEM((1, N_STATES), F32), pltpu.VMEM((1, N_STATES), F32)],
        (u, p, pt, b_blk, c_blk, lr8, li8, p_re, p_im, d_skip), exchange)


def _s5_bwd(u, dy, b_blk, c_blk, tables, d_skip, starts, exchange=None):
    seq = u.shape[0]
    t = S5_TILE
    nt = seq // t
    lr8, li8, p_re, p_im = tables
    p, pt = _scan_order()

    def body(u_ref, dy_ref, p_ref, pt_ref, b_ref, c_ref, lr_ref, li_ref, pr_ref, pi_ref, d_ref, start_ref,
             du_ref, dd_ref, dlam_ref, db_ref, dc_ref,
             xs, as_, xb, ab, carry_re, carry_im, acar_re, acar_im):
        @pl.when(pl.program_id(0) == 0)
        def _():
            acar_re[...] = jnp.zeros_like(acar_re)
            acar_im[...] = jnp.zeros_like(acar_im)
            dd_ref[...] = jnp.zeros_like(dd_ref)
            dlam_ref[...] = jnp.zeros_like(dlam_ref)
            db_ref[...] = jnp.zeros_like(db_ref)
            dc_ref[...] = jnp.zeros_like(dc_ref)

        uv = u_ref[...]
        dyv = dy_ref[...]
        ub = _to_scan_order(p_ref, uv.astype(BF16))
        dyb = _to_scan_order(p_ref, dyv.astype(BF16))
        carry_re[...] = start_ref[0, 0:1, :]
        carry_im[...] = start_ref[0, 1:2, :]
        _s5_channels_to_states(ub, b_ref, xs)
        in_re, in_im, _, _ = _s5_scan_tile(xs, lr_ref, li_ref, pr_ref, pi_ref, carry_re, carry_im, conj=False, reverse=False)
        _s5_channels_to_states(dyb, c_ref, as_)
        _, _, dl_re, dl_im = _s5_scan_tile(as_, lr_ref, li_ref, pr_ref, pi_ref, acar_re, acar_im, conj=True, reverse=True,
                                           prev_of=(xs, in_re, in_im))
        for b in range(S5_COLBLK):
            cols = slice(b * LANES, (b + 1) * LANES)
            dlam_ref[0, :, cols] += dl_re[b]
            dlam_ref[1, :, cols] += dl_im[b]
        _s5_pack(xs, xb)
        _s5_pack(as_, ab)
        du_ref[...] = (_from_scan_order(pt_ref, _s5_states_to_channels(ab, b_ref), pieces=2) + d_ref[...] * dyv).astype(BF16)
        dd_ref[...] += jnp.sum(dyv * uv, axis=0, keepdims=True)
        _s5_outer_acc(db_ref, ub, ab)
        _s5_outer_acc(dc_ref, dyb, xb)

    rev = _rows_rev(t, SSM_WIDTH, nt)
    vec = lambda: pltpu.VMEM((1, N_STATES), F32)
    outer = S5_TILES_SHAPE
    return _call(
        body, "s5_bwd", (nt,),
        [rev, rev, _full((t, t)), _full((t, t)), _full(S5_TILES_SHAPE), _full(S5_TILES_SHAPE),
         _full((SUBLANES, N_STATES)), _full((SUBLANES, N_STATES)), _full((SUBLANES * S5_STEPS, N_STATES)),
         _full((SUBLANES * S5_STEPS, N_STATES)), _full((1, SSM_WIDTH)),
         pl.BlockSpec((1, 2, N_STATES), lambda i: (nt - 1 - i, 0, 0))],
        [rev, _full((1, SSM_WIDTH)), _full((2, SUBLANES, N_STATES)), _full(outer), _full(outer)],
        [jax.ShapeDtypeStruct((seq, SSM_WIDTH), BF16), jax.ShapeDtypeStruct((1, SSM_WIDTH), F32),
         jax.ShapeDtypeStruct((2, SUBLANES, N_STATES), F32), jax.ShapeDtypeStruct(outer, F32), jax.ShapeDtypeStruct(outer, F32)],
        [pltpu.VMEM((2 * S5_COLBLK, t, LANES), F32), pltpu.VMEM((2 * S5_COLBLK, t, LANES), F32),
         pltpu.VMEM((t, 2 * N_STATES), BF16), pltpu.VMEM((t, 2 * N_STATES), BF16), vec(), vec(), vec(), vec()],
        (u, dy, p, pt, b_blk, c_blk, lr8, li8, p_re, p_im, d_skip, starts), exchange)


def _mix_fwd(y_ssm, y_ret, x, gain, w_glu, w_out, exchange=None):
    seq = x.shape[0]
    t = TOK_TILE

    def body(ys_ref, yr_ref, x_ref, g_ref, wg_ref, wo_ref, x1_ref, mix_ref):
        g0 = _gelu(ys_ref[...]).astype(BF16)
        z = _dot(g0, wg_ref[...])
        glu = (z[:, :SSM_WIDTH] * _sigmoid(z[:, SSM_WIDTH:])).astype(BF16)
        mix = _dot(yr_ref[...], wo_ref[:RET_WIDTH, :]) + _dot(glu, wo_ref[RET_WIDTH:, :])
        mix_ref[...] = mix
        n, _ = _rms(mix)
        x1_ref[...] = x_ref[...] + n * g_ref[...]

    act = jax.ShapeDtypeStruct((seq, D_MODEL), F32)
    return _call(
        body, "mix_fwd", (seq // t,),
        [_rows(t, SSM_WIDTH), _rows(t, RET_WIDTH), _rows(t, D_MODEL), _full((1, D_MODEL)),
         _full((SSM_WIDTH, 2 * SSM_WIDTH)), _full((D_MODEL, D_MODEL))],
        [_rows(t, D_MODEL)] * 2, [act, act], [], (y_ssm, y_ret, x, gain, w_glu, w_out), exchange)


def _mix_bwd(dx1, mix, gain, y_ssm, y_ret, w_glu, w_out, exchange=None):
    seq = dx1.shape[0]
    t = TOK_TILE

    def body(dx1_ref, mix_ref, g_ref, ys_ref, yr_ref, wg_ref, wo_ref, dyr_ref, dys_ref, dwo_ref, dwg_ref, dg_ref):
        @pl.when(pl.program_id(0) == 0)
        def _():
            dwo_ref[...] = jnp.zeros_like(dwo_ref)
            dwg_ref[...] = jnp.zeros_like(dwg_ref)
            dg_ref[...] = jnp.zeros_like(dg_ref)

        n, r = _rms(mix_ref[...])
        dmix, dgs = _rms_bwd(n, r, g_ref[...], dx1_ref[...])
        dg_ref[...] += dgs
        dmb = dmix.astype(BF16)
        dcat = _dot_nt(dmb, wo_ref[...])
        dyr_ref[...] = dcat[:, :RET_WIDTH]
        dglu = dcat[:, RET_WIDTH:]
        ys = ys_ref[...]
        g0 = _gelu(ys).astype(BF16)
        z = _dot(g0, wg_ref[...])
        a = z[:, :SSM_WIDTH]
        sb = _sigmoid(z[:, SSM_WIDTH:])
        dwo_ref[:RET_WIDTH, :] += _dot_tn(yr_ref[...], dmb)
        dwo_ref[RET_WIDTH:, :] += _dot_tn((a * sb).astype(BF16), dmb)
        dz = jnp.concatenate([dglu * sb, dglu * a * sb * (1.0 - sb)], axis=1).astype(BF16)
        dwg_ref[...] += _dot_tn(g0, dz)
        dys_ref[...] = _dot_nt(dz, wg_ref[...]) * _gelu_grad(ys)

    half = jax.ShapeDtypeStruct((seq, RET_WIDTH), F32)
    return _call(
        body, "mix_bwd", (seq // t,),
        [_rows(t, D_MODEL), _rows(t, D_MODEL), _full((1, D_MODEL)), _rows(t, SSM_WIDTH), _rows(t, RET_WIDTH),
         _full((SSM_WIDTH, 2 * SSM_WIDTH)), _full((D_MODEL, D_MODEL))],
        [_rows(t, RET_WIDTH), _rows(t, SSM_WIDTH), _full((D_MODEL, D_MODEL)), _full((SSM_WIDTH, 2 * SSM_WIDTH)),
         _full((1, D_MODEL))],
        [half, half, jax.ShapeDtypeStruct((D_MODEL, D_MODEL), F32), jax.ShapeDtypeStruct((SSM_WIDTH, 2 * SSM_WIDTH), F32),
         jax.ShapeDtypeStruct((1, D_MODEL), F32)],
        [], (dx1, mix, gain, y_ssm, y_ret, w_glu, w_out), exchange)


def _mlp_a(x1, target, gain_pre, gain_post, w1, w2):
    seq = x1.shape[0]
    t = MLP_TILE

    def body(x1_ref, tg_ref, gp_ref, gq_ref, w1_ref, w2_ref, df_ref, dx2_ref, dw2_ref, dgq_ref, sq_ref):
        @pl.when(pl.program_id(0) == 0)
        def _():
            dw2_ref[...] = jnp.zeros_like(dw2_ref)
            dgq_ref[...] = jnp.zeros_like(dgq_ref)
            sq_ref[...] = jnp.zeros_like(sq_ref)

        x1v = x1_ref[...]
        n3, _ = _rms(x1v)
        h = (n3 * gp_ref[...]).astype(BF16)
        rl = jnp.maximum(_dot(h, w1_ref[...]), 0.0)
        act = (rl * rl).astype(BF16)
        n4, r4 = _rms(_dot(act, w2_ref[...]))
        gq = gq_ref[...]
        err = x1v + n4 * gq - tg_ref[...]
        sq_ref[...] += jnp.sum(err * err, axis=0, keepdims=True)
        dx2 = err * (1.0 / D_MODEL)
        dx2_ref[...] = dx2
        dm, dgs = _rms_bwd(n4, r4, gq, dx2)
        dgq_ref[...] += dgs
        dmb = dm.astype(BF16)
        dw2_ref[...] += _dot_tn(act, dmb)
        df_ref[...] = (_dot_nt(dmb, w2_ref[...]) * (2.0 * rl)).astype(BF16)

    return pl.pallas_call(
        body, name="mlp_a", grid=(seq // t,),
        in_specs=[_rows(t, D_MODEL), _rows(t, D_MODEL), _full((1, D_MODEL)), _full((1, D_MODEL)),
                  _full((D_MODEL, D_FF)), _full((D_FF, D_MODEL))],
        out_specs=[_rows(t, D_FF), _rows(t, D_MODEL), _full((D_FF, D_MODEL)), _full((1, D_MODEL)), _full((1, D_MODEL))],
        out_shape=[jax.ShapeDtypeStruct((seq, D_FF), BF16), jax.ShapeDtypeStruct((seq, D_MODEL), F32),
                   jax.ShapeDtypeStruct((D_FF, D_MODEL), F32), jax.ShapeDtypeStruct((1, D_MODEL), F32),
                   jax.ShapeDtypeStruct((1, D_MODEL), F32)],
        compiler_params=_params("arbitrary"),
    )(x1, target, gain_pre, gain_post, w1, w2)


def _mlp_b(df, dx2, x1, gain_pre, w1):
    seq = x1.shape[0]
    t = TOK_TILE

    def body(df_ref, dx2_ref, x1_ref, gp_ref, w1_ref, dx1_ref, dw1_ref, dgp_ref):
        @pl.when(pl.program_id(0) == 0)
        def _():
            dw1_ref[...] = jnp.zeros_like(dw1_ref)
            dgp_ref[...] = jnp.zeros_like(dgp_ref)

        n3, r3 = _rms(x1_ref[...])
        gp = gp_ref[...]
        h = (n3 * gp).astype(BF16)
        dfv = df_ref[...]
        dw1_ref[...] += _dot_tn(h, dfv)
        dx, dgs = _rms_bwd(n3, r3, gp, _dot_nt(dfv, w1_ref[...]))
        dgp_ref[...] += dgs
        dx1_ref[...] = dx2_ref[...] + dx

    return pl.pallas_call(
        body, name="mlp_b", grid=(seq // t,),
        in_specs=[_rows(t, D_FF), _rows(t, D_MODEL), _rows(t, D_MODEL), _full((1, D_MODEL)), _full((D_MODEL, D_FF))],
        out_specs=[_rows(t, D_MODEL), _full((D_MODEL, D_FF)), _full((1, D_MODEL))],
        out_shape=[jax.ShapeDtypeStruct((seq, D_MODEL), F32), jax.ShapeDtypeStruct((D_MODEL, D_FF), F32),
                   jax.ShapeDtypeStruct((1, D_MODEL), F32)],
        compiler_params=_params("arbitrary"),
    )(df, dx2, x1, gain_pre, w1)


def _local_step(x, target, small, weights, ids=None, late_state=None):
    cos2, sin2 = _rope_tables(x.shape[0])
    ret_consts = _ret_consts()

    s5_names = ("ssm_lambda_re", "ssm_lambda_im", "ssm_log_dt", "ssm_b_re", "ssm_b_im", "ssm_c_re", "ssm_c_im")
    (lbr, lbi, b_tiles, c_tiles), disc_vjp = jax.vjp(_s5_discretise, *[small[name] for name in s5_names])
    tables = _s5_tables(lbr, lbi)
    b_blk = b_tiles.astype(BF16)
    c_blk = c_tiles.astype(BF16)
    d_skip = small["ssm_d"].reshape(1, SSM_WIDTH)
    gn_gain = small["ret_gn_gain"].reshape(1, RET_WIDTH)
    g_mix_pre = small["norm_mix_pre"].reshape(1, D_MODEL)
    g_mix_post = small["norm_mix_post"].reshape(1, D_MODEL)
    g_mlp_pre = small["norm_mlp_pre"].reshape(1, D_MODEL)
    g_mlp_post = small["norm_mlp_post"].reshape(1, D_MODEL)

    dist = ids is not None
    w_in, w_glu, w_out, w_ff1, w_ff2 = weights
    (q, k, v, gate, u), got = _inproj_fwd(x, g_mix_pre, w_in, cos2, sin2, _x_gather_send((1, 2), [w_glu, w_out]) if dist else None)
    if dist:
        w_glu, w_out = got
    (y_ssm, starts), got = _s5_fwd(u, b_blk, c_blk, tables, d_skip, _x_gather_send(LATE, [w_ff1, w_ff2]) if dist else None)
    if dist:
        w_ff1, w_ff2 = got
    (y_ret, rprev), got = _ret_fwd(q, k, v, gate, gn_gain, ret_consts,
                                   _x_gather_pass(MID + LATE, [w_glu, w_out, w_ff1, w_ff2]) if dist else None)
    if dist:
        w_glu, w_out, w_ff1, w_ff2 = got
    (x1, mix), _ = _mix_fwd(y_ssm, y_ret, x, g_mix_post, w_glu, w_out)
    df, dx2, d_ff2, dg_mlp_post, sq = _mlp_a(x1, target, g_mlp_pre, g_mlp_post, w_ff1, w_ff2)
    dx1, d_ff1, dg_mlp_pre = _mlp_b(df, dx2, x1, g_mlp_pre, w_ff1)
    (dy_ret, dy_ssm, d_out, d_glu, dg_mix_post), got = _mix_bwd(dx1, mix, g_mix_post, y_ssm, y_ret, w_glu, w_out,
                                                               _x_pair(LATE, [d_ff1, d_ff2]) if dist else None)
    if dist:
        sums = _pair_sum(ids, LATE, [d_ff1, d_ff2], got)
    (du, dd, dlam8, db_tiles, dc_tiles), got = _s5_bwd(
        u, dy_ssm, b_blk, c_blk, tables, d_skip, starts,
        _x_join(_x_chip(LATE, sums), _x_pair(MID, [d_glu, d_out])) if dist else None)
    if dist:
        parts = _chip_sum(ids, LATE, sums, got[:len(LATE)])
        sums = _pair_sum(ids, MID, [d_glu, d_out], got[len(LATE):])
    (dq, dk, dv, dgate, dgn), got = _ret_bwd(q, k, v, gate, gn_gain, ret_consts, rprev, dy_ret, cos2, sin2,
                                             _x_join(_x_share(LATE, parts), _x_chip(MID, sums)) if dist else None)
    if dist:
        late_grads = got[:len(LATE)]
        parts = _chip_sum(ids, MID, sums, got[len(LATE):])
        mid_parts = parts
    adam = (late_state[0], late_grads, late_state[1], late_state[2]) if dist else ()
    (dx, d_in, dg_mix_pre), late_stepped = _inproj_bwd(x, g_mix_pre, w_in, [dq, dk, dv, dgate, du], dx1, adam)

    dlam = jnp.sum(dlam8, axis=1)
    s5_grads = disc_vjp((dlam[0].reshape(SSM_GROUPS, SSM_STATE), dlam[1].reshape(SSM_GROUPS, SSM_STATE), db_tiles, dc_tiles))
    small_grads = {
        "norm_mix_pre": dg_mix_pre, "norm_mix_post": dg_mix_post, "ret_gn_gain": dgn, "ssm_d": dd,
        "norm_mlp_pre": dg_mlp_pre, "norm_mlp_post": dg_mlp_post, "loss_sum": 0.5 * jnp.sum(sq) / D_MODEL,
    }
    small_grads.update(dict(zip(s5_names, s5_grads)))
    if dist:
        return dx, d_in, mid_parts, late_grads, late_stepped, small_grads
    return dx, (d_in, d_glu, d_out, d_ff1, d_ff2), small_grads


BIG = (
    ("w_in", D_MODEL, IN_COLS, 1),
    ("w_glu", SSM_WIDTH, 2 * SSM_WIDTH, 1),
    ("w_out", D_MODEL, D_MODEL, 0),
    ("w_ff1", D_MODEL, D_FF, 1),
    ("w_ff2", D_FF, D_MODEL, 0),
)
SMALL = (
    ("norm_mix_pre", (D_MODEL,)), ("norm_mix_post", (D_MODEL,)), ("ret_gn_gain", (RET_WIDTH,)),
    ("ssm_lambda_re", (SSM_GROUPS, SSM_STATE)), ("ssm_lambda_im", (SSM_GROUPS, SSM_STATE)), ("ssm_log_dt", (SSM_GROUPS,)),
    ("ssm_b_re", (SSM_GROUPS, SSM_STATE, SSM_GROUP)), ("ssm_b_im", (SSM_GROUPS, SSM_STATE, SSM_GROUP)),
    ("ssm_c_re", (SSM_GROUPS, SSM_GROUP, SSM_STATE)), ("ssm_c_im", (SSM_GROUPS, SSM_GROUP, SSM_STATE)),
    ("ssm_d", (SSM_WIDTH,)), ("norm_mlp_pre", (D_MODEL,)), ("norm_mlp_post", (D_MODEL,)),
)
SMALL_ROWS = 1152
EXCH_TILES = 8


PACKED = SMALL + (("loss_sum", (1,)),)


def _packed_rows(shape):
    return -(-math.prod(shape) // (SUBLANES * LANES)) * SUBLANES


def _pack_small(tree):
    rows = []
    for name, shape in PACKED:
        size = math.prod(shape)
        flat = tree[name].reshape(-1).astype(F32) if name in tree else jnp.zeros((size,), F32)
        rows.append(jnp.pad(flat, (0, _packed_rows(shape) * LANES - size)).reshape(-1, LANES))
    used = sum(r.shape[0] for r in rows)
    rows.append(jnp.zeros((SMALL_ROWS - used, LANES), F32))
    return jnp.concatenate(rows, axis=0)


def _unpack_small(packed):
    out, row = {}, 0
    for name, shape in PACKED:
        size = math.prod(shape)
        out[name] = packed[row:row + _packed_rows(shape)].reshape(-1)[:size].reshape((1,) + shape)
        row += _packed_rows(shape)
    return out


def _half_shape(r, c, axis):
    return (r // 2, c) if axis == 1 else (r, c // 2)


def _region_shape(r, c, axis):
    return (r // 2, c // N_CHIPS) if axis == 1 else (r // N_CHIPS, c // 2)


def _shard_shape(r, c, axis):
    return (r, c // N_CHIPS) if axis == 1 else (r // N_CHIPS, c)


def _ds(start, size):
    return pl.ds(pl.multiple_of(start * size, size), size)


def _region_of_full(ref, r, c, axis, shard, half):
    if axis == 1:
        return ref.at[_ds(half, r // 2), _ds(shard, c // N_CHIPS)]
    return ref.at[_ds(shard, r // N_CHIPS), _ds(half, c // 2)]


def _half_of_full(ref, r, c, axis, half):
    if axis == 1:
        return ref.at[_ds(half, r // 2), :]
    return ref.at[:, _ds(half, c // 2)]


def _half_of_shard(ref, r, c, axis, half):
    if axis == 1:
        return ref.at[_ds(half, r // 2), :]
    return ref.at[:, _ds(half, c // 2)]


def _region_of_half(ref, r, c, axis, shard):
    if axis == 1:
        return ref.at[:, _ds(shard, c // N_CHIPS)]
    return ref.at[_ds(shard, r // N_CHIPS), :]


def _place():
    x, y, c = lax.axis_index("x"), lax.axis_index("y"), lax.axis_index("c")
    chips = [(1 - x, y), (x, 1 - y), (1 - x, 1 - y)]
    return x, y, c, chips


LATE = (3, 4)
MID = (1, 2)
FIRST = (0,)
SMALL_HALF = (SMALL_ROWS // 2, LANES)


def _remote(src, dst, send_sem, recv_sem, to):
    return pltpu.make_async_remote_copy(src_ref=src, dst_ref=dst, send_sem=send_sem, recv_sem=recv_sem,
                                        device_id=to, device_id_type=MESH)


def _same(arrays):
    return [jax.ShapeDtypeStruct(a.shape, a.dtype) for a in arrays]


def _x_gather_send(ws, fulls):
    def copies(ins, outs, send_sems, recv_sems):
        x, y, c, chips = _place()
        out = []
        for j, w in enumerate(ws):
            _, r, cc, axis = BIG[w]
            mine = _region_of_full(outs[j], r, cc, axis, 2 * x + y, c)
            out += [_remote(mine, mine, send_sems.at[3 * j + k], recv_sems.at[3 * j + k], (cx, cy, c))
                    for k, (cx, cy) in enumerate(chips)]
        return out

    return _Exchange(fulls, _same(fulls), {j: j for j in range(len(ws))}, 3 * len(ws), copies)


def _x_gather_pass(ws, fulls):
    def copies(ins, outs, send_sems, recv_sems):
        x, y, c, chips = _place()
        out = []
        for j, w in enumerate(ws):
            _, r, cc, axis = BIG[w]
            for k, (cx, cy) in enumerate(chips):
                landed = _region_of_full(outs[j], r, cc, axis, 2 * cx + cy, c)
                out.append(_remote(landed, landed, send_sems.at[3 * j + k], recv_sems.at[3 * j + k], (x, y, 1 - c)))
        return out

    return _Exchange(fulls, _same(fulls), {j: j for j in range(len(ws))}, 3 * len(ws), copies)


def _x_pair(ws, grads, small=None):
    def copies(ins, outs, send_sems, recv_sems):
        x, y, c, _ = _place()
        out = []
        for j, w in enumerate(ws):
            _, r, cc, axis = BIG[w]
            out.append(_remote(_half_of_full(ins[j], r, cc, axis, 1 - c), outs[j], send_sems.at[j], recv_sems.at[j], (x, y, 1 - c)))
        if small is not None:
            j = len(ws)
            out.append(_remote(ins[j].at[_ds(1 - c, SMALL_ROWS // 2), :], outs[j], send_sems.at[j], recv_sems.at[j], (x, y, 1 - c)))
        return out

    shapes = [jax.ShapeDtypeStruct(_half_shape(*BIG[w][1:]), F32) for w in ws]
    extra = [] if small is None else [small]
    return _Exchange(list(grads) + extra, shapes + [jax.ShapeDtypeStruct(SMALL_HALF, F32)] * len(extra), {},
                     len(ws) + len(extra), copies)


def _x_chip(ws, sums_bf16, small_sum=None):
    def copies(ins, outs, send_sems, recv_sems):
        x, y, c, chips = _place()
        out = []
        for j, w in enumerate(ws):
            _, r, cc, axis = BIG[w]
            out += [_remote(_region_of_half(ins[j], r, cc, axis, 2 * cx + cy), outs[j].at[k],
                            send_sems.at[3 * j + k], recv_sems.at[3 * j + k], (cx, cy, c)) for k, (cx, cy) in enumerate(chips)]
        if small_sum is not None:
            j = len(ws)
            out += [_remote(ins[j], outs[j].at[k], send_sems.at[3 * j + k], recv_sems.at[3 * j + k], (cx, cy, c))
                    for k, (cx, cy) in enumerate(chips)]
        return out

    shapes = [jax.ShapeDtypeStruct((3,) + _region_shape(*BIG[w][1:]), BF16) for w in ws]
    extra = [] if small_sum is None else [small_sum]
    return _Exchange(list(sums_bf16) + extra, shapes + [jax.ShapeDtypeStruct((3,) + SMALL_HALF, F32)] * len(extra), {},
                     3 * (len(ws) + len(extra)), copies)


def _x_share(ws, shards, small=None):
    def copies(ins, outs, send_sems, recv_sems):
        x, y, c, _ = _place()
        out = []
        for j, w in enumerate(ws):
            _, r, cc, axis = BIG[w]
            mine = _half_of_shard(outs[j], r, cc, axis, c)
            out.append(_remote(mine, mine, send_sems.at[j], recv_sems.at[j], (x, y, 1 - c)))
        if small is not None:
            j = len(ws)
            mine = outs[j].at[_ds(c, SMALL_ROWS // 2), :]
            out.append(_remote(mine, mine, send_sems.at[j], recv_sems.at[j], (x, y, 1 - c)))
        return out

    arrays = list(shards) + ([] if small is None else [small])
    return _Exchange(arrays, _same(arrays), {j: j for j in range(len(arrays))}, len(arrays), copies)


class _Offset:
    def __init__(self, sems, base):
        self._sems, self._base = sems, base

    @property
    def at(self):
        return self

    def __getitem__(self, i):
        return self._sems.at[self._base + i]


def _x_join(a, b):
    ka_in, ka_out = len(a.operands), len(a.out_shapes)

    def copies(ins, outs, send_sems, recv_sems):
        return (a.copies(ins[:ka_in], outs[:ka_out], send_sems, recv_sems)
                + b.copies(ins[ka_in:], outs[ka_out:], _Offset(send_sems, a.n), _Offset(recv_sems, a.n)))

    aliases = dict(a.aliases)
    aliases.update({ka_in + i: ka_out + o for i, o in b.aliases.items()})
    return _Exchange(a.operands + b.operands, a.out_shapes + b.out_shapes, aliases, a.n + b.n, copies)


def _run_exchange(name, exchange):
    k_in, k_out = len(exchange.operands), len(exchange.out_shapes)

    def body(*refs):
        copies = exchange.copies(refs[:k_in], refs[k_in:k_in + k_out], refs[-2], refs[-1])
        for cp in copies:
            cp.start()
        for cp in copies:
            cp.wait()

    return list(pl.pallas_call(
        body, name=name, in_specs=[ANY] * k_in, out_specs=[ANY] * k_out, out_shape=exchange.out_shapes,
        scratch_shapes=[pltpu.SemaphoreType.DMA((exchange.n,)), pltpu.SemaphoreType.DMA((exchange.n,))],
        input_output_aliases=dict(exchange.aliases),
    )(*exchange.operands))


def _gather_now(name, ws, fulls):
    send, onward = _x_gather_send(ws, fulls), _x_gather_pass(ws, fulls)
    k = len(fulls)

    def body(*refs):
        ins, outs, (send_a, recv_a, send_b, recv_b) = refs[:k], refs[k:2 * k], refs[2 * k:]
        first = send.copies(ins, outs, send_a, recv_a)
        second = onward.copies(ins, outs, send_b, recv_b)
        for cp in first:
            cp.start()
        for landed, cp in zip(first, second):
            landed.wait()
            cp.start()
        for cp in second:
            cp.wait()

    sems = lambda n: pltpu.SemaphoreType.DMA((n,))
    return list(pl.pallas_call(
        body, name=name, in_specs=[ANY] * k, out_specs=[ANY] * k, out_shape=send.out_shapes,
        scratch_shapes=[sems(send.n), sems(send.n), sems(onward.n), sems(onward.n)],
        input_output_aliases={j: j for j in range(k)},
    )(*fulls))


def _tile_rows(shape):
    return (shape[0] // EXCH_TILES, shape[1])


def _cast_place(ids, shards):
    def body(ids_ref, *refs):
        for j in range(len(BIG)):
            refs[len(BIG) + j][...] = refs[j][...].astype(BF16)

    def out_spec(w):
        _, r, cc, axis = BIG[w]
        tile = _tile_rows(_shard_shape(r, cc, axis))
        if axis == 1:
            return pl.BlockSpec(tile, lambda i, ids: (i, ids[1]))
        return pl.BlockSpec(tile, lambda i, ids: (ids[1] * EXCH_TILES + i, 0))

    return list(pl.pallas_call(
        body, name="cast_place",
        grid_spec=pltpu.PrefetchScalarGridSpec(
            num_scalar_prefetch=1, grid=(EXCH_TILES,),
            in_specs=[pl.BlockSpec(_tile_rows(_shard_shape(r, cc, axis)), lambda i, ids: (i, 0)) for _, r, cc, axis in BIG],
            out_specs=[out_spec(w) for w in range(len(BIG))]),
        out_shape=[jax.ShapeDtypeStruct((r, cc), BF16) for _, r, cc, _ in BIG],
        compiler_params=pltpu.CompilerParams(dimension_semantics=("parallel",)),
    )(ids, *shards))


def _pair_sum(ids, ws, grads, got, small=None):
    nb = len(ws)
    n = nb + (small is not None)
    halves = [_half_shape(*BIG[w][1:]) for w in ws] + [SMALL_HALF] * (n - nb)

    def body(ids_ref, *refs):
        for j in range(n):
            total = refs[j][...] + refs[n + j][...]
            refs[2 * n + j][...] = total.astype(refs[2 * n + j].dtype)

    def mine_spec(j):
        tile = _tile_rows(halves[j])
        if j < nb and BIG[ws[j]][3] == 0:
            return pl.BlockSpec(tile, lambda i, ids: (i, ids[0]))
        return pl.BlockSpec(tile, lambda i, ids: (ids[0] * EXCH_TILES + i, 0))

    plain = lambda j: pl.BlockSpec(_tile_rows(halves[j]), lambda i, ids: (i, 0))
    outs = pl.pallas_call(
        body, name="pair_sum",
        grid_spec=pltpu.PrefetchScalarGridSpec(
            num_scalar_prefetch=1, grid=(EXCH_TILES,),
            in_specs=[mine_spec(j) for j in range(n)] + [plain(j) for j in range(n)],
            out_specs=[plain(j) for j in range(n)]),
        out_shape=[jax.ShapeDtypeStruct(h, BF16 if j < nb else F32) for j, h in enumerate(halves)],
        compiler_params=pltpu.CompilerParams(dimension_semantics=("parallel",)),
    )(ids, *grads, *([] if small is None else [small]), *got)
    return list(outs)


def _chip_sum(ids, ws, sums, arrived):
    nb, n = len(ws), len(sums)
    regions = [_region_shape(*BIG[w][1:]) for w in ws] + [SMALL_HALF] * (n - nb)

    def body(ids_ref, *refs):
        for j in range(n):
            own, got, out = refs[j], refs[n + j], refs[2 * n + j]
            if j < nb:
                out[...] = ((own[...].astype(F32) + got[0].astype(F32)) + got[1].astype(F32)) + got[2].astype(F32)
            else:
                out[...] = (own[...] + got[1]) + (got[0] + got[2])

    def own_spec(j):
        tile = _tile_rows(regions[j])
        if j >= nb:
            return pl.BlockSpec(tile, lambda i, ids: (i, 0))
        if BIG[ws[j]][3] == 1:
            return pl.BlockSpec(tile, lambda i, ids: (i, ids[1]))
        return pl.BlockSpec(tile, lambda i, ids: (ids[1] * EXCH_TILES + i, 0))

    def out_spec(j):
        tile = _tile_rows(regions[j])
        if j < nb and BIG[ws[j]][3] == 0:
            return pl.BlockSpec(tile, lambda i, ids: (i, ids[0]))
        return pl.BlockSpec(tile, lambda i, ids: (ids[0] * EXCH_TILES + i, 0))

    got_spec = lambda j: pl.BlockSpec((3,) + _tile_rows(regions[j]), lambda i, ids: (0, i, 0))
    return list(pl.pallas_call(
        body, name="chip_sum",
        grid_spec=pltpu.PrefetchScalarGridSpec(
            num_scalar_prefetch=1, grid=(EXCH_TILES,),
            in_specs=[own_spec(j) for j in range(n)] + [got_spec(j) for j in range(n)],
            out_specs=[out_spec(j) for j in range(n)]),
        out_shape=[jax.ShapeDtypeStruct(_shard_shape(*BIG[w][1:]), F32) for w in ws]
        + [jax.ShapeDtypeStruct((SMALL_ROWS, LANES), F32)] * (n - nb),
        compiler_params=pltpu.CompilerParams(dimension_semantics=("parallel",)),
    )(ids, *sums, *arrived))


def _adamw_body(n):
    c1 = 1.0 - ADAM_B1 ** ADAM_STEP
    c2 = 1.0 - ADAM_B2 ** ADAM_STEP

    def body(*refs):
        for j in range(n):
            w, g, m, v = (refs[k * n + j][...] for k in range(4))
            m = ADAM_B1 * m + (1.0 - ADAM_B1) * g
            v = ADAM_B2 * v + (1.0 - ADAM_B2) * (g * g)
            refs[4 * n + j][...] = -ADAM_LR * ((m / c1) / (jnp.sqrt(v / c2) + ADAM_EPS) + ADAM_WD * w)
            refs[5 * n + j][...] = m
            refs[6 * n + j][...] = v

    return body


def _adamw(name, ws, gs, ms, vs):
    n = len(ws)
    specs = [pl.BlockSpec(_tile_rows(w.shape), lambda i: (i, 0)) for w in ws]
    shapes = [jax.ShapeDtypeStruct(w.shape, F32) for w in ws]
    out, _ = _call(_adamw_body(n), name, (EXCH_TILES,), specs * 4, specs * 3, shapes * 3, [], (*ws, *gs, *ms, *vs))
    return out[:n], out[n:2 * n], out[2 * n:]


def _adamw_whole(name, ws, gs, ms, vs):
    n = len(ws)
    out = pl.pallas_call(_adamw_body(n), name=name, out_shape=[jax.ShapeDtypeStruct(w.shape, F32) for w in ws] * 3)(*ws, *gs, *ms, *vs)
    return out[:n], out[n:2 * n], out[2 * n:]


def kernel(x, norm_mix_pre, norm_mix_post, w_in, ret_gn_gain, ssm_lambda_re, ssm_lambda_im, ssm_log_dt, ssm_b_re, ssm_b_im, ssm_c_re, ssm_c_im, ssm_d, w_glu, w_out, norm_mlp_pre, norm_mlp_post, w_ff1, w_ff2, loss_target, m_norm_mix_pre, m_norm_mix_post, m_w_in, m_ret_gn_gain, m_ssm_lambda_re, m_ssm_lambda_im, m_ssm_log_dt, m_ssm_b_re, m_ssm_b_im, m_ssm_c_re, m_ssm_c_im, m_ssm_d, m_w_glu, m_w_out, m_norm_mlp_pre, m_norm_mlp_post, m_w_ff1, m_w_ff2, v_norm_mix_pre, v_norm_mix_post, v_w_in, v_ret_gn_gain, v_ssm_lambda_re, v_ssm_lambda_im, v_ssm_log_dt, v_ssm_b_re, v_ssm_b_im, v_ssm_c_re, v_ssm_c_im, v_ssm_d, v_w_glu, v_w_out, v_norm_mlp_pre, v_norm_mlp_post, v_w_ff1, v_w_ff2):
    given = dict(locals())
    order = ["norm_mix_pre", "norm_mix_post", "w_in", "ret_gn_gain", "ssm_lambda_re", "ssm_lambda_im", "ssm_log_dt",
             "ssm_b_re", "ssm_b_im", "ssm_c_re", "ssm_c_im", "ssm_d", "w_glu", "w_out", "norm_mlp_pre", "norm_mlp_post",
             "w_ff1", "w_ff2"]
    big_names = [name for name, _, _, _ in BIG]
    ids = jnp.stack([lax.axis_index("c"), 2 * lax.axis_index("x") + lax.axis_index("y")]).astype(jnp.int32)

    weights = _cast_place(ids, [given[name][0] for name in big_names])
    w_in = _gather_now("gather_w_in", FIRST, weights[:1])
    small_w = {name: given[name][0] for name, _ in SMALL}
    late_state = [[given[prefix + big_names[w]][0] for w in LATE] for prefix in ("", "m_", "v_")]
    dx, d_in, mid_parts, late_grads, late_stepped, small_grads = _local_step(
        x[0], loss_target[0], small_w, w_in + weights[1:], ids, late_state)

    small_packed = _pack_small(small_grads)
    got = _run_exchange("pair_exchange", _x_join(_x_share(MID, mid_parts), _x_pair(FIRST, [d_in], small_packed)))
    mid_grads, got = got[:len(MID)], got[len(MID):]
    sums = _pair_sum(ids, FIRST, [d_in], got, small_packed)
    arrived = _run_exchange("chip_exchange", _x_chip(FIRST, sums[:-1], sums[-1]))
    parts = _chip_sum(ids, FIRST, sums, arrived)
    g_in, g_small = _run_exchange("pair_share", _x_share(FIRST, parts[:-1], parts[-1]))
    big_grads = [g_in] + mid_grads + late_grads
    rest = lambda prefix: [given[prefix + big_names[w]][0] for w in FIRST + MID]
    rest_out = _adamw("adamw_big", rest(""), big_grads[:len(FIRST + MID)], rest("m_"), rest("v_"))
    big_out = [list(rest_out[k]) + list(late_stepped[k]) for k in range(3)]
    small_names = [name for name, _ in SMALL]
    small_tree = _unpack_small(g_small)
    small_grads = [small_tree[name] for name in small_names]
    small = lambda prefix: [given[prefix + name] for name in small_names]
    small_out = _adamw_whole("adamw_small", small(""), small_grads, small("m_"), small("v_"))

    def in_order(big_arrays, small_arrays):
        tree = {name: a[None] for name, a in zip(big_names, big_arrays)}
        tree.update(zip(small_names, small_arrays))
        return [tree[name] for name in order]

    loss = small_tree["loss_sum"].reshape(())
    return (loss, dx[None], *in_order(big_grads, small_grads), *in_order(big_out[0], small_out[0]),
            *in_order(big_out[1], small_out[1]), *in_order(big_out[2], small_out[2]))
```

```python
import functools
import math

import jax
import jax.numpy as jnp
import numpy as np
from jax import lax
from jax.experimental import pallas as pl
from jax.experimental.pallas import tpu as pltpu

F32 = jnp.float32
BF16 = jnp.bfloat16

D_MODEL = 1024
RET_WIDTH = 512
RET_HEADS = 4
HEAD_DIM = 128
RET_CHUNK = 128
ROPE_BASE = 10000.0
SSM_WIDTH = 512
SSM_GROUP = 16
SSM_GROUPS = 32
SSM_STATE = 64
N_STATES = SSM_GROUPS * SSM_STATE
D_FF = 4096
IN_COLS = 4 * RET_WIDTH + SSM_WIDTH
NORM_EPS = 1e-6
K_SCALE = HEAD_DIM ** -0.5

ADAM_LR = 0.001
ADAM_B1 = 0.9
ADAM_B2 = 0.999
ADAM_EPS = 1e-08
ADAM_WD = 0.01
ADAM_STEP = 10

LANES = 128
SUBLANES = 8
VMEM_LIMIT = 56 * 2 ** 20

TOK_TILE = 512
MLP_TILE = 256
RET_CHUNKS_PER_STEP = 4
S5_TILE = 256
S5_STEPS = S5_TILE // SUBLANES
S5_COLBLK = N_STATES // LANES
S5_GROUP = 4

N_CHIPS = 4
MESH = pl.DeviceIdType.MESH


def _dot(a, b):
    return jnp.dot(a, b, preferred_element_type=F32)


def _dot_nt(a, b):
    return lax.dot_general(a, b, (((1,), (1,)), ((), ())), preferred_element_type=F32)


def _dot_tn(a, b):
    return lax.dot_general(a, b, (((0,), (0,)), ((), ())), preferred_element_type=F32)


def _sigmoid(x):
    return 1.0 / (1.0 + jnp.exp(-x))


_GELU_C = math.sqrt(2.0 / math.pi)


def _gelu(x):
    return 0.5 * x * (1.0 + jnp.tanh(_GELU_C * (x + 0.044715 * (x * x * x))))


def _gelu_grad(x):
    t = jnp.tanh(_GELU_C * (x + 0.044715 * (x * x * x)))
    return 0.5 * (1.0 + t) + 0.5 * x * (1.0 - t * t) * (_GELU_C * (1.0 + 3.0 * 0.044715 * (x * x)))


def _rms(x):
    r = lax.rsqrt(jnp.mean(x * x, axis=-1, keepdims=True) + NORM_EPS)
    return x * r, r


def _rms_bwd(n, r, gain, dy):
    dn = dy * gain
    dx = r * (dn - n * jnp.mean(dn * n, axis=-1, keepdims=True))
    return dx, jnp.sum(dy * n, axis=0, keepdims=True)


def _full(shape):
    nd = len(shape)
    return pl.BlockSpec(shape, lambda i, _n=nd: (0,) * _n, pipeline_mode=pl.Buffered(1))


def _rows(tile, width):
    return pl.BlockSpec((tile, width), lambda i: (i, 0))


def _rows_rev(tile, width, n):
    return pl.BlockSpec((tile, width), lambda i, _n=n: (_n - 1 - i, 0))


def _params(sem):
    return pltpu.CompilerParams(dimension_semantics=(sem,), vmem_limit_bytes=VMEM_LIMIT)


ANY = pl.BlockSpec(memory_space=pl.ANY)


class _Exchange:
    def __init__(self, operands, out_shapes, aliases, n, copies):
        self.operands, self.out_shapes, self.aliases, self.n, self.copies = list(operands), list(out_shapes), aliases, n, copies


def _call(body, name, grid, in_specs, out_specs, out_shape, scratch_shapes, args, exchange=None):
    if exchange is None:
        outs = pl.pallas_call(body, name=name, grid=grid, in_specs=in_specs, out_specs=out_specs, out_shape=out_shape,
                              scratch_shapes=scratch_shapes, compiler_params=_params("arbitrary"))(*args)
        return list(outs), []
    n_in, n_out, n_scr = len(in_specs), len(out_specs), len(scratch_shapes)
    k_in, k_out = len(exchange.operands), len(exchange.out_shapes)
    last = grid[0] - 1

    def hosted(*refs):
        own_in, rest = refs[:n_in], refs[n_in:]
        ex_in, rest = rest[:k_in], rest[k_in:]
        own_out, rest = rest[:n_out], rest[n_out:]
        ex_out, rest = rest[:k_out], rest[k_out:]
        own_scr, (send_sems, recv_sems) = rest[:n_scr], rest[n_scr:]

        @pl.when(pl.program_id(0) == 0)
        def _():
            for cp in exchange.copies(ex_in, ex_out, send_sems, recv_sems):
                cp.start()

        body(*own_in, *own_out, *own_scr)

        @pl.when(pl.program_id(0) == last)
        def _():
            for cp in exchange.copies(ex_in, ex_out, send_sems, recv_sems):
                cp.wait()

    outs = pl.pallas_call(
        hosted, name=name, grid=grid, in_specs=list(in_specs) + [ANY] * k_in, out_specs=list(out_specs) + [ANY] * k_out,
        out_shape=list(out_shape) + exchange.out_shapes,
        scratch_shapes=list(scratch_shapes) + [pltpu.SemaphoreType.DMA((exchange.n,)), pltpu.SemaphoreType.DMA((exchange.n,))],
        input_output_aliases={n_in + i: n_out + o for i, o in exchange.aliases.items()},
        compiler_params=_params("arbitrary"),
    )(*args, *exchange.operands)
    return list(outs[:n_out]), list(outs[n_out:])


def _rope(t, cos2, sin2):
    return t * cos2 + pltpu.roll(t, HEAD_DIM // 2, 1) * sin2


def _rope_bwd(d, cos2, sin2):
    return d * cos2 + pltpu.roll(d * sin2, HEAD_DIM // 2, d.ndim - 1)


def _inproj_fwd(x, gain, w_in, cos2, sin2, exchange=None):
    seq = x.shape[0]

    def body(x_ref, g_ref, w_ref, c_ref, s_ref, q_ref, k_ref, v_ref, gate_ref, u_ref):
        n, _ = _rms(x_ref[...])
        h = (n * g_ref[...]).astype(BF16)
        proj = _dot(h, w_ref[...])
        c = c_ref[...]
        s = s_ref[...]
        for hd in range(RET_HEADS):
            lo = hd * HEAD_DIM
            q_ref[:, lo:lo + HEAD_DIM] = _rope(proj[:, lo:lo + HEAD_DIM], c, s).astype(BF16)
            kh = proj[:, RET_WIDTH + lo:RET_WIDTH + lo + HEAD_DIM]
            k_ref[:, lo:lo + HEAD_DIM] = (_rope(kh, c, s) * K_SCALE).astype(BF16)
        v_ref[...] = proj[:, 2 * RET_WIDTH:3 * RET_WIDTH].astype(BF16)
        gate_ref[...] = proj[:, 3 * RET_WIDTH:4 * RET_WIDTH]
        u_ref[...] = proj[:, 4 * RET_WIDTH:]

    t = TOK_TILE
    half = lambda dt: jax.ShapeDtypeStruct((seq, RET_WIDTH), dt)
    return _call(
        body, "inproj_fwd", (seq // t,),
        [_rows(t, D_MODEL), _full((1, D_MODEL)), _full((D_MODEL, IN_COLS)), _rows(t, HEAD_DIM), _rows(t, HEAD_DIM)],
        [_rows(t, RET_WIDTH)] * 5, [half(BF16), half(BF16), half(BF16), half(F32), half(F32)], [],
        (x, gain, w_in, cos2, sin2), exchange)


def _inproj_bwd(x, gain, w_in, dpieces, dres, adam=()):
    seq = x.shape[0]
    t = TOK_TILE
    steps = seq // t
    n_pieces = len(dpieces)
    n_adam = len(adam[0]) if adam else 0
    adam_step = _adamw_body(n_adam)

    def body(x_ref, g_ref, w_ref, *refs):
        piece_refs, (dres_ref, *adam_in), (dx_ref, dw_ref, dg_ref, *adam_out) = (
            refs[:n_pieces], refs[n_pieces:n_pieces + 1 + 4 * n_adam], refs[n_pieces + 1 + 4 * n_adam:])

        @pl.when(pl.program_id(0) == 0)
        def _():
            dw_ref[...] = jnp.zeros_like(dw_ref)
            dg_ref[...] = jnp.zeros_like(dg_ref)

        adam_step(*adam_in, *adam_out)

        n, r = _rms(x_ref[...])
        gain_v = g_ref[...]
        h = (n * gain_v).astype(BF16)
        dh = None
        for j, piece in enumerate(piece_refs):
            cols = slice(j * RET_WIDTH, (j + 1) * RET_WIDTH)
            dp = piece[...]
            dw_ref[:, cols] += _dot_tn(h, dp)
            term = _dot_nt(dp, w_ref[:, cols])
            dh = term if dh is None else dh + term
        dx, dgs = _rms_bwd(n, r, gain_v, dh)
        dg_ref[...] += dgs
        dx_ref[...] = dres_ref[...] + dx

    adam_arrays = [a for group in adam for a in group]
    adam_spec = lambda a: pl.BlockSpec((a.shape[0] // steps, a.shape[1]), lambda i: (i, 0))
    outs, _ = _call(
        body, "inproj_bwd", (steps,),
        [_rows(t, D_MODEL), _full((1, D_MODEL)), _full((D_MODEL, IN_COLS))] + [_rows(t, RET_WIDTH)] * n_pieces + [_rows(t, D_MODEL)]
        + [adam_spec(a) for a in adam_arrays],
        [_rows(t, D_MODEL), _full((D_MODEL, IN_COLS)), _full((1, D_MODEL))] + [adam_spec(a) for a in adam_arrays[:n_adam]] * 3,
        [jax.ShapeDtypeStruct((seq, D_MODEL), F32), jax.ShapeDtypeStruct((D_MODEL, IN_COLS), F32),
         jax.ShapeDtypeStruct((1, D_MODEL), F32)] + [jax.ShapeDtypeStruct(a.shape, F32) for a in adam_arrays[:n_adam]] * 3,
        [], (x, gain, w_in, *dpieces, dres, *adam_arrays))
    stepped = outs[3:]
    return outs[:3], (stepped[:n_adam], stepped[n_adam:2 * n_adam], stepped[2 * n_adam:])


def _ret_consts():
    c = RET_CHUNK
    f32 = np.float32
    log_gamma = np.log(f32(1.0) - np.exp(np.linspace(math.log(1.0 / 32), math.log(1.0 / 512), RET_HEADS, dtype=f32))).astype(f32)
    idx = np.arange(c, dtype=f32)
    diff = idx[:, None] - idx[None, :]
    decay = np.where(diff[None] >= 0, np.exp(np.maximum(diff, f32(0.0))[None] * log_gamma[:, None, None]), f32(0.0))
    zeta = np.exp((c - 1 - idx)[None, :] * log_gamma[:, None])
    xi = np.exp((idx + f32(1.0))[None, :] * log_gamma[:, None])
    g_chunk = np.exp(f32(c) * log_gamma)
    wide = lambda rowvals: jnp.asarray(np.broadcast_to(rowvals[:, :, None], (RET_HEADS, c, c)).astype(f32))
    return (jnp.asarray(decay.astype(f32)), wide(xi), wide(zeta),
            jnp.asarray(np.broadcast_to(g_chunk[:, None, None], (RET_HEADS, c, c)).astype(f32)))


def _rope_tables(seq):
    f32 = np.float32
    half = HEAD_DIM // 2
    inv_freq = np.power(f32(ROPE_BASE), -np.arange(half, dtype=f32) / f32(half)).astype(f32)
    ang = (np.arange(seq, dtype=f32)[:, None] * inv_freq[None, :]).astype(f32)
    cos, sin = np.cos(ang).astype(f32), np.sin(ang).astype(f32)
    return jnp.asarray(np.concatenate([cos, cos], axis=1)), jnp.asarray(np.concatenate([-sin, sin], axis=1))


def _bmm(a, b):
    return lax.dot_general(a, b, (((2,), (1,)), ((0,), (0,))), preferred_element_type=F32)


def _bmm_nt(a, b):
    return lax.dot_general(a, b, (((2,), (2,)), ((0,), (0,))), preferred_element_type=F32)


def _bmm_tn(a, b):
    return lax.dot_general(a, b, (((1,), (1,)), ((0,), (0,))), preferred_element_type=F32)


def _ret_blocks(ref):
    return jnp.stack([ref[cc * RET_CHUNK:(cc + 1) * RET_CHUNK, hd * HEAD_DIM:(hd + 1) * HEAD_DIM]
                      for cc in range(RET_CHUNKS_PER_STEP) for hd in range(RET_HEADS)])


def _ret_unblock(ref, blocks):
    for cc in range(RET_CHUNKS_PER_STEP):
        for hd in range(RET_HEADS):
            ref[cc * RET_CHUNK:(cc + 1) * RET_CHUNK, hd * HEAD_DIM:(hd + 1) * HEAD_DIM] = blocks[cc * RET_HEADS + hd].astype(ref.dtype)


def _rope_blocks(ref):
    return jnp.stack([ref[cc * RET_CHUNK:(cc + 1) * RET_CHUNK, :] for cc in range(RET_CHUNKS_PER_STEP) for _ in range(RET_HEADS)])


def _per_chunk(ref):
    return jnp.concatenate([ref[...]] * RET_CHUNKS_PER_STEP, axis=0)


def _ret_gain(gain_ref):
    return jnp.stack([gain_ref[:, hd * HEAD_DIM:(hd + 1) * HEAD_DIM]
                      for _ in range(RET_CHUNKS_PER_STEP) for hd in range(RET_HEADS)])


def _ret_fwd(q, k, v, gate, gn_gain, consts, exchange=None):
    seq = q.shape[0]
    cps = RET_CHUNKS_PER_STEP
    t = cps * RET_CHUNK
    n_chunks = seq // RET_CHUNK
    dec, xi, zeta, gch = consts

    def body(q_ref, k_ref, v_ref, gate_ref, gain_ref, dec_ref, xi_ref, zeta_ref, gch_ref, y_ref, rprev_ref, state):
        @pl.when(pl.program_id(0) == 0)
        def _():
            state[...] = jnp.zeros_like(state)

        qb, kb, vb = _ret_blocks(q_ref), _ret_blocks(k_ref), _ret_blocks(v_ref)
        s = _bmm_nt(qb, kb) * _per_chunk(dec_ref)
        vz = (vb.astype(F32) * _per_chunk(zeta_ref)).astype(BF16)
        incr = _bmm_tn(kb, vz)
        gch_v = gch_ref[...]
        rp = state[...]
        before = []
        for cc in range(cps):
            before.append(rp.astype(BF16))
            rprev_ref[cc] = before[-1]
            rp = gch_v * rp + incr[cc * RET_HEADS:(cc + 1) * RET_HEADS]
        state[...] = rp
        qx = (qb.astype(F32) * _per_chunk(xi_ref)).astype(BF16)
        o = _bmm(s.astype(BF16), vb) + _bmm(qx, jnp.concatenate(before, axis=0))
        oc = o - jnp.mean(o, axis=-1, keepdims=True)
        on = oc * lax.rsqrt(jnp.mean(oc * oc, axis=-1, keepdims=True) + NORM_EPS)
        g = _ret_blocks(gate_ref)
        _ret_unblock(y_ref, g * _sigmoid(g) * (on * _ret_gain(gain_ref)))

    cst = _full((RET_HEADS, RET_CHUNK, RET_CHUNK))
    return _call(
        body, "ret_fwd", (seq // t,),
        [_rows(t, RET_WIDTH)] * 4 + [_full((1, RET_WIDTH)), cst, cst, cst, cst],
        [_rows(t, RET_WIDTH), pl.BlockSpec((cps, RET_HEADS, HEAD_DIM, HEAD_DIM), lambda i: (i, 0, 0, 0))],
        [jax.ShapeDtypeStruct((seq, RET_WIDTH), BF16), jax.ShapeDtypeStruct((n_chunks, RET_HEADS, HEAD_DIM, HEAD_DIM), BF16)],
        [pltpu.VMEM((RET_HEADS, HEAD_DIM, HEAD_DIM), F32)],
        (q, k, v, gate, gn_gain, dec, xi, zeta, gch), exchange)


def _ret_bwd(q, k, v, gate, gn_gain, consts, rprev, dy_ret, cos2, sin2, exchange=None):
    seq = q.shape[0]
    cps = RET_CHUNKS_PER_STEP
    t = cps * RET_CHUNK
    nt = seq // t
    dec, xi, zeta, gch = consts

    def body(q_ref, k_ref, v_ref, gate_ref, gain_ref, dec_ref, xi_ref, zeta_ref, gch_ref, rprev_ref, dyr_ref, cos_ref, sin_ref,
             dq_ref, dk_ref, dv_ref, dgate_ref, dgain_ref, dstate):
        @pl.when(pl.program_id(0) == 0)
        def _():
            dstate[...] = jnp.zeros_like(dstate)
            dgain_ref[...] = jnp.zeros_like(dgain_ref)

        nb = cps * RET_HEADS
        qb, kb, vb = _ret_blocks(q_ref), _ret_blocks(k_ref), _ret_blocks(v_ref)
        dec_b, xi_b, zeta_b = _per_chunk(dec_ref), _per_chunk(xi_ref), _per_chunk(zeta_ref)
        rpb = rprev_ref[...].reshape(nb, HEAD_DIM, HEAD_DIM)
        sdb = (_bmm_nt(qb, kb) * dec_b).astype(BF16)
        qx = (qb.astype(F32) * xi_b).astype(BF16)
        o = _bmm(sdb, vb) + _bmm(qx, rpb)
        oc = o - jnp.mean(o, axis=-1, keepdims=True)
        rstd = lax.rsqrt(jnp.mean(oc * oc, axis=-1, keepdims=True) + NORM_EPS)
        on = oc * rstd
        g = _ret_blocks(gate_ref)
        sg = _sigmoid(g)
        gain_b = _ret_gain(gain_ref)
        dyr = _ret_blocks(dyr_ref)
        _ret_unblock(dgate_ref, dyr * (on * gain_b) * (sg * (1.0 + g * (1.0 - sg))))
        dy = dyr * (g * sg)
        dgain = jnp.sum(dy * on, axis=1, keepdims=True)
        for hd in range(RET_HEADS):
            part = dgain[hd]
            for cc in range(1, cps):
                part = part + dgain[cc * RET_HEADS + hd]
            dgain_ref[:, hd * HEAD_DIM:(hd + 1) * HEAD_DIM] += part
        don = dy * gain_b
        do = rstd * (don - jnp.mean(don, axis=-1, keepdims=True) - on * jnp.mean(don * on, axis=-1, keepdims=True))
        dob = do.astype(BF16)
        ds = (_bmm_nt(dob, vb) * dec_b).astype(BF16)
        dq = _bmm(ds, kb) + _bmm_nt(dob, rpb) * xi_b
        dk = _bmm_tn(ds, qb)
        dv = _bmm_tn(sdb, dob)
        dstate_local = _bmm_tn(qx, dob)
        vz = (vb.astype(F32) * zeta_b).astype(BF16)
        gch_v = gch_ref[...]
        zeta_v = zeta_ref[...]
        gh = dstate[...]
        dk_extra, dv_extra = [None] * cps, [None] * cps
        for cc in reversed(range(cps)):
            sl = slice(cc * RET_HEADS, (cc + 1) * RET_HEADS)
            gb = gh.astype(BF16)
            dk_extra[cc] = _bmm_nt(vz[sl], gb)
            dv_extra[cc] = _bmm(kb[sl], gb) * zeta_v
            gh = dstate_local[sl] + gch_v * gh
        dstate[...] = gh
        cos_b, sin_b = _rope_blocks(cos_ref), _rope_blocks(sin_ref)
        _ret_unblock(dq_ref, _rope_bwd(dq, cos_b, sin_b))
        _ret_unblock(dk_ref, _rope_bwd((dk + jnp.concatenate(dk_extra, axis=0)) * K_SCALE, cos_b, sin_b))
        _ret_unblock(dv_ref, dv + jnp.concatenate(dv_extra, axis=0))

    cst = _full((RET_HEADS, RET_CHUNK, RET_CHUNK))
    rev = _rows_rev(t, RET_WIDTH, nt)
    act = jax.ShapeDtypeStruct((seq, RET_WIDTH), BF16)
    return _call(
        body, "ret_bwd", (nt,),
        [rev] * 4 + [_full((1, RET_WIDTH)), cst, cst, cst, cst,
                     pl.BlockSpec((cps, RET_HEADS, HEAD_DIM, HEAD_DIM), lambda i: (nt - 1 - i, 0, 0, 0)), rev,
                     _rows_rev(t, HEAD_DIM, nt), _rows_rev(t, HEAD_DIM, nt)],
        [rev] * 4 + [_full((1, RET_WIDTH))], [act, act, act, act, jax.ShapeDtypeStruct((1, RET_WIDTH), F32)],
        [pltpu.VMEM((RET_HEADS, HEAD_DIM, HEAD_DIM), F32)],
        (q, k, v, gate, gn_gain, dec, xi, zeta, gch, rprev, dy_ret, cos2, sin2), exchange)


MXU_TILE = 256
S5_STATE_TILES = 2 * N_STATES // MXU_TILE
S5_CHANNELS_PER_TILE = MXU_TILE // SSM_STATE * SSM_GROUP
S5_TILES_SHAPE = (S5_STATE_TILES, LANES, MXU_TILE)


def _chan_block(kk):
    return ((kk % (S5_STATE_TILES // 2)) * S5_CHANNELS_PER_TILE) // LANES * LANES


def _s5_tiles(blocks_re, blocks_im):
    per = MXU_TILE // SSM_STATE
    half = S5_STATE_TILES // 2
    eye = jnp.eye(per, dtype=F32)
    in_half = (jnp.arange(half) % (LANES // S5_CHANNELS_PER_TILE))[:, None] == jnp.arange(LANES // S5_CHANNELS_PER_TILE)[None, :]
    out = []
    for blk in (blocks_re, blocks_im):
        t = blk.reshape(half, per, SSM_GROUP, 1, SSM_STATE) * eye[None, :, None, :, None]
        t = t.reshape(half, 1, S5_CHANNELS_PER_TILE, MXU_TILE)
        out.append(jnp.where(in_half[:, :, None, None], t, 0.0).reshape(half, LANES, MXU_TILE))
    return jnp.concatenate(out, axis=0)


def _s5_discretise(lam_re, lam_im, log_dt, b_re, b_im, c_re, c_im):
    lam = lax.complex(jnp.minimum(lam_re, -1e-4), lam_im)
    dt = jnp.exp(log_dt)[:, None]
    lam_bar = jnp.exp(lam * dt)
    b_bar = ((lam_bar - 1.0) / lam)[:, :, None] * lax.complex(b_re, b_im)
    b_tiles = _s5_tiles(jnp.swapaxes(jnp.real(b_bar), 1, 2), jnp.swapaxes(jnp.imag(b_bar), 1, 2))
    return jnp.real(lam_bar), jnp.imag(lam_bar), b_tiles, _s5_tiles(c_re, -c_im)


def _s5_tables(lbr, lbi):
    lr = lbr.reshape(1, N_STATES)
    li = lbi.reshape(1, N_STATES)
    pr, pi = lr, li
    while pr.shape[0] < S5_STEPS:
        top_r, top_i = pr[-1:], pi[-1:]
        pr, pi = (jnp.concatenate([pr, pr * top_r - pi * top_i], axis=0), jnp.concatenate([pi, pr * top_i + pi * top_r], axis=0))
    bc = lambda a: jnp.broadcast_to(a, (SUBLANES, N_STATES))
    rep = lambda rows: jnp.broadcast_to(rows[:, None, :], (S5_STEPS, SUBLANES, N_STATES)).reshape(S5_STEPS * SUBLANES, N_STATES)
    return bc(lr), bc(li), rep(pr), rep(pi)


def _cmul(ar, ai, br, bi):
    return ar * br - ai * bi, ar * bi + ai * br


def _seg(i):
    return pl.ds(pl.multiple_of(i * SUBLANES, SUBLANES), SUBLANES)


def _scan_order():
    r = np.arange(S5_TILE)
    token = (r % SUBLANES) * S5_STEPS + r // SUBLANES
    p = (token[:, None] == np.arange(S5_TILE)[None, :]).astype(np.float32)
    return jnp.asarray(p, dtype=BF16), jnp.asarray(p.T, dtype=BF16)


def _to_scan_order(p_ref, rows_bf16):
    return _dot(p_ref[...], rows_bf16).astype(BF16)


def _from_scan_order(pt_ref, rows, pieces=3):
    pt = pt_ref[...]
    out = None
    for _ in range(pieces):
        piece = rows.astype(BF16)
        rows = rows - piece.astype(F32)
        term = _dot(pt, piece)
        out = term if out is None else out + term
    return out


def _s5_channels_to_states(chan_b, m_ref, xs):
    for kk in range(S5_STATE_TILES):
        lo = _chan_block(kk)
        res = _dot(chan_b[:, lo:lo + LANES], m_ref[kk])
        xs[2 * kk] = res[:, :LANES]
        xs[2 * kk + 1] = res[:, LANES:]


def _s5_states_to_channels(sb_ref, m_ref):
    out = []
    for lo in range(0, SSM_WIDTH, LANES):
        acc = None
        for kk in range(S5_STATE_TILES):
            if _chan_block(kk) == lo:
                term = _dot_nt(sb_ref[:, kk * MXU_TILE:(kk + 1) * MXU_TILE], m_ref[kk])
                acc = term if acc is None else acc + term
        out.append(acc)
    return jnp.concatenate(out, axis=1)


def _s5_outer_acc(acc_ref, chan_b, states_ref):
    for kk in range(S5_STATE_TILES):
        lo = _chan_block(kk)
        acc_ref[kk] += _dot_tn(chan_b[:, lo:lo + LANES], states_ref[:, kk * MXU_TILE:(kk + 1) * MXU_TILE])


def _s5_scan_tile(xs, lr_ref, li_ref, pr_ref, pi_ref, carry_re, carry_im, conj, reverse, prev_of=None):
    sgn = -1.0 if conj else 1.0
    seg_in_re, seg_in_im, sums_re, sums_im = [], [], [], []
    group = S5_GROUP
    for grp in range(S5_COLBLK // group):
        blks = [grp * group + j for j in range(group)]
        lrs = [lr_ref[:, b * LANES:(b + 1) * LANES] for b in blks]
        lis = [sgn * li_ref[:, b * LANES:(b + 1) * LANES] for b in blks]

        def step(it, carry, blks=blks, lrs=lrs, lis=lis):
            i = (S5_STEPS - 1 - it) if reverse else it
            out_r, out_i = [], []
            for j, b in enumerate(blks):
                xr, xi = _cmul(lrs[j], lis[j], carry[j], carry[group + j])
                xr = xr + xs[b, _seg(i), :]
                xi = xi + xs[S5_COLBLK + b, _seg(i), :]
                xs[b, _seg(i), :] = xr
                xs[S5_COLBLK + b, _seg(i), :] = xi
                out_r.append(xr)
                out_i.append(xi)
            return tuple(out_r + out_i)

        zeros = tuple(jnp.zeros((SUBLANES, LANES), F32) for _ in range(2 * group))
        ends = lax.fori_loop(0, S5_STEPS, step, zeros, unroll=True)

        ins_r, ins_i = [], []
        for j, b in enumerate(blks):
            cols = slice(b * LANES, (b + 1) * LANES)
            pr = pr_ref[SUBLANES * S5_STEPS - 1:SUBLANES * S5_STEPS, cols]
            pi = sgn * pi_ref[SUBLANES * S5_STEPS - 1:SUBLANES * S5_STEPS, cols]
            cur_r = carry_re[:, cols]
            cur_i = carry_im[:, cols]
            row_id = lax.broadcasted_iota(jnp.int32, (SUBLANES, LANES), 0)
            in_r = jnp.zeros((SUBLANES, LANES), F32)
            in_i = jnp.zeros((SUBLANES, LANES), F32)
            order = range(SUBLANES - 1, -1, -1) if reverse else range(SUBLANES)
            for sgm in order:
                in_r = jnp.where(row_id == sgm, cur_r, in_r)
                in_i = jnp.where(row_id == sgm, cur_i, in_i)
                mr, mi = _cmul(pr, pi, cur_r, cur_i)
                cur_r = mr + ends[j][sgm:sgm + 1, :]
                cur_i = mi + ends[group + j][sgm:sgm + 1, :]
            carry_re[:, cols] = cur_r
            carry_im[:, cols] = cur_i
            ins_r.append(in_r)
            ins_i.append(in_i)

        def fix(it, c, blks=blks, ins_r=ins_r, ins_i=ins_i):
            pw = (S5_STEPS - 1 - it) if reverse else it
            out = []
            for j, b in enumerate(blks):
                cols = slice(b * LANES, (b + 1) * LANES)
                prow = pl.ds(pl.multiple_of(pw * SUBLANES, SUBLANES), SUBLANES)
                fr, fi = _cmul(pr_ref[prow, cols], sgn * pi_ref[prow, cols], ins_r[j], ins_i[j])
                vr = xs[b, _seg(it), :] + fr
                vi = xs[S5_COLBLK + b, _seg(it), :] + fi
                xs[b, _seg(it), :] = vr
                xs[S5_COLBLK + b, _seg(it), :] = vi
                if prev_of is not None:
                    opr, opi, sr, si = c[4 * j:4 * j + 4]
                    out += [prev_of[0][b, _seg(it), :], prev_of[0][S5_COLBLK + b, _seg(it), :],
                            sr + vr * opr + vi * opi, si + vi * opr - vr * opi]
            return tuple(out)

        init = ()
        if prev_of is not None:
            zero = jnp.zeros((SUBLANES, LANES), F32)
            for b in blks:
                init += (prev_of[1][b], prev_of[2][b], zero, zero)
        done = lax.fori_loop(0, S5_STEPS, fix, init, unroll=True)
        for j in range(len(blks) if prev_of is not None else 0):
            sums_re.append(done[4 * j + 2])
            sums_im.append(done[4 * j + 3])
        seg_in_re += ins_r
        seg_in_im += ins_i
    return seg_in_re, seg_in_im, sums_re, sums_im


def _s5_pack(xs, dst):
    for b in range(2 * S5_COLBLK):
        dst[:, b * LANES:(b + 1) * LANES] = xs[b].astype(BF16)


def _s5_fwd(u, b_blk, c_blk, tables, d_skip, exchange=None):
    seq = u.shape[0]
    t = S5_TILE
    nt = seq // t
    lr8, li8, p_re, p_im = tables

    p, pt = _scan_order()

    def body(u_ref, p_ref, pt_ref, b_ref, c_ref, lr_ref, li_ref, pr_ref, pi_ref, d_ref, y_ref, start_ref,
             xs, xb, carry_re, carry_im):
        @pl.when(pl.program_id(0) == 0)
        def _():
            carry_re[...] = jnp.zeros_like(carry_re)
            carry_im[...] = jnp.zeros_like(carry_im)

        start_ref[0, 0:1, :] = carry_re[...]
        start_ref[0, 1:2, :] = carry_im[...]
        uv = u_ref[...]
        _s5_channels_to_states(_to_scan_order(p_ref, uv.astype(BF16)), b_ref, xs)
        _s5_scan_tile(xs, lr_ref, li_ref, pr_ref, pi_ref, carry_re, carry_im, conj=False, reverse=False)
        _s5_pack(xs, xb)
        y_ref[...] = _from_scan_order(pt_ref, _s5_states_to_channels(xb, c_ref), pieces=2) + d_ref[...] * uv

    return _call(
        body, "s5_fwd", (nt,),
        [_rows(t, SSM_WIDTH), _full((t, t)), _full((t, t)), _full(S5_TILES_SHAPE), _full(S5_TILES_SHAPE),
         _full((SUBLANES, N_STATES)), _full((SUBLANES, N_STATES)), _full((SUBLANES * S5_STEPS, N_STATES)),
         _full((SUBLANES * S5_STEPS, N_STATES)), _full((1, SSM_WIDTH))],
        [_rows(t, SSM_WIDTH), pl.BlockSpec((1, 2, N_STATES), lambda i: (i, 0, 0))],
        [jax.ShapeDtypeStruct((seq, SSM_WIDTH), F32), jax.ShapeDtypeStruct((nt, 2, N_STATES), F32)],
        [pltpu.VMEM((2 * S5_COLBLK, t, LANES), F32), pltpu.VMEM((t, 2 * N_STATES), BF16),
         pltpu.VMEM((1, N_STATES), F32), pltpu.VMEM((1, N_STATES), F32)],
        (u, p, pt, b_blk, c_blk, lr8, li8, p_re, p_im, d_skip), exchange)


def _s5_bwd(u, dy, b_blk, c_blk, tables, d_skip, starts, exchange=None):
    seq = u.shape[0]
    t = S5_TILE
    nt = seq // t
    lr8, li8, p_re, p_im = tables
    p, pt = _scan_order()

    def body(u_ref, dy_ref, p_ref, pt_ref, b_ref, c_ref, lr_ref, li_ref, pr_ref, pi_ref, d_ref, start_ref,
             du_ref, dd_ref, dlam_ref, db_ref, dc_ref,
             xs, as_, xb, ab, carry_re, carry_im, acar_re, acar_im):
        @pl.when(pl.program_id(0) == 0)
        def _():
            acar_re[...] = jnp.zeros_like(acar_re)
            acar_im[...] = jnp.zeros_like(acar_im)
            dd_ref[...] = jnp.zeros_like(dd_ref)
            dlam_ref[...] = jnp.zeros_like(dlam_ref)
            db_ref[...] = jnp.zeros_like(db_ref)
            dc_ref[...] = jnp.zeros_like(dc_ref)

        uv = u_ref[...]
        dyv = dy_ref[...]
        ub = _to_scan_order(p_ref, uv.astype(BF16))
        dyb = _to_scan_order(p_ref, dyv.astype(BF16))
        carry_re[...] = start_ref[0, 0:1, :]
        carry_im[...] = start_ref[0, 1:2, :]
        _s5_channels_to_states(ub, b_ref, xs)
        in_re, in_im, _, _ = _s5_scan_tile(xs, lr_ref, li_ref, pr_ref, pi_ref, carry_re, carry_im, conj=False, reverse=False)
        _s5_channels_to_states(dyb, c_ref, as_)
        _, _, dl_re, dl_im = _s5_scan_tile(as_, lr_ref, li_ref, pr_ref, pi_ref, acar_re, acar_im, conj=True, reverse=True,
                                           prev_of=(xs, in_re, in_im))
        for b in range(S5_COLBLK):
            cols = slice(b * LANES, (b + 1) * LANES)
            dlam_ref[0, :, cols] += dl_re[b]
            dlam_ref[1, :, cols] += dl_im[b]
        _s5_pack(xs, xb)
        _s5_pack(as_, ab)
        du_ref[...] = (_from_scan_order(pt_ref, _s5_states_to_channels(ab, b_ref), pieces=2) + d_ref[...] * dyv).astype(BF16)
        dd_ref[...] += jnp.sum(dyv * uv, axis=0, keepdims=True)
        _s5_outer_acc(db_ref, ub, ab)
        _s5_outer_acc(dc_ref, dyb, xb)

    rev = _rows_rev(t, SSM_WIDTH, nt)
    vec = lambda: pltpu.VMEM((1, N_STATES), F32)
    outer = S5_TILES_SHAPE
    return _call(
        body, "s5_bwd", (nt,),
        [rev, rev, _full((t, t)), _full((t, t)), _full(S5_TILES_SHAPE), _full(S5_TILES_SHAPE),
         _full((SUBLANES, N_STATES)), _full((SUBLANES, N_STATES)), _full((SUBLANES * S5_STEPS, N_STATES)),
         _full((SUBLANES * S5_STEPS, N_STATES)), _full((1, SSM_WIDTH)),
         pl.BlockSpec((1, 2, N_STATES), lambda i: (nt - 1 - i, 0, 0))],
        [rev, _full((1, SSM_WIDTH)), _full((2, SUBLANES, N_STATES)), _full(outer), _full(outer)],
        [jax.ShapeDtypeStruct((seq, SSM_WIDTH), BF16), jax.ShapeDtypeStruct((1, SSM_WIDTH), F32),
         jax.ShapeDtypeStruct((2, SUBLANES, N_STATES), F32), jax.ShapeDtypeStruct(outer, F32), jax.ShapeDtypeStruct(outer, F32)],
        [pltpu.VMEM((2 * S5_COLBLK, t, LANES), F32), pltpu.VMEM((2 * S5_COLBLK, t, LANES), F32),
         pltpu.VMEM((t, 2 * N_STATES), BF16), pltpu.VMEM((t, 2 * N_STATES), BF16), vec(), vec(), vec(), vec()],
        (u, dy, p, pt, b_blk, c_blk, lr8, li8, p_re, p_im, d_skip, starts), exchange)


def _mix_fwd(y_ssm, y_ret, x, gain, w_glu, w_out, exchange=None):
    seq = x.shape[0]
    t = TOK_TILE

    def body(ys_ref, yr_ref, x_ref, g_ref, wg_ref, wo_ref, x1_ref, mix_ref):
        g0 = _gelu(ys_ref[...]).astype(BF16)
        z = _dot(g0, wg_ref[...])
        glu = (z[:, :SSM_WIDTH] * _sigmoid(z[:, SSM_WIDTH:])).astype(BF16)
        mix = _dot(yr_ref[...], wo_ref[:RET_WIDTH, :]) + _dot(glu, wo_ref[RET_WIDTH:, :])
        mix_ref[...] = mix.astype(BF16)
        n, _ = _rms(mix)
        x1_ref[...] = x_ref[...] + n * g_ref[...]

    act = jax.ShapeDtypeStruct((seq, D_MODEL), F32)
    return _call(
        body, "mix_fwd", (seq // t,),
        [_rows(t, SSM_WIDTH), _rows(t, RET_WIDTH), _rows(t, D_MODEL), _full((1, D_MODEL)),
         _full((SSM_WIDTH, 2 * SSM_WIDTH)), _full((D_MODEL, D_MODEL))],
        [_rows(t, D_MODEL)] * 2, [act, jax.ShapeDtypeStruct((seq, D_MODEL), BF16)], [],
        (y_ssm, y_ret, x, gain, w_glu, w_out), exchange)


def _mix_bwd(dx1, mix, gain, y_ssm, y_ret, w_glu, w_out, exchange=None):
    seq = dx1.shape[0]
    t = TOK_TILE

    def body(dx1_ref, mix_ref, g_ref, ys_ref, yr_ref, wg_ref, wo_ref, dyr_ref, dys_ref, dwo_ref, dwg_ref, dg_ref):
        @pl.when(pl.program_id(0) == 0)
        def _():
            dwo_ref[...] = jnp.zeros_like(dwo_ref)
            dwg_ref[...] = jnp.zeros_like(dwg_ref)
            dg_ref[...] = jnp.zeros_like(dg_ref)

        n, r = _rms(mix_ref[...].astype(F32))
        dmix, dgs = _rms_bwd(n, r, g_ref[...], dx1_ref[...])
        dg_ref[...] += dgs
        dmb = dmix.astype(BF16)
        dcat = _dot_nt(dmb, wo_ref[...])
        dyr_ref[...] = dcat[:, :RET_WIDTH]
        dglu = dcat[:, RET_WIDTH:]
        ys = ys_ref[...]
        g0 = _gelu(ys).astype(BF16)
        z = _dot(g0, wg_ref[...])
        a = z[:, :SSM_WIDTH]
        sb = _sigmoid(z[:, SSM_WIDTH:])
        dwo_ref[:RET_WIDTH, :] += _dot_tn(yr_ref[...], dmb)
        dwo_ref[RET_WIDTH:, :] += _dot_tn((a * sb).astype(BF16), dmb)
        dz = jnp.concatenate([dglu * sb, dglu * a * sb * (1.0 - sb)], axis=1).astype(BF16)
        dwg_ref[...] += _dot_tn(g0, dz)
        dys_ref[...] = _dot_nt(dz, wg_ref[...]) * _gelu_grad(ys)

    half = jax.ShapeDtypeStruct((seq, RET_WIDTH), F32)
    return _call(
        body, "mix_bwd", (seq // t,),
        [_rows(t, D_MODEL), _rows(t, D_MODEL), _full((1, D_MODEL)), _rows(t, SSM_WIDTH), _rows(t, RET_WIDTH),
         _full((SSM_WIDTH, 2 * SSM_WIDTH)), _full((D_MODEL, D_MODEL))],
        [_rows(t, RET_WIDTH), _rows(t, SSM_WIDTH), _full((D_MODEL, D_MODEL)), _full((SSM_WIDTH, 2 * SSM_WIDTH)),
         _full((1, D_MODEL))],
        [half, half, jax.ShapeDtypeStruct((D_MODEL, D_MODEL), F32), jax.ShapeDtypeStruct((SSM_WIDTH, 2 * SSM_WIDTH), F32),
         jax.ShapeDtypeStruct((1, D_MODEL), F32)],
        [], (dx1, mix, gain, y_ssm, y_ret, w_glu, w_out), exchange)


def _mlp_a(x1, target, gain_pre, gain_post, w1, w2):
    seq = x1.shape[0]
    t = MLP_TILE

    def body(x1_ref, tg_ref, gp_ref, gq_ref, w1_ref, w2_ref, df_ref, dx2_ref, dw2_ref, dgq_ref, sq_ref):
        @pl.when(pl.program_id(0) == 0)
        def _():
            dw2_ref[...] = jnp.zeros_like(dw2_ref)
            dgq_ref[...] = jnp.zeros_like(dgq_ref)
            sq_ref[...] = jnp.zeros_like(sq_ref)

        x1v = x1_ref[...]
        n3, _ = _rms(x1v)
        h = (n3 * gp_ref[...]).astype(BF16)
        rl = jnp.maximum(_dot(h, w1_ref[...]), 0.0)
        act = (rl * rl).astype(BF16)
        n4, r4 = _rms(_dot(act, w2_ref[...]))
        gq = gq_ref[...]
        err = x1v + n4 * gq - tg_ref[...]
        sq_ref[...] += jnp.sum(err * err, axis=0, keepdims=True)
        dx2 = err * (1.0 / D_MODEL)
        dx2_ref[...] = dx2
        dm, dgs = _rms_bwd(n4, r4, gq, dx2)
        dgq_ref[...] += dgs
        dmb = dm.astype(BF16)
        dw2_ref[...] += _dot_tn(act, dmb)
        df_ref[...] = (_dot_nt(dmb, w2_ref[...]) * (2.0 * rl)).astype(BF16)

    return pl.pallas_call(
        body, name="mlp_a", grid=(seq // t,),
        in_specs=[_rows(t, D_MODEL), _rows(t, D_MODEL), _full((1, D_MODEL)), _full((1, D_MODEL)),
                  _full((D_MODEL, D_FF)), _full((D_FF, D_MODEL))],
        out_specs=[_rows(t, D_FF), _rows(t, D_MODEL), _full((D_FF, D_MODEL)), _full((1, D_MODEL)), _full((1, D_MODEL))],
        out_shape=[jax.ShapeDtypeStruct((seq, D_FF), BF16), jax.ShapeDtypeStruct((seq, D_MODEL), F32),
                   jax.ShapeDtypeStruct((D_FF, D_MODEL), F32), jax.ShapeDtypeStruct((1, D_MODEL), F32),
                   jax.ShapeDtypeStruct((1, D_MODEL), F32)],
        compiler_params=_params("arbitrary"),
    )(x1, target, gain_pre, gain_post, w1, w2)


def _mlp_b(df, dx2, x1, gain_pre, w1):
    seq = x1.shape[0]
    t = TOK_TILE

    def body(df_ref, dx2_ref, x1_ref, gp_ref, w1_ref, dx1_ref, dw1_ref, dgp_ref):
        @pl.when(pl.program_id(0) == 0)
        def _():
            dw1_ref[...] = jnp.zeros_like(dw1_ref)
            dgp_ref[...] = jnp.zeros_like(dgp_ref)

        n3, r3 = _rms(x1_ref[...])
        gp = gp_ref[...]
        h = (n3 * gp).astype(BF16)
        dfv = df_ref[...]
        dw1_ref[...] += _dot_tn(h, dfv)
        dx, dgs = _rms_bwd(n3, r3, gp, _dot_nt(dfv, w1_ref[...]))
        dgp_ref[...] += dgs
        dx1_ref[...] = dx2_ref[...] + dx

    return pl.pallas_call(
        body, name="mlp_b", grid=(seq // t,),
        in_specs=[_rows(t, D_FF), _rows(t, D_MODEL), _rows(t, D_MODEL), _full((1, D_MODEL)), _full((D_MODEL, D_FF))],
        out_specs=[_rows(t, D_MODEL), _full((D_MODEL, D_FF)), _full((1, D_MODEL))],
        out_shape=[jax.ShapeDtypeStruct((seq, D_MODEL), F32), jax.ShapeDtypeStruct((D_MODEL, D_FF), F32),
                   jax.ShapeDtypeStruct((1, D_MODEL), F32)],
        compiler_params=_params("arbitrary"),
    )(df, dx2, x1, gain_pre, w1)


def _local_step(x, target, small, weights, ids=None, late_state=None):
    cos2, sin2 = _rope_tables(x.shape[0])
    ret_consts = _ret_consts()

    s5_names = ("ssm_lambda_re", "ssm_lambda_im", "ssm_log_dt", "ssm_b_re", "ssm_b_im", "ssm_c_re", "ssm_c_im")
    (lbr, lbi, b_tiles, c_tiles), disc_vjp = jax.vjp(_s5_discretise, *[small[name] for name in s5_names])
    tables = _s5_tables(lbr, lbi)
    b_blk = b_tiles.astype(BF16)
    c_blk = c_tiles.astype(BF16)
    d_skip = small["ssm_d"].reshape(1, SSM_WIDTH)
    gn_gain = small["ret_gn_gain"].reshape(1, RET_WIDTH)
    g_mix_pre = small["norm_mix_pre"].reshape(1, D_MODEL)
    g_mix_post = small["norm_mix_post"].reshape(1, D_MODEL)
    g_mlp_pre = small["norm_mlp_pre"].reshape(1, D_MODEL)
    g_mlp_post = small["norm_mlp_post"].reshape(1, D_MODEL)

    dist = ids is not None
    w_in, w_glu, w_out, w_ff1, w_ff2 = weights
    (q, k, v, gate, u), got = _inproj_fwd(x, g_mix_pre, w_in, cos2, sin2, _x_gather_send((1, 2), [w_glu, w_out]) if dist else None)
    if dist:
        w_glu, w_out = got
    (y_ssm, starts), got = _s5_fwd(u, b_blk, c_blk, tables, d_skip, _x_gather_send(LATE, [w_ff1, w_ff2]) if dist else None)
    if dist:
        w_ff1, w_ff2 = got
    (y_ret, rprev), got = _ret_fwd(q, k, v, gate, gn_gain, ret_consts,
                                   _x_gather_pass(MID + LATE, [w_glu, w_out, w_ff1, w_ff2]) if dist else None)
    if dist:
        w_glu, w_out, w_ff1, w_ff2 = got
    (x1, mix), _ = _mix_fwd(y_ssm, y_ret, x, g_mix_post, w_glu, w_out)
    df, dx2, d_ff2, dg_mlp_post, sq = _mlp_a(x1, target, g_mlp_pre, g_mlp_post, w_ff1, w_ff2)
    dx1, d_ff1, dg_mlp_pre = _mlp_b(df, dx2, x1, g_mlp_pre, w_ff1)
    (dy_ret, dy_ssm, d_out, d_glu, dg_mix_post), got = _mix_bwd(dx1, mix, g_mix_post, y_ssm, y_ret, w_glu, w_out,
                                                               _x_pair(LATE, [d_ff1, d_ff2]) if dist else None)
    if dist:
        sums = _pair_sum(ids, LATE, [d_ff1, d_ff2], got)
    (du, dd, dlam8, db_tiles, dc_tiles), got = _s5_bwd(
        u, dy_ssm, b_blk, c_blk, tables, d_skip, starts,
        _x_join(_x_chip(LATE, sums), _x_pair(MID, [d_glu, d_out])) if dist else None)
    if dist:
        parts = _chip_sum(ids, LATE, sums, got[:len(LATE)])
        sums = _pair_sum(ids, MID, [d_glu, d_out], got[len(LATE):])
    (dq, dk, dv, dgate, dgn), got = _ret_bwd(q, k, v, gate, gn_gain, ret_consts, rprev, dy_ret, cos2, sin2,
                                             _x_join(_x_share(LATE, parts), _x_chip(MID, sums)) if dist else None)
    if dist:
        late_grads = got[:len(LATE)]
        parts = _chip_sum(ids, MID, sums, got[len(LATE):])
        mid_parts = parts
    adam = (late_state[0], late_grads, late_state[1], late_state[2]) if dist else ()
    (dx, d_in, dg_mix_pre), late_stepped = _inproj_bwd(x, g_mix_pre, w_in, [dq, dk, dv, dgate, du], dx1, adam)

    dlam = jnp.sum(dlam8, axis=1)
    s5_grads = disc_vjp((dlam[0].reshape(SSM_GROUPS, SSM_STATE), dlam[1].reshape(SSM_GROUPS, SSM_STATE), db_tiles, dc_tiles))
    small_grads = {
        "norm_mix_pre": dg_mix_pre, "norm_mix_post": dg_mix_post, "ret_gn_gain": dgn, "ssm_d": dd,
        "norm_mlp_pre": dg_mlp_pre, "norm_mlp_post": dg_mlp_post, "loss_sum": 0.5 * jnp.sum(sq) / D_MODEL,
    }
    small_grads.update(dict(zip(s5_names, s5_grads)))
    if dist:
        return dx, d_in, mid_parts, late_grads, late_stepped, small_grads
    return dx, (d_in, d_glu, d_out, d_ff1, d_ff2), small_grads


BIG = (
    ("w_in", D_MODEL, IN_COLS, 1),
    ("w_glu", SSM_WIDTH, 2 * SSM_WIDTH, 1),
    ("w_out", D_MODEL, D_MODEL, 0),
    ("w_ff1", D_MODEL, D_FF, 1),
    ("w_ff2", D_FF, D_MODEL, 0),
)
SMALL = (
    ("norm_mix_pre", (D_MODEL,)), ("norm_mix_post", (D_MODEL,)), ("ret_gn_gain", (RET_WIDTH,)),
    ("ssm_lambda_re", (SSM_GROUPS, SSM_STATE)), ("ssm_lambda_im", (SSM_GROUPS, SSM_STATE)), ("ssm_log_dt", (SSM_GROUPS,)),
    ("ssm_b_re", (SSM_GROUPS, SSM_STATE, SSM_GROUP)), ("ssm_b_im", (SSM_GROUPS, SSM_STATE, SSM_GROUP)),
    ("ssm_c_re", (SSM_GROUPS, SSM_GROUP, SSM_STATE)), ("ssm_c_im", (SSM_GROUPS, SSM_GROUP, SSM_STATE)),
    ("ssm_d", (SSM_WIDTH,)), ("norm_mlp_pre", (D_MODEL,)), ("norm_mlp_post", (D_MODEL,)),
)
SMALL_ROWS = 1152
EXCH_TILES = 8


PACKED = SMALL + (("loss_sum", (1,)),)


def _packed_rows(shape):
    return -(-math.prod(shape) // (SUBLANES * LANES)) * SUBLANES


def _pack_small(tree):
    rows = []
    for name, shape in PACKED:
        size = math.prod(shape)
        flat = tree[name].reshape(-1).astype(F32) if name in tree else jnp.zeros((size,), F32)
        rows.append(jnp.pad(flat, (0, _packed_rows(shape) * LANES - size)).reshape(-1, LANES))
    used = sum(r.shape[0] for r in rows)
    rows.append(jnp.zeros((SMALL_ROWS - used, LANES), F32))
    return jnp.concatenate(rows, axis=0)


def _unpack_small(packed):
    out, row = {}, 0
    for name, shape in PACKED:
        size = math.prod(shape)
        out[name] = packed[row:row + _packed_rows(shape)].reshape(-1)[:size].reshape((1,) + shape)
        row += _packed_rows(shape)
    return out


def _half_shape(r, c, axis):
    return (r // 2, c) if axis == 1 else (r, c // 2)


def _region_shape(r, c, axis):
    return (r // 2, c // N_CHIPS) if axis == 1 else (r // N_CHIPS, c // 2)


def _shard_shape(r, c, axis):
    return (r, c // N_CHIPS) if axis == 1 else (r // N_CHIPS, c)


def _ds(start, size):
    return pl.ds(pl.multiple_of(start * size, size), size)


def _region_of_full(ref, r, c, axis, shard, half):
    if axis == 1:
        return ref.at[_ds(half, r // 2), _ds(shard, c // N_CHIPS)]
    return ref.at[_ds(shard, r // N_CHIPS), _ds(half, c // 2)]


def _half_of_full(ref, r, c, axis, half):
    if axis == 1:
        return ref.at[_ds(half, r // 2), :]
    return ref.at[:, _ds(half, c // 2)]


def _half_of_shard(ref, r, c, axis, half):
    if axis == 1:
        return ref.at[_ds(half, r // 2), :]
    return ref.at[:, _ds(half, c // 2)]


def _region_of_half(ref, r, c, axis, shard):
    if axis == 1:
        return ref.at[:, _ds(shard, c // N_CHIPS)]
    return ref.at[_ds(shard, r // N_CHIPS), :]


def _place():
    x, y, c = lax.axis_index("x"), lax.axis_index("y"), lax.axis_index("c")
    chips = [(1 - x, y), (x, 1 - y), (1 - x, 1 - y)]
    return x, y, c, chips


LATE = (3, 4)
MID = (1, 2)
FIRST = (0,)
SMALL_HALF = (SMALL_ROWS // 2, LANES)


def _remote(src, dst, send_sem, recv_sem, to):
    return pltpu.make_async_remote_copy(src_ref=src, dst_ref=dst, send_sem=send_sem, recv_sem=recv_sem,
                                        device_id=to, device_id_type=MESH)


def _same(arrays):
    return [jax.ShapeDtypeStruct(a.shape, a.dtype) for a in arrays]


def _x_gather_send(ws, fulls):
    def copies(ins, outs, send_sems, recv_sems):
        x, y, c, chips = _place()
        out = []
        for j, w in enumerate(ws):
            _, r, cc, axis = BIG[w]
            mine = _region_of_full(outs[j], r, cc, axis, 2 * x + y, c)
            out += [_remote(mine, mine, send_sems.at[3 * j + k], recv_sems.at[3 * j + k], (cx, cy, c))
                    for k, (cx, cy) in enumerate(chips)]
        return out

    return _Exchange(fulls, _same(fulls), {j: j for j in range(len(ws))}, 3 * len(ws), copies)


def _x_gather_pass(ws, fulls):
    def copies(ins, outs, send_sems, recv_sems):
        x, y, c, chips = _place()
        out = []
        for j, w in enumerate(ws):
            _, r, cc, axis = BIG[w]
            for k, (cx, cy) in enumerate(chips):
                landed = _region_of_full(outs[j], r, cc, axis, 2 * cx + cy, c)
                out.append(_remote(landed, landed, send_sems.at[3 * j + k], recv_sems.at[3 * j + k], (x, y, 1 - c)))
        return out

    return _Exchange(fulls, _same(fulls), {j: j for j in range(len(ws))}, 3 * len(ws), copies)


def _x_pair(ws, grads, small=None):
    def copies(ins, outs, send_sems, recv_sems):
        x, y, c, _ = _place()
        out = []
        for j, w in enumerate(ws):
            _, r, cc, axis = BIG[w]
            out.append(_remote(_half_of_full(ins[j], r, cc, axis, 1 - c), outs[j], send_sems.at[j], recv_sems.at[j], (x, y, 1 - c)))
        if small is not None:
            j = len(ws)
            out.append(_remote(ins[j].at[_ds(1 - c, SMALL_ROWS // 2), :], outs[j], send_sems.at[j], recv_sems.at[j], (x, y, 1 - c)))
        return out

    shapes = [jax.ShapeDtypeStruct(_half_shape(*BIG[w][1:]), F32) for w in ws]
    extra = [] if small is None else [small]
    return _Exchange(list(grads) + extra, shapes + [jax.ShapeDtypeStruct(SMALL_HALF, F32)] * len(extra), {},
                     len(ws) + len(extra), copies)


def _x_chip(ws, sums_bf16, small_sum=None):
    def copies(ins, outs, send_sems, recv_sems):
        x, y, c, chips = _place()
        out = []
        for j, w in enumerate(ws):
            _, r, cc, axis = BIG[w]
            out += [_remote(_region_of_half(ins[j], r, cc, axis, 2 * cx + cy), outs[j].at[k],
                            send_sems.at[3 * j + k], recv_sems.at[3 * j + k], (cx, cy, c)) for k, (cx, cy) in enumerate(chips)]
        if small_sum is not None:
            j = len(ws)
            out += [_remote(ins[j], outs[j].at[k], send_sems.at[3 * j + k], recv_sems.at[3 * j + k], (cx, cy, c))
                    for k, (cx, cy) in enumerate(chips)]
        return out

    shapes = [jax.ShapeDtypeStruct((3,) + _region_shape(*BIG[w][1:]), BF16) for w in ws]
    extra = [] if small_sum is None else [small_sum]
    return _Exchange(list(sums_bf16) + extra, shapes + [jax.ShapeDtypeStruct((3,) + SMALL_HALF, F32)] * len(extra), {},
                     3 * (len(ws) + len(extra)), copies)


def _x_share(ws, shards, small=None):
    def copies(ins, outs, send_sems, recv_sems):
        x, y, c, _ = _place()
        out = []
        for j, w in enumerate(ws):
            _, r, cc, axis = BIG[w]
            mine = _half_of_shard(outs[j], r, cc, axis, c)
            out.append(_remote(mine, mine, send_sems.at[j], recv_sems.at[j], (x, y, 1 - c)))
        if small is not None:
            j = len(ws)
            mine = outs[j].at[_ds(c, SMALL_ROWS // 2), :]
            out.append(_remote(mine, mine, send_sems.at[j], recv_sems.at[j], (x, y, 1 - c)))
        return out

    arrays = list(shards) + ([] if small is None else [small])
    return _Exchange(arrays, _same(arrays), {j: j for j in range(len(arrays))}, len(arrays), copies)


class _Offset:
    def __init__(self, sems, base):
        self._sems, self._base = sems, base

    @property
    def at(self):
        return self

    def __getitem__(self, i):
        return self._sems.at[self._base + i]


def _x_join(a, b):
    ka_in, ka_out = len(a.operands), len(a.out_shapes)

    def copies(ins, outs, send_sems, recv_sems):
        return (a.copies(ins[:ka_in], outs[:ka_out], send_sems, recv_sems)
                + b.copies(ins[ka_in:], outs[ka_out:], _Offset(send_sems, a.n), _Offset(recv_sems, a.n)))

    aliases = dict(a.aliases)
    aliases.update({ka_in + i: ka_out + o for i, o in b.aliases.items()})
    return _Exchange(a.operands + b.operands, a.out_shapes + b.out_shapes, aliases, a.n + b.n, copies)


def _run_exchange(name, exchange):
    k_in, k_out = len(exchange.operands), len(exchange.out_shapes)

    def body(*refs):
        copies = exchange.copies(refs[:k_in], refs[k_in:k_in + k_out], refs[-2], refs[-1])
        for cp in copies:
            cp.start()
        for cp in copies:
            cp.wait()

    return list(pl.pallas_call(
        body, name=name, in_specs=[ANY] * k_in, out_specs=[ANY] * k_out, out_shape=exchange.out_shapes,
        scratch_shapes=[pltpu.SemaphoreType.DMA((exchange.n,)), pltpu.SemaphoreType.DMA((exchange.n,))],
        input_output_aliases=dict(exchange.aliases),
    )(*exchange.operands))


def _gather_now(name, ws, fulls):
    send, onward = _x_gather_send(ws, fulls), _x_gather_pass(ws, fulls)
    k = len(fulls)

    def body(*refs):
        ins, outs, (send_a, recv_a, send_b, recv_b) = refs[:k], refs[k:2 * k], refs[2 * k:]
        first = send.copies(ins, outs, send_a, recv_a)
        second = onward.copies(ins, outs, send_b, recv_b)
        for cp in first:
            cp.start()
        for landed, cp in zip(first, second):
            landed.wait()
            cp.start()
        for cp in second:
            cp.wait()

    sems = lambda n: pltpu.SemaphoreType.DMA((n,))
    return list(pl.pallas_call(
        body, name=name, in_specs=[ANY] * k, out_specs=[ANY] * k, out_shape=send.out_shapes,
        scratch_shapes=[sems(send.n), sems(send.n), sems(onward.n), sems(onward.n)],
        input_output_aliases={j: j for j in range(k)},
    )(*fulls))


def _tile_rows(shape):
    return (shape[0] // EXCH_TILES, shape[1])


def _cast_place(ids, shards):
    def body(ids_ref, *refs):
        for j in range(len(BIG)):
            refs[len(BIG) + j][...] = refs[j][...].astype(BF16)

    def out_spec(w):
        _, r, cc, axis = BIG[w]
        tile = _tile_rows(_shard_shape(r, cc, axis))
        if axis == 1:
            return pl.BlockSpec(tile, lambda i, ids: (i, ids[1]))
        return pl.BlockSpec(tile, lambda i, ids: (ids[1] * EXCH_TILES + i, 0))

    return list(pl.pallas_call(
        body, name="cast_place",
        grid_spec=pltpu.PrefetchScalarGridSpec(
            num_scalar_prefetch=1, grid=(EXCH_TILES,),
            in_specs=[pl.BlockSpec(_tile_rows(_shard_shape(r, cc, axis)), lambda i, ids: (i, 0)) for _, r, cc, axis in BIG],
            out_specs=[out_spec(w) for w in range(len(BIG))]),
        out_shape=[jax.ShapeDtypeStruct((r, cc), BF16) for _, r, cc, _ in BIG],
        compiler_params=pltpu.CompilerParams(dimension_semantics=("parallel",)),
    )(ids, *shards))


def _pair_sum(ids, ws, grads, got, small=None):
    nb = len(ws)
    n = nb + (small is not None)
    halves = [_half_shape(*BIG[w][1:]) for w in ws] + [SMALL_HALF] * (n - nb)

    def body(ids_ref, *refs):
        for j in range(n):
            total = refs[j][...] + refs[n + j][...]
            refs[2 * n + j][...] = total.astype(refs[2 * n + j].dtype)

    def mine_spec(j):
        tile = _tile_rows(halves[j])
        if j < nb and BIG[ws[j]][3] == 0:
            return pl.BlockSpec(tile, lambda i, ids: (i, ids[0]))
        return pl.BlockSpec(tile, lambda i, ids: (ids[0] * EXCH_TILES + i, 0))

    plain = lambda j: pl.BlockSpec(_tile_rows(halves[j]), lambda i, ids: (i, 0))
    outs = pl.pallas_call(
        body, name="pair_sum",
        grid_spec=pltpu.PrefetchScalarGridSpec(
            num_scalar_prefetch=1, grid=(EXCH_TILES,),
            in_specs=[mine_spec(j) for j in range(n)] + [plain(j) for j in range(n)],
            out_specs=[plain(j) for j in range(n)]),
        out_shape=[jax.ShapeDtypeStruct(h, BF16 if j < nb else F32) for j, h in enumerate(halves)],
        compiler_params=pltpu.CompilerParams(dimension_semantics=("parallel",)),
    )(ids, *grads, *([] if small is None else [small]), *got)
    return list(outs)


def _chip_sum(ids, ws, sums, arrived):
    nb, n = len(ws), len(sums)
    regions = [_region_shape(*BIG[w][1:]) for w in ws] + [SMALL_HALF] * (n - nb)

    def body(ids_ref, *refs):
        for j in range(n):
            own, got, out = refs[j], refs[n + j], refs[2 * n + j]
            if j < nb:
                out[...] = ((own[...].astype(F32) + got[0].astype(F32)) + got[1].astype(F32)) + got[2].astype(F32)
            else:
                out[...] = (own[...] + got[1]) + (got[0] + got[2])

    def own_spec(j):
        tile = _tile_rows(regions[j])
        if j >= nb:
            return pl.BlockSpec(tile, lambda i, ids: (i, 0))
        if BIG[ws[j]][3] == 1:
            return pl.BlockSpec(tile, lambda i, ids: (i, ids[1]))
        return pl.BlockSpec(tile, lambda i, ids: (ids[1] * EXCH_TILES + i, 0))

    def out_spec(j):
        tile = _tile_rows(regions[j])
        if j < nb and BIG[ws[j]][3] == 0:
            return pl.BlockSpec(tile, lambda i, ids: (i, ids[0]))
        return pl.BlockSpec(tile, lambda i, ids: (ids[0] * EXCH_TILES + i, 0))

    got_spec = lambda j: pl.BlockSpec((3,) + _tile_rows(regions[j]), lambda i, ids: (0, i, 0))
    return list(pl.pallas_call(
        body, name="chip_sum",
        grid_spec=pltpu.PrefetchScalarGridSpec(
            num_scalar_prefetch=1, grid=(EXCH_TILES,),
            in_specs=[own_spec(j) for j in range(n)] + [got_spec(j) for j in range(n)],
            out_specs=[out_spec(j) for j in range(n)]),
        out_shape=[jax.ShapeDtypeStruct(_shard_shape(*BIG[w][1:]), F32) for w in ws]
        + [jax.ShapeDtypeStruct((SMALL_ROWS, LANES), F32)] * (n - nb),
        compiler_params=pltpu.CompilerParams(dimension_semantics=("parallel",)),
    )(ids, *sums, *arrived))


def _adamw_body(n):
    c1 = 1.0 - ADAM_B1 ** ADAM_STEP
    c2 = 1.0 - ADAM_B2 ** ADAM_STEP

    def body(*refs):
        for j in range(n):
            w, g, m, v = (refs[k * n + j][...] for k in range(4))
            m = ADAM_B1 * m + (1.0 - ADAM_B1) * g
            v = ADAM_B2 * v + (1.0 - ADAM_B2) * (g * g)
            refs[4 * n + j][...] = -ADAM_LR * ((m / c1) / (jnp.sqrt(v / c2) + ADAM_EPS) + ADAM_WD * w)
            refs[5 * n + j][...] = m
            refs[6 * n + j][...] = v

    return body


def _adamw(name, ws, gs, ms, vs):
    n = len(ws)
    specs = [pl.BlockSpec(_tile_rows(w.shape), lambda i: (i, 0)) for w in ws]
    shapes = [jax.ShapeDtypeStruct(w.shape, F32) for w in ws]
    out, _ = _call(_adamw_body(n), name, (EXCH_TILES,), specs * 4, specs * 3, shapes * 3, [], (*ws, *gs, *ms, *vs))
    return out[:n], out[n:2 * n], out[2 * n:]


def _adamw_whole(name, ws, gs, ms, vs):
    n = len(ws)
    out = pl.pallas_call(_adamw_body(n), name=name, out_shape=[jax.ShapeDtypeStruct(w.shape, F32) for w in ws] * 3)(*ws, *gs, *ms, *vs)
    return out[:n], out[n:2 * n], out[2 * n:]


def kernel(x, norm_mix_pre, norm_mix_post, w_in, ret_gn_gain, ssm_lambda_re, ssm_lambda_im, ssm_log_dt, ssm_b_re, ssm_b_im, ssm_c_re, ssm_c_im, ssm_d, w_glu, w_out, norm_mlp_pre, norm_mlp_post, w_ff1, w_ff2, loss_target, m_norm_mix_pre, m_norm_mix_post, m_w_in, m_ret_gn_gain, m_ssm_lambda_re, m_ssm_lambda_im, m_ssm_log_dt, m_ssm_b_re, m_ssm_b_im, m_ssm_c_re, m_ssm_c_im, m_ssm_d, m_w_glu, m_w_out, m_norm_mlp_pre, m_norm_mlp_post, m_w_ff1, m_w_ff2, v_norm_mix_pre, v_norm_mix_post, v_w_in, v_ret_gn_gain, v_ssm_lambda_re, v_ssm_lambda_im, v_ssm_log_dt, v_ssm_b_re, v_ssm_b_im, v_ssm_c_re, v_ssm_c_im, v_ssm_d, v_w_glu, v_w_out, v_norm_mlp_pre, v_norm_mlp_post, v_w_ff1, v_w_ff2):
    given = dict(locals())
    order = ["norm_mix_pre", "norm_mix_post", "w_in", "ret_gn_gain", "ssm_lambda_re", "ssm_lambda_im", "ssm_log_dt",
             "ssm_b_re", "ssm_b_im", "ssm_c_re", "ssm_c_im", "ssm_d", "w_glu", "w_out", "norm_mlp_pre", "norm_mlp_post",
             "w_ff1", "w_ff2"]
    big_names = [name for name, _, _, _ in BIG]
    ids = jnp.stack([lax.axis_index("c"), 2 * lax.axis_index("x") + lax.axis_index("y")]).astype(jnp.int32)

    weights = _cast_place(ids, [given[name][0] for name in big_names])
    w_in = _gather_now("gather_w_in", FIRST, weights[:1])
    small_w = {name: given[name][0] for name, _ in SMALL}
    late_state = [[given[prefix + big_names[w]][0] for w in LATE] for prefix in ("", "m_", "v_")]
    dx, d_in, mid_parts, late_grads, late_stepped, small_grads = _local_step(
        x[0], loss_target[0], small_w, w_in + weights[1:], ids, late_state)

    small_packed = _pack_small(small_grads)
    got = _run_exchange("pair_exchange", _x_join(_x_share(MID, mid_parts), _x_pair(FIRST, [d_in], small_packed)))
    mid_grads, got = got[:len(MID)], got[len(MID):]
    sums = _pair_sum(ids, FIRST, [d_in], got, small_packed)
    arrived = _run_exchange("chip_exchange", _x_chip(FIRST, sums[:-1], sums[-1]))
    parts = _chip_sum(ids, FIRST, sums, arrived)
    g_in, g_small = _run_exchange("pair_share", _x_share(FIRST, parts[:-1], parts[-1]))
    big_grads = [g_in] + mid_grads + late_grads
    rest = lambda prefix: [given[prefix + big_names[w]][0] for w in FIRST + MID]
    rest_out = _adamw("adamw_big", rest(""), big_grads[:len(FIRST + MID)], rest("m_"), rest("v_"))
    big_out = [list(rest_out[k]) + list(late_stepped[k]) for k in range(3)]
    small_names = [name for name, _ in SMALL]
    small_tree = _unpack_small(g_small)
    small_grads = [small_tree[name] for name in small_names]
    small = lambda prefix: [given[prefix + name] for name in small_names]
    small_out = _adamw_whole("adamw_small", small(""), small_grads, small("m_"), small("v_"))

    def in_order(big_arrays, small_arrays):
        tree = {name: a[None] for name, a in zip(big_names, big_arrays)}
        tree.update(zip(small_names, small_arrays))
        return [tree[name] for name in order]

    loss = small_tree["loss_sum"].reshape(())
    return (loss, dx[None], *in_order(big_grads, small_grads), *in_order(big_out[0], small_out[0]),
            *in_order(big_out[1], small_out[1]), *in_order(big_out[2], small_out[2]))
```

```python
import functools
import math

import jax
import jax.numpy as jnp
import numpy as np
from jax import lax
from jax.experimental import pallas as pl
from jax.experimental.pallas import tpu as pltpu

F32 = jnp.float32
BF16 = jnp.bfloat16

D_MODEL = 1024
RET_WIDTH = 512
RET_HEADS = 4
HEAD_DIM = 128
RET_CHUNK = 128
ROPE_BASE = 10000.0
SSM_WIDTH = 512
SSM_GROUP = 16
SSM_GROUPS = 32
SSM_STATE = 64
N_STATES = SSM_GROUPS * SSM_STATE
D_FF = 4096
IN_COLS = 4 * RET_WIDTH + SSM_WIDTH
NORM_EPS = 1e-6
K_SCALE = HEAD_DIM ** -0.5

ADAM_LR = 0.001
ADAM_B1 = 0.9
ADAM_B2 = 0.999
ADAM_EPS = 1e-08
ADAM_WD = 0.01
ADAM_STEP = 10

LANES = 128
SUBLANES = 8
VMEM_LIMIT = 56 * 2 ** 20

TOK_TILE = 512
MLP_TILE = 256
RET_CHUNKS_PER_STEP = 4
S5_TILE = 256
S5_STEPS = S5_TILE // SUBLANES
S5_COLBLK = N_STATES // LANES
S5_GROUP = 4

N_CHIPS = 4
MESH = pl.DeviceIdType.MESH


def _dot(a, b):
    return jnp.dot(a, b, preferred_element_type=F32)


def _dot_nt(a, b):
    return lax.dot_general(a, b, (((1,), (1,)), ((), ())), preferred_element_type=F32)


def _dot_tn(a, b):
    return lax.dot_general(a, b, (((0,), (0,)), ((), ())), preferred_element_type=F32)


def _sigmoid(x):
    return 1.0 / (1.0 + jnp.exp(-x))


_GELU_C = math.sqrt(2.0 / math.pi)


def _gelu(x):
    return 0.5 * x * (1.0 + jnp.tanh(_GELU_C * (x + 0.044715 * (x * x * x))))


def _gelu_grad(x):
    t = jnp.tanh(_GELU_C * (x + 0.044715 * (x * x * x)))
    return 0.5 * (1.0 + t) + 0.5 * x * (1.0 - t * t) * (_GELU_C * (1.0 + 3.0 * 0.044715 * (x * x)))


def _rms(x):
    r = lax.rsqrt(jnp.mean(x * x, axis=-1, keepdims=True) + NORM_EPS)
    return x * r, r


def _rms_bwd(n, r, gain, dy):
    dn = dy * gain
    dx = r * (dn - n * jnp.mean(dn * n, axis=-1, keepdims=True))
    return dx, jnp.sum(dy * n, axis=0, keepdims=True)


def _full(shape):
    nd = len(shape)
    return pl.BlockSpec(shape, lambda i, _n=nd: (0,) * _n, pipeline_mode=pl.Buffered(1))


def _rows(tile, width):
    return pl.BlockSpec((tile, width), lambda i: (i, 0))


def _rows_rev(tile, width, n):
    return pl.BlockSpec((tile, width), lambda i, _n=n: (_n - 1 - i, 0))


def _params(sem):
    return pltpu.CompilerParams(dimension_semantics=(sem,), vmem_limit_bytes=VMEM_LIMIT)


ANY = pl.BlockSpec(memory_space=pl.ANY)


class _Exchange:
    def __init__(self, operands, out_shapes, aliases, n, copies):
        self.operands, self.out_shapes, self.aliases, self.n, self.copies = list(operands), list(out_shapes), aliases, n, copies


def _call(body, name, grid, in_specs, out_specs, out_shape, scratch_shapes, args, exchange=None):
    if exchange is None:
        outs = pl.pallas_call(body, name=name, grid=grid, in_specs=in_specs, out_specs=out_specs, out_shape=out_shape,
                              scratch_shapes=scratch_shapes, compiler_params=_params("arbitrary"))(*args)
        return list(outs), []
    n_in, n_out, n_scr = len(in_specs), len(out_specs), len(scratch_shapes)
    k_in, k_out = len(exchange.operands), len(exchange.out_shapes)
    last = grid[0] - 1

    def hosted(*refs):
        own_in, rest = refs[:n_in], refs[n_in:]
        ex_in, rest = rest[:k_in], rest[k_in:]
        own_out, rest = rest[:n_out], rest[n_out:]
        ex_out, rest = rest[:k_out], rest[k_out:]
        own_scr, (send_sems, recv_sems) = rest[:n_scr], rest[n_scr:]

        @pl.when(pl.program_id(0) == 0)
        def _():
            for cp in exchange.copies(ex_in, ex_out, send_sems, recv_sems):
                cp.start()

        body(*own_in, *own_out, *own_scr)

        @pl.when(pl.program_id(0) == last)
        def _():
            for cp in exchange.copies(ex_in, ex_out, send_sems, recv_sems):
                cp.wait()

    outs = pl.pallas_call(
        hosted, name=name, grid=grid, in_specs=list(in_specs) + [ANY] * k_in, out_specs=list(out_specs) + [ANY] * k_out,
        out_shape=list(out_shape) + exchange.out_shapes,
        scratch_shapes=list(scratch_shapes) + [pltpu.SemaphoreType.DMA((exchange.n,)), pltpu.SemaphoreType.DMA((exchange.n,))],
        input_output_aliases={n_in + i: n_out + o for i, o in exchange.aliases.items()},
        compiler_params=_params("arbitrary"),
    )(*args, *exchange.operands)
    return list(outs[:n_out]), list(outs[n_out:])


def _rope(t, cos2, sin2):
    return t * cos2 + pltpu.roll(t, HEAD_DIM // 2, 1) * sin2


def _rope_bwd(d, cos2, sin2):
    return d * cos2 + pltpu.roll(d * sin2, HEAD_DIM // 2, d.ndim - 1)


def _inproj_fwd(x, gain, w_in, cos2, sin2, exchange=None):
    seq = x.shape[0]

    def body(x_ref, g_ref, w_ref, c_ref, s_ref, q_ref, k_ref, v_ref, gate_ref, u_ref):
        n, _ = _rms(x_ref[...])
        h = (n * g_ref[...]).astype(BF16)
        proj = _dot(h, w_ref[...])
        c = c_ref[...]
        s = s_ref[...]
        for hd in range(RET_HEADS):
            lo = hd * HEAD_DIM
            q_ref[:, lo:lo + HEAD_DIM] = _rope(proj[:, lo:lo + HEAD_DIM], c, s).astype(BF16)
            kh = proj[:, RET_WIDTH + lo:RET_WIDTH + lo + HEAD_DIM]
            k_ref[:, lo:lo + HEAD_DIM] = (_rope(kh, c, s) * K_SCALE).astype(BF16)
        v_ref[...] = proj[:, 2 * RET_WIDTH:3 * RET_WIDTH].astype(BF16)
        gate_ref[...] = proj[:, 3 * RET_WIDTH:4 * RET_WIDTH]
        u_ref[...] = proj[:, 4 * RET_WIDTH:]

    t = TOK_TILE
    half = lambda dt: jax.ShapeDtypeStruct((seq, RET_WIDTH), dt)
    return _call(
        body, "inproj_fwd", (seq // t,),
        [_rows(t, D_MODEL), _full((1, D_MODEL)), _full((D_MODEL, IN_COLS)), _rows(t, HEAD_DIM), _rows(t, HEAD_DIM)],
        [_rows(t, RET_WIDTH)] * 5, [half(BF16), half(BF16), half(BF16), half(F32), half(F32)], [],
        (x, gain, w_in, cos2, sin2), exchange)


def _inproj_bwd(x, gain, w_in, dpieces, dres, adam=()):
    seq = x.shape[0]
    t = TOK_TILE
    steps = seq // t
    n_pieces = len(dpieces)
    n_adam = len(adam[0]) if adam else 0
    adam_step = _adamw_body(n_adam)

    def body(x_ref, g_ref, w_ref, *refs):
        piece_refs, (dres_ref, *adam_in), (dx_ref, dw_ref, dg_ref, *adam_out) = (
            refs[:n_pieces], refs[n_pieces:n_pieces + 1 + 4 * n_adam], refs[n_pieces + 1 + 4 * n_adam:])

        @pl.when(pl.program_id(0) == 0)
        def _():
            dw_ref[...] = jnp.zeros_like(dw_ref)
            dg_ref[...] = jnp.zeros_like(dg_ref)

        adam_step(*adam_in, *adam_out)

        n, r = _rms(x_ref[...])
        gain_v = g_ref[...]
        h = (n * gain_v).astype(BF16)
        dh = None
        for j, piece in enumerate(piece_refs):
            cols = slice(j * RET_WIDTH, (j + 1) * RET_WIDTH)
            dp = piece[...]
            dw_ref[:, cols] += _dot_tn(h, dp)
            term = _dot_nt(dp, w_ref[:, cols])
            dh = term if dh is None else dh + term
        dx, dgs = _rms_bwd(n, r, gain_v, dh)
        dg_ref[...] += dgs
        dx_ref[...] = dres_ref[...] + dx

    adam_arrays = [a for group in adam for a in group]
    adam_spec = lambda a: pl.BlockSpec((a.shape[0] // steps, a.shape[1]), lambda i: (i, 0))
    outs, _ = _call(
        body, "inproj_bwd", (steps,),
        [_rows(t, D_MODEL), _full((1, D_MODEL)), _full((D_MODEL, IN_COLS))] + [_rows(t, RET_WIDTH)] * n_pieces + [_rows(t, D_MODEL)]
        + [adam_spec(a) for a in adam_arrays],
        [_rows(t, D_MODEL), _full((D_MODEL, IN_COLS)), _full((1, D_MODEL))] + [adam_spec(a) for a in adam_arrays[:n_adam]] * 3,
        [jax.ShapeDtypeStruct((seq, D_MODEL), F32), jax.ShapeDtypeStruct((D_MODEL, IN_COLS), F32),
         jax.ShapeDtypeStruct((1, D_MODEL), F32)] + [jax.ShapeDtypeStruct(a.shape, F32) for a in adam_arrays[:n_adam]] * 3,
        [], (x, gain, w_in, *dpieces, dres, *adam_arrays))
    stepped = outs[3:]
    return outs[:3], (stepped[:n_adam], stepped[n_adam:2 * n_adam], stepped[2 * n_adam:])


def _ret_consts():
    c = RET_CHUNK
    f32 = np.float32
    log_gamma = np.log(f32(1.0) - np.exp(np.linspace(math.log(1.0 / 32), math.log(1.0 / 512), RET_HEADS, dtype=f32))).astype(f32)
    idx = np.arange(c, dtype=f32)
    diff = idx[:, None] - idx[None, :]
    decay = np.where(diff[None] >= 0, np.exp(np.maximum(diff, f32(0.0))[None] * log_gamma[:, None, None]), f32(0.0))
    zeta = np.exp((c - 1 - idx)[None, :] * log_gamma[:, None])
    xi = np.exp((idx + f32(1.0))[None, :] * log_gamma[:, None])
    g_chunk = np.exp(f32(c) * log_gamma)
    wide = lambda rowvals: jnp.asarray(np.broadcast_to(rowvals[:, :, None], (RET_HEADS, c, c)).astype(f32))
    return (jnp.asarray(decay.astype(f32)), wide(xi), wide(zeta),
            jnp.asarray(np.broadcast_to(g_chunk[:, None, None], (RET_HEADS, c, c)).astype(f32)))


def _rope_tables(seq):
    f32 = np.float32
    half = HEAD_DIM // 2
    inv_freq = np.power(f32(ROPE_BASE), -np.arange(half, dtype=f32) / f32(half)).astype(f32)
    ang = (np.arange(seq, dtype=f32)[:, None] * inv_freq[None, :]).astype(f32)
    cos, sin = np.cos(ang).astype(f32), np.sin(ang).astype(f32)
    return jnp.asarray(np.concatenate([cos, cos], axis=1)), jnp.asarray(np.concatenate([-sin, sin], axis=1))


def _bmm(a, b):
    return lax.dot_general(a, b, (((2,), (1,)), ((0,), (0,))), preferred_element_type=F32)


def _bmm_nt(a, b):
    return lax.dot_general(a, b, (((2,), (2,)), ((0,), (0,))), preferred_element_type=F32)


def _bmm_tn(a, b):
    return lax.dot_general(a, b, (((1,), (1,)), ((0,), (0,))), preferred_element_type=F32)


def _ret_blocks(ref):
    return jnp.stack([ref[cc * RET_CHUNK:(cc + 1) * RET_CHUNK, hd * HEAD_DIM:(hd + 1) * HEAD_DIM]
                      for cc in range(RET_CHUNKS_PER_STEP) for hd in range(RET_HEADS)])


def _ret_unblock(ref, blocks):
    for cc in range(RET_CHUNKS_PER_STEP):
        for hd in range(RET_HEADS):
            ref[cc * RET_CHUNK:(cc + 1) * RET_CHUNK, hd * HEAD_DIM:(hd + 1) * HEAD_DIM] = blocks[cc * RET_HEADS + hd].astype(ref.dtype)


def _rope_blocks(ref):
    return jnp.stack([ref[cc * RET_CHUNK:(cc + 1) * RET_CHUNK, :] for cc in range(RET_CHUNKS_PER_STEP) for _ in range(RET_HEADS)])


def _per_chunk(ref):
    return jnp.concatenate([ref[...]] * RET_CHUNKS_PER_STEP, axis=0)


def _ret_gain(gain_ref):
    return jnp.stack([gain_ref[:, hd * HEAD_DIM:(hd + 1) * HEAD_DIM]
                      for _ in range(RET_CHUNKS_PER_STEP) for hd in range(RET_HEADS)])


def _ret_fwd(q, k, v, gate, gn_gain, consts, exchange=None):
    seq = q.shape[0]
    cps = RET_CHUNKS_PER_STEP
    t = cps * RET_CHUNK
    n_chunks = seq // RET_CHUNK
    dec, xi, zeta, gch = consts

    def body(q_ref, k_ref, v_ref, gate_ref, gain_ref, dec_ref, xi_ref, zeta_ref, gch_ref, y_ref, rprev_ref, state):
        @pl.when(pl.program_id(0) == 0)
        def _():
            state[...] = jnp.zeros_like(state)

        qb, kb, vb = _ret_blocks(q_ref), _ret_blocks(k_ref), _ret_blocks(v_ref)
        s = _bmm_nt(qb, kb) * _per_chunk(dec_ref)
        vz = (vb.astype(F32) * _per_chunk(zeta_ref)).astype(BF16)
        incr = _bmm_tn(kb, vz)
        gch_v = gch_ref[...]
        rp = state[...]
        before = []
        for cc in range(cps):
            before.append(rp.astype(BF16))
            rprev_ref[cc] = before[-1]
            rp = gch_v * rp + incr[cc * RET_HEADS:(cc + 1) * RET_HEADS]
        state[...] = rp
        qx = (qb.astype(F32) * _per_chunk(xi_ref)).astype(BF16)
        o = _bmm(s.astype(BF16), vb) + _bmm(qx, jnp.concatenate(before, axis=0))
        oc = o - jnp.mean(o, axis=-1, keepdims=True)
        on = oc * lax.rsqrt(jnp.mean(oc * oc, axis=-1, keepdims=True) + NORM_EPS)
        g = _ret_blocks(gate_ref)
        _ret_unblock(y_ref, g * _sigmoid(g) * (on * _ret_gain(gain_ref)))

    cst = _full((RET_HEADS, RET_CHUNK, RET_CHUNK))
    return _call(
        body, "ret_fwd", (seq // t,),
        [_rows(t, RET_WIDTH)] * 4 + [_full((1, RET_WIDTH)), cst, cst, cst, cst],
        [_rows(t, RET_WIDTH), pl.BlockSpec((cps, RET_HEADS, HEAD_DIM, HEAD_DIM), lambda i: (i, 0, 0, 0))],
        [jax.ShapeDtypeStruct((seq, RET_WIDTH), BF16), jax.ShapeDtypeStruct((n_chunks, RET_HEADS, HEAD_DIM, HEAD_DIM), BF16)],
        [pltpu.VMEM((RET_HEADS, HEAD_DIM, HEAD_DIM), F32)],
        (q, k, v, gate, gn_gain, dec, xi, zeta, gch), exchange)


def _ret_bwd(q, k, v, gate, gn_gain, consts, rprev, dy_ret, cos2, sin2, exchange=None):
    seq = q.shape[0]
    cps = RET_CHUNKS_PER_STEP
    t = cps * RET_CHUNK
    nt = seq // t
    dec, xi, zeta, gch = consts

    def body(q_ref, k_ref, v_ref, gate_ref, gain_ref, dec_ref, xi_ref, zeta_ref, gch_ref, rprev_ref, dyr_ref, cos_ref, sin_ref,
             dq_ref, dk_ref, dv_ref, dgate_ref, dgain_ref, dstate):
        @pl.when(pl.program_id(0) == 0)
        def _():
            dstate[...] = jnp.zeros_like(dstate)
            dgain_ref[...] = jnp.zeros_like(dgain_ref)

        nb = cps * RET_HEADS
        qb, kb, vb = _ret_blocks(q_ref), _ret_blocks(k_ref), _ret_blocks(v_ref)
        dec_b, xi_b, zeta_b = _per_chunk(dec_ref), _per_chunk(xi_ref), _per_chunk(zeta_ref)
        rpb = rprev_ref[...].reshape(nb, HEAD_DIM, HEAD_DIM)
        sdb = (_bmm_nt(qb, kb) * dec_b).astype(BF16)
        qx = (qb.astype(F32) * xi_b).astype(BF16)
        o = _bmm(sdb, vb) + _bmm(qx, rpb)
        oc = o - jnp.mean(o, axis=-1, keepdims=True)
        rstd = lax.rsqrt(jnp.mean(oc * oc, axis=-1, keepdims=True) + NORM_EPS)
        on = oc * rstd
        g = _ret_blocks(gate_ref)
        sg = _sigmoid(g)
        gain_b = _ret_gain(gain_ref)
        dyr = _ret_blocks(dyr_ref)
        _ret_unblock(dgate_ref, dyr * (on * gain_b) * (sg * (1.0 + g * (1.0 - sg))))
        dy = dyr * (g * sg)
        dgain = jnp.sum(dy * on, axis=1, keepdims=True)
        for hd in range(RET_HEADS):
            part = dgain[hd]
            for cc in range(1, cps):
                part = part + dgain[cc * RET_HEADS + hd]
            dgain_ref[:, hd * HEAD_DIM:(hd + 1) * HEAD_DIM] += part
        don = dy * gain_b
        do = rstd * (don - jnp.mean(don, axis=-1, keepdims=True) - on * jnp.mean(don * on, axis=-1, keepdims=True))
        dob = do.astype(BF16)
        ds = (_bmm_nt(dob, vb) * dec_b).astype(BF16)
        dq = _bmm(ds, kb) + _bmm_nt(dob, rpb) * xi_b
        dk = _bmm_tn(ds, qb)
        dv = _bmm_tn(sdb, dob)
        dstate_local = _bmm_tn(qx, dob)
        vz = (vb.astype(F32) * zeta_b).astype(BF16)
        gch_v = gch_ref[...]
        zeta_v = zeta_ref[...]
        gh = dstate[...]
        dk_extra, dv_extra = [None] * cps, [None] * cps
        for cc in reversed(range(cps)):
            sl = slice(cc * RET_HEADS, (cc + 1) * RET_HEADS)
            gb = gh.astype(BF16)
            dk_extra[cc] = _bmm_nt(vz[sl], gb)
            dv_extra[cc] = _bmm(kb[sl], gb) * zeta_v
            gh = dstate_local[sl] + gch_v * gh
        dstate[...] = gh
        cos_b, sin_b = _rope_blocks(cos_ref), _rope_blocks(sin_ref)
        _ret_unblock(dq_ref, _rope_bwd(dq, cos_b, sin_b))
        _ret_unblock(dk_ref, _rope_bwd((dk + jnp.concatenate(dk_extra, axis=0)) * K_SCALE, cos_b, sin_b))
        _ret_unblock(dv_ref, dv + jnp.concatenate(dv_extra, axis=0))

    cst = _full((RET_HEADS, RET_CHUNK, RET_CHUNK))
    rev = _rows_rev(t, RET_WIDTH, nt)
    act = jax.ShapeDtypeStruct((seq, RET_WIDTH), BF16)
    return _call(
        body, "ret_bwd", (nt,),
        [rev] * 4 + [_full((1, RET_WIDTH)), cst, cst, cst, cst,
                     pl.BlockSpec((cps, RET_HEADS, HEAD_DIM, HEAD_DIM), lambda i: (nt - 1 - i, 0, 0, 0)), rev,
                     _rows_rev(t, HEAD_DIM, nt), _rows_rev(t, HEAD_DIM, nt)],
        [rev] * 4 + [_full((1, RET_WIDTH))], [act, act, act, act, jax.ShapeDtypeStruct((1, RET_WIDTH), F32)],
        [pltpu.VMEM((RET_HEADS, HEAD_DIM, HEAD_DIM), F32)],
        (q, k, v, gate, gn_gain, dec, xi, zeta, gch, rprev, dy_ret, cos2, sin2), exchange)


MXU_TILE = 256
S5_STATE_TILES = 2 * N_STATES // MXU_TILE
S5_CHANNELS_PER_TILE = MXU_TILE // SSM_STATE * SSM_GROUP
S5_TILES_SHAPE = (S5_STATE_TILES, LANES, MXU_TILE)


def _chan_block(kk):
    return ((kk % (S5_STATE_TILES // 2)) * S5_CHANNELS_PER_TILE) // LANES * LANES


def _s5_tiles(blocks_re, blocks_im):
    per = MXU_TILE // SSM_STATE
    half = S5_STATE_TILES // 2
    eye = jnp.eye(per, dtype=F32)
    in_half = (jnp.arange(half) % (LANES // S5_CHANNELS_PER_TILE))[:, None] == jnp.arange(LANES // S5_CHANNELS_PER_TILE)[None, :]
    out = []
    for blk in (blocks_re, blocks_im):
        t = blk.reshape(half, per, SSM_GROUP, 1, SSM_STATE) * eye[None, :, None, :, None]
        t = t.reshape(half, 1, S5_CHANNELS_PER_TILE, MXU_TILE)
        out.append(jnp.where(in_half[:, :, None, None], t, 0.0).reshape(half, LANES, MXU_TILE))
    return jnp.concatenate(out, axis=0)


def _s5_discretise(lam_re, lam_im, log_dt, b_re, b_im, c_re, c_im):
    lam = lax.complex(jnp.minimum(lam_re, -1e-4), lam_im)
    dt = jnp.exp(log_dt)[:, None]
    lam_bar = jnp.exp(lam * dt)
    b_bar = ((lam_bar - 1.0) / lam)[:, :, None] * lax.complex(b_re, b_im)
    b_tiles = _s5_tiles(jnp.swapaxes(jnp.real(b_bar), 1, 2), jnp.swapaxes(jnp.imag(b_bar), 1, 2))
    return jnp.real(lam_bar), jnp.imag(lam_bar), b_tiles, _s5_tiles(c_re, -c_im)


def _s5_tables(lbr, lbi):
    lr = lbr.reshape(1, N_STATES)
    li = lbi.reshape(1, N_STATES)
    pr, pi = lr, li
    while pr.shape[0] < S5_STEPS:
        top_r, top_i = pr[-1:], pi[-1:]
        pr, pi = (jnp.concatenate([pr, pr * top_r - pi * top_i], axis=0), jnp.concatenate([pi, pr * top_i + pi * top_r], axis=0))
    bc = lambda a: jnp.broadcast_to(a, (SUBLANES, N_STATES))
    rep = lambda rows: jnp.broadcast_to(rows[:, None, :], (S5_STEPS, SUBLANES, N_STATES)).reshape(S5_STEPS * SUBLANES, N_STATES)
    return bc(lr), bc(li), rep(pr), rep(pi)


def _cmul(ar, ai, br, bi):
    return ar * br - ai * bi, ar * bi + ai * br


def _seg(i):
    return pl.ds(pl.multiple_of(i * SUBLANES, SUBLANES), SUBLANES)


def _scan_order():
    r = np.arange(S5_TILE)
    token = (r % SUBLANES) * S5_STEPS + r // SUBLANES
    p = (token[:, None] == np.arange(S5_TILE)[None, :]).astype(np.float32)
    return jnp.asarray(p, dtype=BF16), jnp.asarray(p.T, dtype=BF16)


def _to_scan_order(p_ref, rows_bf16):
    return _dot(p_ref[...], rows_bf16).astype(BF16)


def _from_scan_order(pt_ref, rows, pieces=3):
    pt = pt_ref[...]
    out = None
    for _ in range(pieces):
        piece = rows.astype(BF16)
        rows = rows - piece.astype(F32)
        term = _dot(pt, piece)
        out = term if out is None else out + term
    return out


def _s5_channels_to_states(chan_b, m_ref, xs):
    for kk in range(S5_STATE_TILES):
        lo = _chan_block(kk)
        res = _dot(chan_b[:, lo:lo + LANES], m_ref[kk])
        xs[2 * kk] = res[:, :LANES]
        xs[2 * kk + 1] = res[:, LANES:]


def _s5_states_to_channels(sb_ref, m_ref):
    out = []
    for lo in range(0, SSM_WIDTH, LANES):
        acc = None
        for kk in range(S5_STATE_TILES):
            if _chan_block(kk) == lo:
                term = _dot_nt(sb_ref[:, kk * MXU_TILE:(kk + 1) * MXU_TILE], m_ref[kk])
                acc = term if acc is None else acc + term
        out.append(acc)
    return jnp.concatenate(out, axis=1)


def _s5_outer_acc(acc_ref, chan_b, states_ref):
    for kk in range(S5_STATE_TILES):
        lo = _chan_block(kk)
        acc_ref[kk] += _dot_tn(chan_b[:, lo:lo + LANES], states_ref[:, kk * MXU_TILE:(kk + 1) * MXU_TILE])


def _s5_scan_tile(xs, lr_ref, li_ref, pr_ref, pi_ref, carry_re, carry_im, conj, reverse, prev_of=None):
    sgn = -1.0 if conj else 1.0
    seg_in_re, seg_in_im, sums_re, sums_im = [], [], [], []
    group = S5_GROUP
    for grp in range(S5_COLBLK // group):
        blks = [grp * group + j for j in range(group)]
        lrs = [lr_ref[:, b * LANES:(b + 1) * LANES] for b in blks]
        lis = [sgn * li_ref[:, b * LANES:(b + 1) * LANES] for b in blks]

        def step(it, carry, blks=blks, lrs=lrs, lis=lis):
            i = (S5_STEPS - 1 - it) if reverse else it
            out_r, out_i = [], []
            for j, b in enumerate(blks):
                xr, xi = _cmul(lrs[j], lis[j], carry[j], carry[group + j])
                xr = xr + xs[b, _seg(i), :]
                xi = xi + xs[S5_COLBLK + b, _seg(i), :]
                xs[b, _seg(i), :] = xr
                xs[S5_COLBLK + b, _seg(i), :] = xi
                out_r.append(xr)
                out_i.append(xi)
            return tuple(out_r + out_i)

        zeros = tuple(jnp.zeros((SUBLANES, LANES), F32) for _ in range(2 * group))
        ends = lax.fori_loop(0, S5_STEPS, step, zeros, unroll=True)

        ins_r, ins_i = [], []
        for j, b in enumerate(blks):
            cols = slice(b * LANES, (b + 1) * LANES)
            pr = pr_ref[SUBLANES * S5_STEPS - 1:SUBLANES * S5_STEPS, cols]
            pi = sgn * pi_ref[SUBLANES * S5_STEPS - 1:SUBLANES * S5_STEPS, cols]
            cur_r = carry_re[:, cols]
            cur_i = carry_im[:, cols]
            row_id = lax.broadcasted_iota(jnp.int32, (SUBLANES, LANES), 0)
            in_r = jnp.zeros((SUBLANES, LANES), F32)
            in_i = jnp.zeros((SUBLANES, LANES), F32)
            order = range(SUBLANES - 1, -1, -1) if reverse else range(SUBLANES)
            for sgm in order:
                in_r = jnp.where(row_id == sgm, cur_r, in_r)
                in_i = jnp.where(row_id == sgm, cur_i, in_i)
                mr, mi = _cmul(pr, pi, cur_r, cur_i)
                cur_r = mr + ends[j][sgm:sgm + 1, :]
                cur_i = mi + ends[group + j][sgm:sgm + 1, :]
            carry_re[:, cols] = cur_r
            carry_im[:, cols] = cur_i
            ins_r.append(in_r)
            ins_i.append(in_i)

        def fix(it, c, blks=blks, ins_r=ins_r, ins_i=ins_i):
            pw = (S5_STEPS - 1 - it) if reverse else it
            out = []
            for j, b in enumerate(blks):
                cols = slice(b * LANES, (b + 1) * LANES)
                prow = pl.ds(pl.multiple_of(pw * SUBLANES, SUBLANES), SUBLANES)
                fr, fi = _cmul(pr_ref[prow, cols], sgn * pi_ref[prow, cols], ins_r[j], ins_i[j])
                vr = xs[b, _seg(it), :] + fr
                vi = xs[S5_COLBLK + b, _seg(it), :] + fi
                xs[b, _seg(it), :] = vr
                xs[S5_COLBLK + b, _seg(it), :] = vi
                if prev_of is not None:
                    opr, opi, sr, si = c[4 * j:4 * j + 4]
                    out += [prev_of[0][b, _seg(it), :], prev_of[0][S5_COLBLK + b, _seg(it), :],
                            sr + vr * opr + vi * opi, si + vi * opr - vr * opi]
            return tuple(out)

        init = ()
        if prev_of is not None:
            zero = jnp.zeros((SUBLANES, LANES), F32)
            for b in blks:
                init += (prev_of[1][b], prev_of[2][b], zero, zero)
        done = lax.fori_loop(0, S5_STEPS, fix, init, unroll=True)
        for j in range(len(blks) if prev_of is not None else 0):
            sums_re.append(done[4 * j + 2])
            sums_im.append(done[4 * j + 3])
        seg_in_re += ins_r
        seg_in_im += ins_i
    return seg_in_re, seg_in_im, sums_re, sums_im


def _s5_pack(xs, dst):
    for b in range(2 * S5_COLBLK):
        dst[:, b * LANES:(b + 1) * LANES] = xs[b].astype(BF16)


def _s5_fwd(u, b_blk, c_blk, tables, d_skip, exchange=None):
    seq = u.shape[0]
    t = S5_TILE
    nt = seq // t
    lr8, li8, p_re, p_im = tables

    p, pt = _scan_order()

    def body(u_ref, p_ref, pt_ref, b_ref, c_ref, lr_ref, li_ref, pr_ref, pi_ref, d_ref, y_ref, start_ref,
             xs, xb, carry_re, carry_im):
        @pl.when(pl.program_id(0) == 0)
        def _():
            carry_re[...] = jnp.zeros_like(carry_re)
            carry_im[...] = jnp.zeros_like(carry_im)

        start_ref[0, 0:1, :] = carry_re[...]
        start_ref[0, 1:2, :] = carry_im[...]
        uv = u_ref[...]
        _s5_channels_to_states(_to_scan_order(p_ref, uv.astype(BF16)), b_ref, xs)
        _s5_scan_tile(xs, lr_ref, li_ref, pr_ref, pi_ref, carry_re, carry_im, conj=False, reverse=False)
        _s5_pack(xs, xb)
        y_ref[...] = _from_scan_order(pt_ref, _s5_states_to_channels(xb, c_ref), pieces=2) + d_ref[...] * uv

    return _call(
        body, "s5_fwd", (nt,),
        [_rows(t, SSM_WIDTH), _full((t, t)), _full((t, t)), _full(S5_TILES_SHAPE), _full(S5_TILES_SHAPE),
         _full((SUBLANES, N_STATES)), _full((SUBLANES, N_STATES)), _full((SUBLANES * S5_STEPS, N_STATES)),
         _full((SUBLANES * S5_STEPS, N_STATES)), _full((1, SSM_WIDTH))],
        [_rows(t, SSM_WIDTH), pl.BlockSpec((1, 2, N_STATES), lambda i: (i, 0, 0))],
        [jax.ShapeDtypeStruct((seq, SSM_WIDTH), F32), jax.ShapeDtypeStruct((nt, 2, N_STATES), F32)],
        [pltpu.VMEM((2 * S5_COLBLK, t, LANES), F32), pltpu.VMEM((t, 2 * N_STATES), BF16),
         pltpu.VMEM((1, N_STATES), F32), pltpu.VMEM((1, N_STATES), F32)],
        (u, p, pt, b_blk, c_blk, lr8, li8, p_re, p_im, d_skip), exchange)


def _inproj_s5_fwd(x, gain, w_in, cos2, sin2, b_blk, c_blk, tables, d_skip, exchange=None):
    seq = x.shape[0]
    t = TOK_TILE
    sub = S5_TILE
    lr8, li8, p_re, p_im = tables
    p, pt = _scan_order()

    def body(x_ref, g_ref, w_ref, c_ref, s_ref, p_ref, pt_ref, b_ref, cm_ref, lr_ref, li_ref, pr_ref, pi_ref, d_ref,
             q_ref, k_ref, v_ref, gate_ref, u_ref, y_ref, start_ref, xs, xb, carry_re, carry_im):
        @pl.when(pl.program_id(0) == 0)
        def _():
            carry_re[...] = jnp.zeros_like(carry_re)
            carry_im[...] = jnp.zeros_like(carry_im)

        n, _ = _rms(x_ref[...])
        h = (n * g_ref[...]).astype(BF16)
        proj = _dot(h, w_ref[...])
        c = c_ref[...]
        s = s_ref[...]
        for hd in range(RET_HEADS):
            lo = hd * HEAD_DIM
            q_ref[:, lo:lo + HEAD_DIM] = _rope(proj[:, lo:lo + HEAD_DIM], c, s).astype(BF16)
            kh = proj[:, RET_WIDTH + lo:RET_WIDTH + lo + HEAD_DIM]
            k_ref[:, lo:lo + HEAD_DIM] = (_rope(kh, c, s) * K_SCALE).astype(BF16)
        v_ref[...] = proj[:, 2 * RET_WIDTH:3 * RET_WIDTH].astype(BF16)
        gate_ref[...] = proj[:, 3 * RET_WIDTH:4 * RET_WIDTH]
        u = proj[:, 4 * RET_WIDTH:]
        u_ref[...] = u
        for part in range(t // sub):
            rows = slice(part * sub, (part + 1) * sub)
            start_ref[part, 0:1, :] = carry_re[...]
            start_ref[part, 1:2, :] = carry_im[...]
            uv = u[rows]
            _s5_channels_to_states(_to_scan_order(p_ref, uv.astype(BF16)), b_ref, xs)
            _s5_scan_tile(xs, lr_ref, li_ref, pr_ref, pi_ref, carry_re, carry_im, conj=False, reverse=False)
            _s5_pack(xs, xb)
            y_ref[rows, :] = _from_scan_order(pt_ref, _s5_states_to_channels(xb, cm_ref), pieces=2) + d_ref[...] * uv

    half = lambda dt: jax.ShapeDtypeStruct((seq, RET_WIDTH), dt)
    return _call(
        body, "inproj_s5_fwd", (seq // t,),
        [_rows(t, D_MODEL), _full((1, D_MODEL)), _full((D_MODEL, IN_COLS)), _rows(t, HEAD_DIM), _rows(t, HEAD_DIM),
         _full((sub, sub)), _full((sub, sub)), _full(S5_TILES_SHAPE), _full(S5_TILES_SHAPE),
         _full((SUBLANES, N_STATES)), _full((SUBLANES, N_STATES)), _full((SUBLANES * S5_STEPS, N_STATES)),
         _full((SUBLANES * S5_STEPS, N_STATES)), _full((1, SSM_WIDTH))],
        [_rows(t, RET_WIDTH)] * 6 + [pl.BlockSpec((t // sub, 2, N_STATES), lambda i: (i, 0, 0))],
        [half(BF16), half(BF16), half(BF16), half(F32), half(F32), half(F32),
         jax.ShapeDtypeStruct((seq // sub, 2, N_STATES), F32)],
        [pltpu.VMEM((2 * S5_COLBLK, sub, LANES), F32), pltpu.VMEM((sub, 2 * N_STATES), BF16),
         pltpu.VMEM((1, N_STATES), F32), pltpu.VMEM((1, N_STATES), F32)],
        (x, gain, w_in, cos2, sin2, p, pt, b_blk, c_blk, lr8, li8, p_re, p_im, d_skip), exchange)


def _s5_bwd(u, dy, b_blk, c_blk, tables, d_skip, starts, exchange=None):
    seq = u.shape[0]
    t = S5_TILE
    nt = seq // t
    lr8, li8, p_re, p_im = tables
    p, pt = _scan_order()

    def body(u_ref, dy_ref, p_ref, pt_ref, b_ref, c_ref, lr_ref, li_ref, pr_ref, pi_ref, d_ref, start_ref,
             du_ref, dd_ref, dlam_ref, db_ref, dc_ref,
             xs, as_, xb, ab, carry_re, carry_im, acar_re, acar_im):
        @pl.when(pl.program_id(0) == 0)
        def _():
            acar_re[...] = jnp.zeros_like(acar_re)
            acar_im[...] = jnp.zeros_like(acar_im)
            dd_ref[...] = jnp.zeros_like(dd_ref)
            dlam_ref[...] = jnp.zeros_like(dlam_ref)
            db_ref[...] = jnp.zeros_like(db_ref)
            dc_ref[...] = jnp.zeros_like(dc_ref)

        uv = u_ref[...]
        dyv = dy_ref[...]
        ub = _to_scan_order(p_ref, uv.astype(BF16))
        dyb = _to_scan_order(p_ref, dyv.astype(BF16))
        carry_re[...] = start_ref[0, 0:1, :]
        carry_im[...] = start_ref[0, 1:2, :]
        _s5_channels_to_states(ub, b_ref, xs)
        in_re, in_im, _, _ = _s5_scan_tile(xs, lr_ref, li_ref, pr_ref, pi_ref, carry_re, carry_im, conj=False, reverse=False)
        _s5_channels_to_states(dyb, c_ref, as_)
        _, _, dl_re, dl_im = _s5_scan_tile(as_, lr_ref, li_ref, pr_ref, pi_ref, acar_re, acar_im, conj=True, reverse=True,
                                           prev_of=(xs, in_re, in_im))
        for b in range(S5_COLBLK):
            cols = slice(b * LANES, (b + 1) * LANES)
            dlam_ref[0, :, cols] += dl_re[b]
            dlam_ref[1, :, cols] += dl_im[b]
        _s5_pack(xs, xb)
        _s5_pack(as_, ab)
        du_ref[...] = (_from_scan_order(pt_ref, _s5_states_to_channels(ab, b_ref), pieces=2) + d_ref[...] * dyv).astype(BF16)
        dd_ref[...] += jnp.sum(dyv * uv, axis=0, keepdims=True)
        _s5_outer_acc(db_ref, ub, ab)
        _s5_outer_acc(dc_ref, dyb, xb)

    rev = _rows_rev(t, SSM_WIDTH, nt)
    vec = lambda: pltpu.VMEM((1, N_STATES), F32)
    outer = S5_TILES_SHAPE
    return _call(
        body, "s5_bwd", (nt,),
        [rev, rev, _full((t, t)), _full((t, t)), _full(S5_TILES_SHAPE), _full(S5_TILES_SHAPE),
         _full((SUBLANES, N_STATES)), _full((SUBLANES, N_STATES)), _full((SUBLANES * S5_STEPS, N_STATES)),
         _full((SUBLANES * S5_STEPS, N_STATES)), _full((1, SSM_WIDTH)),
         pl.BlockSpec((1, 2, N_STATES), lambda i: (nt - 1 - i, 0, 0))],
        [rev, _full((1, SSM_WIDTH)), _full((2, SUBLANES, N_STATES)), _full(outer), _full(outer)],
        [jax.ShapeDtypeStruct((seq, SSM_WIDTH), BF16), jax.ShapeDtypeStruct((1, SSM_WIDTH), F32),
         jax.ShapeDtypeStruct((2, SUBLANES, N_STATES), F32), jax.ShapeDtypeStruct(outer, F32), jax.ShapeDtypeStruct(outer, F32)],
        [pltpu.VMEM((2 * S5_COLBLK, t, LANES), F32), pltpu.VMEM((2 * S5_COLBLK, t, LANES), F32),
         pltpu.VMEM((t, 2 * N_STATES), BF16), pltpu.VMEM((t, 2 * N_STATES), BF16), vec(), vec(), vec(), vec()],
        (u, dy, p, pt, b_blk, c_blk, lr8, li8, p_re, p_im, d_skip, starts), exchange)


def _mix_fwd(y_ssm, y_ret, x, gain, w_glu, w_out, exchange=None):
    seq = x.shape[0]
    t = TOK_TILE

    def body(ys_ref, yr_ref, x_ref, g_ref, wg_ref, wo_ref, x1_ref, mix_ref):
        g0 = _gelu(ys_ref[...]).astype(BF16)
        z = _dot(g0, wg_ref[...])
        glu = (z[:, :SSM_WIDTH] * _sigmoid(z[:, SSM_WIDTH:])).astype(BF16)
        mix = _dot(yr_ref[...], wo_ref[:RET_WIDTH, :]) + _dot(glu, wo_ref[RET_WIDTH:, :])
        mix_ref[...] = mix.astype(BF16)
        n, _ = _rms(mix)
        x1_ref[...] = x_ref[...] + n * g_ref[...]

    act = jax.ShapeDtypeStruct((seq, D_MODEL), F32)
    return _call(
        body, "mix_fwd", (seq // t,),
        [_rows(t, SSM_WIDTH), _rows(t, RET_WIDTH), _rows(t, D_MODEL), _full((1, D_MODEL)),
         _full((SSM_WIDTH, 2 * SSM_WIDTH)), _full((D_MODEL, D_MODEL))],
        [_rows(t, D_MODEL)] * 2, [act, jax.ShapeDtypeStruct((seq, D_MODEL), BF16)], [],
        (y_ssm, y_ret, x, gain, w_glu, w_out), exchange)


def _mix_bwd(dx1, mix, gain, y_ssm, y_ret, w_glu, w_out, exchange=None):
    seq = dx1.shape[0]
    t = TOK_TILE

    def body(dx1_ref, mix_ref, g_ref, ys_ref, yr_ref, wg_ref, wo_ref, dyr_ref, dys_ref, dwo_ref, dwg_ref, dg_ref):
        @pl.when(pl.program_id(0) == 0)
        def _():
            dwo_ref[...] = jnp.zeros_like(dwo_ref)
            dwg_ref[...] = jnp.zeros_like(dwg_ref)
            dg_ref[...] = jnp.zeros_like(dg_ref)

        n, r = _rms(mix_ref[...].astype(F32))
        dmix, dgs = _rms_bwd(n, r, g_ref[...], dx1_ref[...])
        dg_ref[...] += dgs
        dmb = dmix.astype(BF16)
        dcat = _dot_nt(dmb, wo_ref[...])
        dyr_ref[...] = dcat[:, :RET_WIDTH]
        dglu = dcat[:, RET_WIDTH:]
        ys = ys_ref[...]
        g0 = _gelu(ys).astype(BF16)
        z = _dot(g0, wg_ref[...])
        a = z[:, :SSM_WIDTH]
        sb = _sigmoid(z[:, SSM_WIDTH:])
        dwo_ref[:RET_WIDTH, :] += _dot_tn(yr_ref[...], dmb)
        dwo_ref[RET_WIDTH:, :] += _dot_tn((a * sb).astype(BF16), dmb)
        dz = jnp.concatenate([dglu * sb, dglu * a * sb * (1.0 - sb)], axis=1).astype(BF16)
        dwg_ref[...] += _dot_tn(g0, dz)
        dys_ref[...] = _dot_nt(dz, wg_ref[...]) * _gelu_grad(ys)

    half = jax.ShapeDtypeStruct((seq, RET_WIDTH), F32)
    return _call(
        body, "mix_bwd", (seq // t,),
        [_rows(t, D_MODEL), _rows(t, D_MODEL), _full((1, D_MODEL)), _rows(t, SSM_WIDTH), _rows(t, RET_WIDTH),
         _full((SSM_WIDTH, 2 * SSM_WIDTH)), _full((D_MODEL, D_MODEL))],
        [_rows(t, RET_WIDTH), _rows(t, SSM_WIDTH), _full((D_MODEL, D_MODEL)), _full((SSM_WIDTH, 2 * SSM_WIDTH)),
         _full((1, D_MODEL))],
        [half, half, jax.ShapeDtypeStruct((D_MODEL, D_MODEL), F32), jax.ShapeDtypeStruct((SSM_WIDTH, 2 * SSM_WIDTH), F32),
         jax.ShapeDtypeStruct((1, D_MODEL), F32)],
        [], (dx1, mix, gain, y_ssm, y_ret, w_glu, w_out), exchange)


def _mlp_a(x1, target, gain_pre, gain_post, w1, w2):
    seq = x1.shape[0]
    t = MLP_TILE

    def body(x1_ref, tg_ref, gp_ref, gq_ref, w1_ref, w2_ref, df_ref, dx2_ref, dw2_ref, dgq_ref, sq_ref):
        @pl.when(pl.program_id(0) == 0)
        def _():
            dw2_ref[...] = jnp.zeros_like(dw2_ref)
            dgq_ref[...] = jnp.zeros_like(dgq_ref)
            sq_ref[...] = jnp.zeros_like(sq_ref)

        x1v = x1_ref[...]
        n3, _ = _rms(x1v)
        h = (n3 * gp_ref[...]).astype(BF16)
        rl = jnp.maximum(_dot(h, w1_ref[...]), 0.0)
        act = (rl * rl).astype(BF16)
        n4, r4 = _rms(_dot(act, w2_ref[...]))
        gq = gq_ref[...]
        err = x1v + n4 * gq - tg_ref[...]
        sq_ref[...] += jnp.sum(err * err, axis=0, keepdims=True)
        dx2 = err * (1.0 / D_MODEL)
        dx2_ref[...] = dx2
        dm, dgs = _rms_bwd(n4, r4, gq, dx2)
        dgq_ref[...] += dgs
        dmb = dm.astype(BF16)
        dw2_ref[...] += _dot_tn(act, dmb)
        df_ref[...] = (_dot_nt(dmb, w2_ref[...]) * (2.0 * rl)).astype(BF16)

    return pl.pallas_call(
        body, name="mlp_a", grid=(seq // t,),
        in_specs=[_rows(t, D_MODEL), _rows(t, D_MODEL), _full((1, D_MODEL)), _full((1, D_MODEL)),
                  _full((D_MODEL, D_FF)), _full((D_FF, D_MODEL))],
        out_specs=[_rows(t, D_FF), _rows(t, D_MODEL), _full((D_FF, D_MODEL)), _full((1, D_MODEL)), _full((1, D_MODEL))],
        out_shape=[jax.ShapeDtypeStruct((seq, D_FF), BF16), jax.ShapeDtypeStruct((seq, D_MODEL), F32),
                   jax.ShapeDtypeStruct((D_FF, D_MODEL), F32), jax.ShapeDtypeStruct((1, D_MODEL), F32),
                   jax.ShapeDtypeStruct((1, D_MODEL), F32)],
        compiler_params=_params("arbitrary"),
    )(x1, target, gain_pre, gain_post, w1, w2)


def _mlp_b(df, dx2, x1, gain_pre, w1):
    seq = x1.shape[0]
    t = TOK_TILE

    def body(df_ref, dx2_ref, x1_ref, gp_ref, w1_ref, dx1_ref, dw1_ref, dgp_ref):
        @pl.when(pl.program_id(0) == 0)
        def _():
            dw1_ref[...] = jnp.zeros_like(dw1_ref)
            dgp_ref[...] = jnp.zeros_like(dgp_ref)

        n3, r3 = _rms(x1_ref[...])
        gp = gp_ref[...]
        h = (n3 * gp).astype(BF16)
        dfv = df_ref[...]
        dw1_ref[...] += _dot_tn(h, dfv)
        dx, dgs = _rms_bwd(n3, r3, gp, _dot_nt(dfv, w1_ref[...]))
        dgp_ref[...] += dgs
        dx1_ref[...] = dx2_ref[...] + dx

    return pl.pallas_call(
        body, name="mlp_b", grid=(seq // t,),
        in_specs=[_rows(t, D_FF), _rows(t, D_MODEL), _rows(t, D_MODEL), _full((1, D_MODEL)), _full((D_MODEL, D_FF))],
        out_specs=[_rows(t, D_MODEL), _full((D_MODEL, D_FF)), _full((1, D_MODEL))],
        out_shape=[jax.ShapeDtypeStruct((seq, D_MODEL), F32), jax.ShapeDtypeStruct((D_MODEL, D_FF), F32),
                   jax.ShapeDtypeStruct((1, D_MODEL), F32)],
        compiler_params=_params("arbitrary"),
    )(df, dx2, x1, gain_pre, w1)


def _local_step(x, target, small, weights, ids=None, late_state=None):
    cos2, sin2 = _rope_tables(x.shape[0])
    ret_consts = _ret_consts()

    s5_names = ("ssm_lambda_re", "ssm_lambda_im", "ssm_log_dt", "ssm_b_re", "ssm_b_im", "ssm_c_re", "ssm_c_im")
    (lbr, lbi, b_tiles, c_tiles), disc_vjp = jax.vjp(_s5_discretise, *[small[name] for name in s5_names])
    tables = _s5_tables(lbr, lbi)
    b_blk = b_tiles.astype(BF16)
    c_blk = c_tiles.astype(BF16)
    d_skip = small["ssm_d"].reshape(1, SSM_WIDTH)
    gn_gain = small["ret_gn_gain"].reshape(1, RET_WIDTH)
    g_mix_pre = small["norm_mix_pre"].reshape(1, D_MODEL)
    g_mix_post = small["norm_mix_post"].reshape(1, D_MODEL)
    g_mlp_pre = small["norm_mlp_pre"].reshape(1, D_MODEL)
    g_mlp_post = small["norm_mlp_post"].reshape(1, D_MODEL)

    dist = ids is not None
    w_in, w_glu, w_out, w_ff1, w_ff2 = weights
    (q, k, v, gate, u, y_ssm, starts), got = _inproj_s5_fwd(
        x, g_mix_pre, w_in, cos2, sin2, b_blk, c_blk, tables, d_skip,
        _x_gather_send(MID + LATE, [w_glu, w_out, w_ff1, w_ff2]) if dist else None)
    if dist:
        w_glu, w_out, w_ff1, w_ff2 = got
    (y_ret, rprev), got = _ret_fwd(q, k, v, gate, gn_gain, ret_consts,
                                   _x_gather_pass(MID + LATE, [w_glu, w_out, w_ff1, w_ff2]) if dist else None)
    if dist:
        w_glu, w_out, w_ff1, w_ff2 = got
    (x1, mix), _ = _mix_fwd(y_ssm, y_ret, x, g_mix_post, w_glu, w_out)
    df, dx2, d_ff2, dg_mlp_post, sq = _mlp_a(x1, target, g_mlp_pre, g_mlp_post, w_ff1, w_ff2)
    dx1, d_ff1, dg_mlp_pre = _mlp_b(df, dx2, x1, g_mlp_pre, w_ff1)
    (dy_ret, dy_ssm, d_out, d_glu, dg_mix_post), got = _mix_bwd(dx1, mix, g_mix_post, y_ssm, y_ret, w_glu, w_out,
                                                               _x_pair(LATE, [d_ff1, d_ff2]) if dist else None)
    if dist:
        sums = _pair_sum(ids, LATE, [d_ff1, d_ff2], got)
    (du, dd, dlam8, db_tiles, dc_tiles), got = _s5_bwd(
        u, dy_ssm, b_blk, c_blk, tables, d_skip, starts,
        _x_join(_x_chip(LATE, sums), _x_pair(MID, [d_glu, d_out])) if dist else None)
    if dist:
        parts = _chip_sum(ids, LATE, sums, got[:len(LATE)])
        sums = _pair_sum(ids, MID, [d_glu, d_out], got[len(LATE):])
    (dq, dk, dv, dgate, dgn), got = _ret_bwd(q, k, v, gate, gn_gain, ret_consts, rprev, dy_ret, cos2, sin2,
                                             _x_join(_x_share(LATE, parts), _x_chip(MID, sums)) if dist else None)
    if dist:
        late_grads = got[:len(LATE)]
        parts = _chip_sum(ids, MID, sums, got[len(LATE):])
        mid_parts = parts
    adam = (late_state[0], late_grads, late_state[1], late_state[2]) if dist else ()
    (dx, d_in, dg_mix_pre), late_stepped = _inproj_bwd(x, g_mix_pre, w_in, [dq, dk, dv, dgate, du], dx1, adam)

    dlam = jnp.sum(dlam8, axis=1)
    s5_grads = disc_vjp((dlam[0].reshape(SSM_GROUPS, SSM_STATE), dlam[1].reshape(SSM_GROUPS, SSM_STATE), db_tiles, dc_tiles))
    small_grads = {
        "norm_mix_pre": dg_mix_pre, "norm_mix_post": dg_mix_post, "ret_gn_gain": dgn, "ssm_d": dd,
        "norm_mlp_pre": dg_mlp_pre, "norm_mlp_post": dg_mlp_post, "loss_sum": 0.5 * jnp.sum(sq) / D_MODEL,
    }
    small_grads.update(dict(zip(s5_names, s5_grads)))
    if dist:
        return dx, d_in, mid_parts, late_grads, late_stepped, small_grads
    return dx, (d_in, d_glu, d_out, d_ff1, d_ff2), small_grads


BIG = (
    ("w_in", D_MODEL, IN_COLS, 1),
    ("w_glu", SSM_WIDTH, 2 * SSM_WIDTH, 1),
    ("w_out", D_MODEL, D_MODEL, 0),
    ("w_ff1", D_MODEL, D_FF, 1),
    ("w_ff2", D_FF, D_MODEL, 0),
)
SMALL = (
    ("norm_mix_pre", (D_MODEL,)), ("norm_mix_post", (D_MODEL,)), ("ret_gn_gain", (RET_WIDTH,)),
    ("ssm_lambda_re", (SSM_GROUPS, SSM_STATE)), ("ssm_lambda_im", (SSM_GROUPS, SSM_STATE)), ("ssm_log_dt", (SSM_GROUPS,)),
    ("ssm_b_re", (SSM_GROUPS, SSM_STATE, SSM_GROUP)), ("ssm_b_im", (SSM_GROUPS, SSM_STATE, SSM_GROUP)),
    ("ssm_c_re", (SSM_GROUPS, SSM_GROUP, SSM_STATE)), ("ssm_c_im", (SSM_GROUPS, SSM_GROUP, SSM_STATE)),
    ("ssm_d", (SSM_WIDTH,)), ("norm_mlp_pre", (D_MODEL,)), ("norm_mlp_post", (D_MODEL,)),
)
SMALL_ROWS = 1152
EXCH_TILES = 8


PACKED = SMALL + (("loss_sum", (1,)),)


def _packed_rows(shape):
    return -(-math.prod(shape) // (SUBLANES * LANES)) * SUBLANES


def _pack_small(tree):
    rows = []
    for name, shape in PACKED:
        size = math.prod(shape)
        flat = tree[name].reshape(-1).astype(F32) if name in tree else jnp.zeros((size,), F32)
        rows.append(jnp.pad(flat, (0, _packed_rows(shape) * LANES - size)).reshape(-1, LANES))
    used = sum(r.shape[0] for r in rows)
    rows.append(jnp.zeros((SMALL_ROWS - used, LANES), F32))
    return jnp.concatenate(rows, axis=0)


def _unpack_small(packed):
    out, row = {}, 0
    for name, shape in PACKED:
        size = math.prod(shape)
        out[name] = packed[row:row + _packed_rows(shape)].reshape(-1)[:size].reshape((1,) + shape)
        row += _packed_rows(shape)
    return out


def _half_shape(r, c, axis):
    return (r // 2, c) if axis == 1 else (r, c // 2)


def _region_shape(r, c, axis):
    return (r // 2, c // N_CHIPS) if axis == 1 else (r // N_CHIPS, c // 2)


def _shard_shape(r, c, axis):
    return (r, c // N_CHIPS) if axis == 1 else (r // N_CHIPS, c)


def _ds(start, size):
    return pl.ds(pl.multiple_of(start * size, size), size)


def _region_of_full(ref, r, c, axis, shard, half):
    if axis == 1:
        return ref.at[_ds(half, r // 2), _ds(shard, c // N_CHIPS)]
    return ref.at[_ds(shard, r // N_CHIPS), _ds(half, c // 2)]


def _half_of_full(ref, r, c, axis, half):
    if axis == 1:
        return ref.at[_ds(half, r // 2), :]
    return ref.at[:, _ds(half, c // 2)]


def _half_of_shard(ref, r, c, axis, half):
    if axis == 1:
        return ref.at[_ds(half, r // 2), :]
    return ref.at[:, _ds(half, c // 2)]


def _region_of_half(ref, r, c, axis, shard):
    if axis == 1:
        return ref.at[:, _ds(shard, c // N_CHIPS)]
    return ref.at[_ds(shard, r // N_CHIPS), :]


def _place():
    x, y, c = lax.axis_index("x"), lax.axis_index("y"), lax.axis_index("c")
    chips = [(1 - x, y), (x, 1 - y), (1 - x, 1 - y)]
    return x, y, c, chips


LATE = (3, 4)
MID = (1, 2)
FIRST = (0,)
SMALL_HALF = (SMALL_ROWS // 2, LANES)


def _remote(src, dst, send_sem, recv_sem, to):
    return pltpu.make_async_remote_copy(src_ref=src, dst_ref=dst, send_sem=send_sem, recv_sem=recv_sem,
                                        device_id=to, device_id_type=MESH)


def _same(arrays):
    return [jax.ShapeDtypeStruct(a.shape, a.dtype) for a in arrays]


def _x_gather_send(ws, fulls):
    def copies(ins, outs, send_sems, recv_sems):
        x, y, c, chips = _place()
        out = []
        for j, w in enumerate(ws):
            _, r, cc, axis = BIG[w]
            mine = _region_of_full(outs[j], r, cc, axis, 2 * x + y, c)
            out += [_remote(mine, mine, send_sems.at[3 * j + k], recv_sems.at[3 * j + k], (cx, cy, c))
                    for k, (cx, cy) in enumerate(chips)]
        return out

    return _Exchange(fulls, _same(fulls), {j: j for j in range(len(ws))}, 3 * len(ws), copies)


def _x_gather_pass(ws, fulls):
    def copies(ins, outs, send_sems, recv_sems):
        x, y, c, chips = _place()
        out = []
        for j, w in enumerate(ws):
            _, r, cc, axis = BIG[w]
            for k, (cx, cy) in enumerate(chips):
                landed = _region_of_full(outs[j], r, cc, axis, 2 * cx + cy, c)
                out.append(_remote(landed, landed, send_sems.at[3 * j + k], recv_sems.at[3 * j + k], (x, y, 1 - c)))
        return out

    return _Exchange(fulls, _same(fulls), {j: j for j in range(len(ws))}, 3 * len(ws), copies)


def _x_pair(ws, grads, small=None):
    def copies(ins, outs, send_sems, recv_sems):
        x, y, c, _ = _place()
        out = []
        for j, w in enumerate(ws):
            _, r, cc, axis = BIG[w]
            out.append(_remote(_half_of_full(ins[j], r, cc, axis, 1 - c), outs[j], send_sems.at[j], recv_sems.at[j], (x, y, 1 - c)))
        if small is not None:
            j = len(ws)
            out.append(_remote(ins[j].at[_ds(1 - c, SMALL_ROWS // 2), :], outs[j], send_sems.at[j], recv_sems.at[j], (x, y, 1 - c)))
        return out

    shapes = [jax.ShapeDtypeStruct(_half_shape(*BIG[w][1:]), F32) for w in ws]
    extra = [] if small is None else [small]
    return _Exchange(list(grads) + extra, shapes + [jax.ShapeDtypeStruct(SMALL_HALF, F32)] * len(extra), {},
                     len(ws) + len(extra), copies)


def _x_chip(ws, sums_bf16, small_sum=None):
    def copies(ins, outs, send_sems, recv_sems):
        x, y, c, chips = _place()
        out = []
        for j, w in enumerate(ws):
            _, r, cc, axis = BIG[w]
            out += [_remote(_region_of_half(ins[j], r, cc, axis, 2 * cx + cy), outs[j].at[k],
                            send_sems.at[3 * j + k], recv_sems.at[3 * j + k], (cx, cy, c)) for k, (cx, cy) in enumerate(chips)]
        if small_sum is not None:
            j = len(ws)
            out += [_remote(ins[j], outs[j].at[k], send_sems.at[3 * j + k], recv_sems.at[3 * j + k], (cx, cy, c))
                    for k, (cx, cy) in enumerate(chips)]
        return out

    shapes = [jax.ShapeDtypeStruct((3,) + _region_shape(*BIG[w][1:]), BF16) for w in ws]
    extra = [] if small_sum is None else [small_sum]
    return _Exchange(list(sums_bf16) + extra, shapes + [jax.ShapeDtypeStruct((3,) + SMALL_HALF, F32)] * len(extra), {},
                     3 * (len(ws) + len(extra)), copies)


def _x_share(ws, shards, small=None):
    def copies(ins, outs, send_sems, recv_sems):
        x, y, c, _ = _place()
        out = []
        for j, w in enumerate(ws):
            _, r, cc, axis = BIG[w]
            mine = _half_of_shard(outs[j], r, cc, axis, c)
            out.append(_remote(mine, mine, send_sems.at[j], recv_sems.at[j], (x, y, 1 - c)))
        if small is not None:
            j = len(ws)
            mine = outs[j].at[_ds(c, SMALL_ROWS // 2), :]
            out.append(_remote(mine, mine, send_sems.at[j], recv_sems.at[j], (x, y, 1 - c)))
        return out

    arrays = list(shards) + ([] if small is None else [small])
    return _Exchange(arrays, _same(arrays), {j: j for j in range(len(arrays))}, len(arrays), copies)


class _Offset:
    def __init__(self, sems, base):
        self._sems, self._base = sems, base

    @property
    def at(self):
        return self

    def __getitem__(self, i):
        return self._sems.at[self._base + i]


def _x_join(a, b):
    ka_in, ka_out = len(a.operands), len(a.out_shapes)

    def copies(ins, outs, send_sems, recv_sems):
        return (a.copies(ins[:ka_in], outs[:ka_out], send_sems, recv_sems)
                + b.copies(ins[ka_in:], outs[ka_out:], _Offset(send_sems, a.n), _Offset(recv_sems, a.n)))

    aliases = dict(a.aliases)
    aliases.update({ka_in + i: ka_out + o for i, o in b.aliases.items()})
    return _Exchange(a.operands + b.operands, a.out_shapes + b.out_shapes, aliases, a.n + b.n, copies)


def _run_exchange(name, exchange):
    k_in, k_out = len(exchange.operands), len(exchange.out_shapes)

    def body(*refs):
        copies = exchange.copies(refs[:k_in], refs[k_in:k_in + k_out], refs[-2], refs[-1])
        for cp in copies:
            cp.start()
        for cp in copies:
            cp.wait()

    return list(pl.pallas_call(
        body, name=name, in_specs=[ANY] * k_in, out_specs=[ANY] * k_out, out_shape=exchange.out_shapes,
        scratch_shapes=[pltpu.SemaphoreType.DMA((exchange.n,)), pltpu.SemaphoreType.DMA((exchange.n,))],
        input_output_aliases=dict(exchange.aliases),
    )(*exchange.operands))


def _gather_now(name, ws, fulls):
    send, onward = _x_gather_send(ws, fulls), _x_gather_pass(ws, fulls)
    k = len(fulls)

    def body(*refs):
        ins, outs, (send_a, recv_a, send_b, recv_b) = refs[:k], refs[k:2 * k], refs[2 * k:]
        first = send.copies(ins, outs, send_a, recv_a)
        second = onward.copies(ins, outs, send_b, recv_b)
        for cp in first:
            cp.start()
        for landed, cp in zip(first, second):
            landed.wait()
            cp.start()
        for cp in second:
            cp.wait()

    sems = lambda n: pltpu.SemaphoreType.DMA((n,))
    return list(pl.pallas_call(
        body, name=name, in_specs=[ANY] * k, out_specs=[ANY] * k, out_shape=send.out_shapes,
        scratch_shapes=[sems(send.n), sems(send.n), sems(onward.n), sems(onward.n)],
        input_output_aliases={j: j for j in range(k)},
    )(*fulls))


def _tile_rows(shape):
    return (shape[0] // EXCH_TILES, shape[1])


def _cast_place(ids, shards):
    def body(ids_ref, *refs):
        for j in range(len(BIG)):
            refs[len(BIG) + j][...] = refs[j][...].astype(BF16)

    def out_spec(w):
        _, r, cc, axis = BIG[w]
        tile = _tile_rows(_shard_shape(r, cc, axis))
        if axis == 1:
            return pl.BlockSpec(tile, lambda i, ids: (i, ids[1]))
        return pl.BlockSpec(tile, lambda i, ids: (ids[1] * EXCH_TILES + i, 0))

    return list(pl.pallas_call(
        body, name="cast_place",
        grid_spec=pltpu.PrefetchScalarGridSpec(
            num_scalar_prefetch=1, grid=(EXCH_TILES,),
            in_specs=[pl.BlockSpec(_tile_rows(_shard_shape(r, cc, axis)), lambda i, ids: (i, 0)) for _, r, cc, axis in BIG],
            out_specs=[out_spec(w) for w in range(len(BIG))]),
        out_shape=[jax.ShapeDtypeStruct((r, cc), BF16) for _, r, cc, _ in BIG],
        compiler_params=pltpu.CompilerParams(dimension_semantics=("parallel",)),
    )(ids, *shards))


def _pair_sum(ids, ws, grads, got, small=None):
    nb = len(ws)
    n = nb + (small is not None)
    halves = [_half_shape(*BIG[w][1:]) for w in ws] + [SMALL_HALF] * (n - nb)

    def body(ids_ref, *refs):
        for j in range(n):
            total = refs[j][...] + refs[n + j][...]
            refs[2 * n + j][...] = total.astype(refs[2 * n + j].dtype)

    def mine_spec(j):
        tile = _tile_rows(halves[j])
        if j < nb and BIG[ws[j]][3] == 0:
            return pl.BlockSpec(tile, lambda i, ids: (i, ids[0]))
        return pl.BlockSpec(tile, lambda i, ids: (ids[0] * EXCH_TILES + i, 0))

    plain = lambda j: pl.BlockSpec(_tile_rows(halves[j]), lambda i, ids: (i, 0))
    outs = pl.pallas_call(
        body, name="pair_sum",
        grid_spec=pltpu.PrefetchScalarGridSpec(
            num_scalar_prefetch=1, grid=(EXCH_TILES,),
            in_specs=[mine_spec(j) for j in range(n)] + [plain(j) for j in range(n)],
            out_specs=[plain(j) for j in range(n)]),
        out_shape=[jax.ShapeDtypeStruct(h, BF16 if j < nb else F32) for j, h in enumerate(halves)],
        compiler_params=pltpu.CompilerParams(dimension_semantics=("parallel",)),
    )(ids, *grads, *([] if small is None else [small]), *got)
    return list(outs)


def _chip_sum(ids, ws, sums, arrived):
    nb, n = len(ws), len(sums)
    regions = [_region_shape(*BIG[w][1:]) for w in ws] + [SMALL_HALF] * (n - nb)

    def body(ids_ref, *refs):
        for j in range(n):
            own, got, out = refs[j], refs[n + j], refs[2 * n + j]
            if j < nb:
                out[...] = ((own[...].astype(F32) + got[0].astype(F32)) + got[1].astype(F32)) + got[2].astype(F32)
            else:
                out[...] = (own[...] + got[1]) + (got[0] + got[2])

    def own_spec(j):
        tile = _tile_rows(regions[j])
        if j >= nb:
            return pl.BlockSpec(tile, lambda i, ids: (i, 0))
        if BIG[ws[j]][3] == 1:
            return pl.BlockSpec(tile, lambda i, ids: (i, ids[1]))
        return pl.BlockSpec(tile, lambda i, ids: (ids[1] * EXCH_TILES + i, 0))

    def out_spec(j):
        tile = _tile_rows(regions[j])
        if j < nb and BIG[ws[j]][3] == 0:
            return pl.BlockSpec(tile, lambda i, ids: (i, ids[0]))
        return pl.BlockSpec(tile, lambda i, ids: (ids[0] * EXCH_TILES + i, 0))

    got_spec = lambda j: pl.BlockSpec((3,) + _tile_rows(regions[j]), lambda i, ids: (0, i, 0))
    return list(pl.pallas_call(
        body, name="chip_sum",
        grid_spec=pltpu.PrefetchScalarGridSpec(
            num_scalar_prefetch=1, grid=(EXCH_TILES,),
            in_specs=[own_spec(j) for j in range(n)] + [got_spec(j) for j in range(n)],
            out_specs=[out_spec(j) for j in range(n)]),
        out_shape=[jax.ShapeDtypeStruct(_shard_shape(*BIG[w][1:]), F32) for w in ws]
        + [jax.ShapeDtypeStruct((SMALL_ROWS, LANES), F32)] * (n - nb),
        compiler_params=pltpu.CompilerParams(dimension_semantics=("parallel",)),
    )(ids, *sums, *arrived))


def _adamw_body(n):
    c1 = 1.0 - ADAM_B1 ** ADAM_STEP
    c2 = 1.0 - ADAM_B2 ** ADAM_STEP

    def body(*refs):
        for j in range(n):
            w, g, m, v = (refs[k * n + j][...] for k in range(4))
            m = ADAM_B1 * m + (1.0 - ADAM_B1) * g
            v = ADAM_B2 * v + (1.0 - ADAM_B2) * (g * g)
            refs[4 * n + j][...] = -ADAM_LR * ((m / c1) / (jnp.sqrt(v / c2) + ADAM_EPS) + ADAM_WD * w)
            refs[5 * n + j][...] = m
            refs[6 * n + j][...] = v

    return body


def _adamw(name, ws, gs, ms, vs):
    n = len(ws)
    specs = [pl.BlockSpec(_tile_rows(w.shape), lambda i: (i, 0)) for w in ws]
    shapes = [jax.ShapeDtypeStruct(w.shape, F32) for w in ws]
    out, _ = _call(_adamw_body(n), name, (EXCH_TILES,), specs * 4, specs * 3, shapes * 3, [], (*ws, *gs, *ms, *vs))
    return out[:n], out[n:2 * n], out[2 * n:]


def _adamw_whole(name, ws, gs, ms, vs):
    n = len(ws)
    out = pl.pallas_call(_adamw_body(n), name=name, out_shape=[jax.ShapeDtypeStruct(w.shape, F32) for w in ws] * 3)(*ws, *gs, *ms, *vs)
    return out[:n], out[n:2 * n], out[2 * n:]


def kernel(x, norm_mix_pre, norm_mix_post, w_in, ret_gn_gain, ssm_lambda_re, ssm_lambda_im, ssm_log_dt, ssm_b_re, ssm_b_im, ssm_c_re, ssm_c_im, ssm_d, w_glu, w_out, norm_mlp_pre, norm_mlp_post, w_ff1, w_ff2, loss_target, m_norm_mix_pre, m_norm_mix_post, m_w_in, m_ret_gn_gain, m_ssm_lambda_re, m_ssm_lambda_im, m_ssm_log_dt, m_ssm_b_re, m_ssm_b_im, m_ssm_c_re, m_ssm_c_im, m_ssm_d, m_w_glu, m_w_out, m_norm_mlp_pre, m_norm_mlp_post, m_w_ff1, m_w_ff2, v_norm_mix_pre, v_norm_mix_post, v_w_in, v_ret_gn_gain, v_ssm_lambda_re, v_ssm_lambda_im, v_ssm_log_dt, v_ssm_b_re, v_ssm_b_im, v_ssm_c_re, v_ssm_c_im, v_ssm_d, v_w_glu, v_w_out, v_norm_mlp_pre, v_norm_mlp_post, v_w_ff1, v_w_ff2):
    given = dict(locals())
    order = ["norm_mix_pre", "norm_mix_post", "w_in", "ret_gn_gain", "ssm_lambda_re", "ssm_lambda_im", "ssm_log_dt",
             "ssm_b_re", "ssm_b_im", "ssm_c_re", "ssm_c_im", "ssm_d", "w_glu", "w_out", "norm_mlp_pre", "norm_mlp_post",
             "w_ff1", "w_ff2"]
    big_names = [name for name, _, _, _ in BIG]
    ids = jnp.stack([lax.axis_index("c"), 2 * lax.axis_index("x") + lax.axis_index("y")]).astype(jnp.int32)

    weights = _cast_place(ids, [given[name][0] for name in big_names])
    w_in = _gather_now("gather_w_in", FIRST, weights[:1])
    small_w = {name: given[name][0] for name, _ in SMALL}
    late_state = [[given[prefix + big_names[w]][0] for w in LATE] for prefix in ("", "m_", "v_")]
    dx, d_in, mid_parts, late_grads, late_stepped, small_grads = _local_step(
        x[0], loss_target[0], small_w, w_in + weights[1:], ids, late_state)

    small_packed = _pack_small(small_grads)
    got = _run_exchange("pair_exchange", _x_join(_x_share(MID, mid_parts), _x_pair(FIRST, [d_in], small_packed)))
    mid_grads, got = got[:len(MID)], got[len(MID):]
    sums = _pair_sum(ids, FIRST, [d_in], got, small_packed)
    arrived = _run_exchange("chip_exchange", _x_chip(FIRST, sums[:-1], sums[-1]))
    parts = _chip_sum(ids, FIRST, sums, arrived)
    g_in, g_small = _run_exchange("pair_share", _x_share(FIRST, parts[:-1], parts[-1]))
    big_grads = [g_in] + mid_grads + late_grads
    rest = lambda prefix: [given[prefix + big_names[w]][0] for w in FIRST + MID]
    rest_out = _adamw("adamw_big", rest(""), big_grads[:len(FIRST + MID)], rest("m_"), rest("v_"))
    big_out = [list(rest_out[k]) + list(late_stepped[k]) for k in range(3)]
    small_names = [name for name, _ in SMALL]
    small_tree = _unpack_small(g_small)
    small_grads = [small_tree[name] for name in small_names]
    small = lambda prefix: [given[prefix + name] for name in small_names]
    small_out = _adamw_whole("adamw_small", small(""), small_grads, small("m_"), small("v_"))

    def in_order(big_arrays, small_arrays):
        tree = {name: a[None] for name, a in zip(big_names, big_arrays)}
        tree.update(zip(small_names, small_arrays))
        return [tree[name] for name in order]

    loss = small_tree["loss_sum"].reshape(())
    return (loss, dx[None], *in_order(big_grads, small_grads), *in_order(big_out[0], small_out[0]),
            *in_order(big_out[1], small_out[1]), *in_order(big_out[2], small_out[2]))
```

```python
import functools
import math

import jax
import jax.numpy as jnp
import numpy as np
from jax import lax
from jax.experimental import pallas as pl
from jax.experimental.pallas import tpu as pltpu

F32 = jnp.float32
BF16 = jnp.bfloat16

D_MODEL = 1024
RET_WIDTH = 512
RET_HEADS = 4
HEAD_DIM = 128
RET_CHUNK = 128
ROPE_BASE = 10000.0
SSM_WIDTH = 512
SSM_GROUP = 16
SSM_GROUPS = 32
SSM_STATE = 64
N_STATES = SSM_GROUPS * SSM_STATE
D_FF = 4096
IN_COLS = 4 * RET_WIDTH + SSM_WIDTH
NORM_EPS = 1e-6
K_SCALE = HEAD_DIM ** -0.5

ADAM_LR = 0.001
ADAM_B1 = 0.9
ADAM_B2 = 0.999
ADAM_EPS = 1e-08
ADAM_WD = 0.01
ADAM_STEP = 10

LANES = 128
SUBLANES = 8
VMEM_LIMIT = 56 * 2 ** 20

TOK_TILE = 512
MLP_TILE = 256
RET_CHUNKS_PER_STEP = 4
S5_TILE = 256
S5_STEPS = S5_TILE // SUBLANES
S5_COLBLK = N_STATES // LANES
S5_GROUP = 4

N_CHIPS = 4
MESH = pl.DeviceIdType.MESH


def _dot(a, b):
    return jnp.dot(a, b, preferred_element_type=F32)


def _dot_nt(a, b):
    return lax.dot_general(a, b, (((1,), (1,)), ((), ())), preferred_element_type=F32)


def _dot_tn(a, b):
    return lax.dot_general(a, b, (((0,), (0,)), ((), ())), preferred_element_type=F32)


def _sigmoid(x):
    return 1.0 / (1.0 + jnp.exp(-x))


_GELU_C = math.sqrt(2.0 / math.pi)


def _gelu(x):
    return 0.5 * x * (1.0 + jnp.tanh(_GELU_C * (x + 0.044715 * (x * x * x))))


def _gelu_grad(x):
    t = jnp.tanh(_GELU_C * (x + 0.044715 * (x * x * x)))
    return 0.5 * (1.0 + t) + 0.5 * x * (1.0 - t * t) * (_GELU_C * (1.0 + 3.0 * 0.044715 * (x * x)))


def _rms(x):
    r = lax.rsqrt(jnp.mean(x * x, axis=-1, keepdims=True) + NORM_EPS)
    return x * r, r


def _rms_bwd(n, r, gain, dy):
    dn = dy * gain
    dx = r * (dn - n * jnp.mean(dn * n, axis=-1, keepdims=True))
    return dx, jnp.sum(dy * n, axis=0, keepdims=True)


def _full(shape):
    nd = len(shape)
    return pl.BlockSpec(shape, lambda i, _n=nd: (0,) * _n, pipeline_mode=pl.Buffered(1))


def _rows(tile, width):
    return pl.BlockSpec((tile, width), lambda i: (i, 0))


def _rows_rev(tile, width, n):
    return pl.BlockSpec((tile, width), lambda i, _n=n: (_n - 1 - i, 0))


def _params(sem):
    return pltpu.CompilerParams(dimension_semantics=(sem,), vmem_limit_bytes=VMEM_LIMIT)


ANY = pl.BlockSpec(memory_space=pl.ANY)


class _Exchange:
    def __init__(self, operands, out_shapes, aliases, n, copies):
        self.operands, self.out_shapes, self.aliases, self.n, self.copies = list(operands), list(out_shapes), aliases, n, copies


def _call(body, name, grid, in_specs, out_specs, out_shape, scratch_shapes, args, exchange=None):
    if exchange is None:
        outs = pl.pallas_call(body, name=name, grid=grid, in_specs=in_specs, out_specs=out_specs, out_shape=out_shape,
                              scratch_shapes=scratch_shapes, compiler_params=_params("arbitrary"))(*args)
        return list(outs), []
    n_in, n_out, n_scr = len(in_specs), len(out_specs), len(scratch_shapes)
    k_in, k_out = len(exchange.operands), len(exchange.out_shapes)
    last = grid[0] - 1

    def hosted(*refs):
        own_in, rest = refs[:n_in], refs[n_in:]
        ex_in, rest = rest[:k_in], rest[k_in:]
        own_out, rest = rest[:n_out], rest[n_out:]
        ex_out, rest = rest[:k_out], rest[k_out:]
        own_scr, (send_sems, recv_sems) = rest[:n_scr], rest[n_scr:]

        @pl.when(pl.program_id(0) == 0)
        def _():
            for cp in exchange.copies(ex_in, ex_out, send_sems, recv_sems):
                cp.start()

        body(*own_in, *own_out, *own_scr)

        @pl.when(pl.program_id(0) == last)
        def _():
            for cp in exchange.copies(ex_in, ex_out, send_sems, recv_sems):
                cp.wait()

    outs = pl.pallas_call(
        hosted, name=name, grid=grid, in_specs=list(in_specs) + [ANY] * k_in, out_specs=list(out_specs) + [ANY] * k_out,
        out_shape=list(out_shape) + exchange.out_shapes,
        scratch_shapes=list(scratch_shapes) + [pltpu.SemaphoreType.DMA((exchange.n,)), pltpu.SemaphoreType.DMA((exchange.n,))],
        input_output_aliases={n_in + i: n_out + o for i, o in exchange.aliases.items()},
        compiler_params=_params("arbitrary"),
    )(*args, *exchange.operands)
    return list(outs[:n_out]), list(outs[n_out:])


def _rope(t, cos2, sin2):
    return t * cos2 + pltpu.roll(t, HEAD_DIM // 2, 1) * sin2


def _rope_bwd(d, cos2, sin2):
    return d * cos2 + pltpu.roll(d * sin2, HEAD_DIM // 2, d.ndim - 1)


def _inproj_fwd(x, gain, w_in, cos2, sin2, exchange=None):
    seq = x.shape[0]

    def body(x_ref, g_ref, w_ref, c_ref, s_ref, q_ref, k_ref, v_ref, gate_ref, u_ref):
        n, _ = _rms(x_ref[...])
        h = (n * g_ref[...]).astype(BF16)
        proj = _dot(h, w_ref[...])
        c = c_ref[...]
        s = s_ref[...]
        for hd in range(RET_HEADS):
            lo = hd * HEAD_DIM
            q_ref[:, lo:lo + HEAD_DIM] = _rope(proj[:, lo:lo + HEAD_DIM], c, s).astype(BF16)
            kh = proj[:, RET_WIDTH + lo:RET_WIDTH + lo + HEAD_DIM]
            k_ref[:, lo:lo + HEAD_DIM] = (_rope(kh, c, s) * K_SCALE).astype(BF16)
        v_ref[...] = proj[:, 2 * RET_WIDTH:3 * RET_WIDTH].astype(BF16)
        gate_ref[...] = proj[:, 3 * RET_WIDTH:4 * RET_WIDTH]
        u_ref[...] = proj[:, 4 * RET_WIDTH:]

    t = TOK_TILE
    half = lambda dt: jax.ShapeDtypeStruct((seq, RET_WIDTH), dt)
    return _call(
        body, "inproj_fwd", (seq // t,),
        [_rows(t, D_MODEL), _full((1, D_MODEL)), _full((D_MODEL, IN_COLS)), _rows(t, HEAD_DIM), _rows(t, HEAD_DIM)],
        [_rows(t, RET_WIDTH)] * 5, [half(BF16), half(BF16), half(BF16), half(F32), half(F32)], [],
        (x, gain, w_in, cos2, sin2), exchange)


def _inproj_bwd(x, gain, w_in, dpieces, dres, adam=()):
    seq = x.shape[0]
    t = TOK_TILE
    steps = seq // t
    n_pieces = len(dpieces)
    n_adam = len(adam[0]) if adam else 0
    adam_step = _adamw_body(n_adam)

    def body(x_ref, g_ref, w_ref, *refs):
        piece_refs, (dres_ref, *adam_in), (dx_ref, dw_ref, dg_ref, *adam_out) = (
            refs[:n_pieces], refs[n_pieces:n_pieces + 1 + 4 * n_adam], refs[n_pieces + 1 + 4 * n_adam:])

        @pl.when(pl.program_id(0) == 0)
        def _():
            dw_ref[...] = jnp.zeros_like(dw_ref)
            dg_ref[...] = jnp.zeros_like(dg_ref)

        adam_step(*adam_in, *adam_out)

        n, r = _rms(x_ref[...])
        gain_v = g_ref[...]
        h = (n * gain_v).astype(BF16)
        dh = None
        for j, piece in enumerate(piece_refs):
            cols = slice(j * RET_WIDTH, (j + 1) * RET_WIDTH)
            dp = piece[...]
            dw_ref[:, cols] += _dot_tn(h, dp)
            term = _dot_nt(dp, w_ref[:, cols])
            dh = term if dh is None else dh + term
        dx, dgs = _rms_bwd(n, r, gain_v, dh)
        dg_ref[...] += dgs
        dx_ref[...] = dres_ref[...] + dx

    adam_arrays = [a for group in adam for a in group]
    adam_spec = lambda a: pl.BlockSpec((a.shape[0] // steps, a.shape[1]), lambda i: (i, 0))
    outs, _ = _call(
        body, "inproj_bwd", (steps,),
        [_rows(t, D_MODEL), _full((1, D_MODEL)), _full((D_MODEL, IN_COLS))] + [_rows(t, RET_WIDTH)] * n_pieces + [_rows(t, D_MODEL)]
        + [adam_spec(a) for a in adam_arrays],
        [_rows(t, D_MODEL), _full((D_MODEL, IN_COLS)), _full((1, D_MODEL))] + [adam_spec(a) for a in adam_arrays[:n_adam]] * 3,
        [jax.ShapeDtypeStruct((seq, D_MODEL), F32), jax.ShapeDtypeStruct((D_MODEL, IN_COLS), F32),
         jax.ShapeDtypeStruct((1, D_MODEL), F32)] + [jax.ShapeDtypeStruct(a.shape, F32) for a in adam_arrays[:n_adam]] * 3,
        [], (x, gain, w_in, *dpieces, dres, *adam_arrays))
    stepped = outs[3:]
    return outs[:3], (stepped[:n_adam], stepped[n_adam:2 * n_adam], stepped[2 * n_adam:])


def _ret_consts():
    c = RET_CHUNK
    f32 = np.float32
    log_gamma = np.log(f32(1.0) - np.exp(np.linspace(math.log(1.0 / 32), math.log(1.0 / 512), RET_HEADS, dtype=f32))).astype(f32)
    idx = np.arange(c, dtype=f32)
    diff = idx[:, None] - idx[None, :]
    decay = np.where(diff[None] >= 0, np.exp(np.maximum(diff, f32(0.0))[None] * log_gamma[:, None, None]), f32(0.0))
    zeta = np.exp((c - 1 - idx)[None, :] * log_gamma[:, None])
    xi = np.exp((idx + f32(1.0))[None, :] * log_gamma[:, None])
    g_chunk = np.exp(f32(c) * log_gamma)
    wide = lambda rowvals: jnp.asarray(np.broadcast_to(rowvals[:, :, None], (RET_HEADS, c, c)).astype(f32))
    return (jnp.asarray(decay.astype(f32)), wide(xi), wide(zeta),
            jnp.asarray(np.broadcast_to(g_chunk[:, None, None], (RET_HEADS, c, c)).astype(f32)))


def _rope_tables(seq):
    f32 = np.float32
    half = HEAD_DIM // 2
    inv_freq = np.power(f32(ROPE_BASE), -np.arange(half, dtype=f32) / f32(half)).astype(f32)
    ang = (np.arange(seq, dtype=f32)[:, None] * inv_freq[None, :]).astype(f32)
    cos, sin = np.cos(ang).astype(f32), np.sin(ang).astype(f32)
    return jnp.asarray(np.concatenate([cos, cos], axis=1)), jnp.asarray(np.concatenate([-sin, sin], axis=1))


def _bmm(a, b):
    return lax.dot_general(a, b, (((2,), (1,)), ((0,), (0,))), preferred_element_type=F32)


def _bmm_nt(a, b):
    return lax.dot_general(a, b, (((2,), (2,)), ((0,), (0,))), preferred_element_type=F32)


def _bmm_tn(a, b):
    return lax.dot_general(a, b, (((1,), (1,)), ((0,), (0,))), preferred_element_type=F32)


def _ret_blocks(ref):
    return jnp.stack([ref[cc * RET_CHUNK:(cc + 1) * RET_CHUNK, hd * HEAD_DIM:(hd + 1) * HEAD_DIM]
                      for cc in range(RET_CHUNKS_PER_STEP) for hd in range(RET_HEADS)])


def _ret_unblock(ref, blocks):
    for cc in range(RET_CHUNKS_PER_STEP):
        for hd in range(RET_HEADS):
            ref[cc * RET_CHUNK:(cc + 1) * RET_CHUNK, hd * HEAD_DIM:(hd + 1) * HEAD_DIM] = blocks[cc * RET_HEADS + hd].astype(ref.dtype)


def _rope_blocks(ref):
    return jnp.stack([ref[cc * RET_CHUNK:(cc + 1) * RET_CHUNK, :] for cc in range(RET_CHUNKS_PER_STEP) for _ in range(RET_HEADS)])


def _per_chunk(ref):
    return jnp.concatenate([ref[...]] * RET_CHUNKS_PER_STEP, axis=0)


def _ret_gain(gain_ref):
    return jnp.stack([gain_ref[:, hd * HEAD_DIM:(hd + 1) * HEAD_DIM]
                      for _ in range(RET_CHUNKS_PER_STEP) for hd in range(RET_HEADS)])


def _ret_fwd_step(q_ref, k_ref, v_ref, gate_ref, gain_ref, dec_ref, xi_ref, zeta_ref, gch_ref, y_ref, rprev_ref, state):
    cps = RET_CHUNKS_PER_STEP
    qb, kb, vb = _ret_blocks(q_ref), _ret_blocks(k_ref), _ret_blocks(v_ref)
    s = _bmm_nt(qb, kb) * _per_chunk(dec_ref)
    vz = (vb.astype(F32) * _per_chunk(zeta_ref)).astype(BF16)
    incr = _bmm_tn(kb, vz)
    gch_v = gch_ref[...]
    rp = state[...]
    before = []
    for cc in range(cps):
        before.append(rp.astype(BF16))
        rprev_ref[cc] = before[-1]
        rp = gch_v * rp + incr[cc * RET_HEADS:(cc + 1) * RET_HEADS]
    state[...] = rp
    qx = (qb.astype(F32) * _per_chunk(xi_ref)).astype(BF16)
    o = _bmm(s.astype(BF16), vb) + _bmm(qx, jnp.concatenate(before, axis=0))
    oc = o - jnp.mean(o, axis=-1, keepdims=True)
    on = oc * lax.rsqrt(jnp.mean(oc * oc, axis=-1, keepdims=True) + NORM_EPS)
    g = _ret_blocks(gate_ref)
    _ret_unblock(y_ref, g * _sigmoid(g) * (on * _ret_gain(gain_ref)))


def _ret_fwd(q, k, v, gate, gn_gain, consts, exchange=None):
    seq = q.shape[0]
    cps = RET_CHUNKS_PER_STEP
    t = cps * RET_CHUNK
    n_chunks = seq // RET_CHUNK
    dec, xi, zeta, gch = consts

    def body(*refs):
        state = refs[-1]

        @pl.when(pl.program_id(0) == 0)
        def _():
            state[...] = jnp.zeros_like(state)

        _ret_fwd_step(*refs)

    cst = _full((RET_HEADS, RET_CHUNK, RET_CHUNK))
    return _call(
        body, "ret_fwd", (seq // t,),
        [_rows(t, RET_WIDTH)] * 4 + [_full((1, RET_WIDTH)), cst, cst, cst, cst],
        [_rows(t, RET_WIDTH), pl.BlockSpec((cps, RET_HEADS, HEAD_DIM, HEAD_DIM), lambda i: (i, 0, 0, 0))],
        [jax.ShapeDtypeStruct((seq, RET_WIDTH), BF16), jax.ShapeDtypeStruct((n_chunks, RET_HEADS, HEAD_DIM, HEAD_DIM), BF16)],
        [pltpu.VMEM((RET_HEADS, HEAD_DIM, HEAD_DIM), F32)],
        (q, k, v, gate, gn_gain, dec, xi, zeta, gch), exchange)


def _ret_bwd(q, k, v, gate, gn_gain, consts, rprev, dy_ret, cos2, sin2, exchange=None):
    seq = q.shape[0]
    cps = RET_CHUNKS_PER_STEP
    t = cps * RET_CHUNK
    nt = seq // t
    dec, xi, zeta, gch = consts

    def body(q_ref, k_ref, v_ref, gate_ref, gain_ref, dec_ref, xi_ref, zeta_ref, gch_ref, rprev_ref, dyr_ref, cos_ref, sin_ref,
             dq_ref, dk_ref, dv_ref, dgate_ref, dgain_ref, dstate):
        @pl.when(pl.program_id(0) == 0)
        def _():
            dstate[...] = jnp.zeros_like(dstate)
            dgain_ref[...] = jnp.zeros_like(dgain_ref)

        nb = cps * RET_HEADS
        qb, kb, vb = _ret_blocks(q_ref), _ret_blocks(k_ref), _ret_blocks(v_ref)
        dec_b, xi_b, zeta_b = _per_chunk(dec_ref), _per_chunk(xi_ref), _per_chunk(zeta_ref)
        rpb = rprev_ref[...].reshape(nb, HEAD_DIM, HEAD_DIM)
        sdb = (_bmm_nt(qb, kb) * dec_b).astype(BF16)
        qx = (qb.astype(F32) * xi_b).astype(BF16)
        o = _bmm(sdb, vb) + _bmm(qx, rpb)
        oc = o - jnp.mean(o, axis=-1, keepdims=True)
        rstd = lax.rsqrt(jnp.mean(oc * oc, axis=-1, keepdims=True) + NORM_EPS)
        on = oc * rstd
        g = _ret_blocks(gate_ref)
        sg = _sigmoid(g)
        gain_b = _ret_gain(gain_ref)
        dyr = _ret_blocks(dyr_ref)
        _ret_unblock(dgate_ref, dyr * (on * gain_b) * (sg * (1.0 + g * (1.0 - sg))))
        dy = dyr * (g * sg)
        dgain = jnp.sum(dy * on, axis=1, keepdims=True)
        for hd in range(RET_HEADS):
            part = dgain[hd]
            for cc in range(1, cps):
                part = part + dgain[cc * RET_HEADS + hd]
            dgain_ref[:, hd * HEAD_DIM:(hd + 1) * HEAD_DIM] += part
        don = dy * gain_b
        do = rstd * (don - jnp.mean(don, axis=-1, keepdims=True) - on * jnp.mean(don * on, axis=-1, keepdims=True))
        dob = do.astype(BF16)
        ds = (_bmm_nt(dob, vb) * dec_b).astype(BF16)
        dq = _bmm(ds, kb) + _bmm_nt(dob, rpb) * xi_b
        dk = _bmm_tn(ds, qb)
        dv = _bmm_tn(sdb, dob)
        dstate_local = _bmm_tn(qx, dob)
        vz = (vb.astype(F32) * zeta_b).astype(BF16)
        gch_v = gch_ref[...]
        zeta_v = zeta_ref[...]
        gh = dstate[...]
        dk_extra, dv_extra = [None] * cps, [None] * cps
        for cc in reversed(range(cps)):
            sl = slice(cc * RET_HEADS, (cc + 1) * RET_HEADS)
            gb = gh.astype(BF16)
            dk_extra[cc] = _bmm_nt(vz[sl], gb)
            dv_extra[cc] = _bmm(kb[sl], gb) * zeta_v
            gh = dstate_local[sl] + gch_v * gh
        dstate[...] = gh
        cos_b, sin_b = _rope_blocks(cos_ref), _rope_blocks(sin_ref)
        _ret_unblock(dq_ref, _rope_bwd(dq, cos_b, sin_b))
        _ret_unblock(dk_ref, _rope_bwd((dk + jnp.concatenate(dk_extra, axis=0)) * K_SCALE, cos_b, sin_b))
        _ret_unblock(dv_ref, dv + jnp.concatenate(dv_extra, axis=0))

    cst = _full((RET_HEADS, RET_CHUNK, RET_CHUNK))
    rev = _rows_rev(t, RET_WIDTH, nt)
    act = jax.ShapeDtypeStruct((seq, RET_WIDTH), BF16)
    return _call(
        body, "ret_bwd", (nt,),
        [rev] * 4 + [_full((1, RET_WIDTH)), cst, cst, cst, cst,
                     pl.BlockSpec((cps, RET_HEADS, HEAD_DIM, HEAD_DIM), lambda i: (nt - 1 - i, 0, 0, 0)), rev,
                     _rows_rev(t, HEAD_DIM, nt), _rows_rev(t, HEAD_DIM, nt)],
        [rev] * 4 + [_full((1, RET_WIDTH))], [act, act, act, act, jax.ShapeDtypeStruct((1, RET_WIDTH), F32)],
        [pltpu.VMEM((RET_HEADS, HEAD_DIM, HEAD_DIM), F32)],
        (q, k, v, gate, gn_gain, dec, xi, zeta, gch, rprev, dy_ret, cos2, sin2), exchange)


MXU_TILE = 256
S5_STATE_TILES = 2 * N_STATES // MXU_TILE
S5_CHANNELS_PER_TILE = MXU_TILE // SSM_STATE * SSM_GROUP
S5_TILES_SHAPE = (S5_STATE_TILES, LANES, MXU_TILE)


def _chan_block(kk):
    return ((kk % (S5_STATE_TILES // 2)) * S5_CHANNELS_PER_TILE) // LANES * LANES


def _s5_tiles(blocks_re, blocks_im):
    per = MXU_TILE // SSM_STATE
    half = S5_STATE_TILES // 2
    eye = jnp.eye(per, dtype=F32)
    in_half = (jnp.arange(half) % (LANES // S5_CHANNELS_PER_TILE))[:, None] == jnp.arange(LANES // S5_CHANNELS_PER_TILE)[None, :]
    out = []
    for blk in (blocks_re, blocks_im):
        t = blk.reshape(half, per, SSM_GROUP, 1, SSM_STATE) * eye[None, :, None, :, None]
        t = t.reshape(half, 1, S5_CHANNELS_PER_TILE, MXU_TILE)
        out.append(jnp.where(in_half[:, :, None, None], t, 0.0).reshape(half, LANES, MXU_TILE))
    return jnp.concatenate(out, axis=0)


def _s5_discretise(lam_re, lam_im, log_dt, b_re, b_im, c_re, c_im):
    lam = lax.complex(jnp.minimum(lam_re, -1e-4), lam_im)
    dt = jnp.exp(log_dt)[:, None]
    lam_bar = jnp.exp(lam * dt)
    b_bar = ((lam_bar - 1.0) / lam)[:, :, None] * lax.complex(b_re, b_im)
    b_tiles = _s5_tiles(jnp.swapaxes(jnp.real(b_bar), 1, 2), jnp.swapaxes(jnp.imag(b_bar), 1, 2))
    return jnp.real(lam_bar), jnp.imag(lam_bar), b_tiles, _s5_tiles(c_re, -c_im)


def _s5_tables(lbr, lbi):
    lr = lbr.reshape(1, N_STATES)
    li = lbi.reshape(1, N_STATES)
    pr, pi = lr, li
    while pr.shape[0] < S5_STEPS:
        top_r, top_i = pr[-1:], pi[-1:]
        pr, pi = (jnp.concatenate([pr, pr * top_r - pi * top_i], axis=0), jnp.concatenate([pi, pr * top_i + pi * top_r], axis=0))
    bc = lambda a: jnp.broadcast_to(a, (SUBLANES, N_STATES))
    rep = lambda rows: jnp.broadcast_to(rows[:, None, :], (S5_STEPS, SUBLANES, N_STATES)).reshape(S5_STEPS * SUBLANES, N_STATES)
    return bc(lr), bc(li), rep(pr), rep(pi)


def _cmul(ar, ai, br, bi):
    return ar * br - ai * bi, ar * bi + ai * br


def _seg(i):
    return pl.ds(pl.multiple_of(i * SUBLANES, SUBLANES), SUBLANES)


def _scan_order():
    r = np.arange(S5_TILE)
    token = (r % SUBLANES) * S5_STEPS + r // SUBLANES
    p = (token[:, None] == np.arange(S5_TILE)[None, :]).astype(np.float32)
    return jnp.asarray(p, dtype=BF16), jnp.asarray(p.T, dtype=BF16)


def _to_scan_order(p_ref, rows_bf16):
    return _dot(p_ref[...], rows_bf16).astype(BF16)


def _from_scan_order(pt_ref, rows, pieces=3):
    pt = pt_ref[...]
    out = None
    for _ in range(pieces):
        piece = rows.astype(BF16)
        rows = rows - piece.astype(F32)
        term = _dot(pt, piece)
        out = term if out is None else out + term
    return out


def _s5_channels_to_states(chan_b, m_ref, xs):
    for kk in range(S5_STATE_TILES):
        lo = _chan_block(kk)
        res = _dot(chan_b[:, lo:lo + LANES], m_ref[kk])
        xs[2 * kk] = res[:, :LANES]
        xs[2 * kk + 1] = res[:, LANES:]


def _s5_states_to_channels(sb_ref, m_ref):
    out = []
    for lo in range(0, SSM_WIDTH, LANES):
        acc = None
        for kk in range(S5_STATE_TILES):
            if _chan_block(kk) == lo:
                term = _dot_nt(sb_ref[:, kk * MXU_TILE:(kk + 1) * MXU_TILE], m_ref[kk])
                acc = term if acc is None else acc + term
        out.append(acc)
    return jnp.concatenate(out, axis=1)


def _s5_outer_acc(acc_ref, chan_b, states_ref):
    for kk in range(S5_STATE_TILES):
        lo = _chan_block(kk)
        acc_ref[kk] += _dot_tn(chan_b[:, lo:lo + LANES], states_ref[:, kk * MXU_TILE:(kk + 1) * MXU_TILE])


def _s5_scan_tile(xs, lr_ref, li_ref, pr_ref, pi_ref, carry_re, carry_im, conj, reverse, prev_of=None):
    sgn = -1.0 if conj else 1.0
    seg_in_re, seg_in_im, sums_re, sums_im = [], [], [], []
    group = S5_GROUP
    for grp in range(S5_COLBLK // group):
        blks = [grp * group + j for j in range(group)]
        lrs = [lr_ref[:, b * LANES:(b + 1) * LANES] for b in blks]
        lis = [sgn * li_ref[:, b * LANES:(b + 1) * LANES] for b in blks]

        def step(it, carry, blks=blks, lrs=lrs, lis=lis):
            i = (S5_STEPS - 1 - it) if reverse else it
            out_r, out_i = [], []
            for j, b in enumerate(blks):
                xr, xi = _cmul(lrs[j], lis[j], carry[j], carry[group + j])
                xr = xr + xs[b, _seg(i), :]
                xi = xi + xs[S5_COLBLK + b, _seg(i), :]
                xs[b, _seg(i), :] = xr
                xs[S5_COLBLK + b, _seg(i), :] = xi
                out_r.append(xr)
                out_i.append(xi)
            return tuple(out_r + out_i)

        zeros = tuple(jnp.zeros((SUBLANES, LANES), F32) for _ in range(2 * group))
        ends = lax.fori_loop(0, S5_STEPS, step, zeros, unroll=True)

        ins_r, ins_i = [], []
        for j, b in enumerate(blks):
            cols = slice(b * LANES, (b + 1) * LANES)
            pr = pr_ref[SUBLANES * S5_STEPS - 1:SUBLANES * S5_STEPS, cols]
            pi = sgn * pi_ref[SUBLANES * S5_STEPS - 1:SUBLANES * S5_STEPS, cols]
            cur_r = carry_re[:, cols]
            cur_i = carry_im[:, cols]
            row_id = lax.broadcasted_iota(jnp.int32, (SUBLANES, LANES), 0)
            in_r = jnp.zeros((SUBLANES, LANES), F32)
            in_i = jnp.zeros((SUBLANES, LANES), F32)
            order = range(SUBLANES - 1, -1, -1) if reverse else range(SUBLANES)
            for sgm in order:
                in_r = jnp.where(row_id == sgm, cur_r, in_r)
                in_i = jnp.where(row_id == sgm, cur_i, in_i)
                mr, mi = _cmul(pr, pi, cur_r, cur_i)
                cur_r = mr + ends[j][sgm:sgm + 1, :]
                cur_i = mi + ends[group + j][sgm:sgm + 1, :]
            carry_re[:, cols] = cur_r
            carry_im[:, cols] = cur_i
            ins_r.append(in_r)
            ins_i.append(in_i)

        def fix(it, c, blks=blks, ins_r=ins_r, ins_i=ins_i):
            pw = (S5_STEPS - 1 - it) if reverse else it
            out = []
            for j, b in enumerate(blks):
                cols = slice(b * LANES, (b + 1) * LANES)
                prow = pl.ds(pl.multiple_of(pw * SUBLANES, SUBLANES), SUBLANES)
                fr, fi = _cmul(pr_ref[prow, cols], sgn * pi_ref[prow, cols], ins_r[j], ins_i[j])
                vr = xs[b, _seg(it), :] + fr
                vi = xs[S5_COLBLK + b, _seg(it), :] + fi
                xs[b, _seg(it), :] = vr
                xs[S5_COLBLK + b, _seg(it), :] = vi
                if prev_of is not None:
                    opr, opi, sr, si = c[4 * j:4 * j + 4]
                    out += [prev_of[0][b, _seg(it), :], prev_of[0][S5_COLBLK + b, _seg(it), :],
                            sr + vr * opr + vi * opi, si + vi * opr - vr * opi]
            return tuple(out)

        init = ()
        if prev_of is not None:
            zero = jnp.zeros((SUBLANES, LANES), F32)
            for b in blks:
                init += (prev_of[1][b], prev_of[2][b], zero, zero)
        done = lax.fori_loop(0, S5_STEPS, fix, init, unroll=True)
        for j in range(len(blks) if prev_of is not None else 0):
            sums_re.append(done[4 * j + 2])
            sums_im.append(done[4 * j + 3])
        seg_in_re += ins_r
        seg_in_im += ins_i
    return seg_in_re, seg_in_im, sums_re, sums_im


def _s5_pack(xs, dst):
    for b in range(2 * S5_COLBLK):
        dst[:, b * LANES:(b + 1) * LANES] = xs[b].astype(BF16)


def _s5_fwd(u, b_blk, c_blk, tables, d_skip, exchange=None):
    seq = u.shape[0]
    t = S5_TILE
    nt = seq // t
    lr8, li8, p_re, p_im = tables

    p, pt = _scan_order()

    def body(u_ref, p_ref, pt_ref, b_ref, c_ref, lr_ref, li_ref, pr_ref, pi_ref, d_ref, y_ref, start_ref,
             xs, xb, carry_re, carry_im):
        @pl.when(pl.program_id(0) == 0)
        def _():
            carry_re[...] = jnp.zeros_like(carry_re)
            carry_im[...] = jnp.zeros_like(carry_im)

        start_ref[0, 0:1, :] = carry_re[...]
        start_ref[0, 1:2, :] = carry_im[...]
        uv = u_ref[...]
        _s5_channels_to_states(_to_scan_order(p_ref, uv.astype(BF16)), b_ref, xs)
        _s5_scan_tile(xs, lr_ref, li_ref, pr_ref, pi_ref, carry_re, carry_im, conj=False, reverse=False)
        _s5_pack(xs, xb)
        y_ref[...] = _from_scan_order(pt_ref, _s5_states_to_channels(xb, c_ref), pieces=2) + d_ref[...] * uv

    return _call(
        body, "s5_fwd", (nt,),
        [_rows(t, SSM_WIDTH), _full((t, t)), _full((t, t)), _full(S5_TILES_SHAPE), _full(S5_TILES_SHAPE),
         _full((SUBLANES, N_STATES)), _full((SUBLANES, N_STATES)), _full((SUBLANES * S5_STEPS, N_STATES)),
         _full((SUBLANES * S5_STEPS, N_STATES)), _full((1, SSM_WIDTH))],
        [_rows(t, SSM_WIDTH), pl.BlockSpec((1, 2, N_STATES), lambda i: (i, 0, 0))],
        [jax.ShapeDtypeStruct((seq, SSM_WIDTH), F32), jax.ShapeDtypeStruct((nt, 2, N_STATES), F32)],
        [pltpu.VMEM((2 * S5_COLBLK, t, LANES), F32), pltpu.VMEM((t, 2 * N_STATES), BF16),
         pltpu.VMEM((1, N_STATES), F32), pltpu.VMEM((1, N_STATES), F32)],
        (u, p, pt, b_blk, c_blk, lr8, li8, p_re, p_im, d_skip), exchange)


def _inproj_s5_fwd(x, gain, w_in, cos2, sin2, b_blk, c_blk, tables, d_skip, exchange=None):
    seq = x.shape[0]
    t = TOK_TILE
    sub = S5_TILE
    lr8, li8, p_re, p_im = tables
    p, pt = _scan_order()

    def body(x_ref, g_ref, w_ref, c_ref, s_ref, p_ref, pt_ref, b_ref, cm_ref, lr_ref, li_ref, pr_ref, pi_ref, d_ref,
             q_ref, k_ref, v_ref, gate_ref, u_ref, y_ref, start_ref, xs, xb, carry_re, carry_im):
        @pl.when(pl.program_id(0) == 0)
        def _():
            carry_re[...] = jnp.zeros_like(carry_re)
            carry_im[...] = jnp.zeros_like(carry_im)

        n, _ = _rms(x_ref[...])
        h = (n * g_ref[...]).astype(BF16)
        proj = _dot(h, w_ref[...])
        c = c_ref[...]
        s = s_ref[...]
        for hd in range(RET_HEADS):
            lo = hd * HEAD_DIM
            q_ref[:, lo:lo + HEAD_DIM] = _rope(proj[:, lo:lo + HEAD_DIM], c, s).astype(BF16)
            kh = proj[:, RET_WIDTH + lo:RET_WIDTH + lo + HEAD_DIM]
            k_ref[:, lo:lo + HEAD_DIM] = (_rope(kh, c, s) * K_SCALE).astype(BF16)
        v_ref[...] = proj[:, 2 * RET_WIDTH:3 * RET_WIDTH].astype(BF16)
        gate_ref[...] = proj[:, 3 * RET_WIDTH:4 * RET_WIDTH]
        u = proj[:, 4 * RET_WIDTH:]
        u_ref[...] = u
        for part in range(t // sub):
            rows = slice(part * sub, (part + 1) * sub)
            start_ref[part, 0:1, :] = carry_re[...]
            start_ref[part, 1:2, :] = carry_im[...]
            uv = u[rows]
            _s5_channels_to_states(_to_scan_order(p_ref, uv.astype(BF16)), b_ref, xs)
            _s5_scan_tile(xs, lr_ref, li_ref, pr_ref, pi_ref, carry_re, carry_im, conj=False, reverse=False)
            _s5_pack(xs, xb)
            y_ref[rows, :] = _from_scan_order(pt_ref, _s5_states_to_channels(xb, cm_ref), pieces=2) + d_ref[...] * uv

    half = lambda dt: jax.ShapeDtypeStruct((seq, RET_WIDTH), dt)
    return _call(
        body, "inproj_s5_fwd", (seq // t,),
        [_rows(t, D_MODEL), _full((1, D_MODEL)), _full((D_MODEL, IN_COLS)), _rows(t, HEAD_DIM), _rows(t, HEAD_DIM),
         _full((sub, sub)), _full((sub, sub)), _full(S5_TILES_SHAPE), _full(S5_TILES_SHAPE),
         _full((SUBLANES, N_STATES)), _full((SUBLANES, N_STATES)), _full((SUBLANES * S5_STEPS, N_STATES)),
         _full((SUBLANES * S5_STEPS, N_STATES)), _full((1, SSM_WIDTH))],
        [_rows(t, RET_WIDTH)] * 6 + [pl.BlockSpec((t // sub, 2, N_STATES), lambda i: (i, 0, 0))],
        [half(BF16), half(BF16), half(BF16), half(F32), half(F32), half(F32),
         jax.ShapeDtypeStruct((seq // sub, 2, N_STATES), F32)],
        [pltpu.VMEM((2 * S5_COLBLK, sub, LANES), F32), pltpu.VMEM((sub, 2 * N_STATES), BF16),
         pltpu.VMEM((1, N_STATES), F32), pltpu.VMEM((1, N_STATES), F32)],
        (x, gain, w_in, cos2, sin2, p, pt, b_blk, c_blk, lr8, li8, p_re, p_im, d_skip), exchange)


def _s5_bwd(u, dy, b_blk, c_blk, tables, d_skip, starts, exchange=None):
    seq = u.shape[0]
    t = S5_TILE
    nt = seq // t
    lr8, li8, p_re, p_im = tables
    p, pt = _scan_order()

    def body(u_ref, dy_ref, p_ref, pt_ref, b_ref, c_ref, lr_ref, li_ref, pr_ref, pi_ref, d_ref, start_ref,
             du_ref, dd_ref, dlam_ref, db_ref, dc_ref,
             xs, as_, xb, ab, carry_re, carry_im, acar_re, acar_im):
        @pl.when(pl.program_id(0) == 0)
        def _():
            acar_re[...] = jnp.zeros_like(acar_re)
            acar_im[...] = jnp.zeros_like(acar_im)
            dd_ref[...] = jnp.zeros_like(dd_ref)
            dlam_ref[...] = jnp.zeros_like(dlam_ref)
            db_ref[...] = jnp.zeros_like(db_ref)
            dc_ref[...] = jnp.zeros_like(dc_ref)

        uv = u_ref[...]
        dyv = dy_ref[...]
        ub = _to_scan_order(p_ref, uv.astype(BF16))
        dyb = _to_scan_order(p_ref, dyv.astype(BF16))
        carry_re[...] = start_ref[0, 0:1, :]
        carry_im[...] = start_ref[0, 1:2, :]
        _s5_channels_to_states(ub, b_ref, xs)
        in_re, in_im, _, _ = _s5_scan_tile(xs, lr_ref, li_ref, pr_ref, pi_ref, carry_re, carry_im, conj=False, reverse=False)
        _s5_channels_to_states(dyb, c_ref, as_)
        _, _, dl_re, dl_im = _s5_scan_tile(as_, lr_ref, li_ref, pr_ref, pi_ref, acar_re, acar_im, conj=True, reverse=True,
                                           prev_of=(xs, in_re, in_im))
        for b in range(S5_COLBLK):
            cols = slice(b * LANES, (b + 1) * LANES)
            dlam_ref[0, :, cols] += dl_re[b]
            dlam_ref[1, :, cols] += dl_im[b]
        _s5_pack(xs, xb)
        _s5_pack(as_, ab)
        du_ref[...] = (_from_scan_order(pt_ref, _s5_states_to_channels(ab, b_ref), pieces=2) + d_ref[...] * dyv).astype(BF16)
        dd_ref[...] += jnp.sum(dyv * uv, axis=0, keepdims=True)
        _s5_outer_acc(db_ref, ub, ab)
        _s5_outer_acc(dc_ref, dyb, xb)

    rev = _rows_rev(t, SSM_WIDTH, nt)
    vec = lambda: pltpu.VMEM((1, N_STATES), F32)
    outer = S5_TILES_SHAPE
    return _call(
        body, "s5_bwd", (nt,),
        [rev, rev, _full((t, t)), _full((t, t)), _full(S5_TILES_SHAPE), _full(S5_TILES_SHAPE),
         _full((SUBLANES, N_STATES)), _full((SUBLANES, N_STATES)), _full((SUBLANES * S5_STEPS, N_STATES)),
         _full((SUBLANES * S5_STEPS, N_STATES)), _full((1, SSM_WIDTH)),
         pl.BlockSpec((1, 2, N_STATES), lambda i: (nt - 1 - i, 0, 0))],
        [rev, _full((1, SSM_WIDTH)), _full((2, SUBLANES, N_STATES)), _full(outer), _full(outer)],
        [jax.ShapeDtypeStruct((seq, SSM_WIDTH), BF16), jax.ShapeDtypeStruct((1, SSM_WIDTH), F32),
         jax.ShapeDtypeStruct((2, SUBLANES, N_STATES), F32), jax.ShapeDtypeStruct(outer, F32), jax.ShapeDtypeStruct(outer, F32)],
        [pltpu.VMEM((2 * S5_COLBLK, t, LANES), F32), pltpu.VMEM((2 * S5_COLBLK, t, LANES), F32),
         pltpu.VMEM((t, 2 * N_STATES), BF16), pltpu.VMEM((t, 2 * N_STATES), BF16), vec(), vec(), vec(), vec()],
        (u, dy, p, pt, b_blk, c_blk, lr8, li8, p_re, p_im, d_skip, starts), exchange)


def _mix_fwd_step(ys_ref, yr_ref, x_ref, g_ref, wg_ref, wo_ref, x1_ref, mix_ref):
    g0 = _gelu(ys_ref[...]).astype(BF16)
    z = _dot(g0, wg_ref[...])
    glu = (z[:, :SSM_WIDTH] * _sigmoid(z[:, SSM_WIDTH:])).astype(BF16)
    mix = _dot(yr_ref[...], wo_ref[:RET_WIDTH, :]) + _dot(glu, wo_ref[RET_WIDTH:, :])
    mix_ref[...] = mix.astype(BF16)
    n, _ = _rms(mix)
    x1_ref[...] = x_ref[...] + n * g_ref[...]


def _ret_mix_fwd(q, k, v, gate, gn_gain, consts, y_ssm, x, gain, w_glu, w_out, exchange=None):
    seq = x.shape[0]
    cps = RET_CHUNKS_PER_STEP
    t = cps * RET_CHUNK
    dec, xi, zeta, gch = consts

    def body(q_ref, k_ref, v_ref, gate_ref, gain_ref, dec_ref, xi_ref, zeta_ref, gch_ref, ys_ref, x_ref, g_ref, wg_ref, wo_ref,
             y_ref, rprev_ref, x1_ref, mix_ref, state):
        @pl.when(pl.program_id(0) == 0)
        def _():
            state[...] = jnp.zeros_like(state)

        _ret_fwd_step(q_ref, k_ref, v_ref, gate_ref, gain_ref, dec_ref, xi_ref, zeta_ref, gch_ref, y_ref, rprev_ref, state)
        _mix_fwd_step(ys_ref, y_ref, x_ref, g_ref, wg_ref, wo_ref, x1_ref, mix_ref)

    cst = _full((RET_HEADS, RET_CHUNK, RET_CHUNK))
    return _call(
        body, "ret_mix_fwd", (seq // t,),
        [_rows(t, RET_WIDTH)] * 4 + [_full((1, RET_WIDTH)), cst, cst, cst, cst, _rows(t, SSM_WIDTH), _rows(t, D_MODEL),
                                     _full((1, D_MODEL)), _full((SSM_WIDTH, 2 * SSM_WIDTH)), _full((D_MODEL, D_MODEL))],
        [_rows(t, RET_WIDTH), pl.BlockSpec((cps, RET_HEADS, HEAD_DIM, HEAD_DIM), lambda i: (i, 0, 0, 0)),
         _rows(t, D_MODEL), _rows(t, D_MODEL)],
        [jax.ShapeDtypeStruct((seq, RET_WIDTH), BF16),
         jax.ShapeDtypeStruct((seq // RET_CHUNK, RET_HEADS, HEAD_DIM, HEAD_DIM), BF16),
         jax.ShapeDtypeStruct((seq, D_MODEL), F32), jax.ShapeDtypeStruct((seq, D_MODEL), BF16)],
        [pltpu.VMEM((RET_HEADS, HEAD_DIM, HEAD_DIM), F32)],
        (q, k, v, gate, gn_gain, dec, xi, zeta, gch, y_ssm, x, gain, w_glu, w_out), exchange)


def _mix_fwd(y_ssm, y_ret, x, gain, w_glu, w_out, exchange=None):
    seq = x.shape[0]
    t = TOK_TILE

    act = jax.ShapeDtypeStruct((seq, D_MODEL), F32)
    return _call(
        functools.partial(_mix_fwd_step), "mix_fwd", (seq // t,),
        [_rows(t, SSM_WIDTH), _rows(t, RET_WIDTH), _rows(t, D_MODEL), _full((1, D_MODEL)),
         _full((SSM_WIDTH, 2 * SSM_WIDTH)), _full((D_MODEL, D_MODEL))],
        [_rows(t, D_MODEL)] * 2, [act, jax.ShapeDtypeStruct((seq, D_MODEL), BF16)], [],
        (y_ssm, y_ret, x, gain, w_glu, w_out), exchange)


def _mix_bwd(dx1, mix, gain, y_ssm, y_ret, w_glu, w_out, exchange=None):
    seq = dx1.shape[0]
    t = TOK_TILE

    def body(dx1_ref, mix_ref, g_ref, ys_ref, yr_ref, wg_ref, wo_ref, dyr_ref, dys_ref, dwo_ref, dwg_ref, dg_ref):
        @pl.when(pl.program_id(0) == 0)
        def _():
            dwo_ref[...] = jnp.zeros_like(dwo_ref)
            dwg_ref[...] = jnp.zeros_like(dwg_ref)
            dg_ref[...] = jnp.zeros_like(dg_ref)

        n, r = _rms(mix_ref[...].astype(F32))
        dmix, dgs = _rms_bwd(n, r, g_ref[...], dx1_ref[...])
        dg_ref[...] += dgs
        dmb = dmix.astype(BF16)
        dcat = _dot_nt(dmb, wo_ref[...])
        dyr_ref[...] = dcat[:, :RET_WIDTH]
        dglu = dcat[:, RET_WIDTH:]
        ys = ys_ref[...]
        g0 = _gelu(ys).astype(BF16)
        z = _dot(g0, wg_ref[...])
        a = z[:, :SSM_WIDTH]
        sb = _sigmoid(z[:, SSM_WIDTH:])
        dwo_ref[:RET_WIDTH, :] += _dot_tn(yr_ref[...], dmb)
        dwo_ref[RET_WIDTH:, :] += _dot_tn((a * sb).astype(BF16), dmb)
        dz = jnp.concatenate([dglu * sb, dglu * a * sb * (1.0 - sb)], axis=1).astype(BF16)
        dwg_ref[...] += _dot_tn(g0, dz)
        dys_ref[...] = _dot_nt(dz, wg_ref[...]) * _gelu_grad(ys)

    half = jax.ShapeDtypeStruct((seq, RET_WIDTH), F32)
    return _call(
        body, "mix_bwd", (seq // t,),
        [_rows(t, D_MODEL), _rows(t, D_MODEL), _full((1, D_MODEL)), _rows(t, SSM_WIDTH), _rows(t, RET_WIDTH),
         _full((SSM_WIDTH, 2 * SSM_WIDTH)), _full((D_MODEL, D_MODEL))],
        [_rows(t, RET_WIDTH), _rows(t, SSM_WIDTH), _full((D_MODEL, D_MODEL)), _full((SSM_WIDTH, 2 * SSM_WIDTH)),
         _full((1, D_MODEL))],
        [half, half, jax.ShapeDtypeStruct((D_MODEL, D_MODEL), F32), jax.ShapeDtypeStruct((SSM_WIDTH, 2 * SSM_WIDTH), F32),
         jax.ShapeDtypeStruct((1, D_MODEL), F32)],
        [], (dx1, mix, gain, y_ssm, y_ret, w_glu, w_out), exchange)


def _mlp_a(x1, target, gain_pre, gain_post, w1, w2):
    seq = x1.shape[0]
    t = MLP_TILE

    def body(x1_ref, tg_ref, gp_ref, gq_ref, w1_ref, w2_ref, df_ref, dx2_ref, dw2_ref, dgq_ref, sq_ref):
        @pl.when(pl.program_id(0) == 0)
        def _():
            dw2_ref[...] = jnp.zeros_like(dw2_ref)
            dgq_ref[...] = jnp.zeros_like(dgq_ref)
            sq_ref[...] = jnp.zeros_like(sq_ref)

        x1v = x1_ref[...]
        n3, _ = _rms(x1v)
        h = (n3 * gp_ref[...]).astype(BF16)
        rl = jnp.maximum(_dot(h, w1_ref[...]), 0.0)
        act = (rl * rl).astype(BF16)
        n4, r4 = _rms(_dot(act, w2_ref[...]))
        gq = gq_ref[...]
        err = x1v + n4 * gq - tg_ref[...]
        sq_ref[...] += jnp.sum(err * err, axis=0, keepdims=True)
        dx2 = err * (1.0 / D_MODEL)
        dx2_ref[...] = dx2
        dm, dgs = _rms_bwd(n4, r4, gq, dx2)
        dgq_ref[...] += dgs
        dmb = dm.astype(BF16)
        dw2_ref[...] += _dot_tn(act, dmb)
        df_ref[...] = (_dot_nt(dmb, w2_ref[...]) * (2.0 * rl)).astype(BF16)

    return pl.pallas_call(
        body, name="mlp_a", grid=(seq // t,),
        in_specs=[_rows(t, D_MODEL), _rows(t, D_MODEL), _full((1, D_MODEL)), _full((1, D_MODEL)),
                  _full((D_MODEL, D_FF)), _full((D_FF, D_MODEL))],
        out_specs=[_rows(t, D_FF), _rows(t, D_MODEL), _full((D_FF, D_MODEL)), _full((1, D_MODEL)), _full((1, D_MODEL))],
        out_shape=[jax.ShapeDtypeStruct((seq, D_FF), BF16), jax.ShapeDtypeStruct((seq, D_MODEL), F32),
                   jax.ShapeDtypeStruct((D_FF, D_MODEL), F32), jax.ShapeDtypeStruct((1, D_MODEL), F32),
                   jax.ShapeDtypeStruct((1, D_MODEL), F32)],
        compiler_params=_params("arbitrary"),
    )(x1, target, gain_pre, gain_post, w1, w2)


def _mlp_b(df, dx2, x1, gain_pre, w1):
    seq = x1.shape[0]
    t = TOK_TILE

    def body(df_ref, dx2_ref, x1_ref, gp_ref, w1_ref, dx1_ref, dw1_ref, dgp_ref):
        @pl.when(pl.program_id(0) == 0)
        def _():
            dw1_ref[...] = jnp.zeros_like(dw1_ref)
            dgp_ref[...] = jnp.zeros_like(dgp_ref)

        n3, r3 = _rms(x1_ref[...])
        gp = gp_ref[...]
        h = (n3 * gp).astype(BF16)
        dfv = df_ref[...]
        dw1_ref[...] += _dot_tn(h, dfv)
        dx, dgs = _rms_bwd(n3, r3, gp, _dot_nt(dfv, w1_ref[...]))
        dgp_ref[...] += dgs
        dx1_ref[...] = dx2_ref[...] + dx

    return pl.pallas_call(
        body, name="mlp_b", grid=(seq // t,),
        in_specs=[_rows(t, D_FF), _rows(t, D_MODEL), _rows(t, D_MODEL), _full((1, D_MODEL)), _full((D_MODEL, D_FF))],
        out_specs=[_rows(t, D_MODEL), _full((D_MODEL, D_FF)), _full((1, D_MODEL))],
        out_shape=[jax.ShapeDtypeStruct((seq, D_MODEL), F32), jax.ShapeDtypeStruct((D_MODEL, D_FF), F32),
                   jax.ShapeDtypeStruct((1, D_MODEL), F32)],
        compiler_params=_params("arbitrary"),
    )(df, dx2, x1, gain_pre, w1)


def _local_step(x, target, small, weights, ids=None, late_state=None):
    cos2, sin2 = _rope_tables(x.shape[0])
    ret_consts = _ret_consts()

    s5_names = ("ssm_lambda_re", "ssm_lambda_im", "ssm_log_dt", "ssm_b_re", "ssm_b_im", "ssm_c_re", "ssm_c_im")
    (lbr, lbi, b_tiles, c_tiles), disc_vjp = jax.vjp(_s5_discretise, *[small[name] for name in s5_names])
    tables = _s5_tables(lbr, lbi)
    b_blk = b_tiles.astype(BF16)
    c_blk = c_tiles.astype(BF16)
    d_skip = small["ssm_d"].reshape(1, SSM_WIDTH)
    gn_gain = small["ret_gn_gain"].reshape(1, RET_WIDTH)
    g_mix_pre = small["norm_mix_pre"].reshape(1, D_MODEL)
    g_mix_post = small["norm_mix_post"].reshape(1, D_MODEL)
    g_mlp_pre = small["norm_mlp_pre"].reshape(1, D_MODEL)
    g_mlp_post = small["norm_mlp_post"].reshape(1, D_MODEL)

    dist = ids is not None
    w_in, w_glu, w_out, w_ff1, w_ff2 = weights
    (q, k, v, gate, u, y_ssm, starts), got = _inproj_s5_fwd(
        x, g_mix_pre, w_in, cos2, sin2, b_blk, c_blk, tables, d_skip,
        _x_gather_send(MID + LATE, [w_glu, w_out, w_ff1, w_ff2], to_both=MID) if dist else None)
    if dist:
        w_glu, w_out, w_ff1, w_ff2 = got
    (y_ret, rprev, x1, mix), got = _ret_mix_fwd(q, k, v, gate, gn_gain, ret_consts, y_ssm, x, g_mix_post, w_glu, w_out,
                                                _x_gather_pass(LATE, [w_ff1, w_ff2]) if dist else None)
    if dist:
        w_ff1, w_ff2 = got
    df, dx2, d_ff2, dg_mlp_post, sq = _mlp_a(x1, target, g_mlp_pre, g_mlp_post, w_ff1, w_ff2)
    dx1, d_ff1, dg_mlp_pre = _mlp_b(df, dx2, x1, g_mlp_pre, w_ff1)
    (dy_ret, dy_ssm, d_out, d_glu, dg_mix_post), got = _mix_bwd(dx1, mix, g_mix_post, y_ssm, y_ret, w_glu, w_out,
                                                               _x_pair(LATE, [d_ff1, d_ff2]) if dist else None)
    if dist:
        sums = _pair_sum(ids, LATE, [d_ff1, d_ff2], got)
    (du, dd, dlam8, db_tiles, dc_tiles), got = _s5_bwd(
        u, dy_ssm, b_blk, c_blk, tables, d_skip, starts,
        _x_join(_x_chip(LATE, sums), _x_pair(MID, [d_glu, d_out])) if dist else None)
    if dist:
        parts = _chip_sum(ids, LATE, sums, got[:len(LATE)])
        sums = _pair_sum(ids, MID, [d_glu, d_out], got[len(LATE):])
    (dq, dk, dv, dgate, dgn), got = _ret_bwd(q, k, v, gate, gn_gain, ret_consts, rprev, dy_ret, cos2, sin2,
                                             _x_join(_x_share(LATE, parts), _x_chip(MID, sums)) if dist else None)
    if dist:
        late_grads = got[:len(LATE)]
        parts = _chip_sum(ids, MID, sums, got[len(LATE):])
        mid_parts = parts
    adam = (late_state[0], late_grads, late_state[1], late_state[2]) if dist else ()
    (dx, d_in, dg_mix_pre), late_stepped = _inproj_bwd(x, g_mix_pre, w_in, [dq, dk, dv, dgate, du], dx1, adam)

    dlam = jnp.sum(dlam8, axis=1)
    s5_grads = disc_vjp((dlam[0].reshape(SSM_GROUPS, SSM_STATE), dlam[1].reshape(SSM_GROUPS, SSM_STATE), db_tiles, dc_tiles))
    small_grads = {
        "norm_mix_pre": dg_mix_pre, "norm_mix_post": dg_mix_post, "ret_gn_gain": dgn, "ssm_d": dd,
        "norm_mlp_pre": dg_mlp_pre, "norm_mlp_post": dg_mlp_post, "loss_sum": 0.5 * jnp.sum(sq) / D_MODEL,
    }
    small_grads.update(dict(zip(s5_names, s5_grads)))
    if dist:
        return dx, d_in, mid_parts, late_grads, late_stepped, small_grads
    return dx, (d_in, d_glu, d_out, d_ff1, d_ff2), small_grads


BIG = (
    ("w_in", D_MODEL, IN_COLS, 1),
    ("w_glu", SSM_WIDTH, 2 * SSM_WIDTH, 1),
    ("w_out", D_MODEL, D_MODEL, 0),
    ("w_ff1", D_MODEL, D_FF, 1),
    ("w_ff2", D_FF, D_MODEL, 0),
)
SMALL = (
    ("norm_mix_pre", (D_MODEL,)), ("norm_mix_post", (D_MODEL,)), ("ret_gn_gain", (RET_WIDTH,)),
    ("ssm_lambda_re", (SSM_GROUPS, SSM_STATE)), ("ssm_lambda_im", (SSM_GROUPS, SSM_STATE)), ("ssm_log_dt", (SSM_GROUPS,)),
    ("ssm_b_re", (SSM_GROUPS, SSM_STATE, SSM_GROUP)), ("ssm_b_im", (SSM_GROUPS, SSM_STATE, SSM_GROUP)),
    ("ssm_c_re", (SSM_GROUPS, SSM_GROUP, SSM_STATE)), ("ssm_c_im", (SSM_GROUPS, SSM_GROUP, SSM_STATE)),
    ("ssm_d", (SSM_WIDTH,)), ("norm_mlp_pre", (D_MODEL,)), ("norm_mlp_post", (D_MODEL,)),
)
SMALL_ROWS = 1152
EXCH_TILES = 8


PACKED = SMALL + (("loss_sum", (1,)),)


def _packed_rows(shape):
    return -(-math.prod(shape) // (SUBLANES * LANES)) * SUBLANES


def _pack_small(tree):
    rows = []
    for name, shape in PACKED:
        size = math.prod(shape)
        flat = tree[name].reshape(-1).astype(F32) if name in tree else jnp.zeros((size,), F32)
        rows.append(jnp.pad(flat, (0, _packed_rows(shape) * LANES - size)).reshape(-1, LANES))
    used = sum(r.shape[0] for r in rows)
    rows.append(jnp.zeros((SMALL_ROWS - used, LANES), F32))
    return jnp.concatenate(rows, axis=0)


def _unpack_small(packed):
    out, row = {}, 0
    for name, shape in PACKED:
        size = math.prod(shape)
        out[name] = packed[row:row + _packed_rows(shape)].reshape(-1)[:size].reshape((1,) + shape)
        row += _packed_rows(shape)
    return out


def _half_shape(r, c, axis):
    return (r // 2, c) if axis == 1 else (r, c // 2)


def _region_shape(r, c, axis):
    return (r // 2, c // N_CHIPS) if axis == 1 else (r // N_CHIPS, c // 2)


def _shard_shape(r, c, axis):
    return (r, c // N_CHIPS) if axis == 1 else (r // N_CHIPS, c)


def _ds(start, size):
    return pl.ds(pl.multiple_of(start * size, size), size)


def _region_of_full(ref, r, c, axis, shard, half):
    if axis == 1:
        return ref.at[_ds(half, r // 2), _ds(shard, c // N_CHIPS)]
    return ref.at[_ds(shard, r // N_CHIPS), _ds(half, c // 2)]


def _half_of_full(ref, r, c, axis, half):
    if axis == 1:
        return ref.at[_ds(half, r // 2), :]
    return ref.at[:, _ds(half, c // 2)]


def _half_of_shard(ref, r, c, axis, half):
    if axis == 1:
        return ref.at[_ds(half, r // 2), :]
    return ref.at[:, _ds(half, c // 2)]


def _region_of_half(ref, r, c, axis, shard):
    if axis == 1:
        return ref.at[:, _ds(shard, c // N_CHIPS)]
    return ref.at[_ds(shard, r // N_CHIPS), :]


def _place():
    x, y, c = lax.axis_index("x"), lax.axis_index("y"), lax.axis_index("c")
    chips = [(1 - x, y), (x, 1 - y), (1 - x, 1 - y)]
    return x, y, c, chips


LATE = (3, 4)
MID = (1, 2)
FIRST = (0,)
SMALL_HALF = (SMALL_ROWS // 2, LANES)


def _remote(src, dst, send_sem, recv_sem, to):
    return pltpu.make_async_remote_copy(src_ref=src, dst_ref=dst, send_sem=send_sem, recv_sem=recv_sem,
                                        device_id=to, device_id_type=MESH)


def _same(arrays):
    return [jax.ShapeDtypeStruct(a.shape, a.dtype) for a in arrays]


def _x_gather_send(ws, fulls, to_both=()):
    def copies(ins, outs, send_sems, recv_sems):
        x, y, c, chips = _place()
        out, at = [], 0
        for j, w in enumerate(ws):
            _, r, cc, axis = BIG[w]
            mine = _region_of_full(outs[j], r, cc, axis, 2 * x + y, c)
            for other_core in range(2 if w in to_both else 1):
                to_core = (1 - c) if other_core else c
                out += [_remote(mine, mine, send_sems.at[at + k], recv_sems.at[at + k], (cx, cy, to_core))
                        for k, (cx, cy) in enumerate(chips)]
                at += 3
        return out

    n = sum(6 if w in to_both else 3 for w in ws)
    return _Exchange(fulls, _same(fulls), {j: j for j in range(len(ws))}, n, copies)


def _x_gather_pass(ws, fulls):
    def copies(ins, outs, send_sems, recv_sems):
        x, y, c, chips = _place()
        out = []
        for j, w in enumerate(ws):
            _, r, cc, axis = BIG[w]
            for k, (cx, cy) in enumerate(chips):
                landed = _region_of_full(outs[j], r, cc, axis, 2 * cx + cy, c)
                out.append(_remote(landed, landed, send_sems.at[3 * j + k], recv_sems.at[3 * j + k], (x, y, 1 - c)))
        return out

    return _Exchange(fulls, _same(fulls), {j: j for j in range(len(ws))}, 3 * len(ws), copies)


def _x_pair(ws, grads, small=None):
    def copies(ins, outs, send_sems, recv_sems):
        x, y, c, _ = _place()
        out = []
        for j, w in enumerate(ws):
            _, r, cc, axis = BIG[w]
            out.append(_remote(_half_of_full(ins[j], r, cc, axis, 1 - c), outs[j], send_sems.at[j], recv_sems.at[j], (x, y, 1 - c)))
        if small is not None:
            j = len(ws)
            out.append(_remote(ins[j].at[_ds(1 - c, SMALL_ROWS // 2), :], outs[j], send_sems.at[j], recv_sems.at[j], (x, y, 1 - c)))
        return out

    shapes = [jax.ShapeDtypeStruct(_half_shape(*BIG[w][1:]), F32) for w in ws]
    extra = [] if small is None else [small]
    return _Exchange(list(grads) + extra, shapes + [jax.ShapeDtypeStruct(SMALL_HALF, F32)] * len(extra), {},
                     len(ws) + len(extra), copies)


def _x_chip(ws, sums_bf16, small_sum=None):
    def copies(ins, outs, send_sems, recv_sems):
        x, y, c, chips = _place()
        out = []
        for j, w in enumerate(ws):
            _, r, cc, axis = BIG[w]
            out += [_remote(_region_of_half(ins[j], r, cc, axis, 2 * cx + cy), outs[j].at[k],
                            send_sems.at[3 * j + k], recv_sems.at[3 * j + k], (cx, cy, c)) for k, (cx, cy) in enumerate(chips)]
        if small_sum is not None:
            j = len(ws)
            out += [_remote(ins[j], outs[j].at[k], send_sems.at[3 * j + k], recv_sems.at[3 * j + k], (cx, cy, c))
                    for k, (cx, cy) in enumerate(chips)]
        return out

    shapes = [jax.ShapeDtypeStruct((3,) + _region_shape(*BIG[w][1:]), BF16) for w in ws]
    extra = [] if small_sum is None else [small_sum]
    return _Exchange(list(sums_bf16) + extra, shapes + [jax.ShapeDtypeStruct((3,) + SMALL_HALF, F32)] * len(extra), {},
                     3 * (len(ws) + len(extra)), copies)


def _x_share(ws, shards, small=None):
    def copies(ins, outs, send_sems, recv_sems):
        x, y, c, _ = _place()
        out = []
        for j, w in enumerate(ws):
            _, r, cc, axis = BIG[w]
            mine = _half_of_shard(outs[j], r, cc, axis, c)
            out.append(_remote(mine, mine, send_sems.at[j], recv_sems.at[j], (x, y, 1 - c)))
        if small is not None:
            j = len(ws)
            mine = outs[j].at[_ds(c, SMALL_ROWS // 2), :]
            out.append(_remote(mine, mine, send_sems.at[j], recv_sems.at[j], (x, y, 1 - c)))
        return out

    arrays = list(shards) + ([] if small is None else [small])
    return _Exchange(arrays, _same(arrays), {j: j for j in range(len(arrays))}, len(arrays), copies)


class _Offset:
    def __init__(self, sems, base):
        self._sems, self._base = sems, base

    @property
    def at(self):
        return self

    def __getitem__(self, i):
        return self._sems.at[self._base + i]


def _x_join(a, b):
    ka_in, ka_out = len(a.operands), len(a.out_shapes)

    def copies(ins, outs, send_sems, recv_sems):
        return (a.copies(ins[:ka_in], outs[:ka_out], send_sems, recv_sems)
                + b.copies(ins[ka_in:], outs[ka_out:], _Offset(send_sems, a.n), _Offset(recv_sems, a.n)))

    aliases = dict(a.aliases)
    aliases.update({ka_in + i: ka_out + o for i, o in b.aliases.items()})
    return _Exchange(a.operands + b.operands, a.out_shapes + b.out_shapes, aliases, a.n + b.n, copies)


def _run_exchange(name, exchange):
    k_in, k_out = len(exchange.operands), len(exchange.out_shapes)

    def body(*refs):
        copies = exchange.copies(refs[:k_in], refs[k_in:k_in + k_out], refs[-2], refs[-1])
        for cp in copies:
            cp.start()
        for cp in copies:
            cp.wait()

    return list(pl.pallas_call(
        body, name=name, in_specs=[ANY] * k_in, out_specs=[ANY] * k_out, out_shape=exchange.out_shapes,
        scratch_shapes=[pltpu.SemaphoreType.DMA((exchange.n,)), pltpu.SemaphoreType.DMA((exchange.n,))],
        input_output_aliases=dict(exchange.aliases),
    )(*exchange.operands))


def _gather_now(name, ws, fulls):
    send, onward = _x_gather_send(ws, fulls), _x_gather_pass(ws, fulls)
    k = len(fulls)

    def body(*refs):
        ins, outs, (send_a, recv_a, send_b, recv_b) = refs[:k], refs[k:2 * k], refs[2 * k:]
        first = send.copies(ins, outs, send_a, recv_a)
        second = onward.copies(ins, outs, send_b, recv_b)
        for cp in first:
            cp.start()
        for landed, cp in zip(first, second):
            landed.wait()
            cp.start()
        for cp in second:
            cp.wait()

    sems = lambda n: pltpu.SemaphoreType.DMA((n,))
    return list(pl.pallas_call(
        body, name=name, in_specs=[ANY] * k, out_specs=[ANY] * k, out_shape=send.out_shapes,
        scratch_shapes=[sems(send.n), sems(send.n), sems(onward.n), sems(onward.n)],
        input_output_aliases={j: j for j in range(k)},
    )(*fulls))


def _tile_rows(shape):
    return (shape[0] // EXCH_TILES, shape[1])


def _cast_place(ids, shards):
    def body(ids_ref, *refs):
        for j in range(len(BIG)):
            refs[len(BIG) + j][...] = refs[j][...].astype(BF16)

    def out_spec(w):
        _, r, cc, axis = BIG[w]
        tile = _tile_rows(_shard_shape(r, cc, axis))
        if axis == 1:
            return pl.BlockSpec(tile, lambda i, ids: (i, ids[1]))
        return pl.BlockSpec(tile, lambda i, ids: (ids[1] * EXCH_TILES + i, 0))

    return list(pl.pallas_call(
        body, name="cast_place",
        grid_spec=pltpu.PrefetchScalarGridSpec(
            num_scalar_prefetch=1, grid=(EXCH_TILES,),
            in_specs=[pl.BlockSpec(_tile_rows(_shard_shape(r, cc, axis)), lambda i, ids: (i, 0)) for _, r, cc, axis in BIG],
            out_specs=[out_spec(w) for w in range(len(BIG))]),
        out_shape=[jax.ShapeDtypeStruct((r, cc), BF16) for _, r, cc, _ in BIG],
        compiler_params=pltpu.CompilerParams(dimension_semantics=("parallel",)),
    )(ids, *shards))


def _pair_sum(ids, ws, grads, got, small=None):
    nb = len(ws)
    n = nb + (small is not None)
    halves = [_half_shape(*BIG[w][1:]) for w in ws] + [SMALL_HALF] * (n - nb)

    def body(ids_ref, *refs):
        for j in range(n):
            total = refs[j][...] + refs[n + j][...]
            refs[2 * n + j][...] = total.astype(refs[2 * n + j].dtype)

    def mine_spec(j):
        tile = _tile_rows(halves[j])
        if j < nb and BIG[ws[j]][3] == 0:
            return pl.BlockSpec(tile, lambda i, ids: (i, ids[0]))
        return pl.BlockSpec(tile, lambda i, ids: (ids[0] * EXCH_TILES + i, 0))

    plain = lambda j: pl.BlockSpec(_tile_rows(halves[j]), lambda i, ids: (i, 0))
    outs = pl.pallas_call(
        body, name="pair_sum",
        grid_spec=pltpu.PrefetchScalarGridSpec(
            num_scalar_prefetch=1, grid=(EXCH_TILES,),
            in_specs=[mine_spec(j) for j in range(n)] + [plain(j) for j in range(n)],
            out_specs=[plain(j) for j in range(n)]),
        out_shape=[jax.ShapeDtypeStruct(h, BF16 if j < nb else F32) for j, h in enumerate(halves)],
        compiler_params=pltpu.CompilerParams(dimension_semantics=("parallel",)),
    )(ids, *grads, *([] if small is None else [small]), *got)
    return list(outs)


def _chip_sum(ids, ws, sums, arrived):
    nb, n = len(ws), len(sums)
    regions = [_region_shape(*BIG[w][1:]) for w in ws] + [SMALL_HALF] * (n - nb)

    def body(ids_ref, *refs):
        for j in range(n):
            own, got, out = refs[j], refs[n + j], refs[2 * n + j]
            if j < nb:
                out[...] = ((own[...].astype(F32) + got[0].astype(F32)) + got[1].astype(F32)) + got[2].astype(F32)
            else:
                out[...] = (own[...] + got[1]) + (got[0] + got[2])

    def own_spec(j):
        tile = _tile_rows(regions[j])
        if j >= nb:
            return pl.BlockSpec(tile, lambda i, ids: (i, 0))
        if BIG[ws[j]][3] == 1:
            return pl.BlockSpec(tile, lambda i, ids: (i, ids[1]))
        return pl.BlockSpec(tile, lambda i, ids: (ids[1] * EXCH_TILES + i, 0))

    def out_spec(j):
        tile = _tile_rows(regions[j])
        if j < nb and BIG[ws[j]][3] == 0:
            return pl.BlockSpec(tile, lambda i, ids: (i, ids[0]))
        return pl.BlockSpec(tile, lambda i, ids: (ids[0] * EXCH_TILES + i, 0))

    got_spec = lambda j: pl.BlockSpec((3,) + _tile_rows(regions[j]), lambda i, ids: (0, i, 0))
    return list(pl.pallas_call(
        body, name="chip_sum",
        grid_spec=pltpu.PrefetchScalarGridSpec(
            num_scalar_prefetch=1, grid=(EXCH_TILES,),
            in_specs=[own_spec(j) for j in range(n)] + [got_spec(j) for j in range(n)],
            out_specs=[out_spec(j) for j in range(n)]),
        out_shape=[jax.ShapeDtypeStruct(_shard_shape(*BIG[w][1:]), F32) for w in ws]
        + [jax.ShapeDtypeStruct((SMALL_ROWS, LANES), F32)] * (n - nb),
        compiler_params=pltpu.CompilerParams(dimension_semantics=("parallel",)),
    )(ids, *sums, *arrived))


def _adamw_body(n):
    c1 = 1.0 - ADAM_B1 ** ADAM_STEP
    c2 = 1.0 - ADAM_B2 ** ADAM_STEP

    def body(*refs):
        for j in range(n):
            w, g, m, v = (refs[k * n + j][...] for k in range(4))
            m = ADAM_B1 * m + (1.0 - ADAM_B1) * g
            v = ADAM_B2 * v + (1.0 - ADAM_B2) * (g * g)
            refs[4 * n + j][...] = -ADAM_LR * ((m / c1) / (jnp.sqrt(v / c2) + ADAM_EPS) + ADAM_WD * w)
            refs[5 * n + j][...] = m
            refs[6 * n + j][...] = v

    return body


def _adamw(name, ws, gs, ms, vs):
    n = len(ws)
    specs = [pl.BlockSpec(_tile_rows(w.shape), lambda i: (i, 0)) for w in ws]
    shapes = [jax.ShapeDtypeStruct(w.shape, F32) for w in ws]
    out, _ = _call(_adamw_body(n), name, (EXCH_TILES,), specs * 4, specs * 3, shapes * 3, [], (*ws, *gs, *ms, *vs))
    return out[:n], out[n:2 * n], out[2 * n:]


def _adamw_whole(name, ws, gs, ms, vs):
    n = len(ws)
    out = pl.pallas_call(_adamw_body(n), name=name, out_shape=[jax.ShapeDtypeStruct(w.shape, F32) for w in ws] * 3)(*ws, *gs, *ms, *vs)
    return out[:n], out[n:2 * n], out[2 * n:]


def kernel(x, norm_mix_pre, norm_mix_post, w_in, ret_gn_gain, ssm_lambda_re, ssm_lambda_im, ssm_log_dt, ssm_b_re, ssm_b_im, ssm_c_re, ssm_c_im, ssm_d, w_glu, w_out, norm_mlp_pre, norm_mlp_post, w_ff1, w_ff2, loss_target, m_norm_mix_pre, m_norm_mix_post, m_w_in, m_ret_gn_gain, m_ssm_lambda_re, m_ssm_lambda_im, m_ssm_log_dt, m_ssm_b_re, m_ssm_b_im, m_ssm_c_re, m_ssm_c_im, m_ssm_d, m_w_glu, m_w_out, m_norm_mlp_pre, m_norm_mlp_post, m_w_ff1, m_w_ff2, v_norm_mix_pre, v_norm_mix_post, v_w_in, v_ret_gn_gain, v_ssm_lambda_re, v_ssm_lambda_im, v_ssm_log_dt, v_ssm_b_re, v_ssm_b_im, v_ssm_c_re, v_ssm_c_im, v_ssm_d, v_w_glu, v_w_out, v_norm_mlp_pre, v_norm_mlp_post, v_w_ff1, v_w_ff2):
    given = dict(locals())
    order = ["norm_mix_pre", "norm_mix_post", "w_in", "ret_gn_gain", "ssm_lambda_re", "ssm_lambda_im", "ssm_log_dt",
             "ssm_b_re", "ssm_b_im", "ssm_c_re", "ssm_c_im", "ssm_d", "w_glu", "w_out", "norm_mlp_pre", "norm_mlp_post",
             "w_ff1", "w_ff2"]
    big_names = [name for name, _, _, _ in BIG]
    ids = jnp.stack([lax.axis_index("c"), 2 * lax.axis_index("x") + lax.axis_index("y")]).astype(jnp.int32)

    weights = _cast_place(ids, [given[name][0] for name in big_names])
    w_in = _gather_now("gather_w_in", FIRST, weights[:1])
    small_w = {name: given[name][0] for name, _ in SMALL}
    late_state = [[given[prefix + big_names[w]][0] for w in LATE] for prefix in ("", "m_", "v_")]
    dx, d_in, mid_parts, late_grads, late_stepped, small_grads = _local_step(
        x[0], loss_target[0], small_w, w_in + weights[1:], ids, late_state)

    small_packed = _pack_small(small_grads)
    got = _run_exchange("pair_exchange", _x_join(_x_share(MID, mid_parts), _x_pair(FIRST, [d_in], small_packed)))
    mid_grads, got = got[:len(MID)], got[len(MID):]
    sums = _pair_sum(ids, FIRST, [d_in], got, small_packed)
    arrived = _run_exchange("chip_exchange", _x_chip(FIRST, sums[:-1], sums[-1]))
    parts = _chip_sum(ids, FIRST, sums, arrived)
    g_in, g_small = _run_exchange("pair_share", _x_share(FIRST, parts[:-1], parts[-1]))
    big_grads = [g_in] + mid_grads + late_grads
    rest = lambda prefix: [given[prefix + big_names[w]][0] for w in FIRST + MID]
    rest_out = _adamw("adamw_big", rest(""), big_grads[:len(FIRST + MID)], rest("m_"), rest("v_"))
    big_out = [list(rest_out[k]) + list(late_stepped[k]) for k in range(3)]
    small_names = [name for name, _ in SMALL]
    small_tree = _unpack_small(g_small)
    small_grads = [small_tree[name] for name in small_names]
    small = lambda prefix: [given[prefix + name] for name in small_names]
    small_out = _adamw_whole("adamw_small", small(""), small_grads, small("m_"), small("v_"))

    def in_order(big_arrays, small_arrays):
        tree = {name: a[None] for name, a in zip(big_names, big_arrays)}
        tree.update(zip(small_names, small_arrays))
        return [tree[name] for name in order]

    loss = small_tree["loss_sum"].reshape(())
    return (loss, dx[None], *in_order(big_grads, small_grads), *in_order(big_out[0], small_out[0]),
            *in_order(big_out[1], small_out[1]), *in_order(big_out[2], small_out[2]))
```

```python
import math

import jax
import jax.numpy as jnp
import numpy as np
from jax import lax
from jax.experimental import pallas as pl
from jax.experimental.pallas import tpu as pltpu

F32 = jnp.float32
BF16 = jnp.bfloat16

D_MODEL = 1024
RET_WIDTH = 512
RET_HEADS = 4
HEAD_DIM = 128
RET_CHUNK = 128
ROPE_BASE = 10000.0
SSM_WIDTH = 512
SSM_GROUP = 16
SSM_GROUPS = 32
SSM_STATE = 64
N_STATES = SSM_GROUPS * SSM_STATE
D_FF = 4096
IN_COLS = 4 * RET_WIDTH + SSM_WIDTH
NORM_EPS = 1e-6
K_SCALE = HEAD_DIM ** -0.5

ADAM_LR = 0.001
ADAM_B1 = 0.9
ADAM_B2 = 0.999
ADAM_EPS = 1e-08
ADAM_WD = 0.01
ADAM_STEP = 10

LANES = 128
SUBLANES = 8
VMEM_LIMIT = 56 * 2 ** 20

TOK_TILE = 512
MLP_TILE = 256
RET_CHUNKS_PER_STEP = 4
S5_TILE = 512
S5_STEPS = S5_TILE // SUBLANES
S5_COLBLK = N_STATES // LANES
S5_GROUP = 4

N_CHIPS = 4
MESH = pl.DeviceIdType.MESH


def _dot(a, b):
    return jnp.dot(a, b, preferred_element_type=F32)


def _dot_nt(a, b):
    return lax.dot_general(a, b, (((1,), (1,)), ((), ())), preferred_element_type=F32)


def _dot_tn(a, b):
    return lax.dot_general(a, b, (((0,), (0,)), ((), ())), preferred_element_type=F32)


def _sigmoid(x):
    return 1.0 / (1.0 + jnp.exp(-x))


_GELU_C = math.sqrt(2.0 / math.pi)


def _gelu(x):
    return 0.5 * x * (1.0 + jnp.tanh(_GELU_C * (x + 0.044715 * (x * x * x))))


def _gelu_grad(x):
    t = jnp.tanh(_GELU_C * (x + 0.044715 * (x * x * x)))
    return 0.5 * (1.0 + t) + 0.5 * x * (1.0 - t * t) * (_GELU_C * (1.0 + 3.0 * 0.044715 * (x * x)))


def _rms(x):
    r = lax.rsqrt(jnp.mean(x * x, axis=-1, keepdims=True) + NORM_EPS)
    return x * r, r


def _rms_bwd(n, r, gain, dy):
    dn = dy * gain
    dx = r * (dn - n * jnp.mean(dn * n, axis=-1, keepdims=True))
    return dx, jnp.sum(dy * n, axis=0, keepdims=True)


def _full(shape):
    nd = len(shape)
    return pl.BlockSpec(shape, lambda i, _n=nd: (0,) * _n, pipeline_mode=pl.Buffered(1))


def _rows(tile, width):
    return pl.BlockSpec((tile, width), lambda i: (i, 0))


def _rows_rev(tile, width, n):
    return pl.BlockSpec((tile, width), lambda i, _n=n: (_n - 1 - i, 0))


def _params(sem):
    return pltpu.CompilerParams(dimension_semantics=(sem,), vmem_limit_bytes=VMEM_LIMIT)


ANY = pl.BlockSpec(memory_space=pl.ANY)


class _Exchange:
    def __init__(self, operands, out_shapes, aliases, n, copies):
        self.operands, self.out_shapes, self.aliases, self.n, self.copies = list(operands), list(out_shapes), aliases, n, copies


def _call(body, name, grid, in_specs, out_specs, out_shape, scratch_shapes, args, exchange=None):
    if exchange is None:
        outs = pl.pallas_call(body, name=name, grid=grid, in_specs=in_specs, out_specs=out_specs, out_shape=out_shape,
                              scratch_shapes=scratch_shapes, compiler_params=_params("arbitrary"))(*args)
        return list(outs), []
    n_in, n_out, n_scr = len(in_specs), len(out_specs), len(scratch_shapes)
    k_in, k_out = len(exchange.operands), len(exchange.out_shapes)
    last = grid[0] - 1

    def hosted(*refs):
        own_in, rest = refs[:n_in], refs[n_in:]
        ex_in, rest = rest[:k_in], rest[k_in:]
        own_out, rest = rest[:n_out], rest[n_out:]
        ex_out, rest = rest[:k_out], rest[k_out:]
        own_scr, (send_sems, recv_sems) = rest[:n_scr], rest[n_scr:]

        @pl.when(pl.program_id(0) == 0)
        def _():
            for cp in exchange.copies(ex_in, ex_out, send_sems, recv_sems):
                cp.start()

        body(*own_in, *own_out, *own_scr)

        @pl.when(pl.program_id(0) == last)
        def _():
            for cp in exchange.copies(ex_in, ex_out, send_sems, recv_sems):
                cp.wait()

    outs = pl.pallas_call(
        hosted, name=name, grid=grid, in_specs=list(in_specs) + [ANY] * k_in, out_specs=list(out_specs) + [ANY] * k_out,
        out_shape=list(out_shape) + exchange.out_shapes,
        scratch_shapes=list(scratch_shapes) + [pltpu.SemaphoreType.DMA((exchange.n,)), pltpu.SemaphoreType.DMA((exchange.n,))],
        input_output_aliases={n_in + i: n_out + o for i, o in exchange.aliases.items()},
        compiler_params=_params("arbitrary"),
    )(*args, *exchange.operands)
    return list(outs[:n_out]), list(outs[n_out:])


def _rope(t, cos2, sin2):
    return t * cos2 + pltpu.roll(t, HEAD_DIM // 2, 1) * sin2


def _rope_bwd(d, cos2, sin2):
    return d * cos2 + pltpu.roll(d * sin2, HEAD_DIM // 2, d.ndim - 1)


def _inproj_bwd(x, gain, w_in, dpieces, dres, adam=()):
    seq = x.shape[0]
    t = TOK_TILE
    steps = seq // t
    n_pieces = len(dpieces)
    n_adam = len(adam[0]) if adam else 0
    adam_step = _adamw_body(n_adam)

    def body(x_ref, g_ref, w_ref, *refs):
        piece_refs, (dres_ref, *adam_in), (dx_ref, dw_ref, dg_ref, *adam_out) = (
            refs[:n_pieces], refs[n_pieces:n_pieces + 1 + 4 * n_adam], refs[n_pieces + 1 + 4 * n_adam:])

        @pl.when(pl.program_id(0) == 0)
        def _():
            dw_ref[...] = jnp.zeros_like(dw_ref)
            dg_ref[...] = jnp.zeros_like(dg_ref)

        adam_step(*adam_in, *adam_out)

        n, r = _rms(x_ref[...])
        gain_v = g_ref[...]
        h = (n * gain_v).astype(BF16)
        dh = None
        for j, piece in enumerate(piece_refs):
            cols = slice(j * RET_WIDTH, (j + 1) * RET_WIDTH)
            dp = piece[...]
            dw_ref[:, cols] += _dot_tn(h, dp)
            term = _dot_nt(dp, w_ref[:, cols])
            dh = term if dh is None else dh + term
        dx, dgs = _rms_bwd(n, r, gain_v, dh)
        dg_ref[...] += dgs
        dx_ref[...] = dres_ref[...] + dx

    adam_arrays = [a for group in adam for a in group]
    adam_spec = lambda a: pl.BlockSpec((a.shape[0] // steps, a.shape[1]), lambda i: (i, 0))
    outs, _ = _call(
        body, "inproj_bwd", (steps,),
        [_rows(t, D_MODEL), _full((1, D_MODEL)), _full((D_MODEL, IN_COLS))] + [_rows(t, RET_WIDTH)] * n_pieces + [_rows(t, D_MODEL)]
        + [adam_spec(a) for a in adam_arrays],
        [_rows(t, D_MODEL), _full((D_MODEL, IN_COLS)), _full((1, D_MODEL))] + [adam_spec(a) for a in adam_arrays[:n_adam]] * 3,
        [jax.ShapeDtypeStruct((seq, D_MODEL), F32), jax.ShapeDtypeStruct((D_MODEL, IN_COLS), F32),
         jax.ShapeDtypeStruct((1, D_MODEL), F32)] + [jax.ShapeDtypeStruct(a.shape, F32) for a in adam_arrays[:n_adam]] * 3,
        [], (x, gain, w_in, *dpieces, dres, *adam_arrays))
    stepped = outs[3:]
    return outs[:3], (stepped[:n_adam], stepped[n_adam:2 * n_adam], stepped[2 * n_adam:])


def _ret_consts():
    c = RET_CHUNK
    f32 = np.float32
    log_gamma = np.log(f32(1.0) - np.exp(np.linspace(math.log(1.0 / 32), math.log(1.0 / 512), RET_HEADS, dtype=f32))).astype(f32)
    idx = np.arange(c, dtype=f32)
    diff = idx[:, None] - idx[None, :]
    decay = np.where(diff[None] >= 0, np.exp(np.maximum(diff, f32(0.0))[None] * log_gamma[:, None, None]), f32(0.0))
    zeta = np.exp((c - 1 - idx)[None, :] * log_gamma[:, None])
    xi = np.exp((idx + f32(1.0))[None, :] * log_gamma[:, None])
    g_chunk = np.exp(f32(c) * log_gamma)
    wide = lambda rowvals: jnp.asarray(np.broadcast_to(rowvals[:, :, None], (RET_HEADS, c, c)).astype(f32))
    return (jnp.asarray(decay.astype(f32)), wide(xi), wide(zeta),
            jnp.asarray(np.broadcast_to(g_chunk[:, None, None], (RET_HEADS, c, c)).astype(f32)))


def _rope_tables(seq):
    f32 = np.float32
    half = HEAD_DIM // 2
    inv_freq = np.power(f32(ROPE_BASE), -np.arange(half, dtype=f32) / f32(half)).astype(f32)
    ang = (np.arange(seq, dtype=f32)[:, None] * inv_freq[None, :]).astype(f32)
    cos, sin = np.cos(ang).astype(f32), np.sin(ang).astype(f32)
    return jnp.asarray(np.concatenate([cos, cos], axis=1)), jnp.asarray(np.concatenate([-sin, sin], axis=1))


def _bmm(a, b):
    return lax.dot_general(a, b, (((2,), (1,)), ((0,), (0,))), preferred_element_type=F32)


def _bmm_nt(a, b):
    return lax.dot_general(a, b, (((2,), (2,)), ((0,), (0,))), preferred_element_type=F32)


def _bmm_tn(a, b):
    return lax.dot_general(a, b, (((1,), (1,)), ((0,), (0,))), preferred_element_type=F32)


def _ret_blocks(ref):
    return jnp.stack([ref[cc * RET_CHUNK:(cc + 1) * RET_CHUNK, hd * HEAD_DIM:(hd + 1) * HEAD_DIM]
                      for cc in range(RET_CHUNKS_PER_STEP) for hd in range(RET_HEADS)])


def _ret_unblock(ref, blocks):
    for cc in range(RET_CHUNKS_PER_STEP):
        for hd in range(RET_HEADS):
            ref[cc * RET_CHUNK:(cc + 1) * RET_CHUNK, hd * HEAD_DIM:(hd + 1) * HEAD_DIM] = blocks[cc * RET_HEADS + hd].astype(ref.dtype)


def _rope_blocks(ref):
    return jnp.stack([ref[cc * RET_CHUNK:(cc + 1) * RET_CHUNK, :] for cc in range(RET_CHUNKS_PER_STEP) for _ in range(RET_HEADS)])


def _per_chunk(ref):
    return jnp.concatenate([ref[...]] * RET_CHUNKS_PER_STEP, axis=0)


def _ret_gain(gain_ref):
    return jnp.stack([gain_ref[:, hd * HEAD_DIM:(hd + 1) * HEAD_DIM]
                      for _ in range(RET_CHUNKS_PER_STEP) for hd in range(RET_HEADS)])


def _ret_fwd_step(q_ref, k_ref, v_ref, gate_ref, gain_ref, dec_ref, xi_ref, zeta_ref, gch_ref, y_ref, rprev_ref, state):
    cps = RET_CHUNKS_PER_STEP
    qb, kb, vb = _ret_blocks(q_ref), _ret_blocks(k_ref), _ret_blocks(v_ref)
    s = _bmm_nt(qb, kb) * _per_chunk(dec_ref)
    vz = (vb.astype(F32) * _per_chunk(zeta_ref)).astype(BF16)
    incr = _bmm_tn(kb, vz)
    gch_v = gch_ref[...]
    rp = state[...]
    before = []
    for cc in range(cps):
        before.append(rp.astype(BF16))
        rprev_ref[cc] = before[-1]
        rp = gch_v * rp + incr[cc * RET_HEADS:(cc + 1) * RET_HEADS]
    state[...] = rp
    qx = (qb.astype(F32) * _per_chunk(xi_ref)).astype(BF16)
    o = _bmm(s.astype(BF16), vb) + _bmm(qx, jnp.concatenate(before, axis=0))
    oc = o - jnp.mean(o, axis=-1, keepdims=True)
    on = oc * lax.rsqrt(jnp.mean(oc * oc, axis=-1, keepdims=True) + NORM_EPS)
    g = _ret_blocks(gate_ref)
    _ret_unblock(y_ref, g * _sigmoid(g) * (on * _ret_gain(gain_ref)))


def _ret_bwd_step(q_ref, k_ref, v_ref, gate_ref, gain_ref, dec_ref, xi_ref, zeta_ref, gch_ref, rprev_ref, dyr_ref, cos_ref, sin_ref,
                  dq_ref, dk_ref, dv_ref, dgate_ref, dgain_ref, dstate):
    cps = RET_CHUNKS_PER_STEP
    nb = cps * RET_HEADS
    qb, kb, vb = _ret_blocks(q_ref), _ret_blocks(k_ref), _ret_blocks(v_ref)
    dec_b, xi_b, zeta_b = _per_chunk(dec_ref), _per_chunk(xi_ref), _per_chunk(zeta_ref)
    rpb = rprev_ref[...].reshape(nb, HEAD_DIM, HEAD_DIM)
    sdb = (_bmm_nt(qb, kb) * dec_b).astype(BF16)
    qx = (qb.astype(F32) * xi_b).astype(BF16)
    o = _bmm(sdb, vb) + _bmm(qx, rpb)
    oc = o - jnp.mean(o, axis=-1, keepdims=True)
    rstd = lax.rsqrt(jnp.mean(oc * oc, axis=-1, keepdims=True) + NORM_EPS)
    on = oc * rstd
    g = _ret_blocks(gate_ref)
    sg = _sigmoid(g)
    gain_b = _ret_gain(gain_ref)
    dyr = _ret_blocks(dyr_ref)
    _ret_unblock(dgate_ref, dyr * (on * gain_b) * (sg * (1.0 + g * (1.0 - sg))))
    dy = dyr * (g * sg)
    dgain = jnp.sum(dy * on, axis=1, keepdims=True)
    for hd in range(RET_HEADS):
        part = dgain[hd]
        for cc in range(1, cps):
            part = part + dgain[cc * RET_HEADS + hd]
        dgain_ref[:, hd * HEAD_DIM:(hd + 1) * HEAD_DIM] += part
    don = dy * gain_b
    do = rstd * (don - jnp.mean(don, axis=-1, keepdims=True) - on * jnp.mean(don * on, axis=-1, keepdims=True))
    dob = do.astype(BF16)
    ds = (_bmm_nt(dob, vb) * dec_b).astype(BF16)
    dq = _bmm(ds, kb) + _bmm_nt(dob, rpb) * xi_b
    dk = _bmm_tn(ds, qb)
    dv = _bmm_tn(sdb, dob)
    dstate_local = _bmm_tn(qx, dob)
    vz = (vb.astype(F32) * zeta_b).astype(BF16)
    gch_v = gch_ref[...]
    zeta_v = zeta_ref[...]
    gh = dstate[...]
    dk_extra, dv_extra = [None] * cps, [None] * cps
    for cc in reversed(range(cps)):
        sl = slice(cc * RET_HEADS, (cc + 1) * RET_HEADS)
        gb = gh.astype(BF16)
        dk_extra[cc] = _bmm_nt(vz[sl], gb)
        dv_extra[cc] = _bmm(kb[sl], gb) * zeta_v
        gh = dstate_local[sl] + gch_v * gh
    dstate[...] = gh
    cos_b, sin_b = _rope_blocks(cos_ref), _rope_blocks(sin_ref)
    _ret_unblock(dq_ref, _rope_bwd(dq, cos_b, sin_b))
    _ret_unblock(dk_ref, _rope_bwd((dk + jnp.concatenate(dk_extra, axis=0)) * K_SCALE, cos_b, sin_b))
    _ret_unblock(dv_ref, dv + jnp.concatenate(dv_extra, axis=0))


def _ret_bwd(q, k, v, gate, gn_gain, consts, rprev, dy_ret, cos2, sin2, exchange=None):
    seq = q.shape[0]
    cps = RET_CHUNKS_PER_STEP
    t = cps * RET_CHUNK
    nt = seq // t
    dec, xi, zeta, gch = consts

    def body(*refs):
        dgain_ref, dstate = refs[-2:]

        @pl.when(pl.program_id(0) == 0)
        def _():
            dstate[...] = jnp.zeros_like(dstate)
            dgain_ref[...] = jnp.zeros_like(dgain_ref)

        _ret_bwd_step(*refs)

    cst = _full((RET_HEADS, RET_CHUNK, RET_CHUNK))
    rev = _rows_rev(t, RET_WIDTH, nt)
    act = jax.ShapeDtypeStruct((seq, RET_WIDTH), BF16)
    return _call(
        body, "ret_bwd", (nt,),
        [rev] * 4 + [_full((1, RET_WIDTH)), cst, cst, cst, cst,
                     pl.BlockSpec((cps, RET_HEADS, HEAD_DIM, HEAD_DIM), lambda i: (nt - 1 - i, 0, 0, 0)), rev,
                     _rows_rev(t, HEAD_DIM, nt), _rows_rev(t, HEAD_DIM, nt)],
        [rev] * 4 + [_full((1, RET_WIDTH))], [act, act, act, act, jax.ShapeDtypeStruct((1, RET_WIDTH), F32)],
        [pltpu.VMEM((RET_HEADS, HEAD_DIM, HEAD_DIM), F32)],
        (q, k, v, gate, gn_gain, dec, xi, zeta, gch, rprev, dy_ret, cos2, sin2), exchange)


MXU_TILE = 256
S5_STATE_TILES = 2 * N_STATES // MXU_TILE
S5_CHANNELS_PER_TILE = MXU_TILE // SSM_STATE * SSM_GROUP
S5_TILES_SHAPE = (S5_STATE_TILES, LANES, MXU_TILE)


def _chan_block(kk):
    return ((kk % (S5_STATE_TILES // 2)) * S5_CHANNELS_PER_TILE) // LANES * LANES


def _s5_tiles(blocks_re, blocks_im):
    per = MXU_TILE // SSM_STATE
    half = S5_STATE_TILES // 2
    eye = jnp.eye(per, dtype=F32)
    in_half = (jnp.arange(half) % (LANES // S5_CHANNELS_PER_TILE))[:, None] == jnp.arange(LANES // S5_CHANNELS_PER_TILE)[None, :]
    out = []
    for blk in (blocks_re, blocks_im):
        t = blk.reshape(half, per, SSM_GROUP, 1, SSM_STATE) * eye[None, :, None, :, None]
        t = t.reshape(half, 1, S5_CHANNELS_PER_TILE, MXU_TILE)
        out.append(jnp.where(in_half[:, :, None, None], t, 0.0).reshape(half, LANES, MXU_TILE))
    return jnp.concatenate(out, axis=0)


def _s5_discretise(lam_re, lam_im, log_dt, b_re, b_im, c_re, c_im):
    lam = lax.complex(jnp.minimum(lam_re, -1e-4), lam_im)
    dt = jnp.exp(log_dt)[:, None]
    lam_bar = jnp.exp(lam * dt)
    b_bar = ((lam_bar - 1.0) / lam)[:, :, None] * lax.complex(b_re, b_im)
    b_tiles = _s5_tiles(jnp.swapaxes(jnp.real(b_bar), 1, 2), jnp.swapaxes(jnp.imag(b_bar), 1, 2))
    return jnp.real(lam_bar), jnp.imag(lam_bar), b_tiles, _s5_tiles(c_re, -c_im)


def _s5_tables(lbr, lbi):
    lr = lbr.reshape(1, N_STATES)
    li = lbi.reshape(1, N_STATES)
    pr, pi = lr, li
    while pr.shape[0] < S5_STEPS:
        top_r, top_i = pr[-1:], pi[-1:]
        pr, pi = (jnp.concatenate([pr, pr * top_r - pi * top_i], axis=0), jnp.concatenate([pi, pr * top_i + pi * top_r], axis=0))
    bc = lambda a: jnp.broadcast_to(a, (SUBLANES, N_STATES))
    rep = lambda rows: jnp.broadcast_to(rows[:, None, :], (S5_STEPS, SUBLANES, N_STATES)).reshape(S5_STEPS * SUBLANES, N_STATES)
    return bc(lr), bc(li), rep(pr), rep(pi)


def _cmul(ar, ai, br, bi):
    return ar * br - ai * bi, ar * bi + ai * br


def _seg(i):
    return pl.ds(pl.multiple_of(i * SUBLANES, SUBLANES), SUBLANES)


def _scan_order():
    r = np.arange(S5_TILE)
    token = (r % SUBLANES) * S5_STEPS + r // SUBLANES
    p = (token[:, None] == np.arange(S5_TILE)[None, :]).astype(np.float32)
    return jnp.asarray(p, dtype=BF16), jnp.asarray(p.T, dtype=BF16)


def _to_scan_order(p_ref, rows_bf16):
    return _dot(p_ref[...], rows_bf16).astype(BF16)


def _from_scan_order(pt_ref, rows, pieces=3):
    pt = pt_ref[...]
    out = None
    for _ in range(pieces):
        piece = rows.astype(BF16)
        rows = rows - piece.astype(F32)
        term = _dot(pt, piece)
        out = term if out is None else out + term
    return out


def _s5_channels_to_states(chan_b, m_ref, xs):
    for kk in range(S5_STATE_TILES):
        lo = _chan_block(kk)
        res = _dot(chan_b[:, lo:lo + LANES], m_ref[kk])
        xs[2 * kk] = res[:, :LANES]
        xs[2 * kk + 1] = res[:, LANES:]


def _s5_states_to_channels(sb_ref, m_ref):
    out = []
    for lo in range(0, SSM_WIDTH, LANES):
        acc = None
        for kk in range(S5_STATE_TILES):
            if _chan_block(kk) == lo:
                term = _dot_nt(sb_ref[:, kk * MXU_TILE:(kk + 1) * MXU_TILE], m_ref[kk])
                acc = term if acc is None else acc + term
        out.append(acc)
    return jnp.concatenate(out, axis=1)


def _s5_outer_acc(acc_ref, chan_b, states_ref):
    for kk in range(S5_STATE_TILES):
        lo = _chan_block(kk)
        acc_ref[kk] += _dot_tn(chan_b[:, lo:lo + LANES], states_ref[:, kk * MXU_TILE:(kk + 1) * MXU_TILE])


def _s5_scan_tile(xs, lr_ref, li_ref, pr_ref, pi_ref, carry_re, carry_im, conj, reverse, prev_of=None):
    sgn = -1.0 if conj else 1.0
    seg_in_re, seg_in_im, sums_re, sums_im = [], [], [], []
    group = S5_GROUP
    for grp in range(S5_COLBLK // group):
        blks = [grp * group + j for j in range(group)]
        lrs = [lr_ref[:, b * LANES:(b + 1) * LANES] for b in blks]
        lis = [sgn * li_ref[:, b * LANES:(b + 1) * LANES] for b in blks]

        def step(it, carry, blks=blks, lrs=lrs, lis=lis):
            i = (S5_STEPS - 1 - it) if reverse else it
            out_r, out_i = [], []
            for j, b in enumerate(blks):
                xr, xi = _cmul(lrs[j], lis[j], carry[j], carry[group + j])
                xr = xr + xs[b, _seg(i), :]
                xi = xi + xs[S5_COLBLK + b, _seg(i), :]
                xs[b, _seg(i), :] = xr
                xs[S5_COLBLK + b, _seg(i), :] = xi
                out_r.append(xr)
                out_i.append(xi)
            return tuple(out_r + out_i)

        zeros = tuple(jnp.zeros((SUBLANES, LANES), F32) for _ in range(2 * group))
        ends = lax.fori_loop(0, S5_STEPS, step, zeros, unroll=True)

        ins_r, ins_i = [], []
        for j, b in enumerate(blks):
            cols = slice(b * LANES, (b + 1) * LANES)
            pr = pr_ref[SUBLANES * S5_STEPS - 1:SUBLANES * S5_STEPS, cols]
            pi = sgn * pi_ref[SUBLANES * S5_STEPS - 1:SUBLANES * S5_STEPS, cols]
            cur_r = carry_re[:, cols]
            cur_i = carry_im[:, cols]
            row_id = lax.broadcasted_iota(jnp.int32, (SUBLANES, LANES), 0)
            in_r = jnp.zeros((SUBLANES, LANES), F32)
            in_i = jnp.zeros((SUBLANES, LANES), F32)
            order = range(SUBLANES - 1, -1, -1) if reverse else range(SUBLANES)
            for sgm in order:
                in_r = jnp.where(row_id == sgm, cur_r, in_r)
                in_i = jnp.where(row_id == sgm, cur_i, in_i)
                mr, mi = _cmul(pr, pi, cur_r, cur_i)
                cur_r = mr + ends[j][sgm:sgm + 1, :]
                cur_i = mi + ends[group + j][sgm:sgm + 1, :]
            carry_re[:, cols] = cur_r
            carry_im[:, cols] = cur_i
            ins_r.append(in_r)
            ins_i.append(in_i)

        def fix(it, c, blks=blks, ins_r=ins_r, ins_i=ins_i):
            pw = (S5_STEPS - 1 - it) if reverse else it
            out = []
            for j, b in enumerate(blks):
                cols = slice(b * LANES, (b + 1) * LANES)
                prow = pl.ds(pl.multiple_of(pw * SUBLANES, SUBLANES), SUBLANES)
                fr, fi = _cmul(pr_ref[prow, cols], sgn * pi_ref[prow, cols], ins_r[j], ins_i[j])
                vr = xs[b, _seg(it), :] + fr
                vi = xs[S5_COLBLK + b, _seg(it), :] + fi
                xs[b, _seg(it), :] = vr
                xs[S5_COLBLK + b, _seg(it), :] = vi
                if prev_of is not None:
                    opr, opi, sr, si = c[4 * j:4 * j + 4]
                    out += [prev_of[0][b, _seg(it), :], prev_of[0][S5_COLBLK + b, _seg(it), :],
                            sr + vr * opr + vi * opi, si + vi * opr - vr * opi]
            return tuple(out)

        init = ()
        if prev_of is not None:
            zero = jnp.zeros((SUBLANES, LANES), F32)
            for b in blks:
                init += (prev_of[1][b], prev_of[2][b], zero, zero)
        done = lax.fori_loop(0, S5_STEPS, fix, init, unroll=True)
        for j in range(len(blks) if prev_of is not None else 0):
            sums_re.append(done[4 * j + 2])
            sums_im.append(done[4 * j + 3])
        seg_in_re += ins_r
        seg_in_im += ins_i
    return seg_in_re, seg_in_im, sums_re, sums_im


def _s5_pack(xs, dst):
    for b in range(2 * S5_COLBLK):
        dst[:, b * LANES:(b + 1) * LANES] = xs[b].astype(BF16)


def _inproj_s5_fwd(x, gain, w_in, cos2, sin2, b_blk, c_blk, tables, d_skip, exchange=None):
    seq = x.shape[0]
    t = TOK_TILE
    sub = S5_TILE
    lr8, li8, p_re, p_im = tables
    p, pt = _scan_order()

    def body(x_ref, g_ref, w_ref, c_ref, s_ref, p_ref, pt_ref, b_ref, cm_ref, lr_ref, li_ref, pr_ref, pi_ref, d_ref,
             q_ref, k_ref, v_ref, gate_ref, u_ref, y_ref, start_ref, xs, xb, carry_re, carry_im):
        @pl.when(pl.program_id(0) == 0)
        def _():
            carry_re[...] = jnp.zeros_like(carry_re)
            carry_im[...] = jnp.zeros_like(carry_im)

        n, _ = _rms(x_ref[...])
        h = (n * g_ref[...]).astype(BF16)
        proj = _dot(h, w_ref[...])
        c = c_ref[...]
        s = s_ref[...]
        for hd in range(RET_HEADS):
            lo = hd * HEAD_DIM
            q_ref[:, lo:lo + HEAD_DIM] = _rope(proj[:, lo:lo + HEAD_DIM], c, s).astype(BF16)
            kh = proj[:, RET_WIDTH + lo:RET_WIDTH + lo + HEAD_DIM]
            k_ref[:, lo:lo + HEAD_DIM] = (_rope(kh, c, s) * K_SCALE).astype(BF16)
        v_ref[...] = proj[:, 2 * RET_WIDTH:3 * RET_WIDTH].astype(BF16)
        gate_ref[...] = proj[:, 3 * RET_WIDTH:4 * RET_WIDTH]
        u = proj[:, 4 * RET_WIDTH:]
        u_ref[...] = u
        for part in range(t // sub):
            rows = slice(part * sub, (part + 1) * sub)
            start_ref[part, 0:1, :] = carry_re[...]
            start_ref[part, 1:2, :] = carry_im[...]
            uv = u[rows]
            _s5_channels_to_states(_to_scan_order(p_ref, uv.astype(BF16)), b_ref, xs)
            _s5_scan_tile(xs, lr_ref, li_ref, pr_ref, pi_ref, carry_re, carry_im, conj=False, reverse=False)
            _s5_pack(xs, xb)
            y_ref[rows, :] = _from_scan_order(pt_ref, _s5_states_to_channels(xb, cm_ref), pieces=2) + d_ref[...] * uv

    half = lambda dt: jax.ShapeDtypeStruct((seq, RET_WIDTH), dt)
    return _call(
        body, "inproj_s5_fwd", (seq // t,),
        [_rows(t, D_MODEL), _full((1, D_MODEL)), _full((D_MODEL, IN_COLS)), _rows(t, HEAD_DIM), _rows(t, HEAD_DIM),
         _full((sub, sub)), _full((sub, sub)), _full(S5_TILES_SHAPE), _full(S5_TILES_SHAPE),
         _full((SUBLANES, N_STATES)), _full((SUBLANES, N_STATES)), _full((SUBLANES * S5_STEPS, N_STATES)),
         _full((SUBLANES * S5_STEPS, N_STATES)), _full((1, SSM_WIDTH))],
        [_rows(t, RET_WIDTH)] * 6 + [pl.BlockSpec((t // sub, 2, N_STATES), lambda i: (i, 0, 0))],
        [half(BF16), half(BF16), half(BF16), half(F32), half(F32), half(F32),
         jax.ShapeDtypeStruct((seq // sub, 2, N_STATES), F32)],
        [pltpu.VMEM((2 * S5_COLBLK, sub, LANES), F32), pltpu.VMEM((sub, 2 * N_STATES), BF16),
         pltpu.VMEM((1, N_STATES), F32), pltpu.VMEM((1, N_STATES), F32)],
        (x, gain, w_in, cos2, sin2, p, pt, b_blk, c_blk, lr8, li8, p_re, p_im, d_skip), exchange)


def _s5_bwd(u, dy, b_blk, c_blk, tables, d_skip, starts, exchange=None):
    seq = u.shape[0]
    t = S5_TILE
    nt = seq // t
    lr8, li8, p_re, p_im = tables
    p, pt = _scan_order()

    def body(u_ref, dy_ref, p_ref, pt_ref, b_ref, c_ref, lr_ref, li_ref, pr_ref, pi_ref, d_ref, start_ref,
             du_ref, dd_ref, dlam_ref, db_ref, dc_ref,
             xs, as_, xb, ab, carry_re, carry_im, acar_re, acar_im):
        @pl.when(pl.program_id(0) == 0)
        def _():
            acar_re[...] = jnp.zeros_like(acar_re)
            acar_im[...] = jnp.zeros_like(acar_im)
            dd_ref[...] = jnp.zeros_like(dd_ref)
            dlam_ref[...] = jnp.zeros_like(dlam_ref)
            db_ref[...] = jnp.zeros_like(db_ref)
            dc_ref[...] = jnp.zeros_like(dc_ref)

        uv = u_ref[...]
        dyv = dy_ref[...]
        ub = _to_scan_order(p_ref, uv.astype(BF16))
        dyb = _to_scan_order(p_ref, dyv.astype(BF16))
        carry_re[...] = start_ref[0, 0:1, :]
        carry_im[...] = start_ref[0, 1:2, :]
        _s5_channels_to_states(ub, b_ref, xs)
        in_re, in_im, _, _ = _s5_scan_tile(xs, lr_ref, li_ref, pr_ref, pi_ref, carry_re, carry_im, conj=False, reverse=False)
        _s5_channels_to_states(dyb, c_ref, as_)
        _, _, dl_re, dl_im = _s5_scan_tile(as_, lr_ref, li_ref, pr_ref, pi_ref, acar_re, acar_im, conj=True, reverse=True,
                                           prev_of=(xs, in_re, in_im))
        for b in range(S5_COLBLK):
            cols = slice(b * LANES, (b + 1) * LANES)
            dlam_ref[0, :, cols] += dl_re[b]
            dlam_ref[1, :, cols] += dl_im[b]
        _s5_pack(xs, xb)
        _s5_pack(as_, ab)
        du_ref[...] = (_from_scan_order(pt_ref, _s5_states_to_channels(ab, b_ref), pieces=2) + d_ref[...] * dyv).astype(BF16)
        dd_ref[...] += jnp.sum(dyv * uv, axis=0, keepdims=True)
        _s5_outer_acc(db_ref, ub, ab)
        _s5_outer_acc(dc_ref, dyb, xb)

    rev = _rows_rev(t, SSM_WIDTH, nt)
    vec = lambda: pltpu.VMEM((1, N_STATES), F32)
    outer = S5_TILES_SHAPE
    return _call(
        body, "s5_bwd", (nt,),
        [rev, rev, _full((t, t)), _full((t, t)), _full(S5_TILES_SHAPE), _full(S5_TILES_SHAPE),
         _full((SUBLANES, N_STATES)), _full((SUBLANES, N_STATES)), _full((SUBLANES * S5_STEPS, N_STATES)),
         _full((SUBLANES * S5_STEPS, N_STATES)), _full((1, SSM_WIDTH)),
         pl.BlockSpec((1, 2, N_STATES), lambda i: (nt - 1 - i, 0, 0))],
        [rev, _full((1, SSM_WIDTH)), _full((2, SUBLANES, N_STATES)), _full(outer), _full(outer)],
        [jax.ShapeDtypeStruct((seq, SSM_WIDTH), BF16), jax.ShapeDtypeStruct((1, SSM_WIDTH), F32),
         jax.ShapeDtypeStruct((2, SUBLANES, N_STATES), F32), jax.ShapeDtypeStruct(outer, F32), jax.ShapeDtypeStruct(outer, F32)],
        [pltpu.VMEM((2 * S5_COLBLK, t, LANES), F32), pltpu.VMEM((2 * S5_COLBLK, t, LANES), F32),
         pltpu.VMEM((t, 2 * N_STATES), BF16), pltpu.VMEM((t, 2 * N_STATES), BF16), vec(), vec(), vec(), vec()],
        (u, dy, p, pt, b_blk, c_blk, lr8, li8, p_re, p_im, d_skip, starts), exchange)


def _mix_fwd_step(ys_ref, yr_ref, x_ref, g_ref, wg_ref, wo_ref, x1_ref, mix_ref):
    g0 = _gelu(ys_ref[...]).astype(BF16)
    z = _dot(g0, wg_ref[...])
    glu = (z[:, :SSM_WIDTH] * _sigmoid(z[:, SSM_WIDTH:])).astype(BF16)
    mix = _dot(yr_ref[...], wo_ref[:RET_WIDTH, :]) + _dot(glu, wo_ref[RET_WIDTH:, :])
    mix_ref[...] = mix.astype(BF16)
    n, _ = _rms(mix)
    x1_ref[...] = x_ref[...] + n * g_ref[...]


def _ret_mix_fwd(q, k, v, gate, gn_gain, consts, y_ssm, x, gain, w_glu, w_out, exchange=None):
    seq = x.shape[0]
    cps = RET_CHUNKS_PER_STEP
    t = cps * RET_CHUNK
    dec, xi, zeta, gch = consts

    def body(q_ref, k_ref, v_ref, gate_ref, gain_ref, dec_ref, xi_ref, zeta_ref, gch_ref, ys_ref, x_ref, g_ref, wg_ref, wo_ref,
             y_ref, rprev_ref, x1_ref, mix_ref, state):
        @pl.when(pl.program_id(0) == 0)
        def _():
            state[...] = jnp.zeros_like(state)

        _ret_fwd_step(q_ref, k_ref, v_ref, gate_ref, gain_ref, dec_ref, xi_ref, zeta_ref, gch_ref, y_ref, rprev_ref, state)
        _mix_fwd_step(ys_ref, y_ref, x_ref, g_ref, wg_ref, wo_ref, x1_ref, mix_ref)

    cst = _full((RET_HEADS, RET_CHUNK, RET_CHUNK))
    return _call(
        body, "ret_mix_fwd", (seq // t,),
        [_rows(t, RET_WIDTH)] * 4 + [_full((1, RET_WIDTH)), cst, cst, cst, cst, _rows(t, SSM_WIDTH), _rows(t, D_MODEL),
                                     _full((1, D_MODEL)), _full((SSM_WIDTH, 2 * SSM_WIDTH)), _full((D_MODEL, D_MODEL))],
        [_rows(t, RET_WIDTH), pl.BlockSpec((cps, RET_HEADS, HEAD_DIM, HEAD_DIM), lambda i: (i, 0, 0, 0)),
         _rows(t, D_MODEL), _rows(t, D_MODEL)],
        [jax.ShapeDtypeStruct((seq, RET_WIDTH), BF16),
         jax.ShapeDtypeStruct((seq // RET_CHUNK, RET_HEADS, HEAD_DIM, HEAD_DIM), BF16),
         jax.ShapeDtypeStruct((seq, D_MODEL), F32), jax.ShapeDtypeStruct((seq, D_MODEL), BF16)],
        [pltpu.VMEM((RET_HEADS, HEAD_DIM, HEAD_DIM), F32)],
        (q, k, v, gate, gn_gain, dec, xi, zeta, gch, y_ssm, x, gain, w_glu, w_out), exchange)


def _mix_bwd(dx1, mix, gain, y_ssm, y_ret, w_glu, w_out, exchange=None):
    seq = dx1.shape[0]
    t = TOK_TILE

    def body(dx1_ref, mix_ref, g_ref, ys_ref, yr_ref, wg_ref, wo_ref, dyr_ref, dys_ref, dwo_ref, dwg_ref, dg_ref):
        @pl.when(pl.program_id(0) == 0)
        def _():
            dwo_ref[...] = jnp.zeros_like(dwo_ref)
            dwg_ref[...] = jnp.zeros_like(dwg_ref)
            dg_ref[...] = jnp.zeros_like(dg_ref)

        n, r = _rms(mix_ref[...].astype(F32))
        dmix, dgs = _rms_bwd(n, r, g_ref[...], dx1_ref[...])
        dg_ref[...] += dgs
        dmb = dmix.astype(BF16)
        dcat = _dot_nt(dmb, wo_ref[...])
        dyr_ref[...] = dcat[:, :RET_WIDTH]
        dglu = dcat[:, RET_WIDTH:]
        ys = ys_ref[...]
        g0 = _gelu(ys).astype(BF16)
        z = _dot(g0, wg_ref[...])
        a = z[:, :SSM_WIDTH]
        sb = _sigmoid(z[:, SSM_WIDTH:])
        dwo_ref[:RET_WIDTH, :] += _dot_tn(yr_ref[...], dmb)
        dwo_ref[RET_WIDTH:, :] += _dot_tn((a * sb).astype(BF16), dmb)
        dz = jnp.concatenate([dglu * sb, dglu * a * sb * (1.0 - sb)], axis=1).astype(BF16)
        dwg_ref[...] += _dot_tn(g0, dz)
        dys_ref[...] = _dot_nt(dz, wg_ref[...]) * _gelu_grad(ys)

    half = jax.ShapeDtypeStruct((seq, RET_WIDTH), F32)
    return _call(
        body, "mix_bwd", (seq // t,),
        [_rows(t, D_MODEL), _rows(t, D_MODEL), _full((1, D_MODEL)), _rows(t, SSM_WIDTH), _rows(t, RET_WIDTH),
         _full((SSM_WIDTH, 2 * SSM_WIDTH)), _full((D_MODEL, D_MODEL))],
        [_rows(t, RET_WIDTH), _rows(t, SSM_WIDTH), _full((D_MODEL, D_MODEL)), _full((SSM_WIDTH, 2 * SSM_WIDTH)),
         _full((1, D_MODEL))],
        [half, half, jax.ShapeDtypeStruct((D_MODEL, D_MODEL), F32), jax.ShapeDtypeStruct((SSM_WIDTH, 2 * SSM_WIDTH), F32),
         jax.ShapeDtypeStruct((1, D_MODEL), F32)],
        [], (dx1, mix, gain, y_ssm, y_ret, w_glu, w_out), exchange)


def _mlp_a(x1, target, gain_pre, gain_post, w1, w2):
    seq = x1.shape[0]
    t = MLP_TILE

    def body(x1_ref, tg_ref, gp_ref, gq_ref, w1_ref, w2_ref, df_ref, dx2_ref, dw2_ref, dgq_ref, sq_ref):
        @pl.when(pl.program_id(0) == 0)
        def _():
            dw2_ref[...] = jnp.zeros_like(dw2_ref)
            dgq_ref[...] = jnp.zeros_like(dgq_ref)
            sq_ref[...] = jnp.zeros_like(sq_ref)

        x1v = x1_ref[...]
        n3, _ = _rms(x1v)
        h = (n3 * gp_ref[...]).astype(BF16)
        rl = jnp.maximum(_dot(h, w1_ref[...]), 0.0)
        act = (rl * rl).astype(BF16)
        n4, r4 = _rms(_dot(act, w2_ref[...]))
        gq = gq_ref[...]
        err = x1v + n4 * gq - tg_ref[...]
        sq_ref[...] += jnp.sum(err * err, axis=0, keepdims=True)
        dx2 = err * (1.0 / D_MODEL)
        dx2_ref[...] = dx2
        dm, dgs = _rms_bwd(n4, r4, gq, dx2)
        dgq_ref[...] += dgs
        dmb = dm.astype(BF16)
        dw2_ref[...] += _dot_tn(act, dmb)
        df_ref[...] = (_dot_nt(dmb, w2_ref[...]) * (2.0 * rl)).astype(BF16)

    return pl.pallas_call(
        body, name="mlp_a", grid=(seq // t,),
        in_specs=[_rows(t, D_MODEL), _rows(t, D_MODEL), _full((1, D_MODEL)), _full((1, D_MODEL)),
                  _full((D_MODEL, D_FF)), _full((D_FF, D_MODEL))],
        out_specs=[_rows(t, D_FF), _rows(t, D_MODEL), _full((D_FF, D_MODEL)), _full((1, D_MODEL)), _full((1, D_MODEL))],
        out_shape=[jax.ShapeDtypeStruct((seq, D_FF), BF16), jax.ShapeDtypeStruct((seq, D_MODEL), F32),
                   jax.ShapeDtypeStruct((D_FF, D_MODEL), F32), jax.ShapeDtypeStruct((1, D_MODEL), F32),
                   jax.ShapeDtypeStruct((1, D_MODEL), F32)],
        compiler_params=_params("arbitrary"),
    )(x1, target, gain_pre, gain_post, w1, w2)


def _mlp_b(df, dx2, x1, gain_pre, w1):
    seq = x1.shape[0]
    t = TOK_TILE

    def body(df_ref, dx2_ref, x1_ref, gp_ref, w1_ref, dx1_ref, dw1_ref, dgp_ref):
        @pl.when(pl.program_id(0) == 0)
        def _():
            dw1_ref[...] = jnp.zeros_like(dw1_ref)
            dgp_ref[...] = jnp.zeros_like(dgp_ref)

        n3, r3 = _rms(x1_ref[...])
        gp = gp_ref[...]
        h = (n3 * gp).astype(BF16)
        dfv = df_ref[...]
        dw1_ref[...] += _dot_tn(h, dfv)
        dx, dgs = _rms_bwd(n3, r3, gp, _dot_nt(dfv, w1_ref[...]))
        dgp_ref[...] += dgs
        dx1_ref[...] = dx2_ref[...] + dx

    return pl.pallas_call(
        body, name="mlp_b", grid=(seq // t,),
        in_specs=[_rows(t, D_FF), _rows(t, D_MODEL), _rows(t, D_MODEL), _full((1, D_MODEL)), _full((D_MODEL, D_FF))],
        out_specs=[_rows(t, D_MODEL), _full((D_MODEL, D_FF)), _full((1, D_MODEL))],
        out_shape=[jax.ShapeDtypeStruct((seq, D_MODEL), F32), jax.ShapeDtypeStruct((D_MODEL, D_FF), F32),
                   jax.ShapeDtypeStruct((1, D_MODEL), F32)],
        compiler_params=_params("arbitrary"),
    )(df, dx2, x1, gain_pre, w1)


def _local_step(x, target, small, weights, ids=None, late_state=None):
    cos2, sin2 = _rope_tables(x.shape[0])
    ret_consts = _ret_consts()

    s5_names = ("ssm_lambda_re", "ssm_lambda_im", "ssm_log_dt", "ssm_b_re", "ssm_b_im", "ssm_c_re", "ssm_c_im")
    (lbr, lbi, b_tiles, c_tiles), disc_vjp = jax.vjp(_s5_discretise, *[small[name] for name in s5_names])
    tables = _s5_tables(lbr, lbi)
    b_blk = b_tiles.astype(BF16)
    c_blk = c_tiles.astype(BF16)
    d_skip = small["ssm_d"].reshape(1, SSM_WIDTH)
    gn_gain = small["ret_gn_gain"].reshape(1, RET_WIDTH)
    g_mix_pre = small["norm_mix_pre"].reshape(1, D_MODEL)
    g_mix_post = small["norm_mix_post"].reshape(1, D_MODEL)
    g_mlp_pre = small["norm_mlp_pre"].reshape(1, D_MODEL)
    g_mlp_post = small["norm_mlp_post"].reshape(1, D_MODEL)

    dist = ids is not None
    w_in, w_glu, w_out, w_ff1, w_ff2 = weights
    (q, k, v, gate, u, y_ssm, starts), got = _inproj_s5_fwd(
        x, g_mix_pre, w_in, cos2, sin2, b_blk, c_blk, tables, d_skip,
        _x_gather_send(MID + LATE, [w_glu, w_out, w_ff1, w_ff2], to_both=MID) if dist else None)
    if dist:
        w_glu, w_out, w_ff1, w_ff2 = got
    (y_ret, rprev, x1, mix), got = _ret_mix_fwd(q, k, v, gate, gn_gain, ret_consts, y_ssm, x, g_mix_post, w_glu, w_out,
                                                _x_gather_pass(LATE, [w_ff1, w_ff2]) if dist else None)
    if dist:
        w_ff1, w_ff2 = got
    df, dx2, d_ff2, dg_mlp_post, sq = _mlp_a(x1, target, g_mlp_pre, g_mlp_post, w_ff1, w_ff2)
    dx1, d_ff1, dg_mlp_pre = _mlp_b(df, dx2, x1, g_mlp_pre, w_ff1)
    (dy_ret, dy_ssm, d_out, d_glu, dg_mix_post), got = _mix_bwd(dx1, mix, g_mix_post, y_ssm, y_ret, w_glu, w_out,
                                                               _x_pair(LATE, [d_ff1, d_ff2]) if dist else None)
    if dist:
        sums = _pair_sum(ids, LATE, [d_ff1, d_ff2], got)
    (du, dd, dlam8, db_tiles, dc_tiles), got = _s5_bwd(
        u, dy_ssm, b_blk, c_blk, tables, d_skip, starts,
        _x_join(_x_chip(LATE, sums), _x_pair(MID, [d_glu, d_out])) if dist else None)
    if dist:
        parts = _chip_sum(ids, LATE, sums, got[:len(LATE)])
        sums = _pair_sum(ids, MID, [d_glu, d_out], got[len(LATE):])
    (dq, dk, dv, dgate, dgn), got = _ret_bwd(q, k, v, gate, gn_gain, ret_consts, rprev, dy_ret, cos2, sin2,
                                             _x_join(_x_share(LATE, parts), _x_chip(MID, sums)) if dist else None)
    if dist:
        late_grads = got[:len(LATE)]
        mid_parts = _chip_sum(ids, MID, sums, got[len(LATE):])
    adam = (late_state[0], late_grads, late_state[1], late_state[2]) if dist else ()
    (dx, d_in, dg_mix_pre), late_stepped = _inproj_bwd(x, g_mix_pre, w_in, [dq, dk, dv, dgate, du], dx1, adam)

    dlam = jnp.sum(dlam8, axis=1)
    s5_grads = disc_vjp((dlam[0].reshape(SSM_GROUPS, SSM_STATE), dlam[1].reshape(SSM_GROUPS, SSM_STATE), db_tiles, dc_tiles))
    small_grads = {
        "norm_mix_pre": dg_mix_pre, "norm_mix_post": dg_mix_post, "ret_gn_gain": dgn, "ssm_d": dd,
        "norm_mlp_pre": dg_mlp_pre, "norm_mlp_post": dg_mlp_post, "loss_sum": 0.5 * jnp.sum(sq) / D_MODEL,
    }
    small_grads.update(dict(zip(s5_names, s5_grads)))
    if dist:
        return dx, d_in, mid_parts, late_grads, late_stepped, small_grads
    return dx, (d_in, d_glu, d_out, d_ff1, d_ff2), small_grads


BIG = (
    ("w_in", D_MODEL, IN_COLS, 1),
    ("w_glu", SSM_WIDTH, 2 * SSM_WIDTH, 1),
    ("w_out", D_MODEL, D_MODEL, 0),
    ("w_ff1", D_MODEL, D_FF, 1),
    ("w_ff2", D_FF, D_MODEL, 0),
)
SMALL = (
    ("norm_mix_pre", (D_MODEL,)), ("norm_mix_post", (D_MODEL,)), ("ret_gn_gain", (RET_WIDTH,)),
    ("ssm_lambda_re", (SSM_GROUPS, SSM_STATE)), ("ssm_lambda_im", (SSM_GROUPS, SSM_STATE)), ("ssm_log_dt", (SSM_GROUPS,)),
    ("ssm_b_re", (SSM_GROUPS, SSM_STATE, SSM_GROUP)), ("ssm_b_im", (SSM_GROUPS, SSM_STATE, SSM_GROUP)),
    ("ssm_c_re", (SSM_GROUPS, SSM_GROUP, SSM_STATE)), ("ssm_c_im", (SSM_GROUPS, SSM_GROUP, SSM_STATE)),
    ("ssm_d", (SSM_WIDTH,)), ("norm_mlp_pre", (D_MODEL,)), ("norm_mlp_post", (D_MODEL,)),
)
SMALL_ROWS = 1152
EXCH_TILES = 8


PACKED = SMALL + (("loss_sum", (1,)),)


def _packed_rows(shape):
    return -(-math.prod(shape) // (SUBLANES * LANES)) * SUBLANES


def _pack_small(tree):
    rows = []
    for name, shape in PACKED:
        size = math.prod(shape)
        flat = tree[name].reshape(-1).astype(F32) if name in tree else jnp.zeros((size,), F32)
        rows.append(jnp.pad(flat, (0, _packed_rows(shape) * LANES - size)).reshape(-1, LANES))
    used = sum(r.shape[0] for r in rows)
    rows.append(jnp.zeros((SMALL_ROWS - used, LANES), F32))
    return jnp.concatenate(rows, axis=0)


def _unpack_small(packed):
    out, row = {}, 0
    for name, shape in PACKED:
        size = math.prod(shape)
        out[name] = packed[row:row + _packed_rows(shape)].reshape(-1)[:size].reshape((1,) + shape)
        row += _packed_rows(shape)
    return out


def _half_shape(r, c, axis):
    return (r // 2, c) if axis == 1 else (r, c // 2)


def _region_shape(r, c, axis):
    return (r // 2, c // N_CHIPS) if axis == 1 else (r // N_CHIPS, c // 2)


def _shard_shape(r, c, axis):
    return (r, c // N_CHIPS) if axis == 1 else (r // N_CHIPS, c)


def _ds(start, size):
    return pl.ds(pl.multiple_of(start * size, size), size)


def _region_of_full(ref, r, c, axis, shard, half):
    if axis == 1:
        return ref.at[_ds(half, r // 2), _ds(shard, c // N_CHIPS)]
    return ref.at[_ds(shard, r // N_CHIPS), _ds(half, c // 2)]


def _half_of_full(ref, r, c, axis, half):
    if axis == 1:
        return ref.at[_ds(half, r // 2), :]
    return ref.at[:, _ds(half, c // 2)]


def _half_of_shard(ref, r, c, axis, half):
    if axis == 1:
        return ref.at[_ds(half, r // 2), :]
    return ref.at[:, _ds(half, c // 2)]


def _region_of_half(ref, r, c, axis, shard):
    if axis == 1:
        return ref.at[:, _ds(shard, c // N_CHIPS)]
    return ref.at[_ds(shard, r // N_CHIPS), :]


def _place():
    x, y, c = lax.axis_index("x"), lax.axis_index("y"), lax.axis_index("c")
    chips = [(1 - x, y), (x, 1 - y), (1 - x, 1 - y)]
    return x, y, c, chips


LATE = (3, 4)
MID = (1, 2)
FIRST = (0,)
SMALL_HALF = (SMALL_ROWS // 2, LANES)


def _remote(src, dst, send_sem, recv_sem, to):
    return pltpu.make_async_remote_copy(src_ref=src, dst_ref=dst, send_sem=send_sem, recv_sem=recv_sem,
                                        device_id=to, device_id_type=MESH)


def _same(arrays):
    return [jax.ShapeDtypeStruct(a.shape, a.dtype) for a in arrays]


def _x_gather_send(ws, fulls, to_both=()):
    def copies(ins, outs, send_sems, recv_sems):
        x, y, c, chips = _place()
        out, at = [], 0
        for j, w in enumerate(ws):
            _, r, cc, axis = BIG[w]
            mine = _region_of_full(outs[j], r, cc, axis, 2 * x + y, c)
            for other_core in range(2 if w in to_both else 1):
                to_core = (1 - c) if other_core else c
                out += [_remote(mine, mine, send_sems.at[at + k], recv_sems.at[at + k], (cx, cy, to_core))
                        for k, (cx, cy) in enumerate(chips)]
                at += 3
        return out

    n = sum(6 if w in to_both else 3 for w in ws)
    return _Exchange(fulls, _same(fulls), {j: j for j in range(len(ws))}, n, copies)


def _x_gather_pass(ws, fulls):
    def copies(ins, outs, send_sems, recv_sems):
        x, y, c, chips = _place()
        out = []
        for j, w in enumerate(ws):
            _, r, cc, axis = BIG[w]
            for k, (cx, cy) in enumerate(chips):
                landed = _region_of_full(outs[j], r, cc, axis, 2 * cx + cy, c)
                out.append(_remote(landed, landed, send_sems.at[3 * j + k], recv_sems.at[3 * j + k], (x, y, 1 - c)))
        return out

    return _Exchange(fulls, _same(fulls), {j: j for j in range(len(ws))}, 3 * len(ws), copies)


def _x_pair(ws, grads, small=None):
    def copies(ins, outs, send_sems, recv_sems):
        x, y, c, _ = _place()
        out = []
        for j, w in enumerate(ws):
            _, r, cc, axis = BIG[w]
            out.append(_remote(_half_of_full(ins[j], r, cc, axis, 1 - c), outs[j], send_sems.at[j], recv_sems.at[j], (x, y, 1 - c)))
        if small is not None:
            j = len(ws)
            out.append(_remote(ins[j].at[_ds(1 - c, SMALL_ROWS // 2), :], outs[j], send_sems.at[j], recv_sems.at[j], (x, y, 1 - c)))
        return out

    shapes = [jax.ShapeDtypeStruct(_half_shape(*BIG[w][1:]), F32) for w in ws]
    extra = [] if small is None else [small]
    return _Exchange(list(grads) + extra, shapes + [jax.ShapeDtypeStruct(SMALL_HALF, F32)] * len(extra), {},
                     len(ws) + len(extra), copies)


def _x_chip(ws, sums_bf16, small_sum=None):
    def copies(ins, outs, send_sems, recv_sems):
        x, y, c, chips = _place()
        out = []
        for j, w in enumerate(ws):
            _, r, cc, axis = BIG[w]
            out += [_remote(_region_of_half(ins[j], r, cc, axis, 2 * cx + cy), outs[j].at[k],
                            send_sems.at[3 * j + k], recv_sems.at[3 * j + k], (cx, cy, c)) for k, (cx, cy) in enumerate(chips)]
        if small_sum is not None:
            j = len(ws)
            out += [_remote(ins[j], outs[j].at[k], send_sems.at[3 * j + k], recv_sems.at[3 * j + k], (cx, cy, c))
                    for k, (cx, cy) in enumerate(chips)]
        return out

    shapes = [jax.ShapeDtypeStruct((3,) + _region_shape(*BIG[w][1:]), BF16) for w in ws]
    extra = [] if small_sum is None else [small_sum]
    return _Exchange(list(sums_bf16) + extra, shapes + [jax.ShapeDtypeStruct((3,) + SMALL_HALF, F32)] * len(extra), {},
                     3 * (len(ws) + len(extra)), copies)


def _x_share(ws, shards, small=None):
    def copies(ins, outs, send_sems, recv_sems):
        x, y, c, _ = _place()
        out = []
        for j, w in enumerate(ws):
            _, r, cc, axis = BIG[w]
            mine = _half_of_shard(outs[j], r, cc, axis, c)
            out.append(_remote(mine, mine, send_sems.at[j], recv_sems.at[j], (x, y, 1 - c)))
        if small is not None:
            j = len(ws)
            mine = outs[j].at[_ds(c, SMALL_ROWS // 2), :]
            out.append(_remote(mine, mine, send_sems.at[j], recv_sems.at[j], (x, y, 1 - c)))
        return out

    arrays = list(shards) + ([] if small is None else [small])
    return _Exchange(arrays, _same(arrays), {j: j for j in range(len(arrays))}, len(arrays), copies)


class _Offset:
    def __init__(self, sems, base):
        self._sems, self._base = sems, base

    @property
    def at(self):
        return self

    def __getitem__(self, i):
        return self._sems.at[self._base + i]


def _x_join(a, b):
    ka_in, ka_out = len(a.operands), len(a.out_shapes)

    def copies(ins, outs, send_sems, recv_sems):
        return (a.copies(ins[:ka_in], outs[:ka_out], send_sems, recv_sems)
                + b.copies(ins[ka_in:], outs[ka_out:], _Offset(send_sems, a.n), _Offset(recv_sems, a.n)))

    aliases = dict(a.aliases)
    aliases.update({ka_in + i: ka_out + o for i, o in b.aliases.items()})
    return _Exchange(a.operands + b.operands, a.out_shapes + b.out_shapes, aliases, a.n + b.n, copies)


def _run_exchange(name, exchange):
    k_in, k_out = len(exchange.operands), len(exchange.out_shapes)

    def body(*refs):
        copies = exchange.copies(refs[:k_in], refs[k_in:k_in + k_out], refs[-2], refs[-1])
        for cp in copies:
            cp.start()
        for cp in copies:
            cp.wait()

    return list(pl.pallas_call(
        body, name=name, in_specs=[ANY] * k_in, out_specs=[ANY] * k_out, out_shape=exchange.out_shapes,
        scratch_shapes=[pltpu.SemaphoreType.DMA((exchange.n,)), pltpu.SemaphoreType.DMA((exchange.n,))],
        input_output_aliases=dict(exchange.aliases),
    )(*exchange.operands))


def _gather_now(name, ws, fulls):
    send, onward = _x_gather_send(ws, fulls), _x_gather_pass(ws, fulls)
    k = len(fulls)

    def body(*refs):
        ins, outs, (send_a, recv_a, send_b, recv_b) = refs[:k], refs[k:2 * k], refs[2 * k:]
        first = send.copies(ins, outs, send_a, recv_a)
        second = onward.copies(ins, outs, send_b, recv_b)
        for cp in first:
            cp.start()
        for landed, cp in zip(first, second):
            landed.wait()
            cp.start()
        for cp in second:
            cp.wait()

    sems = lambda n: pltpu.SemaphoreType.DMA((n,))
    return list(pl.pallas_call(
        body, name=name, in_specs=[ANY] * k, out_specs=[ANY] * k, out_shape=send.out_shapes,
        scratch_shapes=[sems(send.n), sems(send.n), sems(onward.n), sems(onward.n)],
        input_output_aliases={j: j for j in range(k)},
    )(*fulls))


def _tile_rows(shape):
    return (shape[0] // EXCH_TILES, shape[1])


def _cast_place(ids, shards):
    def body(ids_ref, *refs):
        for j in range(len(BIG)):
            refs[len(BIG) + j][...] = refs[j][...].astype(BF16)

    def out_spec(w):
        _, r, cc, axis = BIG[w]
        tile = _tile_rows(_shard_shape(r, cc, axis))
        if axis == 1:
            return pl.BlockSpec(tile, lambda i, ids: (i, ids[1]))
        return pl.BlockSpec(tile, lambda i, ids: (ids[1] * EXCH_TILES + i, 0))

    return list(pl.pallas_call(
        body, name="cast_place",
        grid_spec=pltpu.PrefetchScalarGridSpec(
            num_scalar_prefetch=1, grid=(EXCH_TILES,),
            in_specs=[pl.BlockSpec(_tile_rows(_shard_shape(r, cc, axis)), lambda i, ids: (i, 0)) for _, r, cc, axis in BIG],
            out_specs=[out_spec(w) for w in range(len(BIG))]),
        out_shape=[jax.ShapeDtypeStruct((r, cc), BF16) for _, r, cc, _ in BIG],
        compiler_params=pltpu.CompilerParams(dimension_semantics=("parallel",)),
    )(ids, *shards))


def _pair_sum(ids, ws, grads, got, small=None):
    nb = len(ws)
    n = nb + (small is not None)
    halves = [_half_shape(*BIG[w][1:]) for w in ws] + [SMALL_HALF] * (n - nb)

    def body(ids_ref, *refs):
        for j in range(n):
            total = refs[j][...] + refs[n + j][...]
            refs[2 * n + j][...] = total.astype(refs[2 * n + j].dtype)

    def mine_spec(j):
        tile = _tile_rows(halves[j])
        if j < nb and BIG[ws[j]][3] == 0:
            return pl.BlockSpec(tile, lambda i, ids: (i, ids[0]))
        return pl.BlockSpec(tile, lambda i, ids: (ids[0] * EXCH_TILES + i, 0))

    plain = lambda j: pl.BlockSpec(_tile_rows(halves[j]), lambda i, ids: (i, 0))
    outs = pl.pallas_call(
        body, name="pair_sum",
        grid_spec=pltpu.PrefetchScalarGridSpec(
            num_scalar_prefetch=1, grid=(EXCH_TILES,),
            in_specs=[mine_spec(j) for j in range(n)] + [plain(j) for j in range(n)],
            out_specs=[plain(j) for j in range(n)]),
        out_shape=[jax.ShapeDtypeStruct(h, BF16 if j < nb else F32) for j, h in enumerate(halves)],
        compiler_params=pltpu.CompilerParams(dimension_semantics=("parallel",)),
    )(ids, *grads, *([] if small is None else [small]), *got)
    return list(outs)


def _chip_sum(ids, ws, sums, arrived):
    nb, n = len(ws), len(sums)
    regions = [_region_shape(*BIG[w][1:]) for w in ws] + [SMALL_HALF] * (n - nb)

    def body(ids_ref, *refs):
        for j in range(n):
            own, got, out = refs[j], refs[n + j], refs[2 * n + j]
            if j < nb:
                out[...] = ((own[...].astype(F32) + got[0].astype(F32)) + got[1].astype(F32)) + got[2].astype(F32)
            else:
                out[...] = (own[...] + got[1]) + (got[0] + got[2])

    def own_spec(j):
        tile = _tile_rows(regions[j])
        if j >= nb:
            return pl.BlockSpec(tile, lambda i, ids: (i, 0))
        if BIG[ws[j]][3] == 1:
            return pl.BlockSpec(tile, lambda i, ids: (i, ids[1]))
        return pl.BlockSpec(tile, lambda i, ids: (ids[1] * EXCH_TILES + i, 0))

    def out_spec(j):
        tile = _tile_rows(regions[j])
        if j < nb and BIG[ws[j]][3] == 0:
            return pl.BlockSpec(tile, lambda i, ids: (i, ids[0]))
        return pl.BlockSpec(tile, lambda i, ids: (ids[0] * EXCH_TILES + i, 0))

    got_spec = lambda j: pl.BlockSpec((3,) + _tile_rows(regions[j]), lambda i, ids: (0, i, 0))
    return list(pl.pallas_call(
        body, name="chip_sum",
        grid_spec=pltpu.PrefetchScalarGridSpec(
            num_scalar_prefetch=1, grid=(EXCH_TILES,),
            in_specs=[own_spec(j) for j in range(n)] + [got_spec(j) for j in range(n)],
            out_specs=[out_spec(j) for j in range(n)]),
        out_shape=[jax.ShapeDtypeStruct(_shard_shape(*BIG[w][1:]), F32) for w in ws]
        + [jax.ShapeDtypeStruct((SMALL_ROWS, LANES), F32)] * (n - nb),
        compiler_params=pltpu.CompilerParams(dimension_semantics=("parallel",)),
    )(ids, *sums, *arrived))


def _adamw_body(n):
    c1 = 1.0 - ADAM_B1 ** ADAM_STEP
    c2 = 1.0 - ADAM_B2 ** ADAM_STEP

    def body(*refs):
        for j in range(n):
            w, g, m, v = (refs[k * n + j][...] for k in range(4))
            m = ADAM_B1 * m + (1.0 - ADAM_B1) * g
            v = ADAM_B2 * v + (1.0 - ADAM_B2) * (g * g)
            refs[4 * n + j][...] = -ADAM_LR * ((m / c1) / (jnp.sqrt(v / c2) + ADAM_EPS) + ADAM_WD * w)
            refs[5 * n + j][...] = m
            refs[6 * n + j][...] = v

    return body


def _adamw(name, ws, gs, ms, vs):
    n = len(ws)
    specs = [pl.BlockSpec(_tile_rows(w.shape), lambda i: (i, 0)) for w in ws]
    shapes = [jax.ShapeDtypeStruct(w.shape, F32) for w in ws]
    out, _ = _call(_adamw_body(n), name, (EXCH_TILES,), specs * 4, specs * 3, shapes * 3, [], (*ws, *gs, *ms, *vs))
    return out[:n], out[n:2 * n], out[2 * n:]


def _adamw_whole(name, ws, gs, ms, vs):
    n = len(ws)
    out = pl.pallas_call(_adamw_body(n), name=name, out_shape=[jax.ShapeDtypeStruct(w.shape, F32) for w in ws] * 3)(*ws, *gs, *ms, *vs)
    return out[:n], out[n:2 * n], out[2 * n:]


def kernel(x, norm_mix_pre, norm_mix_post, w_in, ret_gn_gain, ssm_lambda_re, ssm_lambda_im, ssm_log_dt, ssm_b_re, ssm_b_im, ssm_c_re, ssm_c_im, ssm_d, w_glu, w_out, norm_mlp_pre, norm_mlp_post, w_ff1, w_ff2, loss_target, m_norm_mix_pre, m_norm_mix_post, m_w_in, m_ret_gn_gain, m_ssm_lambda_re, m_ssm_lambda_im, m_ssm_log_dt, m_ssm_b_re, m_ssm_b_im, m_ssm_c_re, m_ssm_c_im, m_ssm_d, m_w_glu, m_w_out, m_norm_mlp_pre, m_norm_mlp_post, m_w_ff1, m_w_ff2, v_norm_mix_pre, v_norm_mix_post, v_w_in, v_ret_gn_gain, v_ssm_lambda_re, v_ssm_lambda_im, v_ssm_log_dt, v_ssm_b_re, v_ssm_b_im, v_ssm_c_re, v_ssm_c_im, v_ssm_d, v_w_glu, v_w_out, v_norm_mlp_pre, v_norm_mlp_post, v_w_ff1, v_w_ff2):
    given = dict(locals())
    order = ["norm_mix_pre", "norm_mix_post", "w_in", "ret_gn_gain", "ssm_lambda_re", "ssm_lambda_im", "ssm_log_dt",
             "ssm_b_re", "ssm_b_im", "ssm_c_re", "ssm_c_im", "ssm_d", "w_glu", "w_out", "norm_mlp_pre", "norm_mlp_post",
             "w_ff1", "w_ff2"]
    big_names = [name for name, _, _, _ in BIG]
    ids = jnp.stack([lax.axis_index("c"), 2 * lax.axis_index("x") + lax.axis_index("y")]).astype(jnp.int32)

    weights = _cast_place(ids, [given[name][0] for name in big_names])
    w_in = _gather_now("gather_w_in", FIRST, weights[:1])
    small_w = {name: given[name][0] for name, _ in SMALL}
    late_state = [[given[prefix + big_names[w]][0] for w in LATE] for prefix in ("", "m_", "v_")]
    dx, d_in, mid_parts, late_grads, late_stepped, small_grads = _local_step(
        x[0], loss_target[0], small_w, w_in + weights[1:], ids, late_state)

    small_packed = _pack_small(small_grads)
    got = _run_exchange("pair_exchange", _x_join(_x_share(MID, mid_parts), _x_pair(FIRST, [d_in], small_packed)))
    mid_grads, got = got[:len(MID)], got[len(MID):]
    sums = _pair_sum(ids, FIRST, [d_in], got, small_packed)
    arrived = _run_exchange("chip_exchange", _x_chip(FIRST, sums[:-1], sums[-1]))
    parts = _chip_sum(ids, FIRST, sums, arrived)
    g_in, g_small = _run_exchange("pair_share", _x_share(FIRST, parts[:-1], parts[-1]))
    big_grads = [g_in] + mid_grads + late_grads
    rest = lambda prefix: [given[prefix + big_names[w]][0] for w in FIRST + MID]
    rest_out = _adamw("adamw_big", rest(""), big_grads[:len(FIRST + MID)], rest("m_"), rest("v_"))
    big_out = [list(rest_out[k]) + list(late_stepped[k]) for k in range(3)]
    small_names = [name for name, _ in SMALL]
    small_tree = _unpack_small(g_small)
    small_grads = [small_tree[name] for name in small_names]
    small = lambda prefix: [given[prefix + name] for name in small_names]
    small_out = _adamw_whole("adamw_small", small(""), small_grads, small("m_"), small("v_"))

    def in_order(big_arrays, small_arrays):
        tree = {name: a[None] for name, a in zip(big_names, big_arrays)}
        tree.update(zip(small_names, small_arrays))
        return [tree[name] for name in order]

    loss = small_tree["loss_sum"].reshape(())
    return (loss, dx[None], *in_order(big_grads, small_grads), *in_order(big_out[0], small_out[0]),
            *in_order(big_out[1], small_out[1]), *in_order(big_out[2], small_out[2]))
```

```python
import math

import jax
import jax.numpy as jnp
import numpy as np
from jax import lax
from jax.experimental import pallas as pl
from jax.experimental.pallas import tpu as pltpu

F32 = jnp.float32
BF16 = jnp.bfloat16

D_MODEL = 1024
RET_WIDTH = 512
RET_HEADS = 4
HEAD_DIM = 128
RET_CHUNK = 128
ROPE_BASE = 10000.0
SSM_WIDTH = 512
SSM_GROUP = 16
SSM_GROUPS = 32
SSM_STATE = 64
N_STATES = SSM_GROUPS * SSM_STATE
D_FF = 4096
IN_COLS = 4 * RET_WIDTH + SSM_WIDTH
NORM_EPS = 1e-6
K_SCALE = HEAD_DIM ** -0.5

ADAM_LR = 0.001
ADAM_B1 = 0.9
ADAM_B2 = 0.999
ADAM_EPS = 1e-08
ADAM_WD = 0.01
ADAM_STEP = 10

LANES = 128
SUBLANES = 8
VMEM_LIMIT = 56 * 2 ** 20

TOK_TILE = 512
MLP_CHUNK = 1024
MLP_TILE = 256
RET_CHUNKS_PER_STEP = 4
S5_TILE = 256
S5_STEPS = S5_TILE // SUBLANES
S5_COLBLK = N_STATES // LANES
S5_GROUP = 4

N_CHIPS = 4
MESH = pl.DeviceIdType.MESH


def _dot(a, b):
    return jnp.dot(a, b, preferred_element_type=F32)


def _dot_nt(a, b):
    return lax.dot_general(a, b, (((1,), (1,)), ((), ())), preferred_element_type=F32)


def _dot_tn(a, b):
    return lax.dot_general(a, b, (((0,), (0,)), ((), ())), preferred_element_type=F32)


def _sigmoid(x):
    return 1.0 / (1.0 + jnp.exp(-x))


_GELU_C = math.sqrt(2.0 / math.pi)


def _gelu(x):
    return 0.5 * x * (1.0 + jnp.tanh(_GELU_C * (x + 0.044715 * (x * x * x))))


def _gelu_grad(x):
    t = jnp.tanh(_GELU_C * (x + 0.044715 * (x * x * x)))
    return 0.5 * (1.0 + t) + 0.5 * x * (1.0 - t * t) * (_GELU_C * (1.0 + 3.0 * 0.044715 * (x * x)))


def _rms(x):
    r = lax.rsqrt(jnp.mean(x * x, axis=-1, keepdims=True) + NORM_EPS)
    return x * r, r


def _rms_bwd(n, r, gain, dy):
    dn = dy * gain
    dx = r * (dn - n * jnp.mean(dn * n, axis=-1, keepdims=True))
    return dx, jnp.sum(dy * n, axis=0, keepdims=True)


def _full(shape):
    nd = len(shape)
    return pl.BlockSpec(shape, lambda i, _n=nd: (0,) * _n, pipeline_mode=pl.Buffered(1))


def _rows(tile, width):
    return pl.BlockSpec((tile, width), lambda i: (i, 0))


def _rows_rev(tile, width, n):
    return pl.BlockSpec((tile, width), lambda i, _n=n: (_n - 1 - i, 0))


def _params(sem):
    return pltpu.CompilerParams(dimension_semantics=(sem,), vmem_limit_bytes=VMEM_LIMIT)


ANY = pl.BlockSpec(memory_space=pl.ANY)


class _Exchange:
    def __init__(self, operands, out_shapes, aliases, n, copies):
        self.operands, self.out_shapes, self.aliases, self.n, self.copies = list(operands), list(out_shapes), aliases, n, copies


def _call(body, name, grid, in_specs, out_specs, out_shape, scratch_shapes, args, exchange=None):
    if exchange is None:
        outs = pl.pallas_call(body, name=name, grid=grid, in_specs=in_specs, out_specs=out_specs, out_shape=out_shape,
                              scratch_shapes=scratch_shapes, compiler_params=_params("arbitrary"))(*args)
        return list(outs), []
    n_in, n_out, n_scr = len(in_specs), len(out_specs), len(scratch_shapes)
    k_in, k_out = len(exchange.operands), len(exchange.out_shapes)
    last = grid[0] - 1

    def hosted(*refs):
        own_in, rest = refs[:n_in], refs[n_in:]
        ex_in, rest = rest[:k_in], rest[k_in:]
        own_out, rest = rest[:n_out], rest[n_out:]
        ex_out, rest = rest[:k_out], rest[k_out:]
        own_scr, (send_sems, recv_sems) = rest[:n_scr], rest[n_scr:]

        @pl.when(pl.program_id(0) == 0)
        def _():
            for cp in exchange.copies(ex_in, ex_out, send_sems, recv_sems):
                cp.start()

        body(*own_in, *own_out, *own_scr)

        @pl.when(pl.program_id(0) == last)
        def _():
            for cp in exchange.copies(ex_in, ex_out, send_sems, recv_sems):
                cp.wait()

    outs = pl.pallas_call(
        hosted, name=name, grid=grid, in_specs=list(in_specs) + [ANY] * k_in, out_specs=list(out_specs) + [ANY] * k_out,
        out_shape=list(out_shape) + exchange.out_shapes,
        scratch_shapes=list(scratch_shapes) + [pltpu.SemaphoreType.DMA((exchange.n,)), pltpu.SemaphoreType.DMA((exchange.n,))],
        input_output_aliases={n_in + i: n_out + o for i, o in exchange.aliases.items()},
        compiler_params=_params("arbitrary"),
    )(*args, *exchange.operands)
    return list(outs[:n_out]), list(outs[n_out:])


def _rope(t, cos2, sin2):
    return t * cos2 + pltpu.roll(t, HEAD_DIM // 2, 1) * sin2


def _rope_bwd(d, cos2, sin2):
    return d * cos2 + pltpu.roll(d * sin2, HEAD_DIM // 2, d.ndim - 1)


def _inproj_bwd(x, gain, w_in, dpieces, dres, adam=()):
    seq = x.shape[0]
    t = TOK_TILE
    steps = seq // t
    n_pieces = len(dpieces)
    n_adam = len(adam[0]) if adam else 0
    adam_step = _adamw_body(n_adam)

    def body(x_ref, g_ref, w_ref, *refs):
        piece_refs, (dres_ref, *adam_in), (dx_ref, dw_ref, dg_ref, *adam_out) = (
            refs[:n_pieces], refs[n_pieces:n_pieces + 1 + 4 * n_adam], refs[n_pieces + 1 + 4 * n_adam:])

        @pl.when(pl.program_id(0) == 0)
        def _():
            dw_ref[...] = jnp.zeros_like(dw_ref)
            dg_ref[...] = jnp.zeros_like(dg_ref)

        adam_step(*adam_in, *adam_out)

        n, r = _rms(x_ref[...])
        gain_v = g_ref[...]
        h = (n * gain_v).astype(BF16)
        dh = None
        for j, piece in enumerate(piece_refs):
            cols = slice(j * RET_WIDTH, (j + 1) * RET_WIDTH)
            dp = piece[...]
            dw_ref[:, cols] += _dot_tn(h, dp)
            term = _dot_nt(dp, w_ref[:, cols])
            dh = term if dh is None else dh + term
        dx, dgs = _rms_bwd(n, r, gain_v, dh)
        dg_ref[...] += dgs
        dx_ref[...] = dres_ref[...] + dx

    adam_arrays = [a for group in adam for a in group]
    adam_spec = lambda a: pl.BlockSpec((a.shape[0] // steps, a.shape[1]), lambda i: (i, 0))
    outs, _ = _call(
        body, "inproj_bwd", (steps,),
        [_rows(t, D_MODEL), _full((1, D_MODEL)), _full((D_MODEL, IN_COLS))] + [_rows(t, RET_WIDTH)] * n_pieces + [_rows(t, D_MODEL)]
        + [adam_spec(a) for a in adam_arrays],
        [_rows(t, D_MODEL), _full((D_MODEL, IN_COLS)), _full((1, D_MODEL))] + [adam_spec(a) for a in adam_arrays[:n_adam]] * 3,
        [jax.ShapeDtypeStruct((seq, D_MODEL), F32), jax.ShapeDtypeStruct((D_MODEL, IN_COLS), F32),
         jax.ShapeDtypeStruct((1, D_MODEL), F32)] + [jax.ShapeDtypeStruct(a.shape, F32) for a in adam_arrays[:n_adam]] * 3,
        [], (x, gain, w_in, *dpieces, dres, *adam_arrays))
    stepped = outs[3:]
    return outs[:3], (stepped[:n_adam], stepped[n_adam:2 * n_adam], stepped[2 * n_adam:])


def _ret_consts():
    c = RET_CHUNK
    f32 = np.float32
    log_gamma = np.log(f32(1.0) - np.exp(np.linspace(math.log(1.0 / 32), math.log(1.0 / 512), RET_HEADS, dtype=f32))).astype(f32)
    idx = np.arange(c, dtype=f32)
    diff = idx[:, None] - idx[None, :]
    decay = np.where(diff[None] >= 0, np.exp(np.maximum(diff, f32(0.0))[None] * log_gamma[:, None, None]), f32(0.0))
    zeta = np.exp((c - 1 - idx)[None, :] * log_gamma[:, None])
    xi = np.exp((idx + f32(1.0))[None, :] * log_gamma[:, None])
    g_chunk = np.exp(f32(c) * log_gamma)
    wide = lambda rowvals: jnp.asarray(np.broadcast_to(rowvals[:, :, None], (RET_HEADS, c, c)).astype(f32))
    return (jnp.asarray(decay.astype(f32)), wide(xi), wide(zeta),
            jnp.asarray(np.broadcast_to(g_chunk[:, None, None], (RET_HEADS, c, c)).astype(f32)))


def _rope_tables(seq):
    f32 = np.float32
    half = HEAD_DIM // 2
    inv_freq = np.power(f32(ROPE_BASE), -np.arange(half, dtype=f32) / f32(half)).astype(f32)
    ang = (np.arange(seq, dtype=f32)[:, None] * inv_freq[None, :]).astype(f32)
    cos, sin = np.cos(ang).astype(f32), np.sin(ang).astype(f32)
    return jnp.asarray(np.concatenate([cos, cos], axis=1)), jnp.asarray(np.concatenate([-sin, sin], axis=1))


def _bmm(a, b):
    return lax.dot_general(a, b, (((2,), (1,)), ((0,), (0,))), preferred_element_type=F32)


def _bmm_nt(a, b):
    return lax.dot_general(a, b, (((2,), (2,)), ((0,), (0,))), preferred_element_type=F32)


def _bmm_tn(a, b):
    return lax.dot_general(a, b, (((1,), (1,)), ((0,), (0,))), preferred_element_type=F32)


def _ret_blocks(ref):
    return jnp.stack([ref[cc * RET_CHUNK:(cc + 1) * RET_CHUNK, hd * HEAD_DIM:(hd + 1) * HEAD_DIM]
                      for cc in range(RET_CHUNKS_PER_STEP) for hd in range(RET_HEADS)])


def _ret_unblock(ref, blocks):
    for cc in range(RET_CHUNKS_PER_STEP):
        for hd in range(RET_HEADS):
            ref[cc * RET_CHUNK:(cc + 1) * RET_CHUNK, hd * HEAD_DIM:(hd + 1) * HEAD_DIM] = blocks[cc * RET_HEADS + hd].astype(ref.dtype)


def _rope_blocks(ref):
    return jnp.stack([ref[cc * RET_CHUNK:(cc + 1) * RET_CHUNK, :] for cc in range(RET_CHUNKS_PER_STEP) for _ in range(RET_HEADS)])


def _per_chunk(ref):
    return jnp.concatenate([ref[...]] * RET_CHUNKS_PER_STEP, axis=0)


def _ret_gain(gain_ref):
    return jnp.stack([gain_ref[:, hd * HEAD_DIM:(hd + 1) * HEAD_DIM]
                      for _ in range(RET_CHUNKS_PER_STEP) for hd in range(RET_HEADS)])


def _ret_fwd_step(q_ref, k_ref, v_ref, gate_ref, gain_ref, dec_ref, xi_ref, zeta_ref, gch_ref, y_ref, rprev_ref, state):
    cps = RET_CHUNKS_PER_STEP
    qb, kb, vb = _ret_blocks(q_ref), _ret_blocks(k_ref), _ret_blocks(v_ref)
    s = _bmm_nt(qb, kb) * _per_chunk(dec_ref)
    vz = (vb.astype(F32) * _per_chunk(zeta_ref)).astype(BF16)
    incr = _bmm_tn(kb, vz)
    gch_v = gch_ref[...]
    rp = state[...]
    before = []
    for cc in range(cps):
        before.append(rp.astype(BF16))
        rprev_ref[cc] = before[-1]
        rp = gch_v * rp + incr[cc * RET_HEADS:(cc + 1) * RET_HEADS]
    state[...] = rp
    qx = (qb.astype(F32) * _per_chunk(xi_ref)).astype(BF16)
    o = _bmm(s.astype(BF16), vb) + _bmm(qx, jnp.concatenate(before, axis=0))
    oc = o - jnp.mean(o, axis=-1, keepdims=True)
    on = oc * lax.rsqrt(jnp.mean(oc * oc, axis=-1, keepdims=True) + NORM_EPS)
    g = _ret_blocks(gate_ref)
    _ret_unblock(y_ref, g * _sigmoid(g) * (on * _ret_gain(gain_ref)))


def _ret_bwd_step(q_ref, k_ref, v_ref, gate_ref, gain_ref, dec_ref, xi_ref, zeta_ref, gch_ref, rprev_ref, dyr_ref, cos_ref, sin_ref,
                  dq_ref, dk_ref, dv_ref, dgate_ref, dgain_ref, dstate):
    cps = RET_CHUNKS_PER_STEP
    nb = cps * RET_HEADS
    qb, kb, vb = _ret_blocks(q_ref), _ret_blocks(k_ref), _ret_blocks(v_ref)
    dec_b, xi_b, zeta_b = _per_chunk(dec_ref), _per_chunk(xi_ref), _per_chunk(zeta_ref)
    rpb = rprev_ref[...].reshape(nb, HEAD_DIM, HEAD_DIM)
    sdb = (_bmm_nt(qb, kb) * dec_b).astype(BF16)
    qx = (qb.astype(F32) * xi_b).astype(BF16)
    o = _bmm(sdb, vb) + _bmm(qx, rpb)
    oc = o - jnp.mean(o, axis=-1, keepdims=True)
    rstd = lax.rsqrt(jnp.mean(oc * oc, axis=-1, keepdims=True) + NORM_EPS)
    on = oc * rstd
    g = _ret_blocks(gate_ref)
    sg = _sigmoid(g)
    gain_b = _ret_gain(gain_ref)
    dyr = _ret_blocks(dyr_ref)
    _ret_unblock(dgate_ref, dyr * (on * gain_b) * (sg * (1.0 + g * (1.0 - sg))))
    dy = dyr * (g * sg)
    dgain = jnp.sum(dy * on, axis=1, keepdims=True)
    for hd in range(RET_HEADS):
        part = dgain[hd]
        for cc in range(1, cps):
            part = part + dgain[cc * RET_HEADS + hd]
        dgain_ref[:, hd * HEAD_DIM:(hd + 1) * HEAD_DIM] += part
    don = dy * gain_b
    do = rstd * (don - jnp.mean(don, axis=-1, keepdims=True) - on * jnp.mean(don * on, axis=-1, keepdims=True))
    dob = do.astype(BF16)
    ds = (_bmm_nt(dob, vb) * dec_b).astype(BF16)
    dq = _bmm(ds, kb) + _bmm_nt(dob, rpb) * xi_b
    dk = _bmm_tn(ds, qb)
    dv = _bmm_tn(sdb, dob)
    dstate_local = _bmm_tn(qx, dob)
    vz = (vb.astype(F32) * zeta_b).astype(BF16)
    gch_v = gch_ref[...]
    zeta_v = zeta_ref[...]
    gh = dstate[...]
    dk_extra, dv_extra = [None] * cps, [None] * cps
    for cc in reversed(range(cps)):
        sl = slice(cc * RET_HEADS, (cc + 1) * RET_HEADS)
        gb = gh.astype(BF16)
        dk_extra[cc] = _bmm_nt(vz[sl], gb)
        dv_extra[cc] = _bmm(kb[sl], gb) * zeta_v
        gh = dstate_local[sl] + gch_v * gh
    dstate[...] = gh
    cos_b, sin_b = _rope_blocks(cos_ref), _rope_blocks(sin_ref)
    _ret_unblock(dq_ref, _rope_bwd(dq, cos_b, sin_b))
    _ret_unblock(dk_ref, _rope_bwd((dk + jnp.concatenate(dk_extra, axis=0)) * K_SCALE, cos_b, sin_b))
    _ret_unblock(dv_ref, dv + jnp.concatenate(dv_extra, axis=0))


def _ret_bwd(q, k, v, gate, gn_gain, consts, rprev, dy_ret, cos2, sin2, exchange=None):
    seq = q.shape[0]
    cps = RET_CHUNKS_PER_STEP
    t = cps * RET_CHUNK
    nt = seq // t
    dec, xi, zeta, gch = consts

    def body(*refs):
        dgain_ref, dstate = refs[-2:]

        @pl.when(pl.program_id(0) == 0)
        def _():
            dstate[...] = jnp.zeros_like(dstate)
            dgain_ref[...] = jnp.zeros_like(dgain_ref)

        _ret_bwd_step(*refs)

    cst = _full((RET_HEADS, RET_CHUNK, RET_CHUNK))
    rev = _rows_rev(t, RET_WIDTH, nt)
    act = jax.ShapeDtypeStruct((seq, RET_WIDTH), BF16)
    return _call(
        body, "ret_bwd", (nt,),
        [rev] * 4 + [_full((1, RET_WIDTH)), cst, cst, cst, cst,
                     pl.BlockSpec((cps, RET_HEADS, HEAD_DIM, HEAD_DIM), lambda i: (nt - 1 - i, 0, 0, 0)), rev,
                     _rows_rev(t, HEAD_DIM, nt), _rows_rev(t, HEAD_DIM, nt)],
        [rev] * 4 + [_full((1, RET_WIDTH))], [act, act, act, act, jax.ShapeDtypeStruct((1, RET_WIDTH), F32)],
        [pltpu.VMEM((RET_HEADS, HEAD_DIM, HEAD_DIM), F32)],
        (q, k, v, gate, gn_gain, dec, xi, zeta, gch, rprev, dy_ret, cos2, sin2), exchange)


MXU_TILE = 256
S5_STATE_TILES = 2 * N_STATES // MXU_TILE
S5_CHANNELS_PER_TILE = MXU_TILE // SSM_STATE * SSM_GROUP
S5_TILES_SHAPE = (S5_STATE_TILES, LANES, MXU_TILE)


def _chan_block(kk):
    return ((kk % (S5_STATE_TILES // 2)) * S5_CHANNELS_PER_TILE) // LANES * LANES


def _s5_tiles(blocks_re, blocks_im):
    per = MXU_TILE // SSM_STATE
    half = S5_STATE_TILES // 2
    eye = jnp.eye(per, dtype=F32)
    in_half = (jnp.arange(half) % (LANES // S5_CHANNELS_PER_TILE))[:, None] == jnp.arange(LANES // S5_CHANNELS_PER_TILE)[None, :]
    out = []
    for blk in (blocks_re, blocks_im):
        t = blk.reshape(half, per, SSM_GROUP, 1, SSM_STATE) * eye[None, :, None, :, None]
        t = t.reshape(half, 1, S5_CHANNELS_PER_TILE, MXU_TILE)
        out.append(jnp.where(in_half[:, :, None, None], t, 0.0).reshape(half, LANES, MXU_TILE))
    return jnp.concatenate(out, axis=0)


def _s5_discretise(lam_re, lam_im, log_dt, b_re, b_im, c_re, c_im):
    lam = lax.complex(jnp.minimum(lam_re, -1e-4), lam_im)
    dt = jnp.exp(log_dt)[:, None]
    lam_bar = jnp.exp(lam * dt)
    b_bar = ((lam_bar - 1.0) / lam)[:, :, None] * lax.complex(b_re, b_im)
    b_tiles = _s5_tiles(jnp.swapaxes(jnp.real(b_bar), 1, 2), jnp.swapaxes(jnp.imag(b_bar), 1, 2))
    return jnp.real(lam_bar), jnp.imag(lam_bar), b_tiles, _s5_tiles(c_re, -c_im)


def _s5_tables(lbr, lbi):
    lr = lbr.reshape(1, N_STATES)
    li = lbi.reshape(1, N_STATES)
    pr, pi = lr, li
    while pr.shape[0] < S5_STEPS:
        top_r, top_i = pr[-1:], pi[-1:]
        pr, pi = (jnp.concatenate([pr, pr * top_r - pi * top_i], axis=0), jnp.concatenate([pi, pr * top_i + pi * top_r], axis=0))
    bc = lambda a: jnp.broadcast_to(a, (SUBLANES, N_STATES))
    rep = lambda rows: jnp.broadcast_to(rows[:, None, :], (S5_STEPS, SUBLANES, N_STATES)).reshape(S5_STEPS * SUBLANES, N_STATES)
    return bc(lr), bc(li), rep(pr), rep(pi)


def _cmul(ar, ai, br, bi):
    return ar * br - ai * bi, ar * bi + ai * br


def _seg(i):
    return pl.ds(pl.multiple_of(i * SUBLANES, SUBLANES), SUBLANES)


def _scan_order():
    r = np.arange(S5_TILE)
    token = (r % SUBLANES) * S5_STEPS + r // SUBLANES
    p = (token[:, None] == np.arange(S5_TILE)[None, :]).astype(np.float32)
    return jnp.asarray(p, dtype=BF16), jnp.asarray(p.T, dtype=BF16)


def _to_scan_order(p_ref, rows_bf16):
    return _dot(p_ref[...], rows_bf16).astype(BF16)


def _from_scan_order(pt_ref, rows, pieces=3):
    pt = pt_ref[...]
    out = None
    for _ in range(pieces):
        piece = rows.astype(BF16)
        rows = rows - piece.astype(F32)
        term = _dot(pt, piece)
        out = term if out is None else out + term
    return out


def _s5_channels_to_states(chan_b, m_ref, xs):
    for kk in range(S5_STATE_TILES):
        lo = _chan_block(kk)
        res = _dot(chan_b[:, lo:lo + LANES], m_ref[kk])
        xs[2 * kk] = res[:, :LANES]
        xs[2 * kk + 1] = res[:, LANES:]


def _s5_states_to_channels(sb_ref, m_ref):
    out = []
    for lo in range(0, SSM_WIDTH, LANES):
        acc = None
        for kk in range(S5_STATE_TILES):
            if _chan_block(kk) == lo:
                term = _dot_nt(sb_ref[:, kk * MXU_TILE:(kk + 1) * MXU_TILE], m_ref[kk])
                acc = term if acc is None else acc + term
        out.append(acc)
    return jnp.concatenate(out, axis=1)


def _s5_outer_acc(acc_ref, chan_b, states_ref):
    for kk in range(S5_STATE_TILES):
        lo = _chan_block(kk)
        acc_ref[kk] += _dot_tn(chan_b[:, lo:lo + LANES], states_ref[:, kk * MXU_TILE:(kk + 1) * MXU_TILE])


def _s5_scan_tile(xs, lr_ref, li_ref, pr_ref, pi_ref, carry_re, carry_im, conj, reverse, prev_of=None):
    sgn = -1.0 if conj else 1.0
    seg_in_re, seg_in_im, sums_re, sums_im = [], [], [], []
    group = S5_GROUP
    for grp in range(S5_COLBLK // group):
        blks = [grp * group + j for j in range(group)]
        lrs = [lr_ref[:, b * LANES:(b + 1) * LANES] for b in blks]
        lis = [sgn * li_ref[:, b * LANES:(b + 1) * LANES] for b in blks]

        def step(it, carry, blks=blks, lrs=lrs, lis=lis):
            i = (S5_STEPS - 1 - it) if reverse else it
            out_r, out_i = [], []
            for j, b in enumerate(blks):
                xr, xi = _cmul(lrs[j], lis[j], carry[j], carry[group + j])
                xr = xr + xs[b, _seg(i), :]
                xi = xi + xs[S5_COLBLK + b, _seg(i), :]
                xs[b, _seg(i), :] = xr
                xs[S5_COLBLK + b, _seg(i), :] = xi
                out_r.append(xr)
                out_i.append(xi)
            return tuple(out_r + out_i)

        zeros = tuple(jnp.zeros((SUBLANES, LANES), F32) for _ in range(2 * group))
        ends = lax.fori_loop(0, S5_STEPS, step, zeros, unroll=True)

        ins_r, ins_i = [], []
        for j, b in enumerate(blks):
            cols = slice(b * LANES, (b + 1) * LANES)
            pr = pr_ref[SUBLANES * S5_STEPS - 1:SUBLANES * S5_STEPS, cols]
            pi = sgn * pi_ref[SUBLANES * S5_STEPS - 1:SUBLANES * S5_STEPS, cols]
            cur_r = carry_re[:, cols]
            cur_i = carry_im[:, cols]
            row_id = lax.broadcasted_iota(jnp.int32, (SUBLANES, LANES), 0)
            in_r = jnp.zeros((SUBLANES, LANES), F32)
            in_i = jnp.zeros((SUBLANES, LANES), F32)
            order = range(SUBLANES - 1, -1, -1) if reverse else range(SUBLANES)
            for sgm in order:
                in_r = jnp.where(row_id == sgm, cur_r, in_r)
                in_i = jnp.where(row_id == sgm, cur_i, in_i)
                mr, mi = _cmul(pr, pi, cur_r, cur_i)
                cur_r = mr + ends[j][sgm:sgm + 1, :]
                cur_i = mi + ends[group + j][sgm:sgm + 1, :]
            carry_re[:, cols] = cur_r
            carry_im[:, cols] = cur_i
            ins_r.append(in_r)
            ins_i.append(in_i)

        def fix(it, c, blks=blks, ins_r=ins_r, ins_i=ins_i):
            pw = (S5_STEPS - 1 - it) if reverse else it
            out = []
            for j, b in enumerate(blks):
                cols = slice(b * LANES, (b + 1) * LANES)
                prow = pl.ds(pl.multiple_of(pw * SUBLANES, SUBLANES), SUBLANES)
                fr, fi = _cmul(pr_ref[prow, cols], sgn * pi_ref[prow, cols], ins_r[j], ins_i[j])
                vr = xs[b, _seg(it), :] + fr
                vi = xs[S5_COLBLK + b, _seg(it), :] + fi
                xs[b, _seg(it), :] = vr
                xs[S5_COLBLK + b, _seg(it), :] = vi
                if prev_of is not None:
                    opr, opi, sr, si = c[4 * j:4 * j + 4]
                    out += [prev_of[0][b, _seg(it), :], prev_of[0][S5_COLBLK + b, _seg(it), :],
                            sr + vr * opr + vi * opi, si + vi * opr - vr * opi]
            return tuple(out)

        init = ()
        if prev_of is not None:
            zero = jnp.zeros((SUBLANES, LANES), F32)
            for b in blks:
                init += (prev_of[1][b], prev_of[2][b], zero, zero)
        done = lax.fori_loop(0, S5_STEPS, fix, init, unroll=True)
        for j in range(len(blks) if prev_of is not None else 0):
            sums_re.append(done[4 * j + 2])
            sums_im.append(done[4 * j + 3])
        seg_in_re += ins_r
        seg_in_im += ins_i
    return seg_in_re, seg_in_im, sums_re, sums_im


def _s5_pack(xs, dst):
    for b in range(2 * S5_COLBLK):
        dst[:, b * LANES:(b + 1) * LANES] = xs[b].astype(BF16)


def _inproj_s5_fwd(x, gain, w_in, cos2, sin2, b_blk, c_blk, tables, d_skip, exchange=None):
    seq = x.shape[0]
    t = TOK_TILE
    sub = S5_TILE
    lr8, li8, p_re, p_im = tables
    p, pt = _scan_order()

    def body(x_ref, g_ref, w_ref, c_ref, s_ref, p_ref, pt_ref, b_ref, cm_ref, lr_ref, li_ref, pr_ref, pi_ref, d_ref,
             q_ref, k_ref, v_ref, gate_ref, u_ref, y_ref, start_ref, xs, xb, carry_re, carry_im):
        @pl.when(pl.program_id(0) == 0)
        def _():
            carry_re[...] = jnp.zeros_like(carry_re)
            carry_im[...] = jnp.zeros_like(carry_im)

        n, _ = _rms(x_ref[...])
        h = (n * g_ref[...]).astype(BF16)
        proj = _dot(h, w_ref[...])
        c = c_ref[...]
        s = s_ref[...]
        for hd in range(RET_HEADS):
            lo = hd * HEAD_DIM
            q_ref[:, lo:lo + HEAD_DIM] = _rope(proj[:, lo:lo + HEAD_DIM], c, s).astype(BF16)
            kh = proj[:, RET_WIDTH + lo:RET_WIDTH + lo + HEAD_DIM]
            k_ref[:, lo:lo + HEAD_DIM] = (_rope(kh, c, s) * K_SCALE).astype(BF16)
        v_ref[...] = proj[:, 2 * RET_WIDTH:3 * RET_WIDTH].astype(BF16)
        gate_ref[...] = proj[:, 3 * RET_WIDTH:4 * RET_WIDTH]
        u = proj[:, 4 * RET_WIDTH:]
        u_ref[...] = u
        for part in range(t // sub):
            rows = slice(part * sub, (part + 1) * sub)
            start_ref[part, 0:1, :] = carry_re[...]
            start_ref[part, 1:2, :] = carry_im[...]
            uv = u[rows]
            _s5_channels_to_states(_to_scan_order(p_ref, uv.astype(BF16)), b_ref, xs)
            _s5_scan_tile(xs, lr_ref, li_ref, pr_ref, pi_ref, carry_re, carry_im, conj=False, reverse=False)
            _s5_pack(xs, xb)
            y_ref[rows, :] = _from_scan_order(pt_ref, _s5_states_to_channels(xb, cm_ref), pieces=2) + d_ref[...] * uv

    half = lambda dt: jax.ShapeDtypeStruct((seq, RET_WIDTH), dt)
    return _call(
        body, "inproj_s5_fwd", (seq // t,),
        [_rows(t, D_MODEL), _full((1, D_MODEL)), _full((D_MODEL, IN_COLS)), _rows(t, HEAD_DIM), _rows(t, HEAD_DIM),
         _full((sub, sub)), _full((sub, sub)), _full(S5_TILES_SHAPE), _full(S5_TILES_SHAPE),
         _full((SUBLANES, N_STATES)), _full((SUBLANES, N_STATES)), _full((SUBLANES * S5_STEPS, N_STATES)),
         _full((SUBLANES * S5_STEPS, N_STATES)), _full((1, SSM_WIDTH))],
        [_rows(t, RET_WIDTH)] * 6 + [pl.BlockSpec((t // sub, 2, N_STATES), lambda i: (i, 0, 0))],
        [half(BF16), half(BF16), half(BF16), half(F32), half(F32), half(F32),
         jax.ShapeDtypeStruct((seq // sub, 2, N_STATES), F32)],
        [pltpu.VMEM((2 * S5_COLBLK, sub, LANES), F32), pltpu.VMEM((sub, 2 * N_STATES), BF16),
         pltpu.VMEM((1, N_STATES), F32), pltpu.VMEM((1, N_STATES), F32)],
        (x, gain, w_in, cos2, sin2, p, pt, b_blk, c_blk, lr8, li8, p_re, p_im, d_skip), exchange)


def _s5_bwd(u, dy, b_blk, c_blk, tables, d_skip, starts, exchange=None):
    seq = u.shape[0]
    t = S5_TILE
    nt = seq // t
    lr8, li8, p_re, p_im = tables
    p, pt = _scan_order()

    def body(u_ref, dy_ref, p_ref, pt_ref, b_ref, c_ref, lr_ref, li_ref, pr_ref, pi_ref, d_ref, start_ref,
             du_ref, dd_ref, dlam_ref, db_ref, dc_ref,
             xs, as_, xb, ab, carry_re, carry_im, acar_re, acar_im):
        @pl.when(pl.program_id(0) == 0)
        def _():
            acar_re[...] = jnp.zeros_like(acar_re)
            acar_im[...] = jnp.zeros_like(acar_im)
            dd_ref[...] = jnp.zeros_like(dd_ref)
            dlam_ref[...] = jnp.zeros_like(dlam_ref)
            db_ref[...] = jnp.zeros_like(db_ref)
            dc_ref[...] = jnp.zeros_like(dc_ref)

        uv = u_ref[...]
        dyv = dy_ref[...]
        ub = _to_scan_order(p_ref, uv.astype(BF16))
        dyb = _to_scan_order(p_ref, dyv.astype(BF16))
        carry_re[...] = start_ref[0, 0:1, :]
        carry_im[...] = start_ref[0, 1:2, :]
        _s5_channels_to_states(ub, b_ref, xs)
        in_re, in_im, _, _ = _s5_scan_tile(xs, lr_ref, li_ref, pr_ref, pi_ref, carry_re, carry_im, conj=False, reverse=False)
        _s5_channels_to_states(dyb, c_ref, as_)
        _, _, dl_re, dl_im = _s5_scan_tile(as_, lr_ref, li_ref, pr_ref, pi_ref, acar_re, acar_im, conj=True, reverse=True,
                                           prev_of=(xs, in_re, in_im))
        for b in range(S5_COLBLK):
            cols = slice(b * LANES, (b + 1) * LANES)
            dlam_ref[0, :, cols] += dl_re[b]
            dlam_ref[1, :, cols] += dl_im[b]
        _s5_pack(xs, xb)
        _s5_pack(as_, ab)
        du_ref[...] = (_from_scan_order(pt_ref, _s5_states_to_channels(ab, b_ref), pieces=2) + d_ref[...] * dyv).astype(BF16)
        dd_ref[...] += jnp.sum(dyv * uv, axis=0, keepdims=True)
        _s5_outer_acc(db_ref, ub, ab)
        _s5_outer_acc(dc_ref, dyb, xb)

    rev = _rows_rev(t, SSM_WIDTH, nt)
    vec = lambda: pltpu.VMEM((1, N_STATES), F32)
    outer = S5_TILES_SHAPE
    return _call(
        body, "s5_bwd", (nt,),
        [rev, rev, _full((t, t)), _full((t, t)), _full(S5_TILES_SHAPE), _full(S5_TILES_SHAPE),
         _full((SUBLANES, N_STATES)), _full((SUBLANES, N_STATES)), _full((SUBLANES * S5_STEPS, N_STATES)),
         _full((SUBLANES * S5_STEPS, N_STATES)), _full((1, SSM_WIDTH)),
         pl.BlockSpec((1, 2, N_STATES), lambda i: (nt - 1 - i, 0, 0))],
        [rev, _full((1, SSM_WIDTH)), _full((2, SUBLANES, N_STATES)), _full(outer), _full(outer)],
        [jax.ShapeDtypeStruct((seq, SSM_WIDTH), BF16), jax.ShapeDtypeStruct((1, SSM_WIDTH), F32),
         jax.ShapeDtypeStruct((2, SUBLANES, N_STATES), F32), jax.ShapeDtypeStruct(outer, F32), jax.ShapeDtypeStruct(outer, F32)],
        [pltpu.VMEM((2 * S5_COLBLK, t, LANES), F32), pltpu.VMEM((2 * S5_COLBLK, t, LANES), F32),
         pltpu.VMEM((t, 2 * N_STATES), BF16), pltpu.VMEM((t, 2 * N_STATES), BF16), vec(), vec(), vec(), vec()],
        (u, dy, p, pt, b_blk, c_blk, lr8, li8, p_re, p_im, d_skip, starts), exchange)


def _mix_fwd_step(ys_ref, yr_ref, x_ref, g_ref, wg_ref, wo_ref, x1_ref, mix_ref):
    g0 = _gelu(ys_ref[...]).astype(BF16)
    z = _dot(g0, wg_ref[...])
    glu = (z[:, :SSM_WIDTH] * _sigmoid(z[:, SSM_WIDTH:])).astype(BF16)
    mix = _dot(yr_ref[...], wo_ref[:RET_WIDTH, :]) + _dot(glu, wo_ref[RET_WIDTH:, :])
    mix_ref[...] = mix.astype(BF16)
    n, _ = _rms(mix)
    x1_ref[...] = x_ref[...] + n * g_ref[...]


def _ret_mix_fwd(q, k, v, gate, gn_gain, consts, y_ssm, x, gain, w_glu, w_out, exchange=None):
    seq = x.shape[0]
    cps = RET_CHUNKS_PER_STEP
    t = cps * RET_CHUNK
    dec, xi, zeta, gch = consts

    def body(q_ref, k_ref, v_ref, gate_ref, gain_ref, dec_ref, xi_ref, zeta_ref, gch_ref, ys_ref, x_ref, g_ref, wg_ref, wo_ref,
             y_ref, rprev_ref, x1_ref, mix_ref, state):
        @pl.when(pl.program_id(0) == 0)
        def _():
            state[...] = jnp.zeros_like(state)

        _ret_fwd_step(q_ref, k_ref, v_ref, gate_ref, gain_ref, dec_ref, xi_ref, zeta_ref, gch_ref, y_ref, rprev_ref, state)
        _mix_fwd_step(ys_ref, y_ref, x_ref, g_ref, wg_ref, wo_ref, x1_ref, mix_ref)

    cst = _full((RET_HEADS, RET_CHUNK, RET_CHUNK))
    return _call(
        body, "ret_mix_fwd", (seq // t,),
        [_rows(t, RET_WIDTH)] * 4 + [_full((1, RET_WIDTH)), cst, cst, cst, cst, _rows(t, SSM_WIDTH), _rows(t, D_MODEL),
                                     _full((1, D_MODEL)), _full((SSM_WIDTH, 2 * SSM_WIDTH)), _full((D_MODEL, D_MODEL))],
        [_rows(t, RET_WIDTH), pl.BlockSpec((cps, RET_HEADS, HEAD_DIM, HEAD_DIM), lambda i: (i, 0, 0, 0)),
         _rows(t, D_MODEL), _rows(t, D_MODEL)],
        [jax.ShapeDtypeStruct((seq, RET_WIDTH), BF16),
         jax.ShapeDtypeStruct((seq // RET_CHUNK, RET_HEADS, HEAD_DIM, HEAD_DIM), BF16),
         jax.ShapeDtypeStruct((seq, D_MODEL), F32), jax.ShapeDtypeStruct((seq, D_MODEL), BF16)],
        [pltpu.VMEM((RET_HEADS, HEAD_DIM, HEAD_DIM), F32)],
        (q, k, v, gate, gn_gain, dec, xi, zeta, gch, y_ssm, x, gain, w_glu, w_out), exchange)


def _mix_bwd(dx1, mix, gain, y_ssm, y_ret, w_glu, w_out, exchange=None):
    seq = dx1.shape[0]
    t = TOK_TILE

    def body(dx1_ref, mix_ref, g_ref, ys_ref, yr_ref, wg_ref, wo_ref, dyr_ref, dys_ref, dwo_ref, dwg_ref, dg_ref):
        @pl.when(pl.program_id(0) == 0)
        def _():
            dwo_ref[...] = jnp.zeros_like(dwo_ref)
            dwg_ref[...] = jnp.zeros_like(dwg_ref)
            dg_ref[...] = jnp.zeros_like(dg_ref)

        n, r = _rms(mix_ref[...].astype(F32))
        dmix, dgs = _rms_bwd(n, r, g_ref[...], dx1_ref[...])
        dg_ref[...] += dgs
        dmb = dmix.astype(BF16)
        dcat = _dot_nt(dmb, wo_ref[...])
        dyr_ref[...] = dcat[:, :RET_WIDTH]
        dglu = dcat[:, RET_WIDTH:]
        ys = ys_ref[...]
        g0 = _gelu(ys).astype(BF16)
        z = _dot(g0, wg_ref[...])
        a = z[:, :SSM_WIDTH]
        sb = _sigmoid(z[:, SSM_WIDTH:])
        dwo_ref[:RET_WIDTH, :] += _dot_tn(yr_ref[...], dmb)
        dwo_ref[RET_WIDTH:, :] += _dot_tn((a * sb).astype(BF16), dmb)
        dz = jnp.concatenate([dglu * sb, dglu * a * sb * (1.0 - sb)], axis=1).astype(BF16)
        dwg_ref[...] += _dot_tn(g0, dz)
        dys_ref[...] = _dot_nt(dz, wg_ref[...]) * _gelu_grad(ys)

    half = jax.ShapeDtypeStruct((seq, RET_WIDTH), F32)
    return _call(
        body, "mix_bwd", (seq // t,),
        [_rows(t, D_MODEL), _rows(t, D_MODEL), _full((1, D_MODEL)), _rows(t, SSM_WIDTH), _rows(t, RET_WIDTH),
         _full((SSM_WIDTH, 2 * SSM_WIDTH)), _full((D_MODEL, D_MODEL))],
        [_rows(t, RET_WIDTH), _rows(t, SSM_WIDTH), _full((D_MODEL, D_MODEL)), _full((SSM_WIDTH, 2 * SSM_WIDTH)),
         _full((1, D_MODEL))],
        [half, half, jax.ShapeDtypeStruct((D_MODEL, D_MODEL), F32), jax.ShapeDtypeStruct((SSM_WIDTH, 2 * SSM_WIDTH), F32),
         jax.ShapeDtypeStruct((1, D_MODEL), F32)],
        [], (dx1, mix, gain, y_ssm, y_ret, w_glu, w_out), exchange)


def _mlp_a(x1, target, gain_pre, gain_post, w1, w2):
    seq = x1.shape[0]
    t = MLP_TILE

    def body(x1_ref, tg_ref, gp_ref, gq_ref, w1_ref, w2_ref, df_ref, dx2_ref, dw2_ref, dgq_ref, sq_ref, rl_s, act_s):
        @pl.when(pl.program_id(0) == 0)
        def _():
            dw2_ref[...] = jnp.zeros_like(dw2_ref)
            dgq_ref[...] = jnp.zeros_like(dgq_ref)
            sq_ref[...] = jnp.zeros_like(sq_ref)

        x1v = x1_ref[...]
        n3, _ = _rms(x1v)
        h = (n3 * gp_ref[...]).astype(BF16)
        m = None
        for lo in range(0, D_FF, MLP_CHUNK):
            cols = slice(lo, lo + MLP_CHUNK)
            rl = jnp.maximum(_dot(h, w1_ref[:, cols]), 0.0)
            act = (rl * rl).astype(BF16)
            rl_s[:, cols] = rl
            act_s[:, cols] = act
            term = _dot(act, w2_ref[cols, :])
            m = term if m is None else m + term
        n4, r4 = _rms(m)
        gq = gq_ref[...]
        err = x1v + n4 * gq - tg_ref[...]
        sq_ref[...] += jnp.sum(err * err, axis=0, keepdims=True)
        dx2 = err * (1.0 / D_MODEL)
        dx2_ref[...] = dx2
        dm, dgs = _rms_bwd(n4, r4, gq, dx2)
        dgq_ref[...] += dgs
        dmb = dm.astype(BF16)
        for lo in range(0, D_FF, MLP_CHUNK):
            cols = slice(lo, lo + MLP_CHUNK)
            dw2_ref[cols, :] += _dot_tn(act_s[:, cols], dmb)
            df_ref[:, cols] = (_dot_nt(dmb, w2_ref[cols, :]) * (2.0 * rl_s[:, cols])).astype(BF16)

    return pl.pallas_call(
        body, name="mlp_a", grid=(seq // t,),
        scratch_shapes=[pltpu.VMEM((t, D_FF), F32), pltpu.VMEM((t, D_FF), BF16)],
        in_specs=[_rows(t, D_MODEL), _rows(t, D_MODEL), _full((1, D_MODEL)), _full((1, D_MODEL)),
                  _full((D_MODEL, D_FF)), _full((D_FF, D_MODEL))],
        out_specs=[_rows(t, D_FF), _rows(t, D_MODEL), _full((D_FF, D_MODEL)), _full((1, D_MODEL)), _full((1, D_MODEL))],
        out_shape=[jax.ShapeDtypeStruct((seq, D_FF), BF16), jax.ShapeDtypeStruct((seq, D_MODEL), F32),
                   jax.ShapeDtypeStruct((D_FF, D_MODEL), F32), jax.ShapeDtypeStruct((1, D_MODEL), F32),
                   jax.ShapeDtypeStruct((1, D_MODEL), F32)],
        compiler_params=_params("arbitrary"),
    )(x1, target, gain_pre, gain_post, w1, w2)


def _mlp_b(df, dx2, x1, gain_pre, w1):
    seq = x1.shape[0]
    t = TOK_TILE

    def body(df_ref, dx2_ref, x1_ref, gp_ref, w1_ref, dx1_ref, dw1_ref, dgp_ref):
        @pl.when(pl.program_id(0) == 0)
        def _():
            dw1_ref[...] = jnp.zeros_like(dw1_ref)
            dgp_ref[...] = jnp.zeros_like(dgp_ref)

        n3, r3 = _rms(x1_ref[...])
        gp = gp_ref[...]
        h = (n3 * gp).astype(BF16)
        dfv = df_ref[...]
        dw1_ref[...] += _dot_tn(h, dfv)
        dx, dgs = _rms_bwd(n3, r3, gp, _dot_nt(dfv, w1_ref[...]))
        dgp_ref[...] += dgs
        dx1_ref[...] = dx2_ref[...] + dx

    return pl.pallas_call(
        body, name="mlp_b", grid=(seq // t,),
        in_specs=[_rows(t, D_FF), _rows(t, D_MODEL), _rows(t, D_MODEL), _full((1, D_MODEL)), _full((D_MODEL, D_FF))],
        out_specs=[_rows(t, D_MODEL), _full((D_MODEL, D_FF)), _full((1, D_MODEL))],
        out_shape=[jax.ShapeDtypeStruct((seq, D_MODEL), F32), jax.ShapeDtypeStruct((D_MODEL, D_FF), F32),
                   jax.ShapeDtypeStruct((1, D_MODEL), F32)],
        compiler_params=_params("arbitrary"),
    )(df, dx2, x1, gain_pre, w1)


def _local_step(x, target, small, weights, ids=None, late_state=None):
    cos2, sin2 = _rope_tables(x.shape[0])
    ret_consts = _ret_consts()

    s5_names = ("ssm_lambda_re", "ssm_lambda_im", "ssm_log_dt", "ssm_b_re", "ssm_b_im", "ssm_c_re", "ssm_c_im")
    (lbr, lbi, b_tiles, c_tiles), disc_vjp = jax.vjp(_s5_discretise, *[small[name] for name in s5_names])
    tables = _s5_tables(lbr, lbi)
    b_blk = b_tiles.astype(BF16)
    c_blk = c_tiles.astype(BF16)
    d_skip = small["ssm_d"].reshape(1, SSM_WIDTH)
    gn_gain = small["ret_gn_gain"].reshape(1, RET_WIDTH)
    g_mix_pre = small["norm_mix_pre"].reshape(1, D_MODEL)
    g_mix_post = small["norm_mix_post"].reshape(1, D_MODEL)
    g_mlp_pre = small["norm_mlp_pre"].reshape(1, D_MODEL)
    g_mlp_post = small["norm_mlp_post"].reshape(1, D_MODEL)

    dist = ids is not None
    w_in, w_glu, w_out, w_ff1, w_ff2 = weights
    (q, k, v, gate, u, y_ssm, starts), got = _inproj_s5_fwd(
        x, g_mix_pre, w_in, cos2, sin2, b_blk, c_blk, tables, d_skip,
        _x_gather_send(MID + LATE, [w_glu, w_out, w_ff1, w_ff2], to_both=MID) if dist else None)
    if dist:
        w_glu, w_out, w_ff1, w_ff2 = got
    (y_ret, rprev, x1, mix), got = _ret_mix_fwd(q, k, v, gate, gn_gain, ret_consts, y_ssm, x, g_mix_post, w_glu, w_out,
                                                _x_gather_pass(LATE, [w_ff1, w_ff2]) if dist else None)
    if dist:
        w_ff1, w_ff2 = got
    df, dx2, d_ff2, dg_mlp_post, sq = _mlp_a(x1, target, g_mlp_pre, g_mlp_post, w_ff1, w_ff2)
    dx1, d_ff1, dg_mlp_pre = _mlp_b(df, dx2, x1, g_mlp_pre, w_ff1)
    (dy_ret, dy_ssm, d_out, d_glu, dg_mix_post), got = _mix_bwd(dx1, mix, g_mix_post, y_ssm, y_ret, w_glu, w_out,
                                                               _x_pair(LATE, [d_ff1, d_ff2]) if dist else None)
    if dist:
        sums = _pair_sum(ids, LATE, [d_ff1, d_ff2], got)
    (du, dd, dlam8, db_tiles, dc_tiles), got = _s5_bwd(
        u, dy_ssm, b_blk, c_blk, tables, d_skip, starts,
        _x_join(_x_chip(LATE, sums), _x_pair(MID, [d_glu, d_out])) if dist else None)
    if dist:
        parts = _chip_sum(ids, LATE, sums, got[:len(LATE)])
        sums = _pair_sum(ids, MID, [d_glu, d_out], got[len(LATE):])
    (dq, dk, dv, dgate, dgn), got = _ret_bwd(q, k, v, gate, gn_gain, ret_consts, rprev, dy_ret, cos2, sin2,
                                             _x_join(_x_share(LATE, parts), _x_chip(MID, sums)) if dist else None)
    if dist:
        late_grads = got[:len(LATE)]
        mid_parts = _chip_sum(ids, MID, sums, got[len(LATE):])
    adam = (late_state[0], late_grads, late_state[1], late_state[2]) if dist else ()
    (dx, d_in, dg_mix_pre), late_stepped = _inproj_bwd(x, g_mix_pre, w_in, [dq, dk, dv, dgate, du], dx1, adam)

    dlam = jnp.sum(dlam8, axis=1)
    s5_grads = disc_vjp((dlam[0].reshape(SSM_GROUPS, SSM_STATE), dlam[1].reshape(SSM_GROUPS, SSM_STATE), db_tiles, dc_tiles))
    small_grads = {
        "norm_mix_pre": dg_mix_pre, "norm_mix_post": dg_mix_post, "ret_gn_gain": dgn, "ssm_d": dd,
        "norm_mlp_pre": dg_mlp_pre, "norm_mlp_post": dg_mlp_post, "loss_sum": 0.5 * jnp.sum(sq) / D_MODEL,
    }
    small_grads.update(dict(zip(s5_names, s5_grads)))
    if dist:
        return dx, d_in, mid_parts, late_grads, late_stepped, small_grads
    return dx, (d_in, d_glu, d_out, d_ff1, d_ff2), small_grads


BIG = (
    ("w_in", D_MODEL, IN_COLS, 1),
    ("w_glu", SSM_WIDTH, 2 * SSM_WIDTH, 1),
    ("w_out", D_MODEL, D_MODEL, 0),
    ("w_ff1", D_MODEL, D_FF, 1),
    ("w_ff2", D_FF, D_MODEL, 0),
)
SMALL = (
    ("norm_mix_pre", (D_MODEL,)), ("norm_mix_post", (D_MODEL,)), ("ret_gn_gain", (RET_WIDTH,)),
    ("ssm_lambda_re", (SSM_GROUPS, SSM_STATE)), ("ssm_lambda_im", (SSM_GROUPS, SSM_STATE)), ("ssm_log_dt", (SSM_GROUPS,)),
    ("ssm_b_re", (SSM_GROUPS, SSM_STATE, SSM_GROUP)), ("ssm_b_im", (SSM_GROUPS, SSM_STATE, SSM_GROUP)),
    ("ssm_c_re", (SSM_GROUPS, SSM_GROUP, SSM_STATE)), ("ssm_c_im", (SSM_GROUPS, SSM_GROUP, SSM_STATE)),
    ("ssm_d", (SSM_WIDTH,)), ("norm_mlp_pre", (D_MODEL,)), ("norm_mlp_post", (D_MODEL,)),
)
SMALL_ROWS = 1152
EXCH_TILES = 8


PACKED = SMALL + (("loss_sum", (1,)),)


def _packed_rows(shape):
    return -(-math.prod(shape) // (SUBLANES * LANES)) * SUBLANES


def _pack_small(tree):
    rows = []
    for name, shape in PACKED:
        size = math.prod(shape)
        flat = tree[name].reshape(-1).astype(F32) if name in tree else jnp.zeros((size,), F32)
        rows.append(jnp.pad(flat, (0, _packed_rows(shape) * LANES - size)).reshape(-1, LANES))
    used = sum(r.shape[0] for r in rows)
    rows.append(jnp.zeros((SMALL_ROWS - used, LANES), F32))
    return jnp.concatenate(rows, axis=0)


def _unpack_small(packed):
    out, row = {}, 0
    for name, shape in PACKED:
        size = math.prod(shape)
        out[name] = packed[row:row + _packed_rows(shape)].reshape(-1)[:size].reshape((1,) + shape)
        row += _packed_rows(shape)
    return out


def _half_shape(r, c, axis):
    return (r // 2, c) if axis == 1 else (r, c // 2)


def _region_shape(r, c, axis):
    return (r // 2, c // N_CHIPS) if axis == 1 else (r // N_CHIPS, c // 2)


def _shard_shape(r, c, axis):
    return (r, c // N_CHIPS) if axis == 1 else (r // N_CHIPS, c)


def _ds(start, size):
    return pl.ds(pl.multiple_of(start * size, size), size)


def _region_of_full(ref, r, c, axis, shard, half):
    if axis == 1:
        return ref.at[_ds(half, r // 2), _ds(shard, c // N_CHIPS)]
    return ref.at[_ds(shard, r // N_CHIPS), _ds(half, c // 2)]


def _half_of_full(ref, r, c, axis, half):
    if axis == 1:
        return ref.at[_ds(half, r // 2), :]
    return ref.at[:, _ds(half, c // 2)]


def _half_of_shard(ref, r, c, axis, half):
    if axis == 1:
        return ref.at[_ds(half, r // 2), :]
    return ref.at[:, _ds(half, c // 2)]


def _region_of_half(ref, r, c, axis, shard):
    if axis == 1:
        return ref.at[:, _ds(shard, c // N_CHIPS)]
    return ref.at[_ds(shard, r // N_CHIPS), :]


def _place():
    x, y, c = lax.axis_index("x"), lax.axis_index("y"), lax.axis_index("c")
    chips = [(1 - x, y), (x, 1 - y), (1 - x, 1 - y)]
    return x, y, c, chips


LATE = (3, 4)
MID = (1, 2)
FIRST = (0,)
SMALL_HALF = (SMALL_ROWS // 2, LANES)


def _remote(src, dst, send_sem, recv_sem, to):
    return pltpu.make_async_remote_copy(src_ref=src, dst_ref=dst, send_sem=send_sem, recv_sem=recv_sem,
                                        device_id=to, device_id_type=MESH)


def _same(arrays):
    return [jax.ShapeDtypeStruct(a.shape, a.dtype) for a in arrays]


def _x_gather_send(ws, fulls, to_both=()):
    def copies(ins, outs, send_sems, recv_sems):
        x, y, c, chips = _place()
        out, at = [], 0
        for j, w in enumerate(ws):
            _, r, cc, axis = BIG[w]
            mine = _region_of_full(outs[j], r, cc, axis, 2 * x + y, c)
            for other_core in range(2 if w in to_both else 1):
                to_core = (1 - c) if other_core else c
                out += [_remote(mine, mine, send_sems.at[at + k], recv_sems.at[at + k], (cx, cy, to_core))
                        for k, (cx, cy) in enumerate(chips)]
                at += 3
        return out

    n = sum(6 if w in to_both else 3 for w in ws)
    return _Exchange(fulls, _same(fulls), {j: j for j in range(len(ws))}, n, copies)


def _x_gather_pass(ws, fulls):
    def copies(ins, outs, send_sems, recv_sems):
        x, y, c, chips = _place()
        out = []
        for j, w in enumerate(ws):
            _, r, cc, axis = BIG[w]
            for k, (cx, cy) in enumerate(chips):
                landed = _region_of_full(outs[j], r, cc, axis, 2 * cx + cy, c)
                out.append(_remote(landed, landed, send_sems.at[3 * j + k], recv_sems.at[3 * j + k], (x, y, 1 - c)))
        return out

    return _Exchange(fulls, _same(fulls), {j: j for j in range(len(ws))}, 3 * len(ws), copies)


def _x_pair(ws, grads, small=None):
    def copies(ins, outs, send_sems, recv_sems):
        x, y, c, _ = _place()
        out = []
        for j, w in enumerate(ws):
            _, r, cc, axis = BIG[w]
            out.append(_remote(_half_of_full(ins[j], r, cc, axis, 1 - c), outs[j], send_sems.at[j], recv_sems.at[j], (x, y, 1 - c)))
        if small is not None:
            j = len(ws)
            out.append(_remote(ins[j].at[_ds(1 - c, SMALL_ROWS // 2), :], outs[j], send_sems.at[j], recv_sems.at[j], (x, y, 1 - c)))
        return out

    shapes = [jax.ShapeDtypeStruct(_half_shape(*BIG[w][1:]), F32) for w in ws]
    extra = [] if small is None else [small]
    return _Exchange(list(grads) + extra, shapes + [jax.ShapeDtypeStruct(SMALL_HALF, F32)] * len(extra), {},
                     len(ws) + len(extra), copies)


def _x_chip(ws, sums_bf16, small_sum=None):
    def copies(ins, outs, send_sems, recv_sems):
        x, y, c, chips = _place()
        out = []
        for j, w in enumerate(ws):
            _, r, cc, axis = BIG[w]
            out += [_remote(_region_of_half(ins[j], r, cc, axis, 2 * cx + cy), outs[j].at[k],
                            send_sems.at[3 * j + k], recv_sems.at[3 * j + k], (cx, cy, c)) for k, (cx, cy) in enumerate(chips)]
        if small_sum is not None:
            j = len(ws)
            out += [_remote(ins[j], outs[j].at[k], send_sems.at[3 * j + k], recv_sems.at[3 * j + k], (cx, cy, c))
                    for k, (cx, cy) in enumerate(chips)]
        return out

    shapes = [jax.ShapeDtypeStruct((3,) + _region_shape(*BIG[w][1:]), BF16) for w in ws]
    extra = [] if small_sum is None else [small_sum]
    return _Exchange(list(sums_bf16) + extra, shapes + [jax.ShapeDtypeStruct((3,) + SMALL_HALF, F32)] * len(extra), {},
                     3 * (len(ws) + len(extra)), copies)


def _x_share(ws, shards, small=None):
    def copies(ins, outs, send_sems, recv_sems):
        x, y, c, _ = _place()
        out = []
        for j, w in enumerate(ws):
            _, r, cc, axis = BIG[w]
            mine = _half_of_shard(outs[j], r, cc, axis, c)
            out.append(_remote(mine, mine, send_sems.at[j], recv_sems.at[j], (x, y, 1 - c)))
        if small is not None:
            j = len(ws)
            mine = outs[j].at[_ds(c, SMALL_ROWS // 2), :]
            out.append(_remote(mine, mine, send_sems.at[j], recv_sems.at[j], (x, y, 1 - c)))
        return out

    arrays = list(shards) + ([] if small is None else [small])
    return _Exchange(arrays, _same(arrays), {j: j for j in range(len(arrays))}, len(arrays), copies)


class _Offset:
    def __init__(self, sems, base):
        self._sems, self._base = sems, base

    @property
    def at(self):
        return self

    def __getitem__(self, i):
        return self._sems.at[self._base + i]


def _x_join(a, b):
    ka_in, ka_out = len(a.operands), len(a.out_shapes)

    def copies(ins, outs, send_sems, recv_sems):
        return (a.copies(ins[:ka_in], outs[:ka_out], send_sems, recv_sems)
                + b.copies(ins[ka_in:], outs[ka_out:], _Offset(send_sems, a.n), _Offset(recv_sems, a.n)))

    aliases = dict(a.aliases)
    aliases.update({ka_in + i: ka_out + o for i, o in b.aliases.items()})
    return _Exchange(a.operands + b.operands, a.out_shapes + b.out_shapes, aliases, a.n + b.n, copies)


def _run_exchange(name, exchange):
    k_in, k_out = len(exchange.operands), len(exchange.out_shapes)

    def body(*refs):
        copies = exchange.copies(refs[:k_in], refs[k_in:k_in + k_out], refs[-2], refs[-1])
        for cp in copies:
            cp.start()
        for cp in copies:
            cp.wait()

    return list(pl.pallas_call(
        body, name=name, in_specs=[ANY] * k_in, out_specs=[ANY] * k_out, out_shape=exchange.out_shapes,
        scratch_shapes=[pltpu.SemaphoreType.DMA((exchange.n,)), pltpu.SemaphoreType.DMA((exchange.n,))],
        input_output_aliases=dict(exchange.aliases),
    )(*exchange.operands))


def _gather_now(name, ws, fulls):
    send, onward = _x_gather_send(ws, fulls), _x_gather_pass(ws, fulls)
    k = len(fulls)

    def body(*refs):
        ins, outs, (send_a, recv_a, send_b, recv_b) = refs[:k], refs[k:2 * k], refs[2 * k:]
        first = send.copies(ins, outs, send_a, recv_a)
        second = onward.copies(ins, outs, send_b, recv_b)
        for cp in first:
            cp.start()
        for landed, cp in zip(first, second):
            landed.wait()
            cp.start()
        for cp in second:
            cp.wait()

    sems = lambda n: pltpu.SemaphoreType.DMA((n,))
    return list(pl.pallas_call(
        body, name=name, in_specs=[ANY] * k, out_specs=[ANY] * k, out_shape=send.out_shapes,
        scratch_shapes=[sems(send.n), sems(send.n), sems(onward.n), sems(onward.n)],
        input_output_aliases={j: j for j in range(k)},
    )(*fulls))


def _tile_rows(shape):
    return (shape[0] // EXCH_TILES, shape[1])


def _cast_place(ids, shards):
    def body(ids_ref, *refs):
        for j in range(len(BIG)):
            refs[len(BIG) + j][...] = refs[j][...].astype(BF16)

    def out_spec(w):
        _, r, cc, axis = BIG[w]
        tile = _tile_rows(_shard_shape(r, cc, axis))
        if axis == 1:
            return pl.BlockSpec(tile, lambda i, ids: (i, ids[1]))
        return pl.BlockSpec(tile, lambda i, ids: (ids[1] * EXCH_TILES + i, 0))

    return list(pl.pallas_call(
        body, name="cast_place",
        grid_spec=pltpu.PrefetchScalarGridSpec(
            num_scalar_prefetch=1, grid=(EXCH_TILES,),
            in_specs=[pl.BlockSpec(_tile_rows(_shard_shape(r, cc, axis)), lambda i, ids: (i, 0)) for _, r, cc, axis in BIG],
            out_specs=[out_spec(w) for w in range(len(BIG))]),
        out_shape=[jax.ShapeDtypeStruct((r, cc), BF16) for _, r, cc, _ in BIG],
        compiler_params=pltpu.CompilerParams(dimension_semantics=("parallel",)),
    )(ids, *shards))


def _pair_sum(ids, ws, grads, got, small=None):
    nb = len(ws)
    n = nb + (small is not None)
    halves = [_half_shape(*BIG[w][1:]) for w in ws] + [SMALL_HALF] * (n - nb)

    def body(ids_ref, *refs):
        for j in range(n):
            total = refs[j][...] + refs[n + j][...]
            refs[2 * n + j][...] = total.astype(refs[2 * n + j].dtype)

    def mine_spec(j):
        tile = _tile_rows(halves[j])
        if j < nb and BIG[ws[j]][3] == 0:
            return pl.BlockSpec(tile, lambda i, ids: (i, ids[0]))
        return pl.BlockSpec(tile, lambda i, ids: (ids[0] * EXCH_TILES + i, 0))

    plain = lambda j: pl.BlockSpec(_tile_rows(halves[j]), lambda i, ids: (i, 0))
    outs = pl.pallas_call(
        body, name="pair_sum",
        grid_spec=pltpu.PrefetchScalarGridSpec(
            num_scalar_prefetch=1, grid=(EXCH_TILES,),
            in_specs=[mine_spec(j) for j in range(n)] + [plain(j) for j in range(n)],
            out_specs=[plain(j) for j in range(n)]),
        out_shape=[jax.ShapeDtypeStruct(h, BF16 if j < nb else F32) for j, h in enumerate(halves)],
        compiler_params=pltpu.CompilerParams(dimension_semantics=("parallel",)),
    )(ids, *grads, *([] if small is None else [small]), *got)
    return list(outs)


def _chip_sum(ids, ws, sums, arrived):
    nb, n = len(ws), len(sums)
    regions = [_region_shape(*BIG[w][1:]) for w in ws] + [SMALL_HALF] * (n - nb)

    def body(ids_ref, *refs):
        for j in range(n):
            own, got, out = refs[j], refs[n + j], refs[2 * n + j]
            if j < nb:
                out[...] = ((own[...].astype(F32) + got[0].astype(F32)) + got[1].astype(F32)) + got[2].astype(F32)
            else:
                out[...] = (own[...] + got[1]) + (got[0] + got[2])

    def own_spec(j):
        tile = _tile_rows(regions[j])
        if j >= nb:
            return pl.BlockSpec(tile, lambda i, ids: (i, 0))
        if BIG[ws[j]][3] == 1:
            return pl.BlockSpec(tile, lambda i, ids: (i, ids[1]))
        return pl.BlockSpec(tile, lambda i, ids: (ids[1] * EXCH_TILES + i, 0))

    def out_spec(j):
        tile = _tile_rows(regions[j])
        if j < nb and BIG[ws[j]][3] == 0:
            return pl.BlockSpec(tile, lambda i, ids: (i, ids[0]))
        return pl.BlockSpec(tile, lambda i, ids: (ids[0] * EXCH_TILES + i, 0))

    got_spec = lambda j: pl.BlockSpec((3,) + _tile_rows(regions[j]), lambda i, ids: (0, i, 0))
    return list(pl.pallas_call(
        body, name="chip_sum",
        grid_spec=pltpu.PrefetchScalarGridSpec(
            num_scalar_prefetch=1, grid=(EXCH_TILES,),
            in_specs=[own_spec(j) for j in range(n)] + [got_spec(j) for j in range(n)],
            out_specs=[out_spec(j) for j in range(n)]),
        out_shape=[jax.ShapeDtypeStruct(_shard_shape(*BIG[w][1:]), F32) for w in ws]
        + [jax.ShapeDtypeStruct((SMALL_ROWS, LANES), F32)] * (n - nb),
        compiler_params=pltpu.CompilerParams(dimension_semantics=("parallel",)),
    )(ids, *sums, *arrived))


def _adamw_body(n):
    c1 = 1.0 - ADAM_B1 ** ADAM_STEP
    c2 = 1.0 - ADAM_B2 ** ADAM_STEP

    def body(*refs):
        for j in range(n):
            w, g, m, v = (refs[k * n + j][...] for k in range(4))
            m = ADAM_B1 * m + (1.0 - ADAM_B1) * g
            v = ADAM_B2 * v + (1.0 - ADAM_B2) * (g * g)
            refs[4 * n + j][...] = -ADAM_LR * ((m / c1) / (jnp.sqrt(v / c2) + ADAM_EPS) + ADAM_WD * w)
            refs[5 * n + j][...] = m
            refs[6 * n + j][...] = v

    return body


def _adamw(name, ws, gs, ms, vs):
    n = len(ws)
    specs = [pl.BlockSpec(_tile_rows(w.shape), lambda i: (i, 0)) for w in ws]
    shapes = [jax.ShapeDtypeStruct(w.shape, F32) for w in ws]
    out, _ = _call(_adamw_body(n), name, (EXCH_TILES,), specs * 4, specs * 3, shapes * 3, [], (*ws, *gs, *ms, *vs))
    return out[:n], out[n:2 * n], out[2 * n:]


def _adamw_whole(name, ws, gs, ms, vs):
    n = len(ws)
    out = pl.pallas_call(_adamw_body(n), name=name, out_shape=[jax.ShapeDtypeStruct(w.shape, F32) for w in ws] * 3)(*ws, *gs, *ms, *vs)
    return out[:n], out[n:2 * n], out[2 * n:]


def kernel(x, norm_mix_pre, norm_mix_post, w_in, ret_gn_gain, ssm_lambda_re, ssm_lambda_im, ssm_log_dt, ssm_b_re, ssm_b_im, ssm_c_re, ssm_c_im, ssm_d, w_glu, w_out, norm_mlp_pre, norm_mlp_post, w_ff1, w_ff2, loss_target, m_norm_mix_pre, m_norm_mix_post, m_w_in, m_ret_gn_gain, m_ssm_lambda_re, m_ssm_lambda_im, m_ssm_log_dt, m_ssm_b_re, m_ssm_b_im, m_ssm_c_re, m_ssm_c_im, m_ssm_d, m_w_glu, m_w_out, m_norm_mlp_pre, m_norm_mlp_post, m_w_ff1, m_w_ff2, v_norm_mix_pre, v_norm_mix_post, v_w_in, v_ret_gn_gain, v_ssm_lambda_re, v_ssm_lambda_im, v_ssm_log_dt, v_ssm_b_re, v_ssm_b_im, v_ssm_c_re, v_ssm_c_im, v_ssm_d, v_w_glu, v_w_out, v_norm_mlp_pre, v_norm_mlp_post, v_w_ff1, v_w_ff2):
    given = dict(locals())
    order = ["norm_mix_pre", "norm_mix_post", "w_in", "ret_gn_gain", "ssm_lambda_re", "ssm_lambda_im", "ssm_log_dt",
             "ssm_b_re", "ssm_b_im", "ssm_c_re", "ssm_c_im", "ssm_d", "w_glu", "w_out", "norm_mlp_pre", "norm_mlp_post",
             "w_ff1", "w_ff2"]
    big_names = [name for name, _, _, _ in BIG]
    ids = jnp.stack([lax.axis_index("c"), 2 * lax.axis_index("x") + lax.axis_index("y")]).astype(jnp.int32)

    weights = _cast_place(ids, [given[name][0] for name in big_names])
    w_in = _gather_now("gather_w_in", FIRST, weights[:1])
    small_w = {name: given[name][0] for name, _ in SMALL}
    late_state = [[given[prefix + big_names[w]][0] for w in LATE] for prefix in ("", "m_", "v_")]
    dx, d_in, mid_parts, late_grads, late_stepped, small_grads = _local_step(
        x[0], loss_target[0], small_w, w_in + weights[1:], ids, late_state)

    small_packed = _pack_small(small_grads)
    got = _run_exchange("pair_exchange", _x_join(_x_share(MID, mid_parts), _x_pair(FIRST, [d_in], small_packed)))
    mid_grads, got = got[:len(MID)], got[len(MID):]
    sums = _pair_sum(ids, FIRST, [d_in], got, small_packed)
    arrived = _run_exchange("chip_exchange", _x_chip(FIRST, sums[:-1], sums[-1]))
    parts = _chip_sum(ids, FIRST, sums, arrived)
    g_in, g_small = _run_exchange("pair_share", _x_share(FIRST, parts[:-1], parts[-1]))
    big_grads = [g_in] + mid_grads + late_grads
    rest = lambda prefix: [given[prefix + big_names[w]][0] for w in FIRST + MID]
    rest_out = _adamw("adamw_big", rest(""), big_grads[:len(FIRST + MID)], rest("m_"), rest("v_"))
    big_out = [list(rest_out[k]) + list(late_stepped[k]) for k in range(3)]
    small_names = [name for name, _ in SMALL]
    small_tree = _unpack_small(g_small)
    small_grads = [small_tree[name] for name in small_names]
    small = lambda prefix: [given[prefix + name] for name in small_names]
    small_out = _adamw_whole("adamw_small", small(""), small_grads, small("m_"), small("v_"))

    def in_order(big_arrays, small_arrays):
        tree = {name: a[None] for name, a in zip(big_names, big_arrays)}
        tree.update(zip(small_names, small_arrays))
        return [tree[name] for name in order]

    loss = small_tree["loss_sum"].reshape(())
    return (loss, dx[None], *in_order(big_grads, small_grads), *in_order(big_out[0], small_out[0]),
            *in_order(big_out[1], small_out[1]), *in_order(big_out[2], small_out[2]))
```

```python
import math

import jax
import jax.numpy as jnp
import numpy as np
from jax import lax
from jax.experimental import pallas as pl
from jax.experimental.pallas import tpu as pltpu

F32 = jnp.float32
BF16 = jnp.bfloat16

D_MODEL = 1024
RET_WIDTH = 512
RET_HEADS = 4
HEAD_DIM = 128
RET_CHUNK = 128
ROPE_BASE = 10000.0
SSM_WIDTH = 512
SSM_GROUP = 16
SSM_GROUPS = 32
SSM_STATE = 64
N_STATES = SSM_GROUPS * SSM_STATE
D_FF = 4096
IN_COLS = 4 * RET_WIDTH + SSM_WIDTH
NORM_EPS = 1e-6
K_SCALE = HEAD_DIM ** -0.5

ADAM_LR = 0.001
ADAM_B1 = 0.9
ADAM_B2 = 0.999
ADAM_EPS = 1e-08
ADAM_WD = 0.01
ADAM_STEP = 10

LANES = 128
SUBLANES = 8
VMEM_LIMIT = 56 * 2 ** 20

TOK_TILE = 512
MLP_TILE = 256
RET_CHUNKS_PER_STEP = 4
S5_TILE = 256
S5_STEPS = S5_TILE // SUBLANES
S5_COLBLK = N_STATES // LANES
S5_GROUP = 4

N_CHIPS = 4
MESH = pl.DeviceIdType.MESH


def _dot(a, b):
    return jnp.dot(a, b, preferred_element_type=F32)


def _dot_nt(a, b):
    return lax.dot_general(a, b, (((1,), (1,)), ((), ())), preferred_element_type=F32)


def _dot_tn(a, b):
    return lax.dot_general(a, b, (((0,), (0,)), ((), ())), preferred_element_type=F32)


def _sigmoid(x):
    return 1.0 / (1.0 + jnp.exp(-x))


_GELU_C = math.sqrt(2.0 / math.pi)


def _gelu(x):
    return 0.5 * x * (1.0 + jnp.tanh(_GELU_C * (x + 0.044715 * (x * x * x))))


def _gelu_grad(x):
    t = jnp.tanh(_GELU_C * (x + 0.044715 * (x * x * x)))
    return 0.5 * (1.0 + t) + 0.5 * x * (1.0 - t * t) * (_GELU_C * (1.0 + 3.0 * 0.044715 * (x * x)))


def _rms(x):
    r = lax.rsqrt(jnp.mean(x * x, axis=-1, keepdims=True) + NORM_EPS)
    return x * r, r


def _rms_bwd(n, r, gain, dy):
    dn = dy * gain
    dx = r * (dn - n * jnp.mean(dn * n, axis=-1, keepdims=True))
    return dx, jnp.sum(dy * n, axis=0, keepdims=True)


def _full(shape):
    nd = len(shape)
    return pl.BlockSpec(shape, lambda i, _n=nd: (0,) * _n, pipeline_mode=pl.Buffered(1))


def _rows(tile, width):
    return pl.BlockSpec((tile, width), lambda i: (i, 0))


def _rows_rev(tile, width, n):
    return pl.BlockSpec((tile, width), lambda i, _n=n: (_n - 1 - i, 0))


def _params(sem):
    return pltpu.CompilerParams(dimension_semantics=(sem,), vmem_limit_bytes=VMEM_LIMIT)


ANY = pl.BlockSpec(memory_space=pl.ANY)


class _Exchange:
    def __init__(self, operands, out_shapes, aliases, n, copies):
        self.operands, self.out_shapes, self.aliases, self.n, self.copies = list(operands), list(out_shapes), aliases, n, copies


def _call(body, name, grid, in_specs, out_specs, out_shape, scratch_shapes, args, exchange=None):
    if exchange is None:
        outs = pl.pallas_call(body, name=name, grid=grid, in_specs=in_specs, out_specs=out_specs, out_shape=out_shape,
                              scratch_shapes=scratch_shapes, compiler_params=_params("arbitrary"))(*args)
        return list(outs), []
    n_in, n_out, n_scr = len(in_specs), len(out_specs), len(scratch_shapes)
    k_in, k_out = len(exchange.operands), len(exchange.out_shapes)
    last = grid[0] - 1

    def hosted(*refs):
        own_in, rest = refs[:n_in], refs[n_in:]
        ex_in, rest = rest[:k_in], rest[k_in:]
        own_out, rest = rest[:n_out], rest[n_out:]
        ex_out, rest = rest[:k_out], rest[k_out:]
        own_scr, (send_sems, recv_sems) = rest[:n_scr], rest[n_scr:]

        @pl.when(pl.program_id(0) == 0)
        def _():
            for cp in exchange.copies(ex_in, ex_out, send_sems, recv_sems):
                cp.start()

        body(*own_in, *own_out, *own_scr)

        @pl.when(pl.program_id(0) == last)
        def _():
            for cp in exchange.copies(ex_in, ex_out, send_sems, recv_sems):
                cp.wait()

    outs = pl.pallas_call(
        hosted, name=name, grid=grid, in_specs=list(in_specs) + [ANY] * k_in, out_specs=list(out_specs) + [ANY] * k_out,
        out_shape=list(out_shape) + exchange.out_shapes,
        scratch_shapes=list(scratch_shapes) + [pltpu.SemaphoreType.DMA((exchange.n,)), pltpu.SemaphoreType.DMA((exchange.n,))],
        input_output_aliases={n_in + i: n_out + o for i, o in exchange.aliases.items()},
        compiler_params=_params("arbitrary"),
    )(*args, *exchange.operands)
    return list(outs[:n_out]), list(outs[n_out:])


def _rope(t, cos2, sin2):
    return t * cos2 + pltpu.roll(t, HEAD_DIM // 2, 1) * sin2


def _rope_bwd(d, cos2, sin2):
    return d * cos2 + pltpu.roll(d * sin2, HEAD_DIM // 2, d.ndim - 1)


def _inproj_bwd(x, gain, w_in, dpieces, dres, adam=()):
    seq = x.shape[0]
    t = TOK_TILE
    steps = seq // t
    n_pieces = len(dpieces)
    n_adam = len(adam[0]) if adam else 0
    adam_step = _adamw_body(n_adam)

    def body(x_ref, g_ref, w_ref, *refs):
        piece_refs, (dres_ref, *adam_in), (dx_ref, dw_ref, dg_ref, *adam_out) = (
            refs[:n_pieces], refs[n_pieces:n_pieces + 1 + 4 * n_adam], refs[n_pieces + 1 + 4 * n_adam:])

        @pl.when(pl.program_id(0) == 0)
        def _():
            dw_ref[...] = jnp.zeros_like(dw_ref)
            dg_ref[...] = jnp.zeros_like(dg_ref)

        adam_step(*adam_in, *adam_out)

        n, r = _rms(x_ref[...])
        gain_v = g_ref[...]
        h = (n * gain_v).astype(BF16)
        dh = None
        for j, piece in enumerate(piece_refs):
            cols = slice(j * RET_WIDTH, (j + 1) * RET_WIDTH)
            dp = piece[...]
            dw_ref[:, cols] += _dot_tn(h, dp)
            term = _dot_nt(dp, w_ref[:, cols])
            dh = term if dh is None else dh + term
        dx, dgs = _rms_bwd(n, r, gain_v, dh)
        dg_ref[...] += dgs
        dx_ref[...] = dres_ref[...] + dx

    adam_arrays = [a for group in adam for a in group]
    adam_spec = lambda a: pl.BlockSpec((a.shape[0] // steps, a.shape[1]), lambda i: (i, 0))
    outs, _ = _call(
        body, "inproj_bwd", (steps,),
        [_rows(t, D_MODEL), _full((1, D_MODEL)), _full((D_MODEL, IN_COLS))] + [_rows(t, RET_WIDTH)] * n_pieces + [_rows(t, D_MODEL)]
        + [adam_spec(a) for a in adam_arrays],
        [_rows(t, D_MODEL), _full((D_MODEL, IN_COLS)), _full((1, D_MODEL))] + [adam_spec(a) for a in adam_arrays[:n_adam]] * 3,
        [jax.ShapeDtypeStruct((seq, D_MODEL), F32), jax.ShapeDtypeStruct((D_MODEL, IN_COLS), F32),
         jax.ShapeDtypeStruct((1, D_MODEL), F32)] + [jax.ShapeDtypeStruct(a.shape, F32) for a in adam_arrays[:n_adam]] * 3,
        [], (x, gain, w_in, *dpieces, dres, *adam_arrays))
    stepped = outs[3:]
    return outs[:3], (stepped[:n_adam], stepped[n_adam:2 * n_adam], stepped[2 * n_adam:])


def _ret_consts():
    c = RET_CHUNK
    f32 = np.float32
    log_gamma = np.log(f32(1.0) - np.exp(np.linspace(math.log(1.0 / 32), math.log(1.0 / 512), RET_HEADS, dtype=f32))).astype(f32)
    idx = np.arange(c, dtype=f32)
    diff = idx[:, None] - idx[None, :]
    decay = np.where(diff[None] >= 0, np.exp(np.maximum(diff, f32(0.0))[None] * log_gamma[:, None, None]), f32(0.0))
    zeta = np.exp((c - 1 - idx)[None, :] * log_gamma[:, None])
    xi = np.exp((idx + f32(1.0))[None, :] * log_gamma[:, None])
    g_chunk = np.exp(f32(c) * log_gamma)
    wide = lambda rowvals: jnp.asarray(np.broadcast_to(rowvals[:, :, None], (RET_HEADS, c, c)).astype(f32))
    return (jnp.asarray(decay.astype(f32)), wide(xi), wide(zeta),
            jnp.asarray(np.broadcast_to(g_chunk[:, None, None], (RET_HEADS, c, c)).astype(f32)))


def _rope_tables(seq):
    f32 = np.float32
    half = HEAD_DIM // 2
    inv_freq = np.power(f32(ROPE_BASE), -np.arange(half, dtype=f32) / f32(half)).astype(f32)
    ang = (np.arange(seq, dtype=f32)[:, None] * inv_freq[None, :]).astype(f32)
    cos, sin = np.cos(ang).astype(f32), np.sin(ang).astype(f32)
    return jnp.asarray(np.concatenate([cos, cos], axis=1)), jnp.asarray(np.concatenate([-sin, sin], axis=1))


def _bmm(a, b):
    return lax.dot_general(a, b, (((2,), (1,)), ((0,), (0,))), preferred_element_type=F32)


def _bmm_nt(a, b):
    return lax.dot_general(a, b, (((2,), (2,)), ((0,), (0,))), preferred_element_type=F32)


def _bmm_tn(a, b):
    return lax.dot_general(a, b, (((1,), (1,)), ((0,), (0,))), preferred_element_type=F32)


def _ret_blocks(ref):
    return jnp.stack([ref[cc * RET_CHUNK:(cc + 1) * RET_CHUNK, hd * HEAD_DIM:(hd + 1) * HEAD_DIM]
                      for cc in range(RET_CHUNKS_PER_STEP) for hd in range(RET_HEADS)])


def _ret_unblock(ref, blocks):
    for cc in range(RET_CHUNKS_PER_STEP):
        for hd in range(RET_HEADS):
            ref[cc * RET_CHUNK:(cc + 1) * RET_CHUNK, hd * HEAD_DIM:(hd + 1) * HEAD_DIM] = blocks[cc * RET_HEADS + hd].astype(ref.dtype)


def _rope_blocks(ref):
    return jnp.stack([ref[cc * RET_CHUNK:(cc + 1) * RET_CHUNK, :] for cc in range(RET_CHUNKS_PER_STEP) for _ in range(RET_HEADS)])


def _per_chunk(ref):
    return jnp.concatenate([ref[...]] * RET_CHUNKS_PER_STEP, axis=0)


def _ret_gain(gain_ref):
    return jnp.stack([gain_ref[:, hd * HEAD_DIM:(hd + 1) * HEAD_DIM]
                      for _ in range(RET_CHUNKS_PER_STEP) for hd in range(RET_HEADS)])


def _ret_fwd_step(q_ref, k_ref, v_ref, gate_ref, gain_ref, dec_ref, xi_ref, zeta_ref, gch_ref, y_ref, rprev_ref, state):
    cps = RET_CHUNKS_PER_STEP
    qb, kb, vb = _ret_blocks(q_ref), _ret_blocks(k_ref), _ret_blocks(v_ref)
    s = _bmm_nt(qb, kb) * _per_chunk(dec_ref)
    vz = (vb.astype(F32) * _per_chunk(zeta_ref)).astype(BF16)
    incr = _bmm_tn(kb, vz)
    gch_v = gch_ref[...]
    rp = state[...]
    before = []
    for cc in range(cps):
        before.append(rp.astype(BF16))
        rprev_ref[cc] = before[-1]
        rp = gch_v * rp + incr[cc * RET_HEADS:(cc + 1) * RET_HEADS]
    state[...] = rp
    qx = (qb.astype(F32) * _per_chunk(xi_ref)).astype(BF16)
    o = _bmm(s.astype(BF16), vb) + _bmm(qx, jnp.concatenate(before, axis=0))
    oc = o - jnp.mean(o, axis=-1, keepdims=True)
    on = oc * lax.rsqrt(jnp.mean(oc * oc, axis=-1, keepdims=True) + NORM_EPS)
    g = _ret_blocks(gate_ref)
    _ret_unblock(y_ref, g * _sigmoid(g) * (on * _ret_gain(gain_ref)))


def _ret_bwd_step(q_ref, k_ref, v_ref, gate_ref, gain_ref, dec_ref, xi_ref, zeta_ref, gch_ref, rprev_ref, dyr_ref, cos_ref, sin_ref,
                  dq_ref, dk_ref, dv_ref, dgate_ref, dgain_ref, dstate):
    cps = RET_CHUNKS_PER_STEP
    nb = cps * RET_HEADS
    qb, kb, vb = _ret_blocks(q_ref), _ret_blocks(k_ref), _ret_blocks(v_ref)
    dec_b, xi_b, zeta_b = _per_chunk(dec_ref), _per_chunk(xi_ref), _per_chunk(zeta_ref)
    rpb = rprev_ref[...].reshape(nb, HEAD_DIM, HEAD_DIM)
    sdb = (_bmm_nt(qb, kb) * dec_b).astype(BF16)
    qx = (qb.astype(F32) * xi_b).astype(BF16)
    o = _bmm(sdb, vb) + _bmm(qx, rpb)
    oc = o - jnp.mean(o, axis=-1, keepdims=True)
    rstd = lax.rsqrt(jnp.mean(oc * oc, axis=-1, keepdims=True) + NORM_EPS)
    on = oc * rstd
    g = _ret_blocks(gate_ref)
    sg = _sigmoid(g)
    gain_b = _ret_gain(gain_ref)
    dyr = _ret_blocks(dyr_ref)
    _ret_unblock(dgate_ref, dyr * (on * gain_b) * (sg * (1.0 + g * (1.0 - sg))))
    dy = dyr * (g * sg)
    dgain = jnp.sum(dy * on, axis=1, keepdims=True)
    for hd in range(RET_HEADS):
        part = dgain[hd]
        for cc in range(1, cps):
            part = part + dgain[cc * RET_HEADS + hd]
        dgain_ref[:, hd * HEAD_DIM:(hd + 1) * HEAD_DIM] += part
    don = dy * gain_b
    do = rstd * (don - jnp.mean(don, axis=-1, keepdims=True) - on * jnp.mean(don * on, axis=-1, keepdims=True))
    dob = do.astype(BF16)
    ds = (_bmm_nt(dob, vb) * dec_b).astype(BF16)
    dq = _bmm(ds, kb) + _bmm_nt(dob, rpb) * xi_b
    dk = _bmm_tn(ds, qb)
    dv = _bmm_tn(sdb, dob)
    dstate_local = _bmm_tn(qx, dob)
    vz = (vb.astype(F32) * zeta_b).astype(BF16)
    gch_v = gch_ref[...]
    zeta_v = zeta_ref[...]
    gh = dstate[...]
    dk_extra, dv_extra = [None] * cps, [None] * cps
    for cc in reversed(range(cps)):
        sl = slice(cc * RET_HEADS, (cc + 1) * RET_HEADS)
        gb = gh.astype(BF16)
        dk_extra[cc] = _bmm_nt(vz[sl], gb)
        dv_extra[cc] = _bmm(kb[sl], gb) * zeta_v
        gh = dstate_local[sl] + gch_v * gh
    dstate[...] = gh
    cos_b, sin_b = _rope_blocks(cos_ref), _rope_blocks(sin_ref)
    _ret_unblock(dq_ref, _rope_bwd(dq, cos_b, sin_b))
    _ret_unblock(dk_ref, _rope_bwd((dk + jnp.concatenate(dk_extra, axis=0)) * K_SCALE, cos_b, sin_b))
    _ret_unblock(dv_ref, dv + jnp.concatenate(dv_extra, axis=0))


def _ret_bwd(q, k, v, gate, gn_gain, consts, rprev, dy_ret, cos2, sin2, exchange=None):
    seq = q.shape[0]
    cps = RET_CHUNKS_PER_STEP
    t = cps * RET_CHUNK
    nt = seq // t
    dec, xi, zeta, gch = consts

    def body(*refs):
        dgain_ref, dstate = refs[-2:]

        @pl.when(pl.program_id(0) == 0)
        def _():
            dstate[...] = jnp.zeros_like(dstate)
            dgain_ref[...] = jnp.zeros_like(dgain_ref)

        _ret_bwd_step(*refs)

    cst = _full((RET_HEADS, RET_CHUNK, RET_CHUNK))
    rev = _rows_rev(t, RET_WIDTH, nt)
    act = jax.ShapeDtypeStruct((seq, RET_WIDTH), BF16)
    return _call(
        body, "ret_bwd", (nt,),
        [rev] * 4 + [_full((1, RET_WIDTH)), cst, cst, cst, cst,
                     pl.BlockSpec((cps, RET_HEADS, HEAD_DIM, HEAD_DIM), lambda i: (nt - 1 - i, 0, 0, 0)), rev,
                     _rows_rev(t, HEAD_DIM, nt), _rows_rev(t, HEAD_DIM, nt)],
        [rev] * 4 + [_full((1, RET_WIDTH))], [act, act, act, act, jax.ShapeDtypeStruct((1, RET_WIDTH), F32)],
        [pltpu.VMEM((RET_HEADS, HEAD_DIM, HEAD_DIM), F32)],
        (q, k, v, gate, gn_gain, dec, xi, zeta, gch, rprev, dy_ret, cos2, sin2), exchange)


MXU_TILE = 256
S5_STATE_TILES = 2 * N_STATES // MXU_TILE
S5_CHANNELS_PER_TILE = MXU_TILE // SSM_STATE * SSM_GROUP
S5_TILES_SHAPE = (S5_STATE_TILES, LANES, MXU_TILE)


def _chan_block(kk):
    return ((kk % (S5_STATE_TILES // 2)) * S5_CHANNELS_PER_TILE) // LANES * LANES


def _s5_tiles(blocks_re, blocks_im):
    per = MXU_TILE // SSM_STATE
    half = S5_STATE_TILES // 2
    eye = jnp.eye(per, dtype=F32)
    in_half = (jnp.arange(half) % (LANES // S5_CHANNELS_PER_TILE))[:, None] == jnp.arange(LANES // S5_CHANNELS_PER_TILE)[None, :]
    out = []
    for blk in (blocks_re, blocks_im):
        t = blk.reshape(half, per, SSM_GROUP, 1, SSM_STATE) * eye[None, :, None, :, None]
        t = t.reshape(half, 1, S5_CHANNELS_PER_TILE, MXU_TILE)
        out.append(jnp.where(in_half[:, :, None, None], t, 0.0).reshape(half, LANES, MXU_TILE))
    return jnp.concatenate(out, axis=0)


def _s5_discretise(lam_re, lam_im, log_dt, b_re, b_im, c_re, c_im):
    lam = lax.complex(jnp.minimum(lam_re, -1e-4), lam_im)
    dt = jnp.exp(log_dt)[:, None]
    lam_bar = jnp.exp(lam * dt)
    b_bar = ((lam_bar - 1.0) / lam)[:, :, None] * lax.complex(b_re, b_im)
    b_tiles = _s5_tiles(jnp.swapaxes(jnp.real(b_bar), 1, 2), jnp.swapaxes(jnp.imag(b_bar), 1, 2))
    return jnp.real(lam_bar), jnp.imag(lam_bar), b_tiles, _s5_tiles(c_re, -c_im)


def _s5_tables(lbr, lbi):
    lr = lbr.reshape(1, N_STATES)
    li = lbi.reshape(1, N_STATES)
    pr, pi = lr, li
    while pr.shape[0] < S5_STEPS:
        top_r, top_i = pr[-1:], pi[-1:]
        pr, pi = (jnp.concatenate([pr, pr * top_r - pi * top_i], axis=0), jnp.concatenate([pi, pr * top_i + pi * top_r], axis=0))
    bc = lambda a: jnp.broadcast_to(a, (SUBLANES, N_STATES))
    rep = lambda rows: jnp.broadcast_to(rows[:, None, :], (S5_STEPS, SUBLANES, N_STATES)).reshape(S5_STEPS * SUBLANES, N_STATES)
    return bc(lr), bc(li), rep(pr), rep(pi)


def _cmul(ar, ai, br, bi):
    return ar * br - ai * bi, ar * bi + ai * br


def _seg(i):
    return pl.ds(pl.multiple_of(i * SUBLANES, SUBLANES), SUBLANES)


def _scan_order():
    r = np.arange(S5_TILE)
    token = (r % SUBLANES) * S5_STEPS + r // SUBLANES
    p = (token[:, None] == np.arange(S5_TILE)[None, :]).astype(np.float32)
    return jnp.asarray(p, dtype=BF16), jnp.asarray(p.T, dtype=BF16)


def _to_scan_order(p_ref, rows_bf16):
    return _dot(p_ref[...], rows_bf16).astype(BF16)


def _from_scan_order(pt_ref, rows, pieces=3):
    pt = pt_ref[...]
    out = None
    for _ in range(pieces):
        piece = rows.astype(BF16)
        rows = rows - piece.astype(F32)
        term = _dot(pt, piece)
        out = term if out is None else out + term
    return out


def _s5_channels_to_states(chan_b, m_ref, xs):
    for kk in range(S5_STATE_TILES):
        lo = _chan_block(kk)
        res = _dot(chan_b[:, lo:lo + LANES], m_ref[kk])
        xs[2 * kk] = res[:, :LANES]
        xs[2 * kk + 1] = res[:, LANES:]


def _s5_states_to_channels(sb_ref, m_ref):
    out = []
    for lo in range(0, SSM_WIDTH, LANES):
        acc = None
        for kk in range(S5_STATE_TILES):
            if _chan_block(kk) == lo:
                term = _dot_nt(sb_ref[:, kk * MXU_TILE:(kk + 1) * MXU_TILE], m_ref[kk])
                acc = term if acc is None else acc + term
        out.append(acc)
    return jnp.concatenate(out, axis=1)


def _s5_outer_acc(acc_ref, chan_b, states_ref):
    for kk in range(S5_STATE_TILES):
        lo = _chan_block(kk)
        acc_ref[kk] += _dot_tn(chan_b[:, lo:lo + LANES], states_ref[:, kk * MXU_TILE:(kk + 1) * MXU_TILE])


def _s5_scan_tile(xs, lr_ref, li_ref, pr_ref, pi_ref, carry_re, carry_im, conj, reverse, prev_of=None):
    sgn = -1.0 if conj else 1.0
    seg_in_re, seg_in_im, sums_re, sums_im = [], [], [], []
    group = S5_GROUP
    for grp in range(S5_COLBLK // group):
        blks = [grp * group + j for j in range(group)]
        lrs = [lr_ref[:, b * LANES:(b + 1) * LANES] for b in blks]
        lis = [sgn * li_ref[:, b * LANES:(b + 1) * LANES] for b in blks]

        def step(it, carry, blks=blks, lrs=lrs, lis=lis):
            i = (S5_STEPS - 1 - it) if reverse else it
            out_r, out_i = [], []
            for j, b in enumerate(blks):
                xr, xi = _cmul(lrs[j], lis[j], carry[j], carry[group + j])
                xr = xr + xs[b, _seg(i), :]
                xi = xi + xs[S5_COLBLK + b, _seg(i), :]
                xs[b, _seg(i), :] = xr
                xs[S5_COLBLK + b, _seg(i), :] = xi
                out_r.append(xr)
                out_i.append(xi)
            return tuple(out_r + out_i)

        zeros = tuple(jnp.zeros((SUBLANES, LANES), F32) for _ in range(2 * group))
        ends = lax.fori_loop(0, S5_STEPS, step, zeros, unroll=True)

        ins_r, ins_i = [], []
        for j, b in enumerate(blks):
            cols = slice(b * LANES, (b + 1) * LANES)
            pr = pr_ref[SUBLANES * S5_STEPS - 1:SUBLANES * S5_STEPS, cols]
            pi = sgn * pi_ref[SUBLANES * S5_STEPS - 1:SUBLANES * S5_STEPS, cols]
            cur_r = carry_re[:, cols]
            cur_i = carry_im[:, cols]
            row_id = lax.broadcasted_iota(jnp.int32, (SUBLANES, LANES), 0)
            in_r = jnp.zeros((SUBLANES, LANES), F32)
            in_i = jnp.zeros((SUBLANES, LANES), F32)
            order = range(SUBLANES - 1, -1, -1) if reverse else range(SUBLANES)
            for sgm in order:
                in_r = jnp.where(row_id == sgm, cur_r, in_r)
                in_i = jnp.where(row_id == sgm, cur_i, in_i)
                mr, mi = _cmul(pr, pi, cur_r, cur_i)
                cur_r = mr + ends[j][sgm:sgm + 1, :]
                cur_i = mi + ends[group + j][sgm:sgm + 1, :]
            carry_re[:, cols] = cur_r
            carry_im[:, cols] = cur_i
            ins_r.append(in_r)
            ins_i.append(in_i)

        def fix(it, c, blks=blks, ins_r=ins_r, ins_i=ins_i):
            pw = (S5_STEPS - 1 - it) if reverse else it
            out = []
            for j, b in enumerate(blks):
                cols = slice(b * LANES, (b + 1) * LANES)
                prow = pl.ds(pl.multiple_of(pw * SUBLANES, SUBLANES), SUBLANES)
                fr, fi = _cmul(pr_ref[prow, cols], sgn * pi_ref[prow, cols], ins_r[j], ins_i[j])
                vr = xs[b, _seg(it), :] + fr
                vi = xs[S5_COLBLK + b, _seg(it), :] + fi
                xs[b, _seg(it), :] = vr
                xs[S5_COLBLK + b, _seg(it), :] = vi
                if prev_of is not None:
                    opr, opi, sr, si = c[4 * j:4 * j + 4]
                    out += [prev_of[0][b, _seg(it), :], prev_of[0][S5_COLBLK + b, _seg(it), :],
                            sr + vr * opr + vi * opi, si + vi * opr - vr * opi]
            return tuple(out)

        init = ()
        if prev_of is not None:
            zero = jnp.zeros((SUBLANES, LANES), F32)
            for b in blks:
                init += (prev_of[1][b], prev_of[2][b], zero, zero)
        done = lax.fori_loop(0, S5_STEPS, fix, init, unroll=True)
        for j in range(len(blks) if prev_of is not None else 0):
            sums_re.append(done[4 * j + 2])
            sums_im.append(done[4 * j + 3])
        seg_in_re += ins_r
        seg_in_im += ins_i
    return seg_in_re, seg_in_im, sums_re, sums_im


def _s5_pack(xs, dst):
    for b in range(2 * S5_COLBLK):
        dst[:, b * LANES:(b + 1) * LANES] = xs[b].astype(BF16)


def _inproj_s5_fwd(x, gain, w_in, cos2, sin2, b_blk, c_blk, tables, d_skip, exchange=None):
    seq = x.shape[0]
    t = TOK_TILE
    sub = S5_TILE
    lr8, li8, p_re, p_im = tables
    p, pt = _scan_order()

    def body(x_ref, g_ref, w_ref, c_ref, s_ref, p_ref, pt_ref, b_ref, cm_ref, lr_ref, li_ref, pr_ref, pi_ref, d_ref,
             q_ref, k_ref, v_ref, gate_ref, u_ref, y_ref, start_ref, xs, xb, carry_re, carry_im):
        @pl.when(pl.program_id(0) == 0)
        def _():
            carry_re[...] = jnp.zeros_like(carry_re)
            carry_im[...] = jnp.zeros_like(carry_im)

        n, _ = _rms(x_ref[...])
        h = (n * g_ref[...]).astype(BF16)
        proj = _dot(h, w_ref[...])
        c = c_ref[...]
        s = s_ref[...]
        for hd in range(RET_HEADS):
            lo = hd * HEAD_DIM
            q_ref[:, lo:lo + HEAD_DIM] = _rope(proj[:, lo:lo + HEAD_DIM], c, s).astype(BF16)
            kh = proj[:, RET_WIDTH + lo:RET_WIDTH + lo + HEAD_DIM]
            k_ref[:, lo:lo + HEAD_DIM] = (_rope(kh, c, s) * K_SCALE).astype(BF16)
        v_ref[...] = proj[:, 2 * RET_WIDTH:3 * RET_WIDTH].astype(BF16)
        gate_ref[...] = proj[:, 3 * RET_WIDTH:4 * RET_WIDTH]
        u = proj[:, 4 * RET_WIDTH:]
        u_ref[...] = u
        for part in range(t // sub):
            rows = slice(part * sub, (part + 1) * sub)
            start_ref[part, 0:1, :] = carry_re[...]
            start_ref[part, 1:2, :] = carry_im[...]
            uv = u[rows]
            _s5_channels_to_states(_to_scan_order(p_ref, uv.astype(BF16)), b_ref, xs)
            _s5_scan_tile(xs, lr_ref, li_ref, pr_ref, pi_ref, carry_re, carry_im, conj=False, reverse=False)
            _s5_pack(xs, xb)
            y_ref[rows, :] = _from_scan_order(pt_ref, _s5_states_to_channels(xb, cm_ref), pieces=2) + d_ref[...] * uv

    half = lambda dt: jax.ShapeDtypeStruct((seq, RET_WIDTH), dt)
    return _call(
        body, "inproj_s5_fwd", (seq // t,),
        [_rows(t, D_MODEL), _full((1, D_MODEL)), _full((D_MODEL, IN_COLS)), _rows(t, HEAD_DIM), _rows(t, HEAD_DIM),
         _full((sub, sub)), _full((sub, sub)), _full(S5_TILES_SHAPE), _full(S5_TILES_SHAPE),
         _full((SUBLANES, N_STATES)), _full((SUBLANES, N_STATES)), _full((SUBLANES * S5_STEPS, N_STATES)),
         _full((SUBLANES * S5_STEPS, N_STATES)), _full((1, SSM_WIDTH))],
        [_rows(t, RET_WIDTH)] * 6 + [pl.BlockSpec((t // sub, 2, N_STATES), lambda i: (i, 0, 0))],
        [half(BF16), half(BF16), half(BF16), half(F32), half(F32), half(F32),
         jax.ShapeDtypeStruct((seq // sub, 2, N_STATES), F32)],
        [pltpu.VMEM((2 * S5_COLBLK, sub, LANES), F32), pltpu.VMEM((sub, 2 * N_STATES), BF16),
         pltpu.VMEM((1, N_STATES), F32), pltpu.VMEM((1, N_STATES), F32)],
        (x, gain, w_in, cos2, sin2, p, pt, b_blk, c_blk, lr8, li8, p_re, p_im, d_skip), exchange)


def _s5_bwd(u, dy, b_blk, c_blk, tables, d_skip, starts, exchange=None):
    seq = u.shape[0]
    t = S5_TILE
    nt = seq // t
    lr8, li8, p_re, p_im = tables
    p, pt = _scan_order()

    def body(u_ref, dy_ref, p_ref, pt_ref, b_ref, c_ref, lr_ref, li_ref, pr_ref, pi_ref, d_ref, start_ref,
             du_ref, dd_ref, dlam_ref, db_ref, dc_ref,
             xs, as_, xb, ab, carry_re, carry_im, acar_re, acar_im):
        @pl.when(pl.program_id(0) == 0)
        def _():
            acar_re[...] = jnp.zeros_like(acar_re)
            acar_im[...] = jnp.zeros_like(acar_im)
            dd_ref[...] = jnp.zeros_like(dd_ref)
            dlam_ref[...] = jnp.zeros_like(dlam_ref)
            db_ref[...] = jnp.zeros_like(db_ref)
            dc_ref[...] = jnp.zeros_like(dc_ref)

        uv = u_ref[...]
        dyv = dy_ref[...]
        ub = _to_scan_order(p_ref, uv.astype(BF16))
        dyb = _to_scan_order(p_ref, dyv.astype(BF16))
        carry_re[...] = start_ref[0, 0:1, :]
        carry_im[...] = start_ref[0, 1:2, :]
        _s5_channels_to_states(ub, b_ref, xs)
        in_re, in_im, _, _ = _s5_scan_tile(xs, lr_ref, li_ref, pr_ref, pi_ref, carry_re, carry_im, conj=False, reverse=False)
        _s5_channels_to_states(dyb, c_ref, as_)
        _, _, dl_re, dl_im = _s5_scan_tile(as_, lr_ref, li_ref, pr_ref, pi_ref, acar_re, acar_im, conj=True, reverse=True,
                                           prev_of=(xs, in_re, in_im))
        for b in range(S5_COLBLK):
            cols = slice(b * LANES, (b + 1) * LANES)
            dlam_ref[0, :, cols] += dl_re[b]
            dlam_ref[1, :, cols] += dl_im[b]
        _s5_pack(xs, xb)
        _s5_pack(as_, ab)
        du_ref[...] = (_from_scan_order(pt_ref, _s5_states_to_channels(ab, b_ref), pieces=2) + d_ref[...] * dyv).astype(BF16)
        dd_ref[...] += jnp.sum(dyv * uv, axis=0, keepdims=True)
        _s5_outer_acc(db_ref, ub, ab)
        _s5_outer_acc(dc_ref, dyb, xb)

    rev = _rows_rev(t, SSM_WIDTH, nt)
    vec = lambda: pltpu.VMEM((1, N_STATES), F32)
    outer = S5_TILES_SHAPE
    return _call(
        body, "s5_bwd", (nt,),
        [rev, rev, _full((t, t)), _full((t, t)), _full(S5_TILES_SHAPE), _full(S5_TILES_SHAPE),
         _full((SUBLANES, N_STATES)), _full((SUBLANES, N_STATES)), _full((SUBLANES * S5_STEPS, N_STATES)),
         _full((SUBLANES * S5_STEPS, N_STATES)), _full((1, SSM_WIDTH)),
         pl.BlockSpec((1, 2, N_STATES), lambda i: (nt - 1 - i, 0, 0))],
        [rev, _full((1, SSM_WIDTH)), _full((2, SUBLANES, N_STATES)), _full(outer), _full(outer)],
        [jax.ShapeDtypeStruct((seq, SSM_WIDTH), BF16), jax.ShapeDtypeStruct((1, SSM_WIDTH), F32),
         jax.ShapeDtypeStruct((2, SUBLANES, N_STATES), F32), jax.ShapeDtypeStruct(outer, F32), jax.ShapeDtypeStruct(outer, F32)],
        [pltpu.VMEM((2 * S5_COLBLK, t, LANES), F32), pltpu.VMEM((2 * S5_COLBLK, t, LANES), F32),
         pltpu.VMEM((t, 2 * N_STATES), BF16), pltpu.VMEM((t, 2 * N_STATES), BF16), vec(), vec(), vec(), vec()],
        (u, dy, p, pt, b_blk, c_blk, lr8, li8, p_re, p_im, d_skip, starts), exchange)


def _mix_fwd_step(ys_ref, yr_ref, x_ref, g_ref, wg_ref, wo_ref, x1_ref, mix_ref):
    g0 = _gelu(ys_ref[...]).astype(BF16)
    z = _dot(g0, wg_ref[...])
    glu = (z[:, :SSM_WIDTH] * _sigmoid(z[:, SSM_WIDTH:])).astype(BF16)
    mix = _dot(yr_ref[...], wo_ref[:RET_WIDTH, :]) + _dot(glu, wo_ref[RET_WIDTH:, :])
    mix_ref[...] = mix.astype(BF16)
    n, _ = _rms(mix)
    x1_ref[...] = x_ref[...] + n * g_ref[...]


def _ret_mix_fwd(q, k, v, gate, gn_gain, consts, y_ssm, x, gain, w_glu, w_out, exchange=None):
    seq = x.shape[0]
    cps = RET_CHUNKS_PER_STEP
    t = cps * RET_CHUNK
    dec, xi, zeta, gch = consts

    def body(q_ref, k_ref, v_ref, gate_ref, gain_ref, dec_ref, xi_ref, zeta_ref, gch_ref, ys_ref, x_ref, g_ref, wg_ref, wo_ref,
             y_ref, rprev_ref, x1_ref, mix_ref, state):
        @pl.when(pl.program_id(0) == 0)
        def _():
            state[...] = jnp.zeros_like(state)

        _ret_fwd_step(q_ref, k_ref, v_ref, gate_ref, gain_ref, dec_ref, xi_ref, zeta_ref, gch_ref, y_ref, rprev_ref, state)
        _mix_fwd_step(ys_ref, y_ref, x_ref, g_ref, wg_ref, wo_ref, x1_ref, mix_ref)

    cst = _full((RET_HEADS, RET_CHUNK, RET_CHUNK))
    return _call(
        body, "ret_mix_fwd", (seq // t,),
        [_rows(t, RET_WIDTH)] * 4 + [_full((1, RET_WIDTH)), cst, cst, cst, cst, _rows(t, SSM_WIDTH), _rows(t, D_MODEL),
                                     _full((1, D_MODEL)), _full((SSM_WIDTH, 2 * SSM_WIDTH)), _full((D_MODEL, D_MODEL))],
        [_rows(t, RET_WIDTH), pl.BlockSpec((cps, RET_HEADS, HEAD_DIM, HEAD_DIM), lambda i: (i, 0, 0, 0)),
         _rows(t, D_MODEL), _rows(t, D_MODEL)],
        [jax.ShapeDtypeStruct((seq, RET_WIDTH), BF16),
         jax.ShapeDtypeStruct((seq // RET_CHUNK, RET_HEADS, HEAD_DIM, HEAD_DIM), BF16),
         jax.ShapeDtypeStruct((seq, D_MODEL), F32), jax.ShapeDtypeStruct((seq, D_MODEL), BF16)],
        [pltpu.VMEM((RET_HEADS, HEAD_DIM, HEAD_DIM), F32)],
        (q, k, v, gate, gn_gain, dec, xi, zeta, gch, y_ssm, x, gain, w_glu, w_out), exchange)


def _mix_bwd(dx1, mix, gain, y_ssm, y_ret, w_glu, w_out, exchange=None):
    seq = dx1.shape[0]
    t = TOK_TILE

    def body(dx1_ref, mix_ref, g_ref, ys_ref, yr_ref, wg_ref, wo_ref, dyr_ref, dys_ref, dwo_ref, dwg_ref, dg_ref):
        @pl.when(pl.program_id(0) == 0)
        def _():
            dwo_ref[...] = jnp.zeros_like(dwo_ref)
            dwg_ref[...] = jnp.zeros_like(dwg_ref)
            dg_ref[...] = jnp.zeros_like(dg_ref)

        n, r = _rms(mix_ref[...].astype(F32))
        dmix, dgs = _rms_bwd(n, r, g_ref[...], dx1_ref[...])
        dg_ref[...] += dgs
        dmb = dmix.astype(BF16)
        dcat = _dot_nt(dmb, wo_ref[...])
        dyr_ref[...] = dcat[:, :RET_WIDTH]
        dglu = dcat[:, RET_WIDTH:]
        ys = ys_ref[...]
        g0 = _gelu(ys).astype(BF16)
        z = _dot(g0, wg_ref[...])
        a = z[:, :SSM_WIDTH]
        sb = _sigmoid(z[:, SSM_WIDTH:])
        dwo_ref[:RET_WIDTH, :] += _dot_tn(yr_ref[...], dmb)
        dwo_ref[RET_WIDTH:, :] += _dot_tn((a * sb).astype(BF16), dmb)
        dz = jnp.concatenate([dglu * sb, dglu * a * sb * (1.0 - sb)], axis=1).astype(BF16)
        dwg_ref[...] += _dot_tn(g0, dz)
        dys_ref[...] = _dot_nt(dz, wg_ref[...]) * _gelu_grad(ys)

    half = jax.ShapeDtypeStruct((seq, RET_WIDTH), F32)
    return _call(
        body, "mix_bwd", (seq // t,),
        [_rows(t, D_MODEL), _rows(t, D_MODEL), _full((1, D_MODEL)), _rows(t, SSM_WIDTH), _rows(t, RET_WIDTH),
         _full((SSM_WIDTH, 2 * SSM_WIDTH)), _full((D_MODEL, D_MODEL))],
        [_rows(t, RET_WIDTH), _rows(t, SSM_WIDTH), _full((D_MODEL, D_MODEL)), _full((SSM_WIDTH, 2 * SSM_WIDTH)),
         _full((1, D_MODEL))],
        [half, half, jax.ShapeDtypeStruct((D_MODEL, D_MODEL), F32), jax.ShapeDtypeStruct((SSM_WIDTH, 2 * SSM_WIDTH), F32),
         jax.ShapeDtypeStruct((1, D_MODEL), F32)],
        [], (dx1, mix, gain, y_ssm, y_ret, w_glu, w_out), exchange)


def _mlp_a(x1, target, gain_pre, gain_post, w1, w2):
    seq = x1.shape[0]
    t = MLP_TILE

    def body(x1_ref, tg_ref, gp_ref, gq_ref, w1_ref, w2_ref, df_ref, dx2_ref, dw2_ref, dgq_ref, sq_ref):
        @pl.when(pl.program_id(0) == 0)
        def _():
            dw2_ref[...] = jnp.zeros_like(dw2_ref)
            dgq_ref[...] = jnp.zeros_like(dgq_ref)
            sq_ref[...] = jnp.zeros_like(sq_ref)

        x1v = x1_ref[...]
        n3, _ = _rms(x1v)
        h = (n3 * gp_ref[...]).astype(BF16)
        rl = jnp.maximum(_dot(h, w1_ref[...]), 0.0)
        act = (rl * rl).astype(BF16)
        n4, r4 = _rms(_dot(act, w2_ref[...]))
        gq = gq_ref[...]
        err = x1v + n4 * gq - tg_ref[...]
        sq_ref[...] += jnp.sum(err * err, axis=0, keepdims=True)
        dx2 = err * (1.0 / D_MODEL)
        dx2_ref[...] = dx2
        dm, dgs = _rms_bwd(n4, r4, gq, dx2)
        dgq_ref[...] += dgs
        dmb = dm.astype(BF16)
        dw2_ref[...] += _dot_tn(act, dmb)
        df_ref[...] = (_dot_nt(dmb, w2_ref[...]) * (2.0 * rl)).astype(BF16)

    return pl.pallas_call(
        body, name="mlp_a", grid=(seq // t,),
        in_specs=[_rows(t, D_MODEL), _rows(t, D_MODEL), _full((1, D_MODEL)), _full((1, D_MODEL)),
                  _full((D_MODEL, D_FF)), _full((D_FF, D_MODEL))],
        out_specs=[_rows(t, D_FF), _rows(t, D_MODEL), _full((D_FF, D_MODEL)), _full((1, D_MODEL)), _full((1, D_MODEL))],
        out_shape=[jax.ShapeDtypeStruct((seq, D_FF), BF16), jax.ShapeDtypeStruct((seq, D_MODEL), F32),
                   jax.ShapeDtypeStruct((D_FF, D_MODEL), F32), jax.ShapeDtypeStruct((1, D_MODEL), F32),
                   jax.ShapeDtypeStruct((1, D_MODEL), F32)],
        compiler_params=_params("arbitrary"),
    )(x1, target, gain_pre, gain_post, w1, w2)


def _mlp_b(df, dx2, x1, gain_pre, w1):
    seq = x1.shape[0]
    t = TOK_TILE

    def body(df_ref, dx2_ref, x1_ref, gp_ref, w1_ref, dx1_ref, dw1_ref, dgp_ref):
        @pl.when(pl.program_id(0) == 0)
        def _():
            dw1_ref[...] = jnp.zeros_like(dw1_ref)
            dgp_ref[...] = jnp.zeros_like(dgp_ref)

        n3, r3 = _rms(x1_ref[...])
        gp = gp_ref[...]
        h = (n3 * gp).astype(BF16)
        dfv = df_ref[...]
        dw1_ref[...] += _dot_tn(h, dfv)
        dx, dgs = _rms_bwd(n3, r3, gp, _dot_nt(dfv, w1_ref[...]))
        dgp_ref[...] += dgs
        dx1_ref[...] = dx2_ref[...] + dx

    return pl.pallas_call(
        body, name="mlp_b", grid=(seq // t,),
        in_specs=[_rows(t, D_FF), _rows(t, D_MODEL), _rows(t, D_MODEL), _full((1, D_MODEL)), _full((D_MODEL, D_FF))],
        out_specs=[_rows(t, D_MODEL), _full((D_MODEL, D_FF)), _full((1, D_MODEL))],
        out_shape=[jax.ShapeDtypeStruct((seq, D_MODEL), F32), jax.ShapeDtypeStruct((D_MODEL, D_FF), F32),
                   jax.ShapeDtypeStruct((1, D_MODEL), F32)],
        compiler_params=_params("arbitrary"),
    )(df, dx2, x1, gain_pre, w1)


def _local_step(x, target, small, weights, ids=None, late_state=None):
    cos2, sin2 = _rope_tables(x.shape[0])
    ret_consts = _ret_consts()

    s5_names = ("ssm_lambda_re", "ssm_lambda_im", "ssm_log_dt", "ssm_b_re", "ssm_b_im", "ssm_c_re", "ssm_c_im")
    (lbr, lbi, b_tiles, c_tiles), disc_vjp = jax.vjp(_s5_discretise, *[small[name] for name in s5_names])
    tables = _s5_tables(lbr, lbi)
    b_blk = b_tiles.astype(BF16)
    c_blk = c_tiles.astype(BF16)
    d_skip = small["ssm_d"].reshape(1, SSM_WIDTH)
    gn_gain = small["ret_gn_gain"].reshape(1, RET_WIDTH)
    g_mix_pre = small["norm_mix_pre"].reshape(1, D_MODEL)
    g_mix_post = small["norm_mix_post"].reshape(1, D_MODEL)
    g_mlp_pre = small["norm_mlp_pre"].reshape(1, D_MODEL)
    g_mlp_post = small["norm_mlp_post"].reshape(1, D_MODEL)

    dist = ids is not None
    w_in, w_glu, w_out, w_ff1, w_ff2 = weights
    (q, k, v, gate, u, y_ssm, starts), got = _inproj_s5_fwd(
        x, g_mix_pre, w_in, cos2, sin2, b_blk, c_blk, tables, d_skip,
        _x_gather_send(MID + LATE, [w_glu, w_out, w_ff1, w_ff2], to_both=MID) if dist else None)
    if dist:
        w_glu, w_out, w_ff1, w_ff2 = got
    (y_ret, rprev, x1, mix), got = _ret_mix_fwd(q, k, v, gate, gn_gain, ret_consts, y_ssm, x, g_mix_post, w_glu, w_out,
                                                _x_gather_pass(LATE, [w_ff1, w_ff2]) if dist else None)
    if dist:
        w_ff1, w_ff2 = got
    df, dx2, d_ff2, dg_mlp_post, sq = _mlp_a(x1, target, g_mlp_pre, g_mlp_post, w_ff1, w_ff2)
    dx1, d_ff1, dg_mlp_pre = _mlp_b(df, dx2, x1, g_mlp_pre, w_ff1)
    (dy_ret, dy_ssm, d_out, d_glu, dg_mix_post), got = _mix_bwd(dx1, mix, g_mix_post, y_ssm, y_ret, w_glu, w_out,
                                                               _x_pair(LATE, [d_ff1, d_ff2]) if dist else None)
    if dist:
        sums = _pair_sum(ids, LATE, [d_ff1, d_ff2], got)
    (du, dd, dlam8, db_tiles, dc_tiles), got = _s5_bwd(
        u, dy_ssm, b_blk, c_blk, tables, d_skip, starts,
        _x_join(_x_chip(LATE, sums), _x_pair(MID, [d_glu, d_out])) if dist else None)
    if dist:
        parts = _chip_sum(ids, LATE, sums, got[:len(LATE)])
        sums = _pair_sum(ids, MID, [d_glu, d_out], got[len(LATE):])
    (dq, dk, dv, dgate, dgn), got = _ret_bwd(q, k, v, gate, gn_gain, ret_consts, rprev, dy_ret, cos2, sin2,
                                             _x_join(_x_share(LATE, parts), _x_chip(MID, sums)) if dist else None)
    if dist:
        late_grads = got[:len(LATE)]
        mid_parts = _chip_sum(ids, MID, sums, got[len(LATE):])
    adam = (late_state[0], late_grads, late_state[1], late_state[2]) if dist else ()
    (dx, d_in, dg_mix_pre), late_stepped = _inproj_bwd(x, g_mix_pre, w_in, [dq, dk, dv, dgate, du], dx1, adam)

    dlam = jnp.sum(dlam8, axis=1)
    s5_grads = disc_vjp((dlam[0].reshape(SSM_GROUPS, SSM_STATE), dlam[1].reshape(SSM_GROUPS, SSM_STATE), db_tiles, dc_tiles))
    small_grads = {
        "norm_mix_pre": dg_mix_pre, "norm_mix_post": dg_mix_post, "ret_gn_gain": dgn, "ssm_d": dd,
        "norm_mlp_pre": dg_mlp_pre, "norm_mlp_post": dg_mlp_post, "loss_sum": 0.5 * jnp.sum(sq) / D_MODEL,
    }
    small_grads.update(dict(zip(s5_names, s5_grads)))
    if dist:
        return dx, d_in, mid_parts, late_grads, late_stepped, small_grads
    return dx, (d_in, d_glu, d_out, d_ff1, d_ff2), small_grads


BIG = (
    ("w_in", D_MODEL, IN_COLS, 1),
    ("w_glu", SSM_WIDTH, 2 * SSM_WIDTH, 1),
    ("w_out", D_MODEL, D_MODEL, 0),
    ("w_ff1", D_MODEL, D_FF, 1),
    ("w_ff2", D_FF, D_MODEL, 0),
)
SMALL = (
    ("norm_mix_pre", (D_MODEL,)), ("norm_mix_post", (D_MODEL,)), ("ret_gn_gain", (RET_WIDTH,)),
    ("ssm_lambda_re", (SSM_GROUPS, SSM_STATE)), ("ssm_lambda_im", (SSM_GROUPS, SSM_STATE)), ("ssm_log_dt", (SSM_GROUPS,)),
    ("ssm_b_re", (SSM_GROUPS, SSM_STATE, SSM_GROUP)), ("ssm_b_im", (SSM_GROUPS, SSM_STATE, SSM_GROUP)),
    ("ssm_c_re", (SSM_GROUPS, SSM_GROUP, SSM_STATE)), ("ssm_c_im", (SSM_GROUPS, SSM_GROUP, SSM_STATE)),
    ("ssm_d", (SSM_WIDTH,)), ("norm_mlp_pre", (D_MODEL,)), ("norm_mlp_post", (D_MODEL,)),
)
SMALL_ROWS = 1152
EXCH_TILES = 4


PACKED = SMALL + (("loss_sum", (1,)),)


def _packed_rows(shape):
    return -(-math.prod(shape) // (SUBLANES * LANES)) * SUBLANES


def _pack_small(tree):
    rows = []
    for name, shape in PACKED:
        size = math.prod(shape)
        flat = tree[name].reshape(-1).astype(F32) if name in tree else jnp.zeros((size,), F32)
        rows.append(jnp.pad(flat, (0, _packed_rows(shape) * LANES - size)).reshape(-1, LANES))
    used = sum(r.shape[0] for r in rows)
    rows.append(jnp.zeros((SMALL_ROWS - used, LANES), F32))
    return jnp.concatenate(rows, axis=0)


def _unpack_small(packed):
    out, row = {}, 0
    for name, shape in PACKED:
        size = math.prod(shape)
        out[name] = packed[row:row + _packed_rows(shape)].reshape(-1)[:size].reshape((1,) + shape)
        row += _packed_rows(shape)
    return out


def _half_shape(r, c, axis):
    return (r // 2, c) if axis == 1 else (r, c // 2)


def _region_shape(r, c, axis):
    return (r // 2, c // N_CHIPS) if axis == 1 else (r // N_CHIPS, c // 2)


def _shard_shape(r, c, axis):
    return (r, c // N_CHIPS) if axis == 1 else (r // N_CHIPS, c)


def _ds(start, size):
    return pl.ds(pl.multiple_of(start * size, size), size)


def _region_of_full(ref, r, c, axis, shard, half):
    if axis == 1:
        return ref.at[_ds(half, r // 2), _ds(shard, c // N_CHIPS)]
    return ref.at[_ds(shard, r // N_CHIPS), _ds(half, c // 2)]


def _half_of_full(ref, r, c, axis, half):
    if axis == 1:
        return ref.at[_ds(half, r // 2), :]
    return ref.at[:, _ds(half, c // 2)]


def _half_of_shard(ref, r, c, axis, half):
    if axis == 1:
        return ref.at[_ds(half, r // 2), :]
    return ref.at[:, _ds(half, c // 2)]


def _region_of_half(ref, r, c, axis, shard):
    if axis == 1:
        return ref.at[:, _ds(shard, c // N_CHIPS)]
    return ref.at[_ds(shard, r // N_CHIPS), :]


def _place():
    x, y, c = lax.axis_index("x"), lax.axis_index("y"), lax.axis_index("c")
    chips = [(1 - x, y), (x, 1 - y), (1 - x, 1 - y)]
    return x, y, c, chips


LATE = (3, 4)
MID = (1, 2)
FIRST = (0,)
SMALL_HALF = (SMALL_ROWS // 2, LANES)


def _remote(src, dst, send_sem, recv_sem, to):
    return pltpu.make_async_remote_copy(src_ref=src, dst_ref=dst, send_sem=send_sem, recv_sem=recv_sem,
                                        device_id=to, device_id_type=MESH)


def _same(arrays):
    return [jax.ShapeDtypeStruct(a.shape, a.dtype) for a in arrays]


def _x_gather_send(ws, fulls, to_both=()):
    def copies(ins, outs, send_sems, recv_sems):
        x, y, c, chips = _place()
        out, at = [], 0
        for j, w in enumerate(ws):
            _, r, cc, axis = BIG[w]
            mine = _region_of_full(outs[j], r, cc, axis, 2 * x + y, c)
            for other_core in range(2 if w in to_both else 1):
                to_core = (1 - c) if other_core else c
                out += [_remote(mine, mine, send_sems.at[at + k], recv_sems.at[at + k], (cx, cy, to_core))
                        for k, (cx, cy) in enumerate(chips)]
                at += 3
        return out

    n = sum(6 if w in to_both else 3 for w in ws)
    return _Exchange(fulls, _same(fulls), {j: j for j in range(len(ws))}, n, copies)


def _x_gather_pass(ws, fulls):
    def copies(ins, outs, send_sems, recv_sems):
        x, y, c, chips = _place()
        out = []
        for j, w in enumerate(ws):
            _, r, cc, axis = BIG[w]
            for k, (cx, cy) in enumerate(chips):
                landed = _region_of_full(outs[j], r, cc, axis, 2 * cx + cy, c)
                out.append(_remote(landed, landed, send_sems.at[3 * j + k], recv_sems.at[3 * j + k], (x, y, 1 - c)))
        return out

    return _Exchange(fulls, _same(fulls), {j: j for j in range(len(ws))}, 3 * len(ws), copies)


def _x_pair(ws, grads, small=None):
    def copies(ins, outs, send_sems, recv_sems):
        x, y, c, _ = _place()
        out = []
        for j, w in enumerate(ws):
            _, r, cc, axis = BIG[w]
            out.append(_remote(_half_of_full(ins[j], r, cc, axis, 1 - c), outs[j], send_sems.at[j], recv_sems.at[j], (x, y, 1 - c)))
        if small is not None:
            j = len(ws)
            out.append(_remote(ins[j].at[_ds(1 - c, SMALL_ROWS // 2), :], outs[j], send_sems.at[j], recv_sems.at[j], (x, y, 1 - c)))
        return out

    shapes = [jax.ShapeDtypeStruct(_half_shape(*BIG[w][1:]), F32) for w in ws]
    extra = [] if small is None else [small]
    return _Exchange(list(grads) + extra, shapes + [jax.ShapeDtypeStruct(SMALL_HALF, F32)] * len(extra), {},
                     len(ws) + len(extra), copies)


def _x_chip(ws, sums_bf16, small_sum=None):
    def copies(ins, outs, send_sems, recv_sems):
        x, y, c, chips = _place()
        out = []
        for j, w in enumerate(ws):
            _, r, cc, axis = BIG[w]
            out += [_remote(_region_of_half(ins[j], r, cc, axis, 2 * cx + cy), outs[j].at[k],
                            send_sems.at[3 * j + k], recv_sems.at[3 * j + k], (cx, cy, c)) for k, (cx, cy) in enumerate(chips)]
        if small_sum is not None:
            j = len(ws)
            out += [_remote(ins[j], outs[j].at[k], send_sems.at[3 * j + k], recv_sems.at[3 * j + k], (cx, cy, c))
                    for k, (cx, cy) in enumerate(chips)]
        return out

    shapes = [jax.ShapeDtypeStruct((3,) + _region_shape(*BIG[w][1:]), BF16) for w in ws]
    extra = [] if small_sum is None else [small_sum]
    return _Exchange(list(sums_bf16) + extra, shapes + [jax.ShapeDtypeStruct((3,) + SMALL_HALF, F32)] * len(extra), {},
                     3 * (len(ws) + len(extra)), copies)


def _x_share(ws, shards, small=None):
    def copies(ins, outs, send_sems, recv_sems):
        x, y, c, _ = _place()
        out = []
        for j, w in enumerate(ws):
            _, r, cc, axis = BIG[w]
            mine = _half_of_shard(outs[j], r, cc, axis, c)
            out.append(_remote(mine, mine, send_sems.at[j], recv_sems.at[j], (x, y, 1 - c)))
        if small is not None:
            j = len(ws)
            mine = outs[j].at[_ds(c, SMALL_ROWS // 2), :]
            out.append(_remote(mine, mine, send_sems.at[j], recv_sems.at[j], (x, y, 1 - c)))
        return out

    arrays = list(shards) + ([] if small is None else [small])
    return _Exchange(arrays, _same(arrays), {j: j for j in range(len(arrays))}, len(arrays), copies)


class _Offset:
    def __init__(self, sems, base):
        self._sems, self._base = sems, base

    @property
    def at(self):
        return self

    def __getitem__(self, i):
        return self._sems.at[self._base + i]


def _x_join(a, b):
    ka_in, ka_out = len(a.operands), len(a.out_shapes)

    def copies(ins, outs, send_sems, recv_sems):
        return (a.copies(ins[:ka_in], outs[:ka_out], send_sems, recv_sems)
                + b.copies(ins[ka_in:], outs[ka_out:], _Offset(send_sems, a.n), _Offset(recv_sems, a.n)))

    aliases = dict(a.aliases)
    aliases.update({ka_in + i: ka_out + o for i, o in b.aliases.items()})
    return _Exchange(a.operands + b.operands, a.out_shapes + b.out_shapes, aliases, a.n + b.n, copies)


def _run_exchange(name, exchange):
    k_in, k_out = len(exchange.operands), len(exchange.out_shapes)

    def body(*refs):
        copies = exchange.copies(refs[:k_in], refs[k_in:k_in + k_out], refs[-2], refs[-1])
        for cp in copies:
            cp.start()
        for cp in copies:
            cp.wait()

    return list(pl.pallas_call(
        body, name=name, in_specs=[ANY] * k_in, out_specs=[ANY] * k_out, out_shape=exchange.out_shapes,
        scratch_shapes=[pltpu.SemaphoreType.DMA((exchange.n,)), pltpu.SemaphoreType.DMA((exchange.n,))],
        input_output_aliases=dict(exchange.aliases),
    )(*exchange.operands))


def _gather_now(name, ws, fulls):
    send, onward = _x_gather_send(ws, fulls), _x_gather_pass(ws, fulls)
    k = len(fulls)

    def body(*refs):
        ins, outs, (send_a, recv_a, send_b, recv_b) = refs[:k], refs[k:2 * k], refs[2 * k:]
        first = send.copies(ins, outs, send_a, recv_a)
        second = onward.copies(ins, outs, send_b, recv_b)
        for cp in first:
            cp.start()
        for landed, cp in zip(first, second):
            landed.wait()
            cp.start()
        for cp in second:
            cp.wait()

    sems = lambda n: pltpu.SemaphoreType.DMA((n,))
    return list(pl.pallas_call(
        body, name=name, in_specs=[ANY] * k, out_specs=[ANY] * k, out_shape=send.out_shapes,
        scratch_shapes=[sems(send.n), sems(send.n), sems(onward.n), sems(onward.n)],
        input_output_aliases={j: j for j in range(k)},
    )(*fulls))


def _tile_rows(shape):
    return (shape[0] // EXCH_TILES, shape[1])


def _cast_place(ids, shards):
    def body(ids_ref, *refs):
        for j in range(len(BIG)):
            refs[len(BIG) + j][...] = refs[j][...].astype(BF16)

    def out_spec(w):
        _, r, cc, axis = BIG[w]
        tile = _tile_rows(_shard_shape(r, cc, axis))
        if axis == 1:
            return pl.BlockSpec(tile, lambda i, ids: (i, ids[1]))
        return pl.BlockSpec(tile, lambda i, ids: (ids[1] * EXCH_TILES + i, 0))

    return list(pl.pallas_call(
        body, name="cast_place",
        grid_spec=pltpu.PrefetchScalarGridSpec(
            num_scalar_prefetch=1, grid=(EXCH_TILES,),
            in_specs=[pl.BlockSpec(_tile_rows(_shard_shape(r, cc, axis)), lambda i, ids: (i, 0)) for _, r, cc, axis in BIG],
            out_specs=[out_spec(w) for w in range(len(BIG))]),
        out_shape=[jax.ShapeDtypeStruct((r, cc), BF16) for _, r, cc, _ in BIG],
        compiler_params=pltpu.CompilerParams(dimension_semantics=("parallel",)),
    )(ids, *shards))


def _pair_sum(ids, ws, grads, got, small=None):
    nb = len(ws)
    n = nb + (small is not None)
    halves = [_half_shape(*BIG[w][1:]) for w in ws] + [SMALL_HALF] * (n - nb)

    def body(ids_ref, *refs):
        for j in range(n):
            total = refs[j][...] + refs[n + j][...]
            refs[2 * n + j][...] = total.astype(refs[2 * n + j].dtype)

    def mine_spec(j):
        tile = _tile_rows(halves[j])
        if j < nb and BIG[ws[j]][3] == 0:
            return pl.BlockSpec(tile, lambda i, ids: (i, ids[0]))
        return pl.BlockSpec(tile, lambda i, ids: (ids[0] * EXCH_TILES + i, 0))

    plain = lambda j: pl.BlockSpec(_tile_rows(halves[j]), lambda i, ids: (i, 0))
    outs = pl.pallas_call(
        body, name="pair_sum",
        grid_spec=pltpu.PrefetchScalarGridSpec(
            num_scalar_prefetch=1, grid=(EXCH_TILES,),
            in_specs=[mine_spec(j) for j in range(n)] + [plain(j) for j in range(n)],
            out_specs=[plain(j) for j in range(n)]),
        out_shape=[jax.ShapeDtypeStruct(h, BF16 if j < nb else F32) for j, h in enumerate(halves)],
        compiler_params=pltpu.CompilerParams(dimension_semantics=("parallel",)),
    )(ids, *grads, *([] if small is None else [small]), *got)
    return list(outs)


def _chip_sum(ids, ws, sums, arrived):
    nb, n = len(ws), len(sums)
    regions = [_region_shape(*BIG[w][1:]) for w in ws] + [SMALL_HALF] * (n - nb)

    def body(ids_ref, *refs):
        for j in range(n):
            own, got, out = refs[j], refs[n + j], refs[2 * n + j]
            if j < nb:
                out[...] = ((own[...].astype(F32) + got[0].astype(F32)) + got[1].astype(F32)) + got[2].astype(F32)
            else:
                out[...] = (own[...] + got[1]) + (got[0] + got[2])

    def own_spec(j):
        tile = _tile_rows(regions[j])
        if j >= nb:
            return pl.BlockSpec(tile, lambda i, ids: (i, 0))
        if BIG[ws[j]][3] == 1:
            return pl.BlockSpec(tile, lambda i, ids: (i, ids[1]))
        return pl.BlockSpec(tile, lambda i, ids: (ids[1] * EXCH_TILES + i, 0))

    def out_spec(j):
        tile = _tile_rows(regions[j])
        if j < nb and BIG[ws[j]][3] == 0:
            return pl.BlockSpec(tile, lambda i, ids: (i, ids[0]))
        return pl.BlockSpec(tile, lambda i, ids: (ids[0] * EXCH_TILES + i, 0))

    got_spec = lambda j: pl.BlockSpec((3,) + _tile_rows(regions[j]), lambda i, ids: (0, i, 0))
    return list(pl.pallas_call(
        body, name="chip_sum",
        grid_spec=pltpu.PrefetchScalarGridSpec(
            num_scalar_prefetch=1, grid=(EXCH_TILES,),
            in_specs=[own_spec(j) for j in range(n)] + [got_spec(j) for j in range(n)],
            out_specs=[out_spec(j) for j in range(n)]),
        out_shape=[jax.ShapeDtypeStruct(_shard_shape(*BIG[w][1:]), F32) for w in ws]
        + [jax.ShapeDtypeStruct((SMALL_ROWS, LANES), F32)] * (n - nb),
        compiler_params=pltpu.CompilerParams(dimension_semantics=("parallel",)),
    )(ids, *sums, *arrived))


def _adamw_body(n):
    c1 = 1.0 - ADAM_B1 ** ADAM_STEP
    c2 = 1.0 - ADAM_B2 ** ADAM_STEP

    def body(*refs):
        for j in range(n):
            w, g, m, v = (refs[k * n + j][...] for k in range(4))
            m = ADAM_B1 * m + (1.0 - ADAM_B1) * g
            v = ADAM_B2 * v + (1.0 - ADAM_B2) * (g * g)
            refs[4 * n + j][...] = -ADAM_LR * ((m / c1) / (jnp.sqrt(v / c2) + ADAM_EPS) + ADAM_WD * w)
            refs[5 * n + j][...] = m
            refs[6 * n + j][...] = v

    return body


def _adamw(name, ws, gs, ms, vs):
    n = len(ws)
    specs = [pl.BlockSpec(_tile_rows(w.shape), lambda i: (i, 0)) for w in ws]
    shapes = [jax.ShapeDtypeStruct(w.shape, F32) for w in ws]
    out, _ = _call(_adamw_body(n), name, (EXCH_TILES,), specs * 4, specs * 3, shapes * 3, [], (*ws, *gs, *ms, *vs))
    return out[:n], out[n:2 * n], out[2 * n:]


def _adamw_whole(name, ws, gs, ms, vs):
    n = len(ws)
    out = pl.pallas_call(_adamw_body(n), name=name, out_shape=[jax.ShapeDtypeStruct(w.shape, F32) for w in ws] * 3)(*ws, *gs, *ms, *vs)
    return out[:n], out[n:2 * n], out[2 * n:]


def kernel(x, norm_mix_pre, norm_mix_post, w_in, ret_gn_gain, ssm_lambda_re, ssm_lambda_im, ssm_log_dt, ssm_b_re, ssm_b_im, ssm_c_re, ssm_c_im, ssm_d, w_glu, w_out, norm_mlp_pre, norm_mlp_post, w_ff1, w_ff2, loss_target, m_norm_mix_pre, m_norm_mix_post, m_w_in, m_ret_gn_gain, m_ssm_lambda_re, m_ssm_lambda_im, m_ssm_log_dt, m_ssm_b_re, m_ssm_b_im, m_ssm_c_re, m_ssm_c_im, m_ssm_d, m_w_glu, m_w_out, m_norm_mlp_pre, m_norm_mlp_post, m_w_ff1, m_w_ff2, v_norm_mix_pre, v_norm_mix_post, v_w_in, v_ret_gn_gain, v_ssm_lambda_re, v_ssm_lambda_im, v_ssm_log_dt, v_ssm_b_re, v_ssm_b_im, v_ssm_c_re, v_ssm_c_im, v_ssm_d, v_w_glu, v_w_out, v_norm_mlp_pre, v_norm_mlp_post, v_w_ff1, v_w_ff2):
    given = dict(locals())
    order = ["norm_mix_pre", "norm_mix_post", "w_in", "ret_gn_gain", "ssm_lambda_re", "ssm_lambda_im", "ssm_log_dt",
             "ssm_b_re", "ssm_b_im", "ssm_c_re", "ssm_c_im", "ssm_d", "w_glu", "w_out", "norm_mlp_pre", "norm_mlp_post",
             "w_ff1", "w_ff2"]
    big_names = [name for name, _, _, _ in BIG]
    ids = jnp.stack([lax.axis_index("c"), 2 * lax.axis_index("x") + lax.axis_index("y")]).astype(jnp.int32)

    weights = _cast_place(ids, [given[name][0] for name in big_names])
    w_in = _gather_now("gather_w_in", FIRST, weights[:1])
    small_w = {name: given[name][0] for name, _ in SMALL}
    late_state = [[given[prefix + big_names[w]][0] for w in LATE] for prefix in ("", "m_", "v_")]
    dx, d_in, mid_parts, late_grads, late_stepped, small_grads = _local_step(
        x[0], loss_target[0], small_w, w_in + weights[1:], ids, late_state)

    small_packed = _pack_small(small_grads)
    got = _run_exchange("pair_exchange", _x_join(_x_share(MID, mid_parts), _x_pair(FIRST, [d_in], small_packed)))
    mid_grads, got = got[:len(MID)], got[len(MID):]
    sums = _pair_sum(ids, FIRST, [d_in], got, small_packed)
    arrived = _run_exchange("chip_exchange", _x_chip(FIRST, sums[:-1], sums[-1]))
    parts = _chip_sum(ids, FIRST, sums, arrived)
    g_in, g_small = _run_exchange("pair_share", _x_share(FIRST, parts[:-1], parts[-1]))
    big_grads = [g_in] + mid_grads + late_grads
    rest = lambda prefix: [given[prefix + big_names[w]][0] for w in FIRST + MID]
    rest_out = _adamw("adamw_big", rest(""), big_grads[:len(FIRST + MID)], rest("m_"), rest("v_"))
    big_out = [list(rest_out[k]) + list(late_stepped[k]) for k in range(3)]
    small_names = [name for name, _ in SMALL]
    small_tree = _unpack_small(g_small)
    small_grads = [small_tree[name] for name in small_names]
    small = lambda prefix: [given[prefix + name] for name in small_names]
    small_out = _adamw_whole("adamw_small", small(""), small_grads, small("m_"), small("v_"))

    def in_order(big_arrays, small_arrays):
        tree = {name: a[None] for name, a in zip(big_names, big_arrays)}
        tree.update(zip(small_names, small_arrays))
        return [tree[name] for name in order]

    loss = small_tree["loss_sum"].reshape(())
    return (loss, dx[None], *in_order(big_grads, small_grads), *in_order(big_out[0], small_out[0]),
            *in_order(big_out[1], small_out[1]), *in_order(big_out[2], small_out[2]))
```

```python
import math

import jax
import jax.numpy as jnp
import numpy as np
from jax import lax
from jax.experimental import pallas as pl
from jax.experimental.pallas import tpu as pltpu

F32 = jnp.float32
BF16 = jnp.bfloat16

D_MODEL = 1024
RET_WIDTH = 512
RET_HEADS = 4
HEAD_DIM = 128
RET_CHUNK = 128
ROPE_BASE = 10000.0
SSM_WIDTH = 512
SSM_GROUP = 16
SSM_GROUPS = 32
SSM_STATE = 64
N_STATES = SSM_GROUPS * SSM_STATE
D_FF = 4096
IN_COLS = 4 * RET_WIDTH + SSM_WIDTH
NORM_EPS = 1e-6
K_SCALE = HEAD_DIM ** -0.5

ADAM_LR = 0.001
ADAM_B1 = 0.9
ADAM_B2 = 0.999
ADAM_EPS = 1e-08
ADAM_WD = 0.01
ADAM_STEP = 10

LANES = 128
SUBLANES = 8
VMEM_LIMIT = 56 * 2 ** 20

TOK_TILE = 512
MLP_TILE = 256
RET_CHUNKS_PER_STEP = 4
S5_TILE = 256
S5_STEPS = S5_TILE // SUBLANES
S5_COLBLK = N_STATES // LANES
S5_GROUP = 4

N_CHIPS = 4
MESH = pl.DeviceIdType.MESH


def _dot(a, b):
    return jnp.dot(a, b, preferred_element_type=F32)


def _dot_nt(a, b):
    return lax.dot_general(a, b, (((1,), (1,)), ((), ())), preferred_element_type=F32)


def _dot_tn(a, b):
    return lax.dot_general(a, b, (((0,), (0,)), ((), ())), preferred_element_type=F32)


def _sigmoid(x):
    return 1.0 / (1.0 + jnp.exp(-x))


_GELU_C = math.sqrt(2.0 / math.pi)


def _gelu(x):
    return 0.5 * x * (1.0 + jnp.tanh(_GELU_C * (x + 0.044715 * (x * x * x))))


def _gelu_grad(x):
    t = jnp.tanh(_GELU_C * (x + 0.044715 * (x * x * x)))
    return 0.5 * (1.0 + t) + 0.5 * x * (1.0 - t * t) * (_GELU_C * (1.0 + 3.0 * 0.044715 * (x * x)))


def _rms(x):
    r = lax.rsqrt(jnp.mean(x * x, axis=-1, keepdims=True) + NORM_EPS)
    return x * r, r


def _rms_bwd(n, r, gain, dy):
    dn = dy * gain
    dx = r * (dn - n * jnp.mean(dn * n, axis=-1, keepdims=True))
    return dx, jnp.sum(dy * n, axis=0, keepdims=True)


def _full(shape):
    nd = len(shape)
    return pl.BlockSpec(shape, lambda i, _n=nd: (0,) * _n, pipeline_mode=pl.Buffered(1))


def _rows(tile, width):
    return pl.BlockSpec((tile, width), lambda i: (i, 0))


def _rows_rev(tile, width, n):
    return pl.BlockSpec((tile, width), lambda i, _n=n: (_n - 1 - i, 0))


def _params(sem):
    return pltpu.CompilerParams(dimension_semantics=(sem,), vmem_limit_bytes=VMEM_LIMIT)


ANY = pl.BlockSpec(memory_space=pl.ANY)


class _Exchange:
    def __init__(self, operands, out_shapes, aliases, n, copies):
        self.operands, self.out_shapes, self.aliases, self.n, self.copies = list(operands), list(out_shapes), aliases, n, copies


def _call(body, name, grid, in_specs, out_specs, out_shape, scratch_shapes, args, exchange=None):
    if exchange is None:
        outs = pl.pallas_call(body, name=name, grid=grid, in_specs=in_specs, out_specs=out_specs, out_shape=out_shape,
                              scratch_shapes=scratch_shapes, compiler_params=_params("arbitrary"))(*args)
        return list(outs), []
    n_in, n_out, n_scr = len(in_specs), len(out_specs), len(scratch_shapes)
    k_in, k_out = len(exchange.operands), len(exchange.out_shapes)
    last = grid[0] - 1

    def hosted(*refs):
        own_in, rest = refs[:n_in], refs[n_in:]
        ex_in, rest = rest[:k_in], rest[k_in:]
        own_out, rest = rest[:n_out], rest[n_out:]
        ex_out, rest = rest[:k_out], rest[k_out:]
        own_scr, (send_sems, recv_sems) = rest[:n_scr], rest[n_scr:]

        @pl.when(pl.program_id(0) == 0)
        def _():
            for cp in exchange.copies(ex_in, ex_out, send_sems, recv_sems):
                cp.start()

        body(*own_in, *own_out, *own_scr)

        @pl.when(pl.program_id(0) == last)
        def _():
            for cp in exchange.copies(ex_in, ex_out, send_sems, recv_sems):
                cp.wait()

    outs = pl.pallas_call(
        hosted, name=name, grid=grid, in_specs=list(in_specs) + [ANY] * k_in, out_specs=list(out_specs) + [ANY] * k_out,
        out_shape=list(out_shape) + exchange.out_shapes,
        scratch_shapes=list(scratch_shapes) + [pltpu.SemaphoreType.DMA((exchange.n,)), pltpu.SemaphoreType.DMA((exchange.n,))],
        input_output_aliases={n_in + i: n_out + o for i, o in exchange.aliases.items()},
        compiler_params=_params("arbitrary"),
    )(*args, *exchange.operands)
    return list(outs[:n_out]), list(outs[n_out:])


def _rope(t, cos2, sin2):
    return t * cos2 + pltpu.roll(t, HEAD_DIM // 2, 1) * sin2


def _rope_bwd(d, cos2, sin2):
    return d * cos2 + pltpu.roll(d * sin2, HEAD_DIM // 2, d.ndim - 1)


def _inproj_bwd(x, gain, w_in, dpieces, dres, adam=()):
    seq = x.shape[0]
    t = TOK_TILE
    steps = seq // t
    n_pieces = len(dpieces)
    n_adam = len(adam[0]) if adam else 0
    adam_step = _adamw_body(n_adam)

    def body(x_ref, g_ref, w_ref, *refs):
        piece_refs, (dres_ref, *adam_in), (dx_ref, dw_ref, dg_ref, *adam_out) = (
            refs[:n_pieces], refs[n_pieces:n_pieces + 1 + 4 * n_adam], refs[n_pieces + 1 + 4 * n_adam:])

        @pl.when(pl.program_id(0) == 0)
        def _():
            dw_ref[...] = jnp.zeros_like(dw_ref)
            dg_ref[...] = jnp.zeros_like(dg_ref)

        adam_step(*adam_in, *adam_out)

        n, r = _rms(x_ref[...])
        gain_v = g_ref[...]
        h = (n * gain_v).astype(BF16)
        dh = None
        for j, piece in enumerate(piece_refs):
            cols = slice(j * RET_WIDTH, (j + 1) * RET_WIDTH)
            dp = piece[...]
            dw_ref[:, cols] += _dot_tn(h, dp)
            term = _dot_nt(dp, w_ref[:, cols])
            dh = term if dh is None else dh + term
        dx, dgs = _rms_bwd(n, r, gain_v, dh)
        dg_ref[...] += dgs
        dx_ref[...] = dres_ref[...] + dx

    adam_arrays = [a for group in adam for a in group]
    adam_spec = lambda a: pl.BlockSpec((a.shape[0] // steps, a.shape[1]), lambda i: (i, 0))
    outs, _ = _call(
        body, "inproj_bwd", (steps,),
        [_rows(t, D_MODEL), _full((1, D_MODEL)), _full((D_MODEL, IN_COLS))] + [_rows(t, RET_WIDTH)] * n_pieces + [_rows(t, D_MODEL)]
        + [adam_spec(a) for a in adam_arrays],
        [_rows(t, D_MODEL), _full((D_MODEL, IN_COLS)), _full((1, D_MODEL))] + [adam_spec(a) for a in adam_arrays[:n_adam]] * 3,
        [jax.ShapeDtypeStruct((seq, D_MODEL), F32), jax.ShapeDtypeStruct((D_MODEL, IN_COLS), F32),
         jax.ShapeDtypeStruct((1, D_MODEL), F32)] + [jax.ShapeDtypeStruct(a.shape, F32) for a in adam_arrays[:n_adam]] * 3,
        [], (x, gain, w_in, *dpieces, dres, *adam_arrays))
    stepped = outs[3:]
    return outs[:3], (stepped[:n_adam], stepped[n_adam:2 * n_adam], stepped[2 * n_adam:])


def _ret_consts():
    c = RET_CHUNK
    f32 = np.float32
    log_gamma = np.log(f32(1.0) - np.exp(np.linspace(math.log(1.0 / 32), math.log(1.0 / 512), RET_HEADS, dtype=f32))).astype(f32)
    idx = np.arange(c, dtype=f32)
    diff = idx[:, None] - idx[None, :]
    decay = np.where(diff[None] >= 0, np.exp(np.maximum(diff, f32(0.0))[None] * log_gamma[:, None, None]), f32(0.0))
    zeta = np.exp((c - 1 - idx)[None, :] * log_gamma[:, None])
    xi = np.exp((idx + f32(1.0))[None, :] * log_gamma[:, None])
    g_chunk = np.exp(f32(c) * log_gamma)
    wide = lambda rowvals: jnp.asarray(np.broadcast_to(rowvals[:, :, None], (RET_HEADS, c, c)).astype(f32))
    return (jnp.asarray(decay.astype(f32)), wide(xi), wide(zeta),
            jnp.asarray(np.broadcast_to(g_chunk[:, None, None], (RET_HEADS, c, c)).astype(f32)))


def _rope_tables(seq):
    f32 = np.float32
    half = HEAD_DIM // 2
    inv_freq = np.power(f32(ROPE_BASE), -np.arange(half, dtype=f32) / f32(half)).astype(f32)
    ang = (np.arange(seq, dtype=f32)[:, None] * inv_freq[None, :]).astype(f32)
    cos, sin = np.cos(ang).astype(f32), np.sin(ang).astype(f32)
    return jnp.asarray(np.concatenate([cos, cos], axis=1)), jnp.asarray(np.concatenate([-sin, sin], axis=1))


def _bmm(a, b):
    return lax.dot_general(a, b, (((2,), (1,)), ((0,), (0,))), preferred_element_type=F32)


def _bmm_nt(a, b):
    return lax.dot_general(a, b, (((2,), (2,)), ((0,), (0,))), preferred_element_type=F32)


def _bmm_tn(a, b):
    return lax.dot_general(a, b, (((1,), (1,)), ((0,), (0,))), preferred_element_type=F32)


def _ret_blocks(ref):
    return jnp.stack([ref[cc * RET_CHUNK:(cc + 1) * RET_CHUNK, hd * HEAD_DIM:(hd + 1) * HEAD_DIM]
                      for cc in range(RET_CHUNKS_PER_STEP) for hd in range(RET_HEADS)])


def _ret_unblock(ref, blocks):
    for cc in range(RET_CHUNKS_PER_STEP):
        for hd in range(RET_HEADS):
            ref[cc * RET_CHUNK:(cc + 1) * RET_CHUNK, hd * HEAD_DIM:(hd + 1) * HEAD_DIM] = blocks[cc * RET_HEADS + hd].astype(ref.dtype)


def _rope_blocks(ref):
    return jnp.stack([ref[cc * RET_CHUNK:(cc + 1) * RET_CHUNK, :] for cc in range(RET_CHUNKS_PER_STEP) for _ in range(RET_HEADS)])


def _per_chunk(ref):
    return jnp.concatenate([ref[...]] * RET_CHUNKS_PER_STEP, axis=0)


def _ret_gain(gain_ref):
    return jnp.stack([gain_ref[:, hd * HEAD_DIM:(hd + 1) * HEAD_DIM]
                      for _ in range(RET_CHUNKS_PER_STEP) for hd in range(RET_HEADS)])


def _ret_fwd_step(q_ref, k_ref, v_ref, gate_ref, gain_ref, dec_ref, xi_ref, zeta_ref, gch_ref, y_ref, rprev_ref, state):
    cps = RET_CHUNKS_PER_STEP
    qb, kb, vb = _ret_blocks(q_ref), _ret_blocks(k_ref), _ret_blocks(v_ref)
    s = _bmm_nt(qb, kb) * _per_chunk(dec_ref)
    vz = (vb.astype(F32) * _per_chunk(zeta_ref)).astype(BF16)
    incr = _bmm_tn(kb, vz)
    gch_v = gch_ref[...]
    rp = state[...]
    before = []
    for cc in range(cps):
        before.append(rp.astype(BF16))
        rprev_ref[cc] = before[-1]
        rp = gch_v * rp + incr[cc * RET_HEADS:(cc + 1) * RET_HEADS]
    state[...] = rp
    qx = (qb.astype(F32) * _per_chunk(xi_ref)).astype(BF16)
    o = _bmm(s.astype(BF16), vb) + _bmm(qx, jnp.concatenate(before, axis=0))
    oc = o - jnp.mean(o, axis=-1, keepdims=True)
    on = oc * lax.rsqrt(jnp.mean(oc * oc, axis=-1, keepdims=True) + NORM_EPS)
    g = _ret_blocks(gate_ref)
    _ret_unblock(y_ref, g * _sigmoid(g) * (on * _ret_gain(gain_ref)))


def _ret_bwd_step(q_ref, k_ref, v_ref, gate_ref, gain_ref, dec_ref, xi_ref, zeta_ref, gch_ref, rprev_ref, dyr_ref, cos_ref, sin_ref,
                  dq_ref, dk_ref, dv_ref, dgate_ref, dgain_ref, dstate):
    cps = RET_CHUNKS_PER_STEP
    nb = cps * RET_HEADS
    qb, kb, vb = _ret_blocks(q_ref), _ret_blocks(k_ref), _ret_blocks(v_ref)
    dec_b, xi_b, zeta_b = _per_chunk(dec_ref), _per_chunk(xi_ref), _per_chunk(zeta_ref)
    rpb = rprev_ref[...].reshape(nb, HEAD_DIM, HEAD_DIM)
    sdb = (_bmm_nt(qb, kb) * dec_b).astype(BF16)
    qx = (qb.astype(F32) * xi_b).astype(BF16)
    o = _bmm(sdb, vb) + _bmm(qx, rpb)
    oc = o - jnp.mean(o, axis=-1, keepdims=True)
    rstd = lax.rsqrt(jnp.mean(oc * oc, axis=-1, keepdims=True) + NORM_EPS)
    on = oc * rstd
    g = _ret_blocks(gate_ref)
    sg = _sigmoid(g)
    gain_b = _ret_gain(gain_ref)
    dyr = _ret_blocks(dyr_ref)
    _ret_unblock(dgate_ref, dyr * (on * gain_b) * (sg * (1.0 + g * (1.0 - sg))))
    dy = dyr * (g * sg)
    dgain = jnp.sum(dy * on, axis=1, keepdims=True)
    for hd in range(RET_HEADS):
        part = dgain[hd]
        for cc in range(1, cps):
            part = part + dgain[cc * RET_HEADS + hd]
        dgain_ref[:, hd * HEAD_DIM:(hd + 1) * HEAD_DIM] += part
    don = dy * gain_b
    do = rstd * (don - jnp.mean(don, axis=-1, keepdims=True) - on * jnp.mean(don * on, axis=-1, keepdims=True))
    dob = do.astype(BF16)
    ds = (_bmm_nt(dob, vb) * dec_b).astype(BF16)
    dq = _bmm(ds, kb) + _bmm_nt(dob, rpb) * xi_b
    dk = _bmm_tn(ds, qb)
    dv = _bmm_tn(sdb, dob)
    dstate_local = _bmm_tn(qx, dob)
    vz = (vb.astype(F32) * zeta_b).astype(BF16)
    gch_v = gch_ref[...]
    zeta_v = zeta_ref[...]
    gh = dstate[...]
    dk_extra, dv_extra = [None] * cps, [None] * cps
    for cc in reversed(range(cps)):
        sl = slice(cc * RET_HEADS, (cc + 1) * RET_HEADS)
        gb = gh.astype(BF16)
        dk_extra[cc] = _bmm_nt(vz[sl], gb)
        dv_extra[cc] = _bmm(kb[sl], gb) * zeta_v
        gh = dstate_local[sl] + gch_v * gh
    dstate[...] = gh
    cos_b, sin_b = _rope_blocks(cos_ref), _rope_blocks(sin_ref)
    _ret_unblock(dq_ref, _rope_bwd(dq, cos_b, sin_b))
    _ret_unblock(dk_ref, _rope_bwd((dk + jnp.concatenate(dk_extra, axis=0)) * K_SCALE, cos_b, sin_b))
    _ret_unblock(dv_ref, dv + jnp.concatenate(dv_extra, axis=0))


def _ret_bwd(q, k, v, gate, gn_gain, consts, rprev, dy_ret, cos2, sin2, exchange=None):
    seq = q.shape[0]
    cps = RET_CHUNKS_PER_STEP
    t = cps * RET_CHUNK
    nt = seq // t
    dec, xi, zeta, gch = consts

    def body(*refs):
        dgain_ref, dstate = refs[-2:]

        @pl.when(pl.program_id(0) == 0)
        def _():
            dstate[...] = jnp.zeros_like(dstate)
            dgain_ref[...] = jnp.zeros_like(dgain_ref)

        _ret_bwd_step(*refs)

    cst = _full((RET_HEADS, RET_CHUNK, RET_CHUNK))
    rev = _rows_rev(t, RET_WIDTH, nt)
    act = jax.ShapeDtypeStruct((seq, RET_WIDTH), BF16)
    return _call(
        body, "ret_bwd", (nt,),
        [rev] * 4 + [_full((1, RET_WIDTH)), cst, cst, cst, cst,
                     pl.BlockSpec((cps, RET_HEADS, HEAD_DIM, HEAD_DIM), lambda i: (nt - 1 - i, 0, 0, 0)), rev,
                     _rows_rev(t, HEAD_DIM, nt), _rows_rev(t, HEAD_DIM, nt)],
        [rev] * 4 + [_full((1, RET_WIDTH))], [act, act, act, act, jax.ShapeDtypeStruct((1, RET_WIDTH), F32)],
        [pltpu.VMEM((RET_HEADS, HEAD_DIM, HEAD_DIM), F32)],
        (q, k, v, gate, gn_gain, dec, xi, zeta, gch, rprev, dy_ret, cos2, sin2), exchange)


MXU_TILE = 256
S5_STATE_TILES = 2 * N_STATES // MXU_TILE
S5_CHANNELS_PER_TILE = MXU_TILE // SSM_STATE * SSM_GROUP
S5_TILES_SHAPE = (S5_STATE_TILES, LANES, MXU_TILE)


def _chan_block(kk):
    return ((kk % (S5_STATE_TILES // 2)) * S5_CHANNELS_PER_TILE) // LANES * LANES


def _s5_tiles(blocks_re, blocks_im):
    per = MXU_TILE // SSM_STATE
    half = S5_STATE_TILES // 2
    eye = jnp.eye(per, dtype=F32)
    in_half = (jnp.arange(half) % (LANES // S5_CHANNELS_PER_TILE))[:, None] == jnp.arange(LANES // S5_CHANNELS_PER_TILE)[None, :]
    out = []
    for blk in (blocks_re, blocks_im):
        t = blk.reshape(half, per, SSM_GROUP, 1, SSM_STATE) * eye[None, :, None, :, None]
        t = t.reshape(half, 1, S5_CHANNELS_PER_TILE, MXU_TILE)
        out.append(jnp.where(in_half[:, :, None, None], t, 0.0).reshape(half, LANES, MXU_TILE))
    return jnp.concatenate(out, axis=0)


def _s5_discretise(lam_re, lam_im, log_dt, b_re, b_im, c_re, c_im):
    lam = lax.complex(jnp.minimum(lam_re, -1e-4), lam_im)
    dt = jnp.exp(log_dt)[:, None]
    lam_bar = jnp.exp(lam * dt)
    b_bar = ((lam_bar - 1.0) / lam)[:, :, None] * lax.complex(b_re, b_im)
    b_tiles = _s5_tiles(jnp.swapaxes(jnp.real(b_bar), 1, 2), jnp.swapaxes(jnp.imag(b_bar), 1, 2))
    return jnp.real(lam_bar), jnp.imag(lam_bar), b_tiles, _s5_tiles(c_re, -c_im)


def _s5_tables(lbr, lbi):
    lr = lbr.reshape(1, N_STATES)
    li = lbi.reshape(1, N_STATES)
    pr, pi = lr, li
    while pr.shape[0] < S5_STEPS:
        top_r, top_i = pr[-1:], pi[-1:]
        pr, pi = (jnp.concatenate([pr, pr * top_r - pi * top_i], axis=0), jnp.concatenate([pi, pr * top_i + pi * top_r], axis=0))
    bc = lambda a: jnp.broadcast_to(a, (SUBLANES, N_STATES))
    rep = lambda rows: jnp.broadcast_to(rows[:, None, :], (S5_STEPS, SUBLANES, N_STATES)).reshape(S5_STEPS * SUBLANES, N_STATES)
    return bc(lr), bc(li), rep(pr), rep(pi)


def _cmul(ar, ai, br, bi):
    return ar * br - ai * bi, ar * bi + ai * br


def _seg(i):
    return pl.ds(pl.multiple_of(i * SUBLANES, SUBLANES), SUBLANES)


def _scan_order():
    r = np.arange(S5_TILE)
    token = (r % SUBLANES) * S5_STEPS + r // SUBLANES
    p = (token[:, None] == np.arange(S5_TILE)[None, :]).astype(np.float32)
    return jnp.asarray(p, dtype=BF16), jnp.asarray(p.T, dtype=BF16)


def _to_scan_order(p_ref, rows_bf16):
    return _dot(p_ref[...], rows_bf16).astype(BF16)


def _from_scan_order(pt_ref, rows, pieces=3):
    pt = pt_ref[...]
    out = None
    for _ in range(pieces):
        piece = rows.astype(BF16)
        rows = rows - piece.astype(F32)
        term = _dot(pt, piece)
        out = term if out is None else out + term
    return out


def _s5_channels_to_states(chan_b, m_ref, xs):
    for kk in range(S5_STATE_TILES):
        lo = _chan_block(kk)
        res = _dot(chan_b[:, lo:lo + LANES], m_ref[kk])
        xs[2 * kk] = res[:, :LANES]
        xs[2 * kk + 1] = res[:, LANES:]


def _s5_states_to_channels(sb_ref, m_ref):
    out = []
    for lo in range(0, SSM_WIDTH, LANES):
        acc = None
        for kk in range(S5_STATE_TILES):
            if _chan_block(kk) == lo:
                term = _dot_nt(sb_ref[:, kk * MXU_TILE:(kk + 1) * MXU_TILE], m_ref[kk])
                acc = term if acc is None else acc + term
        out.append(acc)
    return jnp.concatenate(out, axis=1)


def _s5_outer_acc(acc_ref, chan_b, states_ref):
    for kk in range(S5_STATE_TILES):
        lo = _chan_block(kk)
        acc_ref[kk] += _dot_tn(chan_b[:, lo:lo + LANES], states_ref[:, kk * MXU_TILE:(kk + 1) * MXU_TILE])


def _s5_scan_tile(xs, lr_ref, li_ref, pr_ref, pi_ref, carry_re, carry_im, conj, reverse, prev_of=None):
    sgn = -1.0 if conj else 1.0
    seg_in_re, seg_in_im, sums_re, sums_im = [], [], [], []
    group = S5_GROUP
    for grp in range(S5_COLBLK // group):
        blks = [grp * group + j for j in range(group)]
        lrs = [lr_ref[:, b * LANES:(b + 1) * LANES] for b in blks]
        lis = [sgn * li_ref[:, b * LANES:(b + 1) * LANES] for b in blks]

        def step(it, carry, blks=blks, lrs=lrs, lis=lis):
            i = (S5_STEPS - 1 - it) if reverse else it
            out_r, out_i = [], []
            for j, b in enumerate(blks):
                xr, xi = _cmul(lrs[j], lis[j], carry[j], carry[group + j])
                xr = xr + xs[b, _seg(i), :]
                xi = xi + xs[S5_COLBLK + b, _seg(i), :]
                xs[b, _seg(i), :] = xr
                xs[S5_COLBLK + b, _seg(i), :] = xi
                out_r.append(xr)
                out_i.append(xi)
            return tuple(out_r + out_i)

        zeros = tuple(jnp.zeros((SUBLANES, LANES), F32) for _ in range(2 * group))
        ends = lax.fori_loop(0, S5_STEPS, step, zeros, unroll=True)

        ins_r, ins_i = [], []
        for j, b in enumerate(blks):
            cols = slice(b * LANES, (b + 1) * LANES)
            pr = pr_ref[SUBLANES * S5_STEPS - 1:SUBLANES * S5_STEPS, cols]
            pi = sgn * pi_ref[SUBLANES * S5_STEPS - 1:SUBLANES * S5_STEPS, cols]
            cur_r = carry_re[:, cols]
            cur_i = carry_im[:, cols]
            row_id = lax.broadcasted_iota(jnp.int32, (SUBLANES, LANES), 0)
            in_r = jnp.zeros((SUBLANES, LANES), F32)
            in_i = jnp.zeros((SUBLANES, LANES), F32)
            order = range(SUBLANES - 1, -1, -1) if reverse else range(SUBLANES)
            for sgm in order:
                in_r = jnp.where(row_id == sgm, cur_r, in_r)
                in_i = jnp.where(row_id == sgm, cur_i, in_i)
                mr, mi = _cmul(pr, pi, cur_r, cur_i)
                cur_r = mr + ends[j][sgm:sgm + 1, :]
                cur_i = mi + ends[group + j][sgm:sgm + 1, :]
            carry_re[:, cols] = cur_r
            carry_im[:, cols] = cur_i
            ins_r.append(in_r)
            ins_i.append(in_i)

        def fix(it, c, blks=blks, ins_r=ins_r, ins_i=ins_i):
            pw = (S5_STEPS - 1 - it) if reverse else it
            out = []
            for j, b in enumerate(blks):
                cols = slice(b * LANES, (b + 1) * LANES)
                prow = pl.ds(pl.multiple_of(pw * SUBLANES, SUBLANES), SUBLANES)
                fr, fi = _cmul(pr_ref[prow, cols], sgn * pi_ref[prow, cols], ins_r[j], ins_i[j])
                vr = xs[b, _seg(it), :] + fr
                vi = xs[S5_COLBLK + b, _seg(it), :] + fi
                xs[b, _seg(it), :] = vr
                xs[S5_COLBLK + b, _seg(it), :] = vi
                if prev_of is not None:
                    opr, opi, sr, si = c[4 * j:4 * j + 4]
                    out += [prev_of[0][b, _seg(it), :], prev_of[0][S5_COLBLK + b, _seg(it), :],
                            sr + vr * opr + vi * opi, si + vi * opr - vr * opi]
            return tuple(out)

        init = ()
        if prev_of is not None:
            zero = jnp.zeros((SUBLANES, LANES), F32)
            for b in blks:
                init += (prev_of[1][b], prev_of[2][b], zero, zero)
        done = lax.fori_loop(0, S5_STEPS, fix, init, unroll=True)
        for j in range(len(blks) if prev_of is not None else 0):
            sums_re.append(done[4 * j + 2])
            sums_im.append(done[4 * j + 3])
        seg_in_re += ins_r
        seg_in_im += ins_i
    return seg_in_re, seg_in_im, sums_re, sums_im


def _s5_pack(xs, dst):
    for b in range(2 * S5_COLBLK):
        dst[:, b * LANES:(b + 1) * LANES] = xs[b].astype(BF16)


def _inproj_s5_fwd(x, gain, w_in, cos2, sin2, b_blk, c_blk, tables, d_skip, exchange=None):
    seq = x.shape[0]
    t = TOK_TILE
    sub = S5_TILE
    lr8, li8, p_re, p_im = tables
    p, pt = _scan_order()

    def body(x_ref, g_ref, w_ref, c_ref, s_ref, p_ref, pt_ref, b_ref, cm_ref, lr_ref, li_ref, pr_ref, pi_ref, d_ref,
             q_ref, k_ref, v_ref, gate_ref, u_ref, y_ref, start_ref, xs, xb, carry_re, carry_im):
        @pl.when(pl.program_id(0) == 0)
        def _():
            carry_re[...] = jnp.zeros_like(carry_re)
            carry_im[...] = jnp.zeros_like(carry_im)

        n, _ = _rms(x_ref[...])
        h = (n * g_ref[...]).astype(BF16)
        proj = _dot(h, w_ref[...])
        c = c_ref[...]
        s = s_ref[...]
        for hd in range(RET_HEADS):
            lo = hd * HEAD_DIM
            q_ref[:, lo:lo + HEAD_DIM] = _rope(proj[:, lo:lo + HEAD_DIM], c, s).astype(BF16)
            kh = proj[:, RET_WIDTH + lo:RET_WIDTH + lo + HEAD_DIM]
            k_ref[:, lo:lo + HEAD_DIM] = (_rope(kh, c, s) * K_SCALE).astype(BF16)
        v_ref[...] = proj[:, 2 * RET_WIDTH:3 * RET_WIDTH].astype(BF16)
        gate_ref[...] = proj[:, 3 * RET_WIDTH:4 * RET_WIDTH]
        u = proj[:, 4 * RET_WIDTH:]
        u_ref[...] = u
        for part in range(t // sub):
            rows = slice(part * sub, (part + 1) * sub)
            start_ref[part, 0:1, :] = carry_re[...]
            start_ref[part, 1:2, :] = carry_im[...]
            uv = u[rows]
            _s5_channels_to_states(_to_scan_order(p_ref, uv.astype(BF16)), b_ref, xs)
            _s5_scan_tile(xs, lr_ref, li_ref, pr_ref, pi_ref, carry_re, carry_im, conj=False, reverse=False)
            _s5_pack(xs, xb)
            y_ref[rows, :] = _from_scan_order(pt_ref, _s5_states_to_channels(xb, cm_ref), pieces=2) + d_ref[...] * uv

    half = lambda dt: jax.ShapeDtypeStruct((seq, RET_WIDTH), dt)
    return _call(
        body, "inproj_s5_fwd", (seq // t,),
        [_rows(t, D_MODEL), _full((1, D_MODEL)), _full((D_MODEL, IN_COLS)), _rows(t, HEAD_DIM), _rows(t, HEAD_DIM),
         _full((sub, sub)), _full((sub, sub)), _full(S5_TILES_SHAPE), _full(S5_TILES_SHAPE),
         _full((SUBLANES, N_STATES)), _full((SUBLANES, N_STATES)), _full((SUBLANES * S5_STEPS, N_STATES)),
         _full((SUBLANES * S5_STEPS, N_STATES)), _full((1, SSM_WIDTH))],
        [_rows(t, RET_WIDTH)] * 6 + [pl.BlockSpec((t // sub, 2, N_STATES), lambda i: (i, 0, 0))],
        [half(BF16), half(BF16), half(BF16), half(F32), half(F32), half(F32),
         jax.ShapeDtypeStruct((seq // sub, 2, N_STATES), F32)],
        [pltpu.VMEM((2 * S5_COLBLK, sub, LANES), F32), pltpu.VMEM((sub, 2 * N_STATES), BF16),
         pltpu.VMEM((1, N_STATES), F32), pltpu.VMEM((1, N_STATES), F32)],
        (x, gain, w_in, cos2, sin2, p, pt, b_blk, c_blk, lr8, li8, p_re, p_im, d_skip), exchange)


def _s5_bwd(u, dy, b_blk, c_blk, tables, d_skip, starts, exchange=None):
    seq = u.shape[0]
    t = S5_TILE
    nt = seq // t
    lr8, li8, p_re, p_im = tables
    p, pt = _scan_order()

    def body(u_ref, dy_ref, p_ref, pt_ref, b_ref, c_ref, lr_ref, li_ref, pr_ref, pi_ref, d_ref, start_ref,
             du_ref, dd_ref, dlam_ref, db_ref, dc_ref,
             xs, as_, xb, ab, carry_re, carry_im, acar_re, acar_im):
        @pl.when(pl.program_id(0) == 0)
        def _():
            acar_re[...] = jnp.zeros_like(acar_re)
            acar_im[...] = jnp.zeros_like(acar_im)
            dd_ref[...] = jnp.zeros_like(dd_ref)
            dlam_ref[...] = jnp.zeros_like(dlam_ref)
            db_ref[...] = jnp.zeros_like(db_ref)
            dc_ref[...] = jnp.zeros_like(dc_ref)

        uv = u_ref[...]
        dyv = dy_ref[...]
        ub = _to_scan_order(p_ref, uv.astype(BF16))
        dyb = _to_scan_order(p_ref, dyv.astype(BF16))
        carry_re[...] = start_ref[0, 0:1, :]
        carry_im[...] = start_ref[0, 1:2, :]
        _s5_channels_to_states(ub, b_ref, xs)
        in_re, in_im, _, _ = _s5_scan_tile(xs, lr_ref, li_ref, pr_ref, pi_ref, carry_re, carry_im, conj=False, reverse=False)
        _s5_channels_to_states(dyb, c_ref, as_)
        _, _, dl_re, dl_im = _s5_scan_tile(as_, lr_ref, li_ref, pr_ref, pi_ref, acar_re, acar_im, conj=True, reverse=True,
                                           prev_of=(xs, in_re, in_im))
        for b in range(S5_COLBLK):
            cols = slice(b * LANES, (b + 1) * LANES)
            dlam_ref[0, :, cols] += dl_re[b]
            dlam_ref[1, :, cols] += dl_im[b]
        _s5_pack(xs, xb)
        _s5_pack(as_, ab)
        du_ref[...] = (_from_scan_order(pt_ref, _s5_states_to_channels(ab, b_ref), pieces=2) + d_ref[...] * dyv).astype(BF16)
        dd_ref[...] += jnp.sum(dyv * uv, axis=0, keepdims=True)
        _s5_outer_acc(db_ref, ub, ab)
        _s5_outer_acc(dc_ref, dyb, xb)

    rev = _rows_rev(t, SSM_WIDTH, nt)
    vec = lambda: pltpu.VMEM((1, N_STATES), F32)
    outer = S5_TILES_SHAPE
    return _call(
        body, "s5_bwd", (nt,),
        [rev, rev, _full((t, t)), _full((t, t)), _full(S5_TILES_SHAPE), _full(S5_TILES_SHAPE),
         _full((SUBLANES, N_STATES)), _full((SUBLANES, N_STATES)), _full((SUBLANES * S5_STEPS, N_STATES)),
         _full((SUBLANES * S5_STEPS, N_STATES)), _full((1, SSM_WIDTH)),
         pl.BlockSpec((1, 2, N_STATES), lambda i: (nt - 1 - i, 0, 0))],
        [rev, _full((1, SSM_WIDTH)), _full((2, SUBLANES, N_STATES)), _full(outer), _full(outer)],
        [jax.ShapeDtypeStruct((seq, SSM_WIDTH), BF16), jax.ShapeDtypeStruct((1, SSM_WIDTH), F32),
         jax.ShapeDtypeStruct((2, SUBLANES, N_STATES), F32), jax.ShapeDtypeStruct(outer, F32), jax.ShapeDtypeStruct(outer, F32)],
        [pltpu.VMEM((2 * S5_COLBLK, t, LANES), F32), pltpu.VMEM((2 * S5_COLBLK, t, LANES), F32),
         pltpu.VMEM((t, 2 * N_STATES), BF16), pltpu.VMEM((t, 2 * N_STATES), BF16), vec(), vec(), vec(), vec()],
        (u, dy, p, pt, b_blk, c_blk, lr8, li8, p_re, p_im, d_skip, starts), exchange)


def _mix_fwd_step(ys_ref, yr_ref, x_ref, g_ref, wg_ref, wo_ref, x1_ref, mix_ref):
    g0 = _gelu(ys_ref[...]).astype(BF16)
    z = _dot(g0, wg_ref[...])
    glu = (z[:, :SSM_WIDTH] * _sigmoid(z[:, SSM_WIDTH:])).astype(BF16)
    mix = _dot(yr_ref[...], wo_ref[:RET_WIDTH, :]) + _dot(glu, wo_ref[RET_WIDTH:, :])
    mix_ref[...] = mix.astype(BF16)
    n, _ = _rms(mix)
    x1_ref[...] = x_ref[...] + n * g_ref[...]


def _ret_mix_fwd(q, k, v, gate, gn_gain, consts, y_ssm, x, gain, w_glu, w_out, exchange=None):
    seq = x.shape[0]
    cps = RET_CHUNKS_PER_STEP
    t = cps * RET_CHUNK
    dec, xi, zeta, gch = consts

    def body(q_ref, k_ref, v_ref, gate_ref, gain_ref, dec_ref, xi_ref, zeta_ref, gch_ref, ys_ref, x_ref, g_ref, wg_ref, wo_ref,
             y_ref, rprev_ref, x1_ref, mix_ref, state):
        @pl.when(pl.program_id(0) == 0)
        def _():
            state[...] = jnp.zeros_like(state)

        _ret_fwd_step(q_ref, k_ref, v_ref, gate_ref, gain_ref, dec_ref, xi_ref, zeta_ref, gch_ref, y_ref, rprev_ref, state)
        _mix_fwd_step(ys_ref, y_ref, x_ref, g_ref, wg_ref, wo_ref, x1_ref, mix_ref)

    cst = _full((RET_HEADS, RET_CHUNK, RET_CHUNK))
    return _call(
        body, "ret_mix_fwd", (seq // t,),
        [_rows(t, RET_WIDTH)] * 4 + [_full((1, RET_WIDTH)), cst, cst, cst, cst, _rows(t, SSM_WIDTH), _rows(t, D_MODEL),
                                     _full((1, D_MODEL)), _full((SSM_WIDTH, 2 * SSM_WIDTH)), _full((D_MODEL, D_MODEL))],
        [_rows(t, RET_WIDTH), pl.BlockSpec((cps, RET_HEADS, HEAD_DIM, HEAD_DIM), lambda i: (i, 0, 0, 0)),
         _rows(t, D_MODEL), _rows(t, D_MODEL)],
        [jax.ShapeDtypeStruct((seq, RET_WIDTH), BF16),
         jax.ShapeDtypeStruct((seq // RET_CHUNK, RET_HEADS, HEAD_DIM, HEAD_DIM), BF16),
         jax.ShapeDtypeStruct((seq, D_MODEL), F32), jax.ShapeDtypeStruct((seq, D_MODEL), BF16)],
        [pltpu.VMEM((RET_HEADS, HEAD_DIM, HEAD_DIM), F32)],
        (q, k, v, gate, gn_gain, dec, xi, zeta, gch, y_ssm, x, gain, w_glu, w_out), exchange)


def _mix_bwd(dx1, mix, gain, y_ssm, y_ret, w_glu, w_out, exchange=None):
    seq = dx1.shape[0]
    t = TOK_TILE

    def body(dx1_ref, mix_ref, g_ref, ys_ref, yr_ref, wg_ref, wo_ref, dyr_ref, dys_ref, dwo_ref, dwg_ref, dg_ref):
        @pl.when(pl.program_id(0) == 0)
        def _():
            dwo_ref[...] = jnp.zeros_like(dwo_ref)
            dwg_ref[...] = jnp.zeros_like(dwg_ref)
            dg_ref[...] = jnp.zeros_like(dg_ref)

        n, r = _rms(mix_ref[...].astype(F32))
        dmix, dgs = _rms_bwd(n, r, g_ref[...], dx1_ref[...])
        dg_ref[...] += dgs
        dmb = dmix.astype(BF16)
        dcat = _dot_nt(dmb, wo_ref[...])
        dyr_ref[...] = dcat[:, :RET_WIDTH]
        dglu = dcat[:, RET_WIDTH:]
        ys = ys_ref[...]
        g0 = _gelu(ys).astype(BF16)
        z = _dot(g0, wg_ref[...])
        a = z[:, :SSM_WIDTH]
        sb = _sigmoid(z[:, SSM_WIDTH:])
        dwo_ref[:RET_WIDTH, :] += _dot_tn(yr_ref[...], dmb)
        dwo_ref[RET_WIDTH:, :] += _dot_tn((a * sb).astype(BF16), dmb)
        dz = jnp.concatenate([dglu * sb, dglu * a * sb * (1.0 - sb)], axis=1).astype(BF16)
        dwg_ref[...] += _dot_tn(g0, dz)
        dys_ref[...] = _dot_nt(dz, wg_ref[...]) * _gelu_grad(ys)

    half = jax.ShapeDtypeStruct((seq, RET_WIDTH), F32)
    return _call(
        body, "mix_bwd", (seq // t,),
        [_rows(t, D_MODEL), _rows(t, D_MODEL), _full((1, D_MODEL)), _rows(t, SSM_WIDTH), _rows(t, RET_WIDTH),
         _full((SSM_WIDTH, 2 * SSM_WIDTH)), _full((D_MODEL, D_MODEL))],
        [_rows(t, RET_WIDTH), _rows(t, SSM_WIDTH), _full((D_MODEL, D_MODEL)), _full((SSM_WIDTH, 2 * SSM_WIDTH)),
         _full((1, D_MODEL))],
        [half, half, jax.ShapeDtypeStruct((D_MODEL, D_MODEL), F32), jax.ShapeDtypeStruct((SSM_WIDTH, 2 * SSM_WIDTH), F32),
         jax.ShapeDtypeStruct((1, D_MODEL), F32)],
        [], (dx1, mix, gain, y_ssm, y_ret, w_glu, w_out), exchange)


def _mlp_a(x1, target, gain_pre, gain_post, w1, w2):
    seq = x1.shape[0]
    t = MLP_TILE

    def body(x1_ref, tg_ref, gp_ref, gq_ref, w1_ref, w2_ref, df_ref, dx2_ref, dw2_ref, dgq_ref, sq_ref):
        @pl.when(pl.program_id(0) == 0)
        def _():
            dw2_ref[...] = jnp.zeros_like(dw2_ref)
            dgq_ref[...] = jnp.zeros_like(dgq_ref)
            sq_ref[...] = jnp.zeros_like(sq_ref)

        x1v = x1_ref[...]
        n3, _ = _rms(x1v)
        h = (n3 * gp_ref[...]).astype(BF16)
        rl = jnp.maximum(_dot(h, w1_ref[...]), 0.0)
        act = (rl * rl).astype(BF16)
        n4, r4 = _rms(_dot(act, w2_ref[...]))
        gq = gq_ref[...]
        err = x1v + n4 * gq - tg_ref[...]
        sq_ref[...] += jnp.sum(err * err, axis=0, keepdims=True)
        dx2 = err * (1.0 / D_MODEL)
        dx2_ref[...] = dx2
        dm, dgs = _rms_bwd(n4, r4, gq, dx2)
        dgq_ref[...] += dgs
        dmb = dm.astype(BF16)
        dw2_ref[...] += _dot_tn(act, dmb)
        df_ref[...] = (_dot_nt(dmb, w2_ref[...]) * (2.0 * rl)).astype(BF16)

    return pl.pallas_call(
        body, name="mlp_a", grid=(seq // t,),
        in_specs=[_rows(t, D_MODEL), _rows(t, D_MODEL), _full((1, D_MODEL)), _full((1, D_MODEL)),
                  _full((D_MODEL, D_FF)), _full((D_FF, D_MODEL))],
        out_specs=[_rows(t, D_FF), _rows(t, D_MODEL), _full((D_FF, D_MODEL)), _full((1, D_MODEL)), _full((1, D_MODEL))],
        out_shape=[jax.ShapeDtypeStruct((seq, D_FF), BF16), jax.ShapeDtypeStruct((seq, D_MODEL), F32),
                   jax.ShapeDtypeStruct((D_FF, D_MODEL), F32), jax.ShapeDtypeStruct((1, D_MODEL), F32),
                   jax.ShapeDtypeStruct((1, D_MODEL), F32)],
        compiler_params=_params("arbitrary"),
    )(x1, target, gain_pre, gain_post, w1, w2)


def _mlp_b(df, dx2, x1, gain_pre, w1):
    seq = x1.shape[0]
    t = TOK_TILE

    def body(df_ref, dx2_ref, x1_ref, gp_ref, w1_ref, dx1_ref, dw1_ref, dgp_ref):
        @pl.when(pl.program_id(0) == 0)
        def _():
            dw1_ref[...] = jnp.zeros_like(dw1_ref)
            dgp_ref[...] = jnp.zeros_like(dgp_ref)

        n3, r3 = _rms(x1_ref[...])
        gp = gp_ref[...]
        h = (n3 * gp).astype(BF16)
        dfv = df_ref[...]
        dw1_ref[...] += _dot_tn(h, dfv)
        dx, dgs = _rms_bwd(n3, r3, gp, _dot_nt(dfv, w1_ref[...]))
        dgp_ref[...] += dgs
        dx1_ref[...] = dx2_ref[...] + dx

    return pl.pallas_call(
        body, name="mlp_b", grid=(seq // t,),
        in_specs=[_rows(t, D_FF), _rows(t, D_MODEL), _rows(t, D_MODEL), _full((1, D_MODEL)), _full((D_MODEL, D_FF))],
        out_specs=[_rows(t, D_MODEL), _full((D_MODEL, D_FF)), _full((1, D_MODEL))],
        out_shape=[jax.ShapeDtypeStruct((seq, D_MODEL), F32), jax.ShapeDtypeStruct((D_MODEL, D_FF), F32),
                   jax.ShapeDtypeStruct((1, D_MODEL), F32)],
        compiler_params=_params("arbitrary"),
    )(df, dx2, x1, gain_pre, w1)


def _local_step(x, target, small, weights, ids=None, late_state=None):
    cos2, sin2 = _rope_tables(x.shape[0])
    ret_consts = _ret_consts()

    s5_names = ("ssm_lambda_re", "ssm_lambda_im", "ssm_log_dt", "ssm_b_re", "ssm_b_im", "ssm_c_re", "ssm_c_im")
    (lbr, lbi, b_tiles, c_tiles), disc_vjp = jax.vjp(_s5_discretise, *[small[name] for name in s5_names])
    tables = _s5_tables(lbr, lbi)
    b_blk = b_tiles.astype(BF16)
    c_blk = c_tiles.astype(BF16)
    d_skip = small["ssm_d"].reshape(1, SSM_WIDTH)
    gn_gain = small["ret_gn_gain"].reshape(1, RET_WIDTH)
    g_mix_pre = small["norm_mix_pre"].reshape(1, D_MODEL)
    g_mix_post = small["norm_mix_post"].reshape(1, D_MODEL)
    g_mlp_pre = small["norm_mlp_pre"].reshape(1, D_MODEL)
    g_mlp_post = small["norm_mlp_post"].reshape(1, D_MODEL)

    dist = ids is not None
    w_in, w_glu, w_out, w_ff1, w_ff2 = weights
    (q, k, v, gate, u, y_ssm, starts), got = _inproj_s5_fwd(
        x, g_mix_pre, w_in, cos2, sin2, b_blk, c_blk, tables, d_skip,
        _x_gather_send(MID + LATE, [w_glu, w_out, w_ff1, w_ff2], to_both=MID) if dist else None)
    if dist:
        w_glu, w_out, w_ff1, w_ff2 = got
    (y_ret, rprev, x1, mix), got = _ret_mix_fwd(q, k, v, gate, gn_gain, ret_consts, y_ssm, x, g_mix_post, w_glu, w_out,
                                                _x_gather_pass(LATE, [w_ff1, w_ff2]) if dist else None)
    if dist:
        w_ff1, w_ff2 = got
    df, dx2, d_ff2, dg_mlp_post, sq = _mlp_a(x1, target, g_mlp_pre, g_mlp_post, w_ff1, w_ff2)
    dx1, d_ff1, dg_mlp_pre = _mlp_b(df, dx2, x1, g_mlp_pre, w_ff1)
    (dy_ret, dy_ssm, d_out, d_glu, dg_mix_post), got = _mix_bwd(dx1, mix, g_mix_post, y_ssm, y_ret, w_glu, w_out,
                                                               _x_pair(LATE, [d_ff1, d_ff2]) if dist else None)
    if dist:
        sums = _pair_sum(ids, LATE, [d_ff1, d_ff2], got)
    (du, dd, dlam8, db_tiles, dc_tiles), got = _s5_bwd(
        u, dy_ssm, b_blk, c_blk, tables, d_skip, starts,
        _x_join(_x_chip(LATE, sums), _x_pair(MID, [d_glu, d_out])) if dist else None)
    if dist:
        parts = _chip_sum(ids, LATE, sums, got[:len(LATE)])
        sums = _pair_sum(ids, MID, [d_glu, d_out], got[len(LATE):])
    (dq, dk, dv, dgate, dgn), got = _ret_bwd(q, k, v, gate, gn_gain, ret_consts, rprev, dy_ret, cos2, sin2,
                                             _x_join(_x_share(LATE, parts), _x_chip(MID, sums)) if dist else None)
    if dist:
        late_grads = got[:len(LATE)]
        mid_parts = _chip_sum(ids, MID, sums, got[len(LATE):])
    adam = (late_state[0], late_grads, late_state[1], late_state[2]) if dist else ()
    (dx, d_in, dg_mix_pre), late_stepped = _inproj_bwd(x, g_mix_pre, w_in, [dq, dk, dv, dgate, du], dx1, adam)

    dlam = jnp.sum(dlam8, axis=1)
    s5_grads = disc_vjp((dlam[0].reshape(SSM_GROUPS, SSM_STATE), dlam[1].reshape(SSM_GROUPS, SSM_STATE), db_tiles, dc_tiles))
    small_grads = {
        "norm_mix_pre": dg_mix_pre, "norm_mix_post": dg_mix_post, "ret_gn_gain": dgn, "ssm_d": dd,
        "norm_mlp_pre": dg_mlp_pre, "norm_mlp_post": dg_mlp_post, "loss_sum": 0.5 * jnp.sum(sq) / D_MODEL,
    }
    small_grads.update(dict(zip(s5_names, s5_grads)))
    if dist:
        return dx, d_in, mid_parts, late_grads, late_stepped, small_grads
    return dx, (d_in, d_glu, d_out, d_ff1, d_ff2), small_grads


BIG = (
    ("w_in", D_MODEL, IN_COLS, 1),
    ("w_glu", SSM_WIDTH, 2 * SSM_WIDTH, 1),
    ("w_out", D_MODEL, D_MODEL, 0),
    ("w_ff1", D_MODEL, D_FF, 1),
    ("w_ff2", D_FF, D_MODEL, 0),
)
SMALL = (
    ("norm_mix_pre", (D_MODEL,)), ("norm_mix_post", (D_MODEL,)), ("ret_gn_gain", (RET_WIDTH,)),
    ("ssm_lambda_re", (SSM_GROUPS, SSM_STATE)), ("ssm_lambda_im", (SSM_GROUPS, SSM_STATE)), ("ssm_log_dt", (SSM_GROUPS,)),
    ("ssm_b_re", (SSM_GROUPS, SSM_STATE, SSM_GROUP)), ("ssm_b_im", (SSM_GROUPS, SSM_STATE, SSM_GROUP)),
    ("ssm_c_re", (SSM_GROUPS, SSM_GROUP, SSM_STATE)), ("ssm_c_im", (SSM_GROUPS, SSM_GROUP, SSM_STATE)),
    ("ssm_d", (SSM_WIDTH,)), ("norm_mlp_pre", (D_MODEL,)), ("norm_mlp_post", (D_MODEL,)),
)
SMALL_ROWS = 1152
EXCH_TILES = 2


PACKED = SMALL + (("loss_sum", (1,)),)


def _packed_rows(shape):
    return -(-math.prod(shape) // (SUBLANES * LANES)) * SUBLANES


def _pack_small(tree):
    rows = []
    for name, shape in PACKED:
        size = math.prod(shape)
        flat = tree[name].reshape(-1).astype(F32) if name in tree else jnp.zeros((size,), F32)
        rows.append(jnp.pad(flat, (0, _packed_rows(shape) * LANES - size)).reshape(-1, LANES))
    used = sum(r.shape[0] for r in rows)
    rows.append(jnp.zeros((SMALL_ROWS - used, LANES), F32))
    return jnp.concatenate(rows, axis=0)


def _unpack_small(packed):
    out, row = {}, 0
    for name, shape in PACKED:
        size = math.prod(shape)
        out[name] = packed[row:row + _packed_rows(shape)].reshape(-1)[:size].reshape((1,) + shape)
        row += _packed_rows(shape)
    return out


def _half_shape(r, c, axis):
    return (r // 2, c) if axis == 1 else (r, c // 2)


def _region_shape(r, c, axis):
    return (r // 2, c // N_CHIPS) if axis == 1 else (r // N_CHIPS, c // 2)


def _shard_shape(r, c, axis):
    return (r, c // N_CHIPS) if axis == 1 else (r // N_CHIPS, c)


def _ds(start, size):
    return pl.ds(pl.multiple_of(start * size, size), size)


def _region_of_full(ref, r, c, axis, shard, half):
    if axis == 1:
        return ref.at[_ds(half, r // 2), _ds(shard, c // N_CHIPS)]
    return ref.at[_ds(shard, r // N_CHIPS), _ds(half, c // 2)]


def _half_of_full(ref, r, c, axis, half):
    if axis == 1:
        return ref.at[_ds(half, r // 2), :]
    return ref.at[:, _ds(half, c // 2)]


def _half_of_shard(ref, r, c, axis, half):
    if axis == 1:
        return ref.at[_ds(half, r // 2), :]
    return ref.at[:, _ds(half, c // 2)]


def _region_of_half(ref, r, c, axis, shard):
    if axis == 1:
        return ref.at[:, _ds(shard, c // N_CHIPS)]
    return ref.at[_ds(shard, r // N_CHIPS), :]


def _place():
    x, y, c = lax.axis_index("x"), lax.axis_index("y"), lax.axis_index("c")
    chips = [(1 - x, y), (x, 1 - y), (1 - x, 1 - y)]
    return x, y, c, chips


LATE = (3, 4)
MID = (1, 2)
FIRST = (0,)
SMALL_HALF = (SMALL_ROWS // 2, LANES)


def _remote(src, dst, send_sem, recv_sem, to):
    return pltpu.make_async_remote_copy(src_ref=src, dst_ref=dst, send_sem=send_sem, recv_sem=recv_sem,
                                        device_id=to, device_id_type=MESH)


def _same(arrays):
    return [jax.ShapeDtypeStruct(a.shape, a.dtype) for a in arrays]


def _x_gather_send(ws, fulls, to_both=()):
    def copies(ins, outs, send_sems, recv_sems):
        x, y, c, chips = _place()
        out, at = [], 0
        for j, w in enumerate(ws):
            _, r, cc, axis = BIG[w]
            mine = _region_of_full(outs[j], r, cc, axis, 2 * x + y, c)
            for other_core in range(2 if w in to_both else 1):
                to_core = (1 - c) if other_core else c
                out += [_remote(mine, mine, send_sems.at[at + k], recv_sems.at[at + k], (cx, cy, to_core))
                        for k, (cx, cy) in enumerate(chips)]
                at += 3
        return out

    n = sum(6 if w in to_both else 3 for w in ws)
    return _Exchange(fulls, _same(fulls), {j: j for j in range(len(ws))}, n, copies)


def _x_gather_pass(ws, fulls):
    def copies(ins, outs, send_sems, recv_sems):
        x, y, c, chips = _place()
        out = []
        for j, w in enumerate(ws):
            _, r, cc, axis = BIG[w]
            for k, (cx, cy) in enumerate(chips):
                landed = _region_of_full(outs[j], r, cc, axis, 2 * cx + cy, c)
                out.append(_remote(landed, landed, send_sems.at[3 * j + k], recv_sems.at[3 * j + k], (x, y, 1 - c)))
        return out

    return _Exchange(fulls, _same(fulls), {j: j for j in range(len(ws))}, 3 * len(ws), copies)


def _x_pair(ws, grads, small=None):
    def copies(ins, outs, send_sems, recv_sems):
        x, y, c, _ = _place()
        out = []
        for j, w in enumerate(ws):
            _, r, cc, axis = BIG[w]
            out.append(_remote(_half_of_full(ins[j], r, cc, axis, 1 - c), outs[j], send_sems.at[j], recv_sems.at[j], (x, y, 1 - c)))
        if small is not None:
            j = len(ws)
            out.append(_remote(ins[j].at[_ds(1 - c, SMALL_ROWS // 2), :], outs[j], send_sems.at[j], recv_sems.at[j], (x, y, 1 - c)))
        return out

    shapes = [jax.ShapeDtypeStruct(_half_shape(*BIG[w][1:]), F32) for w in ws]
    extra = [] if small is None else [small]
    return _Exchange(list(grads) + extra, shapes + [jax.ShapeDtypeStruct(SMALL_HALF, F32)] * len(extra), {},
                     len(ws) + len(extra), copies)


def _x_chip(ws, sums_bf16, small_sum=None):
    def copies(ins, outs, send_sems, recv_sems):
        x, y, c, chips = _place()
        out = []
        for j, w in enumerate(ws):
            _, r, cc, axis = BIG[w]
            out += [_remote(_region_of_half(ins[j], r, cc, axis, 2 * cx + cy), outs[j].at[k],
                            send_sems.at[3 * j + k], recv_sems.at[3 * j + k], (cx, cy, c)) for k, (cx, cy) in enumerate(chips)]
        if small_sum is not None:
            j = len(ws)
            out += [_remote(ins[j], outs[j].at[k], send_sems.at[3 * j + k], recv_sems.at[3 * j + k], (cx, cy, c))
                    for k, (cx, cy) in enumerate(chips)]
        return out

    shapes = [jax.ShapeDtypeStruct((3,) + _region_shape(*BIG[w][1:]), BF16) for w in ws]
    extra = [] if small_sum is None else [small_sum]
    return _Exchange(list(sums_bf16) + extra, shapes + [jax.ShapeDtypeStruct((3,) + SMALL_HALF, F32)] * len(extra), {},
                     3 * (len(ws) + len(extra)), copies)


def _x_share(ws, shards, small=None):
    def copies(ins, outs, send_sems, recv_sems):
        x, y, c, _ = _place()
        out = []
        for j, w in enumerate(ws):
            _, r, cc, axis = BIG[w]
            mine = _half_of_shard(outs[j], r, cc, axis, c)
            out.append(_remote(mine, mine, send_sems.at[j], recv_sems.at[j], (x, y, 1 - c)))
        if small is not None:
            j = len(ws)
            mine = outs[j].at[_ds(c, SMALL_ROWS // 2), :]
            out.append(_remote(mine, mine, send_sems.at[j], recv_sems.at[j], (x, y, 1 - c)))
        return out

    arrays = list(shards) + ([] if small is None else [small])
    return _Exchange(arrays, _same(arrays), {j: j for j in range(len(arrays))}, len(arrays), copies)


class _Offset:
    def __init__(self, sems, base):
        self._sems, self._base = sems, base

    @property
    def at(self):
        return self

    def __getitem__(self, i):
        return self._sems.at[self._base + i]


def _x_join(a, b):
    ka_in, ka_out = len(a.operands), len(a.out_shapes)

    def copies(ins, outs, send_sems, recv_sems):
        return (a.copies(ins[:ka_in], outs[:ka_out], send_sems, recv_sems)
                + b.copies(ins[ka_in:], outs[ka_out:], _Offset(send_sems, a.n), _Offset(recv_sems, a.n)))

    aliases = dict(a.aliases)
    aliases.update({ka_in + i: ka_out + o for i, o in b.aliases.items()})
    return _Exchange(a.operands + b.operands, a.out_shapes + b.out_shapes, aliases, a.n + b.n, copies)


def _run_exchange(name, exchange):
    k_in, k_out = len(exchange.operands), len(exchange.out_shapes)

    def body(*refs):
        copies = exchange.copies(refs[:k_in], refs[k_in:k_in + k_out], refs[-2], refs[-1])
        for cp in copies:
            cp.start()
        for cp in copies:
            cp.wait()

    return list(pl.pallas_call(
        body, name=name, in_specs=[ANY] * k_in, out_specs=[ANY] * k_out, out_shape=exchange.out_shapes,
        scratch_shapes=[pltpu.SemaphoreType.DMA((exchange.n,)), pltpu.SemaphoreType.DMA((exchange.n,))],
        input_output_aliases=dict(exchange.aliases),
    )(*exchange.operands))


def _gather_now(name, ws, fulls):
    send, onward = _x_gather_send(ws, fulls), _x_gather_pass(ws, fulls)
    k = len(fulls)

    def body(*refs):
        ins, outs, (send_a, recv_a, send_b, recv_b) = refs[:k], refs[k:2 * k], refs[2 * k:]
        first = send.copies(ins, outs, send_a, recv_a)
        second = onward.copies(ins, outs, send_b, recv_b)
        for cp in first:
            cp.start()
        for landed, cp in zip(first, second):
            landed.wait()
            cp.start()
        for cp in second:
            cp.wait()

    sems = lambda n: pltpu.SemaphoreType.DMA((n,))
    return list(pl.pallas_call(
        body, name=name, in_specs=[ANY] * k, out_specs=[ANY] * k, out_shape=send.out_shapes,
        scratch_shapes=[sems(send.n), sems(send.n), sems(onward.n), sems(onward.n)],
        input_output_aliases={j: j for j in range(k)},
    )(*fulls))


def _tile_rows(shape):
    return (shape[0] // EXCH_TILES, shape[1])


def _cast_place(ids, shards):
    def body(ids_ref, *refs):
        for j in range(len(BIG)):
            refs[len(BIG) + j][...] = refs[j][...].astype(BF16)

    def out_spec(w):
        _, r, cc, axis = BIG[w]
        tile = _tile_rows(_shard_shape(r, cc, axis))
        if axis == 1:
            return pl.BlockSpec(tile, lambda i, ids: (i, ids[1]))
        return pl.BlockSpec(tile, lambda i, ids: (ids[1] * EXCH_TILES + i, 0))

    return list(pl.pallas_call(
        body, name="cast_place",
        grid_spec=pltpu.PrefetchScalarGridSpec(
            num_scalar_prefetch=1, grid=(EXCH_TILES,),
            in_specs=[pl.BlockSpec(_tile_rows(_shard_shape(r, cc, axis)), lambda i, ids: (i, 0)) for _, r, cc, axis in BIG],
            out_specs=[out_spec(w) for w in range(len(BIG))]),
        out_shape=[jax.ShapeDtypeStruct((r, cc), BF16) for _, r, cc, _ in BIG],
        compiler_params=pltpu.CompilerParams(dimension_semantics=("parallel",)),
    )(ids, *shards))


def _pair_sum(ids, ws, grads, got, small=None):
    nb = len(ws)
    n = nb + (small is not None)
    halves = [_half_shape(*BIG[w][1:]) for w in ws] + [SMALL_HALF] * (n - nb)

    def body(ids_ref, *refs):
        for j in range(n):
            total = refs[j][...] + refs[n + j][...]
            refs[2 * n + j][...] = total.astype(refs[2 * n + j].dtype)

    def mine_spec(j):
        tile = _tile_rows(halves[j])
        if j < nb and BIG[ws[j]][3] == 0:
            return pl.BlockSpec(tile, lambda i, ids: (i, ids[0]))
        return pl.BlockSpec(tile, lambda i, ids: (ids[0] * EXCH_TILES + i, 0))

    plain = lambda j: pl.BlockSpec(_tile_rows(halves[j]), lambda i, ids: (i, 0))
    outs = pl.pallas_call(
        body, name="pair_sum",
        grid_spec=pltpu.PrefetchScalarGridSpec(
            num_scalar_prefetch=1, grid=(EXCH_TILES,),
            in_specs=[mine_spec(j) for j in range(n)] + [plain(j) for j in range(n)],
            out_specs=[plain(j) for j in range(n)]),
        out_shape=[jax.ShapeDtypeStruct(h, BF16 if j < nb else F32) for j, h in enumerate(halves)],
        compiler_params=pltpu.CompilerParams(dimension_semantics=("parallel",)),
    )(ids, *grads, *([] if small is None else [small]), *got)
    return list(outs)


def _chip_sum(ids, ws, sums, arrived):
    nb, n = len(ws), len(sums)
    regions = [_region_shape(*BIG[w][1:]) for w in ws] + [SMALL_HALF] * (n - nb)

    def body(ids_ref, *refs):
        for j in range(n):
            own, got, out = refs[j], refs[n + j], refs[2 * n + j]
            if j < nb:
                out[...] = ((own[...].astype(F32) + got[0].astype(F32)) + got[1].astype(F32)) + got[2].astype(F32)
            else:
                out[...] = (own[...] + got[1]) + (got[0] + got[2])

    def own_spec(j):
        tile = _tile_rows(regions[j])
        if j >= nb:
            return pl.BlockSpec(tile, lambda i, ids: (i, 0))
        if BIG[ws[j]][3] == 1:
            return pl.BlockSpec(tile, lambda i, ids: (i, ids[1]))
        return pl.BlockSpec(tile, lambda i, ids: (ids[1] * EXCH_TILES + i, 0))

    def out_spec(j):
        tile = _tile_rows(regions[j])
        if j < nb and BIG[ws[j]][3] == 0:
            return pl.BlockSpec(tile, lambda i, ids: (i, ids[0]))
        return pl.BlockSpec(tile, lambda i, ids: (ids[0] * EXCH_TILES + i, 0))

    got_spec = lambda j: pl.BlockSpec((3,) + _tile_rows(regions[j]), lambda i, ids: (0, i, 0))
    return list(pl.pallas_call(
        body, name="chip_sum",
        grid_spec=pltpu.PrefetchScalarGridSpec(
            num_scalar_prefetch=1, grid=(EXCH_TILES,),
            in_specs=[own_spec(j) for j in range(n)] + [got_spec(j) for j in range(n)],
            out_specs=[out_spec(j) for j in range(n)]),
        out_shape=[jax.ShapeDtypeStruct(_shard_shape(*BIG[w][1:]), F32) for w in ws]
        + [jax.ShapeDtypeStruct((SMALL_ROWS, LANES), F32)] * (n - nb),
        compiler_params=pltpu.CompilerParams(dimension_semantics=("parallel",)),
    )(ids, *sums, *arrived))


def _adamw_body(n):
    c1 = 1.0 - ADAM_B1 ** ADAM_STEP
    c2 = 1.0 - ADAM_B2 ** ADAM_STEP

    def body(*refs):
        for j in range(n):
            w, g, m, v = (refs[k * n + j][...] for k in range(4))
            m = ADAM_B1 * m + (1.0 - ADAM_B1) * g
            v = ADAM_B2 * v + (1.0 - ADAM_B2) * (g * g)
            refs[4 * n + j][...] = -ADAM_LR * ((m / c1) / (jnp.sqrt(v / c2) + ADAM_EPS) + ADAM_WD * w)
            refs[5 * n + j][...] = m
            refs[6 * n + j][...] = v

    return body


def _adamw(name, ws, gs, ms, vs):
    n = len(ws)
    specs = [pl.BlockSpec(_tile_rows(w.shape), lambda i: (i, 0)) for w in ws]
    shapes = [jax.ShapeDtypeStruct(w.shape, F32) for w in ws]
    out, _ = _call(_adamw_body(n), name, (EXCH_TILES,), specs * 4, specs * 3, shapes * 3, [], (*ws, *gs, *ms, *vs))
    return out[:n], out[n:2 * n], out[2 * n:]


def _adamw_whole(name, ws, gs, ms, vs):
    n = len(ws)
    out = pl.pallas_call(_adamw_body(n), name=name, out_shape=[jax.ShapeDtypeStruct(w.shape, F32) for w in ws] * 3)(*ws, *gs, *ms, *vs)
    return out[:n], out[n:2 * n], out[2 * n:]


def kernel(x, norm_mix_pre, norm_mix_post, w_in, ret_gn_gain, ssm_lambda_re, ssm_lambda_im, ssm_log_dt, ssm_b_re, ssm_b_im, ssm_c_re, ssm_c_im, ssm_d, w_glu, w_out, norm_mlp_pre, norm_mlp_post, w_ff1, w_ff2, loss_target, m_norm_mix_pre, m_norm_mix_post, m_w_in, m_ret_gn_gain, m_ssm_lambda_re, m_ssm_lambda_im, m_ssm_log_dt, m_ssm_b_re, m_ssm_b_im, m_ssm_c_re, m_ssm_c_im, m_ssm_d, m_w_glu, m_w_out, m_norm_mlp_pre, m_norm_mlp_post, m_w_ff1, m_w_ff2, v_norm_mix_pre, v_norm_mix_post, v_w_in, v_ret_gn_gain, v_ssm_lambda_re, v_ssm_lambda_im, v_ssm_log_dt, v_ssm_b_re, v_ssm_b_im, v_ssm_c_re, v_ssm_c_im, v_ssm_d, v_w_glu, v_w_out, v_norm_mlp_pre, v_norm_mlp_post, v_w_ff1, v_w_ff2):
    given = dict(locals())
    order = ["norm_mix_pre", "norm_mix_post", "w_in", "ret_gn_gain", "ssm_lambda_re", "ssm_lambda_im", "ssm_log_dt",
             "ssm_b_re", "ssm_b_im", "ssm_c_re", "ssm_c_im", "ssm_d", "w_glu", "w_out", "norm_mlp_pre", "norm_mlp_post",
             "w_ff1", "w_ff2"]
    big_names = [name for name, _, _, _ in BIG]
    ids = jnp.stack([lax.axis_index("c"), 2 * lax.axis_index("x") + lax.axis_index("y")]).astype(jnp.int32)

    weights = _cast_place(ids, [given[name][0] for name in big_names])
    w_in = _gather_now("gather_w_in", FIRST, weights[:1])
    small_w = {name: given[name][0] for name, _ in SMALL}
    late_state = [[given[prefix + big_names[w]][0] for w in LATE] for prefix in ("", "m_", "v_")]
    dx, d_in, mid_parts, late_grads, late_stepped, small_grads = _local_step(
        x[0], loss_target[0], small_w, w_in + weights[1:], ids, late_state)

    small_packed = _pack_small(small_grads)
    got = _run_exchange("pair_exchange", _x_join(_x_share(MID, mid_parts), _x_pair(FIRST, [d_in], small_packed)))
    mid_grads, got = got[:len(MID)], got[len(MID):]
    sums = _pair_sum(ids, FIRST, [d_in], got, small_packed)
    arrived = _run_exchange("chip_exchange", _x_chip(FIRST, sums[:-1], sums[-1]))
    parts = _chip_sum(ids, FIRST, sums, arrived)
    g_in, g_small = _run_exchange("pair_share", _x_share(FIRST, parts[:-1], parts[-1]))
    big_grads = [g_in] + mid_grads + late_grads
    rest = lambda prefix: [given[prefix + big_names[w]][0] for w in FIRST + MID]
    rest_out = _adamw("adamw_big", rest(""), big_grads[:len(FIRST + MID)], rest("m_"), rest("v_"))
    big_out = [list(rest_out[k]) + list(late_stepped[k]) for k in range(3)]
    small_names = [name for name, _ in SMALL]
    small_tree = _unpack_small(g_small)
    small_grads = [small_tree[name] for name in small_names]
    small = lambda prefix: [given[prefix + name] for name in small_names]
    small_out = _adamw_whole("adamw_small", small(""), small_grads, small("m_"), small("v_"))

    def in_order(big_arrays, small_arrays):
        tree = {name: a[None] for name, a in zip(big_names, big_arrays)}
        tree.update(zip(small_names, small_arrays))
        return [tree[name] for name in order]

    loss = small_tree["loss_sum"].reshape(())
    return (loss, dx[None], *in_order(big_grads, small_grads), *in_order(big_out[0], small_out[0]),
            *in_order(big_out[1], small_out[1]), *in_order(big_out[2], small_out[2]))
```
